```python
import jax, jax.numpy as jnp
from jax import lax
import numpy as np

D_MODEL = 1024
BATCH = 8
SEQ = 2048
DEPTH = 1
DEC_BATCH = 128
DEC_SEQ = 1
PAST_LEN = 16384
PAGE_SIZE = 128

EPS = 1e-6
POOL_GROUPS = 4
POOL_IN = D_MODEL // 2
POOL_GC = POOL_IN // POOL_GROUPS
POOL_OUT_GC = D_MODEL // POOL_GROUPS
POOL_WINDOWS = (2, 4, 8, 16)
POOL_HIST = max(POOL_WINDOWS) - 1
RET_HEADS = 4
RET_DK = D_MODEL // 8
RET_DV = D_MODEL // RET_HEADS
RET_CHUNK = 128
ROPE_BASE = 10000.0
N_GROUPS = 4
EXPERTS_PER_GROUP = 8
N_EXPERTS = N_GROUPS * EXPERTS_PER_GROUP
TOP_K_EXPERT = 2
D_EXPERT = D_MODEL // 4
MOE_BLOCK = 128
QK_W = RET_HEADS * RET_DK
V_W = RET_HEADS * RET_DV
IN_SPLITS = (POOL_IN, POOL_IN + QK_W, POOL_IN + 2 * QK_W, POOL_IN + 2 * QK_W + V_W, POOL_IN + 2 * QK_W + V_W + D_MODEL)
IN_WIDTH = POOL_IN + 2 * QK_W + V_W + 2 * D_MODEL

kernel_name = 'hybrid_pool_retention_hmoe_step'


def rmsnorm(x, g):
    xf = x.astype(jnp.float32)
    inv = lax.rsqrt(jnp.mean(xf * xf, axis=-1, keepdims=True) + EPS)
    return (xf * inv * g.astype(jnp.float32)).astype(x.dtype)


def rotary(x, pos):
    half = x.shape[-1] // 2
    freqs = ROPE_BASE ** (-jnp.arange(half, dtype=jnp.float32) / half)
    ang = pos[:, None] * freqs[None, :]
    cos = jnp.cos(ang)[None, :, None, :]
    sin = jnp.sin(ang)[None, :, None, :]
    x1, x2 = x[..., :half], x[..., half:]
    return jnp.concatenate([x1 * cos - x2 * sin, x1 * sin + x2 * cos], axis=-1)


def pool_mixer(u, hist, hist_valid, w_pool, pool_scale):
    B, L, _ = u.shape
    full = jnp.concatenate([hist.astype(u.dtype), u], axis=1)
    valid = jnp.concatenate([hist_valid.astype(jnp.float32), jnp.ones((L,), jnp.float32)])
    cs = jnp.pad(jnp.cumsum(full.astype(jnp.float32), axis=1), ((0, 0), (1, 0), (0, 0)))
    cv = jnp.pad(jnp.cumsum(valid), (1, 0))
    hi = POOL_HIST + 1
    groups = []
    for g, w in enumerate(POOL_WINDOWS):
        sl = slice(g * POOL_GC, (g + 1) * POOL_GC)
        s = cs[:, hi:hi + L, sl] - cs[:, hi - w:hi - w + L, sl]
        n = cv[hi:hi + L] - cv[hi - w:hi - w + L]
        groups.append(s / n[None, :, None] - u[:, :, sl].astype(jnp.float32))
    pooled = jnp.stack(groups, axis=2)
    out = jnp.einsum('blgc,gcd->blgd', pooled, w_pool.astype(jnp.float32)).reshape(B, L, D_MODEL)
    out = out * pool_scale.astype(jnp.float32)
    new_hist = full[:, -POOL_HIST:]
    return out, new_hist


def retention(q, k, v, s0):
    B, L, H, _ = q.shape
    C = RET_CHUNK if L % RET_CHUNK == 0 else L
    n = L // C
    log_g = jnp.log(1.0 - jnp.exp2(-5.0 - jnp.arange(H, dtype=jnp.float32)))
    i = jnp.arange(C, dtype=jnp.float32)
    diff = i[:, None] - i[None, :]
    dmask = jnp.where(diff >= 0, jnp.exp(jnp.maximum(diff, 0.0)[None] * log_g[:, None, None]), 0.0)
    xi = jnp.exp((i[:, None] + 1.0) * log_g[None, :])
    zeta = jnp.exp((C - 1.0 - i)[:, None] * log_g[None, :])
    g_chunk = jnp.exp(C * log_g)

    def to_chunks(a):
        return jnp.moveaxis(a.reshape((B, n, C) + a.shape[2:]), 1, 0)

    def step(s, qkv):
        qc, kc, vc = qkv
        scores = jnp.einsum('bihk,bjhk->bhij', qc, kc) * dmask[None]
        o = (jnp.einsum('bhij,bjhv->bihv', scores, vc)
             + jnp.einsum('bihk,bhkv->bihv', qc, s) * xi[None, :, :, None])
        s = g_chunk[None, :, None, None] * s + jnp.einsum('bjhk,bjhv->bhkv', kc * zeta[None, :, :, None], vc)
        return s, o

    s_new, o = lax.scan(step, s0, (to_chunks(q), to_chunks(k), to_chunks(v)))
    o = jnp.moveaxis(o, 0, 1).reshape(B, L, H, v.shape[-1])
    return o, s_new


def head_group_norm(o, gain):
    B, L, H, DV = o.shape
    mu = jnp.mean(o, axis=-1, keepdims=True)
    var = jnp.mean(jnp.square(o - mu), axis=-1, keepdims=True)
    o = (o - mu) * lax.rsqrt(var + EPS)
    return o.reshape(B, L, H * DV) * gain.astype(jnp.float32)


def hier_moe(h, w_grp, w_exp, w1, w3, w2):
    T, D = h.shape
    logits_g = jnp.einsum('td,dg->tg', h, w_grp).astype(jnp.float32)
    p_grp = jax.nn.softmax(logits_g, axis=-1)
    g_idx = jnp.argmax(logits_g, axis=-1)
    p_g = jnp.take_along_axis(p_grp, g_idx[:, None], axis=1)[:, 0]
    logits_e = jnp.einsum('td,dge->tge', h, w_exp).astype(jnp.float32)
    le = jnp.take_along_axis(logits_e, g_idx[:, None, None], axis=1)[:, 0]
    top_v, top_i = lax.top_k(le, TOP_K_EXPERT)
    p_e = jax.nn.softmax(top_v, axis=-1)
    expert = g_idx[:, None] * EXPERTS_PER_GROUP + top_i
    weight = p_g[:, None] * p_e
    A = T * TOP_K_EXPERT
    a_exp = expert.reshape(A).astype(jnp.int32)
    a_w = weight.reshape(A)
    a_tok = jnp.repeat(jnp.arange(T, dtype=jnp.int32), TOP_K_EXPERT)
    order = jnp.argsort(a_exp)
    s_exp, s_tok, s_w = a_exp[order], a_tok[order], a_w[order]
    counts = jnp.bincount(a_exp, length=N_EXPERTS)
    padded = ((counts + MOE_BLOCK - 1) // MOE_BLOCK) * MOE_BLOCK
    pad_end = jnp.cumsum(padded)
    pad_start = pad_end - padded
    start = jnp.cumsum(counts) - counts
    slot = pad_start[s_exp] + (jnp.arange(A, dtype=jnp.int32) - start[s_exp])
    n_blocks = -(-A // MOE_BLOCK) + N_EXPERTS
    n_slots = n_blocks * MOE_BLOCK
    slot_tok = jnp.zeros((n_slots,), jnp.int32).at[slot].set(s_tok)
    slot_w = jnp.zeros((n_slots,), jnp.float32).at[slot].set(s_w)
    block_exp = jnp.minimum(jnp.searchsorted(pad_end, jnp.arange(n_blocks) * MOE_BLOCK, side='right'), N_EXPERTS - 1)

    def run_block(args):
        tok, w, e = args
        xb = h[tok]
        a = jax.nn.silu(xb @ w1[e]) * (xb @ w3[e])
        return ((a @ w2[e]).astype(jnp.float32) * w[:, None]).astype(h.dtype)

    out_blocks = lax.map(run_block, (slot_tok.reshape(n_blocks, MOE_BLOCK), slot_w.reshape(n_blocks, MOE_BLOCK), block_exp))
    return jax.ops.segment_sum(out_blocks.reshape(n_slots, D), slot_tok, num_segments=T)


def layer(x, pool_hist, pool_valid, ret_state, pos0, g_mix, w_in, w_pool, pool_scale, ret_gn, w_out, g_ffn, w_grp, w_exp, w1, w3, w2):
    B, L, _ = x.shape
    h = rmsnorm(x, g_mix)
    z = h @ w_in
    u, q, k, v, ga, gb = jnp.split(z, IN_SPLITS, axis=-1)
    pool_out, new_hist = pool_mixer(u, pool_hist, pool_valid, w_pool, pool_scale)
    pos = (pos0 + jnp.arange(L)).astype(jnp.float32)
    qf = rotary(q.astype(jnp.float32).reshape(B, L, RET_HEADS, RET_DK), pos)
    kf = rotary(k.astype(jnp.float32).reshape(B, L, RET_HEADS, RET_DK), pos) * (RET_DK ** -0.5)
    vf = v.astype(jnp.float32).reshape(B, L, RET_HEADS, RET_DV)
    ret, s_new = retention(qf, kf, vf, ret_state.astype(jnp.float32))
    ret_out = head_group_norm(ret, ret_gn)
    mixed = (jax.nn.sigmoid(ga.astype(jnp.float32)) * pool_out
             + jax.nn.sigmoid(gb.astype(jnp.float32)) * ret_out).astype(x.dtype)
    x = x + mixed @ w_out
    h2 = rmsnorm(x, g_ffn).reshape(B * L, D_MODEL)
    x = x + hier_moe(h2, w_grp, w_exp, w1, w3, w2).reshape(B, L, D_MODEL)
    return x, new_hist, s_new


def setup_inputs(seed: int = 0) -> dict:
    key = jax.random.key(seed)
    ks = jax.random.split(key, 18)

    def nrm(k, shape, scale):
        return jax.random.normal(k, shape, jnp.float32) * scale

    return {
        'x_prompt': nrm(ks[0], (BATCH, SEQ, D_MODEL), 1.0),
        'x_sample': nrm(ks[1], (DEC_BATCH, DEC_SEQ, D_MODEL), 1.0),
        'state_pool': nrm(ks[2], (DEPTH, DEC_BATCH, POOL_HIST, POOL_IN), 1.0),
        'state_ret': nrm(ks[3], (DEPTH, DEC_BATCH, RET_HEADS, RET_DK, RET_DV), 0.5),
        'g_mix': 1.0 + nrm(ks[4], (DEPTH, D_MODEL), 0.05),
        'w_in': nrm(ks[5], (DEPTH, D_MODEL, IN_WIDTH), D_MODEL ** -0.5),
        'w_pool': nrm(ks[6], (DEPTH, POOL_GROUPS, POOL_GC, POOL_OUT_GC), POOL_GC ** -0.5),
        'pool_scale': 1.0 + nrm(ks[7], (DEPTH, D_MODEL), 0.1),
        'ret_gn': 1.0 + nrm(ks[8], (DEPTH, D_MODEL), 0.05),
        'w_out': nrm(ks[9], (DEPTH, D_MODEL, D_MODEL), D_MODEL ** -0.5),
        'g_ffn': 1.0 + nrm(ks[10], (DEPTH, D_MODEL), 0.05),
        'w_grp': nrm(ks[11], (DEPTH, D_MODEL, N_GROUPS), D_MODEL ** -0.5),
        'w_exp': nrm(ks[12], (DEPTH, D_MODEL, N_GROUPS, EXPERTS_PER_GROUP), D_MODEL ** -0.5),
        'w1': nrm(ks[13], (DEPTH, N_EXPERTS, D_MODEL, D_EXPERT), D_MODEL ** -0.5),
        'w3': nrm(ks[14], (DEPTH, N_EXPERTS, D_MODEL, D_EXPERT), D_MODEL ** -0.5),
        'w2': nrm(ks[15], (DEPTH, N_EXPERTS, D_EXPERT, D_MODEL), D_EXPERT ** -0.5),
        'g_final': 1.0 + nrm(ks[16], (D_MODEL,), 0.05),
    }


def reference(x_prompt, x_sample, state_pool, state_ret, g_mix, w_in, w_pool, pool_scale, ret_gn, w_out, g_ffn, w_grp, w_exp, w1, w3, w2, g_final):
    bp = x_prompt.shape[0]
    yp, ys = x_prompt, x_sample
    pool_p, ret_p, pool_s, ret_s = [], [], [], []
    for l in range(DEPTH):
        params = (g_mix[l], w_in[l], w_pool[l], pool_scale[l], ret_gn[l], w_out[l], g_ffn[l], w_grp[l], w_exp[l], w1[l], w3[l], w2[l])
        yp, hp, sp = layer(yp, jnp.zeros((bp, POOL_HIST, POOL_IN), yp.dtype), jnp.zeros((POOL_HIST,), jnp.float32),
                           jnp.zeros((bp, RET_HEADS, RET_DK, RET_DV), jnp.float32), 0, *params)
        ys, hs, ss = layer(ys, state_pool[l], jnp.ones((POOL_HIST,), jnp.float32), state_ret[l], PAST_LEN, *params)
        pool_p.append(hp)
        ret_p.append(sp)
        pool_s.append(hs)
        ret_s.append(ss)
    y_prompt = rmsnorm(yp, g_final)
    y_sample = rmsnorm(ys, g_final)
    new_pool_prompt = jnp.stack(pool_p, axis=0)
    new_ret_prompt = jnp.stack(ret_p, axis=0)
    new_pool_sample = jnp.stack(pool_s, axis=0)
    new_ret_sample = jnp.stack(ret_s, axis=0)
    return (y_prompt, y_sample, new_pool_prompt, new_ret_prompt, new_pool_sample, new_ret_sample)
```

```python
import functools

import jax
import jax.numpy as jnp
from jax import lax
from jax.experimental import pallas as pl
from jax.experimental.pallas import tpu as pltpu

D_MODEL = 1024
EPS = 1e-6
POOL_GROUPS = 4
POOL_IN = D_MODEL // 2
POOL_GC = POOL_IN // POOL_GROUPS
POOL_OUT_GC = D_MODEL // POOL_GROUPS
POOL_WINDOWS = (2, 4, 8, 16)
POOL_HIST = max(POOL_WINDOWS) - 1
RET_HEADS = 4
RET_DK = D_MODEL // 8
RET_DV = D_MODEL // RET_HEADS
ROPE_BASE = 10000.0
PAST_LEN = 16384
N_GROUPS = 4
EXPERTS_PER_GROUP = 8
N_EXPERTS = N_GROUPS * EXPERTS_PER_GROUP
D_EXPERT = D_MODEL // 4
QK_W = RET_HEADS * RET_DK
V_W = RET_HEADS * RET_DV
OFF_U = 0
OFF_Q = POOL_IN
OFF_K = OFF_Q + QK_W
OFF_V = OFF_K + QK_W
OFF_GA = OFF_V + V_W
OFF_GB = OFF_GA + D_MODEL
IN_WIDTH = OFF_GB + D_MODEL

LANES = 128
SUBLANES = 8
HALO = 16
RET_CHUNK = 256
MOE_BLOCK = 256
ROUTER_W = LANES
VMEM_LIMIT = 56 * 1024 * 1024

_BF = jnp.bfloat16
_F32 = jnp.float32


def _rms(x, g):
    inv = lax.rsqrt(jnp.mean(x * x, axis=-1, keepdims=True) + EPS)
    return x * inv * g


def _dot(a, b):
    return jnp.dot(a, b, preferred_element_type=_F32)


def _rotary(x, cos2, sin2):
    return x * cos2 + pltpu.roll(x, RET_DK // 2, 1) * sin2


def _route(logits):
    lane = lax.broadcasted_iota(jnp.int32, logits.shape, 1)
    neg = jnp.float32(-jnp.inf)
    big = jnp.int32(1 << 20)
    lg = jnp.where(lane < N_GROUPS, logits, neg)
    mg = jnp.max(lg, axis=-1, keepdims=True)
    g_idx = jnp.min(jnp.where(lg == mg, lane, big), axis=-1, keepdims=True)
    p_g = 1.0 / jnp.sum(jnp.exp(lg - mg), axis=-1, keepdims=True)
    lo = N_GROUPS + g_idx * EXPERTS_PER_GROUP
    in_grp = (lane >= lo) & (lane < lo + EXPERTS_PER_GROUP)
    le = jnp.where(in_grp, logits, neg)
    m1 = jnp.max(le, axis=-1, keepdims=True)
    i1 = jnp.min(jnp.where(le == m1, lane, big), axis=-1, keepdims=True)
    le2 = jnp.where(lane == i1, neg, le)
    m2 = jnp.max(le2, axis=-1, keepdims=True)
    i2 = jnp.min(jnp.where(le2 == m2, lane, big), axis=-1, keepdims=True)
    t = jnp.exp(m2 - m1)
    den = 1.0 + t
    return i1 - N_GROUPS, i2 - N_GROUPS, p_g * (1.0 / den), p_g * (t / den)


def _post_mix(x, mixed_ref, w_out_ref, g_ffn_ref, w_rt_ref, x1_ref, h2_ref, re_ref, rw_ref):
    x1 = x + _dot(mixed_ref[...], w_out_ref[...])
    x1_ref[...] = x1
    h2 = _rms(x1, g_ffn_ref[...])
    h2_ref[...] = h2
    e0, e1, w0, w1 = _route(_dot(h2.astype(_BF), w_rt_ref[...]))
    lane = lax.broadcasted_iota(jnp.int32, re_ref.shape, 1)
    re_ref[...] = jnp.where(lane == 0, e0, jnp.where(lane == 1, e1, 0))
    rw_ref[...] = jnp.where(lane == 0, w0, jnp.where(lane == 1, w1, 0.0))


def _group_norm(o, gain):
    mu = jnp.mean(o, axis=-1, keepdims=True)
    d = o - mu
    var = jnp.mean(d * d, axis=-1, keepdims=True)
    return d * lax.rsqrt(var + EPS) * gain


def _mixer_prompt_kernel(x_ref, g_mix_ref, w_in_ref, w_pool_ref, pscale_ref, gn_ref, w_out_ref, g_ffn_ref,
                         w_rt_ref, cos_ref, sin_ref, dmask_ref, xi_ref, zeta_ref, gc_ref,
                         x1_ref, h2_ref, re_ref, rw_ref, npool_ref, nret_ref,
                         u_scr, s_scr, mixed_scr):
    c = pl.program_id(1)
    n_c = pl.num_programs(1)
    C = x_ref.shape[0]

    @pl.when(c == 0)
    def _():
        u_scr[0:HALO, :] = jnp.zeros((HALO, POOL_IN), _F32)
        s_scr[...] = jnp.zeros(s_scr.shape, _F32)

    x = x_ref[...]
    h = _rms(x, g_mix_ref[...]).astype(_BF)

    u_scr[HALO:HALO + C, :] = _dot(h, w_in_ref[:, OFF_U:OFF_U + POOL_IN])
    q = _dot(h, w_in_ref[:, OFF_Q:OFF_Q + QK_W])
    k = _dot(h, w_in_ref[:, OFF_K:OFF_K + QK_W])
    cos2 = cos_ref[...]
    sin2 = sin_ref[...]
    pos1 = (c * C + 1 + lax.broadcasted_iota(jnp.int32, (C, POOL_GC), 0)).astype(_F32)

    for j in range(RET_HEADS):
        win = POOL_WINDOWS[j]
        cs = slice(j * POOL_GC, (j + 1) * POOL_GC)
        u_j = u_scr[HALO:HALO + C, cs]
        s = u_j
        for d in range(1, win):
            s = s + u_scr[HALO - d:HALO - d + C, cs]
        pooled = s / jnp.minimum(pos1, jnp.float32(win)) - u_j
        osl = slice(j * POOL_OUT_GC, (j + 1) * POOL_OUT_GC)
        pool_out = _dot(pooled.astype(_BF), w_pool_ref[j]) * pscale_ref[:, osl]

        qs = slice(j * RET_DK, (j + 1) * RET_DK)
        qb = _rotary(q[:, qs], cos2, sin2).astype(_BF)
        kf = _rotary(k[:, qs], cos2, sin2) * (RET_DK ** -0.5)
        kb = kf.astype(_BF)
        vb = _dot(h, w_in_ref[:, OFF_V + j * RET_DV:OFF_V + (j + 1) * RET_DV]).astype(_BF)
        scores = lax.dot_general(qb, kb, (((1,), (1,)), ((), ())), preferred_element_type=_F32) * dmask_ref[j]
        s_old = s_scr[j]
        o = _dot(scores.astype(_BF), vb) + _dot(qb, s_old.astype(_BF)) * xi_ref[j]
        kz = (kf * zeta_ref[j]).astype(_BF)
        s_scr[j] = gc_ref[j] * s_old + lax.dot_general(kz, vb, (((0,), (0,)), ((), ())),
                                                       preferred_element_type=_F32)
        ret_out = _group_norm(o, gn_ref[:, osl])

        ga = _dot(h, w_in_ref[:, OFF_GA + j * RET_DV:OFF_GA + (j + 1) * RET_DV])
        gb = _dot(h, w_in_ref[:, OFF_GB + j * RET_DV:OFF_GB + (j + 1) * RET_DV])
        mixed_scr[:, osl] = (jax.nn.sigmoid(ga) * pool_out + jax.nn.sigmoid(gb) * ret_out).astype(_BF)

    _post_mix(x, mixed_scr, w_out_ref, g_ffn_ref, w_rt_ref, x1_ref, h2_ref, re_ref, rw_ref)

    u_scr[0:HALO, :] = u_scr[C:C + HALO, :]

    @pl.when(c == n_c - 1)
    def _():
        npool_ref[...] = u_scr[HALO + C - POOL_HIST:HALO + C, :]
        nret_ref[...] = s_scr[...]


def _decay_tables(C):
    log_g = jnp.log(1.0 - jnp.exp2(-5.0 - jnp.arange(RET_HEADS, dtype=_F32)))
    i = jnp.arange(C, dtype=_F32)
    diff = i[:, None] - i[None, :]
    dmask = jnp.where(diff >= 0, jnp.exp(jnp.maximum(diff, 0.0)[None] * log_g[:, None, None]), 0.0)
    xi = jnp.exp((i[None, :] + 1.0) * log_g[:, None])
    zeta = jnp.exp((C - 1.0 - i)[None, :] * log_g[:, None])
    g_chunk = jnp.exp(C * log_g)
    return dmask, xi, zeta, g_chunk


def _rope_tables(pos):
    half = RET_DK // 2
    freqs = ROPE_BASE ** (-jnp.arange(half, dtype=_F32) / half)
    ang = pos[:, None] * freqs[None, :]
    cos, sin = jnp.cos(ang), jnp.sin(ang)
    return jnp.concatenate([cos, cos], axis=-1), jnp.concatenate([-sin, sin], axis=-1)


def _full(shape):
    n = len(shape)
    return pl.BlockSpec(shape, lambda *_: (0,) * n)


def _mixer_prompt(x, wts):
    B, L, _ = x.shape
    C = RET_CHUNK if L % RET_CHUNK == 0 else L
    n_c = L // C
    T = B * L
    dmask, xi, zeta, g_chunk = _decay_tables(C)
    xi_b = jnp.broadcast_to(xi[:, :, None], (RET_HEADS, C, RET_DV))
    zeta_b = jnp.broadcast_to(zeta[:, :, None], (RET_HEADS, C, RET_DK))
    gc_b = jnp.broadcast_to(g_chunk[:, None, None], (RET_HEADS, 1, RET_DV))
    cos2, sin2 = _rope_tables(jnp.arange(L).astype(_F32))

    tok = lambda b, c: (b * n_c + c, 0)
    in_specs = [
        pl.BlockSpec((None, C, D_MODEL), lambda b, c: (b, c, 0)),
        _full((1, D_MODEL)), _full((D_MODEL, IN_WIDTH)), _full((POOL_GROUPS, POOL_GC, POOL_OUT_GC)),
        _full((1, D_MODEL)), _full((1, D_MODEL)), _full((D_MODEL, D_MODEL)), _full((1, D_MODEL)),
        _full((D_MODEL, ROUTER_W)),
        pl.BlockSpec((C, RET_DK), lambda b, c: (c, 0)), pl.BlockSpec((C, RET_DK), lambda b, c: (c, 0)),
        _full((RET_HEADS, C, C)), _full((RET_HEADS, C, RET_DV)), _full((RET_HEADS, C, RET_DK)),
        _full((RET_HEADS, 1, RET_DV)),
    ]
    out_shape = (
        jax.ShapeDtypeStruct((T, D_MODEL), _F32),
        jax.ShapeDtypeStruct((T, D_MODEL), _F32),
        jax.ShapeDtypeStruct((T, LANES), jnp.int32),
        jax.ShapeDtypeStruct((T, LANES), _F32),
        jax.ShapeDtypeStruct((B, POOL_HIST, POOL_IN), _F32),
        jax.ShapeDtypeStruct((B, RET_HEADS, RET_DK, RET_DV), _F32),
    )
    out_specs = (
        pl.BlockSpec((C, D_MODEL), tok), pl.BlockSpec((C, D_MODEL), tok),
        pl.BlockSpec((C, LANES), tok), pl.BlockSpec((C, LANES), tok),
        pl.BlockSpec((None, POOL_HIST, POOL_IN), lambda b, c: (b, 0, 0)),
        pl.BlockSpec((None, RET_HEADS, RET_DK, RET_DV), lambda b, c: (b, 0, 0, 0)),
    )
    return pl.pallas_call(
        _mixer_prompt_kernel,
        grid=(B, n_c),
        in_specs=in_specs,
        out_specs=out_specs,
        out_shape=out_shape,
        scratch_shapes=[pltpu.VMEM((HALO + C, POOL_IN), _F32),
                        pltpu.VMEM((RET_HEADS, RET_DK, RET_DV), _F32),
                        pltpu.VMEM((C, D_MODEL), _BF)],
        compiler_params=pltpu.CompilerParams(dimension_semantics=("arbitrary", "arbitrary"),
                                             vmem_limit_bytes=VMEM_LIMIT),
        name="mixer_prompt",
    )(x, wts["g_mix"], wts["w_in"], wts["w_pool"], wts["pool_scale"], wts["ret_gn"], wts["w_out"],
      wts["g_ffn"], wts["w_rt"], cos2, sin2, dmask, xi_b, zeta_b, gc_b)


def _prep_weights(g_mix, w_in, w_pool, pool_scale, ret_gn, w_out, g_ffn, w_grp, w_exp):
    w_rt = jnp.concatenate([w_grp, w_exp.reshape(D_MODEL, N_EXPERTS)], axis=1)
    w_rt = jnp.pad(w_rt, ((0, 0), (0, ROUTER_W - w_rt.shape[1])))
    row = lambda v: v.reshape(1, D_MODEL)
    return dict(g_mix=row(g_mix), w_in=w_in.astype(_BF), w_pool=w_pool.astype(_BF), pool_scale=row(pool_scale),
                ret_gn=row(ret_gn), w_out=w_out.astype(_BF), g_ffn=row(g_ffn), w_rt=w_rt.astype(_BF))


SAMPLE_TB = 8


def _mixer_sample_kernel(x_ref, spool_ref, sret_ref, g_mix_ref, w_in_ref, w_pool_ref, pscale_ref, gn_ref,
                         w_out_ref, g_ffn_ref, w_rt_ref, cos_ref, sin_ref, dm_ref, xi_ref, zeta_ref, gc_ref,
                         x1_ref, h2_ref, re_ref, rw_ref, npool_ref, nret_ref,
                         u_scr, q_scr, k_scr, qt_scr, kt_scr, v_scr, ga_scr, gb_scr, pooled_scr, o_scr, mixed_scr):
    t = pl.program_id(0)
    n_t = pl.num_programs(0)
    Bs = x_ref.shape[0]
    TB = sret_ref.shape[0]

    @pl.when(t == 0)
    def _():
        h = _rms(x_ref[...], g_mix_ref[...]).astype(_BF)
        u_scr[...] = _dot(h, w_in_ref[:, OFF_U:OFF_U + POOL_IN])
        q = _dot(h, w_in_ref[:, OFF_Q:OFF_Q + QK_W])
        k = _dot(h, w_in_ref[:, OFF_K:OFF_K + QK_W])
        for j in range(RET_HEADS):
            qs = slice(j * RET_DK, (j + 1) * RET_DK)
            qf = _rotary(q[:, qs], cos_ref[...], sin_ref[...])
            kf = _rotary(k[:, qs], cos_ref[...], sin_ref[...]) * (RET_DK ** -0.5)
            q_scr[:, qs] = qf
            k_scr[:, qs] = kf
            qt_scr[j] = qf.T
            kt_scr[j] = kf.T
        v_scr[...] = _dot(h, w_in_ref[:, OFF_V:OFF_V + V_W])
        ga_scr[...] = _dot(h, w_in_ref[:, OFF_GA:OFF_GA + D_MODEL])
        gb_scr[...] = _dot(h, w_in_ref[:, OFF_GB:OFF_GB + D_MODEL])

    shift = (Bs - t * TB) % Bs
    qt = [pltpu.roll(qt_scr[j], shift, 1) for j in range(RET_HEADS)]
    kt = [pltpu.roll(kt_scr[j], shift, 1) for j in range(RET_HEADS)]

    blk = pl.ds(pl.multiple_of(t * TB, TB), TB)
    u_blk = u_scr[blk, :]
    q_blk = q_scr[blk, :]
    k_blk = k_scr[blk, :]
    v_blk = v_scr[blk, :]
    score = [jnp.sum(q_blk[:, j * RET_DK:(j + 1) * RET_DK] * k_blk[:, j * RET_DK:(j + 1) * RET_DK],
                     axis=-1, keepdims=True) * dm_ref[j] for j in range(RET_HEADS)]

    pooled_rows, o_rows = [], []
    for i in range(TB):
        u_row = u_blk[i:i + 1, :]
        groups = []
        for g, win in enumerate(POOL_WINDOWS):
            cs = slice(g * POOL_GC, (g + 1) * POOL_GC)
            hist_sum = jnp.sum(spool_ref[i, POOL_HIST - (win - 1):POOL_HIST, cs], axis=0, keepdims=True)
            groups.append((u_row[:, cs] + hist_sum) / jnp.float32(win) - u_row[:, cs])
        pooled_rows.append(jnp.concatenate(groups, axis=1))
        npool_ref[i, 0:POOL_HIST - 1, :] = spool_ref[i, 1:POOL_HIST, :]
        npool_ref[i, POOL_HIST - 1:POOL_HIST, :] = u_row

        heads = []
        for j in range(RET_HEADS):
            s_old = sret_ref[i, j]
            v_row = v_blk[i:i + 1, j * RET_DV:(j + 1) * RET_DV]
            qcol = qt[j][:, i:i + 1]
            kcol = kt[j][:, i:i + 1]
            qs_old = jnp.sum(qcol * s_old, axis=0, keepdims=True)
            heads.append(score[j][i:i + 1, :] * v_row + qs_old * xi_ref[j])
            nret_ref[i, j] = gc_ref[j] * s_old + (kcol * zeta_ref[j]) * v_row
        o_rows.append(jnp.concatenate(heads, axis=1))
    pooled_scr[blk, :] = jnp.concatenate(pooled_rows, axis=0)
    o_scr[blk, :] = jnp.concatenate(o_rows, axis=0)

    @pl.when(t == n_t - 1)
    def _():
        for j in range(RET_HEADS):
            cs = slice(j * POOL_GC, (j + 1) * POOL_GC)
            osl = slice(j * POOL_OUT_GC, (j + 1) * POOL_OUT_GC)
            pool_out = _dot(pooled_scr[:, cs].astype(_BF), w_pool_ref[j]) * pscale_ref[:, osl]
            ret_out = _group_norm(o_scr[:, osl], gn_ref[:, osl])
            mixed_scr[:, osl] = (jax.nn.sigmoid(ga_scr[:, osl]) * pool_out
                                 + jax.nn.sigmoid(gb_scr[:, osl]) * ret_out).astype(_BF)
        _post_mix(x_ref[...], mixed_scr, w_out_ref, g_ffn_ref, w_rt_ref, x1_ref, h2_ref, re_ref, rw_ref)


def _mixer_sample(x, state_pool, state_ret, wts, pos0):
    Bs = x.shape[0]
    assert Bs == LANES and Bs % SAMPLE_TB == 0
    TB = SAMPLE_TB
    dmask, xi, zeta, g_chunk = _decay_tables(1)
    dm_b = jnp.broadcast_to(dmask, (RET_HEADS, 1, 1))
    xi_b = jnp.broadcast_to(xi[:, :, None], (RET_HEADS, 1, RET_DV))
    zeta_b = jnp.broadcast_to(zeta[:, :, None], (RET_HEADS, 1, 1))
    gc_b = jnp.broadcast_to(g_chunk[:, None, None], (RET_HEADS, 1, RET_DV))
    cos2, sin2 = _rope_tables((pos0 + jnp.arange(1)).astype(_F32))

    in_specs = [
        _full((Bs, D_MODEL)),
        pl.BlockSpec((TB, POOL_HIST, POOL_IN), lambda t: (t, 0, 0)),
        pl.BlockSpec((TB, RET_HEADS, RET_DK, RET_DV), lambda t: (t, 0, 0, 0)),
        _full((1, D_MODEL)), _full((D_MODEL, IN_WIDTH)), _full((POOL_GROUPS, POOL_GC, POOL_OUT_GC)),
        _full((1, D_MODEL)), _full((1, D_MODEL)), _full((D_MODEL, D_MODEL)), _full((1, D_MODEL)),
        _full((D_MODEL, ROUTER_W)),
        _full((1, RET_DK)), _full((1, RET_DK)),
        _full((RET_HEADS, 1, 1)), _full((RET_HEADS, 1, RET_DV)), _full((RET_HEADS, 1, 1)),
        _full((RET_HEADS, 1, RET_DV)),
    ]
    out_shape = (
        jax.ShapeDtypeStruct((Bs, D_MODEL), _F32),
        jax.ShapeDtypeStruct((Bs, D_MODEL), _F32),
        jax.ShapeDtypeStruct((Bs, LANES), jnp.int32),
        jax.ShapeDtypeStruct((Bs, LANES), _F32),
        jax.ShapeDtypeStruct((Bs, POOL_HIST, POOL_IN), _F32),
        jax.ShapeDtypeStruct((Bs, RET_HEADS, RET_DK, RET_DV), _F32),
    )
    out_specs = (
        _full((Bs, D_MODEL)), _full((Bs, D_MODEL)), _full((Bs, LANES)), _full((Bs, LANES)),
        pl.BlockSpec((TB, POOL_HIST, POOL_IN), lambda t: (t, 0, 0)),
        pl.BlockSpec((TB, RET_HEADS, RET_DK, RET_DV), lambda t: (t, 0, 0, 0)),
    )
    f32 = lambda *s: pltpu.VMEM(s, _F32)
    return pl.pallas_call(
        _mixer_sample_kernel,
        grid=(Bs // TB,),
        in_specs=in_specs,
        out_specs=out_specs,
        out_shape=out_shape,
        scratch_shapes=[f32(Bs, POOL_IN), f32(Bs, QK_W), f32(Bs, QK_W), f32(RET_HEADS, RET_DK, Bs),
                        f32(RET_HEADS, RET_DK, Bs), f32(Bs, V_W), f32(Bs, D_MODEL), f32(Bs, D_MODEL),
                        f32(Bs, POOL_IN), f32(Bs, V_W), pltpu.VMEM((Bs, D_MODEL), _BF)],
        compiler_params=pltpu.CompilerParams(dimension_semantics=("arbitrary",), vmem_limit_bytes=VMEM_LIMIT),
        name="mixer_sample",
    )(x, state_pool, state_ret, wts["g_mix"], wts["w_in"], wts["w_pool"], wts["pool_scale"], wts["ret_gn"],
      wts["w_out"], wts["g_ffn"], wts["w_rt"], cos2, sin2, dm_b, xi_b, zeta_b, gc_b)


def _plan_kernel(re_ref, rank_ref, cnt_ref, carry_scr):
    i = pl.program_id(0)
    TP = re_ref.shape[0]

    @pl.when(i == 0)
    def _():
        carry_scr[...] = jnp.zeros(carry_scr.shape, _F32)

    re = re_ref[...]
    lane = lax.broadcasted_iota(jnp.int32, re.shape, 1)
    m0 = lane == re[:, 0:1]
    m1 = lane == re[:, 1:2]
    onehot = jnp.where(m0 | m1, 1.0, 0.0)
    r_i = lax.broadcasted_iota(jnp.int32, (TP, TP), 0)
    c_i = lax.broadcasted_iota(jnp.int32, (TP, TP), 1)
    tri = jnp.where(c_i < r_i, 1.0, 0.0).astype(_BF)
    before = _dot(tri, onehot.astype(_BF)) + carry_scr[...]
    r0 = jnp.sum(jnp.where(m0, before, 0.0), axis=-1, keepdims=True)
    r1 = jnp.sum(jnp.where(m1, before, 0.0), axis=-1, keepdims=True)
    rank_ref[...] = jnp.where(lane == 0, r0, jnp.where(lane == 1, r1, 0.0)).astype(jnp.int32)
    carry_scr[...] += jnp.sum(onehot, axis=0, keepdims=True)
    cnt_ref[...] = carry_scr[...].astype(jnp.int32)


def _plan(re_all, tile):
    T = re_all.shape[0]
    assert T % tile == 0
    return pl.pallas_call(
        _plan_kernel,
        grid=(T // tile,),
        in_specs=[pl.BlockSpec((tile, LANES), lambda i: (i, 0))],
        out_specs=(pl.BlockSpec((tile, LANES), lambda i: (i, 0)), _full((1, LANES))),
        out_shape=(jax.ShapeDtypeStruct((T, LANES), jnp.int32), jax.ShapeDtypeStruct((1, LANES), jnp.int32)),
        scratch_shapes=[pltpu.VMEM((1, LANES), _F32)],
        compiler_params=pltpu.CompilerParams(dimension_semantics=("arbitrary",)),
        name="moe_plan",
    )(re_all)


IDX_CHUNK = 1024


def _scatter_kernel(zero_fill, zstart_ref, slots_hbm, h2_ref, *rest):
    if zero_fill:
        xs_hbm, idx_smem, zbuf, sem_idx, sem_rows = rest
    else:
        _, xs_hbm, idx_smem, zbuf, sem_idx, sem_rows = rest
    t = pl.program_id(0)
    TS = h2_ref.shape[0]

    if zero_fill:
        @pl.when(t == 0)
        def _():
            zbuf[...] = jnp.zeros(zbuf.shape, zbuf.dtype)
            for e in range(N_EXPERTS):
                z0 = pl.multiple_of(zstart_ref[e], SUBLANES)
                pltpu.make_async_copy(zbuf, xs_hbm.at[pl.ds(z0, MOE_BLOCK + SUBLANES)], sem_rows).start()
            for e in range(N_EXPERTS):
                z0 = pl.multiple_of(zstart_ref[e], SUBLANES)
                pltpu.make_async_copy(zbuf, xs_hbm.at[pl.ds(z0, MOE_BLOCK + SUBLANES)], sem_rows).wait()

    idx_cp = pltpu.make_async_copy(slots_hbm.at[pl.ds(t * IDX_CHUNK, IDX_CHUNK)], idx_smem, sem_idx)
    idx_cp.start()
    idx_cp.wait()

    def issue(r, carry):
        for kk in range(2):
            pltpu.make_async_copy(h2_ref.at[r], xs_hbm.at[idx_smem[2 * r + kk]], sem_rows).start()
        return carry

    lax.fori_loop(0, TS, issue, 0)
    for kk in range(2):
        pltpu.make_async_copy(h2_ref, xs_hbm.at[pl.ds(0, TS)], sem_rows).wait()


def _scatter(h2, slots, zstart, xs, n_rows, tile):
    T = h2.shape[0]
    assert T % tile == 0 and 2 * tile <= IDX_CHUNK and slots.shape[0] == (T // tile) * IDX_CHUNK
    zero_fill = xs is None
    any_spec = pl.BlockSpec(memory_space=pl.ANY)
    in_specs = [any_spec, pl.BlockSpec((tile, D_MODEL), lambda t, z: (t, 0))]
    args = [slots, h2]
    aliases = {}
    if not zero_fill:
        in_specs.append(any_spec)
        args.append(xs)
        aliases = {3: 0}
    grid_spec = pltpu.PrefetchScalarGridSpec(
        num_scalar_prefetch=1,
        grid=(T // tile,),
        in_specs=in_specs,
        out_specs=any_spec,
        scratch_shapes=[pltpu.SMEM((IDX_CHUNK,), jnp.int32), pltpu.VMEM((MOE_BLOCK + SUBLANES, D_MODEL), _F32),
                        pltpu.SemaphoreType.DMA, pltpu.SemaphoreType.DMA],
    )
    return pl.pallas_call(
        functools.partial(_scatter_kernel, zero_fill),
        grid_spec=grid_spec,
        out_shape=jax.ShapeDtypeStruct((n_rows, D_MODEL), _F32),
        input_output_aliases=aliases,
        compiler_params=pltpu.CompilerParams(dimension_semantics=("arbitrary",), has_side_effects=True),
        name="moe_scatter_first" if zero_fill else "moe_scatter_more",
    )(zstart, *args)


def _ffn_kernel(bexp_ref, nused_ref, xs_ref, w1_ref, w3_ref, w2_ref, ys_ref, w13_scr, w2_scr):
    i = pl.program_id(0)

    @pl.when(i < nused_ref[0])
    def _():
        e = bexp_ref[i]
        e_prev = bexp_ref[jnp.maximum(i - 1, 0)]

        @pl.when((i == 0) | (e != e_prev))
        def _():
            w13_scr[:, 0:D_EXPERT] = w1_ref[...].astype(_BF)
            w13_scr[:, D_EXPERT:2 * D_EXPERT] = w3_ref[...].astype(_BF)
            w2_scr[...] = w2_ref[...].astype(_BF)

        ab = _dot(xs_ref[...].astype(_BF), w13_scr[...])
        hid = jax.nn.silu(ab[:, 0:D_EXPERT]) * ab[:, D_EXPERT:2 * D_EXPERT]
        ys_ref[...] = _dot(hid.astype(_BF), w2_scr[...])


def _ffn(xs, block_exp, n_used, w1, w3, w2, n_blocks):
    blk = lambda i, be, nu: (jnp.minimum(i, nu[0] - 1), 0)
    wmap = lambda i, be, nu: (be[i], 0, 0)
    grid_spec = pltpu.PrefetchScalarGridSpec(
        num_scalar_prefetch=2,
        grid=(n_blocks,),
        in_specs=[pl.BlockSpec((MOE_BLOCK, D_MODEL), blk),
                  pl.BlockSpec((None, D_MODEL, D_EXPERT), wmap),
                  pl.BlockSpec((None, D_MODEL, D_EXPERT), wmap),
                  pl.BlockSpec((None, D_EXPERT, D_MODEL), wmap)],
        out_specs=pl.BlockSpec((MOE_BLOCK, D_MODEL), blk),
        scratch_shapes=[pltpu.VMEM((D_MODEL, 2 * D_EXPERT), _BF), pltpu.VMEM((D_EXPERT, D_MODEL), _BF)],
    )
    return pl.pallas_call(
        _ffn_kernel,
        grid_spec=grid_spec,
        out_shape=jax.ShapeDtypeStruct(xs.shape, _F32),
        compiler_params=pltpu.CompilerParams(dimension_semantics=("arbitrary",), vmem_limit_bytes=VMEM_LIMIT),
        name="moe_ffn",
    )(block_exp, n_used, xs, w1, w3, w2)


def _combine_kernel(slots_hbm, x1_ref, rw_ref, g_ref, ys_hbm, y_ref, idx_smem, buf, sem_idx, sem_rows):
    t = pl.program_id(0)
    TC = x1_ref.shape[0]
    idx_cp = pltpu.make_async_copy(slots_hbm.at[pl.ds(t * IDX_CHUNK, IDX_CHUNK)], idx_smem, sem_idx)
    idx_cp.start()
    idx_cp.wait()

    def issue(r, carry):
        for kk in range(2):
            pltpu.make_async_copy(ys_hbm.at[idx_smem[2 * r + kk]], buf.at[kk, r], sem_rows).start()
        return carry

    lax.fori_loop(0, TC, issue, 0)
    for kk in range(2):
        pltpu.make_async_copy(ys_hbm.at[pl.ds(0, TC)], buf.at[kk], sem_rows).wait()

    w = rw_ref[...]
    x2 = x1_ref[...] + (w[:, 0:1] * buf[0] + w[:, 1:2] * buf[1])
    y_ref[...] = _rms(x2, g_ref[...])


def _combine(x1, rw, slots, ys, g_final, tile):
    T = x1.shape[0]
    assert T % tile == 0 and 2 * tile <= IDX_CHUNK and slots.shape[0] == (T // tile) * IDX_CHUNK
    any_spec = pl.BlockSpec(memory_space=pl.ANY)
    return pl.pallas_call(
        _combine_kernel,
        grid=(T // tile,),
        in_specs=[any_spec, pl.BlockSpec((tile, D_MODEL), lambda t: (t, 0)),
                  pl.BlockSpec((tile, LANES), lambda t: (t, 0)), _full((1, D_MODEL)), any_spec],
        out_specs=pl.BlockSpec((tile, D_MODEL), lambda t: (t, 0)),
        out_shape=jax.ShapeDtypeStruct((T, D_MODEL), _F32),
        scratch_shapes=[pltpu.SMEM((IDX_CHUNK,), jnp.int32), pltpu.VMEM((2, tile, D_MODEL), _F32),
                        pltpu.SemaphoreType.DMA, pltpu.SemaphoreType.DMA],
        compiler_params=pltpu.CompilerParams(dimension_semantics=("arbitrary",), vmem_limit_bytes=VMEM_LIMIT),
        name="moe_combine",
    )(slots, x1, rw, g_final.reshape(1, D_MODEL), ys)


def _tile_for(n_tokens):
    tile = IDX_CHUNK // 2
    return tile if n_tokens % tile == 0 else n_tokens


def _chunked_slots(slot, tile):
    n_tiles = slot.shape[0] // tile
    s = slot.reshape(n_tiles, 2 * tile)
    return jnp.pad(s, ((0, 0), (0, IDX_CHUNK - 2 * tile))).reshape(-1)


def kernel(x_prompt, x_sample, state_pool, state_ret, g_mix, w_in, w_pool, pool_scale, ret_gn, w_out, g_ffn, w_grp, w_exp, w1, w3, w2, g_final):
    Bp, Lp, _ = x_prompt.shape
    Bs = x_sample.shape[0]
    Tp = Bp * Lp
    wts = _prep_weights(g_mix[0], w_in[0], w_pool[0], pool_scale[0], ret_gn[0], w_out[0], g_ffn[0], w_grp[0], w_exp[0])

    x1p, h2p, rep, rwp, npool_p, nret_p = _mixer_prompt(x_prompt, wts)
    x1s, h2s, res, rws, npool_s, nret_s = _mixer_sample(x_sample.reshape(Bs, D_MODEL), state_pool[0], state_ret[0],
                                                        wts, PAST_LEN)

    re_all = jnp.concatenate([rep, res], axis=0)
    T_all = Tp + Bs
    rank, counts = _plan(re_all, LANES)
    counts = counts[0, :N_EXPERTS]
    padded = ((counts + MOE_BLOCK - 1) // MOE_BLOCK) * MOE_BLOCK
    pad_end = jnp.cumsum(padded)
    pad_start = pad_end - padded
    e_all = re_all[:, :2]
    onehot = e_all[:, :, None] == jnp.arange(N_EXPERTS, dtype=jnp.int32)
    slot = jnp.sum(jnp.where(onehot, pad_start, 0), axis=-1) + rank[:, :2]
    n_blocks = -(-2 * T_all // MOE_BLOCK) + N_EXPERTS
    n_used = (pad_end[-1:] // MOE_BLOCK).astype(jnp.int32)
    block_exp = jnp.minimum(jnp.searchsorted(pad_end, jnp.arange(n_blocks, dtype=jnp.int32) * MOE_BLOCK, side="right"),
                            N_EXPERTS - 1).astype(jnp.int32)
    zstart = ((pad_start + counts) // SUBLANES * SUBLANES).astype(jnp.int32)
    n_rows = (n_blocks + 2) * MOE_BLOCK

    tile_p, tile_s = _tile_for(Tp), _tile_for(Bs)
    slots_p = _chunked_slots(slot[:Tp], tile_p)
    slots_s = _chunked_slots(slot[Tp:], tile_s)
    xs = _scatter(h2p, slots_p, zstart, None, n_rows, tile_p)
    xs = _scatter(h2s, slots_s, zstart, xs, n_rows, tile_s)
    ys = _ffn(xs, block_exp, n_used, w1[0], w3[0], w2[0], n_blocks)
    y_p = _combine(x1p, rwp, slots_p, ys, g_final, tile_p)
    y_s = _combine(x1s, rws, slots_s, ys, g_final, tile_s)

    return (y_p.reshape(Bp, Lp, D_MODEL), y_s.reshape(Bs, 1, D_MODEL),
            npool_p[None], nret_p[None], npool_s[None], nret_s[None])
```

```python
import functools

import jax
import jax.numpy as jnp
from jax import lax
from jax.experimental import pallas as pl
from jax.experimental.pallas import tpu as pltpu

D_MODEL = 1024
EPS = 1e-6
POOL_GROUPS = 4
POOL_IN = D_MODEL // 2
POOL_GC = POOL_IN // POOL_GROUPS
POOL_OUT_GC = D_MODEL // POOL_GROUPS
POOL_WINDOWS = (2, 4, 8, 16)
POOL_HIST = max(POOL_WINDOWS) - 1
RET_HEADS = 4
RET_DK = D_MODEL // 8
RET_DV = D_MODEL // RET_HEADS
ROPE_BASE = 10000.0
PAST_LEN = 16384
N_GROUPS = 4
EXPERTS_PER_GROUP = 8
N_EXPERTS = N_GROUPS * EXPERTS_PER_GROUP
D_EXPERT = D_MODEL // 4
QK_W = RET_HEADS * RET_DK
V_W = RET_HEADS * RET_DV
OFF_U = 0
OFF_Q = POOL_IN
OFF_K = OFF_Q + QK_W
OFF_V = OFF_K + QK_W
OFF_GA = OFF_V + V_W
OFF_GB = OFF_GA + D_MODEL
IN_WIDTH = OFF_GB + D_MODEL

LANES = 128
SUBLANES = 8
HALO = 16
RET_CHUNK = 256
MOE_BLOCK = 256
ROUTER_W = LANES
VMEM_LIMIT = 56 * 1024 * 1024

_BF = jnp.bfloat16
_F32 = jnp.float32


def _rms(x, g):
    inv = lax.rsqrt(jnp.mean(x * x, axis=-1, keepdims=True) + EPS)
    return x * inv * g


def _dot(a, b):
    return jnp.dot(a, b, preferred_element_type=_F32)


def _rotary(x, cos2, sin2):
    return x * cos2 + pltpu.roll(x, RET_DK // 2, 1) * sin2


def _route(logits):
    lane = lax.broadcasted_iota(jnp.int32, logits.shape, 1)
    neg = jnp.float32(-jnp.inf)
    big = jnp.int32(1 << 20)
    lg = jnp.where(lane < N_GROUPS, logits, neg)
    mg = jnp.max(lg, axis=-1, keepdims=True)
    g_idx = jnp.min(jnp.where(lg == mg, lane, big), axis=-1, keepdims=True)
    p_g = 1.0 / jnp.sum(jnp.exp(lg - mg), axis=-1, keepdims=True)
    lo = N_GROUPS + g_idx * EXPERTS_PER_GROUP
    in_grp = (lane >= lo) & (lane < lo + EXPERTS_PER_GROUP)
    le = jnp.where(in_grp, logits, neg)
    m1 = jnp.max(le, axis=-1, keepdims=True)
    i1 = jnp.min(jnp.where(le == m1, lane, big), axis=-1, keepdims=True)
    le2 = jnp.where(lane == i1, neg, le)
    m2 = jnp.max(le2, axis=-1, keepdims=True)
    i2 = jnp.min(jnp.where(le2 == m2, lane, big), axis=-1, keepdims=True)
    t = jnp.exp(m2 - m1)
    den = 1.0 + t
    return i1 - N_GROUPS, i2 - N_GROUPS, p_g * (1.0 / den), p_g * (t / den)


def _post_mix(x, mixed_ref, w_out_ref, g_ffn_ref, w_rt_ref, x1_ref, h2_ref, re_ref, rw_ref):
    x1 = x + _dot(mixed_ref[...], w_out_ref[...])
    x1_ref[...] = x1
    h2 = _rms(x1, g_ffn_ref[...])
    h2_ref[...] = h2
    e0, e1, w0, w1 = _route(_dot(h2.astype(_BF), w_rt_ref[...]))
    lane = lax.broadcasted_iota(jnp.int32, re_ref.shape, 1)
    re_ref[...] = jnp.where(lane == 0, e0, jnp.where(lane == 1, e1, 0))
    rw_ref[...] = jnp.where(lane == 0, w0, jnp.where(lane == 1, w1, 0.0))


def _group_norm(o, gain):
    mu = jnp.mean(o, axis=-1, keepdims=True)
    d = o - mu
    var = jnp.mean(d * d, axis=-1, keepdims=True)
    return d * lax.rsqrt(var + EPS) * gain


def _mixer_prompt_kernel(x_ref, g_mix_ref, w_in_ref, w_pool_ref, pscale_ref, gn_ref, w_out_ref, g_ffn_ref,
                         w_rt_ref, cos_ref, sin_ref, dmask_ref, xi_ref, zeta_ref, gc_ref,
                         x1_ref, h2_ref, re_ref, rw_ref, npool_ref, nret_ref,
                         u_scr, s_scr, mixed_scr):
    c = pl.program_id(1)
    n_c = pl.num_programs(1)
    C = x_ref.shape[0]

    @pl.when(c == 0)
    def _():
        u_scr[0:HALO, :] = jnp.zeros((HALO, POOL_IN), _F32)
        s_scr[...] = jnp.zeros(s_scr.shape, _F32)

    x = x_ref[...]
    h = _rms(x, g_mix_ref[...]).astype(_BF)

    u_scr[HALO:HALO + C, :] = _dot(h, w_in_ref[:, OFF_U:OFF_U + POOL_IN])
    q = _dot(h, w_in_ref[:, OFF_Q:OFF_Q + QK_W])
    k = _dot(h, w_in_ref[:, OFF_K:OFF_K + QK_W])
    cos2 = cos_ref[...]
    sin2 = sin_ref[...]
    pos1 = (c * C + 1 + lax.broadcasted_iota(jnp.int32, (C, POOL_GC), 0)).astype(_F32)

    for j in range(RET_HEADS):
        win = POOL_WINDOWS[j]
        cs = slice(j * POOL_GC, (j + 1) * POOL_GC)
        u_j = u_scr[HALO:HALO + C, cs]
        s = u_j
        for d in range(1, win):
            s = s + u_scr[HALO - d:HALO - d + C, cs]
        pooled = s / jnp.minimum(pos1, jnp.float32(win)) - u_j
        osl = slice(j * POOL_OUT_GC, (j + 1) * POOL_OUT_GC)
        pool_out = _dot(pooled.astype(_BF), w_pool_ref[j]) * pscale_ref[:, osl]

        qs = slice(j * RET_DK, (j + 1) * RET_DK)
        qb = _rotary(q[:, qs], cos2, sin2).astype(_BF)
        kf = _rotary(k[:, qs], cos2, sin2) * (RET_DK ** -0.5)
        kb = kf.astype(_BF)
        vb = _dot(h, w_in_ref[:, OFF_V + j * RET_DV:OFF_V + (j + 1) * RET_DV]).astype(_BF)
        scores = lax.dot_general(qb, kb, (((1,), (1,)), ((), ())), preferred_element_type=_F32) * dmask_ref[j]
        s_old = s_scr[j]
        o = _dot(scores.astype(_BF), vb) + _dot(qb, s_old.astype(_BF)) * xi_ref[j]
        kz = (kf * zeta_ref[j]).astype(_BF)
        s_scr[j] = gc_ref[j] * s_old + lax.dot_general(kz, vb, (((0,), (0,)), ((), ())),
                                                       preferred_element_type=_F32)
        ret_out = _group_norm(o, gn_ref[:, osl])

        ga = _dot(h, w_in_ref[:, OFF_GA + j * RET_DV:OFF_GA + (j + 1) * RET_DV])
        gb = _dot(h, w_in_ref[:, OFF_GB + j * RET_DV:OFF_GB + (j + 1) * RET_DV])
        mixed_scr[:, osl] = (jax.nn.sigmoid(ga) * pool_out + jax.nn.sigmoid(gb) * ret_out).astype(_BF)

    _post_mix(x, mixed_scr, w_out_ref, g_ffn_ref, w_rt_ref, x1_ref, h2_ref, re_ref, rw_ref)

    u_scr[0:HALO, :] = u_scr[C:C + HALO, :]

    @pl.when(c == n_c - 1)
    def _():
        npool_ref[...] = u_scr[HALO + C - POOL_HIST:HALO + C, :]
        nret_ref[...] = s_scr[...]


def _decay_tables(C):
    log_g = jnp.log(1.0 - jnp.exp2(-5.0 - jnp.arange(RET_HEADS, dtype=_F32)))
    i = jnp.arange(C, dtype=_F32)
    diff = i[:, None] - i[None, :]
    dmask = jnp.where(diff >= 0, jnp.exp(jnp.maximum(diff, 0.0)[None] * log_g[:, None, None]), 0.0)
    xi = jnp.exp((i[None, :] + 1.0) * log_g[:, None])
    zeta = jnp.exp((C - 1.0 - i)[None, :] * log_g[:, None])
    g_chunk = jnp.exp(C * log_g)
    return dmask, xi, zeta, g_chunk


def _rope_tables(pos):
    half = RET_DK // 2
    freqs = ROPE_BASE ** (-jnp.arange(half, dtype=_F32) / half)
    ang = pos[:, None] * freqs[None, :]
    cos, sin = jnp.cos(ang), jnp.sin(ang)
    return jnp.concatenate([cos, cos], axis=-1), jnp.concatenate([-sin, sin], axis=-1)


def _full(shape):
    n = len(shape)
    return pl.BlockSpec(shape, lambda *_: (0,) * n)


def _mixer_prompt(x, wts):
    B, L, _ = x.shape
    C = RET_CHUNK if L % RET_CHUNK == 0 else L
    n_c = L // C
    T = B * L
    dmask, xi, zeta, g_chunk = _decay_tables(C)
    xi_b = jnp.broadcast_to(xi[:, :, None], (RET_HEADS, C, RET_DV))
    zeta_b = jnp.broadcast_to(zeta[:, :, None], (RET_HEADS, C, RET_DK))
    gc_b = jnp.broadcast_to(g_chunk[:, None, None], (RET_HEADS, 1, RET_DV))
    cos2, sin2 = _rope_tables(jnp.arange(L).astype(_F32))

    tok = lambda b, c: (b * n_c + c, 0)
    in_specs = [
        pl.BlockSpec((None, C, D_MODEL), lambda b, c: (b, c, 0)),
        _full((1, D_MODEL)), _full((D_MODEL, IN_WIDTH)), _full((POOL_GROUPS, POOL_GC, POOL_OUT_GC)),
        _full((1, D_MODEL)), _full((1, D_MODEL)), _full((D_MODEL, D_MODEL)), _full((1, D_MODEL)),
        _full((D_MODEL, ROUTER_W)),
        pl.BlockSpec((C, RET_DK), lambda b, c: (c, 0)), pl.BlockSpec((C, RET_DK), lambda b, c: (c, 0)),
        _full((RET_HEADS, C, C)), _full((RET_HEADS, C, RET_DV)), _full((RET_HEADS, C, RET_DK)),
        _full((RET_HEADS, 1, RET_DV)),
    ]
    out_shape = (
        jax.ShapeDtypeStruct((T, D_MODEL), _F32),
        jax.ShapeDtypeStruct((T, D_MODEL), _F32),
        jax.ShapeDtypeStruct((T, LANES), jnp.int32),
        jax.ShapeDtypeStruct((T, LANES), _F32),
        jax.ShapeDtypeStruct((B, POOL_HIST, POOL_IN), _F32),
        jax.ShapeDtypeStruct((B, RET_HEADS, RET_DK, RET_DV), _F32),
    )
    out_specs = (
        pl.BlockSpec((C, D_MODEL), tok), pl.BlockSpec((C, D_MODEL), tok),
        pl.BlockSpec((C, LANES), tok), pl.BlockSpec((C, LANES), tok),
        pl.BlockSpec((None, POOL_HIST, POOL_IN), lambda b, c: (b, 0, 0)),
        pl.BlockSpec((None, RET_HEADS, RET_DK, RET_DV), lambda b, c: (b, 0, 0, 0)),
    )
    return pl.pallas_call(
        _mixer_prompt_kernel,
        grid=(B, n_c),
        in_specs=in_specs,
        out_specs=out_specs,
        out_shape=out_shape,
        scratch_shapes=[pltpu.VMEM((HALO + C, POOL_IN), _F32),
                        pltpu.VMEM((RET_HEADS, RET_DK, RET_DV), _F32),
                        pltpu.VMEM((C, D_MODEL), _BF)],
        compiler_params=pltpu.CompilerParams(dimension_semantics=("arbitrary", "arbitrary"),
                                             vmem_limit_bytes=VMEM_LIMIT),
        name="mixer_prompt",
    )(x, wts["g_mix"], wts["w_in"], wts["w_pool"], wts["pool_scale"], wts["ret_gn"], wts["w_out"],
      wts["g_ffn"], wts["w_rt"], cos2, sin2, dmask, xi_b, zeta_b, gc_b)


def _prep_weights(g_mix, w_in, w_pool, pool_scale, ret_gn, w_out, g_ffn, w_grp, w_exp):
    w_rt = jnp.concatenate([w_grp, w_exp.reshape(D_MODEL, N_EXPERTS)], axis=1)
    w_rt = jnp.pad(w_rt, ((0, 0), (0, ROUTER_W - w_rt.shape[1])))
    row = lambda v: v.reshape(1, D_MODEL)
    return dict(g_mix=row(g_mix), w_in=w_in.astype(_BF), w_pool=w_pool.astype(_BF), pool_scale=row(pool_scale),
                ret_gn=row(ret_gn), w_out=w_out.astype(_BF), g_ffn=row(g_ffn), w_rt=w_rt.astype(_BF))


SAMPLE_TB = 8


def _mixer_sample_kernel(x_ref, spool_ref, sret_ref, g_mix_ref, w_in_ref, w_pool_ref, pscale_ref, gn_ref,
                         w_out_ref, g_ffn_ref, w_rt_ref, cos_ref, sin_ref, dm_ref, xi_ref, zeta_ref, gc_ref,
                         x1_ref, h2_ref, re_ref, rw_ref, npool_ref, nret_ref,
                         u_scr, q_scr, k_scr, qt_scr, kt_scr, v_scr, ga_scr, gb_scr, pooled_scr, o_scr, mixed_scr):
    t = pl.program_id(0)
    n_t = pl.num_programs(0)
    Bs = x_ref.shape[0]
    TB = sret_ref.shape[0]

    @pl.when(t == 0)
    def _():
        h = _rms(x_ref[...], g_mix_ref[...]).astype(_BF)
        u_scr[...] = _dot(h, w_in_ref[:, OFF_U:OFF_U + POOL_IN])
        q = _dot(h, w_in_ref[:, OFF_Q:OFF_Q + QK_W])
        k = _dot(h, w_in_ref[:, OFF_K:OFF_K + QK_W])
        for j in range(RET_HEADS):
            qs = slice(j * RET_DK, (j + 1) * RET_DK)
            qf = _rotary(q[:, qs], cos_ref[...], sin_ref[...])
            kf = _rotary(k[:, qs], cos_ref[...], sin_ref[...]) * (RET_DK ** -0.5)
            q_scr[:, qs] = qf
            k_scr[:, qs] = kf
            qt_scr[j] = qf.T
            kt_scr[j] = kf.T
        v_scr[...] = _dot(h, w_in_ref[:, OFF_V:OFF_V + V_W])
        ga_scr[...] = _dot(h, w_in_ref[:, OFF_GA:OFF_GA + D_MODEL])
        gb_scr[...] = _dot(h, w_in_ref[:, OFF_GB:OFF_GB + D_MODEL])

    shift = (Bs - t * TB) % Bs
    qt = [pltpu.roll(qt_scr[j], shift, 1) for j in range(RET_HEADS)]
    kt = [pltpu.roll(kt_scr[j], shift, 1) for j in range(RET_HEADS)]

    blk = pl.ds(pl.multiple_of(t * TB, TB), TB)
    u_blk = u_scr[blk, :]
    q_blk = q_scr[blk, :]
    k_blk = k_scr[blk, :]
    v_blk = v_scr[blk, :]
    score = [jnp.sum(q_blk[:, j * RET_DK:(j + 1) * RET_DK] * k_blk[:, j * RET_DK:(j + 1) * RET_DK],
                     axis=-1, keepdims=True) * dm_ref[j] for j in range(RET_HEADS)]

    pooled_rows, o_rows = [], []
    for i in range(TB):
        u_row = u_blk[i:i + 1, :]
        groups = []
        for g, win in enumerate(POOL_WINDOWS):
            cs = slice(g * POOL_GC, (g + 1) * POOL_GC)
            hist_sum = jnp.sum(spool_ref[i, POOL_HIST - (win - 1):POOL_HIST, cs], axis=0, keepdims=True)
            groups.append((u_row[:, cs] + hist_sum) / jnp.float32(win) - u_row[:, cs])
        pooled_rows.append(jnp.concatenate(groups, axis=1))
        npool_ref[i, 0:POOL_HIST - 1, :] = spool_ref[i, 1:POOL_HIST, :]
        npool_ref[i, POOL_HIST - 1:POOL_HIST, :] = u_row

        heads = []
        for j in range(RET_HEADS):
            s_old = sret_ref[i, j]
            v_row = v_blk[i:i + 1, j * RET_DV:(j + 1) * RET_DV]
            qcol = qt[j][:, i:i + 1]
            kcol = kt[j][:, i:i + 1]
            qs_old = jnp.sum(qcol * s_old, axis=0, keepdims=True)
            heads.append(score[j][i:i + 1, :] * v_row + qs_old * xi_ref[j])
            nret_ref[i, j] = gc_ref[j] * s_old + (kcol * zeta_ref[j]) * v_row
        o_rows.append(jnp.concatenate(heads, axis=1))
    pooled_scr[blk, :] = jnp.concatenate(pooled_rows, axis=0)
    o_scr[blk, :] = jnp.concatenate(o_rows, axis=0)

    @pl.when(t == n_t - 1)
    def _():
        for j in range(RET_HEADS):
            cs = slice(j * POOL_GC, (j + 1) * POOL_GC)
            osl = slice(j * POOL_OUT_GC, (j + 1) * POOL_OUT_GC)
            pool_out = _dot(pooled_scr[:, cs].astype(_BF), w_pool_ref[j]) * pscale_ref[:, osl]
            ret_out = _group_norm(o_scr[:, osl], gn_ref[:, osl])
            mixed_scr[:, osl] = (jax.nn.sigmoid(ga_scr[:, osl]) * pool_out
                                 + jax.nn.sigmoid(gb_scr[:, osl]) * ret_out).astype(_BF)
        _post_mix(x_ref[...], mixed_scr, w_out_ref, g_ffn_ref, w_rt_ref, x1_ref, h2_ref, re_ref, rw_ref)


def _mixer_sample(x, state_pool, state_ret, wts, pos0):
    Bs = x.shape[0]
    assert Bs == LANES and Bs % SAMPLE_TB == 0
    TB = SAMPLE_TB
    dmask, xi, zeta, g_chunk = _decay_tables(1)
    dm_b = jnp.broadcast_to(dmask, (RET_HEADS, 1, 1))
    xi_b = jnp.broadcast_to(xi[:, :, None], (RET_HEADS, 1, RET_DV))
    zeta_b = jnp.broadcast_to(zeta[:, :, None], (RET_HEADS, 1, 1))
    gc_b = jnp.broadcast_to(g_chunk[:, None, None], (RET_HEADS, 1, RET_DV))
    cos2, sin2 = _rope_tables((pos0 + jnp.arange(1)).astype(_F32))

    in_specs = [
        _full((Bs, D_MODEL)),
        pl.BlockSpec((TB, POOL_HIST, POOL_IN), lambda t: (t, 0, 0)),
        pl.BlockSpec((TB, RET_HEADS, RET_DK, RET_DV), lambda t: (t, 0, 0, 0)),
        _full((1, D_MODEL)), _full((D_MODEL, IN_WIDTH)), _full((POOL_GROUPS, POOL_GC, POOL_OUT_GC)),
        _full((1, D_MODEL)), _full((1, D_MODEL)), _full((D_MODEL, D_MODEL)), _full((1, D_MODEL)),
        _full((D_MODEL, ROUTER_W)),
        _full((1, RET_DK)), _full((1, RET_DK)),
        _full((RET_HEADS, 1, 1)), _full((RET_HEADS, 1, RET_DV)), _full((RET_HEADS, 1, 1)),
        _full((RET_HEADS, 1, RET_DV)),
    ]
    out_shape = (
        jax.ShapeDtypeStruct((Bs, D_MODEL), _F32),
        jax.ShapeDtypeStruct((Bs, D_MODEL), _F32),
        jax.ShapeDtypeStruct((Bs, LANES), jnp.int32),
        jax.ShapeDtypeStruct((Bs, LANES), _F32),
        jax.ShapeDtypeStruct((Bs, POOL_HIST, POOL_IN), _F32),
        jax.ShapeDtypeStruct((Bs, RET_HEADS, RET_DK, RET_DV), _F32),
    )
    out_specs = (
        _full((Bs, D_MODEL)), _full((Bs, D_MODEL)), _full((Bs, LANES)), _full((Bs, LANES)),
        pl.BlockSpec((TB, POOL_HIST, POOL_IN), lambda t: (t, 0, 0)),
        pl.BlockSpec((TB, RET_HEADS, RET_DK, RET_DV), lambda t: (t, 0, 0, 0)),
    )
    f32 = lambda *s: pltpu.VMEM(s, _F32)
    return pl.pallas_call(
        _mixer_sample_kernel,
        grid=(Bs // TB,),
        in_specs=in_specs,
        out_specs=out_specs,
        out_shape=out_shape,
        scratch_shapes=[f32(Bs, POOL_IN), f32(Bs, QK_W), f32(Bs, QK_W), f32(RET_HEADS, RET_DK, Bs),
                        f32(RET_HEADS, RET_DK, Bs), f32(Bs, V_W), f32(Bs, D_MODEL), f32(Bs, D_MODEL),
                        f32(Bs, POOL_IN), f32(Bs, V_W), pltpu.VMEM((Bs, D_MODEL), _BF)],
        compiler_params=pltpu.CompilerParams(dimension_semantics=("arbitrary",), vmem_limit_bytes=VMEM_LIMIT),
        name="mixer_sample",
    )(x, state_pool, state_ret, wts["g_mix"], wts["w_in"], wts["w_pool"], wts["pool_scale"], wts["ret_gn"],
      wts["w_out"], wts["g_ffn"], wts["w_rt"], cos2, sin2, dm_b, xi_b, zeta_b, gc_b)


def _plan_kernel(re_ref, rank_ref, cnt_ref, carry_scr):
    i = pl.program_id(0)
    TP = re_ref.shape[0]

    @pl.when(i == 0)
    def _():
        carry_scr[...] = jnp.zeros(carry_scr.shape, _F32)

    re = re_ref[...]
    lane = lax.broadcasted_iota(jnp.int32, re.shape, 1)
    m0 = lane == re[:, 0:1]
    m1 = lane == re[:, 1:2]
    onehot = jnp.where(m0 | m1, 1.0, 0.0)
    r_i = lax.broadcasted_iota(jnp.int32, (TP, TP), 0)
    c_i = lax.broadcasted_iota(jnp.int32, (TP, TP), 1)
    tri = jnp.where(c_i < r_i, 1.0, 0.0).astype(_BF)
    before = _dot(tri, onehot.astype(_BF)) + carry_scr[...]
    r0 = jnp.sum(jnp.where(m0, before, 0.0), axis=-1, keepdims=True)
    r1 = jnp.sum(jnp.where(m1, before, 0.0), axis=-1, keepdims=True)
    rank_ref[...] = jnp.where(lane == 0, r0, jnp.where(lane == 1, r1, 0.0)).astype(jnp.int32)
    carry_scr[...] += jnp.sum(onehot, axis=0, keepdims=True)
    cnt_ref[...] = carry_scr[...].astype(jnp.int32)


def _plan_tile(n_tokens):
    return max(t for t in range(SUBLANES, 513, SUBLANES) if n_tokens % t == 0)


def _plan(re_all, tile):
    T = re_all.shape[0]
    assert T % tile == 0
    return pl.pallas_call(
        _plan_kernel,
        grid=(T // tile,),
        in_specs=[pl.BlockSpec((tile, LANES), lambda i: (i, 0))],
        out_specs=(pl.BlockSpec((tile, LANES), lambda i: (i, 0)), _full((1, LANES))),
        out_shape=(jax.ShapeDtypeStruct((T, LANES), jnp.int32), jax.ShapeDtypeStruct((1, LANES), jnp.int32)),
        scratch_shapes=[pltpu.VMEM((1, LANES), _F32)],
        compiler_params=pltpu.CompilerParams(dimension_semantics=("arbitrary",)),
        name="moe_plan",
    )(re_all)


IDX_CHUNK = 1024
ISSUE_UNROLL = 8


def _scatter_kernel(zero_fill, zstart_ref, slots_hbm, h2_ref, *rest):
    if zero_fill:
        xs_hbm, idx_smem, zbuf, sem_idx, sem_rows = rest
    else:
        _, xs_hbm, idx_smem, zbuf, sem_idx, sem_rows = rest
    t = pl.program_id(0)
    TS = h2_ref.shape[0]

    if zero_fill:
        @pl.when(t == 0)
        def _():
            zbuf[...] = jnp.zeros(zbuf.shape, zbuf.dtype)
            for e in range(N_EXPERTS):
                z0 = pl.multiple_of(zstart_ref[e], SUBLANES)
                pltpu.make_async_copy(zbuf, xs_hbm.at[pl.ds(z0, MOE_BLOCK + SUBLANES)], sem_rows).start()
            for e in range(N_EXPERTS):
                z0 = pl.multiple_of(zstart_ref[e], SUBLANES)
                pltpu.make_async_copy(zbuf, xs_hbm.at[pl.ds(z0, MOE_BLOCK + SUBLANES)], sem_rows).wait()

    idx_cp = pltpu.make_async_copy(slots_hbm.at[pl.ds(t * IDX_CHUNK, IDX_CHUNK)], idx_smem, sem_idx)
    idx_cp.start()
    idx_cp.wait()

    def issue(r, carry):
        for kk in range(2):
            pltpu.make_async_copy(h2_ref.at[r], xs_hbm.at[idx_smem[2 * r + kk]], sem_rows).start(priority=kk)
        return carry

    lax.fori_loop(0, TS, issue, 0, unroll=ISSUE_UNROLL)
    for kk in range(2):
        pltpu.make_async_copy(h2_ref, xs_hbm.at[pl.ds(0, TS)], sem_rows).wait()


def _scatter(h2, slots, zstart, xs, n_rows, tile):
    T = h2.shape[0]
    assert T % tile == 0 and 2 * tile <= IDX_CHUNK and slots.shape[0] == (T // tile) * IDX_CHUNK
    zero_fill = xs is None
    any_spec = pl.BlockSpec(memory_space=pl.ANY)
    in_specs = [any_spec, pl.BlockSpec((tile, D_MODEL), lambda t, z: (t, 0))]
    args = [slots, h2]
    aliases = {}
    if not zero_fill:
        in_specs.append(any_spec)
        args.append(xs)
        aliases = {3: 0}
    grid_spec = pltpu.PrefetchScalarGridSpec(
        num_scalar_prefetch=1,
        grid=(T // tile,),
        in_specs=in_specs,
        out_specs=any_spec,
        scratch_shapes=[pltpu.SMEM((IDX_CHUNK,), jnp.int32), pltpu.VMEM((MOE_BLOCK + SUBLANES, D_MODEL), _F32),
                        pltpu.SemaphoreType.DMA, pltpu.SemaphoreType.DMA],
    )
    return pl.pallas_call(
        functools.partial(_scatter_kernel, zero_fill),
        grid_spec=grid_spec,
        out_shape=jax.ShapeDtypeStruct((n_rows, D_MODEL), _F32),
        input_output_aliases=aliases,
        compiler_params=pltpu.CompilerParams(dimension_semantics=("arbitrary",), has_side_effects=True),
        name="moe_scatter_first" if zero_fill else "moe_scatter_more",
    )(zstart, *args)


def _ffn_kernel(bexp_ref, nused_ref, xs_ref, w1_ref, w3_ref, w2_ref, ys_ref, w13_scr, w2_scr):
    i = pl.program_id(0)

    @pl.when(i < nused_ref[0])
    def _():
        e = bexp_ref[i]
        e_prev = bexp_ref[jnp.maximum(i - 1, 0)]

        @pl.when((i == 0) | (e != e_prev))
        def _():
            w13_scr[:, 0:D_EXPERT] = w1_ref[...].astype(_BF)
            w13_scr[:, D_EXPERT:2 * D_EXPERT] = w3_ref[...].astype(_BF)
            w2_scr[...] = w2_ref[...].astype(_BF)

        ab = _dot(xs_ref[...].astype(_BF), w13_scr[...])
        hid = jax.nn.silu(ab[:, 0:D_EXPERT]) * ab[:, D_EXPERT:2 * D_EXPERT]
        ys_ref[...] = _dot(hid.astype(_BF), w2_scr[...])


def _ffn(xs, block_exp, n_used, w1, w3, w2, n_blocks):
    blk = lambda i, be, nu: (jnp.minimum(i, nu[0] - 1), 0)
    wmap = lambda i, be, nu: (be[i], 0, 0)
    grid_spec = pltpu.PrefetchScalarGridSpec(
        num_scalar_prefetch=2,
        grid=(n_blocks,),
        in_specs=[pl.BlockSpec((MOE_BLOCK, D_MODEL), blk),
                  pl.BlockSpec((None, D_MODEL, D_EXPERT), wmap),
                  pl.BlockSpec((None, D_MODEL, D_EXPERT), wmap),
                  pl.BlockSpec((None, D_EXPERT, D_MODEL), wmap)],
        out_specs=pl.BlockSpec((MOE_BLOCK, D_MODEL), blk),
        scratch_shapes=[pltpu.VMEM((D_MODEL, 2 * D_EXPERT), _BF), pltpu.VMEM((D_EXPERT, D_MODEL), _BF)],
    )
    return pl.pallas_call(
        _ffn_kernel,
        grid_spec=grid_spec,
        out_shape=jax.ShapeDtypeStruct(xs.shape, _F32),
        compiler_params=pltpu.CompilerParams(dimension_semantics=("arbitrary",), vmem_limit_bytes=VMEM_LIMIT),
        name="moe_ffn",
    )(block_exp, n_used, xs, w1, w3, w2)


def _combine_kernel(slots_hbm, x1_ref, rw_ref, g_ref, ys_hbm, y_ref, idx_smem, buf, sem_idx, sem_rows):
    t = pl.program_id(0)
    TC = x1_ref.shape[0]
    idx_cp = pltpu.make_async_copy(slots_hbm.at[pl.ds(t * IDX_CHUNK, IDX_CHUNK)], idx_smem, sem_idx)
    idx_cp.start()
    idx_cp.wait()

    def issue(r, carry):
        for kk in range(2):
            pltpu.make_async_copy(ys_hbm.at[idx_smem[2 * r + kk]], buf.at[kk, r], sem_rows).start(priority=kk)
        return carry

    lax.fori_loop(0, TC, issue, 0, unroll=ISSUE_UNROLL)
    for kk in range(2):
        pltpu.make_async_copy(ys_hbm.at[pl.ds(0, TC)], buf.at[kk], sem_rows).wait()

    w = rw_ref[...]
    x2 = x1_ref[...] + (w[:, 0:1] * buf[0] + w[:, 1:2] * buf[1])
    y_ref[...] = _rms(x2, g_ref[...])


def _combine(x1, rw, slots, ys, g_final, tile):
    T = x1.shape[0]
    assert T % tile == 0 and 2 * tile <= IDX_CHUNK and slots.shape[0] == (T // tile) * IDX_CHUNK
    any_spec = pl.BlockSpec(memory_space=pl.ANY)
    return pl.pallas_call(
        _combine_kernel,
        grid=(T // tile,),
        in_specs=[any_spec, pl.BlockSpec((tile, D_MODEL), lambda t: (t, 0)),
                  pl.BlockSpec((tile, LANES), lambda t: (t, 0)), _full((1, D_MODEL)), any_spec],
        out_specs=pl.BlockSpec((tile, D_MODEL), lambda t: (t, 0)),
        out_shape=jax.ShapeDtypeStruct((T, D_MODEL), _F32),
        scratch_shapes=[pltpu.SMEM((IDX_CHUNK,), jnp.int32), pltpu.VMEM((2, tile, D_MODEL), _F32),
                        pltpu.SemaphoreType.DMA, pltpu.SemaphoreType.DMA],
        compiler_params=pltpu.CompilerParams(dimension_semantics=("arbitrary",), vmem_limit_bytes=VMEM_LIMIT),
        name="moe_combine",
    )(slots, x1, rw, g_final.reshape(1, D_MODEL), ys)


def _tile_for(n_tokens):
    tile = IDX_CHUNK // 2
    return tile if n_tokens % tile == 0 else n_tokens


def _chunked_slots(slot, tile):
    n_tiles = slot.shape[0] // tile
    s = slot.reshape(n_tiles, 2 * tile)
    return jnp.pad(s, ((0, 0), (0, IDX_CHUNK - 2 * tile))).reshape(-1)


def kernel(x_prompt, x_sample, state_pool, state_ret, g_mix, w_in, w_pool, pool_scale, ret_gn, w_out, g_ffn, w_grp, w_exp, w1, w3, w2, g_final):
    Bp, Lp, _ = x_prompt.shape
    Bs = x_sample.shape[0]
    Tp = Bp * Lp
    wts = _prep_weights(g_mix[0], w_in[0], w_pool[0], pool_scale[0], ret_gn[0], w_out[0], g_ffn[0], w_grp[0], w_exp[0])

    x1p, h2p, rep, rwp, npool_p, nret_p = _mixer_prompt(x_prompt, wts)
    x1s, h2s, res, rws, npool_s, nret_s = _mixer_sample(x_sample.reshape(Bs, D_MODEL), state_pool[0], state_ret[0],
                                                        wts, PAST_LEN)

    re_all = jnp.concatenate([rep, res], axis=0)
    T_all = Tp + Bs
    rank, counts = _plan(re_all, _plan_tile(T_all))
    counts = counts[0, :N_EXPERTS]
    padded = ((counts + MOE_BLOCK - 1) // MOE_BLOCK) * MOE_BLOCK
    pad_end = jnp.cumsum(padded)
    pad_start = pad_end - padded
    e_all = re_all[:, :2]
    onehot = e_all[:, :, None] == jnp.arange(N_EXPERTS, dtype=jnp.int32)
    slot = jnp.sum(jnp.where(onehot, pad_start, 0), axis=-1) + rank[:, :2]
    n_blocks = -(-2 * T_all // MOE_BLOCK) + N_EXPERTS
    n_used = (pad_end[-1:] // MOE_BLOCK).astype(jnp.int32)
    block_lo = jnp.arange(n_blocks, dtype=jnp.int32) * MOE_BLOCK
    block_exp = jnp.minimum(jnp.sum((pad_end[None, :] <= block_lo[:, None]).astype(jnp.int32), axis=1),
                            N_EXPERTS - 1).astype(jnp.int32)
    zstart = ((pad_start + counts) // SUBLANES * SUBLANES).astype(jnp.int32)
    n_rows = (n_blocks + 2) * MOE_BLOCK

    tile_p, tile_s = _tile_for(Tp), _tile_for(Bs)
    slots_p = _chunked_slots(slot[:Tp], tile_p)
    slots_s = _chunked_slots(slot[Tp:], tile_s)
    xs = _scatter(h2p, slots_p, zstart, None, n_rows, tile_p)
    xs = _scatter(h2s, slots_s, zstart, xs, n_rows, tile_s)
    ys = _ffn(xs, block_exp, n_used, w1[0], w3[0], w2[0], n_blocks)
    y_p = _combine(x1p, rwp, slots_p, ys, g_final, tile_p)
    y_s = _combine(x1s, rws, slots_s, ys, g_final, tile_s)

    return (y_p.reshape(Bp, Lp, D_MODEL), y_s.reshape(Bs, 1, D_MODEL),
            npool_p[None], nret_p[None], npool_s[None], nret_s[None])
```

```python
import functools

import jax
import jax.numpy as jnp
from jax import lax
from jax.experimental import pallas as pl
from jax.experimental.pallas import tpu as pltpu

D_MODEL = 1024
EPS = 1e-6
POOL_GROUPS = 4
POOL_IN = D_MODEL // 2
POOL_GC = POOL_IN // POOL_GROUPS
POOL_OUT_GC = D_MODEL // POOL_GROUPS
POOL_WINDOWS = (2, 4, 8, 16)
POOL_HIST = max(POOL_WINDOWS) - 1
RET_HEADS = 4
RET_DK = D_MODEL // 8
RET_DV = D_MODEL // RET_HEADS
ROPE_BASE = 10000.0
PAST_LEN = 16384
N_GROUPS = 4
EXPERTS_PER_GROUP = 8
N_EXPERTS = N_GROUPS * EXPERTS_PER_GROUP
D_EXPERT = D_MODEL // 4
QK_W = RET_HEADS * RET_DK
V_W = RET_HEADS * RET_DV
OFF_U = 0
OFF_Q = POOL_IN
OFF_K = OFF_Q + QK_W
OFF_V = OFF_K + QK_W
OFF_GA = OFF_V + V_W
OFF_GB = OFF_GA + D_MODEL
IN_WIDTH = OFF_GB + D_MODEL

LANES = 128
SUBLANES = 8
ROW_TILE = (SUBLANES, LANES)
assert SUBLANES * LANES == D_MODEL
HALO = 16
RET_CHUNK = 256
MOE_BLOCK = 256
ROUTER_W = LANES
VMEM_LIMIT = 56 * 1024 * 1024

_BF = jnp.bfloat16
_F32 = jnp.float32


def _rms(x, g):
    inv = lax.rsqrt(jnp.mean(x * x, axis=-1, keepdims=True) + EPS)
    return x * inv * g


def _dot(a, b):
    return jnp.dot(a, b, preferred_element_type=_F32)


def _rotary(x, cos2, sin2):
    return x * cos2 + pltpu.roll(x, RET_DK // 2, 1) * sin2


def _route(logits):
    lane = lax.broadcasted_iota(jnp.int32, logits.shape, 1)
    neg = jnp.float32(-jnp.inf)
    big = jnp.int32(1 << 20)
    lg = jnp.where(lane < N_GROUPS, logits, neg)
    mg = jnp.max(lg, axis=-1, keepdims=True)
    g_idx = jnp.min(jnp.where(lg == mg, lane, big), axis=-1, keepdims=True)
    p_g = 1.0 / jnp.sum(jnp.exp(lg - mg), axis=-1, keepdims=True)
    lo = N_GROUPS + g_idx * EXPERTS_PER_GROUP
    in_grp = (lane >= lo) & (lane < lo + EXPERTS_PER_GROUP)
    le = jnp.where(in_grp, logits, neg)
    m1 = jnp.max(le, axis=-1, keepdims=True)
    i1 = jnp.min(jnp.where(le == m1, lane, big), axis=-1, keepdims=True)
    le2 = jnp.where(lane == i1, neg, le)
    m2 = jnp.max(le2, axis=-1, keepdims=True)
    i2 = jnp.min(jnp.where(le2 == m2, lane, big), axis=-1, keepdims=True)
    t = jnp.exp(m2 - m1)
    den = 1.0 + t
    return i1 - N_GROUPS, i2 - N_GROUPS, p_g * (1.0 / den), p_g * (t / den)


def _post_mix(x, mixed_ref, w_out_ref, g_ffn_ref, w_rt_ref, x1_ref, h2_ref, re_ref, rw_ref):
    x1 = x + _dot(mixed_ref[...], w_out_ref[...])
    x1_ref[...] = x1
    h2 = _rms(x1, g_ffn_ref[...])
    h2_ref[...] = h2.reshape(h2_ref.shape)
    e0, e1, w0, w1 = _route(_dot(h2.astype(_BF), w_rt_ref[...]))
    lane = lax.broadcasted_iota(jnp.int32, re_ref.shape, 1)
    re_ref[...] = jnp.where(lane == 0, e0, jnp.where(lane == 1, e1, 0))
    rw_ref[...] = jnp.where(lane == 0, w0, jnp.where(lane == 1, w1, 0.0))


def _group_norm(o, gain):
    mu = jnp.mean(o, axis=-1, keepdims=True)
    d = o - mu
    var = jnp.mean(d * d, axis=-1, keepdims=True)
    return d * lax.rsqrt(var + EPS) * gain


def _mixer_prompt_kernel(x_ref, g_mix_ref, w_in_ref, w_pool_ref, pscale_ref, gn_ref, w_out_ref, g_ffn_ref,
                         w_rt_ref, cos_ref, sin_ref, dmask_ref, xi_ref, zeta_ref, gc_ref,
                         x1_ref, h2_ref, re_ref, rw_ref, npool_ref, nret_ref,
                         u_scr, s_scr, mixed_scr):
    c = pl.program_id(1)
    n_c = pl.num_programs(1)
    C = x_ref.shape[0]

    @pl.when(c == 0)
    def _():
        u_scr[0:HALO, :] = jnp.zeros((HALO, POOL_IN), _F32)
        s_scr[...] = jnp.zeros(s_scr.shape, _F32)

    x = x_ref[...]
    h = _rms(x, g_mix_ref[...]).astype(_BF)

    u_scr[HALO:HALO + C, :] = _dot(h, w_in_ref[:, OFF_U:OFF_U + POOL_IN])
    q = _dot(h, w_in_ref[:, OFF_Q:OFF_Q + QK_W])
    k = _dot(h, w_in_ref[:, OFF_K:OFF_K + QK_W])
    cos2 = cos_ref[...]
    sin2 = sin_ref[...]
    pos1 = (c * C + 1 + lax.broadcasted_iota(jnp.int32, (C, POOL_GC), 0)).astype(_F32)

    for j in range(RET_HEADS):
        win = POOL_WINDOWS[j]
        cs = slice(j * POOL_GC, (j + 1) * POOL_GC)
        u_j = u_scr[HALO:HALO + C, cs]
        s = u_j
        for d in range(1, win):
            s = s + u_scr[HALO - d:HALO - d + C, cs]
        pooled = s / jnp.minimum(pos1, jnp.float32(win)) - u_j
        osl = slice(j * POOL_OUT_GC, (j + 1) * POOL_OUT_GC)
        pool_out = _dot(pooled.astype(_BF), w_pool_ref[j]) * pscale_ref[:, osl]

        qs = slice(j * RET_DK, (j + 1) * RET_DK)
        qb = _rotary(q[:, qs], cos2, sin2).astype(_BF)
        kf = _rotary(k[:, qs], cos2, sin2) * (RET_DK ** -0.5)
        kb = kf.astype(_BF)
        vb = _dot(h, w_in_ref[:, OFF_V + j * RET_DV:OFF_V + (j + 1) * RET_DV]).astype(_BF)
        scores = lax.dot_general(qb, kb, (((1,), (1,)), ((), ())), preferred_element_type=_F32) * dmask_ref[j]
        s_old = s_scr[j]
        o = _dot(scores.astype(_BF), vb) + _dot(qb, s_old.astype(_BF)) * xi_ref[j]
        kz = (kf * zeta_ref[j]).astype(_BF)
        s_scr[j] = gc_ref[j] * s_old + lax.dot_general(kz, vb, (((0,), (0,)), ((), ())),
                                                       preferred_element_type=_F32)
        ret_out = _group_norm(o, gn_ref[:, osl])

        ga = _dot(h, w_in_ref[:, OFF_GA + j * RET_DV:OFF_GA + (j + 1) * RET_DV])
        gb = _dot(h, w_in_ref[:, OFF_GB + j * RET_DV:OFF_GB + (j + 1) * RET_DV])
        mixed_scr[:, osl] = (jax.nn.sigmoid(ga) * pool_out + jax.nn.sigmoid(gb) * ret_out).astype(_BF)

    _post_mix(x, mixed_scr, w_out_ref, g_ffn_ref, w_rt_ref, x1_ref, h2_ref, re_ref, rw_ref)

    u_scr[0:HALO, :] = u_scr[C:C + HALO, :]

    @pl.when(c == n_c - 1)
    def _():
        npool_ref[...] = u_scr[HALO + C - POOL_HIST:HALO + C, :]
        nret_ref[...] = s_scr[...]


def _decay_tables(C):
    log_g = jnp.log(1.0 - jnp.exp2(-5.0 - jnp.arange(RET_HEADS, dtype=_F32)))
    i = jnp.arange(C, dtype=_F32)
    diff = i[:, None] - i[None, :]
    dmask = jnp.where(diff >= 0, jnp.exp(jnp.maximum(diff, 0.0)[None] * log_g[:, None, None]), 0.0)
    xi = jnp.exp((i[None, :] + 1.0) * log_g[:, None])
    zeta = jnp.exp((C - 1.0 - i)[None, :] * log_g[:, None])
    g_chunk = jnp.exp(C * log_g)
    return dmask, xi, zeta, g_chunk


def _rope_tables(pos):
    half = RET_DK // 2
    freqs = ROPE_BASE ** (-jnp.arange(half, dtype=_F32) / half)
    ang = pos[:, None] * freqs[None, :]
    cos, sin = jnp.cos(ang), jnp.sin(ang)
    return jnp.concatenate([cos, cos], axis=-1), jnp.concatenate([-sin, sin], axis=-1)


def _full(shape):
    n = len(shape)
    return pl.BlockSpec(shape, lambda *_: (0,) * n)


def _mixer_prompt(x, wts):
    B, L, _ = x.shape
    C = RET_CHUNK if L % RET_CHUNK == 0 else L
    n_c = L // C
    T = B * L
    dmask, xi, zeta, g_chunk = _decay_tables(C)
    xi_b = jnp.broadcast_to(xi[:, :, None], (RET_HEADS, C, RET_DV))
    zeta_b = jnp.broadcast_to(zeta[:, :, None], (RET_HEADS, C, RET_DK))
    gc_b = jnp.broadcast_to(g_chunk[:, None, None], (RET_HEADS, 1, RET_DV))
    cos2, sin2 = _rope_tables(jnp.arange(L).astype(_F32))

    tok = lambda b, c: (b * n_c + c, 0)
    in_specs = [
        pl.BlockSpec((None, C, D_MODEL), lambda b, c: (b, c, 0)),
        _full((1, D_MODEL)), _full((D_MODEL, IN_WIDTH)), _full((POOL_GROUPS, POOL_GC, POOL_OUT_GC)),
        _full((1, D_MODEL)), _full((1, D_MODEL)), _full((D_MODEL, D_MODEL)), _full((1, D_MODEL)),
        _full((D_MODEL, ROUTER_W)),
        pl.BlockSpec((C, RET_DK), lambda b, c: (c, 0)), pl.BlockSpec((C, RET_DK), lambda b, c: (c, 0)),
        _full((RET_HEADS, C, C)), _full((RET_HEADS, C, RET_DV)), _full((RET_HEADS, C, RET_DK)),
        _full((RET_HEADS, 1, RET_DV)),
    ]
    out_shape = (
        jax.ShapeDtypeStruct((T, D_MODEL), _F32),
        jax.ShapeDtypeStruct((T,) + ROW_TILE, _F32),
        jax.ShapeDtypeStruct((T, LANES), jnp.int32),
        jax.ShapeDtypeStruct((T, LANES), _F32),
        jax.ShapeDtypeStruct((B, POOL_HIST, POOL_IN), _F32),
        jax.ShapeDtypeStruct((B, RET_HEADS, RET_DK, RET_DV), _F32),
    )
    out_specs = (
        pl.BlockSpec((C, D_MODEL), tok), pl.BlockSpec((C,) + ROW_TILE, lambda b, c: (b * n_c + c, 0, 0)),
        pl.BlockSpec((C, LANES), tok), pl.BlockSpec((C, LANES), tok),
        pl.BlockSpec((None, POOL_HIST, POOL_IN), lambda b, c: (b, 0, 0)),
        pl.BlockSpec((None, RET_HEADS, RET_DK, RET_DV), lambda b, c: (b, 0, 0, 0)),
    )
    return pl.pallas_call(
        _mixer_prompt_kernel,
        grid=(B, n_c),
        in_specs=in_specs,
        out_specs=out_specs,
        out_shape=out_shape,
        scratch_shapes=[pltpu.VMEM((HALO + C, POOL_IN), _F32),
                        pltpu.VMEM((RET_HEADS, RET_DK, RET_DV), _F32),
                        pltpu.VMEM((C, D_MODEL), _BF)],
        compiler_params=pltpu.CompilerParams(dimension_semantics=("arbitrary", "arbitrary"),
                                             vmem_limit_bytes=VMEM_LIMIT),
        name="mixer_prompt",
    )(x, wts["g_mix"], wts["w_in"], wts["w_pool"], wts["pool_scale"], wts["ret_gn"], wts["w_out"],
      wts["g_ffn"], wts["w_rt"], cos2, sin2, dmask, xi_b, zeta_b, gc_b)


def _prep_weights(g_mix, w_in, w_pool, pool_scale, ret_gn, w_out, g_ffn, w_grp, w_exp):
    w_rt = jnp.concatenate([w_grp, w_exp.reshape(D_MODEL, N_EXPERTS)], axis=1)
    w_rt = jnp.pad(w_rt, ((0, 0), (0, ROUTER_W - w_rt.shape[1])))
    row = lambda v: v.reshape(1, D_MODEL)
    return dict(g_mix=row(g_mix), w_in=w_in.astype(_BF), w_pool=w_pool.astype(_BF), pool_scale=row(pool_scale),
                ret_gn=row(ret_gn), w_out=w_out.astype(_BF), g_ffn=row(g_ffn), w_rt=w_rt.astype(_BF))


SAMPLE_TB = 8


def _mixer_sample_kernel(x_ref, spool_ref, sret_ref, g_mix_ref, w_in_ref, w_pool_ref, pscale_ref, gn_ref,
                         w_out_ref, g_ffn_ref, w_rt_ref, cos_ref, sin_ref, dm_ref, xi_ref, zeta_ref, gc_ref,
                         x1_ref, h2_ref, re_ref, rw_ref, npool_ref, nret_ref,
                         u_scr, q_scr, k_scr, qt_scr, kt_scr, v_scr, ga_scr, gb_scr, pooled_scr, o_scr, mixed_scr):
    t = pl.program_id(0)
    n_t = pl.num_programs(0)
    Bs = x_ref.shape[0]
    TB = sret_ref.shape[0]

    @pl.when(t == 0)
    def _():
        h = _rms(x_ref[...], g_mix_ref[...]).astype(_BF)
        u_scr[...] = _dot(h, w_in_ref[:, OFF_U:OFF_U + POOL_IN])
        q = _dot(h, w_in_ref[:, OFF_Q:OFF_Q + QK_W])
        k = _dot(h, w_in_ref[:, OFF_K:OFF_K + QK_W])
        for j in range(RET_HEADS):
            qs = slice(j * RET_DK, (j + 1) * RET_DK)
            qf = _rotary(q[:, qs], cos_ref[...], sin_ref[...])
            kf = _rotary(k[:, qs], cos_ref[...], sin_ref[...]) * (RET_DK ** -0.5)
            q_scr[:, qs] = qf
            k_scr[:, qs] = kf
            qt_scr[j] = qf.T
            kt_scr[j] = kf.T
        v_scr[...] = _dot(h, w_in_ref[:, OFF_V:OFF_V + V_W])
        ga_scr[...] = _dot(h, w_in_ref[:, OFF_GA:OFF_GA + D_MODEL])
        gb_scr[...] = _dot(h, w_in_ref[:, OFF_GB:OFF_GB + D_MODEL])

    shift = (Bs - t * TB) % Bs
    qt = [pltpu.roll(qt_scr[j], shift, 1) for j in range(RET_HEADS)]
    kt = [pltpu.roll(kt_scr[j], shift, 1) for j in range(RET_HEADS)]

    blk = pl.ds(pl.multiple_of(t * TB, TB), TB)
    u_blk = u_scr[blk, :]
    q_blk = q_scr[blk, :]
    k_blk = k_scr[blk, :]
    v_blk = v_scr[blk, :]
    score = [jnp.sum(q_blk[:, j * RET_DK:(j + 1) * RET_DK] * k_blk[:, j * RET_DK:(j + 1) * RET_DK],
                     axis=-1, keepdims=True) * dm_ref[j] for j in range(RET_HEADS)]

    pooled_rows, o_rows = [], []
    for i in range(TB):
        u_row = u_blk[i:i + 1, :]
        groups = []
        for g, win in enumerate(POOL_WINDOWS):
            cs = slice(g * POOL_GC, (g + 1) * POOL_GC)
            hist_sum = jnp.sum(spool_ref[i, POOL_HIST - (win - 1):POOL_HIST, cs], axis=0, keepdims=True)
            groups.append((u_row[:, cs] + hist_sum) / jnp.float32(win) - u_row[:, cs])
        pooled_rows.append(jnp.concatenate(groups, axis=1))
        npool_ref[i, 0:POOL_HIST - 1, :] = spool_ref[i, 1:POOL_HIST, :]
        npool_ref[i, POOL_HIST - 1:POOL_HIST, :] = u_row

        heads = []
        for j in range(RET_HEADS):
            s_old = sret_ref[i, j]
            v_row = v_blk[i:i + 1, j * RET_DV:(j + 1) * RET_DV]
            qcol = qt[j][:, i:i + 1]
            kcol = kt[j][:, i:i + 1]
            qs_old = jnp.sum(qcol * s_old, axis=0, keepdims=True)
            heads.append(score[j][i:i + 1, :] * v_row + qs_old * xi_ref[j])
            nret_ref[i, j] = gc_ref[j] * s_old + (kcol * zeta_ref[j]) * v_row
        o_rows.append(jnp.concatenate(heads, axis=1))
    pooled_scr[blk, :] = jnp.concatenate(pooled_rows, axis=0)
    o_scr[blk, :] = jnp.concatenate(o_rows, axis=0)

    @pl.when(t == n_t - 1)
    def _():
        for j in range(RET_HEADS):
            cs = slice(j * POOL_GC, (j + 1) * POOL_GC)
            osl = slice(j * POOL_OUT_GC, (j + 1) * POOL_OUT_GC)
            pool_out = _dot(pooled_scr[:, cs].astype(_BF), w_pool_ref[j]) * pscale_ref[:, osl]
            ret_out = _group_norm(o_scr[:, osl], gn_ref[:, osl])
            mixed_scr[:, osl] = (jax.nn.sigmoid(ga_scr[:, osl]) * pool_out
                                 + jax.nn.sigmoid(gb_scr[:, osl]) * ret_out).astype(_BF)
        _post_mix(x_ref[...], mixed_scr, w_out_ref, g_ffn_ref, w_rt_ref, x1_ref, h2_ref, re_ref, rw_ref)


def _mixer_sample(x, state_pool, state_ret, wts, pos0):
    Bs = x.shape[0]
    assert Bs == LANES and Bs % SAMPLE_TB == 0
    TB = SAMPLE_TB
    dmask, xi, zeta, g_chunk = _decay_tables(1)
    dm_b = jnp.broadcast_to(dmask, (RET_HEADS, 1, 1))
    xi_b = jnp.broadcast_to(xi[:, :, None], (RET_HEADS, 1, RET_DV))
    zeta_b = jnp.broadcast_to(zeta[:, :, None], (RET_HEADS, 1, 1))
    gc_b = jnp.broadcast_to(g_chunk[:, None, None], (RET_HEADS, 1, RET_DV))
    cos2, sin2 = _rope_tables((pos0 + jnp.arange(1)).astype(_F32))

    in_specs = [
        _full((Bs, D_MODEL)),
        pl.BlockSpec((TB, POOL_HIST, POOL_IN), lambda t: (t, 0, 0)),
        pl.BlockSpec((TB, RET_HEADS, RET_DK, RET_DV), lambda t: (t, 0, 0, 0)),
        _full((1, D_MODEL)), _full((D_MODEL, IN_WIDTH)), _full((POOL_GROUPS, POOL_GC, POOL_OUT_GC)),
        _full((1, D_MODEL)), _full((1, D_MODEL)), _full((D_MODEL, D_MODEL)), _full((1, D_MODEL)),
        _full((D_MODEL, ROUTER_W)),
        _full((1, RET_DK)), _full((1, RET_DK)),
        _full((RET_HEADS, 1, 1)), _full((RET_HEADS, 1, RET_DV)), _full((RET_HEADS, 1, 1)),
        _full((RET_HEADS, 1, RET_DV)),
    ]
    out_shape = (
        jax.ShapeDtypeStruct((Bs, D_MODEL), _F32),
        jax.ShapeDtypeStruct((Bs,) + ROW_TILE, _F32),
        jax.ShapeDtypeStruct((Bs, LANES), jnp.int32),
        jax.ShapeDtypeStruct((Bs, LANES), _F32),
        jax.ShapeDtypeStruct((Bs, POOL_HIST, POOL_IN), _F32),
        jax.ShapeDtypeStruct((Bs, RET_HEADS, RET_DK, RET_DV), _F32),
    )
    out_specs = (
        _full((Bs, D_MODEL)), _full((Bs,) + ROW_TILE), _full((Bs, LANES)), _full((Bs, LANES)),
        pl.BlockSpec((TB, POOL_HIST, POOL_IN), lambda t: (t, 0, 0)),
        pl.BlockSpec((TB, RET_HEADS, RET_DK, RET_DV), lambda t: (t, 0, 0, 0)),
    )
    f32 = lambda *s: pltpu.VMEM(s, _F32)
    return pl.pallas_call(
        _mixer_sample_kernel,
        grid=(Bs // TB,),
        in_specs=in_specs,
        out_specs=out_specs,
        out_shape=out_shape,
        scratch_shapes=[f32(Bs, POOL_IN), f32(Bs, QK_W), f32(Bs, QK_W), f32(RET_HEADS, RET_DK, Bs),
                        f32(RET_HEADS, RET_DK, Bs), f32(Bs, V_W), f32(Bs, D_MODEL), f32(Bs, D_MODEL),
                        f32(Bs, POOL_IN), f32(Bs, V_W), pltpu.VMEM((Bs, D_MODEL), _BF)],
        compiler_params=pltpu.CompilerParams(dimension_semantics=("arbitrary",), vmem_limit_bytes=VMEM_LIMIT),
        name="mixer_sample",
    )(x, state_pool, state_ret, wts["g_mix"], wts["w_in"], wts["w_pool"], wts["pool_scale"], wts["ret_gn"],
      wts["w_out"], wts["g_ffn"], wts["w_rt"], cos2, sin2, dm_b, xi_b, zeta_b, gc_b)


def _plan_kernel(re_ref, rank_ref, cnt_ref, carry_scr):
    i = pl.program_id(0)
    TP = re_ref.shape[0]

    @pl.when(i == 0)
    def _():
        carry_scr[...] = jnp.zeros(carry_scr.shape, _F32)

    re = re_ref[...]
    lane = lax.broadcasted_iota(jnp.int32, re.shape, 1)
    m0 = lane == re[:, 0:1]
    m1 = lane == re[:, 1:2]
    onehot = jnp.where(m0 | m1, 1.0, 0.0)
    r_i = lax.broadcasted_iota(jnp.int32, (TP, TP), 0)
    c_i = lax.broadcasted_iota(jnp.int32, (TP, TP), 1)
    tri = jnp.where(c_i < r_i, 1.0, 0.0).astype(_BF)
    before = _dot(tri, onehot.astype(_BF)) + carry_scr[...]
    r0 = jnp.sum(jnp.where(m0, before, 0.0), axis=-1, keepdims=True)
    r1 = jnp.sum(jnp.where(m1, before, 0.0), axis=-1, keepdims=True)
    rank_ref[...] = jnp.where(lane == 0, r0, jnp.where(lane == 1, r1, 0.0)).astype(jnp.int32)
    carry_scr[...] += jnp.sum(onehot, axis=0, keepdims=True)
    cnt_ref[...] = carry_scr[...].astype(jnp.int32)


def _plan_tile(n_tokens):
    return max(t for t in range(SUBLANES, 513, SUBLANES) if n_tokens % t == 0)


def _plan(re_all, tile):
    T = re_all.shape[0]
    assert T % tile == 0
    return pl.pallas_call(
        _plan_kernel,
        grid=(T // tile,),
        in_specs=[pl.BlockSpec((tile, LANES), lambda i: (i, 0))],
        out_specs=(pl.BlockSpec((tile, LANES), lambda i: (i, 0)), _full((1, LANES))),
        out_shape=(jax.ShapeDtypeStruct((T, LANES), jnp.int32), jax.ShapeDtypeStruct((1, LANES), jnp.int32)),
        scratch_shapes=[pltpu.VMEM((1, LANES), _F32)],
        compiler_params=pltpu.CompilerParams(dimension_semantics=("arbitrary",)),
        name="moe_plan",
    )(re_all)


IDX_CHUNK = 1024
ISSUE_UNROLL = 8


def _scatter_kernel(zero_fill, zstart_ref, slots_hbm, h2_ref, *rest):
    if zero_fill:
        xs_hbm, idx_smem, zbuf, sem_idx, sem_rows = rest
    else:
        _, xs_hbm, idx_smem, zbuf, sem_idx, sem_rows = rest
    t = pl.program_id(0)
    TS = h2_ref.shape[0]

    if zero_fill:
        @pl.when(t == 0)
        def _():
            zbuf[...] = jnp.zeros(zbuf.shape, zbuf.dtype)
            for e in range(N_EXPERTS):
                pltpu.make_async_copy(zbuf, xs_hbm.at[pl.ds(zstart_ref[e], MOE_BLOCK)], sem_rows).start()
            for e in range(N_EXPERTS):
                pltpu.make_async_copy(zbuf, xs_hbm.at[pl.ds(zstart_ref[e], MOE_BLOCK)], sem_rows).wait()

    idx_cp = pltpu.make_async_copy(slots_hbm.at[pl.ds(t * IDX_CHUNK, IDX_CHUNK)], idx_smem, sem_idx)
    idx_cp.start()
    idx_cp.wait()

    def issue(r, carry):
        for kk in range(2):
            pltpu.make_async_copy(h2_ref.at[r], xs_hbm.at[idx_smem[2 * r + kk]], sem_rows).start(priority=kk)
        return carry

    lax.fori_loop(0, TS, issue, 0, unroll=ISSUE_UNROLL)
    for kk in range(2):
        pltpu.make_async_copy(h2_ref, xs_hbm.at[pl.ds(0, TS)], sem_rows).wait()


def _scatter(h2, slots, zstart, xs, n_rows, tile):
    T = h2.shape[0]
    assert T % tile == 0 and 2 * tile <= IDX_CHUNK and slots.shape[0] == (T // tile) * IDX_CHUNK
    zero_fill = xs is None
    any_spec = pl.BlockSpec(memory_space=pl.ANY)
    in_specs = [any_spec, pl.BlockSpec((tile,) + ROW_TILE, lambda t, z: (t, 0, 0))]
    args = [slots, h2]
    aliases = {}
    if not zero_fill:
        in_specs.append(any_spec)
        args.append(xs)
        aliases = {3: 0}
    grid_spec = pltpu.PrefetchScalarGridSpec(
        num_scalar_prefetch=1,
        grid=(T // tile,),
        in_specs=in_specs,
        out_specs=any_spec,
        scratch_shapes=[pltpu.SMEM((IDX_CHUNK,), jnp.int32), pltpu.VMEM((MOE_BLOCK,) + ROW_TILE, _F32),
                        pltpu.SemaphoreType.DMA, pltpu.SemaphoreType.DMA],
    )
    return pl.pallas_call(
        functools.partial(_scatter_kernel, zero_fill),
        grid_spec=grid_spec,
        out_shape=jax.ShapeDtypeStruct((n_rows,) + ROW_TILE, _F32),
        input_output_aliases=aliases,
        compiler_params=pltpu.CompilerParams(dimension_semantics=("arbitrary",), has_side_effects=True),
        name="moe_scatter_first" if zero_fill else "moe_scatter_more",
    )(zstart, *args)


def _ffn_kernel(bexp_ref, nused_ref, xs_ref, w1_ref, w3_ref, w2_ref, ys_ref, w13_scr, w2_scr):
    i = pl.program_id(0)

    @pl.when(i < nused_ref[0])
    def _():
        e = bexp_ref[i]
        e_prev = bexp_ref[jnp.maximum(i - 1, 0)]

        @pl.when((i == 0) | (e != e_prev))
        def _():
            w13_scr[:, 0:D_EXPERT] = w1_ref[...].astype(_BF)
            w13_scr[:, D_EXPERT:2 * D_EXPERT] = w3_ref[...].astype(_BF)
            w2_scr[...] = w2_ref[...].astype(_BF)

        xb = xs_ref[...].reshape(MOE_BLOCK, D_MODEL).astype(_BF)
        ab = _dot(xb, w13_scr[...])
        hid = jax.nn.silu(ab[:, 0:D_EXPERT]) * ab[:, D_EXPERT:2 * D_EXPERT]
        ys_ref[...] = _dot(hid.astype(_BF), w2_scr[...]).reshape(ys_ref.shape)


def _ffn(xs, block_exp, n_used, w1, w3, w2, n_blocks):
    blk = lambda i, be, nu: (jnp.minimum(i, nu[0] - 1), 0, 0)
    wmap = lambda i, be, nu: (be[i], 0, 0)
    grid_spec = pltpu.PrefetchScalarGridSpec(
        num_scalar_prefetch=2,
        grid=(n_blocks,),
        in_specs=[pl.BlockSpec((MOE_BLOCK,) + ROW_TILE, blk),
                  pl.BlockSpec((None, D_MODEL, D_EXPERT), wmap),
                  pl.BlockSpec((None, D_MODEL, D_EXPERT), wmap),
                  pl.BlockSpec((None, D_EXPERT, D_MODEL), wmap)],
        out_specs=pl.BlockSpec((MOE_BLOCK,) + ROW_TILE, blk),
        scratch_shapes=[pltpu.VMEM((D_MODEL, 2 * D_EXPERT), _BF), pltpu.VMEM((D_EXPERT, D_MODEL), _BF)],
    )
    return pl.pallas_call(
        _ffn_kernel,
        grid_spec=grid_spec,
        out_shape=jax.ShapeDtypeStruct(xs.shape, _F32),
        compiler_params=pltpu.CompilerParams(dimension_semantics=("arbitrary",), vmem_limit_bytes=VMEM_LIMIT),
        name="moe_ffn",
    )(block_exp, n_used, xs, w1, w3, w2)


def _combine_kernel(slots_hbm, x1_ref, rw_ref, g_ref, ys_hbm, y_ref, idx_smem, buf, sem_idx, sem_rows):
    t = pl.program_id(0)
    TC = x1_ref.shape[0]
    idx_cp = pltpu.make_async_copy(slots_hbm.at[pl.ds(t * IDX_CHUNK, IDX_CHUNK)], idx_smem, sem_idx)
    idx_cp.start()
    idx_cp.wait()

    def issue(r, carry):
        for kk in range(2):
            pltpu.make_async_copy(ys_hbm.at[idx_smem[2 * r + kk]], buf.at[kk, r], sem_rows).start(priority=kk)
        return carry

    lax.fori_loop(0, TC, issue, 0, unroll=ISSUE_UNROLL)
    for kk in range(2):
        pltpu.make_async_copy(ys_hbm.at[pl.ds(0, TC)], buf.at[kk], sem_rows).wait()

    w = rw_ref[...]
    y0 = buf[0].reshape(TC, D_MODEL)
    y1 = buf[1].reshape(TC, D_MODEL)
    x2 = x1_ref[...] + (w[:, 0:1] * y0 + w[:, 1:2] * y1)
    y_ref[...] = _rms(x2, g_ref[...])


def _combine(x1, rw, slots, ys, g_final, tile):
    T = x1.shape[0]
    assert T % tile == 0 and 2 * tile <= IDX_CHUNK and slots.shape[0] == (T // tile) * IDX_CHUNK
    any_spec = pl.BlockSpec(memory_space=pl.ANY)
    return pl.pallas_call(
        _combine_kernel,
        grid=(T // tile,),
        in_specs=[any_spec, pl.BlockSpec((tile, D_MODEL), lambda t: (t, 0)),
                  pl.BlockSpec((tile, LANES), lambda t: (t, 0)), _full((1, D_MODEL)), any_spec],
        out_specs=pl.BlockSpec((tile, D_MODEL), lambda t: (t, 0)),
        out_shape=jax.ShapeDtypeStruct((T, D_MODEL), _F32),
        scratch_shapes=[pltpu.SMEM((IDX_CHUNK,), jnp.int32), pltpu.VMEM((2, tile) + ROW_TILE, _F32),
                        pltpu.SemaphoreType.DMA, pltpu.SemaphoreType.DMA],
        compiler_params=pltpu.CompilerParams(dimension_semantics=("arbitrary",), vmem_limit_bytes=VMEM_LIMIT),
        name="moe_combine",
    )(slots, x1, rw, g_final.reshape(1, D_MODEL), ys)


def _tile_for(n_tokens):
    tile = IDX_CHUNK // 2
    return tile if n_tokens % tile == 0 else n_tokens


def _chunked_slots(slot, tile):
    n_tiles = slot.shape[0] // tile
    s = slot.reshape(n_tiles, 2 * tile)
    return jnp.pad(s, ((0, 0), (0, IDX_CHUNK - 2 * tile))).reshape(-1)


def kernel(x_prompt, x_sample, state_pool, state_ret, g_mix, w_in, w_pool, pool_scale, ret_gn, w_out, g_ffn, w_grp, w_exp, w1, w3, w2, g_final):
    Bp, Lp, _ = x_prompt.shape
    Bs = x_sample.shape[0]
    Tp = Bp * Lp
    wts = _prep_weights(g_mix[0], w_in[0], w_pool[0], pool_scale[0], ret_gn[0], w_out[0], g_ffn[0], w_grp[0], w_exp[0])

    x1p, h2p, rep, rwp, npool_p, nret_p = _mixer_prompt(x_prompt, wts)
    x1s, h2s, res, rws, npool_s, nret_s = _mixer_sample(x_sample.reshape(Bs, D_MODEL), state_pool[0], state_ret[0],
                                                        wts, PAST_LEN)

    re_all = jnp.concatenate([rep, res], axis=0)
    T_all = Tp + Bs
    rank, counts = _plan(re_all, _plan_tile(T_all))
    counts = counts[0, :N_EXPERTS]
    padded = ((counts + MOE_BLOCK - 1) // MOE_BLOCK) * MOE_BLOCK
    pad_end = jnp.cumsum(padded)
    pad_start = pad_end - padded
    e_all = re_all[:, :2]
    onehot = e_all[:, :, None] == jnp.arange(N_EXPERTS, dtype=jnp.int32)
    slot = jnp.sum(jnp.where(onehot, pad_start, 0), axis=-1) + rank[:, :2]
    n_blocks = -(-2 * T_all // MOE_BLOCK) + N_EXPERTS
    n_used = (pad_end[-1:] // MOE_BLOCK).astype(jnp.int32)
    block_lo = jnp.arange(n_blocks, dtype=jnp.int32) * MOE_BLOCK
    block_exp = jnp.minimum(jnp.sum((pad_end[None, :] <= block_lo[:, None]).astype(jnp.int32), axis=1),
                            N_EXPERTS - 1).astype(jnp.int32)
    zstart = (pad_start + counts).astype(jnp.int32)
    n_rows = (n_blocks + 1) * MOE_BLOCK

    tile_p, tile_s = _tile_for(Tp), _tile_for(Bs)
    slots_p = _chunked_slots(slot[:Tp], tile_p)
    slots_s = _chunked_slots(slot[Tp:], tile_s)
    xs = _scatter(h2p, slots_p, zstart, None, n_rows, tile_p)
    xs = _scatter(h2s, slots_s, zstart, xs, n_rows, tile_s)
    ys = _ffn(xs, block_exp, n_used, w1[0], w3[0], w2[0], n_blocks)
    y_p = _combine(x1p, rwp, slots_p, ys, g_final, tile_p)
    y_s = _combine(x1s, rws, slots_s, ys, g_final, tile_s)

    return (y_p.reshape(Bp, Lp, D_MODEL), y_s.reshape(Bs, 1, D_MODEL),
            npool_p[None], nret_p[None], npool_s[None], nret_s[None])
```

```python
import functools

import jax
import jax.numpy as jnp
from jax import lax
from jax.experimental import pallas as pl
from jax.experimental.pallas import tpu as pltpu

D_MODEL = 1024
EPS = 1e-6
POOL_GROUPS = 4
POOL_IN = D_MODEL // 2
POOL_GC = POOL_IN // POOL_GROUPS
POOL_OUT_GC = D_MODEL // POOL_GROUPS
POOL_WINDOWS = (2, 4, 8, 16)
POOL_HIST = max(POOL_WINDOWS) - 1
RET_HEADS = 4
RET_DK = D_MODEL // 8
RET_DV = D_MODEL // RET_HEADS
ROPE_BASE = 10000.0
PAST_LEN = 16384
N_GROUPS = 4
EXPERTS_PER_GROUP = 8
N_EXPERTS = N_GROUPS * EXPERTS_PER_GROUP
D_EXPERT = D_MODEL // 4
QK_W = RET_HEADS * RET_DK
V_W = RET_HEADS * RET_DV
OFF_U = 0
OFF_Q = POOL_IN
OFF_K = OFF_Q + QK_W
OFF_V = OFF_K + QK_W
OFF_GA = OFF_V + V_W
OFF_GB = OFF_GA + D_MODEL
IN_WIDTH = OFF_GB + D_MODEL

LANES = 128
SUBLANES = 8
ROW_TILE = (SUBLANES, LANES)
assert SUBLANES * LANES == D_MODEL
HALO = 16
RET_CHUNK = 256
MIXER_SEQS = 2
MOE_BLOCK = 256
ROUTER_W = LANES
VMEM_LIMIT = 56 * 1024 * 1024

_BF = jnp.bfloat16
_F32 = jnp.float32


def _rms(x, g):
    inv = lax.rsqrt(jnp.mean(x * x, axis=-1, keepdims=True) + EPS)
    return x * inv * g


def _dot(a, b):
    return jnp.dot(a, b, preferred_element_type=_F32)


def _rotary(x, cos2, sin2):
    return x * cos2 + pltpu.roll(x, RET_DK // 2, 1) * sin2


def _route(logits):
    lane = lax.broadcasted_iota(jnp.int32, logits.shape, 1).astype(_F32)
    neg = jnp.float32(-jnp.inf)
    big = jnp.float32(1 << 20)
    lg = jnp.where(lane < N_GROUPS, logits, neg)
    mg = jnp.max(lg, axis=-1, keepdims=True)
    g_idx = jnp.min(jnp.where(lg == mg, lane, big), axis=-1, keepdims=True)
    p_g = 1.0 / jnp.sum(jnp.exp(lg - mg), axis=-1, keepdims=True)
    lo = N_GROUPS + g_idx * EXPERTS_PER_GROUP
    in_grp = (lane >= lo) & (lane < lo + EXPERTS_PER_GROUP)
    le = jnp.where(in_grp, logits, neg)
    m1 = jnp.max(le, axis=-1, keepdims=True)
    i1 = jnp.min(jnp.where(le == m1, lane, big), axis=-1, keepdims=True)
    le2 = jnp.where(lane == i1, neg, le)
    m2 = jnp.max(le2, axis=-1, keepdims=True)
    i2 = jnp.min(jnp.where(le2 == m2, lane, big), axis=-1, keepdims=True)
    t = jnp.exp(m2 - m1)
    den = 1.0 + t
    e0 = (i1 - N_GROUPS).astype(jnp.int32)
    e1 = (i2 - N_GROUPS).astype(jnp.int32)
    return e0, e1, p_g * (1.0 / den), p_g * (t / den)


def _post_mix(x, mixed_ref, w_out_ref, g_ffn_ref, w_rt_ref, x1_ref, h2_ref, re_ref, rw_ref):
    x1 = x + _dot(mixed_ref[...], w_out_ref[...])
    x1_ref[...] = x1.reshape(x1_ref.shape)
    h2 = _rms(x1, g_ffn_ref[...])
    h2_ref[...] = h2.reshape(h2_ref.shape)
    e0, e1, w0, w1 = _route(_dot(h2.astype(_BF), w_rt_ref[...]))
    lane = lax.broadcasted_iota(jnp.int32, (x.shape[0], LANES), 1)
    re_ref[...] = jnp.where(lane == 0, e0, jnp.where(lane == 1, e1, 0)).reshape(re_ref.shape)
    rw_ref[...] = jnp.where(lane == 0, w0, jnp.where(lane == 1, w1, 0.0)).reshape(rw_ref.shape)


def _group_norm(o, gain):
    mu = jnp.mean(o, axis=-1, keepdims=True)
    d = o - mu
    var = jnp.mean(d * d, axis=-1, keepdims=True)
    return d * lax.rsqrt(var + EPS) * gain


def _mixer_prompt_kernel(x_ref, g_mix_ref, w_in_ref, w_pool_ref, pscale_ref, gn_ref, w_out_ref, g_ffn_ref,
                         w_rt_ref, cos_ref, sin_ref, dmask_ref, xi_ref, zeta_ref, gc_ref,
                         x1_ref, h2_ref, re_ref, rw_ref, npool_ref, nret_ref,
                         u_scr, s_scr, mixed_scr):
    c = pl.program_id(1)
    n_c = pl.num_programs(1)
    NSEQ, C, _ = x_ref.shape
    R = NSEQ * C

    @pl.when(c == 0)
    def _():
        u_scr[:, 0:HALO, :] = jnp.zeros((NSEQ, HALO, POOL_IN), _F32)
        s_scr[...] = jnp.zeros(s_scr.shape, _F32)

    x = x_ref[...].reshape(R, D_MODEL)
    h = _rms(x, g_mix_ref[...]).astype(_BF)

    u_scr[:, HALO:HALO + C, :] = _dot(h, w_in_ref[:, OFF_U:OFF_U + POOL_IN]).reshape(NSEQ, C, POOL_IN)
    q = _dot(h, w_in_ref[:, OFF_Q:OFF_Q + QK_W])
    k = _dot(h, w_in_ref[:, OFF_K:OFF_K + QK_W])
    cos2 = jnp.concatenate([cos_ref[...]] * NSEQ, axis=0)
    sin2 = jnp.concatenate([sin_ref[...]] * NSEQ, axis=0)
    pos1 = (c * C + 1 + lax.broadcasted_iota(jnp.int32, (C, POOL_GC), 0)).astype(_F32)

    for j in range(RET_HEADS):
        win = POOL_WINDOWS[j]
        cs = slice(j * POOL_GC, (j + 1) * POOL_GC)
        n_rows = jnp.minimum(pos1, jnp.float32(win))
        pooled = []
        for sq in range(NSEQ):
            u_j = u_scr[sq, HALO:HALO + C, cs]
            s = u_j
            for d in range(1, win):
                s = s + u_scr[sq, HALO - d:HALO - d + C, cs]
            pooled.append(s / n_rows - u_j)
        osl = slice(j * POOL_OUT_GC, (j + 1) * POOL_OUT_GC)
        pool_out = _dot(jnp.concatenate(pooled, axis=0).astype(_BF), w_pool_ref[j]) * pscale_ref[:, osl]

        qs = slice(j * RET_DK, (j + 1) * RET_DK)
        qb = _rotary(q[:, qs], cos2, sin2).astype(_BF)
        kf = _rotary(k[:, qs], cos2, sin2) * (RET_DK ** -0.5)
        kb = kf.astype(_BF)
        vb = _dot(h, w_in_ref[:, OFF_V + j * RET_DV:OFF_V + (j + 1) * RET_DV]).astype(_BF)
        ret = []
        for sq in range(NSEQ):
            rows = slice(sq * C, (sq + 1) * C)
            scores = lax.dot_general(qb[rows], kb[rows], (((1,), (1,)), ((), ())),
                                     preferred_element_type=_F32) * dmask_ref[j]
            s_old = s_scr[sq, j]
            o = _dot(scores.astype(_BF), vb[rows]) + _dot(qb[rows], s_old.astype(_BF)) * xi_ref[j]
            kz = (kf[rows] * zeta_ref[j]).astype(_BF)
            s_scr[sq, j] = gc_ref[j] * s_old + lax.dot_general(kz, vb[rows], (((0,), (0,)), ((), ())),
                                                               preferred_element_type=_F32)
            ret.append(_group_norm(o, gn_ref[:, osl]))
        ret_out = jnp.concatenate(ret, axis=0)

        ga = _dot(h, w_in_ref[:, OFF_GA + j * RET_DV:OFF_GA + (j + 1) * RET_DV])
        gb = _dot(h, w_in_ref[:, OFF_GB + j * RET_DV:OFF_GB + (j + 1) * RET_DV])
        mixed_scr[:, osl] = (jax.nn.sigmoid(ga) * pool_out + jax.nn.sigmoid(gb) * ret_out).astype(_BF)

    _post_mix(x, mixed_scr, w_out_ref, g_ffn_ref, w_rt_ref, x1_ref, h2_ref, re_ref, rw_ref)

    u_scr[:, 0:HALO, :] = u_scr[:, C:C + HALO, :]

    @pl.when(c == n_c - 1)
    def _():
        npool_ref[...] = u_scr[:, HALO + C - POOL_HIST:HALO + C, :]
        nret_ref[...] = s_scr[...]


def _decay_tables(C):
    log_g = jnp.log(1.0 - jnp.exp2(-5.0 - jnp.arange(RET_HEADS, dtype=_F32)))
    i = jnp.arange(C, dtype=_F32)
    diff = i[:, None] - i[None, :]
    dmask = jnp.where(diff >= 0, jnp.exp(jnp.maximum(diff, 0.0)[None] * log_g[:, None, None]), 0.0)
    xi = jnp.exp((i[None, :] + 1.0) * log_g[:, None])
    zeta = jnp.exp((C - 1.0 - i)[None, :] * log_g[:, None])
    g_chunk = jnp.exp(C * log_g)
    return dmask, xi, zeta, g_chunk


def _rope_tables(pos):
    half = RET_DK // 2
    freqs = ROPE_BASE ** (-jnp.arange(half, dtype=_F32) / half)
    ang = pos[:, None] * freqs[None, :]
    cos, sin = jnp.cos(ang), jnp.sin(ang)
    return jnp.concatenate([cos, cos], axis=-1), jnp.concatenate([-sin, sin], axis=-1)


def _full(shape):
    n = len(shape)
    return pl.BlockSpec(shape, lambda *_: (0,) * n)


def _mixer_prompt(x, wts):
    B, L, _ = x.shape
    C = RET_CHUNK if L % RET_CHUNK == 0 else L
    n_c = L // C
    T = B * L
    dmask, xi, zeta, g_chunk = _decay_tables(C)
    xi_b = jnp.broadcast_to(xi[:, :, None], (RET_HEADS, C, RET_DV))
    zeta_b = jnp.broadcast_to(zeta[:, :, None], (RET_HEADS, C, RET_DK))
    gc_b = jnp.broadcast_to(g_chunk[:, None, None], (RET_HEADS, 1, RET_DV))
    cos2, sin2 = _rope_tables(jnp.arange(L).astype(_F32))

    NSEQ = MIXER_SEQS if B % MIXER_SEQS == 0 else 1
    B2 = B // NSEQ
    T2 = T // NSEQ
    x = x.reshape(NSEQ, B2, L, D_MODEL)
    tok = lambda b, c: (0, b * n_c + c, 0)
    in_specs = [
        pl.BlockSpec((NSEQ, None, C, D_MODEL), lambda b, c: (0, b, c, 0)),
        _full((1, D_MODEL)), _full((D_MODEL, IN_WIDTH)), _full((POOL_GROUPS, POOL_GC, POOL_OUT_GC)),
        _full((1, D_MODEL)), _full((1, D_MODEL)), _full((D_MODEL, D_MODEL)), _full((1, D_MODEL)),
        _full((D_MODEL, ROUTER_W)),
        pl.BlockSpec((C, RET_DK), lambda b, c: (c, 0)), pl.BlockSpec((C, RET_DK), lambda b, c: (c, 0)),
        _full((RET_HEADS, C, C)), _full((RET_HEADS, C, RET_DV)), _full((RET_HEADS, C, RET_DK)),
        _full((RET_HEADS, 1, RET_DV)),
    ]
    out_shape = (
        jax.ShapeDtypeStruct((NSEQ, T2, D_MODEL), _F32),
        jax.ShapeDtypeStruct((NSEQ, T2) + ROW_TILE, _F32),
        jax.ShapeDtypeStruct((NSEQ, T2, LANES), jnp.int32),
        jax.ShapeDtypeStruct((NSEQ, T2, LANES), _F32),
        jax.ShapeDtypeStruct((NSEQ, B2, POOL_HIST, POOL_IN), _F32),
        jax.ShapeDtypeStruct((NSEQ, B2, RET_HEADS, RET_DK, RET_DV), _F32),
    )
    out_specs = (
        pl.BlockSpec((NSEQ, C, D_MODEL), tok),
        pl.BlockSpec((NSEQ, C) + ROW_TILE, lambda b, c: (0, b * n_c + c, 0, 0)),
        pl.BlockSpec((NSEQ, C, LANES), tok), pl.BlockSpec((NSEQ, C, LANES), tok),
        pl.BlockSpec((NSEQ, None, POOL_HIST, POOL_IN), lambda b, c: (0, b, 0, 0)),
        pl.BlockSpec((NSEQ, None, RET_HEADS, RET_DK, RET_DV), lambda b, c: (0, b, 0, 0, 0)),
    )
    x1, h2, re, rw, npool, nret = pl.pallas_call(
        _mixer_prompt_kernel,
        grid=(B2, n_c),
        in_specs=in_specs,
        out_specs=out_specs,
        out_shape=out_shape,
        scratch_shapes=[pltpu.VMEM((NSEQ, HALO + C, POOL_IN), _F32),
                        pltpu.VMEM((NSEQ, RET_HEADS, RET_DK, RET_DV), _F32),
                        pltpu.VMEM((NSEQ * C, D_MODEL), _BF)],
        compiler_params=pltpu.CompilerParams(dimension_semantics=("arbitrary", "arbitrary"),
                                             vmem_limit_bytes=VMEM_LIMIT),
        name="mixer_prompt",
    )(x, wts["g_mix"], wts["w_in"], wts["w_pool"], wts["pool_scale"], wts["ret_gn"], wts["w_out"],
      wts["g_ffn"], wts["w_rt"], cos2, sin2, dmask, xi_b, zeta_b, gc_b)
    return (x1.reshape(T, D_MODEL), h2.reshape((T,) + ROW_TILE), re.reshape(T, LANES), rw.reshape(T, LANES),
            npool.reshape(B, POOL_HIST, POOL_IN), nret.reshape(B, RET_HEADS, RET_DK, RET_DV))


def _prep_weights(g_mix, w_in, w_pool, pool_scale, ret_gn, w_out, g_ffn, w_grp, w_exp):
    w_rt = jnp.concatenate([w_grp, w_exp.reshape(D_MODEL, N_EXPERTS)], axis=1)
    w_rt = jnp.pad(w_rt, ((0, 0), (0, ROUTER_W - w_rt.shape[1])))
    row = lambda v: v.reshape(1, D_MODEL)
    return dict(g_mix=row(g_mix), w_in=w_in.astype(_BF), w_pool=w_pool.astype(_BF), pool_scale=row(pool_scale),
                ret_gn=row(ret_gn), w_out=w_out.astype(_BF), g_ffn=row(g_ffn), w_rt=w_rt.astype(_BF))


SAMPLE_TB = 8


def _mixer_sample_kernel(x_ref, spool_ref, sret_ref, g_mix_ref, w_in_ref, w_pool_ref, pscale_ref, gn_ref,
                         w_out_ref, g_ffn_ref, w_rt_ref, cos_ref, sin_ref, dm_ref, xi_ref, zeta_ref, gc_ref,
                         x1_ref, h2_ref, re_ref, rw_ref, npool_ref, nret_ref,
                         u_scr, q_scr, k_scr, qt_scr, kt_scr, v_scr, ga_scr, gb_scr, pooled_scr, o_scr, mixed_scr):
    t = pl.program_id(0)
    n_t = pl.num_programs(0)
    Bs = x_ref.shape[0]
    TB = sret_ref.shape[0]

    @pl.when(t == 0)
    def _():
        h = _rms(x_ref[...], g_mix_ref[...]).astype(_BF)
        u_scr[...] = _dot(h, w_in_ref[:, OFF_U:OFF_U + POOL_IN])
        q = _dot(h, w_in_ref[:, OFF_Q:OFF_Q + QK_W])
        k = _dot(h, w_in_ref[:, OFF_K:OFF_K + QK_W])
        for j in range(RET_HEADS):
            qs = slice(j * RET_DK, (j + 1) * RET_DK)
            qf = _rotary(q[:, qs], cos_ref[...], sin_ref[...])
            kf = _rotary(k[:, qs], cos_ref[...], sin_ref[...]) * (RET_DK ** -0.5)
            q_scr[:, qs] = qf
            k_scr[:, qs] = kf
            qt_scr[j] = qf.T
            kt_scr[j] = kf.T
        v_scr[...] = _dot(h, w_in_ref[:, OFF_V:OFF_V + V_W])
        ga_scr[...] = _dot(h, w_in_ref[:, OFF_GA:OFF_GA + D_MODEL])
        gb_scr[...] = _dot(h, w_in_ref[:, OFF_GB:OFF_GB + D_MODEL])

    shift = (Bs - t * TB) % Bs
    qt = [pltpu.roll(qt_scr[j], shift, 1) for j in range(RET_HEADS)]
    kt = [pltpu.roll(kt_scr[j], shift, 1) for j in range(RET_HEADS)]

    blk = pl.ds(pl.multiple_of(t * TB, TB), TB)
    u_blk = u_scr[blk, :]
    q_blk = q_scr[blk, :]
    k_blk = k_scr[blk, :]
    v_blk = v_scr[blk, :]
    score = [jnp.sum(q_blk[:, j * RET_DK:(j + 1) * RET_DK] * k_blk[:, j * RET_DK:(j + 1) * RET_DK],
                     axis=-1, keepdims=True) * dm_ref[j] for j in range(RET_HEADS)]

    pooled_rows, o_rows = [], []
    for i in range(TB):
        u_row = u_blk[i:i + 1, :]
        groups = []
        for g, win in enumerate(POOL_WINDOWS):
            cs = slice(g * POOL_GC, (g + 1) * POOL_GC)
            hist_sum = jnp.sum(spool_ref[i, POOL_HIST - (win - 1):POOL_HIST, cs], axis=0, keepdims=True)
            groups.append((u_row[:, cs] + hist_sum) / jnp.float32(win) - u_row[:, cs])
        pooled_rows.append(jnp.concatenate(groups, axis=1))
        npool_ref[i, 0:POOL_HIST - 1, :] = spool_ref[i, 1:POOL_HIST, :]
        npool_ref[i, POOL_HIST - 1:POOL_HIST, :] = u_row

        heads = []
        for j in range(RET_HEADS):
            s_old = sret_ref[i, j]
            v_row = v_blk[i:i + 1, j * RET_DV:(j + 1) * RET_DV]
            qcol = qt[j][:, i:i + 1]
            kcol = kt[j][:, i:i + 1]
            qs_old = jnp.sum(qcol * s_old, axis=0, keepdims=True)
            heads.append(score[j][i:i + 1, :] * v_row + qs_old * xi_ref[j])
            nret_ref[i, j] = gc_ref[j] * s_old + (kcol * zeta_ref[j]) * v_row
        o_rows.append(jnp.concatenate(heads, axis=1))
    pooled_scr[blk, :] = jnp.concatenate(pooled_rows, axis=0)
    o_scr[blk, :] = jnp.concatenate(o_rows, axis=0)

    @pl.when(t == n_t - 1)
    def _():
        for j in range(RET_HEADS):
            cs = slice(j * POOL_GC, (j + 1) * POOL_GC)
            osl = slice(j * POOL_OUT_GC, (j + 1) * POOL_OUT_GC)
            pool_out = _dot(pooled_scr[:, cs].astype(_BF), w_pool_ref[j]) * pscale_ref[:, osl]
            ret_out = _group_norm(o_scr[:, osl], gn_ref[:, osl])
            mixed_scr[:, osl] = (jax.nn.sigmoid(ga_scr[:, osl]) * pool_out
                                 + jax.nn.sigmoid(gb_scr[:, osl]) * ret_out).astype(_BF)
        _post_mix(x_ref[...], mixed_scr, w_out_ref, g_ffn_ref, w_rt_ref, x1_ref, h2_ref, re_ref, rw_ref)


def _mixer_sample(x, state_pool, state_ret, wts, pos0):
    Bs = x.shape[0]
    assert Bs == LANES and Bs % SAMPLE_TB == 0
    TB = SAMPLE_TB
    dmask, xi, zeta, g_chunk = _decay_tables(1)
    dm_b = jnp.broadcast_to(dmask, (RET_HEADS, 1, 1))
    xi_b = jnp.broadcast_to(xi[:, :, None], (RET_HEADS, 1, RET_DV))
    zeta_b = jnp.broadcast_to(zeta[:, :, None], (RET_HEADS, 1, 1))
    gc_b = jnp.broadcast_to(g_chunk[:, None, None], (RET_HEADS, 1, RET_DV))
    cos2, sin2 = _rope_tables((pos0 + jnp.arange(1)).astype(_F32))

    in_specs = [
        _full((Bs, D_MODEL)),
        pl.BlockSpec((TB, POOL_HIST, POOL_IN), lambda t: (t, 0, 0)),
        pl.BlockSpec((TB, RET_HEADS, RET_DK, RET_DV), lambda t: (t, 0, 0, 0)),
        _full((1, D_MODEL)), _full((D_MODEL, IN_WIDTH)), _full((POOL_GROUPS, POOL_GC, POOL_OUT_GC)),
        _full((1, D_MODEL)), _full((1, D_MODEL)), _full((D_MODEL, D_MODEL)), _full((1, D_MODEL)),
        _full((D_MODEL, ROUTER_W)),
        _full((1, RET_DK)), _full((1, RET_DK)),
        _full((RET_HEADS, 1, 1)), _full((RET_HEADS, 1, RET_DV)), _full((RET_HEADS, 1, 1)),
        _full((RET_HEADS, 1, RET_DV)),
    ]
    out_shape = (
        jax.ShapeDtypeStruct((Bs, D_MODEL), _F32),
        jax.ShapeDtypeStruct((Bs,) + ROW_TILE, _F32),
        jax.ShapeDtypeStruct((Bs, LANES), jnp.int32),
        jax.ShapeDtypeStruct((Bs, LANES), _F32),
        jax.ShapeDtypeStruct((Bs, POOL_HIST, POOL_IN), _F32),
        jax.ShapeDtypeStruct((Bs, RET_HEADS, RET_DK, RET_DV), _F32),
    )
    out_specs = (
        _full((Bs, D_MODEL)), _full((Bs,) + ROW_TILE), _full((Bs, LANES)), _full((Bs, LANES)),
        pl.BlockSpec((TB, POOL_HIST, POOL_IN), lambda t: (t, 0, 0)),
        pl.BlockSpec((TB, RET_HEADS, RET_DK, RET_DV), lambda t: (t, 0, 0, 0)),
    )
    f32 = lambda *s: pltpu.VMEM(s, _F32)
    return pl.pallas_call(
        _mixer_sample_kernel,
        grid=(Bs // TB,),
        in_specs=in_specs,
        out_specs=out_specs,
        out_shape=out_shape,
        scratch_shapes=[f32(Bs, POOL_IN), f32(Bs, QK_W), f32(Bs, QK_W), f32(RET_HEADS, RET_DK, Bs),
                        f32(RET_HEADS, RET_DK, Bs), f32(Bs, V_W), f32(Bs, D_MODEL), f32(Bs, D_MODEL),
                        f32(Bs, POOL_IN), f32(Bs, V_W), pltpu.VMEM((Bs, D_MODEL), _BF)],
        compiler_params=pltpu.CompilerParams(dimension_semantics=("arbitrary",), vmem_limit_bytes=VMEM_LIMIT),
        name="mixer_sample",
    )(x, state_pool, state_ret, wts["g_mix"], wts["w_in"], wts["w_pool"], wts["pool_scale"], wts["ret_gn"],
      wts["w_out"], wts["g_ffn"], wts["w_rt"], cos2, sin2, dm_b, xi_b, zeta_b, gc_b)


def _plan_kernel(re_ref, rank_ref, cnt_ref, carry_scr):
    i = pl.program_id(0)
    TP = re_ref.shape[0]

    @pl.when(i == 0)
    def _():
        carry_scr[...] = jnp.zeros(carry_scr.shape, _F32)

    re = re_ref[...]
    lane = lax.broadcasted_iota(jnp.int32, re.shape, 1)
    m0 = lane == re[:, 0:1]
    m1 = lane == re[:, 1:2]
    onehot = jnp.where(m0 | m1, 1.0, 0.0)
    r_i = lax.broadcasted_iota(jnp.int32, (TP, TP), 0)
    c_i = lax.broadcasted_iota(jnp.int32, (TP, TP), 1)
    tri = jnp.where(c_i < r_i, 1.0, 0.0).astype(_BF)
    before = _dot(tri, onehot.astype(_BF)) + carry_scr[...]
    r0 = jnp.sum(jnp.where(m0, before, 0.0), axis=-1, keepdims=True)
    r1 = jnp.sum(jnp.where(m1, before, 0.0), axis=-1, keepdims=True)
    rank_ref[...] = jnp.where(lane == 0, r0, jnp.where(lane == 1, r1, 0.0)).astype(jnp.int32)
    carry_scr[...] += jnp.sum(onehot, axis=0, keepdims=True)
    cnt_ref[...] = carry_scr[...].astype(jnp.int32)


def _plan_tile(n_tokens):
    return max(t for t in range(SUBLANES, 513, SUBLANES) if n_tokens % t == 0)


def _plan(re_all, tile):
    T = re_all.shape[0]
    assert T % tile == 0
    return pl.pallas_call(
        _plan_kernel,
        grid=(T // tile,),
        in_specs=[pl.BlockSpec((tile, LANES), lambda i: (i, 0))],
        out_specs=(pl.BlockSpec((tile, LANES), lambda i: (i, 0)), _full((1, LANES))),
        out_shape=(jax.ShapeDtypeStruct((T, LANES), jnp.int32), jax.ShapeDtypeStruct((1, LANES), jnp.int32)),
        scratch_shapes=[pltpu.VMEM((1, LANES), _F32)],
        compiler_params=pltpu.CompilerParams(dimension_semantics=("arbitrary",)),
        name="moe_plan",
    )(re_all)


IDX_CHUNK = 1024
ISSUE_UNROLL = 8


def _scatter_kernel(zero_fill, zstart_ref, slots_hbm, h2_ref, *rest):
    if zero_fill:
        xs_hbm, idx_smem, zbuf, sem_idx, sem_rows = rest
    else:
        _, xs_hbm, idx_smem, zbuf, sem_idx, sem_rows = rest
    t = pl.program_id(0)
    TS = h2_ref.shape[0]

    if zero_fill:
        @pl.when(t == 0)
        def _():
            zbuf[...] = jnp.zeros(zbuf.shape, zbuf.dtype)
            for e in range(N_EXPERTS):
                pltpu.make_async_copy(zbuf, xs_hbm.at[pl.ds(zstart_ref[e], MOE_BLOCK)], sem_rows).start()
            for e in range(N_EXPERTS):
                pltpu.make_async_copy(zbuf, xs_hbm.at[pl.ds(zstart_ref[e], MOE_BLOCK)], sem_rows).wait()

    idx_cp = pltpu.make_async_copy(slots_hbm.at[pl.ds(t * IDX_CHUNK, IDX_CHUNK)], idx_smem, sem_idx)
    idx_cp.start()
    idx_cp.wait()

    def issue(r, carry):
        for kk in range(2):
            pltpu.make_async_copy(h2_ref.at[r], xs_hbm.at[idx_smem[2 * r + kk]], sem_rows).start(priority=kk)
        return carry

    lax.fori_loop(0, TS, issue, 0, unroll=ISSUE_UNROLL)
    for kk in range(2):
        pltpu.make_async_copy(h2_ref, xs_hbm.at[pl.ds(0, TS)], sem_rows).wait()


def _scatter(h2, slots, zstart, xs, n_rows, tile):
    T = h2.shape[0]
    assert T % tile == 0 and 2 * tile <= IDX_CHUNK and slots.shape[0] == (T // tile) * IDX_CHUNK
    zero_fill = xs is None
    any_spec = pl.BlockSpec(memory_space=pl.ANY)
    in_specs = [any_spec, pl.BlockSpec((tile,) + ROW_TILE, lambda t, z: (t, 0, 0))]
    args = [slots, h2]
    aliases = {}
    if not zero_fill:
        in_specs.append(any_spec)
        args.append(xs)
        aliases = {3: 0}
    grid_spec = pltpu.PrefetchScalarGridSpec(
        num_scalar_prefetch=1,
        grid=(T // tile,),
        in_specs=in_specs,
        out_specs=any_spec,
        scratch_shapes=[pltpu.SMEM((IDX_CHUNK,), jnp.int32), pltpu.VMEM((MOE_BLOCK,) + ROW_TILE, _F32),
                        pltpu.SemaphoreType.DMA, pltpu.SemaphoreType.DMA],
    )
    return pl.pallas_call(
        functools.partial(_scatter_kernel, zero_fill),
        grid_spec=grid_spec,
        out_shape=jax.ShapeDtypeStruct((n_rows,) + ROW_TILE, _F32),
        input_output_aliases=aliases,
        compiler_params=pltpu.CompilerParams(dimension_semantics=("arbitrary",), has_side_effects=True),
        name="moe_scatter_first" if zero_fill else "moe_scatter_more",
    )(zstart, *args)


def _ffn_kernel(bexp_ref, nused_ref, xs_ref, w1_ref, w3_ref, w2_ref, ys_ref, w13_scr, w2_scr):
    i = pl.program_id(0)

    @pl.when(i < nused_ref[0])
    def _():
        e = bexp_ref[i]
        e_prev = bexp_ref[jnp.maximum(i - 1, 0)]

        @pl.when((i == 0) | (e != e_prev))
        def _():
            w13_scr[:, 0:D_EXPERT] = w1_ref[...].astype(_BF)
            w13_scr[:, D_EXPERT:2 * D_EXPERT] = w3_ref[...].astype(_BF)
            w2_scr[...] = w2_ref[...].astype(_BF)

        xb = xs_ref[...].reshape(MOE_BLOCK, D_MODEL).astype(_BF)
        ab = _dot(xb, w13_scr[...])
        hid = jax.nn.silu(ab[:, 0:D_EXPERT]) * ab[:, D_EXPERT:2 * D_EXPERT]
        ys_ref[...] = _dot(hid.astype(_BF), w2_scr[...]).reshape(ys_ref.shape)


def _ffn(xs, block_exp, n_used, w1, w3, w2, n_blocks):
    blk = lambda i, be, nu: (jnp.minimum(i, nu[0] - 1), 0, 0)
    wmap = lambda i, be, nu: (be[i], 0, 0)
    grid_spec = pltpu.PrefetchScalarGridSpec(
        num_scalar_prefetch=2,
        grid=(n_blocks,),
        in_specs=[pl.BlockSpec((MOE_BLOCK,) + ROW_TILE, blk),
                  pl.BlockSpec((None, D_MODEL, D_EXPERT), wmap),
                  pl.BlockSpec((None, D_MODEL, D_EXPERT), wmap),
                  pl.BlockSpec((None, D_EXPERT, D_MODEL), wmap)],
        out_specs=pl.BlockSpec((MOE_BLOCK,) + ROW_TILE, blk),
        scratch_shapes=[pltpu.VMEM((D_MODEL, 2 * D_EXPERT), _BF), pltpu.VMEM((D_EXPERT, D_MODEL), _BF)],
    )
    return pl.pallas_call(
        _ffn_kernel,
        grid_spec=grid_spec,
        out_shape=jax.ShapeDtypeStruct(xs.shape, _F32),
        compiler_params=pltpu.CompilerParams(dimension_semantics=("arbitrary",), vmem_limit_bytes=VMEM_LIMIT),
        name="moe_ffn",
    )(block_exp, n_used, xs, w1, w3, w2)


def _combine_kernel(slots_hbm, x1_ref, rw_ref, g_ref, ys_hbm, y_ref, idx_smem, buf, sem_idx, sem_rows):
    t = pl.program_id(0)
    n_t = pl.num_programs(0)
    TC = x1_ref.shape[0]
    cur = t % 2

    def gather_tile(step, b):
        idx_cp = pltpu.make_async_copy(slots_hbm.at[pl.ds(step * IDX_CHUNK, IDX_CHUNK)], idx_smem.at[b], sem_idx)
        idx_cp.start()
        idx_cp.wait()

        def issue(r, carry):
            for kk in range(2):
                pltpu.make_async_copy(ys_hbm.at[idx_smem[b, 2 * r + kk]], buf.at[b, kk, r],
                                      sem_rows.at[b]).start(priority=kk)
            return carry

        lax.fori_loop(0, TC, issue, 0, unroll=ISSUE_UNROLL)

    @pl.when(t == 0)
    def _():
        gather_tile(t, cur)

    @pl.when(t + 1 < n_t)
    def _():
        gather_tile(t + 1, 1 - cur)

    for kk in range(2):
        pltpu.make_async_copy(ys_hbm.at[pl.ds(0, TC)], buf.at[cur, kk], sem_rows.at[cur]).wait()

    w = rw_ref[...]
    y0 = buf[cur, 0].reshape(TC, D_MODEL)
    y1 = buf[cur, 1].reshape(TC, D_MODEL)
    x2 = x1_ref[...] + (w[:, 0:1] * y0 + w[:, 1:2] * y1)
    y_ref[...] = _rms(x2, g_ref[...])


def _combine(x1, rw, slots, ys, g_final, tile):
    T = x1.shape[0]
    assert T % tile == 0 and 2 * tile <= IDX_CHUNK and slots.shape[0] == (T // tile) * IDX_CHUNK
    any_spec = pl.BlockSpec(memory_space=pl.ANY)
    return pl.pallas_call(
        _combine_kernel,
        grid=(T // tile,),
        in_specs=[any_spec, pl.BlockSpec((tile, D_MODEL), lambda t: (t, 0)),
                  pl.BlockSpec((tile, LANES), lambda t: (t, 0)), _full((1, D_MODEL)), any_spec],
        out_specs=pl.BlockSpec((tile, D_MODEL), lambda t: (t, 0)),
        out_shape=jax.ShapeDtypeStruct((T, D_MODEL), _F32),
        scratch_shapes=[pltpu.SMEM((2, IDX_CHUNK), jnp.int32), pltpu.VMEM((2, 2, tile) + ROW_TILE, _F32),
                        pltpu.SemaphoreType.DMA, pltpu.SemaphoreType.DMA((2,))],
        compiler_params=pltpu.CompilerParams(dimension_semantics=("arbitrary",), vmem_limit_bytes=VMEM_LIMIT),
        name="moe_combine",
    )(slots, x1, rw, g_final.reshape(1, D_MODEL), ys)


def _tile_for(n_tokens):
    tile = IDX_CHUNK // 2
    return tile if n_tokens % tile == 0 else n_tokens


def _chunked_slots(slot, tile):
    n_tiles = slot.shape[0] // tile
    s = slot.reshape(n_tiles, 2 * tile)
    return jnp.pad(s, ((0, 0), (0, IDX_CHUNK - 2 * tile))).reshape(-1)


def kernel(x_prompt, x_sample, state_pool, state_ret, g_mix, w_in, w_pool, pool_scale, ret_gn, w_out, g_ffn, w_grp, w_exp, w1, w3, w2, g_final):
    Bp, Lp, _ = x_prompt.shape
    Bs = x_sample.shape[0]
    Tp = Bp * Lp
    wts = _prep_weights(g_mix[0], w_in[0], w_pool[0], pool_scale[0], ret_gn[0], w_out[0], g_ffn[0], w_grp[0], w_exp[0])

    x1p, h2p, rep, rwp, npool_p, nret_p = _mixer_prompt(x_prompt, wts)
    x1s, h2s, res, rws, npool_s, nret_s = _mixer_sample(x_sample.reshape(Bs, D_MODEL), state_pool[0], state_ret[0],
                                                        wts, PAST_LEN)

    re_all = jnp.concatenate([rep, res], axis=0)
    T_all = Tp + Bs
    rank, counts = _plan(re_all, _plan_tile(T_all))
    counts = counts[0, :N_EXPERTS]
    padded = ((counts + MOE_BLOCK - 1) // MOE_BLOCK) * MOE_BLOCK
    pad_end = jnp.cumsum(padded)
    pad_start = pad_end - padded
    e_all = re_all[:, :2]
    onehot = e_all[:, :, None] == jnp.arange(N_EXPERTS, dtype=jnp.int32)
    slot = jnp.sum(jnp.where(onehot, pad_start, 0), axis=-1) + rank[:, :2]
    n_blocks = -(-2 * T_all // MOE_BLOCK) + N_EXPERTS
    n_used = (pad_end[-1:] // MOE_BLOCK).astype(jnp.int32)
    block_lo = jnp.arange(n_blocks, dtype=jnp.int32) * MOE_BLOCK
    block_exp = jnp.minimum(jnp.sum((pad_end[None, :] <= block_lo[:, None]).astype(jnp.int32), axis=1),
                            N_EXPERTS - 1).astype(jnp.int32)
    zstart = (pad_start + counts).astype(jnp.int32)
    n_rows = (n_blocks + 1) * MOE_BLOCK

    tile_p, tile_s = _tile_for(Tp), _tile_for(Bs)
    slots_p = _chunked_slots(slot[:Tp], tile_p)
    slots_s = _chunked_slots(slot[Tp:], tile_s)
    xs = _scatter(h2p, slots_p, zstart, None, n_rows, tile_p)
    xs = _scatter(h2s, slots_s, zstart, xs, n_rows, tile_s)
    ys = _ffn(xs, block_exp, n_used, w1[0], w3[0], w2[0], n_blocks)
    y_p = _combine(x1p, rwp, slots_p, ys, g_final, tile_p)
    y_s = _combine(x1s, rws, slots_s, ys, g_final, tile_s)

    return (y_p.reshape(Bp, Lp, D_MODEL), y_s.reshape(Bs, 1, D_MODEL),
            npool_p[None], nret_p[None], npool_s[None], nret_s[None])
```

```python
import functools

import jax
import jax.numpy as jnp
from jax import lax
from jax.experimental import pallas as pl
from jax.experimental.pallas import tpu as pltpu

D_MODEL = 1024
EPS = 1e-6
POOL_GROUPS = 4
POOL_IN = D_MODEL // 2
POOL_GC = POOL_IN // POOL_GROUPS
POOL_OUT_GC = D_MODEL // POOL_GROUPS
POOL_WINDOWS = (2, 4, 8, 16)
POOL_HIST = max(POOL_WINDOWS) - 1
RET_HEADS = 4
RET_DK = D_MODEL // 8
RET_DV = D_MODEL // RET_HEADS
ROPE_BASE = 10000.0
PAST_LEN = 16384
N_GROUPS = 4
EXPERTS_PER_GROUP = 8
N_EXPERTS = N_GROUPS * EXPERTS_PER_GROUP
D_EXPERT = D_MODEL // 4
QK_W = RET_HEADS * RET_DK
V_W = RET_HEADS * RET_DV
OFF_U = 0
OFF_Q = POOL_IN
OFF_K = OFF_Q + QK_W
OFF_V = OFF_K + QK_W
OFF_GA = OFF_V + V_W
OFF_GB = OFF_GA + D_MODEL
IN_WIDTH = OFF_GB + D_MODEL

LANES = 128
SUBLANES = 8
ROW_TILE = (SUBLANES, LANES)
assert SUBLANES * LANES == D_MODEL
HALO = 16
RET_CHUNK = 256
MIXER_SEQS = 2
MOE_BLOCK = 256
ROUTER_W = LANES
VMEM_LIMIT = 56 * 1024 * 1024

_BF = jnp.bfloat16
_F32 = jnp.float32


def _rms(x, g):
    inv = lax.rsqrt(jnp.mean(x * x, axis=-1, keepdims=True) + EPS)
    return x * inv * g


def _dot(a, b):
    return jnp.dot(a, b, preferred_element_type=_F32)


def _rotary(x, cos2, sin2):
    return x * cos2 + pltpu.roll(x, RET_DK // 2, 1) * sin2


def _route(logits):
    lane = lax.broadcasted_iota(jnp.int32, logits.shape, 1).astype(_F32)
    neg = jnp.float32(-jnp.inf)
    big = jnp.float32(1 << 20)
    lg = jnp.where(lane < N_GROUPS, logits, neg)
    mg = jnp.max(lg, axis=-1, keepdims=True)
    g_idx = jnp.min(jnp.where(lg == mg, lane, big), axis=-1, keepdims=True)
    p_g = 1.0 / jnp.sum(jnp.exp(lg - mg), axis=-1, keepdims=True)
    lo = N_GROUPS + g_idx * EXPERTS_PER_GROUP
    in_grp = (lane >= lo) & (lane < lo + EXPERTS_PER_GROUP)
    le = jnp.where(in_grp, logits, neg)
    m1 = jnp.max(le, axis=-1, keepdims=True)
    i1 = jnp.min(jnp.where(le == m1, lane, big), axis=-1, keepdims=True)
    le2 = jnp.where(lane == i1, neg, le)
    m2 = jnp.max(le2, axis=-1, keepdims=True)
    i2 = jnp.min(jnp.where(le2 == m2, lane, big), axis=-1, keepdims=True)
    t = jnp.exp(m2 - m1)
    den = 1.0 + t
    e0 = (i1 - N_GROUPS).astype(jnp.int32)
    e1 = (i2 - N_GROUPS).astype(jnp.int32)
    return e0, e1, p_g * (1.0 / den), p_g * (t / den)


def _post_mix(x, mixed_ref, w_out_ref, g_ffn_ref, w_rt_ref, x1_ref, rw_ref):
    x1 = x + _dot(mixed_ref[...], w_out_ref[...])
    x1_ref[...] = x1.reshape(x1_ref.shape)
    h2 = _rms(x1, g_ffn_ref[...])
    e0, e1, w0, w1 = _route(_dot(h2.astype(_BF), w_rt_ref[...]))
    lane = lax.broadcasted_iota(jnp.int32, (x.shape[0], LANES), 1)
    rw_ref[...] = jnp.where(lane == 0, w0, jnp.where(lane == 1, w1, 0.0)).reshape(rw_ref.shape)
    return h2, e0, e1


def _dispatch_start(h2, e0, e1, cap, carry_scr, h2_scr, dst_v, dst_s, sem_idx, sem_rows, xs_hbm, dst_ref):
    R = h2.shape[0]
    lane = lax.broadcasted_iota(jnp.int32, (R, LANES), 1)
    m0 = lane == e0
    m1 = lane == e1
    onehot = jnp.where(m0 | m1, 1.0, 0.0)
    r_i = lax.broadcasted_iota(jnp.int32, (R, R), 0)
    c_i = lax.broadcasted_iota(jnp.int32, (R, R), 1)
    tri = jnp.where(c_i < r_i, 1.0, 0.0).astype(_BF)
    before = _dot(tri, onehot.astype(_BF)) + carry_scr[...]
    d0 = e0.astype(_F32) * cap + jnp.sum(jnp.where(m0, before, 0.0), axis=-1, keepdims=True)
    d1 = e1.astype(_F32) * cap + jnp.sum(jnp.where(m1, before, 0.0), axis=-1, keepdims=True)
    carry_scr[...] += jnp.sum(onehot, axis=0, keepdims=True)
    dst = jnp.where(lane == 0, d0, jnp.where(lane == 1, d1, 0.0))
    dst_ref[...] = dst.astype(jnp.int32).reshape(dst_ref.shape)
    dst_v[...] = dst.T[0:SUBLANES, :].astype(jnp.int32)
    h2_scr[...] = h2.reshape(h2_scr.shape)
    idx_cp = pltpu.make_async_copy(dst_v.at[pl.ds(0, 2)], dst_s, sem_idx)
    idx_cp.start()
    idx_cp.wait()

    def issue(r, carry):
        for kk in range(2):
            pltpu.make_async_copy(h2_scr.at[r], xs_hbm.at[dst_s[kk, r]], sem_rows).start(priority=kk)
        return carry

    lax.fori_loop(0, R, issue, 0, unroll=ISSUE_UNROLL)


def _dispatch_wait(h2_scr, xs_hbm, sem_rows):
    for _ in range(2):
        pltpu.make_async_copy(h2_scr, xs_hbm.at[pl.ds(0, h2_scr.shape[0])], sem_rows).wait()


def _group_norm(o, gain):
    mu = jnp.mean(o, axis=-1, keepdims=True)
    d = o - mu
    var = jnp.mean(d * d, axis=-1, keepdims=True)
    return d * lax.rsqrt(var + EPS) * gain


def _mixer_prompt_kernel(x_ref, g_mix_ref, w_in_ref, w_pool_ref, pscale_ref, gn_ref, w_out_ref, g_ffn_ref,
                         w_rt_ref, cos_ref, sin_ref, dmask_ref, xi_ref, zeta_ref, gc_ref,
                         x1_ref, dst_ref, rw_ref, npool_ref, nret_ref, cnt_ref, xs_hbm,
                         u_scr, s_scr, mixed_scr, carry_scr, h2_scr, dst_v, dst_s, sem_idx, sem_rows, *, cap):
    c = pl.program_id(1)
    n_c = pl.num_programs(1)
    first = (pl.program_id(0) == 0) & (c == 0)
    last = (pl.program_id(0) == pl.num_programs(0) - 1) & (c == n_c - 1)

    @pl.when(first)
    def _():
        carry_scr[...] = jnp.zeros(carry_scr.shape, _F32)
    NSEQ, C, _ = x_ref.shape
    R = NSEQ * C

    @pl.when(c == 0)
    def _():
        u_scr[:, 0:HALO, :] = jnp.zeros((NSEQ, HALO, POOL_IN), _F32)
        s_scr[...] = jnp.zeros(s_scr.shape, _F32)

    x = x_ref[...].reshape(R, D_MODEL)
    h = _rms(x, g_mix_ref[...]).astype(_BF)

    u_scr[:, HALO:HALO + C, :] = _dot(h, w_in_ref[:, OFF_U:OFF_U + POOL_IN]).reshape(NSEQ, C, POOL_IN)
    q = _dot(h, w_in_ref[:, OFF_Q:OFF_Q + QK_W])
    k = _dot(h, w_in_ref[:, OFF_K:OFF_K + QK_W])
    cos2 = jnp.concatenate([cos_ref[...]] * NSEQ, axis=0)
    sin2 = jnp.concatenate([sin_ref[...]] * NSEQ, axis=0)
    pos1 = (c * C + 1 + lax.broadcasted_iota(jnp.int32, (C, POOL_GC), 0)).astype(_F32)

    for j in range(RET_HEADS):
        win = POOL_WINDOWS[j]
        cs = slice(j * POOL_GC, (j + 1) * POOL_GC)
        n_rows = jnp.minimum(pos1, jnp.float32(win))
        pooled = []
        for sq in range(NSEQ):
            u_j = u_scr[sq, HALO:HALO + C, cs]
            s = u_j
            for d in range(1, win):
                s = s + u_scr[sq, HALO - d:HALO - d + C, cs]
            pooled.append(s / n_rows - u_j)
        osl = slice(j * POOL_OUT_GC, (j + 1) * POOL_OUT_GC)
        pool_out = _dot(jnp.concatenate(pooled, axis=0).astype(_BF), w_pool_ref[j]) * pscale_ref[:, osl]

        qs = slice(j * RET_DK, (j + 1) * RET_DK)
        qb = _rotary(q[:, qs], cos2, sin2).astype(_BF)
        kf = _rotary(k[:, qs], cos2, sin2) * (RET_DK ** -0.5)
        kb = kf.astype(_BF)
        vb = _dot(h, w_in_ref[:, OFF_V + j * RET_DV:OFF_V + (j + 1) * RET_DV]).astype(_BF)
        ret = []
        for sq in range(NSEQ):
            rows = slice(sq * C, (sq + 1) * C)
            scores = lax.dot_general(qb[rows], kb[rows], (((1,), (1,)), ((), ())),
                                     preferred_element_type=_F32) * dmask_ref[j]
            s_old = s_scr[sq, j]
            o = _dot(scores.astype(_BF), vb[rows]) + _dot(qb[rows], s_old.astype(_BF)) * xi_ref[j]
            kz = (kf[rows] * zeta_ref[j]).astype(_BF)
            s_scr[sq, j] = gc_ref[j] * s_old + lax.dot_general(kz, vb[rows], (((0,), (0,)), ((), ())),
                                                               preferred_element_type=_F32)
            ret.append(_group_norm(o, gn_ref[:, osl]))
        ret_out = jnp.concatenate(ret, axis=0)

        ga = _dot(h, w_in_ref[:, OFF_GA + j * RET_DV:OFF_GA + (j + 1) * RET_DV])
        gb = _dot(h, w_in_ref[:, OFF_GB + j * RET_DV:OFF_GB + (j + 1) * RET_DV])
        mixed_scr[:, osl] = (jax.nn.sigmoid(ga) * pool_out + jax.nn.sigmoid(gb) * ret_out).astype(_BF)

    h2, e0, e1 = _post_mix(x, mixed_scr, w_out_ref, g_ffn_ref, w_rt_ref, x1_ref, rw_ref)

    @pl.when(jnp.logical_not(first))
    def _():
        _dispatch_wait(h2_scr, xs_hbm, sem_rows)

    _dispatch_start(h2, e0, e1, cap, carry_scr, h2_scr, dst_v, dst_s, sem_idx, sem_rows, xs_hbm, dst_ref)
    cnt_ref[...] = carry_scr[...]

    @pl.when(last)
    def _():
        _dispatch_wait(h2_scr, xs_hbm, sem_rows)

    u_scr[:, 0:HALO, :] = u_scr[:, C:C + HALO, :]

    @pl.when(c == n_c - 1)
    def _():
        npool_ref[...] = u_scr[:, HALO + C - POOL_HIST:HALO + C, :]
        nret_ref[...] = s_scr[...]


def _decay_tables(C):
    log_g = jnp.log(1.0 - jnp.exp2(-5.0 - jnp.arange(RET_HEADS, dtype=_F32)))
    i = jnp.arange(C, dtype=_F32)
    diff = i[:, None] - i[None, :]
    dmask = jnp.where(diff >= 0, jnp.exp(jnp.maximum(diff, 0.0)[None] * log_g[:, None, None]), 0.0)
    xi = jnp.exp((i[None, :] + 1.0) * log_g[:, None])
    zeta = jnp.exp((C - 1.0 - i)[None, :] * log_g[:, None])
    g_chunk = jnp.exp(C * log_g)
    return dmask, xi, zeta, g_chunk


def _rope_tables(pos):
    half = RET_DK // 2
    freqs = ROPE_BASE ** (-jnp.arange(half, dtype=_F32) / half)
    ang = pos[:, None] * freqs[None, :]
    cos, sin = jnp.cos(ang), jnp.sin(ang)
    return jnp.concatenate([cos, cos], axis=-1), jnp.concatenate([-sin, sin], axis=-1)


def _full(shape):
    n = len(shape)
    return pl.BlockSpec(shape, lambda *_: (0,) * n)


def _mixer_prompt(x, wts, cap):
    B, L, _ = x.shape
    C = RET_CHUNK if L % RET_CHUNK == 0 else L
    n_c = L // C
    T = B * L
    dmask, xi, zeta, g_chunk = _decay_tables(C)
    xi_b = jnp.broadcast_to(xi[:, :, None], (RET_HEADS, C, RET_DV))
    zeta_b = jnp.broadcast_to(zeta[:, :, None], (RET_HEADS, C, RET_DK))
    gc_b = jnp.broadcast_to(g_chunk[:, None, None], (RET_HEADS, 1, RET_DV))
    cos2, sin2 = _rope_tables(jnp.arange(L).astype(_F32))

    NSEQ = MIXER_SEQS if B % MIXER_SEQS == 0 else 1
    B2 = B // NSEQ
    T2 = T // NSEQ
    x = x.reshape(NSEQ, B2, L, D_MODEL)
    tok = lambda b, c: (0, b * n_c + c, 0)
    in_specs = [
        pl.BlockSpec((NSEQ, None, C, D_MODEL), lambda b, c: (0, b, c, 0)),
        _full((1, D_MODEL)), _full((D_MODEL, IN_WIDTH)), _full((POOL_GROUPS, POOL_GC, POOL_OUT_GC)),
        _full((1, D_MODEL)), _full((1, D_MODEL)), _full((D_MODEL, D_MODEL)), _full((1, D_MODEL)),
        _full((D_MODEL, ROUTER_W)),
        pl.BlockSpec((C, RET_DK), lambda b, c: (c, 0)), pl.BlockSpec((C, RET_DK), lambda b, c: (c, 0)),
        _full((RET_HEADS, C, C)), _full((RET_HEADS, C, RET_DV)), _full((RET_HEADS, C, RET_DK)),
        _full((RET_HEADS, 1, RET_DV)),
    ]
    R = NSEQ * C
    out_shape = (
        jax.ShapeDtypeStruct((NSEQ, T2, D_MODEL), _F32),
        jax.ShapeDtypeStruct((NSEQ, T2, LANES), jnp.int32),
        jax.ShapeDtypeStruct((NSEQ, T2, LANES), _F32),
        jax.ShapeDtypeStruct((NSEQ, B2, POOL_HIST, POOL_IN), _F32),
        jax.ShapeDtypeStruct((NSEQ, B2, RET_HEADS, RET_DK, RET_DV), _F32),
        jax.ShapeDtypeStruct((1, LANES), _F32),
        jax.ShapeDtypeStruct((N_EXPERTS * cap,) + ROW_TILE, _F32),
    )
    out_specs = (
        pl.BlockSpec((NSEQ, C, D_MODEL), tok),
        pl.BlockSpec((NSEQ, C, LANES), tok), pl.BlockSpec((NSEQ, C, LANES), tok),
        pl.BlockSpec((NSEQ, None, POOL_HIST, POOL_IN), lambda b, c: (0, b, 0, 0)),
        pl.BlockSpec((NSEQ, None, RET_HEADS, RET_DK, RET_DV), lambda b, c: (0, b, 0, 0, 0)),
        _full((1, LANES)),
        pl.BlockSpec(memory_space=pl.ANY),
    )
    x1, dst, rw, npool, nret, counts, xs = pl.pallas_call(
        functools.partial(_mixer_prompt_kernel, cap=cap),
        grid=(B2, n_c),
        in_specs=in_specs,
        out_specs=out_specs,
        out_shape=out_shape,
        scratch_shapes=[pltpu.VMEM((NSEQ, HALO + C, POOL_IN), _F32),
                        pltpu.VMEM((NSEQ, RET_HEADS, RET_DK, RET_DV), _F32),
                        pltpu.VMEM((R, D_MODEL), _BF),
                        pltpu.VMEM((1, LANES), _F32),
                        pltpu.VMEM((R,) + ROW_TILE, _F32),
                        pltpu.VMEM((SUBLANES, R), jnp.int32),
                        pltpu.SMEM((2, R), jnp.int32),
                        pltpu.SemaphoreType.DMA, pltpu.SemaphoreType.DMA],
        compiler_params=pltpu.CompilerParams(dimension_semantics=("arbitrary", "arbitrary"),
                                             vmem_limit_bytes=VMEM_LIMIT, has_side_effects=True),
        name="mixer_prompt",
    )(x, wts["g_mix"], wts["w_in"], wts["w_pool"], wts["pool_scale"], wts["ret_gn"], wts["w_out"],
      wts["g_ffn"], wts["w_rt"], cos2, sin2, dmask, xi_b, zeta_b, gc_b)
    return (x1.reshape(T, D_MODEL), dst.reshape(T, LANES), rw.reshape(T, LANES),
            npool.reshape(B, POOL_HIST, POOL_IN), nret.reshape(B, RET_HEADS, RET_DK, RET_DV), counts, xs)


def _prep_weights(g_mix, w_in, w_pool, pool_scale, ret_gn, w_out, g_ffn, w_grp, w_exp):
    w_rt = jnp.concatenate([w_grp, w_exp.reshape(D_MODEL, N_EXPERTS)], axis=1)
    w_rt = jnp.pad(w_rt, ((0, 0), (0, ROUTER_W - w_rt.shape[1])))
    row = lambda v: v.reshape(1, D_MODEL)
    return dict(g_mix=row(g_mix), w_in=w_in.astype(_BF), w_pool=w_pool.astype(_BF), pool_scale=row(pool_scale),
                ret_gn=row(ret_gn), w_out=w_out.astype(_BF), g_ffn=row(g_ffn), w_rt=w_rt.astype(_BF))


SAMPLE_TB = 8


def _mixer_sample_kernel(x_ref, spool_ref, sret_ref, g_mix_ref, w_in_ref, w_pool_ref, pscale_ref, gn_ref,
                         w_out_ref, g_ffn_ref, w_rt_ref, cos_ref, sin_ref, dm_ref, xi_ref, zeta_ref, gc_ref,
                         carry_in_ref, xs_in_hbm,
                         x1_ref, dst_ref, rw_ref, npool_ref, nret_ref, cnt_ref, xs_hbm,
                         u_scr, q_scr, k_scr, qt_scr, kt_scr, v_scr, ga_scr, gb_scr, pooled_scr, o_scr, mixed_scr,
                         carry_scr, h2_scr, dst_v, dst_s, cnt_v, cnt_s, zbuf, sem_idx, sem_rows, sem_zero, *, cap):
    del xs_in_hbm
    t = pl.program_id(0)
    n_t = pl.num_programs(0)
    Bs = x_ref.shape[0]
    TB = sret_ref.shape[0]

    @pl.when(t == 0)
    def _():
        h = _rms(x_ref[...], g_mix_ref[...]).astype(_BF)
        u_scr[...] = _dot(h, w_in_ref[:, OFF_U:OFF_U + POOL_IN])
        q = _dot(h, w_in_ref[:, OFF_Q:OFF_Q + QK_W])
        k = _dot(h, w_in_ref[:, OFF_K:OFF_K + QK_W])
        for j in range(RET_HEADS):
            qs = slice(j * RET_DK, (j + 1) * RET_DK)
            qf = _rotary(q[:, qs], cos_ref[...], sin_ref[...])
            kf = _rotary(k[:, qs], cos_ref[...], sin_ref[...]) * (RET_DK ** -0.5)
            q_scr[:, qs] = qf
            k_scr[:, qs] = kf
            qt_scr[j] = qf.T
            kt_scr[j] = kf.T
        v_scr[...] = _dot(h, w_in_ref[:, OFF_V:OFF_V + V_W])
        ga_scr[...] = _dot(h, w_in_ref[:, OFF_GA:OFF_GA + D_MODEL])
        gb_scr[...] = _dot(h, w_in_ref[:, OFF_GB:OFF_GB + D_MODEL])

    shift = (Bs - t * TB) % Bs
    qt = [pltpu.roll(qt_scr[j], shift, 1) for j in range(RET_HEADS)]
    kt = [pltpu.roll(kt_scr[j], shift, 1) for j in range(RET_HEADS)]

    blk = pl.ds(pl.multiple_of(t * TB, TB), TB)
    u_blk = u_scr[blk, :]
    q_blk = q_scr[blk, :]
    k_blk = k_scr[blk, :]
    v_blk = v_scr[blk, :]
    score = [jnp.sum(q_blk[:, j * RET_DK:(j + 1) * RET_DK] * k_blk[:, j * RET_DK:(j + 1) * RET_DK],
                     axis=-1, keepdims=True) * dm_ref[j] for j in range(RET_HEADS)]

    pooled_rows, o_rows = [], []
    for i in range(TB):
        u_row = u_blk[i:i + 1, :]
        groups = []
        for g, win in enumerate(POOL_WINDOWS):
            cs = slice(g * POOL_GC, (g + 1) * POOL_GC)
            hist_sum = jnp.sum(spool_ref[i, POOL_HIST - (win - 1):POOL_HIST, cs], axis=0, keepdims=True)
            groups.append((u_row[:, cs] + hist_sum) / jnp.float32(win) - u_row[:, cs])
        pooled_rows.append(jnp.concatenate(groups, axis=1))
        npool_ref[i, 0:POOL_HIST - 1, :] = spool_ref[i, 1:POOL_HIST, :]
        npool_ref[i, POOL_HIST - 1:POOL_HIST, :] = u_row

        heads = []
        for j in range(RET_HEADS):
            s_old = sret_ref[i, j]
            v_row = v_blk[i:i + 1, j * RET_DV:(j + 1) * RET_DV]
            qcol = qt[j][:, i:i + 1]
            kcol = kt[j][:, i:i + 1]
            qs_old = jnp.sum(qcol * s_old, axis=0, keepdims=True)
            heads.append(score[j][i:i + 1, :] * v_row + qs_old * xi_ref[j])
            nret_ref[i, j] = gc_ref[j] * s_old + (kcol * zeta_ref[j]) * v_row
        o_rows.append(jnp.concatenate(heads, axis=1))
    pooled_scr[blk, :] = jnp.concatenate(pooled_rows, axis=0)
    o_scr[blk, :] = jnp.concatenate(o_rows, axis=0)

    @pl.when(t == n_t - 1)
    def _():
        for j in range(RET_HEADS):
            cs = slice(j * POOL_GC, (j + 1) * POOL_GC)
            osl = slice(j * POOL_OUT_GC, (j + 1) * POOL_OUT_GC)
            pool_out = _dot(pooled_scr[:, cs].astype(_BF), w_pool_ref[j]) * pscale_ref[:, osl]
            ret_out = _group_norm(o_scr[:, osl], gn_ref[:, osl])
            mixed_scr[:, osl] = (jax.nn.sigmoid(ga_scr[:, osl]) * pool_out
                                 + jax.nn.sigmoid(gb_scr[:, osl]) * ret_out).astype(_BF)
        h2, e0, e1 = _post_mix(x_ref[...], mixed_scr, w_out_ref, g_ffn_ref, w_rt_ref, x1_ref, rw_ref)
        carry_scr[...] = carry_in_ref[...]
        _dispatch_start(h2, e0, e1, cap, carry_scr, h2_scr, dst_v, dst_s, sem_idx, sem_rows, xs_hbm, dst_ref)
        cnt_ref[...] = carry_scr[...]

        zbuf[...] = jnp.zeros(zbuf.shape, _F32)
        cnt_v[...] = jnp.broadcast_to(carry_scr[...], cnt_v.shape).astype(jnp.int32)
        cnt_cp = pltpu.make_async_copy(cnt_v.at[0], cnt_s, sem_idx)
        cnt_cp.start()
        cnt_cp.wait()
        for e in range(N_EXPERTS):
            pltpu.make_async_copy(zbuf, xs_hbm.at[pl.ds(e * cap + cnt_s[e], MOE_BLOCK)], sem_zero).start()
        for e in range(N_EXPERTS):
            pltpu.make_async_copy(zbuf, xs_hbm.at[pl.ds(e * cap + cnt_s[e], MOE_BLOCK)], sem_zero).wait()
        _dispatch_wait(h2_scr, xs_hbm, sem_rows)


def _mixer_sample(x, state_pool, state_ret, wts, pos0, carry_in, xs, cap):
    Bs = x.shape[0]
    assert Bs == LANES and Bs % SAMPLE_TB == 0
    TB = SAMPLE_TB
    dmask, xi, zeta, g_chunk = _decay_tables(1)
    dm_b = jnp.broadcast_to(dmask, (RET_HEADS, 1, 1))
    xi_b = jnp.broadcast_to(xi[:, :, None], (RET_HEADS, 1, RET_DV))
    zeta_b = jnp.broadcast_to(zeta[:, :, None], (RET_HEADS, 1, 1))
    gc_b = jnp.broadcast_to(g_chunk[:, None, None], (RET_HEADS, 1, RET_DV))
    cos2, sin2 = _rope_tables((pos0 + jnp.arange(1)).astype(_F32))

    in_specs = [
        _full((Bs, D_MODEL)),
        pl.BlockSpec((TB, POOL_HIST, POOL_IN), lambda t: (t, 0, 0)),
        pl.BlockSpec((TB, RET_HEADS, RET_DK, RET_DV), lambda t: (t, 0, 0, 0)),
        _full((1, D_MODEL)), _full((D_MODEL, IN_WIDTH)), _full((POOL_GROUPS, POOL_GC, POOL_OUT_GC)),
        _full((1, D_MODEL)), _full((1, D_MODEL)), _full((D_MODEL, D_MODEL)), _full((1, D_MODEL)),
        _full((D_MODEL, ROUTER_W)),
        _full((1, RET_DK)), _full((1, RET_DK)),
        _full((RET_HEADS, 1, 1)), _full((RET_HEADS, 1, RET_DV)), _full((RET_HEADS, 1, 1)),
        _full((RET_HEADS, 1, RET_DV)),
        _full((1, LANES)), pl.BlockSpec(memory_space=pl.ANY),
    ]
    out_shape = (
        jax.ShapeDtypeStruct((Bs, D_MODEL), _F32),
        jax.ShapeDtypeStruct((Bs, LANES), jnp.int32),
        jax.ShapeDtypeStruct((Bs, LANES), _F32),
        jax.ShapeDtypeStruct((Bs, POOL_HIST, POOL_IN), _F32),
        jax.ShapeDtypeStruct((Bs, RET_HEADS, RET_DK, RET_DV), _F32),
        jax.ShapeDtypeStruct((1, LANES), _F32),
        jax.ShapeDtypeStruct(xs.shape, _F32),
    )
    out_specs = (
        _full((Bs, D_MODEL)), _full((Bs, LANES)), _full((Bs, LANES)),
        pl.BlockSpec((TB, POOL_HIST, POOL_IN), lambda t: (t, 0, 0)),
        pl.BlockSpec((TB, RET_HEADS, RET_DK, RET_DV), lambda t: (t, 0, 0, 0)),
        _full((1, LANES)), pl.BlockSpec(memory_space=pl.ANY),
    )
    f32 = lambda *s: pltpu.VMEM(s, _F32)
    n_in = len(in_specs)
    return pl.pallas_call(
        functools.partial(_mixer_sample_kernel, cap=cap),
        grid=(Bs // TB,),
        in_specs=in_specs,
        out_specs=out_specs,
        out_shape=out_shape,
        scratch_shapes=[f32(Bs, POOL_IN), f32(Bs, QK_W), f32(Bs, QK_W), f32(RET_HEADS, RET_DK, Bs),
                        f32(RET_HEADS, RET_DK, Bs), f32(Bs, V_W), f32(Bs, D_MODEL), f32(Bs, D_MODEL),
                        f32(Bs, POOL_IN), f32(Bs, V_W), pltpu.VMEM((Bs, D_MODEL), _BF),
                        f32(1, LANES), f32(Bs, *ROW_TILE), pltpu.VMEM((SUBLANES, Bs), jnp.int32),
                        pltpu.SMEM((2, Bs), jnp.int32), pltpu.VMEM((SUBLANES, LANES), jnp.int32),
                        pltpu.SMEM((LANES,), jnp.int32), f32(MOE_BLOCK, *ROW_TILE),
                        pltpu.SemaphoreType.DMA, pltpu.SemaphoreType.DMA, pltpu.SemaphoreType.DMA],
        input_output_aliases={n_in - 1: 6},
        compiler_params=pltpu.CompilerParams(dimension_semantics=("arbitrary",), vmem_limit_bytes=VMEM_LIMIT,
                                             has_side_effects=True),
        name="mixer_sample",
    )(x, state_pool, state_ret, wts["g_mix"], wts["w_in"], wts["w_pool"], wts["pool_scale"], wts["ret_gn"],
      wts["w_out"], wts["g_ffn"], wts["w_rt"], cos2, sin2, dm_b, xi_b, zeta_b, gc_b, carry_in, xs)


def _plan_kernel(re_ref, rank_ref, cnt_ref, carry_scr):
    i = pl.program_id(0)
    TP = re_ref.shape[0]

    @pl.when(i == 0)
    def _():
        carry_scr[...] = jnp.zeros(carry_scr.shape, _F32)

    re = re_ref[...]
    lane = lax.broadcasted_iota(jnp.int32, re.shape, 1)
    m0 = lane == re[:, 0:1]
    m1 = lane == re[:, 1:2]
    onehot = jnp.where(m0 | m1, 1.0, 0.0)
    r_i = lax.broadcasted_iota(jnp.int32, (TP, TP), 0)
    c_i = lax.broadcasted_iota(jnp.int32, (TP, TP), 1)
    tri = jnp.where(c_i < r_i, 1.0, 0.0).astype(_BF)
    before = _dot(tri, onehot.astype(_BF)) + carry_scr[...]
    r0 = jnp.sum(jnp.where(m0, before, 0.0), axis=-1, keepdims=True)
    r1 = jnp.sum(jnp.where(m1, before, 0.0), axis=-1, keepdims=True)
    rank_ref[...] = jnp.where(lane == 0, r0, jnp.where(lane == 1, r1, 0.0)).astype(jnp.int32)
    carry_scr[...] += jnp.sum(onehot, axis=0, keepdims=True)
    cnt_ref[...] = carry_scr[...].astype(jnp.int32)


def _plan_tile(n_tokens):
    return max(t for t in range(SUBLANES, 513, SUBLANES) if n_tokens % t == 0)


def _plan(re_all, tile):
    T = re_all.shape[0]
    assert T % tile == 0
    return pl.pallas_call(
        _plan_kernel,
        grid=(T // tile,),
        in_specs=[pl.BlockSpec((tile, LANES), lambda i: (i, 0))],
        out_specs=(pl.BlockSpec((tile, LANES), lambda i: (i, 0)), _full((1, LANES))),
        out_shape=(jax.ShapeDtypeStruct((T, LANES), jnp.int32), jax.ShapeDtypeStruct((1, LANES), jnp.int32)),
        scratch_shapes=[pltpu.VMEM((1, LANES), _F32)],
        compiler_params=pltpu.CompilerParams(dimension_semantics=("arbitrary",)),
        name="moe_plan",
    )(re_all)


IDX_CHUNK = 1024
ISSUE_UNROLL = 8


def _scatter_kernel(zero_fill, zstart_ref, slots_hbm, h2_ref, *rest):
    if zero_fill:
        xs_hbm, idx_smem, zbuf, sem_idx, sem_rows = rest
    else:
        _, xs_hbm, idx_smem, zbuf, sem_idx, sem_rows = rest
    t = pl.program_id(0)
    TS = h2_ref.shape[0]

    if zero_fill:
        @pl.when(t == 0)
        def _():
            zbuf[...] = jnp.zeros(zbuf.shape, zbuf.dtype)
            for e in range(N_EXPERTS):
                pltpu.make_async_copy(zbuf, xs_hbm.at[pl.ds(zstart_ref[e], MOE_BLOCK)], sem_rows).start()
            for e in range(N_EXPERTS):
                pltpu.make_async_copy(zbuf, xs_hbm.at[pl.ds(zstart_ref[e], MOE_BLOCK)], sem_rows).wait()

    idx_cp = pltpu.make_async_copy(slots_hbm.at[pl.ds(t * IDX_CHUNK, IDX_CHUNK)], idx_smem, sem_idx)
    idx_cp.start()
    idx_cp.wait()

    def issue(r, carry):
        for kk in range(2):
            pltpu.make_async_copy(h2_ref.at[r], xs_hbm.at[idx_smem[2 * r + kk]], sem_rows).start(priority=kk)
        return carry

    lax.fori_loop(0, TS, issue, 0, unroll=ISSUE_UNROLL)
    for kk in range(2):
        pltpu.make_async_copy(h2_ref, xs_hbm.at[pl.ds(0, TS)], sem_rows).wait()


def _scatter(h2, slots, zstart, xs, n_rows, tile):
    T = h2.shape[0]
    assert T % tile == 0 and 2 * tile <= IDX_CHUNK and slots.shape[0] == (T // tile) * IDX_CHUNK
    zero_fill = xs is None
    any_spec = pl.BlockSpec(memory_space=pl.ANY)
    in_specs = [any_spec, pl.BlockSpec((tile,) + ROW_TILE, lambda t, z: (t, 0, 0))]
    args = [slots, h2]
    aliases = {}
    if not zero_fill:
        in_specs.append(any_spec)
        args.append(xs)
        aliases = {3: 0}
    grid_spec = pltpu.PrefetchScalarGridSpec(
        num_scalar_prefetch=1,
        grid=(T // tile,),
        in_specs=in_specs,
        out_specs=any_spec,
        scratch_shapes=[pltpu.SMEM((IDX_CHUNK,), jnp.int32), pltpu.VMEM((MOE_BLOCK,) + ROW_TILE, _F32),
                        pltpu.SemaphoreType.DMA, pltpu.SemaphoreType.DMA],
    )
    return pl.pallas_call(
        functools.partial(_scatter_kernel, zero_fill),
        grid_spec=grid_spec,
        out_shape=jax.ShapeDtypeStruct((n_rows,) + ROW_TILE, _F32),
        input_output_aliases=aliases,
        compiler_params=pltpu.CompilerParams(dimension_semantics=("arbitrary",), has_side_effects=True),
        name="moe_scatter_first" if zero_fill else "moe_scatter_more",
    )(zstart, *args)


def _ffn_kernel(bexp_ref, brow_ref, nused_ref, xs_ref, w1_ref, w3_ref, w2_ref, ys_ref, w13_scr, w2_scr):
    del brow_ref
    i = pl.program_id(0)

    @pl.when(i < nused_ref[0])
    def _():
        e = bexp_ref[i]
        e_prev = bexp_ref[jnp.maximum(i - 1, 0)]

        @pl.when((i == 0) | (e != e_prev))
        def _():
            w13_scr[:, 0:D_EXPERT] = w1_ref[...].astype(_BF)
            w13_scr[:, D_EXPERT:2 * D_EXPERT] = w3_ref[...].astype(_BF)
            w2_scr[...] = w2_ref[...].astype(_BF)

        xb = xs_ref[...].reshape(MOE_BLOCK, D_MODEL).astype(_BF)
        ab = _dot(xb, w13_scr[...])
        hid = jax.nn.silu(ab[:, 0:D_EXPERT]) * ab[:, D_EXPERT:2 * D_EXPERT]
        ys_ref[...] = _dot(hid.astype(_BF), w2_scr[...]).reshape(ys_ref.shape)


def _ffn(xs, block_exp, block_row, n_used, w1, w3, w2, n_blocks):
    blk = lambda i, be, br, nu: (br[i], 0, 0)
    wmap = lambda i, be, br, nu: (be[i], 0, 0)
    grid_spec = pltpu.PrefetchScalarGridSpec(
        num_scalar_prefetch=3,
        grid=(n_blocks,),
        in_specs=[pl.BlockSpec((MOE_BLOCK,) + ROW_TILE, blk),
                  pl.BlockSpec((None, D_MODEL, D_EXPERT), wmap),
                  pl.BlockSpec((None, D_MODEL, D_EXPERT), wmap),
                  pl.BlockSpec((None, D_EXPERT, D_MODEL), wmap)],
        out_specs=pl.BlockSpec((MOE_BLOCK,) + ROW_TILE, blk),
        scratch_shapes=[pltpu.VMEM((D_MODEL, 2 * D_EXPERT), _BF), pltpu.VMEM((D_EXPERT, D_MODEL), _BF)],
    )
    return pl.pallas_call(
        _ffn_kernel,
        grid_spec=grid_spec,
        out_shape=jax.ShapeDtypeStruct(xs.shape, _F32),
        compiler_params=pltpu.CompilerParams(dimension_semantics=("arbitrary",), vmem_limit_bytes=VMEM_LIMIT),
        name="moe_ffn",
    )(block_exp, block_row, n_used, xs, w1, w3, w2)


def _combine_kernel(slots_hbm, x1_ref, rw_ref, g_ref, ys_hbm, y_ref, idx_smem, buf, sem_idx, sem_rows):
    t = pl.program_id(0)
    TC = x1_ref.shape[0]
    idx_cp = pltpu.make_async_copy(slots_hbm.at[pl.ds(t * IDX_CHUNK, IDX_CHUNK)], idx_smem, sem_idx)
    idx_cp.start()
    idx_cp.wait()

    def issue(r, carry):
        for kk in range(2):
            pltpu.make_async_copy(ys_hbm.at[idx_smem[2 * r + kk]], buf.at[kk, r], sem_rows).start(priority=kk)
        return carry

    lax.fori_loop(0, TC, issue, 0, unroll=ISSUE_UNROLL)
    for kk in range(2):
        pltpu.make_async_copy(ys_hbm.at[pl.ds(0, TC)], buf.at[kk], sem_rows).wait()

    w = rw_ref[...]
    y0 = buf[0].reshape(TC, D_MODEL)
    y1 = buf[1].reshape(TC, D_MODEL)
    x2 = x1_ref[...] + (w[:, 0:1] * y0 + w[:, 1:2] * y1)
    y_ref[...] = _rms(x2, g_ref[...])


def _combine(x1, rw, slots, ys, g_final, tile):
    T = x1.shape[0]
    assert T % tile == 0 and 2 * tile <= IDX_CHUNK and slots.shape[0] == (T // tile) * IDX_CHUNK
    any_spec = pl.BlockSpec(memory_space=pl.ANY)
    return pl.pallas_call(
        _combine_kernel,
        grid=(T // tile,),
        in_specs=[any_spec, pl.BlockSpec((tile, D_MODEL), lambda t: (t, 0)),
                  pl.BlockSpec((tile, LANES), lambda t: (t, 0)), _full((1, D_MODEL)), any_spec],
        out_specs=pl.BlockSpec((tile, D_MODEL), lambda t: (t, 0)),
        out_shape=jax.ShapeDtypeStruct((T, D_MODEL), _F32),
        scratch_shapes=[pltpu.SMEM((IDX_CHUNK,), jnp.int32), pltpu.VMEM((2, tile) + ROW_TILE, _F32),
                        pltpu.SemaphoreType.DMA, pltpu.SemaphoreType.DMA],
        compiler_params=pltpu.CompilerParams(dimension_semantics=("arbitrary",), vmem_limit_bytes=VMEM_LIMIT),
        name="moe_combine",
    )(slots, x1, rw, g_final.reshape(1, D_MODEL), ys)


def _tile_for(n_tokens):
    tile = IDX_CHUNK // 2
    return tile if n_tokens % tile == 0 else n_tokens


def _chunked_slots(slot, tile):
    n_tiles = slot.shape[0] // tile
    s = slot.reshape(n_tiles, 2 * tile)
    return jnp.pad(s, ((0, 0), (0, IDX_CHUNK - 2 * tile))).reshape(-1)


def kernel(x_prompt, x_sample, state_pool, state_ret, g_mix, w_in, w_pool, pool_scale, ret_gn, w_out, g_ffn, w_grp, w_exp, w1, w3, w2, g_final):
    Bp, Lp, _ = x_prompt.shape
    Bs = x_sample.shape[0]
    Tp = Bp * Lp
    wts = _prep_weights(g_mix[0], w_in[0], w_pool[0], pool_scale[0], ret_gn[0], w_out[0], g_ffn[0], w_grp[0], w_exp[0])

    T_all = Tp + Bs
    cap = (-(-T_all // MOE_BLOCK) + 1) * MOE_BLOCK
    cap_blocks = cap // MOE_BLOCK

    x1p, dst_p, rwp, npool_p, nret_p, counts_p, xs = _mixer_prompt(x_prompt, wts, cap)
    x1s, dst_s, rws, npool_s, nret_s, counts, xs = _mixer_sample(
        x_sample.reshape(Bs, D_MODEL), state_pool[0], state_ret[0], wts, PAST_LEN, counts_p, xs, cap)

    counts = counts[0, :N_EXPERTS].astype(jnp.int32)
    nblk = (counts + MOE_BLOCK - 1) // MOE_BLOCK
    blk_end = jnp.cumsum(nblk)
    n_used = blk_end[-1:].astype(jnp.int32)
    n_blocks = -(-2 * T_all // MOE_BLOCK) + N_EXPERTS
    step = jnp.minimum(jnp.arange(n_blocks, dtype=jnp.int32), n_used - 1)
    block_exp = jnp.minimum(jnp.sum((blk_end[None, :] <= step[:, None]).astype(jnp.int32), axis=1), N_EXPERTS - 1)
    blk_start = jnp.sum(jnp.where(block_exp[:, None] == jnp.arange(N_EXPERTS, dtype=jnp.int32), (blk_end - nblk)[None, :], 0),
                        axis=1)
    block_row = (block_exp * cap_blocks + (step - blk_start)).astype(jnp.int32)
    block_exp = block_exp.astype(jnp.int32)

    tile_p, tile_s = _tile_for(Tp), _tile_for(Bs)
    slots_p = _chunked_slots(dst_p[:, :2], tile_p)
    slots_s = _chunked_slots(dst_s[:, :2], tile_s)
    ys = _ffn(xs, block_exp, block_row, n_used, w1[0], w3[0], w2[0], n_blocks)
    y_p = _combine(x1p, rwp, slots_p, ys, g_final, tile_p)
    y_s = _combine(x1s, rws, slots_s, ys, g_final, tile_s)

    return (y_p.reshape(Bp, Lp, D_MODEL), y_s.reshape(Bs, 1, D_MODEL),
            npool_p[None], nret_p[None], npool_s[None], nret_s[None])
```

```python
import functools

import jax
import jax.numpy as jnp
from jax import lax
from jax.experimental import pallas as pl
from jax.experimental.pallas import tpu as pltpu

D_MODEL = 1024
EPS = 1e-6
POOL_GROUPS = 4
POOL_IN = D_MODEL // 2
POOL_GC = POOL_IN // POOL_GROUPS
POOL_OUT_GC = D_MODEL // POOL_GROUPS
POOL_WINDOWS = (2, 4, 8, 16)
POOL_HIST = max(POOL_WINDOWS) - 1
RET_HEADS = 4
RET_DK = D_MODEL // 8
RET_DV = D_MODEL // RET_HEADS
ROPE_BASE = 10000.0
PAST_LEN = 16384
N_GROUPS = 4
EXPERTS_PER_GROUP = 8
N_EXPERTS = N_GROUPS * EXPERTS_PER_GROUP
D_EXPERT = D_MODEL // 4
QK_W = RET_HEADS * RET_DK
V_W = RET_HEADS * RET_DV
OFF_U = 0
OFF_Q = POOL_IN
OFF_K = OFF_Q + QK_W
OFF_V = OFF_K + QK_W
OFF_GA = OFF_V + V_W
OFF_GB = OFF_GA + D_MODEL
IN_WIDTH = OFF_GB + D_MODEL

LANES = 128
SUBLANES = 8
ROW_TILE = (SUBLANES, LANES)
assert SUBLANES * LANES == D_MODEL
HALO = 16
RET_CHUNK = 256
MIXER_SEQS = 2
MOE_BLOCK = 512
ROUTER_W = LANES
VMEM_LIMIT = 56 * 1024 * 1024

_BF = jnp.bfloat16
_F32 = jnp.float32


def _rms(x, g):
    inv = lax.rsqrt(jnp.mean(x * x, axis=-1, keepdims=True) + EPS)
    return x * inv * g


def _dot(a, b):
    return jnp.dot(a, b, preferred_element_type=_F32)


def _rotary(x, cos2, sin2):
    return x * cos2 + pltpu.roll(x, RET_DK // 2, 1) * sin2


def _route(logits):
    lane = lax.broadcasted_iota(jnp.int32, logits.shape, 1).astype(_F32)
    neg = jnp.float32(-jnp.inf)
    big = jnp.float32(1 << 20)
    lg = jnp.where(lane < N_GROUPS, logits, neg)
    mg = jnp.max(lg, axis=-1, keepdims=True)
    g_idx = jnp.min(jnp.where(lg == mg, lane, big), axis=-1, keepdims=True)
    p_g = 1.0 / jnp.sum(jnp.exp(lg - mg), axis=-1, keepdims=True)
    lo = N_GROUPS + g_idx * EXPERTS_PER_GROUP
    in_grp = (lane >= lo) & (lane < lo + EXPERTS_PER_GROUP)
    le = jnp.where(in_grp, logits, neg)
    m1 = jnp.max(le, axis=-1, keepdims=True)
    i1 = jnp.min(jnp.where(le == m1, lane, big), axis=-1, keepdims=True)
    le2 = jnp.where(lane == i1, neg, le)
    m2 = jnp.max(le2, axis=-1, keepdims=True)
    i2 = jnp.min(jnp.where(le2 == m2, lane, big), axis=-1, keepdims=True)
    t = jnp.exp(m2 - m1)
    den = 1.0 + t
    e0 = (i1 - N_GROUPS).astype(jnp.int32)
    e1 = (i2 - N_GROUPS).astype(jnp.int32)
    return e0, e1, p_g * (1.0 / den), p_g * (t / den)


def _post_mix(x, mixed_ref, w_out_ref, g_ffn_ref, w_rt_ref, x1_ref, rw_ref):
    x1 = x + _dot(mixed_ref[...], w_out_ref[...])
    x1_ref[...] = x1.reshape(x1_ref.shape)
    h2 = _rms(x1, g_ffn_ref[...])
    e0, e1, w0, w1 = _route(_dot(h2.astype(_BF), w_rt_ref[...]))
    lane = lax.broadcasted_iota(jnp.int32, (x.shape[0], LANES), 1)
    rw_ref[...] = jnp.where(lane == 0, w0, jnp.where(lane == 1, w1, 0.0)).reshape(rw_ref.shape)
    return h2, e0, e1


def _dispatch_start(h2, e0, e1, cap, carry_scr, h2_scr, dst_v, dst_s, sem_idx, sem_rows, xs_hbm, dst_ref):
    R = h2.shape[0]
    lane = lax.broadcasted_iota(jnp.int32, (R, LANES), 1)
    m0 = lane == e0
    m1 = lane == e1
    onehot = jnp.where(m0 | m1, 1.0, 0.0)
    r_i = lax.broadcasted_iota(jnp.int32, (R, R), 0)
    c_i = lax.broadcasted_iota(jnp.int32, (R, R), 1)
    tri = jnp.where(c_i < r_i, 1.0, 0.0).astype(_BF)
    before = _dot(tri, onehot.astype(_BF)) + carry_scr[...]
    d0 = e0.astype(_F32) * cap + jnp.sum(jnp.where(m0, before, 0.0), axis=-1, keepdims=True)
    d1 = e1.astype(_F32) * cap + jnp.sum(jnp.where(m1, before, 0.0), axis=-1, keepdims=True)
    carry_scr[...] += jnp.sum(onehot, axis=0, keepdims=True)
    dst = jnp.where(lane == 0, d0, jnp.where(lane == 1, d1, 0.0))
    dst_ref[...] = dst.astype(jnp.int32).reshape(dst_ref.shape)
    dst_v[...] = dst.T[0:SUBLANES, :].astype(jnp.int32)
    h2_scr[...] = h2.reshape(h2_scr.shape)
    idx_cps = [pltpu.make_async_copy(dst_v.at[kk], dst_s.at[pl.ds(kk * R, R)], sem_idx) for kk in range(2)]
    for cp in idx_cps:
        cp.start()
    for cp in idx_cps:
        cp.wait()

    def issue(r, carry):
        for kk in range(2):
            pltpu.make_async_copy(h2_scr.at[r], xs_hbm.at[dst_s[kk * R + r]], sem_rows).start(priority=kk)
        return carry

    lax.fori_loop(0, R, issue, 0, unroll=ISSUE_UNROLL)


def _dispatch_wait(h2_scr, xs_hbm, sem_rows):
    for _ in range(2):
        pltpu.make_async_copy(h2_scr, xs_hbm.at[pl.ds(0, h2_scr.shape[0])], sem_rows).wait()


def _group_norm(o, gain):
    mu = jnp.mean(o, axis=-1, keepdims=True)
    d = o - mu
    var = jnp.mean(d * d, axis=-1, keepdims=True)
    return d * lax.rsqrt(var + EPS) * gain


def _mixer_prompt_kernel(x_ref, g_mix_ref, w_in_ref, w_pool_ref, pscale_ref, gn_ref, w_out_ref, g_ffn_ref,
                         w_rt_ref, cos_ref, sin_ref, dmask_ref, xi_ref, zeta_ref, gc_ref,
                         x1_ref, dst_ref, rw_ref, npool_ref, nret_ref, cnt_ref, xs_hbm,
                         u_scr, s_scr, mixed_scr, carry_scr, h2_scr, dst_v, dst_s, sem_idx, sem_rows, *, cap):
    c = pl.program_id(1)
    n_c = pl.num_programs(1)
    first = (pl.program_id(0) == 0) & (c == 0)
    last = (pl.program_id(0) == pl.num_programs(0) - 1) & (c == n_c - 1)

    @pl.when(first)
    def _():
        carry_scr[...] = jnp.zeros(carry_scr.shape, _F32)
    NSEQ, C, _ = x_ref.shape
    R = NSEQ * C

    @pl.when(c == 0)
    def _():
        u_scr[:, 0:HALO, :] = jnp.zeros((NSEQ, HALO, POOL_IN), _F32)
        s_scr[...] = jnp.zeros(s_scr.shape, _F32)

    x = x_ref[...].reshape(R, D_MODEL)
    h = _rms(x, g_mix_ref[...]).astype(_BF)

    u_scr[:, HALO:HALO + C, :] = _dot(h, w_in_ref[:, OFF_U:OFF_U + POOL_IN]).reshape(NSEQ, C, POOL_IN)
    q = _dot(h, w_in_ref[:, OFF_Q:OFF_Q + QK_W])
    k = _dot(h, w_in_ref[:, OFF_K:OFF_K + QK_W])
    cos2 = jnp.concatenate([cos_ref[...]] * NSEQ, axis=0)
    sin2 = jnp.concatenate([sin_ref[...]] * NSEQ, axis=0)
    pos1 = (c * C + 1 + lax.broadcasted_iota(jnp.int32, (C, POOL_GC), 0)).astype(_F32)

    for j in range(RET_HEADS):
        win = POOL_WINDOWS[j]
        cs = slice(j * POOL_GC, (j + 1) * POOL_GC)
        n_rows = jnp.minimum(pos1, jnp.float32(win))
        pooled = []
        for sq in range(NSEQ):
            u_j = u_scr[sq, HALO:HALO + C, cs]
            s = u_j
            for d in range(1, win):
                s = s + u_scr[sq, HALO - d:HALO - d + C, cs]
            pooled.append(s / n_rows - u_j)
        osl = slice(j * POOL_OUT_GC, (j + 1) * POOL_OUT_GC)
        pool_out = _dot(jnp.concatenate(pooled, axis=0).astype(_BF), w_pool_ref[j]) * pscale_ref[:, osl]

        qs = slice(j * RET_DK, (j + 1) * RET_DK)
        qb = _rotary(q[:, qs], cos2, sin2).astype(_BF)
        kf = _rotary(k[:, qs], cos2, sin2) * (RET_DK ** -0.5)
        kb = kf.astype(_BF)
        vb = _dot(h, w_in_ref[:, OFF_V + j * RET_DV:OFF_V + (j + 1) * RET_DV]).astype(_BF)
        ret = []
        for sq in range(NSEQ):
            rows = slice(sq * C, (sq + 1) * C)
            scores = lax.dot_general(qb[rows], kb[rows], (((1,), (1,)), ((), ())),
                                     preferred_element_type=_F32) * dmask_ref[j]
            s_old = s_scr[sq, j]
            o = _dot(scores.astype(_BF), vb[rows]) + _dot(qb[rows], s_old.astype(_BF)) * xi_ref[j]
            kz = (kf[rows] * zeta_ref[j]).astype(_BF)
            s_scr[sq, j] = gc_ref[j] * s_old + lax.dot_general(kz, vb[rows], (((0,), (0,)), ((), ())),
                                                               preferred_element_type=_F32)
            ret.append(_group_norm(o, gn_ref[:, osl]))
        ret_out = jnp.concatenate(ret, axis=0)

        ga = _dot(h, w_in_ref[:, OFF_GA + j * RET_DV:OFF_GA + (j + 1) * RET_DV])
        gb = _dot(h, w_in_ref[:, OFF_GB + j * RET_DV:OFF_GB + (j + 1) * RET_DV])
        mixed_scr[:, osl] = (jax.nn.sigmoid(ga) * pool_out + jax.nn.sigmoid(gb) * ret_out).astype(_BF)

    h2, e0, e1 = _post_mix(x, mixed_scr, w_out_ref, g_ffn_ref, w_rt_ref, x1_ref, rw_ref)

    @pl.when(jnp.logical_not(first))
    def _():
        _dispatch_wait(h2_scr, xs_hbm, sem_rows)

    _dispatch_start(h2, e0, e1, cap, carry_scr, h2_scr, dst_v, dst_s, sem_idx, sem_rows, xs_hbm, dst_ref)
    cnt_ref[...] = carry_scr[...]

    @pl.when(last)
    def _():
        _dispatch_wait(h2_scr, xs_hbm, sem_rows)

    u_scr[:, 0:HALO, :] = u_scr[:, C:C + HALO, :]

    @pl.when(c == n_c - 1)
    def _():
        npool_ref[...] = u_scr[:, HALO + C - POOL_HIST:HALO + C, :]
        nret_ref[...] = s_scr[...]


def _decay_tables(C):
    log_g = jnp.log(1.0 - jnp.exp2(-5.0 - jnp.arange(RET_HEADS, dtype=_F32)))
    i = jnp.arange(C, dtype=_F32)
    diff = i[:, None] - i[None, :]
    dmask = jnp.where(diff >= 0, jnp.exp(jnp.maximum(diff, 0.0)[None] * log_g[:, None, None]), 0.0)
    xi = jnp.exp((i[None, :] + 1.0) * log_g[:, None])
    zeta = jnp.exp((C - 1.0 - i)[None, :] * log_g[:, None])
    g_chunk = jnp.exp(C * log_g)
    return dmask, xi, zeta, g_chunk


def _rope_tables(pos):
    half = RET_DK // 2
    freqs = ROPE_BASE ** (-jnp.arange(half, dtype=_F32) / half)
    ang = pos[:, None] * freqs[None, :]
    cos, sin = jnp.cos(ang), jnp.sin(ang)
    return jnp.concatenate([cos, cos], axis=-1), jnp.concatenate([-sin, sin], axis=-1)


def _full(shape):
    n = len(shape)
    return pl.BlockSpec(shape, lambda *_: (0,) * n)


def _mixer_prompt(x, wts, cap):
    B, L, _ = x.shape
    C = RET_CHUNK if L % RET_CHUNK == 0 else L
    n_c = L // C
    T = B * L
    dmask, xi, zeta, g_chunk = _decay_tables(C)
    xi_b = jnp.broadcast_to(xi[:, :, None], (RET_HEADS, C, RET_DV))
    zeta_b = jnp.broadcast_to(zeta[:, :, None], (RET_HEADS, C, RET_DK))
    gc_b = jnp.broadcast_to(g_chunk[:, None, None], (RET_HEADS, 1, RET_DV))
    cos2, sin2 = _rope_tables(jnp.arange(L).astype(_F32))

    NSEQ = MIXER_SEQS if B % MIXER_SEQS == 0 else 1
    B2 = B // NSEQ
    T2 = T // NSEQ
    x = x.reshape(NSEQ, B2, L, D_MODEL)
    tok = lambda b, c: (0, b * n_c + c, 0)
    in_specs = [
        pl.BlockSpec((NSEQ, None, C, D_MODEL), lambda b, c: (0, b, c, 0)),
        _full((1, D_MODEL)), _full((D_MODEL, IN_WIDTH)), _full((POOL_GROUPS, POOL_GC, POOL_OUT_GC)),
        _full((1, D_MODEL)), _full((1, D_MODEL)), _full((D_MODEL, D_MODEL)), _full((1, D_MODEL)),
        _full((D_MODEL, ROUTER_W)),
        pl.BlockSpec((C, RET_DK), lambda b, c: (c, 0)), pl.BlockSpec((C, RET_DK), lambda b, c: (c, 0)),
        _full((RET_HEADS, C, C)), _full((RET_HEADS, C, RET_DV)), _full((RET_HEADS, C, RET_DK)),
        _full((RET_HEADS, 1, RET_DV)),
    ]
    R = NSEQ * C
    out_shape = (
        jax.ShapeDtypeStruct((NSEQ, T2, D_MODEL), _F32),
        jax.ShapeDtypeStruct((NSEQ, T2, LANES), jnp.int32),
        jax.ShapeDtypeStruct((NSEQ, T2, LANES), _F32),
        jax.ShapeDtypeStruct((NSEQ, B2, POOL_HIST, POOL_IN), _F32),
        jax.ShapeDtypeStruct((NSEQ, B2, RET_HEADS, RET_DK, RET_DV), _F32),
        jax.ShapeDtypeStruct((1, LANES), _F32),
        jax.ShapeDtypeStruct((N_EXPERTS * cap,) + ROW_TILE, _F32),
    )
    out_specs = (
        pl.BlockSpec((NSEQ, C, D_MODEL), tok),
        pl.BlockSpec((NSEQ, C, LANES), tok), pl.BlockSpec((NSEQ, C, LANES), tok),
        pl.BlockSpec((NSEQ, None, POOL_HIST, POOL_IN), lambda b, c: (0, b, 0, 0)),
        pl.BlockSpec((NSEQ, None, RET_HEADS, RET_DK, RET_DV), lambda b, c: (0, b, 0, 0, 0)),
        _full((1, LANES)),
        pl.BlockSpec(memory_space=pl.ANY),
    )
    x1, dst, rw, npool, nret, counts, xs = pl.pallas_call(
        functools.partial(_mixer_prompt_kernel, cap=cap),
        grid=(B2, n_c),
        in_specs=in_specs,
        out_specs=out_specs,
        out_shape=out_shape,
        scratch_shapes=[pltpu.VMEM((NSEQ, HALO + C, POOL_IN), _F32),
                        pltpu.VMEM((NSEQ, RET_HEADS, RET_DK, RET_DV), _F32),
                        pltpu.VMEM((R, D_MODEL), _BF),
                        pltpu.VMEM((1, LANES), _F32),
                        pltpu.VMEM((R,) + ROW_TILE, _F32),
                        pltpu.VMEM((SUBLANES, R), jnp.int32),
                        pltpu.SMEM((2 * R,), jnp.int32),
                        pltpu.SemaphoreType.DMA, pltpu.SemaphoreType.DMA],
        compiler_params=pltpu.CompilerParams(dimension_semantics=("arbitrary", "arbitrary"),
                                             vmem_limit_bytes=VMEM_LIMIT, has_side_effects=True),
        name="mixer_prompt",
    )(x, wts["g_mix"], wts["w_in"], wts["w_pool"], wts["pool_scale"], wts["ret_gn"], wts["w_out"],
      wts["g_ffn"], wts["w_rt"], cos2, sin2, dmask, xi_b, zeta_b, gc_b)
    return (x1.reshape(T, D_MODEL), dst.reshape(T, LANES), rw.reshape(T, LANES),
            npool.reshape(B, POOL_HIST, POOL_IN), nret.reshape(B, RET_HEADS, RET_DK, RET_DV), counts, xs)


def _prep_weights(g_mix, w_in, w_pool, pool_scale, ret_gn, w_out, g_ffn, w_grp, w_exp):
    w_rt = jnp.concatenate([w_grp, w_exp.reshape(D_MODEL, N_EXPERTS)], axis=1)
    w_rt = jnp.pad(w_rt, ((0, 0), (0, ROUTER_W - w_rt.shape[1])))
    row = lambda v: v.reshape(1, D_MODEL)
    return dict(g_mix=row(g_mix), w_in=w_in.astype(_BF), w_pool=w_pool.astype(_BF), pool_scale=row(pool_scale),
                ret_gn=row(ret_gn), w_out=w_out.astype(_BF), g_ffn=row(g_ffn), w_rt=w_rt.astype(_BF))


SAMPLE_TB = 8


def _mixer_sample_kernel(x_ref, spool_ref, sret_ref, g_mix_ref, w_in_ref, w_pool_ref, pscale_ref, gn_ref,
                         w_out_ref, g_ffn_ref, w_rt_ref, cos_ref, sin_ref, dm_ref, xi_ref, zeta_ref, gc_ref,
                         carry_in_ref, xs_in_hbm,
                         x1_ref, dst_ref, rw_ref, npool_ref, nret_ref, cnt_ref, xs_hbm,
                         u_scr, q_scr, k_scr, qt_scr, kt_scr, v_scr, ga_scr, gb_scr, pooled_scr, o_scr, mixed_scr,
                         carry_scr, h2_scr, dst_v, dst_s, cnt_v, cnt_s, zbuf, sem_idx, sem_rows, sem_zero, *, cap):
    del xs_in_hbm
    t = pl.program_id(0)
    n_t = pl.num_programs(0)
    Bs = x_ref.shape[0]
    TB = sret_ref.shape[0]

    @pl.when(t == 0)
    def _():
        h = _rms(x_ref[...], g_mix_ref[...]).astype(_BF)
        u_scr[...] = _dot(h, w_in_ref[:, OFF_U:OFF_U + POOL_IN])
        q = _dot(h, w_in_ref[:, OFF_Q:OFF_Q + QK_W])
        k = _dot(h, w_in_ref[:, OFF_K:OFF_K + QK_W])
        for j in range(RET_HEADS):
            qs = slice(j * RET_DK, (j + 1) * RET_DK)
            qf = _rotary(q[:, qs], cos_ref[...], sin_ref[...])
            kf = _rotary(k[:, qs], cos_ref[...], sin_ref[...]) * (RET_DK ** -0.5)
            q_scr[:, qs] = qf
            k_scr[:, qs] = kf
            qt_scr[j] = qf.T
            kt_scr[j] = kf.T
        v_scr[...] = _dot(h, w_in_ref[:, OFF_V:OFF_V + V_W])
        ga_scr[...] = _dot(h, w_in_ref[:, OFF_GA:OFF_GA + D_MODEL])
        gb_scr[...] = _dot(h, w_in_ref[:, OFF_GB:OFF_GB + D_MODEL])

    shift = (Bs - t * TB) % Bs
    qt = [pltpu.roll(qt_scr[j], shift, 1) for j in range(RET_HEADS)]
    kt = [pltpu.roll(kt_scr[j], shift, 1) for j in range(RET_HEADS)]

    blk = pl.ds(pl.multiple_of(t * TB, TB), TB)
    u_blk = u_scr[blk, :]
    q_blk = q_scr[blk, :]
    k_blk = k_scr[blk, :]
    v_blk = v_scr[blk, :]
    score = [jnp.sum(q_blk[:, j * RET_DK:(j + 1) * RET_DK] * k_blk[:, j * RET_DK:(j + 1) * RET_DK],
                     axis=-1, keepdims=True) * dm_ref[j] for j in range(RET_HEADS)]

    pooled_rows, o_rows = [], []
    for i in range(TB):
        u_row = u_blk[i:i + 1, :]
        groups = []
        for g, win in enumerate(POOL_WINDOWS):
            cs = slice(g * POOL_GC, (g + 1) * POOL_GC)
            hist_sum = jnp.sum(spool_ref[i, POOL_HIST - (win - 1):POOL_HIST, cs], axis=0, keepdims=True)
            groups.append((u_row[:, cs] + hist_sum) / jnp.float32(win) - u_row[:, cs])
        pooled_rows.append(jnp.concatenate(groups, axis=1))
        npool_ref[i, 0:POOL_HIST - 1, :] = spool_ref[i, 1:POOL_HIST, :]
        npool_ref[i, POOL_HIST - 1:POOL_HIST, :] = u_row

        heads = []
        for j in range(RET_HEADS):
            s_old = sret_ref[i, j]
            v_row = v_blk[i:i + 1, j * RET_DV:(j + 1) * RET_DV]
            qcol = qt[j][:, i:i + 1]
            kcol = kt[j][:, i:i + 1]
            qs_old = jnp.sum(qcol * s_old, axis=0, keepdims=True)
            heads.append(score[j][i:i + 1, :] * v_row + qs_old * xi_ref[j])
            nret_ref[i, j] = gc_ref[j] * s_old + (kcol * zeta_ref[j]) * v_row
        o_rows.append(jnp.concatenate(heads, axis=1))
    pooled_scr[blk, :] = jnp.concatenate(pooled_rows, axis=0)
    o_scr[blk, :] = jnp.concatenate(o_rows, axis=0)

    @pl.when(t == n_t - 1)
    def _():
        for j in range(RET_HEADS):
            cs = slice(j * POOL_GC, (j + 1) * POOL_GC)
            osl = slice(j * POOL_OUT_GC, (j + 1) * POOL_OUT_GC)
            pool_out = _dot(pooled_scr[:, cs].astype(_BF), w_pool_ref[j]) * pscale_ref[:, osl]
            ret_out = _group_norm(o_scr[:, osl], gn_ref[:, osl])
            mixed_scr[:, osl] = (jax.nn.sigmoid(ga_scr[:, osl]) * pool_out
                                 + jax.nn.sigmoid(gb_scr[:, osl]) * ret_out).astype(_BF)
        h2, e0, e1 = _post_mix(x_ref[...], mixed_scr, w_out_ref, g_ffn_ref, w_rt_ref, x1_ref, rw_ref)
        carry_scr[...] = carry_in_ref[...]
        _dispatch_start(h2, e0, e1, cap, carry_scr, h2_scr, dst_v, dst_s, sem_idx, sem_rows, xs_hbm, dst_ref)
        cnt_ref[...] = carry_scr[...]

        zbuf[...] = jnp.zeros(zbuf.shape, _F32)
        cnt_v[...] = jnp.broadcast_to(carry_scr[...], cnt_v.shape).astype(jnp.int32)
        cnt_cp = pltpu.make_async_copy(cnt_v.at[0], cnt_s, sem_idx)
        cnt_cp.start()
        cnt_cp.wait()
        for e in range(N_EXPERTS):
            pltpu.make_async_copy(zbuf, xs_hbm.at[pl.ds(e * cap + cnt_s[e], MOE_BLOCK)], sem_zero).start()
        for e in range(N_EXPERTS):
            pltpu.make_async_copy(zbuf, xs_hbm.at[pl.ds(e * cap + cnt_s[e], MOE_BLOCK)], sem_zero).wait()
        _dispatch_wait(h2_scr, xs_hbm, sem_rows)


def _mixer_sample(x, state_pool, state_ret, wts, pos0, carry_in, xs, cap):
    Bs = x.shape[0]
    assert Bs == LANES and Bs % SAMPLE_TB == 0
    TB = SAMPLE_TB
    dmask, xi, zeta, g_chunk = _decay_tables(1)
    dm_b = jnp.broadcast_to(dmask, (RET_HEADS, 1, 1))
    xi_b = jnp.broadcast_to(xi[:, :, None], (RET_HEADS, 1, RET_DV))
    zeta_b = jnp.broadcast_to(zeta[:, :, None], (RET_HEADS, 1, 1))
    gc_b = jnp.broadcast_to(g_chunk[:, None, None], (RET_HEADS, 1, RET_DV))
    cos2, sin2 = _rope_tables((pos0 + jnp.arange(1)).astype(_F32))

    in_specs = [
        _full((Bs, D_MODEL)),
        pl.BlockSpec((TB, POOL_HIST, POOL_IN), lambda t: (t, 0, 0)),
        pl.BlockSpec((TB, RET_HEADS, RET_DK, RET_DV), lambda t: (t, 0, 0, 0)),
        _full((1, D_MODEL)), _full((D_MODEL, IN_WIDTH)), _full((POOL_GROUPS, POOL_GC, POOL_OUT_GC)),
        _full((1, D_MODEL)), _full((1, D_MODEL)), _full((D_MODEL, D_MODEL)), _full((1, D_MODEL)),
        _full((D_MODEL, ROUTER_W)),
        _full((1, RET_DK)), _full((1, RET_DK)),
        _full((RET_HEADS, 1, 1)), _full((RET_HEADS, 1, RET_DV)), _full((RET_HEADS, 1, 1)),
        _full((RET_HEADS, 1, RET_DV)),
        _full((1, LANES)), pl.BlockSpec(memory_space=pl.ANY),
    ]
    out_shape = (
        jax.ShapeDtypeStruct((Bs, D_MODEL), _F32),
        jax.ShapeDtypeStruct((Bs, LANES), jnp.int32),
        jax.ShapeDtypeStruct((Bs, LANES), _F32),
        jax.ShapeDtypeStruct((Bs, POOL_HIST, POOL_IN), _F32),
        jax.ShapeDtypeStruct((Bs, RET_HEADS, RET_DK, RET_DV), _F32),
        jax.ShapeDtypeStruct((1, LANES), _F32),
        jax.ShapeDtypeStruct(xs.shape, _F32),
    )
    out_specs = (
        _full((Bs, D_MODEL)), _full((Bs, LANES)), _full((Bs, LANES)),
        pl.BlockSpec((TB, POOL_HIST, POOL_IN), lambda t: (t, 0, 0)),
        pl.BlockSpec((TB, RET_HEADS, RET_DK, RET_DV), lambda t: (t, 0, 0, 0)),
        _full((1, LANES)), pl.BlockSpec(memory_space=pl.ANY),
    )
    f32 = lambda *s: pltpu.VMEM(s, _F32)
    n_in = len(in_specs)
    return pl.pallas_call(
        functools.partial(_mixer_sample_kernel, cap=cap),
        grid=(Bs // TB,),
        in_specs=in_specs,
        out_specs=out_specs,
        out_shape=out_shape,
        scratch_shapes=[f32(Bs, POOL_IN), f32(Bs, QK_W), f32(Bs, QK_W), f32(RET_HEADS, RET_DK, Bs),
                        f32(RET_HEADS, RET_DK, Bs), f32(Bs, V_W), f32(Bs, D_MODEL), f32(Bs, D_MODEL),
                        f32(Bs, POOL_IN), f32(Bs, V_W), pltpu.VMEM((Bs, D_MODEL), _BF),
                        f32(1, LANES), f32(Bs, *ROW_TILE), pltpu.VMEM((SUBLANES, Bs), jnp.int32),
                        pltpu.SMEM((2 * Bs,), jnp.int32), pltpu.VMEM((SUBLANES, LANES), jnp.int32),
                        pltpu.SMEM((LANES,), jnp.int32), f32(MOE_BLOCK, *ROW_TILE),
                        pltpu.SemaphoreType.DMA, pltpu.SemaphoreType.DMA, pltpu.SemaphoreType.DMA],
        input_output_aliases={n_in - 1: 6},
        compiler_params=pltpu.CompilerParams(dimension_semantics=("arbitrary",), vmem_limit_bytes=VMEM_LIMIT,
                                             has_side_effects=True),
        name="mixer_sample",
    )(x, state_pool, state_ret, wts["g_mix"], wts["w_in"], wts["w_pool"], wts["pool_scale"], wts["ret_gn"],
      wts["w_out"], wts["g_ffn"], wts["w_rt"], cos2, sin2, dm_b, xi_b, zeta_b, gc_b, carry_in, xs)


def _plan_kernel(re_ref, rank_ref, cnt_ref, carry_scr):
    i = pl.program_id(0)
    TP = re_ref.shape[0]

    @pl.when(i == 0)
    def _():
        carry_scr[...] = jnp.zeros(carry_scr.shape, _F32)

    re = re_ref[...]
    lane = lax.broadcasted_iota(jnp.int32, re.shape, 1)
    m0 = lane == re[:, 0:1]
    m1 = lane == re[:, 1:2]
    onehot = jnp.where(m0 | m1, 1.0, 0.0)
    r_i = lax.broadcasted_iota(jnp.int32, (TP, TP), 0)
    c_i = lax.broadcasted_iota(jnp.int32, (TP, TP), 1)
    tri = jnp.where(c_i < r_i, 1.0, 0.0).astype(_BF)
    before = _dot(tri, onehot.astype(_BF)) + carry_scr[...]
    r0 = jnp.sum(jnp.where(m0, before, 0.0), axis=-1, keepdims=True)
    r1 = jnp.sum(jnp.where(m1, before, 0.0), axis=-1, keepdims=True)
    rank_ref[...] = jnp.where(lane == 0, r0, jnp.where(lane == 1, r1, 0.0)).astype(jnp.int32)
    carry_scr[...] += jnp.sum(onehot, axis=0, keepdims=True)
    cnt_ref[...] = carry_scr[...].astype(jnp.int32)


def _plan_tile(n_tokens):
    return max(t for t in range(SUBLANES, 513, SUBLANES) if n_tokens % t == 0)


def _plan(re_all, tile):
    T = re_all.shape[0]
    assert T % tile == 0
    return pl.pallas_call(
        _plan_kernel,
        grid=(T // tile,),
        in_specs=[pl.BlockSpec((tile, LANES), lambda i: (i, 0))],
        out_specs=(pl.BlockSpec((tile, LANES), lambda i: (i, 0)), _full((1, LANES))),
        out_shape=(jax.ShapeDtypeStruct((T, LANES), jnp.int32), jax.ShapeDtypeStruct((1, LANES), jnp.int32)),
        scratch_shapes=[pltpu.VMEM((1, LANES), _F32)],
        compiler_params=pltpu.CompilerParams(dimension_semantics=("arbitrary",)),
        name="moe_plan",
    )(re_all)


IDX_CHUNK = 1024
ISSUE_UNROLL = 8


def _scatter_kernel(zero_fill, zstart_ref, slots_hbm, h2_ref, *rest):
    if zero_fill:
        xs_hbm, idx_smem, zbuf, sem_idx, sem_rows = rest
    else:
        _, xs_hbm, idx_smem, zbuf, sem_idx, sem_rows = rest
    t = pl.program_id(0)
    TS = h2_ref.shape[0]

    if zero_fill:
        @pl.when(t == 0)
        def _():
            zbuf[...] = jnp.zeros(zbuf.shape, zbuf.dtype)
            for e in range(N_EXPERTS):
                pltpu.make_async_copy(zbuf, xs_hbm.at[pl.ds(zstart_ref[e], MOE_BLOCK)], sem_rows).start()
            for e in range(N_EXPERTS):
                pltpu.make_async_copy(zbuf, xs_hbm.at[pl.ds(zstart_ref[e], MOE_BLOCK)], sem_rows).wait()

    idx_cp = pltpu.make_async_copy(slots_hbm.at[pl.ds(t * IDX_CHUNK, IDX_CHUNK)], idx_smem, sem_idx)
    idx_cp.start()
    idx_cp.wait()

    def issue(r, carry):
        for kk in range(2):
            pltpu.make_async_copy(h2_ref.at[r], xs_hbm.at[idx_smem[2 * r + kk]], sem_rows).start(priority=kk)
        return carry

    lax.fori_loop(0, TS, issue, 0, unroll=ISSUE_UNROLL)
    for kk in range(2):
        pltpu.make_async_copy(h2_ref, xs_hbm.at[pl.ds(0, TS)], sem_rows).wait()


def _scatter(h2, slots, zstart, xs, n_rows, tile):
    T = h2.shape[0]
    assert T % tile == 0 and 2 * tile <= IDX_CHUNK and slots.shape[0] == (T // tile) * IDX_CHUNK
    zero_fill = xs is None
    any_spec = pl.BlockSpec(memory_space=pl.ANY)
    in_specs = [any_spec, pl.BlockSpec((tile,) + ROW_TILE, lambda t, z: (t, 0, 0))]
    args = [slots, h2]
    aliases = {}
    if not zero_fill:
        in_specs.append(any_spec)
        args.append(xs)
        aliases = {3: 0}
    grid_spec = pltpu.PrefetchScalarGridSpec(
        num_scalar_prefetch=1,
        grid=(T // tile,),
        in_specs=in_specs,
        out_specs=any_spec,
        scratch_shapes=[pltpu.SMEM((IDX_CHUNK,), jnp.int32), pltpu.VMEM((MOE_BLOCK,) + ROW_TILE, _F32),
                        pltpu.SemaphoreType.DMA, pltpu.SemaphoreType.DMA],
    )
    return pl.pallas_call(
        functools.partial(_scatter_kernel, zero_fill),
        grid_spec=grid_spec,
        out_shape=jax.ShapeDtypeStruct((n_rows,) + ROW_TILE, _F32),
        input_output_aliases=aliases,
        compiler_params=pltpu.CompilerParams(dimension_semantics=("arbitrary",), has_side_effects=True),
        name="moe_scatter_first" if zero_fill else "moe_scatter_more",
    )(zstart, *args)


def _ffn_kernel(bexp_ref, brow_ref, nused_ref, first_ref, slot_ref, next_ref,
                xs_ref, w1_hbm, w3_hbm, w2_hbm, ys_ref, w13_scr, w2_scr, st1, st3, st2, sem_w):
    del brow_ref
    i = pl.program_id(0)

    def weight_copies(e, s):
        return (pltpu.make_async_copy(w1_hbm.at[e], st1.at[s], sem_w.at[s]),
                pltpu.make_async_copy(w3_hbm.at[e], st3.at[s], sem_w.at[s]),
                pltpu.make_async_copy(w2_hbm.at[e], st2.at[s], sem_w.at[s]))

    @pl.when(i < nused_ref[0])
    def _():
        @pl.when(first_ref[i] == 1)
        def _():
            e = bexp_ref[i]
            s = slot_ref[i]

            @pl.when(i == 0)
            def _():
                for cp in weight_copies(e, s):
                    cp.start()

            for cp in weight_copies(e, s):
                cp.wait()
            @pl.when(next_ref[i] >= 0)
            def _():
                for cp in weight_copies(next_ref[i], 1 - s):
                    cp.start()

            w13_scr[:, 0:D_EXPERT] = st1[s].astype(_BF)
            w13_scr[:, D_EXPERT:2 * D_EXPERT] = st3[s].astype(_BF)
            w2_scr[...] = st2[s].astype(_BF)

        xb = xs_ref[...].reshape(MOE_BLOCK, D_MODEL).astype(_BF)
        ab = _dot(xb, w13_scr[...])
        hid = jax.nn.silu(ab[:, 0:D_EXPERT]) * ab[:, D_EXPERT:2 * D_EXPERT]
        ys_ref[...] = _dot(hid.astype(_BF), w2_scr[...]).reshape(ys_ref.shape)


def _ffn(xs, block_exp, block_row, n_used, run_first, run_slot, run_next, w1, w3, w2, n_blocks):
    blk = lambda i, *_: (_[1][i], 0, 0)
    any_spec = pl.BlockSpec(memory_space=pl.ANY)
    grid_spec = pltpu.PrefetchScalarGridSpec(
        num_scalar_prefetch=6,
        grid=(n_blocks,),
        in_specs=[pl.BlockSpec((MOE_BLOCK,) + ROW_TILE, blk), any_spec, any_spec, any_spec],
        out_specs=pl.BlockSpec((MOE_BLOCK,) + ROW_TILE, blk),
        scratch_shapes=[pltpu.VMEM((D_MODEL, 2 * D_EXPERT), _BF), pltpu.VMEM((D_EXPERT, D_MODEL), _BF),
                        pltpu.VMEM((2, D_MODEL, D_EXPERT), _F32), pltpu.VMEM((2, D_MODEL, D_EXPERT), _F32),
                        pltpu.VMEM((2, D_EXPERT, D_MODEL), _F32), pltpu.SemaphoreType.DMA((2,))],
    )
    return pl.pallas_call(
        _ffn_kernel,
        grid_spec=grid_spec,
        out_shape=jax.ShapeDtypeStruct(xs.shape, _F32),
        compiler_params=pltpu.CompilerParams(dimension_semantics=("arbitrary",), vmem_limit_bytes=VMEM_LIMIT),
        name="moe_ffn",
    )(block_exp, block_row, n_used, run_first, run_slot, run_next, xs, w1, w3, w2)


def _combine_kernel(slots_hbm, x1_ref, rw_ref, g_ref, ys_hbm, y_ref, idx_smem, buf, sem_idx, sem_rows):
    t = pl.program_id(0)
    TC = x1_ref.shape[0]
    idx_cp = pltpu.make_async_copy(slots_hbm.at[pl.ds(t * IDX_CHUNK, IDX_CHUNK)], idx_smem, sem_idx)
    idx_cp.start()
    idx_cp.wait()

    def issue(r, carry):
        for kk in range(2):
            pltpu.make_async_copy(ys_hbm.at[idx_smem[2 * r + kk]], buf.at[kk, r], sem_rows).start(priority=kk)
        return carry

    lax.fori_loop(0, TC, issue, 0, unroll=ISSUE_UNROLL)
    for kk in range(2):
        pltpu.make_async_copy(ys_hbm.at[pl.ds(0, TC)], buf.at[kk], sem_rows).wait()

    w = rw_ref[...]
    y0 = buf[0].reshape(TC, D_MODEL)
    y1 = buf[1].reshape(TC, D_MODEL)
    x2 = x1_ref[...] + (w[:, 0:1] * y0 + w[:, 1:2] * y1)
    y_ref[...] = _rms(x2, g_ref[...])


def _combine(x1, rw, slots, ys, g_final, tile):
    T = x1.shape[0]
    assert T % tile == 0 and 2 * tile <= IDX_CHUNK and slots.shape[0] == (T // tile) * IDX_CHUNK
    any_spec = pl.BlockSpec(memory_space=pl.ANY)
    return pl.pallas_call(
        _combine_kernel,
        grid=(T // tile,),
        in_specs=[any_spec, pl.BlockSpec((tile, D_MODEL), lambda t: (t, 0)),
                  pl.BlockSpec((tile, LANES), lambda t: (t, 0)), _full((1, D_MODEL)), any_spec],
        out_specs=pl.BlockSpec((tile, D_MODEL), lambda t: (t, 0)),
        out_shape=jax.ShapeDtypeStruct((T, D_MODEL), _F32),
        scratch_shapes=[pltpu.SMEM((IDX_CHUNK,), jnp.int32), pltpu.VMEM((2, tile) + ROW_TILE, _F32),
                        pltpu.SemaphoreType.DMA, pltpu.SemaphoreType.DMA],
        compiler_params=pltpu.CompilerParams(dimension_semantics=("arbitrary",), vmem_limit_bytes=VMEM_LIMIT),
        name="moe_combine",
    )(slots, x1, rw, g_final.reshape(1, D_MODEL), ys)


def _tile_for(n_tokens):
    tile = IDX_CHUNK // 2
    return tile if n_tokens % tile == 0 else n_tokens


def _chunked_slots(slot, tile):
    n_tiles = slot.shape[0] // tile
    s = slot.reshape(n_tiles, 2 * tile)
    return jnp.pad(s, ((0, 0), (0, IDX_CHUNK - 2 * tile))).reshape(-1)


def kernel(x_prompt, x_sample, state_pool, state_ret, g_mix, w_in, w_pool, pool_scale, ret_gn, w_out, g_ffn, w_grp, w_exp, w1, w3, w2, g_final):
    Bp, Lp, _ = x_prompt.shape
    Bs = x_sample.shape[0]
    Tp = Bp * Lp
    wts = _prep_weights(g_mix[0], w_in[0], w_pool[0], pool_scale[0], ret_gn[0], w_out[0], g_ffn[0], w_grp[0], w_exp[0])

    T_all = Tp + Bs
    cap = (-(-T_all // MOE_BLOCK) + 1) * MOE_BLOCK
    cap_blocks = cap // MOE_BLOCK

    x1p, dst_p, rwp, npool_p, nret_p, counts_p, xs = _mixer_prompt(x_prompt, wts, cap)
    x1s, dst_s, rws, npool_s, nret_s, counts, xs = _mixer_sample(
        x_sample.reshape(Bs, D_MODEL), state_pool[0], state_ret[0], wts, PAST_LEN, counts_p, xs, cap)

    counts = counts[0, :N_EXPERTS].astype(jnp.int32)
    nblk = (counts + MOE_BLOCK - 1) // MOE_BLOCK
    blk_end = jnp.cumsum(nblk)
    n_used = blk_end[-1:].astype(jnp.int32)
    n_blocks = -(-2 * T_all // MOE_BLOCK) + N_EXPERTS
    step = jnp.minimum(jnp.arange(n_blocks, dtype=jnp.int32), n_used - 1)
    block_exp = jnp.minimum(jnp.sum((blk_end[None, :] <= step[:, None]).astype(jnp.int32), axis=1), N_EXPERTS - 1)
    blk_start = jnp.sum(jnp.where(block_exp[:, None] == jnp.arange(N_EXPERTS, dtype=jnp.int32), (blk_end - nblk)[None, :], 0),
                        axis=1)
    block_row = (block_exp * cap_blocks + (step - blk_start)).astype(jnp.int32)
    block_exp = block_exp.astype(jnp.int32)
    experts = jnp.arange(N_EXPERTS, dtype=jnp.int32)
    nonempty = nblk > 0
    run_first = (step == blk_start).astype(jnp.int32)
    run_slot = (jnp.sum((nonempty[None, :] & (experts[None, :] < block_exp[:, None])).astype(jnp.int32), axis=1) % 2)
    later = nonempty[None, :] & (experts[None, :] > block_exp[:, None])
    run_next = jnp.min(jnp.where(later, experts[None, :], N_EXPERTS), axis=1)
    run_next = jnp.where(run_next == N_EXPERTS, -1, run_next).astype(jnp.int32)

    tile_p, tile_s = _tile_for(Tp), _tile_for(Bs)
    slots_p = _chunked_slots(dst_p[:, :2], tile_p)
    slots_s = _chunked_slots(dst_s[:, :2], tile_s)
    ys = _ffn(xs, block_exp, block_row, n_used, run_first, run_slot.astype(jnp.int32), run_next,
              w1[0], w3[0], w2[0], n_blocks)
    y_p = _combine(x1p, rwp, slots_p, ys, g_final, tile_p)
    y_s = _combine(x1s, rws, slots_s, ys, g_final, tile_s)

    return (y_p.reshape(Bp, Lp, D_MODEL), y_s.reshape(Bs, 1, D_MODEL),
            npool_p[None], nret_p[None], npool_s[None], nret_s[None])
```

```python
import functools

import jax
import jax.numpy as jnp
from jax import lax
from jax.experimental import pallas as pl
from jax.experimental.pallas import tpu as pltpu
from jax.experimental.pallas import tpu_sc as plsc

D_MODEL = 1024
EPS = 1e-6
POOL_GROUPS = 4
POOL_IN = D_MODEL // 2
POOL_GC = POOL_IN // POOL_GROUPS
POOL_OUT_GC = D_MODEL // POOL_GROUPS
POOL_WINDOWS = (2, 4, 8, 16)
POOL_HIST = max(POOL_WINDOWS) - 1
RET_HEADS = 4
RET_DK = D_MODEL // 8
RET_DV = D_MODEL // RET_HEADS
ROPE_BASE = 10000.0
PAST_LEN = 16384
N_GROUPS = 4
EXPERTS_PER_GROUP = 8
N_EXPERTS = N_GROUPS * EXPERTS_PER_GROUP
D_EXPERT = D_MODEL // 4
QK_W = RET_HEADS * RET_DK
V_W = RET_HEADS * RET_DV
OFF_U = 0
OFF_Q = POOL_IN
OFF_K = OFF_Q + QK_W
OFF_V = OFF_K + QK_W
OFF_GA = OFF_V + V_W
OFF_GB = OFF_GA + D_MODEL
IN_WIDTH = OFF_GB + D_MODEL

LANES = 128
SUBLANES = 8
ROW_TILE = (SUBLANES, LANES)
assert SUBLANES * LANES == D_MODEL
HALO = 16
RET_CHUNK = 256
MIXER_SEQS = 2
MOE_BLOCK = 512
ROUTER_W = LANES
VMEM_LIMIT = 56 * 1024 * 1024

_BF = jnp.bfloat16
_F32 = jnp.float32


def _rms(x, g):
    inv = lax.rsqrt(jnp.mean(x * x, axis=-1, keepdims=True) + EPS)
    return x * inv * g


def _dot(a, b):
    return jnp.dot(a, b, preferred_element_type=_F32)


def _rotary(x, cos2, sin2):
    return x * cos2 + pltpu.roll(x, RET_DK // 2, 1) * sin2


def _route(logits):
    lane = lax.broadcasted_iota(jnp.int32, logits.shape, 1).astype(_F32)
    neg = jnp.float32(-jnp.inf)
    big = jnp.float32(1 << 20)
    lg = jnp.where(lane < N_GROUPS, logits, neg)
    mg = jnp.max(lg, axis=-1, keepdims=True)
    g_idx = jnp.min(jnp.where(lg == mg, lane, big), axis=-1, keepdims=True)
    p_g = 1.0 / jnp.sum(jnp.exp(lg - mg), axis=-1, keepdims=True)
    lo = N_GROUPS + g_idx * EXPERTS_PER_GROUP
    in_grp = (lane >= lo) & (lane < lo + EXPERTS_PER_GROUP)
    le = jnp.where(in_grp, logits, neg)
    m1 = jnp.max(le, axis=-1, keepdims=True)
    i1 = jnp.min(jnp.where(le == m1, lane, big), axis=-1, keepdims=True)
    le2 = jnp.where(lane == i1, neg, le)
    m2 = jnp.max(le2, axis=-1, keepdims=True)
    i2 = jnp.min(jnp.where(le2 == m2, lane, big), axis=-1, keepdims=True)
    t = jnp.exp(m2 - m1)
    den = 1.0 + t
    e0 = (i1 - N_GROUPS).astype(jnp.int32)
    e1 = (i2 - N_GROUPS).astype(jnp.int32)
    return e0, e1, p_g * (1.0 / den), p_g * (t / den)


def _post_mix(x, mixed_ref, w_out_ref, g_ffn_ref, w_rt_ref, x1_ref, rw_ref):
    x1 = x + _dot(mixed_ref[...], w_out_ref[...])
    x1_ref[...] = x1.reshape(x1_ref.shape)
    h2 = _rms(x1, g_ffn_ref[...])
    e0, e1, w0, w1 = _route(_dot(h2.astype(_BF), w_rt_ref[...]))
    lane = lax.broadcasted_iota(jnp.int32, (x.shape[0], LANES), 1)
    rw_ref[...] = jnp.where(lane == 0, w0, jnp.where(lane == 1, w1, 0.0)).reshape(rw_ref.shape)
    return h2, e0, e1


def _rank_rows(e0, e1, cap, carry_scr):
    R = e0.shape[0]
    lane = lax.broadcasted_iota(jnp.int32, (R, LANES), 1)
    m0 = lane == e0
    m1 = lane == e1
    onehot = jnp.where(m0 | m1, 1.0, 0.0)
    r_i = lax.broadcasted_iota(jnp.int32, (R, R), 0)
    c_i = lax.broadcasted_iota(jnp.int32, (R, R), 1)
    tri = jnp.where(c_i < r_i, 1.0, 0.0).astype(_BF)
    before = _dot(tri, onehot.astype(_BF)) + carry_scr[...]
    d0 = e0.astype(_F32) * cap + jnp.sum(jnp.where(m0, before, 0.0), axis=-1, keepdims=True)
    d1 = e1.astype(_F32) * cap + jnp.sum(jnp.where(m1, before, 0.0), axis=-1, keepdims=True)
    carry_scr[...] += jnp.sum(onehot, axis=0, keepdims=True)
    return jnp.where(lane == 0, d0, jnp.where(lane == 1, d1, 0.0))


def _group_norm(o, gain):
    mu = jnp.mean(o, axis=-1, keepdims=True)
    d = o - mu
    var = jnp.mean(d * d, axis=-1, keepdims=True)
    return d * lax.rsqrt(var + EPS) * gain


def _mixer_prompt_kernel(x_ref, g_mix_ref, w_in_ref, w_pool_ref, pscale_ref, gn_ref, w_out_ref, g_ffn_ref,
                         w_rt_ref, cos_ref, sin_ref, dmask_ref, xi_ref, zeta_ref, gc_ref,
                         x1_ref, dst_ref, dst_t_ref, rw_ref, npool_ref, nret_ref, cnt_ref, h2_ref,
                         u_scr, s_scr, mixed_scr, carry_scr, *, cap):
    c = pl.program_id(1)
    n_c = pl.num_programs(1)
    @pl.when((pl.program_id(0) == 0) & (c == 0))
    def _():
        carry_scr[...] = jnp.zeros(carry_scr.shape, _F32)
    NSEQ, C, _ = x_ref.shape
    R = NSEQ * C

    @pl.when(c == 0)
    def _():
        u_scr[:, 0:HALO, :] = jnp.zeros((NSEQ, HALO, POOL_IN), _F32)
        s_scr[...] = jnp.zeros(s_scr.shape, _F32)

    x = x_ref[...].reshape(R, D_MODEL)
    h = _rms(x, g_mix_ref[...]).astype(_BF)

    u_scr[:, HALO:HALO + C, :] = _dot(h, w_in_ref[:, OFF_U:OFF_U + POOL_IN]).reshape(NSEQ, C, POOL_IN)
    q = _dot(h, w_in_ref[:, OFF_Q:OFF_Q + QK_W])
    k = _dot(h, w_in_ref[:, OFF_K:OFF_K + QK_W])
    cos2 = jnp.concatenate([cos_ref[...]] * NSEQ, axis=0)
    sin2 = jnp.concatenate([sin_ref[...]] * NSEQ, axis=0)
    pos1 = (c * C + 1 + lax.broadcasted_iota(jnp.int32, (C, POOL_GC), 0)).astype(_F32)

    for j in range(RET_HEADS):
        win = POOL_WINDOWS[j]
        cs = slice(j * POOL_GC, (j + 1) * POOL_GC)
        n_rows = jnp.minimum(pos1, jnp.float32(win))
        pooled = []
        for sq in range(NSEQ):
            u_j = u_scr[sq, HALO:HALO + C, cs]
            s = u_j
            for d in range(1, win):
                s = s + u_scr[sq, HALO - d:HALO - d + C, cs]
            pooled.append(s / n_rows - u_j)
        osl = slice(j * POOL_OUT_GC, (j + 1) * POOL_OUT_GC)
        pool_out = _dot(jnp.concatenate(pooled, axis=0).astype(_BF), w_pool_ref[j]) * pscale_ref[:, osl]

        qs = slice(j * RET_DK, (j + 1) * RET_DK)
        qb = _rotary(q[:, qs], cos2, sin2).astype(_BF)
        kf = _rotary(k[:, qs], cos2, sin2) * (RET_DK ** -0.5)
        kb = kf.astype(_BF)
        vb = _dot(h, w_in_ref[:, OFF_V + j * RET_DV:OFF_V + (j + 1) * RET_DV]).astype(_BF)
        ret = []
        for sq in range(NSEQ):
            rows = slice(sq * C, (sq + 1) * C)
            scores = lax.dot_general(qb[rows], kb[rows], (((1,), (1,)), ((), ())),
                                     preferred_element_type=_F32) * dmask_ref[j]
            s_old = s_scr[sq, j]
            o = _dot(scores.astype(_BF), vb[rows]) + _dot(qb[rows], s_old.astype(_BF)) * xi_ref[j]
            kz = (kf[rows] * zeta_ref[j]).astype(_BF)
            s_scr[sq, j] = gc_ref[j] * s_old + lax.dot_general(kz, vb[rows], (((0,), (0,)), ((), ())),
                                                               preferred_element_type=_F32)
            ret.append(_group_norm(o, gn_ref[:, osl]))
        ret_out = jnp.concatenate(ret, axis=0)

        ga = _dot(h, w_in_ref[:, OFF_GA + j * RET_DV:OFF_GA + (j + 1) * RET_DV])
        gb = _dot(h, w_in_ref[:, OFF_GB + j * RET_DV:OFF_GB + (j + 1) * RET_DV])
        mixed_scr[:, osl] = (jax.nn.sigmoid(ga) * pool_out + jax.nn.sigmoid(gb) * ret_out).astype(_BF)

    h2, e0, e1 = _post_mix(x, mixed_scr, w_out_ref, g_ffn_ref, w_rt_ref, x1_ref, rw_ref)

    h2_ref[...] = h2.reshape(h2_ref.shape)
    dst = _rank_rows(e0, e1, cap, carry_scr)
    dst_ref[...] = dst.astype(jnp.int32).reshape(dst_ref.shape)
    dst_t = dst.T[0:SUBLANES, :].astype(jnp.int32)
    for sq in range(NSEQ):
        dst_t_ref[sq] = dst_t[:, sq * C:(sq + 1) * C]
    cnt_ref[...] = carry_scr[...]

    u_scr[:, 0:HALO, :] = u_scr[:, C:C + HALO, :]

    @pl.when(c == n_c - 1)
    def _():
        npool_ref[...] = u_scr[:, HALO + C - POOL_HIST:HALO + C, :]
        nret_ref[...] = s_scr[...]


def _decay_tables(C):
    log_g = jnp.log(1.0 - jnp.exp2(-5.0 - jnp.arange(RET_HEADS, dtype=_F32)))
    i = jnp.arange(C, dtype=_F32)
    diff = i[:, None] - i[None, :]
    dmask = jnp.where(diff >= 0, jnp.exp(jnp.maximum(diff, 0.0)[None] * log_g[:, None, None]), 0.0)
    xi = jnp.exp((i[None, :] + 1.0) * log_g[:, None])
    zeta = jnp.exp((C - 1.0 - i)[None, :] * log_g[:, None])
    g_chunk = jnp.exp(C * log_g)
    return dmask, xi, zeta, g_chunk


def _rope_tables(pos):
    half = RET_DK // 2
    freqs = ROPE_BASE ** (-jnp.arange(half, dtype=_F32) / half)
    ang = pos[:, None] * freqs[None, :]
    cos, sin = jnp.cos(ang), jnp.sin(ang)
    return jnp.concatenate([cos, cos], axis=-1), jnp.concatenate([-sin, sin], axis=-1)


def _full(shape):
    n = len(shape)
    return pl.BlockSpec(shape, lambda *_: (0,) * n)


def _mixer_prompt(x, wts, cap):
    B, L, _ = x.shape
    C = RET_CHUNK if L % RET_CHUNK == 0 else L
    n_c = L // C
    T = B * L
    dmask, xi, zeta, g_chunk = _decay_tables(C)
    xi_b = jnp.broadcast_to(xi[:, :, None], (RET_HEADS, C, RET_DV))
    zeta_b = jnp.broadcast_to(zeta[:, :, None], (RET_HEADS, C, RET_DK))
    gc_b = jnp.broadcast_to(g_chunk[:, None, None], (RET_HEADS, 1, RET_DV))
    cos2, sin2 = _rope_tables(jnp.arange(L).astype(_F32))

    NSEQ = MIXER_SEQS if B % MIXER_SEQS == 0 else 1
    B2 = B // NSEQ
    T2 = T // NSEQ
    x = x.reshape(NSEQ, B2, L, D_MODEL)
    tok = lambda b, c: (0, b * n_c + c, 0)
    in_specs = [
        pl.BlockSpec((NSEQ, None, C, D_MODEL), lambda b, c: (0, b, c, 0)),
        _full((1, D_MODEL)), _full((D_MODEL, IN_WIDTH)), _full((POOL_GROUPS, POOL_GC, POOL_OUT_GC)),
        _full((1, D_MODEL)), _full((1, D_MODEL)), _full((D_MODEL, D_MODEL)), _full((1, D_MODEL)),
        _full((D_MODEL, ROUTER_W)),
        pl.BlockSpec((C, RET_DK), lambda b, c: (c, 0)), pl.BlockSpec((C, RET_DK), lambda b, c: (c, 0)),
        _full((RET_HEADS, C, C)), _full((RET_HEADS, C, RET_DV)), _full((RET_HEADS, C, RET_DK)),
        _full((RET_HEADS, 1, RET_DV)),
    ]
    R = NSEQ * C
    out_shape = (
        jax.ShapeDtypeStruct((NSEQ, T2, D_MODEL), _F32),
        jax.ShapeDtypeStruct((NSEQ, T2, LANES), jnp.int32),
        jax.ShapeDtypeStruct((NSEQ, SUBLANES, T2), jnp.int32),
        jax.ShapeDtypeStruct((NSEQ, T2, LANES), _F32),
        jax.ShapeDtypeStruct((NSEQ, B2, POOL_HIST, POOL_IN), _F32),
        jax.ShapeDtypeStruct((NSEQ, B2, RET_HEADS, RET_DK, RET_DV), _F32),
        jax.ShapeDtypeStruct((1, LANES), _F32),
        jax.ShapeDtypeStruct((NSEQ, T2) + ROW_TILE, _F32),
    )
    out_specs = (
        pl.BlockSpec((NSEQ, C, D_MODEL), tok),
        pl.BlockSpec((NSEQ, C, LANES), tok),
        pl.BlockSpec((NSEQ, SUBLANES, C), lambda b, c: (0, 0, b * n_c + c)),
        pl.BlockSpec((NSEQ, C, LANES), tok),
        pl.BlockSpec((NSEQ, None, POOL_HIST, POOL_IN), lambda b, c: (0, b, 0, 0)),
        pl.BlockSpec((NSEQ, None, RET_HEADS, RET_DK, RET_DV), lambda b, c: (0, b, 0, 0, 0)),
        _full((1, LANES)),
        pl.BlockSpec((NSEQ, C) + ROW_TILE, lambda b, c: (0, b * n_c + c, 0, 0)),
    )
    x1, dst, dst_t, rw, npool, nret, counts, h2 = pl.pallas_call(
        functools.partial(_mixer_prompt_kernel, cap=cap),
        grid=(B2, n_c),
        in_specs=in_specs,
        out_specs=out_specs,
        out_shape=out_shape,
        scratch_shapes=[pltpu.VMEM((NSEQ, HALO + C, POOL_IN), _F32),
                        pltpu.VMEM((NSEQ, RET_HEADS, RET_DK, RET_DV), _F32),
                        pltpu.VMEM((R, D_MODEL), _BF),
                        pltpu.VMEM((1, LANES), _F32)],
        compiler_params=pltpu.CompilerParams(dimension_semantics=("arbitrary", "arbitrary"),
                                             vmem_limit_bytes=VMEM_LIMIT),
        name="mixer_prompt",
    )(x, wts["g_mix"], wts["w_in"], wts["w_pool"], wts["pool_scale"], wts["ret_gn"], wts["w_out"],
      wts["g_ffn"], wts["w_rt"], cos2, sin2, dmask, xi_b, zeta_b, gc_b)
    dst_rows = jnp.moveaxis(dst_t[:, 0:2, :], 1, 0).reshape(2, T)
    return (x1.reshape(T, D_MODEL), dst.reshape(T, LANES), dst_rows, rw.reshape(T, LANES),
            npool.reshape(B, POOL_HIST, POOL_IN), nret.reshape(B, RET_HEADS, RET_DK, RET_DV), counts,
            h2.reshape((T,) + ROW_TILE))


def _prep_weights(g_mix, w_in, w_pool, pool_scale, ret_gn, w_out, g_ffn, w_grp, w_exp):
    w_rt = jnp.concatenate([w_grp, w_exp.reshape(D_MODEL, N_EXPERTS)], axis=1)
    w_rt = jnp.pad(w_rt, ((0, 0), (0, ROUTER_W - w_rt.shape[1])))
    row = lambda v: v.reshape(1, D_MODEL)
    return dict(g_mix=row(g_mix), w_in=w_in.astype(_BF), w_pool=w_pool.astype(_BF), pool_scale=row(pool_scale),
                ret_gn=row(ret_gn), w_out=w_out.astype(_BF), g_ffn=row(g_ffn), w_rt=w_rt.astype(_BF))


SAMPLE_TB = 8


def _mixer_sample_kernel(x_ref, spool_ref, sret_ref, g_mix_ref, w_in_ref, w_pool_ref, pscale_ref, gn_ref,
                         w_out_ref, g_ffn_ref, w_rt_ref, cos_ref, sin_ref, dm_ref, xi_ref, zeta_ref, gc_ref,
                         carry_in_ref,
                         x1_ref, dst_ref, rw_ref, npool_ref, nret_ref, cnt_ref, h2_ref,
                         u_scr, q_scr, k_scr, qt_scr, kt_scr, v_scr, ga_scr, gb_scr, pooled_scr, o_scr, mixed_scr,
                         carry_scr, *, cap):
    t = pl.program_id(0)
    n_t = pl.num_programs(0)
    Bs = x_ref.shape[0]
    TB = sret_ref.shape[0]

    @pl.when(t == 0)
    def _():
        h = _rms(x_ref[...], g_mix_ref[...]).astype(_BF)
        u_scr[...] = _dot(h, w_in_ref[:, OFF_U:OFF_U + POOL_IN])
        q = _dot(h, w_in_ref[:, OFF_Q:OFF_Q + QK_W])
        k = _dot(h, w_in_ref[:, OFF_K:OFF_K + QK_W])
        for j in range(RET_HEADS):
            qs = slice(j * RET_DK, (j + 1) * RET_DK)
            qf = _rotary(q[:, qs], cos_ref[...], sin_ref[...])
            kf = _rotary(k[:, qs], cos_ref[...], sin_ref[...]) * (RET_DK ** -0.5)
            q_scr[:, qs] = qf
            k_scr[:, qs] = kf
            qt_scr[j] = qf.T
            kt_scr[j] = kf.T
        v_scr[...] = _dot(h, w_in_ref[:, OFF_V:OFF_V + V_W])
        ga_scr[...] = _dot(h, w_in_ref[:, OFF_GA:OFF_GA + D_MODEL])
        gb_scr[...] = _dot(h, w_in_ref[:, OFF_GB:OFF_GB + D_MODEL])

    shift = (Bs - t * TB) % Bs
    qt = [pltpu.roll(qt_scr[j], shift, 1) for j in range(RET_HEADS)]
    kt = [pltpu.roll(kt_scr[j], shift, 1) for j in range(RET_HEADS)]

    blk = pl.ds(pl.multiple_of(t * TB, TB), TB)
    u_blk = u_scr[blk, :]
    q_blk = q_scr[blk, :]
    k_blk = k_scr[blk, :]
    v_blk = v_scr[blk, :]
    score = [jnp.sum(q_blk[:, j * RET_DK:(j + 1) * RET_DK] * k_blk[:, j * RET_DK:(j + 1) * RET_DK],
                     axis=-1, keepdims=True) * dm_ref[j] for j in range(RET_HEADS)]

    pooled_rows, o_rows = [], []
    for i in range(TB):
        u_row = u_blk[i:i + 1, :]
        groups = []
        for g, win in enumerate(POOL_WINDOWS):
            cs = slice(g * POOL_GC, (g + 1) * POOL_GC)
            hist_sum = jnp.sum(spool_ref[i, POOL_HIST - (win - 1):POOL_HIST, cs], axis=0, keepdims=True)
            groups.append((u_row[:, cs] + hist_sum) / jnp.float32(win) - u_row[:, cs])
        pooled_rows.append(jnp.concatenate(groups, axis=1))
        npool_ref[i, 0:POOL_HIST - 1, :] = spool_ref[i, 1:POOL_HIST, :]
        npool_ref[i, POOL_HIST - 1:POOL_HIST, :] = u_row

        heads = []
        for j in range(RET_HEADS):
            s_old = sret_ref[i, j]
            v_row = v_blk[i:i + 1, j * RET_DV:(j + 1) * RET_DV]
            qcol = qt[j][:, i:i + 1]
            kcol = kt[j][:, i:i + 1]
            qs_old = jnp.sum(qcol * s_old, axis=0, keepdims=True)
            heads.append(score[j][i:i + 1, :] * v_row + qs_old * xi_ref[j])
            nret_ref[i, j] = gc_ref[j] * s_old + (kcol * zeta_ref[j]) * v_row
        o_rows.append(jnp.concatenate(heads, axis=1))
    pooled_scr[blk, :] = jnp.concatenate(pooled_rows, axis=0)
    o_scr[blk, :] = jnp.concatenate(o_rows, axis=0)

    @pl.when(t == n_t - 1)
    def _():
        for j in range(RET_HEADS):
            cs = slice(j * POOL_GC, (j + 1) * POOL_GC)
            osl = slice(j * POOL_OUT_GC, (j + 1) * POOL_OUT_GC)
            pool_out = _dot(pooled_scr[:, cs].astype(_BF), w_pool_ref[j]) * pscale_ref[:, osl]
            ret_out = _group_norm(o_scr[:, osl], gn_ref[:, osl])
            mixed_scr[:, osl] = (jax.nn.sigmoid(ga_scr[:, osl]) * pool_out
                                 + jax.nn.sigmoid(gb_scr[:, osl]) * ret_out).astype(_BF)
        h2, e0, e1 = _post_mix(x_ref[...], mixed_scr, w_out_ref, g_ffn_ref, w_rt_ref, x1_ref, rw_ref)
        h2_ref[...] = h2.reshape(h2_ref.shape)
        carry_scr[...] = carry_in_ref[...]
        dst_ref[...] = _rank_rows(e0, e1, cap, carry_scr).astype(jnp.int32)
        cnt_ref[...] = carry_scr[...]


def _mixer_sample(x, state_pool, state_ret, wts, pos0, carry_in, cap):
    Bs = x.shape[0]
    assert Bs == LANES and Bs % SAMPLE_TB == 0
    TB = SAMPLE_TB
    dmask, xi, zeta, g_chunk = _decay_tables(1)
    dm_b = jnp.broadcast_to(dmask, (RET_HEADS, 1, 1))
    xi_b = jnp.broadcast_to(xi[:, :, None], (RET_HEADS, 1, RET_DV))
    zeta_b = jnp.broadcast_to(zeta[:, :, None], (RET_HEADS, 1, 1))
    gc_b = jnp.broadcast_to(g_chunk[:, None, None], (RET_HEADS, 1, RET_DV))
    cos2, sin2 = _rope_tables((pos0 + jnp.arange(1)).astype(_F32))

    in_specs = [
        _full((Bs, D_MODEL)),
        pl.BlockSpec((TB, POOL_HIST, POOL_IN), lambda t: (t, 0, 0)),
        pl.BlockSpec((TB, RET_HEADS, RET_DK, RET_DV), lambda t: (t, 0, 0, 0)),
        _full((1, D_MODEL)), _full((D_MODEL, IN_WIDTH)), _full((POOL_GROUPS, POOL_GC, POOL_OUT_GC)),
        _full((1, D_MODEL)), _full((1, D_MODEL)), _full((D_MODEL, D_MODEL)), _full((1, D_MODEL)),
        _full((D_MODEL, ROUTER_W)),
        _full((1, RET_DK)), _full((1, RET_DK)),
        _full((RET_HEADS, 1, 1)), _full((RET_HEADS, 1, RET_DV)), _full((RET_HEADS, 1, 1)),
        _full((RET_HEADS, 1, RET_DV)),
        _full((1, LANES)),
    ]
    out_shape = (
        jax.ShapeDtypeStruct((Bs, D_MODEL), _F32),
        jax.ShapeDtypeStruct((Bs, LANES), jnp.int32),
        jax.ShapeDtypeStruct((Bs, LANES), _F32),
        jax.ShapeDtypeStruct((Bs, POOL_HIST, POOL_IN), _F32),
        jax.ShapeDtypeStruct((Bs, RET_HEADS, RET_DK, RET_DV), _F32),
        jax.ShapeDtypeStruct((1, LANES), _F32),
        jax.ShapeDtypeStruct((Bs,) + ROW_TILE, _F32),
    )
    out_specs = (
        _full((Bs, D_MODEL)), _full((Bs, LANES)), _full((Bs, LANES)),
        pl.BlockSpec((TB, POOL_HIST, POOL_IN), lambda t: (t, 0, 0)),
        pl.BlockSpec((TB, RET_HEADS, RET_DK, RET_DV), lambda t: (t, 0, 0, 0)),
        _full((1, LANES)), _full((Bs,) + ROW_TILE),
    )
    f32 = lambda *s: pltpu.VMEM(s, _F32)
    return pl.pallas_call(
        functools.partial(_mixer_sample_kernel, cap=cap),
        grid=(Bs // TB,),
        in_specs=in_specs,
        out_specs=out_specs,
        out_shape=out_shape,
        scratch_shapes=[f32(Bs, POOL_IN), f32(Bs, QK_W), f32(Bs, QK_W), f32(RET_HEADS, RET_DK, Bs),
                        f32(RET_HEADS, RET_DK, Bs), f32(Bs, V_W), f32(Bs, D_MODEL), f32(Bs, D_MODEL),
                        f32(Bs, POOL_IN), f32(Bs, V_W), pltpu.VMEM((Bs, D_MODEL), _BF), f32(1, LANES)],
        compiler_params=pltpu.CompilerParams(dimension_semantics=("arbitrary",), vmem_limit_bytes=VMEM_LIMIT),
        name="mixer_sample",
    )(x, state_pool, state_ret, wts["g_mix"], wts["w_in"], wts["w_pool"], wts["pool_scale"], wts["ret_gn"],
      wts["w_out"], wts["g_ffn"], wts["w_rt"], cos2, sin2, dm_b, xi_b, zeta_b, gc_b, carry_in)


IDX_CHUNK = 1024
ISSUE_UNROLL = 8
SC_CORES = 2
SC_SUBCORES = 16
SC_WINDOW = 32


def _sc_scatter(h2, dst_rows, n_rows):
    T = h2.shape[0]
    n_workers = SC_CORES * SC_SUBCORES
    W = SC_WINDOW
    assert T % (n_workers * W) == 0
    per_w = T // n_workers
    n_win = per_w // W
    d0 = dst_rows[0].reshape(T // W, W)
    d1 = dst_rows[1].reshape(T // W, W)
    mesh = plsc.VectorSubcoreMesh(core_axis_name="c", subcore_axis_name="s")

    def body(h2_hbm, d0_hbm, d1_hbm, xs_hbm, i0_v, i1_v, rows_v, sem_load, sem_store):
        wid = lax.axis_index("s") * SC_CORES + lax.axis_index("c")
        base = wid * per_w
        pltpu.sync_copy(d0_hbm.at[pl.ds(wid * n_win, n_win)], i0_v)
        pltpu.sync_copy(d1_hbm.at[pl.ds(wid * n_win, n_win)], i1_v)
        pltpu.async_copy(h2_hbm.at[pl.ds(base, W)], rows_v.at[0], sem_load)
        for i in range(n_win):
            b = i % 2
            pltpu.make_async_copy(h2_hbm.at[pl.ds(base, W)], rows_v.at[b], sem_load).wait()
            if i + 1 < n_win:
                pltpu.async_copy(h2_hbm.at[pl.ds(base + (i + 1) * W, W)], rows_v.at[1 - b], sem_load)
            c0 = pltpu.async_copy(rows_v.at[b], xs_hbm.at[i0_v.at[i]], sem_store)
            c1 = pltpu.async_copy(rows_v.at[b], xs_hbm.at[i1_v.at[i]], sem_store)
            c0.wait()
            c1.wait()

    return pl.kernel(
        body, mesh=mesh,
        out_type=jax.ShapeDtypeStruct((n_rows,) + ROW_TILE, _F32),
        scratch_types=[pltpu.VMEM((n_win, W), jnp.int32), pltpu.VMEM((n_win, W), jnp.int32),
                       pltpu.VMEM((2, W) + ROW_TILE, _F32),
                       pltpu.SemaphoreType.DMA, pltpu.SemaphoreType.DMA],
        name="moe_sc_scatter",
    )(h2, d0, d1)


def _finish_dispatch_kernel(zstart_ref, slots_hbm, h2_ref, xs_in_hbm, xs_hbm, idx_smem, zbuf, sem_idx, sem_rows,
                            sem_zero):
    del xs_in_hbm
    TS = h2_ref.shape[0]
    idx_cp = pltpu.make_async_copy(slots_hbm.at[pl.ds(0, IDX_CHUNK)], idx_smem, sem_idx)
    idx_cp.start()
    zbuf[...] = jnp.zeros(zbuf.shape, zbuf.dtype)
    for e in range(N_EXPERTS):
        pltpu.make_async_copy(zbuf, xs_hbm.at[pl.ds(zstart_ref[e], MOE_BLOCK)], sem_zero).start()
    idx_cp.wait()

    def issue(r, carry):
        for kk in range(2):
            pltpu.make_async_copy(h2_ref.at[r], xs_hbm.at[idx_smem[2 * r + kk]], sem_rows).start(priority=kk)
        return carry

    lax.fori_loop(0, TS, issue, 0, unroll=ISSUE_UNROLL)
    for e in range(N_EXPERTS):
        pltpu.make_async_copy(zbuf, xs_hbm.at[pl.ds(zstart_ref[e], MOE_BLOCK)], sem_zero).wait()
    for kk in range(2):
        pltpu.make_async_copy(h2_ref, xs_hbm.at[pl.ds(0, TS)], sem_rows).wait()


def _finish_dispatch(xs, h2s, slots, zstart):
    Bs = h2s.shape[0]
    assert 2 * Bs <= IDX_CHUNK and slots.shape[0] == IDX_CHUNK
    any_spec = pl.BlockSpec(memory_space=pl.ANY)
    grid_spec = pltpu.PrefetchScalarGridSpec(
        num_scalar_prefetch=1,
        grid=(1,),
        in_specs=[any_spec, pl.BlockSpec((Bs,) + ROW_TILE, lambda t, z: (0, 0, 0)), any_spec],
        out_specs=any_spec,
        scratch_shapes=[pltpu.SMEM((IDX_CHUNK,), jnp.int32), pltpu.VMEM((MOE_BLOCK,) + ROW_TILE, _F32),
                        pltpu.SemaphoreType.DMA, pltpu.SemaphoreType.DMA, pltpu.SemaphoreType.DMA],
    )
    return pl.pallas_call(
        _finish_dispatch_kernel,
        grid_spec=grid_spec,
        out_shape=jax.ShapeDtypeStruct(xs.shape, _F32),
        input_output_aliases={3: 0},
        compiler_params=pltpu.CompilerParams(dimension_semantics=("arbitrary",), has_side_effects=True),
        name="moe_finish_dispatch",
    )(zstart, slots, h2s, xs)


def _ffn_kernel(bexp_ref, brow_ref, nused_ref, first_ref, slot_ref, next_ref,
                xs_ref, w1_hbm, w3_hbm, w2_hbm, ys_ref, w13_scr, w2_scr, st1, st3, st2, sem_w):
    del brow_ref
    i = pl.program_id(0)

    def weight_copies(e, s):
        return (pltpu.make_async_copy(w1_hbm.at[e], st1.at[s], sem_w.at[s]),
                pltpu.make_async_copy(w3_hbm.at[e], st3.at[s], sem_w.at[s]),
                pltpu.make_async_copy(w2_hbm.at[e], st2.at[s], sem_w.at[s]))

    @pl.when(i < nused_ref[0])
    def _():
        @pl.when(first_ref[i] == 1)
        def _():
            e = bexp_ref[i]
            s = slot_ref[i]

            @pl.when(i == 0)
            def _():
                for cp in weight_copies(e, s):
                    cp.start()

            for cp in weight_copies(e, s):
                cp.wait()
            @pl.when(next_ref[i] >= 0)
            def _():
                for cp in weight_copies(next_ref[i], 1 - s):
                    cp.start()

            w13_scr[:, 0:D_EXPERT] = st1[s].astype(_BF)
            w13_scr[:, D_EXPERT:2 * D_EXPERT] = st3[s].astype(_BF)
            w2_scr[...] = st2[s].astype(_BF)

        xb = xs_ref[...].reshape(MOE_BLOCK, D_MODEL).astype(_BF)
        ab = _dot(xb, w13_scr[...])
        hid = jax.nn.silu(ab[:, 0:D_EXPERT]) * ab[:, D_EXPERT:2 * D_EXPERT]
        ys_ref[...] = _dot(hid.astype(_BF), w2_scr[...]).reshape(ys_ref.shape)


def _ffn(xs, block_exp, block_row, n_used, run_first, run_slot, run_next, w1, w3, w2, n_blocks):
    blk = lambda i, *_: (_[1][i], 0, 0)
    any_spec = pl.BlockSpec(memory_space=pl.ANY)
    grid_spec = pltpu.PrefetchScalarGridSpec(
        num_scalar_prefetch=6,
        grid=(n_blocks,),
        in_specs=[pl.BlockSpec((MOE_BLOCK,) + ROW_TILE, blk), any_spec, any_spec, any_spec],
        out_specs=pl.BlockSpec((MOE_BLOCK,) + ROW_TILE, blk),
        scratch_shapes=[pltpu.VMEM((D_MODEL, 2 * D_EXPERT), _BF), pltpu.VMEM((D_EXPERT, D_MODEL), _BF),
                        pltpu.VMEM((2, D_MODEL, D_EXPERT), _F32), pltpu.VMEM((2, D_MODEL, D_EXPERT), _F32),
                        pltpu.VMEM((2, D_EXPERT, D_MODEL), _F32), pltpu.SemaphoreType.DMA((2,))],
    )
    return pl.pallas_call(
        _ffn_kernel,
        grid_spec=grid_spec,
        out_shape=jax.ShapeDtypeStruct(xs.shape, _F32),
        compiler_params=pltpu.CompilerParams(dimension_semantics=("arbitrary",), vmem_limit_bytes=VMEM_LIMIT),
        name="moe_ffn",
    )(block_exp, block_row, n_used, run_first, run_slot, run_next, xs, w1, w3, w2)


def _combine_kernel(slots_hbm, x1_ref, rw_ref, g_ref, ys_hbm, y_ref, idx_smem, buf, sem_idx, sem_rows):
    t = pl.program_id(0)
    TC = x1_ref.shape[0]
    idx_cp = pltpu.make_async_copy(slots_hbm.at[pl.ds(t * IDX_CHUNK, IDX_CHUNK)], idx_smem, sem_idx)
    idx_cp.start()
    idx_cp.wait()

    def issue(r, carry):
        for kk in range(2):
            pltpu.make_async_copy(ys_hbm.at[idx_smem[2 * r + kk]], buf.at[kk, r], sem_rows).start(priority=kk)
        return carry

    lax.fori_loop(0, TC, issue, 0, unroll=ISSUE_UNROLL)
    for kk in range(2):
        pltpu.make_async_copy(ys_hbm.at[pl.ds(0, TC)], buf.at[kk], sem_rows).wait()

    w = rw_ref[...]
    y0 = buf[0].reshape(TC, D_MODEL)
    y1 = buf[1].reshape(TC, D_MODEL)
    x2 = x1_ref[...] + (w[:, 0:1] * y0 + w[:, 1:2] * y1)
    y_ref[...] = _rms(x2, g_ref[...])


def _combine(x1, rw, slots, ys, g_final, tile):
    T = x1.shape[0]
    assert T % tile == 0 and 2 * tile <= IDX_CHUNK and slots.shape[0] == (T // tile) * IDX_CHUNK
    any_spec = pl.BlockSpec(memory_space=pl.ANY)
    return pl.pallas_call(
        _combine_kernel,
        grid=(T // tile,),
        in_specs=[any_spec, pl.BlockSpec((tile, D_MODEL), lambda t: (t, 0)),
                  pl.BlockSpec((tile, LANES), lambda t: (t, 0)), _full((1, D_MODEL)), any_spec],
        out_specs=pl.BlockSpec((tile, D_MODEL), lambda t: (t, 0)),
        out_shape=jax.ShapeDtypeStruct((T, D_MODEL), _F32),
        scratch_shapes=[pltpu.SMEM((IDX_CHUNK,), jnp.int32), pltpu.VMEM((2, tile) + ROW_TILE, _F32),
                        pltpu.SemaphoreType.DMA, pltpu.SemaphoreType.DMA],
        compiler_params=pltpu.CompilerParams(dimension_semantics=("arbitrary",), vmem_limit_bytes=VMEM_LIMIT),
        name="moe_combine",
    )(slots, x1, rw, g_final.reshape(1, D_MODEL), ys)


def _tile_for(n_tokens):
    tile = IDX_CHUNK // 2
    return tile if n_tokens % tile == 0 else n_tokens


def _chunked_slots(slot, tile):
    n_tiles = slot.shape[0] // tile
    s = slot.reshape(n_tiles, 2 * tile)
    return jnp.pad(s, ((0, 0), (0, IDX_CHUNK - 2 * tile))).reshape(-1)


def kernel(x_prompt, x_sample, state_pool, state_ret, g_mix, w_in, w_pool, pool_scale, ret_gn, w_out, g_ffn, w_grp, w_exp, w1, w3, w2, g_final):
    Bp, Lp, _ = x_prompt.shape
    Bs = x_sample.shape[0]
    Tp = Bp * Lp
    wts = _prep_weights(g_mix[0], w_in[0], w_pool[0], pool_scale[0], ret_gn[0], w_out[0], g_ffn[0], w_grp[0], w_exp[0])

    T_all = Tp + Bs
    cap = (-(-T_all // MOE_BLOCK) + 1) * MOE_BLOCK
    cap_blocks = cap // MOE_BLOCK

    x1p, dst_p, dst_rows_p, rwp, npool_p, nret_p, counts_p, h2p = _mixer_prompt(x_prompt, wts, cap)
    xs = _sc_scatter(h2p, dst_rows_p, N_EXPERTS * cap)
    x1s, dst_s, rws, npool_s, nret_s, counts, h2s = _mixer_sample(
        x_sample.reshape(Bs, D_MODEL), state_pool[0], state_ret[0], wts, PAST_LEN, counts_p, cap)

    counts = counts[0, :N_EXPERTS].astype(jnp.int32)
    tile_p, tile_s = _tile_for(Tp), _tile_for(Bs)
    slots_p = _chunked_slots(dst_p[:, :2], tile_p)
    slots_s = _chunked_slots(dst_s[:, :2], tile_s)
    zstart = jnp.arange(N_EXPERTS, dtype=jnp.int32) * cap + counts
    xs = _finish_dispatch(xs, h2s, slots_s, zstart)
    nblk = (counts + MOE_BLOCK - 1) // MOE_BLOCK
    blk_end = jnp.cumsum(nblk)
    n_used = blk_end[-1:].astype(jnp.int32)
    n_blocks = -(-2 * T_all // MOE_BLOCK) + N_EXPERTS
    step = jnp.minimum(jnp.arange(n_blocks, dtype=jnp.int32), n_used - 1)
    block_exp = jnp.minimum(jnp.sum((blk_end[None, :] <= step[:, None]).astype(jnp.int32), axis=1), N_EXPERTS - 1)
    blk_start = jnp.sum(jnp.where(block_exp[:, None] == jnp.arange(N_EXPERTS, dtype=jnp.int32), (blk_end - nblk)[None, :], 0),
                        axis=1)
    block_row = (block_exp * cap_blocks + (step - blk_start)).astype(jnp.int32)
    block_exp = block_exp.astype(jnp.int32)
    experts = jnp.arange(N_EXPERTS, dtype=jnp.int32)
    nonempty = nblk > 0
    run_first = (step == blk_start).astype(jnp.int32)
    run_slot = (jnp.sum((nonempty[None, :] & (experts[None, :] < block_exp[:, None])).astype(jnp.int32), axis=1) % 2)
    later = nonempty[None, :] & (experts[None, :] > block_exp[:, None])
    run_next = jnp.min(jnp.where(later, experts[None, :], N_EXPERTS), axis=1)
    run_next = jnp.where(run_next == N_EXPERTS, -1, run_next).astype(jnp.int32)

    ys = _ffn(xs, block_exp, block_row, n_used, run_first, run_slot.astype(jnp.int32), run_next,
              w1[0], w3[0], w2[0], n_blocks)
    y_p = _combine(x1p, rwp, slots_p, ys, g_final, tile_p)
    y_s = _combine(x1s, rws, slots_s, ys, g_final, tile_s)

    return (y_p.reshape(Bp, Lp, D_MODEL), y_s.reshape(Bs, 1, D_MODEL),
            npool_p[None], nret_p[None], npool_s[None], nret_s[None])
```

```python
import functools

import jax
import jax.numpy as jnp
from jax import lax
from jax.experimental import pallas as pl
from jax.experimental.pallas import tpu as pltpu
from jax.experimental.pallas import tpu_sc as plsc

D_MODEL = 1024
EPS = 1e-6
POOL_GROUPS = 4
POOL_IN = D_MODEL // 2
POOL_GC = POOL_IN // POOL_GROUPS
POOL_OUT_GC = D_MODEL // POOL_GROUPS
POOL_WINDOWS = (2, 4, 8, 16)
POOL_HIST = max(POOL_WINDOWS) - 1
RET_HEADS = 4
RET_DK = D_MODEL // 8
RET_DV = D_MODEL // RET_HEADS
ROPE_BASE = 10000.0
PAST_LEN = 16384
N_GROUPS = 4
EXPERTS_PER_GROUP = 8
N_EXPERTS = N_GROUPS * EXPERTS_PER_GROUP
D_EXPERT = D_MODEL // 4
QK_W = RET_HEADS * RET_DK
V_W = RET_HEADS * RET_DV
OFF_U = 0
OFF_Q = POOL_IN
OFF_K = OFF_Q + QK_W
OFF_V = OFF_K + QK_W
OFF_GA = OFF_V + V_W
OFF_GB = OFF_GA + D_MODEL
IN_WIDTH = OFF_GB + D_MODEL

LANES = 128
SUBLANES = 8
ROW_TILE = (SUBLANES, LANES)
assert SUBLANES * LANES == D_MODEL
HALO = 16
RET_CHUNK = 256
MIXER_SEQS = 2
MOE_BLOCK = 512
ROUTER_W = LANES
VMEM_LIMIT = 56 * 1024 * 1024

_BF = jnp.bfloat16
_F32 = jnp.float32


def _rms(x, g):
    inv = lax.rsqrt(jnp.mean(x * x, axis=-1, keepdims=True) + EPS)
    return x * inv * g


def _dot(a, b):
    return jnp.dot(a, b, preferred_element_type=_F32)


def _rotary(x, cos2, sin2):
    return x * cos2 + pltpu.roll(x, RET_DK // 2, 1) * sin2


def _route(logits):
    lane = lax.broadcasted_iota(jnp.int32, logits.shape, 1).astype(_F32)
    neg = jnp.float32(-jnp.inf)
    big = jnp.float32(1 << 20)
    lg = jnp.where(lane < N_GROUPS, logits, neg)
    mg = jnp.max(lg, axis=-1, keepdims=True)
    g_idx = jnp.min(jnp.where(lg == mg, lane, big), axis=-1, keepdims=True)
    p_g = 1.0 / jnp.sum(jnp.exp(lg - mg), axis=-1, keepdims=True)
    lo = N_GROUPS + g_idx * EXPERTS_PER_GROUP
    in_grp = (lane >= lo) & (lane < lo + EXPERTS_PER_GROUP)
    le = jnp.where(in_grp, logits, neg)
    m1 = jnp.max(le, axis=-1, keepdims=True)
    i1 = jnp.min(jnp.where(le == m1, lane, big), axis=-1, keepdims=True)
    le2 = jnp.where(lane == i1, neg, le)
    m2 = jnp.max(le2, axis=-1, keepdims=True)
    i2 = jnp.min(jnp.where(le2 == m2, lane, big), axis=-1, keepdims=True)
    t = jnp.exp(m2 - m1)
    den = 1.0 + t
    e0 = (i1 - N_GROUPS).astype(jnp.int32)
    e1 = (i2 - N_GROUPS).astype(jnp.int32)
    return e0, e1, p_g * (1.0 / den), p_g * (t / den)


def _post_mix(x, mixed_ref, w_out_ref, g_ffn_ref, w_rt_ref, x1_ref, rw_ref):
    x1 = x + _dot(mixed_ref[...], w_out_ref[...])
    x1_ref[...] = x1.reshape(x1_ref.shape)
    h2 = _rms(x1, g_ffn_ref[...])
    e0, e1, w0, w1 = _route(_dot(h2.astype(_BF), w_rt_ref[...]))
    lane = lax.broadcasted_iota(jnp.int32, (x.shape[0], LANES), 1)
    rw_ref[...] = jnp.where(lane == 0, w0, jnp.where(lane == 1, w1, 0.0)).reshape(rw_ref.shape)
    return h2, e0, e1


def _rank_rows(e0, e1, cap, carry_scr):
    R = e0.shape[0]
    lane = lax.broadcasted_iota(jnp.int32, (R, LANES), 1)
    m0 = lane == e0
    m1 = lane == e1
    onehot = jnp.where(m0 | m1, 1.0, 0.0)
    r_i = lax.broadcasted_iota(jnp.int32, (R, R), 0)
    c_i = lax.broadcasted_iota(jnp.int32, (R, R), 1)
    tri = jnp.where(c_i < r_i, 1.0, 0.0).astype(_BF)
    before = _dot(tri, onehot.astype(_BF)) + carry_scr[...]
    d0 = e0.astype(_F32) * cap + jnp.sum(jnp.where(m0, before, 0.0), axis=-1, keepdims=True)
    d1 = e1.astype(_F32) * cap + jnp.sum(jnp.where(m1, before, 0.0), axis=-1, keepdims=True)
    carry_scr[...] += jnp.sum(onehot, axis=0, keepdims=True)
    return jnp.where(lane == 0, d0, jnp.where(lane == 1, d1, 0.0))


def _group_norm(o, gain):
    mu = jnp.mean(o, axis=-1, keepdims=True)
    d = o - mu
    var = jnp.mean(d * d, axis=-1, keepdims=True)
    return d * lax.rsqrt(var + EPS) * gain


def _mixer_prompt_kernel(x_ref, g_mix_ref, w_in_ref, w_pool_ref, pscale_ref, gn_ref, w_out_ref, g_ffn_ref,
                         w_rt_ref, cos_ref, sin_ref, dmask_ref, xi_ref, zeta_ref, gc_ref,
                         x1_ref, dst_ref, dst_t_ref, rw_ref, npool_ref, nret_ref, cnt_ref, h2_ref,
                         u_scr, s_scr, mixed_scr, carry_scr, *, cap):
    c = pl.program_id(1)
    n_c = pl.num_programs(1)
    @pl.when((pl.program_id(0) == 0) & (c == 0))
    def _():
        carry_scr[...] = jnp.zeros(carry_scr.shape, _F32)
    NSEQ, C, _ = x_ref.shape
    R = NSEQ * C

    @pl.when(c == 0)
    def _():
        u_scr[:, 0:HALO, :] = jnp.zeros((NSEQ, HALO, POOL_IN), _F32)
        s_scr[...] = jnp.zeros(s_scr.shape, _F32)

    x = x_ref[...].reshape(R, D_MODEL)
    h = _rms(x, g_mix_ref[...]).astype(_BF)

    u_scr[:, HALO:HALO + C, :] = _dot(h, w_in_ref[:, OFF_U:OFF_U + POOL_IN]).reshape(NSEQ, C, POOL_IN)
    q = _dot(h, w_in_ref[:, OFF_Q:OFF_Q + QK_W])
    k = _dot(h, w_in_ref[:, OFF_K:OFF_K + QK_W])
    cos2 = jnp.concatenate([cos_ref[...]] * NSEQ, axis=0)
    sin2 = jnp.concatenate([sin_ref[...]] * NSEQ, axis=0)
    pos1 = (c * C + 1 + lax.broadcasted_iota(jnp.int32, (C, POOL_GC), 0)).astype(_F32)

    for j in range(RET_HEADS):
        win = POOL_WINDOWS[j]
        cs = slice(j * POOL_GC, (j + 1) * POOL_GC)
        n_rows = jnp.minimum(pos1, jnp.float32(win))
        pooled = []
        for sq in range(NSEQ):
            u_j = u_scr[sq, HALO:HALO + C, cs]
            s = u_j
            for d in range(1, win):
                s = s + u_scr[sq, HALO - d:HALO - d + C, cs]
            pooled.append(s / n_rows - u_j)
        osl = slice(j * POOL_OUT_GC, (j + 1) * POOL_OUT_GC)
        pool_out = _dot(jnp.concatenate(pooled, axis=0).astype(_BF), w_pool_ref[j]) * pscale_ref[:, osl]

        qs = slice(j * RET_DK, (j + 1) * RET_DK)
        qb = _rotary(q[:, qs], cos2, sin2).astype(_BF)
        kf = _rotary(k[:, qs], cos2, sin2) * (RET_DK ** -0.5)
        kb = kf.astype(_BF)
        vb = _dot(h, w_in_ref[:, OFF_V + j * RET_DV:OFF_V + (j + 1) * RET_DV]).astype(_BF)
        ret = []
        for sq in range(NSEQ):
            rows = slice(sq * C, (sq + 1) * C)
            scores = lax.dot_general(qb[rows], kb[rows], (((1,), (1,)), ((), ())),
                                     preferred_element_type=_F32) * dmask_ref[j]
            s_old = s_scr[sq, j]
            o = _dot(scores.astype(_BF), vb[rows]) + _dot(qb[rows], s_old.astype(_BF)) * xi_ref[j]
            kz = (kf[rows] * zeta_ref[j]).astype(_BF)
            s_scr[sq, j] = gc_ref[j] * s_old + lax.dot_general(kz, vb[rows], (((0,), (0,)), ((), ())),
                                                               preferred_element_type=_F32)
            ret.append(_group_norm(o, gn_ref[:, osl]))
        ret_out = jnp.concatenate(ret, axis=0)

        ga = _dot(h, w_in_ref[:, OFF_GA + j * RET_DV:OFF_GA + (j + 1) * RET_DV])
        gb = _dot(h, w_in_ref[:, OFF_GB + j * RET_DV:OFF_GB + (j + 1) * RET_DV])
        mixed_scr[:, osl] = (jax.nn.sigmoid(ga) * pool_out + jax.nn.sigmoid(gb) * ret_out).astype(_BF)

    h2, e0, e1 = _post_mix(x, mixed_scr, w_out_ref, g_ffn_ref, w_rt_ref, x1_ref, rw_ref)

    h2_ref[...] = h2.reshape(h2_ref.shape)
    dst = _rank_rows(e0, e1, cap, carry_scr)
    dst_ref[...] = dst.astype(jnp.int32).reshape(dst_ref.shape)
    dst_t = dst.T[0:SUBLANES, :].astype(jnp.int32)
    for sq in range(NSEQ):
        dst_t_ref[sq] = dst_t[:, sq * C:(sq + 1) * C]
    cnt_ref[...] = carry_scr[...]

    u_scr[:, 0:HALO, :] = u_scr[:, C:C + HALO, :]

    @pl.when(c == n_c - 1)
    def _():
        npool_ref[...] = u_scr[:, HALO + C - POOL_HIST:HALO + C, :]
        nret_ref[...] = s_scr[...]


def _decay_tables(C):
    log_g = jnp.log(1.0 - jnp.exp2(-5.0 - jnp.arange(RET_HEADS, dtype=_F32)))
    i = jnp.arange(C, dtype=_F32)
    diff = i[:, None] - i[None, :]
    dmask = jnp.where(diff >= 0, jnp.exp(jnp.maximum(diff, 0.0)[None] * log_g[:, None, None]), 0.0)
    xi = jnp.exp((i[None, :] + 1.0) * log_g[:, None])
    zeta = jnp.exp((C - 1.0 - i)[None, :] * log_g[:, None])
    g_chunk = jnp.exp(C * log_g)
    return dmask, xi, zeta, g_chunk


def _rope_tables(pos):
    half = RET_DK // 2
    freqs = ROPE_BASE ** (-jnp.arange(half, dtype=_F32) / half)
    ang = pos[:, None] * freqs[None, :]
    cos, sin = jnp.cos(ang), jnp.sin(ang)
    return jnp.concatenate([cos, cos], axis=-1), jnp.concatenate([-sin, sin], axis=-1)


def _full(shape):
    n = len(shape)
    return pl.BlockSpec(shape, lambda *_: (0,) * n)


def _mixer_prompt(x, wts, cap):
    B, L, _ = x.shape
    C = RET_CHUNK if L % RET_CHUNK == 0 else L
    n_c = L // C
    T = B * L
    dmask, xi, zeta, g_chunk = _decay_tables(C)
    xi_b = jnp.broadcast_to(xi[:, :, None], (RET_HEADS, C, RET_DV))
    zeta_b = jnp.broadcast_to(zeta[:, :, None], (RET_HEADS, C, RET_DK))
    gc_b = jnp.broadcast_to(g_chunk[:, None, None], (RET_HEADS, 1, RET_DV))
    cos2, sin2 = _rope_tables(jnp.arange(L).astype(_F32))

    NSEQ = MIXER_SEQS if B % MIXER_SEQS == 0 else 1
    B2 = B // NSEQ
    T2 = T // NSEQ
    x = x.reshape(NSEQ, B2, L, D_MODEL)
    tok = lambda b, c: (0, b * n_c + c, 0)
    in_specs = [
        pl.BlockSpec((NSEQ, None, C, D_MODEL), lambda b, c: (0, b, c, 0)),
        _full((1, D_MODEL)), _full((D_MODEL, IN_WIDTH)), _full((POOL_GROUPS, POOL_GC, POOL_OUT_GC)),
        _full((1, D_MODEL)), _full((1, D_MODEL)), _full((D_MODEL, D_MODEL)), _full((1, D_MODEL)),
        _full((D_MODEL, ROUTER_W)),
        pl.BlockSpec((C, RET_DK), lambda b, c: (c, 0)), pl.BlockSpec((C, RET_DK), lambda b, c: (c, 0)),
        _full((RET_HEADS, C, C)), _full((RET_HEADS, C, RET_DV)), _full((RET_HEADS, C, RET_DK)),
        _full((RET_HEADS, 1, RET_DV)),
    ]
    R = NSEQ * C
    out_shape = (
        jax.ShapeDtypeStruct((NSEQ, T2, D_MODEL), _F32),
        jax.ShapeDtypeStruct((NSEQ, T2, LANES), jnp.int32),
        jax.ShapeDtypeStruct((NSEQ, SUBLANES, T2), jnp.int32),
        jax.ShapeDtypeStruct((NSEQ, T2, LANES), _F32),
        jax.ShapeDtypeStruct((NSEQ, B2, POOL_HIST, POOL_IN), _F32),
        jax.ShapeDtypeStruct((NSEQ, B2, RET_HEADS, RET_DK, RET_DV), _F32),
        jax.ShapeDtypeStruct((1, LANES), _F32),
        jax.ShapeDtypeStruct((NSEQ, T2) + ROW_TILE, _F32),
    )
    out_specs = (
        pl.BlockSpec((NSEQ, C, D_MODEL), tok),
        pl.BlockSpec((NSEQ, C, LANES), tok),
        pl.BlockSpec((NSEQ, SUBLANES, C), lambda b, c: (0, 0, b * n_c + c)),
        pl.BlockSpec((NSEQ, C, LANES), tok),
        pl.BlockSpec((NSEQ, None, POOL_HIST, POOL_IN), lambda b, c: (0, b, 0, 0)),
        pl.BlockSpec((NSEQ, None, RET_HEADS, RET_DK, RET_DV), lambda b, c: (0, b, 0, 0, 0)),
        _full((1, LANES)),
        pl.BlockSpec((NSEQ, C) + ROW_TILE, lambda b, c: (0, b * n_c + c, 0, 0)),
    )
    x1, dst, dst_t, rw, npool, nret, counts, h2 = pl.pallas_call(
        functools.partial(_mixer_prompt_kernel, cap=cap),
        grid=(B2, n_c),
        in_specs=in_specs,
        out_specs=out_specs,
        out_shape=out_shape,
        scratch_shapes=[pltpu.VMEM((NSEQ, HALO + C, POOL_IN), _F32),
                        pltpu.VMEM((NSEQ, RET_HEADS, RET_DK, RET_DV), _F32),
                        pltpu.VMEM((R, D_MODEL), _BF),
                        pltpu.VMEM((1, LANES), _F32)],
        compiler_params=pltpu.CompilerParams(dimension_semantics=("arbitrary", "arbitrary"),
                                             vmem_limit_bytes=VMEM_LIMIT),
        name="mixer_prompt",
    )(x, wts["g_mix"], wts["w_in"], wts["w_pool"], wts["pool_scale"], wts["ret_gn"], wts["w_out"],
      wts["g_ffn"], wts["w_rt"], cos2, sin2, dmask, xi_b, zeta_b, gc_b)
    dst_rows = jnp.moveaxis(dst_t[:, 0:2, :], 1, 0).reshape(2, T)
    return (x1.reshape(T, D_MODEL), dst.reshape(T, LANES), dst_rows, rw.reshape(T, LANES),
            npool.reshape(B, POOL_HIST, POOL_IN), nret.reshape(B, RET_HEADS, RET_DK, RET_DV), counts,
            h2.reshape((T,) + ROW_TILE))


def _prep_weights(g_mix, w_in, w_pool, pool_scale, ret_gn, w_out, g_ffn, w_grp, w_exp):
    w_rt = jnp.concatenate([w_grp, w_exp.reshape(D_MODEL, N_EXPERTS)], axis=1)
    w_rt = jnp.pad(w_rt, ((0, 0), (0, ROUTER_W - w_rt.shape[1])))
    row = lambda v: v.reshape(1, D_MODEL)
    return dict(g_mix=row(g_mix), w_in=w_in.astype(_BF), w_pool=w_pool.astype(_BF), pool_scale=row(pool_scale),
                ret_gn=row(ret_gn), w_out=w_out.astype(_BF), g_ffn=row(g_ffn), w_rt=w_rt.astype(_BF))


SAMPLE_TB = 8


def _mixer_sample_kernel(x_ref, spool_ref, sret_ref, g_mix_ref, w_in_ref, w_pool_ref, pscale_ref, gn_ref,
                         w_out_ref, g_ffn_ref, w_rt_ref, cos_ref, sin_ref, dm_ref, xi_ref, zeta_ref, gc_ref,
                         carry_in_ref,
                         x1_ref, dst_ref, rw_ref, npool_ref, nret_ref, cnt_ref, h2_ref,
                         u_scr, q_scr, k_scr, qt_scr, kt_scr, v_scr, ga_scr, gb_scr, pooled_scr, o_scr, mixed_scr,
                         carry_scr, *, cap):
    t = pl.program_id(0)
    n_t = pl.num_programs(0)
    Bs = x_ref.shape[0]
    TB = sret_ref.shape[0]

    @pl.when(t == 0)
    def _():
        h = _rms(x_ref[...], g_mix_ref[...]).astype(_BF)
        u_scr[...] = _dot(h, w_in_ref[:, OFF_U:OFF_U + POOL_IN])
        q = _dot(h, w_in_ref[:, OFF_Q:OFF_Q + QK_W])
        k = _dot(h, w_in_ref[:, OFF_K:OFF_K + QK_W])
        for j in range(RET_HEADS):
            qs = slice(j * RET_DK, (j + 1) * RET_DK)
            qf = _rotary(q[:, qs], cos_ref[...], sin_ref[...])
            kf = _rotary(k[:, qs], cos_ref[...], sin_ref[...]) * (RET_DK ** -0.5)
            q_scr[:, qs] = qf
            k_scr[:, qs] = kf
            qt_scr[j] = qf.T
            kt_scr[j] = kf.T
        v_scr[...] = _dot(h, w_in_ref[:, OFF_V:OFF_V + V_W])
        ga_scr[...] = _dot(h, w_in_ref[:, OFF_GA:OFF_GA + D_MODEL])
        gb_scr[...] = _dot(h, w_in_ref[:, OFF_GB:OFF_GB + D_MODEL])

    shift = (Bs - t * TB) % Bs
    qt = [pltpu.roll(qt_scr[j], shift, 1) for j in range(RET_HEADS)]
    kt = [pltpu.roll(kt_scr[j], shift, 1) for j in range(RET_HEADS)]

    blk = pl.ds(pl.multiple_of(t * TB, TB), TB)
    u_blk = u_scr[blk, :]
    q_blk = q_scr[blk, :]
    k_blk = k_scr[blk, :]
    v_blk = v_scr[blk, :]
    score = [jnp.sum(q_blk[:, j * RET_DK:(j + 1) * RET_DK] * k_blk[:, j * RET_DK:(j + 1) * RET_DK],
                     axis=-1, keepdims=True) * dm_ref[j] for j in range(RET_HEADS)]

    pooled_rows, o_rows = [], []
    for i in range(TB):
        u_row = u_blk[i:i + 1, :]
        groups = []
        for g, win in enumerate(POOL_WINDOWS):
            cs = slice(g * POOL_GC, (g + 1) * POOL_GC)
            hist_sum = jnp.sum(spool_ref[i, POOL_HIST - (win - 1):POOL_HIST, cs], axis=0, keepdims=True)
            groups.append((u_row[:, cs] + hist_sum) / jnp.float32(win) - u_row[:, cs])
        pooled_rows.append(jnp.concatenate(groups, axis=1))
        npool_ref[i, 0:POOL_HIST - 1, :] = spool_ref[i, 1:POOL_HIST, :]
        npool_ref[i, POOL_HIST - 1:POOL_HIST, :] = u_row

        heads = []
        for j in range(RET_HEADS):
            s_old = sret_ref[i, j]
            v_row = v_blk[i:i + 1, j * RET_DV:(j + 1) * RET_DV]
            qcol = qt[j][:, i:i + 1]
            kcol = kt[j][:, i:i + 1]
            qs_old = jnp.sum(qcol * s_old, axis=0, keepdims=True)
            heads.append(score[j][i:i + 1, :] * v_row + qs_old * xi_ref[j])
            nret_ref[i, j] = gc_ref[j] * s_old + (kcol * zeta_ref[j]) * v_row
        o_rows.append(jnp.concatenate(heads, axis=1))
    pooled_scr[blk, :] = jnp.concatenate(pooled_rows, axis=0)
    o_scr[blk, :] = jnp.concatenate(o_rows, axis=0)

    @pl.when(t == n_t - 1)
    def _():
        for j in range(RET_HEADS):
            cs = slice(j * POOL_GC, (j + 1) * POOL_GC)
            osl = slice(j * POOL_OUT_GC, (j + 1) * POOL_OUT_GC)
            pool_out = _dot(pooled_scr[:, cs].astype(_BF), w_pool_ref[j]) * pscale_ref[:, osl]
            ret_out = _group_norm(o_scr[:, osl], gn_ref[:, osl])
            mixed_scr[:, osl] = (jax.nn.sigmoid(ga_scr[:, osl]) * pool_out
                                 + jax.nn.sigmoid(gb_scr[:, osl]) * ret_out).astype(_BF)
        h2, e0, e1 = _post_mix(x_ref[...], mixed_scr, w_out_ref, g_ffn_ref, w_rt_ref, x1_ref, rw_ref)
        h2_ref[...] = h2.reshape(h2_ref.shape)
        carry_scr[...] = carry_in_ref[...]
        dst_ref[...] = _rank_rows(e0, e1, cap, carry_scr).astype(jnp.int32)
        cnt_ref[...] = carry_scr[...]


def _mixer_sample(x, state_pool, state_ret, wts, pos0, carry_in, cap):
    Bs = x.shape[0]
    assert Bs == LANES and Bs % SAMPLE_TB == 0
    TB = SAMPLE_TB
    dmask, xi, zeta, g_chunk = _decay_tables(1)
    dm_b = jnp.broadcast_to(dmask, (RET_HEADS, 1, 1))
    xi_b = jnp.broadcast_to(xi[:, :, None], (RET_HEADS, 1, RET_DV))
    zeta_b = jnp.broadcast_to(zeta[:, :, None], (RET_HEADS, 1, 1))
    gc_b = jnp.broadcast_to(g_chunk[:, None, None], (RET_HEADS, 1, RET_DV))
    cos2, sin2 = _rope_tables((pos0 + jnp.arange(1)).astype(_F32))

    in_specs = [
        _full((Bs, D_MODEL)),
        pl.BlockSpec((TB, POOL_HIST, POOL_IN), lambda t: (t, 0, 0)),
        pl.BlockSpec((TB, RET_HEADS, RET_DK, RET_DV), lambda t: (t, 0, 0, 0)),
        _full((1, D_MODEL)), _full((D_MODEL, IN_WIDTH)), _full((POOL_GROUPS, POOL_GC, POOL_OUT_GC)),
        _full((1, D_MODEL)), _full((1, D_MODEL)), _full((D_MODEL, D_MODEL)), _full((1, D_MODEL)),
        _full((D_MODEL, ROUTER_W)),
        _full((1, RET_DK)), _full((1, RET_DK)),
        _full((RET_HEADS, 1, 1)), _full((RET_HEADS, 1, RET_DV)), _full((RET_HEADS, 1, 1)),
        _full((RET_HEADS, 1, RET_DV)),
        _full((1, LANES)),
    ]
    out_shape = (
        jax.ShapeDtypeStruct((Bs, D_MODEL), _F32),
        jax.ShapeDtypeStruct((Bs, LANES), jnp.int32),
        jax.ShapeDtypeStruct((Bs, LANES), _F32),
        jax.ShapeDtypeStruct((Bs, POOL_HIST, POOL_IN), _F32),
        jax.ShapeDtypeStruct((Bs, RET_HEADS, RET_DK, RET_DV), _F32),
        jax.ShapeDtypeStruct((1, LANES), _F32),
        jax.ShapeDtypeStruct((Bs,) + ROW_TILE, _F32),
    )
    out_specs = (
        _full((Bs, D_MODEL)), _full((Bs, LANES)), _full((Bs, LANES)),
        pl.BlockSpec((TB, POOL_HIST, POOL_IN), lambda t: (t, 0, 0)),
        pl.BlockSpec((TB, RET_HEADS, RET_DK, RET_DV), lambda t: (t, 0, 0, 0)),
        _full((1, LANES)), _full((Bs,) + ROW_TILE),
    )
    f32 = lambda *s: pltpu.VMEM(s, _F32)
    return pl.pallas_call(
        functools.partial(_mixer_sample_kernel, cap=cap),
        grid=(Bs // TB,),
        in_specs=in_specs,
        out_specs=out_specs,
        out_shape=out_shape,
        scratch_shapes=[f32(Bs, POOL_IN), f32(Bs, QK_W), f32(Bs, QK_W), f32(RET_HEADS, RET_DK, Bs),
                        f32(RET_HEADS, RET_DK, Bs), f32(Bs, V_W), f32(Bs, D_MODEL), f32(Bs, D_MODEL),
                        f32(Bs, POOL_IN), f32(Bs, V_W), pltpu.VMEM((Bs, D_MODEL), _BF), f32(1, LANES)],
        compiler_params=pltpu.CompilerParams(dimension_semantics=("arbitrary",), vmem_limit_bytes=VMEM_LIMIT),
        name="mixer_sample",
    )(x, state_pool, state_ret, wts["g_mix"], wts["w_in"], wts["w_pool"], wts["pool_scale"], wts["ret_gn"],
      wts["w_out"], wts["g_ffn"], wts["w_rt"], cos2, sin2, dm_b, xi_b, zeta_b, gc_b, carry_in)


IDX_CHUNK = 1024
ISSUE_UNROLL = 8
SC_CORES = 2
SC_SUBCORES = 16
SC_WINDOW = 32


def _sc_scatter(h2, dst_rows, n_rows):
    T = h2.shape[0]
    n_workers = SC_CORES * SC_SUBCORES
    W = SC_WINDOW
    assert T % (n_workers * W) == 0
    per_w = T // n_workers
    n_win = per_w // W
    d0 = dst_rows[0].reshape(T // W, W)
    d1 = dst_rows[1].reshape(T // W, W)
    mesh = plsc.VectorSubcoreMesh(core_axis_name="c", subcore_axis_name="s")

    def body(h2_hbm, d0_hbm, d1_hbm, xs_hbm, i0_v, i1_v, rows_v, sem_load, sem_store):
        wid = lax.axis_index("s") * SC_CORES + lax.axis_index("c")
        base = wid * per_w
        pltpu.sync_copy(d0_hbm.at[pl.ds(wid * n_win, n_win)], i0_v)
        pltpu.sync_copy(d1_hbm.at[pl.ds(wid * n_win, n_win)], i1_v)
        pltpu.async_copy(h2_hbm.at[pl.ds(base, W)], rows_v.at[0], sem_load)
        for i in range(n_win):
            b = i % 2
            pltpu.make_async_copy(h2_hbm.at[pl.ds(base, W)], rows_v.at[b], sem_load).wait()
            if i + 1 < n_win:
                pltpu.async_copy(h2_hbm.at[pl.ds(base + (i + 1) * W, W)], rows_v.at[1 - b], sem_load)
            c0 = pltpu.async_copy(rows_v.at[b], xs_hbm.at[i0_v.at[i]], sem_store)
            c1 = pltpu.async_copy(rows_v.at[b], xs_hbm.at[i1_v.at[i]], sem_store)
            c0.wait()
            c1.wait()

    return pl.kernel(
        body, mesh=mesh,
        out_type=jax.ShapeDtypeStruct((n_rows,) + ROW_TILE, _F32),
        scratch_types=[pltpu.VMEM((n_win, W), jnp.int32), pltpu.VMEM((n_win, W), jnp.int32),
                       pltpu.VMEM((2, W) + ROW_TILE, _F32),
                       pltpu.SemaphoreType.DMA, pltpu.SemaphoreType.DMA],
        name="moe_sc_scatter",
    )(h2, d0, d1)


ZERO_CHUNK = 64


def _finish_dispatch_kernel(zstart_ref, zchunks_ref, slots_hbm, h2_ref, xs_in_hbm, xs_hbm, idx_smem, zbuf, sem_idx,
                            sem_rows, sem_zero):
    del xs_in_hbm
    TS = h2_ref.shape[0]
    idx_cp = pltpu.make_async_copy(slots_hbm.at[pl.ds(0, IDX_CHUNK)], idx_smem, sem_idx)
    idx_cp.start()
    zbuf[...] = jnp.zeros(zbuf.shape, zbuf.dtype)

    def zero_copy(e, j):
        return pltpu.make_async_copy(zbuf, xs_hbm.at[pl.ds(zstart_ref[e] + j * ZERO_CHUNK, ZERO_CHUNK)], sem_zero)

    for e in range(N_EXPERTS):
        lax.fori_loop(0, zchunks_ref[e], lambda j, c, e=e: (zero_copy(e, j).start(), c)[1], 0)
    idx_cp.wait()

    def issue(r, carry):
        for kk in range(2):
            pltpu.make_async_copy(h2_ref.at[r], xs_hbm.at[idx_smem[2 * r + kk]], sem_rows).start(priority=kk)
        return carry

    lax.fori_loop(0, TS, issue, 0, unroll=ISSUE_UNROLL)
    for e in range(N_EXPERTS):
        lax.fori_loop(0, zchunks_ref[e], lambda j, c, e=e: (zero_copy(e, j).wait(), c)[1], 0)
    for kk in range(2):
        pltpu.make_async_copy(h2_ref, xs_hbm.at[pl.ds(0, TS)], sem_rows).wait()


def _finish_dispatch(xs, h2s, slots, zstart, zchunks):
    Bs = h2s.shape[0]
    assert 2 * Bs <= IDX_CHUNK and slots.shape[0] == IDX_CHUNK and MOE_BLOCK % ZERO_CHUNK == 0
    any_spec = pl.BlockSpec(memory_space=pl.ANY)
    grid_spec = pltpu.PrefetchScalarGridSpec(
        num_scalar_prefetch=2,
        grid=(1,),
        in_specs=[any_spec, pl.BlockSpec((Bs,) + ROW_TILE, lambda t, z, n: (0, 0, 0)), any_spec],
        out_specs=any_spec,
        scratch_shapes=[pltpu.SMEM((IDX_CHUNK,), jnp.int32), pltpu.VMEM((ZERO_CHUNK,) + ROW_TILE, _F32),
                        pltpu.SemaphoreType.DMA, pltpu.SemaphoreType.DMA, pltpu.SemaphoreType.DMA],
    )
    return pl.pallas_call(
        _finish_dispatch_kernel,
        grid_spec=grid_spec,
        out_shape=jax.ShapeDtypeStruct(xs.shape, _F32),
        input_output_aliases={4: 0},
        compiler_params=pltpu.CompilerParams(dimension_semantics=("arbitrary",), has_side_effects=True),
        name="moe_finish_dispatch",
    )(zstart, zchunks, slots, h2s, xs)


def _ffn_kernel(bexp_ref, brow_ref, nused_ref, first_ref, slot_ref, next_ref,
                xs_ref, w1_hbm, w3_hbm, w2_hbm, ys_ref, w13_scr, w2_scr, st1, st3, st2, sem_w):
    del brow_ref
    i = pl.program_id(0)

    def weight_copies(e, s):
        return (pltpu.make_async_copy(w1_hbm.at[e], st1.at[s], sem_w.at[s]),
                pltpu.make_async_copy(w3_hbm.at[e], st3.at[s], sem_w.at[s]),
                pltpu.make_async_copy(w2_hbm.at[e], st2.at[s], sem_w.at[s]))

    @pl.when(i < nused_ref[0])
    def _():
        @pl.when(first_ref[i] == 1)
        def _():
            e = bexp_ref[i]
            s = slot_ref[i]

            @pl.when(i == 0)
            def _():
                for cp in weight_copies(e, s):
                    cp.start()

            for cp in weight_copies(e, s):
                cp.wait()
            @pl.when(next_ref[i] >= 0)
            def _():
                for cp in weight_copies(next_ref[i], 1 - s):
                    cp.start()

            w13_scr[:, 0:D_EXPERT] = st1[s].astype(_BF)
            w13_scr[:, D_EXPERT:2 * D_EXPERT] = st3[s].astype(_BF)
            w2_scr[...] = st2[s].astype(_BF)

        xb = xs_ref[...].reshape(MOE_BLOCK, D_MODEL).astype(_BF)
        ab = _dot(xb, w13_scr[...])
        hid = jax.nn.silu(ab[:, 0:D_EXPERT]) * ab[:, D_EXPERT:2 * D_EXPERT]
        ys_ref[...] = _dot(hid.astype(_BF), w2_scr[...]).reshape(ys_ref.shape)


def _ffn(xs, block_exp, block_row, n_used, run_first, run_slot, run_next, w1, w3, w2, n_blocks):
    blk = lambda i, *_: (_[1][i], 0, 0)
    any_spec = pl.BlockSpec(memory_space=pl.ANY)
    grid_spec = pltpu.PrefetchScalarGridSpec(
        num_scalar_prefetch=6,
        grid=(n_blocks,),
        in_specs=[pl.BlockSpec((MOE_BLOCK,) + ROW_TILE, blk), any_spec, any_spec, any_spec],
        out_specs=pl.BlockSpec((MOE_BLOCK,) + ROW_TILE, blk),
        scratch_shapes=[pltpu.VMEM((D_MODEL, 2 * D_EXPERT), _BF), pltpu.VMEM((D_EXPERT, D_MODEL), _BF),
                        pltpu.VMEM((2, D_MODEL, D_EXPERT), _F32), pltpu.VMEM((2, D_MODEL, D_EXPERT), _F32),
                        pltpu.VMEM((2, D_EXPERT, D_MODEL), _F32), pltpu.SemaphoreType.DMA((2,))],
    )
    return pl.pallas_call(
        _ffn_kernel,
        grid_spec=grid_spec,
        out_shape=jax.ShapeDtypeStruct(xs.shape, _F32),
        compiler_params=pltpu.CompilerParams(dimension_semantics=("arbitrary",), vmem_limit_bytes=VMEM_LIMIT),
        name="moe_ffn",
    )(block_exp, block_row, n_used, run_first, run_slot, run_next, xs, w1, w3, w2)


def _combine_kernel(slots_hbm, x1_ref, rw_ref, g_ref, ys_hbm, y_ref, idx_smem, buf, sem_idx, sem_rows):
    t = pl.program_id(0)
    TC = x1_ref.shape[0]
    idx_cp = pltpu.make_async_copy(slots_hbm.at[pl.ds(t * IDX_CHUNK, IDX_CHUNK)], idx_smem, sem_idx)
    idx_cp.start()
    idx_cp.wait()

    def issue(r, carry):
        for kk in range(2):
            pltpu.make_async_copy(ys_hbm.at[idx_smem[2 * r + kk]], buf.at[kk, r], sem_rows).start(priority=kk)
        return carry

    lax.fori_loop(0, TC, issue, 0, unroll=ISSUE_UNROLL)
    for kk in range(2):
        pltpu.make_async_copy(ys_hbm.at[pl.ds(0, TC)], buf.at[kk], sem_rows).wait()

    w = rw_ref[...]
    y0 = buf[0].reshape(TC, D_MODEL)
    y1 = buf[1].reshape(TC, D_MODEL)
    x2 = x1_ref[...] + (w[:, 0:1] * y0 + w[:, 1:2] * y1)
    y_ref[...] = _rms(x2, g_ref[...])


TC_GATHER_EIGHTHS = 3
SC_GATHER_BUFFERS = 3
SC_GATHER_MAX_WINDOW = 40


def _sc_gather(ys, rows):
    Tg = rows.shape[1]
    n_workers = SC_CORES * SC_SUBCORES
    per_w = Tg // n_workers
    assert per_w * n_workers == Tg and per_w % SUBLANES == 0
    W = max(w for w in range(SUBLANES, SC_GATHER_MAX_WINDOW + 1, SUBLANES) if per_w % w == 0)
    n_win = per_w // W
    NB = SC_GATHER_BUFFERS
    idx = rows.reshape(2, n_workers, n_win, W)
    jobs = [(kk, i) for i in range(n_win) for kk in range(2)]
    mesh = plsc.VectorSubcoreMesh(core_axis_name="c", subcore_axis_name="s")

    def body(ys_hbm, idx_hbm, out_hbm, i_v, rows_v, sem_g, sem_s):
        wid = lax.axis_index("s") * SC_CORES + lax.axis_index("c")
        base = wid * per_w
        for kk in range(2):
            pltpu.sync_copy(idx_hbm.at[kk, wid], i_v.at[kk])

        def gather(j):
            kk, i = jobs[j]
            return pltpu.make_async_copy(ys_hbm.at[i_v.at[kk, i]], rows_v.at[j % NB], sem_g.at[j % NB])

        def store(j):
            kk, i = jobs[j]
            return pltpu.make_async_copy(rows_v.at[j % NB], out_hbm.at[kk, pl.ds(base + i * W, W)], sem_s.at[j % NB])

        gather(0).start()
        for j in range(len(jobs)):
            gather(j).wait()
            store(j).start()
            if j + 1 < len(jobs):
                if j + 1 >= NB:
                    store(j + 1 - NB).wait()
                gather(j + 1).start()
        for j in range(max(0, len(jobs) - NB), len(jobs)):
            store(j).wait()

    return pl.kernel(
        body, mesh=mesh,
        out_type=jax.ShapeDtypeStruct((2, Tg) + ROW_TILE, _F32),
        scratch_types=[pltpu.VMEM((2, n_win, W), jnp.int32), pltpu.VMEM((NB, W) + ROW_TILE, _F32),
                       pltpu.SemaphoreType.DMA((NB,)), pltpu.SemaphoreType.DMA((NB,))],
        name="moe_sc_gather",
    )(ys, idx)


def _combine_dense_kernel(x1_ref, rw_ref, g_ref, rows_ref, y_in_hbm, y_ref):
    del y_in_hbm
    TC = x1_ref.shape[0]
    w = rw_ref[...]
    y0 = rows_ref[0].reshape(TC, D_MODEL)
    y1 = rows_ref[1].reshape(TC, D_MODEL)
    x2 = x1_ref[...] + (w[:, 0:1] * y0 + w[:, 1:2] * y1)
    y_ref[...] = _rms(x2, g_ref[...])


def _combine_dense(x1, rw, rows, y, g_final, tile, first_tile):
    Tg = rows.shape[1]
    assert Tg % tile == 0
    tok = lambda t: (first_tile + t, 0)
    return pl.pallas_call(
        _combine_dense_kernel,
        grid=(Tg // tile,),
        in_specs=[pl.BlockSpec((tile, D_MODEL), tok), pl.BlockSpec((tile, LANES), tok), _full((1, D_MODEL)),
                  pl.BlockSpec((2, tile) + ROW_TILE, lambda t: (0, t, 0, 0)), pl.BlockSpec(memory_space=pl.ANY)],
        out_specs=pl.BlockSpec((tile, D_MODEL), tok),
        out_shape=jax.ShapeDtypeStruct(y.shape, _F32),
        input_output_aliases={4: 0},
        compiler_params=pltpu.CompilerParams(dimension_semantics=("arbitrary",), vmem_limit_bytes=VMEM_LIMIT),
        name="moe_combine_dense",
    )(x1, rw, g_final.reshape(1, D_MODEL), rows, y)


def _combine(x1, rw, slots, ys, g_final, tile, n_tiles):
    T = x1.shape[0]
    assert T % tile == 0 and 2 * tile <= IDX_CHUNK and slots.shape[0] == n_tiles * IDX_CHUNK
    any_spec = pl.BlockSpec(memory_space=pl.ANY)
    return pl.pallas_call(
        _combine_kernel,
        grid=(n_tiles,),
        in_specs=[any_spec, pl.BlockSpec((tile, D_MODEL), lambda t: (t, 0)),
                  pl.BlockSpec((tile, LANES), lambda t: (t, 0)), _full((1, D_MODEL)), any_spec],
        out_specs=pl.BlockSpec((tile, D_MODEL), lambda t: (t, 0)),
        out_shape=jax.ShapeDtypeStruct((T, D_MODEL), _F32),
        scratch_shapes=[pltpu.SMEM((IDX_CHUNK,), jnp.int32), pltpu.VMEM((2, tile) + ROW_TILE, _F32),
                        pltpu.SemaphoreType.DMA, pltpu.SemaphoreType.DMA],
        compiler_params=pltpu.CompilerParams(dimension_semantics=("arbitrary",), vmem_limit_bytes=VMEM_LIMIT),
        name="moe_combine",
    )(slots, x1, rw, g_final.reshape(1, D_MODEL), ys)


def _tile_for(n_tokens):
    tile = IDX_CHUNK // 2
    return tile if n_tokens % tile == 0 else n_tokens


def _chunked_slots(slot, tile):
    n_tiles = slot.shape[0] // tile
    s = slot.reshape(n_tiles, 2 * tile)
    return jnp.pad(s, ((0, 0), (0, IDX_CHUNK - 2 * tile))).reshape(-1)


def kernel(x_prompt, x_sample, state_pool, state_ret, g_mix, w_in, w_pool, pool_scale, ret_gn, w_out, g_ffn, w_grp, w_exp, w1, w3, w2, g_final):
    Bp, Lp, _ = x_prompt.shape
    Bs = x_sample.shape[0]
    Tp = Bp * Lp
    wts = _prep_weights(g_mix[0], w_in[0], w_pool[0], pool_scale[0], ret_gn[0], w_out[0], g_ffn[0], w_grp[0], w_exp[0])

    T_all = Tp + Bs
    cap = (-(-T_all // MOE_BLOCK) + 1) * MOE_BLOCK
    cap_blocks = cap // MOE_BLOCK

    x1p, dst_p, dst_rows_p, rwp, npool_p, nret_p, counts_p, h2p = _mixer_prompt(x_prompt, wts, cap)
    xs = _sc_scatter(h2p, dst_rows_p, N_EXPERTS * cap)
    x1s, dst_s, rws, npool_s, nret_s, counts, h2s = _mixer_sample(
        x_sample.reshape(Bs, D_MODEL), state_pool[0], state_ret[0], wts, PAST_LEN, counts_p, cap)

    counts = counts[0, :N_EXPERTS].astype(jnp.int32)
    tile_p, tile_s = _tile_for(Tp), _tile_for(Bs)
    slots_p = _chunked_slots(dst_p[:, :2], tile_p)
    slots_s = _chunked_slots(dst_s[:, :2], tile_s)
    zstart = jnp.arange(N_EXPERTS, dtype=jnp.int32) * cap + counts
    zchunks = ((-counts) % MOE_BLOCK + ZERO_CHUNK - 1) // ZERO_CHUNK
    xs = _finish_dispatch(xs, h2s, slots_s, zstart, zchunks.astype(jnp.int32))
    nblk = (counts + MOE_BLOCK - 1) // MOE_BLOCK
    blk_end = jnp.cumsum(nblk)
    n_used = blk_end[-1:].astype(jnp.int32)
    n_blocks = -(-2 * T_all // MOE_BLOCK) + N_EXPERTS
    step = jnp.minimum(jnp.arange(n_blocks, dtype=jnp.int32), n_used - 1)
    block_exp = jnp.minimum(jnp.sum((blk_end[None, :] <= step[:, None]).astype(jnp.int32), axis=1), N_EXPERTS - 1)
    blk_start = jnp.sum(jnp.where(block_exp[:, None] == jnp.arange(N_EXPERTS, dtype=jnp.int32), (blk_end - nblk)[None, :], 0),
                        axis=1)
    block_row = (block_exp * cap_blocks + (step - blk_start)).astype(jnp.int32)
    block_exp = block_exp.astype(jnp.int32)
    experts = jnp.arange(N_EXPERTS, dtype=jnp.int32)
    nonempty = nblk > 0
    run_first = (step == blk_start).astype(jnp.int32)
    run_slot = (jnp.sum((nonempty[None, :] & (experts[None, :] < block_exp[:, None])).astype(jnp.int32), axis=1) % 2)
    later = nonempty[None, :] & (experts[None, :] > block_exp[:, None])
    run_next = jnp.min(jnp.where(later, experts[None, :], N_EXPERTS), axis=1)
    run_next = jnp.where(run_next == N_EXPERTS, -1, run_next).astype(jnp.int32)

    ys = _ffn(xs, block_exp, block_row, n_used, run_first, run_slot.astype(jnp.int32), run_next,
              w1[0], w3[0], w2[0], n_blocks)
    n_tiles_p = Tp // tile_p
    n_tc = max(1, n_tiles_p * TC_GATHER_EIGHTHS // 8)
    y_s = _combine(x1s, rws, slots_s, ys, g_final, tile_s, 1)
    if n_tc < n_tiles_p:
        gathered = _sc_gather(ys, dst_rows_p[:, n_tc * tile_p:])
        y_p = _combine(x1p, rwp, slots_p[:n_tc * IDX_CHUNK], ys, g_final, tile_p, n_tc)
        y_p = _combine_dense(x1p, rwp, gathered, y_p, g_final, tile_p, n_tc)
    else:
        y_p = _combine(x1p, rwp, slots_p, ys, g_final, tile_p, n_tc)

    return (y_p.reshape(Bp, Lp, D_MODEL), y_s.reshape(Bs, 1, D_MODEL),
            npool_p[None], nret_p[None], npool_s[None], nret_s[None])
```

```python
import functools

import jax
import jax.numpy as jnp
from jax import lax
from jax.experimental import pallas as pl
from jax.experimental.pallas import tpu as pltpu
from jax.experimental.pallas import tpu_sc as plsc

D_MODEL = 1024
EPS = 1e-6
POOL_GROUPS = 4
POOL_IN = D_MODEL // 2
POOL_GC = POOL_IN // POOL_GROUPS
POOL_OUT_GC = D_MODEL // POOL_GROUPS
POOL_WINDOWS = (2, 4, 8, 16)
POOL_HIST = max(POOL_WINDOWS) - 1
RET_HEADS = 4
RET_DK = D_MODEL // 8
RET_DV = D_MODEL // RET_HEADS
ROPE_BASE = 10000.0
PAST_LEN = 16384
N_GROUPS = 4
EXPERTS_PER_GROUP = 8
N_EXPERTS = N_GROUPS * EXPERTS_PER_GROUP
D_EXPERT = D_MODEL // 4
QK_W = RET_HEADS * RET_DK
V_W = RET_HEADS * RET_DV
OFF_U = 0
OFF_Q = POOL_IN
OFF_K = OFF_Q + QK_W
OFF_V = OFF_K + QK_W
OFF_GA = OFF_V + V_W
OFF_GB = OFF_GA + D_MODEL
IN_WIDTH = OFF_GB + D_MODEL

LANES = 128
SUBLANES = 8
ROW_TILE = (SUBLANES, LANES)
assert SUBLANES * LANES == D_MODEL
HALO = 16
RET_CHUNK = 256
MIXER_SEQS = 2
MOE_BLOCK = 512
ROUTER_W = LANES
VMEM_LIMIT = 56 * 1024 * 1024

_BF = jnp.bfloat16
_F32 = jnp.float32


def _rms(x, g):
    inv = lax.rsqrt(jnp.mean(x * x, axis=-1, keepdims=True) + EPS)
    return x * inv * g


def _dot(a, b):
    return jnp.dot(a, b, preferred_element_type=_F32)


def _rotary(x, cos2, sin2):
    return x * cos2 + pltpu.roll(x, RET_DK // 2, 1) * sin2


def _route(logits):
    lane = lax.broadcasted_iota(jnp.int32, logits.shape, 1).astype(_F32)
    neg = jnp.float32(-jnp.inf)
    big = jnp.float32(1 << 20)
    lg = jnp.where(lane < N_GROUPS, logits, neg)
    mg = jnp.max(lg, axis=-1, keepdims=True)
    g_idx = jnp.min(jnp.where(lg == mg, lane, big), axis=-1, keepdims=True)
    p_g = 1.0 / jnp.sum(jnp.exp(lg - mg), axis=-1, keepdims=True)
    lo = N_GROUPS + g_idx * EXPERTS_PER_GROUP
    in_grp = (lane >= lo) & (lane < lo + EXPERTS_PER_GROUP)
    le = jnp.where(in_grp, logits, neg)
    m1 = jnp.max(le, axis=-1, keepdims=True)
    i1 = jnp.min(jnp.where(le == m1, lane, big), axis=-1, keepdims=True)
    le2 = jnp.where(lane == i1, neg, le)
    m2 = jnp.max(le2, axis=-1, keepdims=True)
    i2 = jnp.min(jnp.where(le2 == m2, lane, big), axis=-1, keepdims=True)
    t = jnp.exp(m2 - m1)
    den = 1.0 + t
    e0 = (i1 - N_GROUPS).astype(jnp.int32)
    e1 = (i2 - N_GROUPS).astype(jnp.int32)
    return e0, e1, p_g * (1.0 / den), p_g * (t / den)


def _post_mix(x, mixed_ref, w_out_ref, g_ffn_ref, w_rt_ref, x1_ref, rw_ref):
    x1 = x + _dot(mixed_ref[...], w_out_ref[...])
    x1_ref[...] = x1.reshape(x1_ref.shape)
    h2 = _rms(x1, g_ffn_ref[...])
    e0, e1, w0, w1 = _route(_dot(h2.astype(_BF), w_rt_ref[...]))
    lane = lax.broadcasted_iota(jnp.int32, (x.shape[0], LANES), 1)
    rw_ref[...] = jnp.where(lane == 0, w0, jnp.where(lane == 1, w1, 0.0)).reshape(rw_ref.shape)
    return h2, e0, e1


def _rank_rows(e0, e1, cap, carry_scr):
    R = e0.shape[0]
    lane = lax.broadcasted_iota(jnp.int32, (R, LANES), 1)
    m0 = lane == e0
    m1 = lane == e1
    onehot = jnp.where(m0 | m1, 1.0, 0.0)
    r_i = lax.broadcasted_iota(jnp.int32, (R, R), 0)
    c_i = lax.broadcasted_iota(jnp.int32, (R, R), 1)
    tri = jnp.where(c_i < r_i, 1.0, 0.0).astype(_BF)
    before = _dot(tri, onehot.astype(_BF)) + carry_scr[...]
    d0 = e0.astype(_F32) * cap + jnp.sum(jnp.where(m0, before, 0.0), axis=-1, keepdims=True)
    d1 = e1.astype(_F32) * cap + jnp.sum(jnp.where(m1, before, 0.0), axis=-1, keepdims=True)
    carry_scr[...] += jnp.sum(onehot, axis=0, keepdims=True)
    return jnp.where(lane == 0, d0, jnp.where(lane == 1, d1, 0.0))


def _group_norm(o, gain):
    mu = jnp.mean(o, axis=-1, keepdims=True)
    d = o - mu
    var = jnp.mean(d * d, axis=-1, keepdims=True)
    return d * lax.rsqrt(var + EPS) * gain


def _mixer_prompt_kernel(x_ref, g_mix_ref, w_in_ref, w_pool_ref, pscale_ref, gn_ref, w_out_ref, g_ffn_ref,
                         w_rt_ref, cos_ref, sin_ref, dmask_ref, xi_ref, zeta_ref, gc_ref,
                         x1_ref, dst_ref, dst_t_ref, rw_ref, npool_ref, nret_ref, cnt_ref, h2_ref,
                         u_scr, s_scr, mixed_scr, carry_scr, *, cap):
    c = pl.program_id(1)
    n_c = pl.num_programs(1)
    @pl.when((pl.program_id(0) == 0) & (c == 0))
    def _():
        carry_scr[...] = jnp.zeros(carry_scr.shape, _F32)
    NSEQ, C, _ = x_ref.shape
    R = NSEQ * C

    @pl.when(c == 0)
    def _():
        u_scr[:, 0:HALO, :] = jnp.zeros((NSEQ, HALO, POOL_IN), _F32)
        s_scr[...] = jnp.zeros(s_scr.shape, _F32)

    x = x_ref[...].reshape(R, D_MODEL)
    h = _rms(x, g_mix_ref[...]).astype(_BF)

    u_scr[:, HALO:HALO + C, :] = _dot(h, w_in_ref[:, OFF_U:OFF_U + POOL_IN]).reshape(NSEQ, C, POOL_IN)
    q = _dot(h, w_in_ref[:, OFF_Q:OFF_Q + QK_W])
    k = _dot(h, w_in_ref[:, OFF_K:OFF_K + QK_W])
    cos2 = jnp.concatenate([cos_ref[...]] * NSEQ, axis=0)
    sin2 = jnp.concatenate([sin_ref[...]] * NSEQ, axis=0)
    pos1 = (c * C + 1 + lax.broadcasted_iota(jnp.int32, (C, POOL_GC), 0)).astype(_F32)

    for j in range(RET_HEADS):
        win = POOL_WINDOWS[j]
        cs = slice(j * POOL_GC, (j + 1) * POOL_GC)
        n_rows = jnp.minimum(pos1, jnp.float32(win))
        pooled = []
        for sq in range(NSEQ):
            u_j = u_scr[sq, HALO:HALO + C, cs]
            s = u_j
            for d in range(1, win):
                s = s + u_scr[sq, HALO - d:HALO - d + C, cs]
            pooled.append(s / n_rows - u_j)
        osl = slice(j * POOL_OUT_GC, (j + 1) * POOL_OUT_GC)
        pool_out = _dot(jnp.concatenate(pooled, axis=0).astype(_BF), w_pool_ref[j]) * pscale_ref[:, osl]

        qs = slice(j * RET_DK, (j + 1) * RET_DK)
        qb = _rotary(q[:, qs], cos2, sin2).astype(_BF)
        kf = _rotary(k[:, qs], cos2, sin2) * (RET_DK ** -0.5)
        kb = kf.astype(_BF)
        vb = _dot(h, w_in_ref[:, OFF_V + j * RET_DV:OFF_V + (j + 1) * RET_DV]).astype(_BF)
        ret = []
        for sq in range(NSEQ):
            rows = slice(sq * C, (sq + 1) * C)
            scores = lax.dot_general(qb[rows], kb[rows], (((1,), (1,)), ((), ())),
                                     preferred_element_type=_F32) * dmask_ref[j]
            s_old = s_scr[sq, j]
            o = _dot(scores.astype(_BF), vb[rows]) + _dot(qb[rows], s_old.astype(_BF)) * xi_ref[j]
            kz = (kf[rows] * zeta_ref[j]).astype(_BF)
            s_scr[sq, j] = gc_ref[j] * s_old + lax.dot_general(kz, vb[rows], (((0,), (0,)), ((), ())),
                                                               preferred_element_type=_F32)
            ret.append(_group_norm(o, gn_ref[:, osl]))
        ret_out = jnp.concatenate(ret, axis=0)

        ga = _dot(h, w_in_ref[:, OFF_GA + j * RET_DV:OFF_GA + (j + 1) * RET_DV])
        gb = _dot(h, w_in_ref[:, OFF_GB + j * RET_DV:OFF_GB + (j + 1) * RET_DV])
        mixed_scr[:, osl] = (jax.nn.sigmoid(ga) * pool_out + jax.nn.sigmoid(gb) * ret_out).astype(_BF)

    h2, e0, e1 = _post_mix(x, mixed_scr, w_out_ref, g_ffn_ref, w_rt_ref, x1_ref, rw_ref)

    h2_ref[...] = h2.reshape(h2_ref.shape)
    dst = _rank_rows(e0, e1, cap, carry_scr)
    dst_ref[...] = dst.astype(jnp.int32).reshape(dst_ref.shape)
    dst_t = dst.T[0:SUBLANES, :].astype(jnp.int32)
    for sq in range(NSEQ):
        dst_t_ref[sq] = dst_t[:, sq * C:(sq + 1) * C]
    cnt_ref[...] = carry_scr[...]

    u_scr[:, 0:HALO, :] = u_scr[:, C:C + HALO, :]

    @pl.when(c == n_c - 1)
    def _():
        npool_ref[...] = u_scr[:, HALO + C - POOL_HIST:HALO + C, :]
        nret_ref[...] = s_scr[...]


def _decay_tables(C):
    log_g = jnp.log(1.0 - jnp.exp2(-5.0 - jnp.arange(RET_HEADS, dtype=_F32)))
    i = jnp.arange(C, dtype=_F32)
    diff = i[:, None] - i[None, :]
    dmask = jnp.where(diff >= 0, jnp.exp(jnp.maximum(diff, 0.0)[None] * log_g[:, None, None]), 0.0)
    xi = jnp.exp((i[None, :] + 1.0) * log_g[:, None])
    zeta = jnp.exp((C - 1.0 - i)[None, :] * log_g[:, None])
    g_chunk = jnp.exp(C * log_g)
    return dmask, xi, zeta, g_chunk


def _rope_tables(pos):
    half = RET_DK // 2
    freqs = ROPE_BASE ** (-jnp.arange(half, dtype=_F32) / half)
    ang = pos[:, None] * freqs[None, :]
    cos, sin = jnp.cos(ang), jnp.sin(ang)
    return jnp.concatenate([cos, cos], axis=-1), jnp.concatenate([-sin, sin], axis=-1)


def _full(shape):
    n = len(shape)
    return pl.BlockSpec(shape, lambda *_: (0,) * n)


def _mixer_prompt(x, wts, cap):
    B, L, _ = x.shape
    C = RET_CHUNK if L % RET_CHUNK == 0 else L
    n_c = L // C
    T = B * L
    dmask, xi, zeta, g_chunk = _decay_tables(C)
    xi_b = jnp.broadcast_to(xi[:, :, None], (RET_HEADS, C, RET_DV))
    zeta_b = jnp.broadcast_to(zeta[:, :, None], (RET_HEADS, C, RET_DK))
    gc_b = jnp.broadcast_to(g_chunk[:, None, None], (RET_HEADS, 1, RET_DV))
    cos2, sin2 = _rope_tables(jnp.arange(L).astype(_F32))

    NSEQ = MIXER_SEQS if B % MIXER_SEQS == 0 else 1
    B2 = B // NSEQ
    T2 = T // NSEQ
    x = x.reshape(NSEQ, B2, L, D_MODEL)
    tok = lambda b, c: (0, b * n_c + c, 0)
    in_specs = [
        pl.BlockSpec((NSEQ, None, C, D_MODEL), lambda b, c: (0, b, c, 0)),
        _full((1, D_MODEL)), _full((D_MODEL, IN_WIDTH)), _full((POOL_GROUPS, POOL_GC, POOL_OUT_GC)),
        _full((1, D_MODEL)), _full((1, D_MODEL)), _full((D_MODEL, D_MODEL)), _full((1, D_MODEL)),
        _full((D_MODEL, ROUTER_W)),
        pl.BlockSpec((C, RET_DK), lambda b, c: (c, 0)), pl.BlockSpec((C, RET_DK), lambda b, c: (c, 0)),
        _full((RET_HEADS, C, C)), _full((RET_HEADS, C, RET_DV)), _full((RET_HEADS, C, RET_DK)),
        _full((RET_HEADS, 1, RET_DV)),
    ]
    R = NSEQ * C
    out_shape = (
        jax.ShapeDtypeStruct((NSEQ, T2, D_MODEL), _F32),
        jax.ShapeDtypeStruct((NSEQ, T2, LANES), jnp.int32),
        jax.ShapeDtypeStruct((NSEQ, SUBLANES, T2), jnp.int32),
        jax.ShapeDtypeStruct((NSEQ, T2, LANES), _F32),
        jax.ShapeDtypeStruct((NSEQ, B2, POOL_HIST, POOL_IN), _F32),
        jax.ShapeDtypeStruct((NSEQ, B2, RET_HEADS, RET_DK, RET_DV), _F32),
        jax.ShapeDtypeStruct((1, LANES), _F32),
        jax.ShapeDtypeStruct((NSEQ, T2) + ROW_TILE, _F32),
    )
    out_specs = (
        pl.BlockSpec((NSEQ, C, D_MODEL), tok),
        pl.BlockSpec((NSEQ, C, LANES), tok),
        pl.BlockSpec((NSEQ, SUBLANES, C), lambda b, c: (0, 0, b * n_c + c)),
        pl.BlockSpec((NSEQ, C, LANES), tok),
        pl.BlockSpec((NSEQ, None, POOL_HIST, POOL_IN), lambda b, c: (0, b, 0, 0)),
        pl.BlockSpec((NSEQ, None, RET_HEADS, RET_DK, RET_DV), lambda b, c: (0, b, 0, 0, 0)),
        _full((1, LANES)),
        pl.BlockSpec((NSEQ, C) + ROW_TILE, lambda b, c: (0, b * n_c + c, 0, 0)),
    )
    x1, dst, dst_t, rw, npool, nret, counts, h2 = pl.pallas_call(
        functools.partial(_mixer_prompt_kernel, cap=cap),
        grid=(B2, n_c),
        in_specs=in_specs,
        out_specs=out_specs,
        out_shape=out_shape,
        scratch_shapes=[pltpu.VMEM((NSEQ, HALO + C, POOL_IN), _F32),
                        pltpu.VMEM((NSEQ, RET_HEADS, RET_DK, RET_DV), _F32),
                        pltpu.VMEM((R, D_MODEL), _BF),
                        pltpu.VMEM((1, LANES), _F32)],
        compiler_params=pltpu.CompilerParams(dimension_semantics=("arbitrary", "arbitrary"),
                                             vmem_limit_bytes=VMEM_LIMIT),
        name="mixer_prompt",
    )(x, wts["g_mix"], wts["w_in"], wts["w_pool"], wts["pool_scale"], wts["ret_gn"], wts["w_out"],
      wts["g_ffn"], wts["w_rt"], cos2, sin2, dmask, xi_b, zeta_b, gc_b)
    dst_rows = jnp.moveaxis(dst_t[:, 0:2, :], 1, 0).reshape(2, T)
    return (x1.reshape(T, D_MODEL), dst.reshape(T, LANES), dst_rows, rw.reshape(T, LANES),
            npool.reshape(B, POOL_HIST, POOL_IN), nret.reshape(B, RET_HEADS, RET_DK, RET_DV), counts,
            h2.reshape((T,) + ROW_TILE))


def _prep_weights(g_mix, w_in, w_pool, pool_scale, ret_gn, w_out, g_ffn, w_grp, w_exp):
    w_rt = jnp.concatenate([w_grp, w_exp.reshape(D_MODEL, N_EXPERTS)], axis=1)
    w_rt = jnp.pad(w_rt, ((0, 0), (0, ROUTER_W - w_rt.shape[1])))
    row = lambda v: v.reshape(1, D_MODEL)
    return dict(g_mix=row(g_mix), w_in=w_in.astype(_BF), w_pool=w_pool.astype(_BF), pool_scale=row(pool_scale),
                ret_gn=row(ret_gn), w_out=w_out.astype(_BF), g_ffn=row(g_ffn), w_rt=w_rt.astype(_BF))


SAMPLE_TB = 8


def _mixer_sample_kernel(x_ref, spool_ref, sret_ref, g_mix_ref, w_in_ref, w_pool_ref, pscale_ref, gn_ref,
                         w_out_ref, g_ffn_ref, w_rt_ref, cos_ref, sin_ref, dm_ref, xi_ref, zeta_ref, gc_ref,
                         carry_in_ref,
                         x1_ref, dst_ref, rw_ref, npool_ref, nret_ref, cnt_ref, h2_ref,
                         u_scr, q_scr, k_scr, qt_scr, kt_scr, v_scr, ga_scr, gb_scr, pooled_scr, o_scr, mixed_scr,
                         carry_scr, *, cap):
    t = pl.program_id(0)
    n_t = pl.num_programs(0)
    Bs = x_ref.shape[0]
    TB = sret_ref.shape[0]

    @pl.when(t == 0)
    def _():
        h = _rms(x_ref[...], g_mix_ref[...]).astype(_BF)
        u_scr[...] = _dot(h, w_in_ref[:, OFF_U:OFF_U + POOL_IN])
        q = _dot(h, w_in_ref[:, OFF_Q:OFF_Q + QK_W])
        k = _dot(h, w_in_ref[:, OFF_K:OFF_K + QK_W])
        for j in range(RET_HEADS):
            qs = slice(j * RET_DK, (j + 1) * RET_DK)
            qf = _rotary(q[:, qs], cos_ref[...], sin_ref[...])
            kf = _rotary(k[:, qs], cos_ref[...], sin_ref[...]) * (RET_DK ** -0.5)
            q_scr[:, qs] = qf
            k_scr[:, qs] = kf
            qt_scr[j] = qf.T
            kt_scr[j] = kf.T
        v_scr[...] = _dot(h, w_in_ref[:, OFF_V:OFF_V + V_W])
        ga_scr[...] = _dot(h, w_in_ref[:, OFF_GA:OFF_GA + D_MODEL])
        gb_scr[...] = _dot(h, w_in_ref[:, OFF_GB:OFF_GB + D_MODEL])

    shift = (Bs - t * TB) % Bs
    qt = [pltpu.roll(qt_scr[j], shift, 1) for j in range(RET_HEADS)]
    kt = [pltpu.roll(kt_scr[j], shift, 1) for j in range(RET_HEADS)]

    blk = pl.ds(pl.multiple_of(t * TB, TB), TB)
    u_blk = u_scr[blk, :]
    q_blk = q_scr[blk, :]
    k_blk = k_scr[blk, :]
    v_blk = v_scr[blk, :]
    score = [jnp.sum(q_blk[:, j * RET_DK:(j + 1) * RET_DK] * k_blk[:, j * RET_DK:(j + 1) * RET_DK],
                     axis=-1, keepdims=True) * dm_ref[j] for j in range(RET_HEADS)]

    groups = []
    for g, win in enumerate(POOL_WINDOWS):
        cs = slice(g * POOL_GC, (g + 1) * POOL_GC)
        s = u_blk[:, cs]
        for r in range(POOL_HIST - (win - 1), POOL_HIST):
            s = s + spool_ref[r, :, cs]
        groups.append(s / jnp.float32(win) - u_blk[:, cs])
    pooled_scr[blk, :] = jnp.concatenate(groups, axis=1)
    npool_ref[0:POOL_HIST - 1] = spool_ref[1:POOL_HIST]
    npool_ref[POOL_HIST - 1] = u_blk

    o_rows = []
    for i in range(TB):
        heads = []
        for j in range(RET_HEADS):
            s_old = sret_ref[i, j]
            v_row = v_blk[i:i + 1, j * RET_DV:(j + 1) * RET_DV]
            qcol = qt[j][:, i:i + 1]
            kcol = kt[j][:, i:i + 1]
            qs_old = jnp.sum(qcol * s_old, axis=0, keepdims=True)
            heads.append(score[j][i:i + 1, :] * v_row + qs_old * xi_ref[j])
            nret_ref[i, j] = gc_ref[j] * s_old + (kcol * zeta_ref[j]) * v_row
        o_rows.append(jnp.concatenate(heads, axis=1))
    o_scr[blk, :] = jnp.concatenate(o_rows, axis=0)

    @pl.when(t == n_t - 1)
    def _():
        for j in range(RET_HEADS):
            cs = slice(j * POOL_GC, (j + 1) * POOL_GC)
            osl = slice(j * POOL_OUT_GC, (j + 1) * POOL_OUT_GC)
            pool_out = _dot(pooled_scr[:, cs].astype(_BF), w_pool_ref[j]) * pscale_ref[:, osl]
            ret_out = _group_norm(o_scr[:, osl], gn_ref[:, osl])
            mixed_scr[:, osl] = (jax.nn.sigmoid(ga_scr[:, osl]) * pool_out
                                 + jax.nn.sigmoid(gb_scr[:, osl]) * ret_out).astype(_BF)
        h2, e0, e1 = _post_mix(x_ref[...], mixed_scr, w_out_ref, g_ffn_ref, w_rt_ref, x1_ref, rw_ref)
        h2_ref[...] = h2.reshape(h2_ref.shape)
        carry_scr[...] = carry_in_ref[...]
        dst_ref[...] = _rank_rows(e0, e1, cap, carry_scr).astype(jnp.int32)
        cnt_ref[...] = carry_scr[...]


def _mixer_sample(x, state_pool, state_ret, wts, pos0, carry_in, cap):
    Bs = x.shape[0]
    assert Bs == LANES and Bs % SAMPLE_TB == 0
    TB = SAMPLE_TB
    dmask, xi, zeta, g_chunk = _decay_tables(1)
    dm_b = jnp.broadcast_to(dmask, (RET_HEADS, 1, 1))
    xi_b = jnp.broadcast_to(xi[:, :, None], (RET_HEADS, 1, RET_DV))
    zeta_b = jnp.broadcast_to(zeta[:, :, None], (RET_HEADS, 1, 1))
    gc_b = jnp.broadcast_to(g_chunk[:, None, None], (RET_HEADS, 1, RET_DV))
    cos2, sin2 = _rope_tables((pos0 + jnp.arange(1)).astype(_F32))

    in_specs = [
        _full((Bs, D_MODEL)),
        pl.BlockSpec((POOL_HIST, TB, POOL_IN), lambda t: (0, t, 0)),
        pl.BlockSpec((TB, RET_HEADS, RET_DK, RET_DV), lambda t: (t, 0, 0, 0)),
        _full((1, D_MODEL)), _full((D_MODEL, IN_WIDTH)), _full((POOL_GROUPS, POOL_GC, POOL_OUT_GC)),
        _full((1, D_MODEL)), _full((1, D_MODEL)), _full((D_MODEL, D_MODEL)), _full((1, D_MODEL)),
        _full((D_MODEL, ROUTER_W)),
        _full((1, RET_DK)), _full((1, RET_DK)),
        _full((RET_HEADS, 1, 1)), _full((RET_HEADS, 1, RET_DV)), _full((RET_HEADS, 1, 1)),
        _full((RET_HEADS, 1, RET_DV)),
        _full((1, LANES)),
    ]
    out_shape = (
        jax.ShapeDtypeStruct((Bs, D_MODEL), _F32),
        jax.ShapeDtypeStruct((Bs, LANES), jnp.int32),
        jax.ShapeDtypeStruct((Bs, LANES), _F32),
        jax.ShapeDtypeStruct((POOL_HIST, Bs, POOL_IN), _F32),
        jax.ShapeDtypeStruct((Bs, RET_HEADS, RET_DK, RET_DV), _F32),
        jax.ShapeDtypeStruct((1, LANES), _F32),
        jax.ShapeDtypeStruct((Bs,) + ROW_TILE, _F32),
    )
    out_specs = (
        _full((Bs, D_MODEL)), _full((Bs, LANES)), _full((Bs, LANES)),
        pl.BlockSpec((POOL_HIST, TB, POOL_IN), lambda t: (0, t, 0)),
        pl.BlockSpec((TB, RET_HEADS, RET_DK, RET_DV), lambda t: (t, 0, 0, 0)),
        _full((1, LANES)), _full((Bs,) + ROW_TILE),
    )
    f32 = lambda *s: pltpu.VMEM(s, _F32)
    return pl.pallas_call(
        functools.partial(_mixer_sample_kernel, cap=cap),
        grid=(Bs // TB,),
        in_specs=in_specs,
        out_specs=out_specs,
        out_shape=out_shape,
        scratch_shapes=[f32(Bs, POOL_IN), f32(Bs, QK_W), f32(Bs, QK_W), f32(RET_HEADS, RET_DK, Bs),
                        f32(RET_HEADS, RET_DK, Bs), f32(Bs, V_W), f32(Bs, D_MODEL), f32(Bs, D_MODEL),
                        f32(Bs, POOL_IN), f32(Bs, V_W), pltpu.VMEM((Bs, D_MODEL), _BF), f32(1, LANES)],
        compiler_params=pltpu.CompilerParams(dimension_semantics=("arbitrary",), vmem_limit_bytes=VMEM_LIMIT),
        name="mixer_sample",
    )(x, state_pool, state_ret, wts["g_mix"], wts["w_in"], wts["w_pool"], wts["pool_scale"], wts["ret_gn"],
      wts["w_out"], wts["g_ffn"], wts["w_rt"], cos2, sin2, dm_b, xi_b, zeta_b, gc_b, carry_in)


IDX_CHUNK = 1024
ISSUE_UNROLL = 8
SC_CORES = 2
SC_SUBCORES = 16
SC_WINDOW = 32


def _sc_scatter(h2, dst_rows, n_rows):
    T = h2.shape[0]
    n_workers = SC_CORES * SC_SUBCORES
    W = SC_WINDOW
    assert T % (n_workers * W) == 0
    per_w = T // n_workers
    n_win = per_w // W
    d0 = dst_rows[0].reshape(T // W, W)
    d1 = dst_rows[1].reshape(T // W, W)
    mesh = plsc.VectorSubcoreMesh(core_axis_name="c", subcore_axis_name="s")

    def body(h2_hbm, d0_hbm, d1_hbm, xs_hbm, i0_v, i1_v, rows_v, sem_load, sem_store):
        wid = lax.axis_index("s") * SC_CORES + lax.axis_index("c")
        base = wid * per_w
        pltpu.sync_copy(d0_hbm.at[pl.ds(wid * n_win, n_win)], i0_v)
        pltpu.sync_copy(d1_hbm.at[pl.ds(wid * n_win, n_win)], i1_v)
        pltpu.async_copy(h2_hbm.at[pl.ds(base, W)], rows_v.at[0], sem_load)
        for i in range(n_win):
            b = i % 2
            pltpu.make_async_copy(h2_hbm.at[pl.ds(base, W)], rows_v.at[b], sem_load).wait()
            if i + 1 < n_win:
                pltpu.async_copy(h2_hbm.at[pl.ds(base + (i + 1) * W, W)], rows_v.at[1 - b], sem_load)
            c0 = pltpu.async_copy(rows_v.at[b], xs_hbm.at[i0_v.at[i]], sem_store)
            c1 = pltpu.async_copy(rows_v.at[b], xs_hbm.at[i1_v.at[i]], sem_store)
            c0.wait()
            c1.wait()

    return pl.kernel(
        body, mesh=mesh,
        out_type=jax.ShapeDtypeStruct((n_rows,) + ROW_TILE, _F32),
        scratch_types=[pltpu.VMEM((n_win, W), jnp.int32), pltpu.VMEM((n_win, W), jnp.int32),
                       pltpu.VMEM((2, W) + ROW_TILE, _F32),
                       pltpu.SemaphoreType.DMA, pltpu.SemaphoreType.DMA],
        name="moe_sc_scatter",
    )(h2, d0, d1)


ZERO_CHUNK = 64


def _finish_dispatch_kernel(zstart_ref, zchunks_ref, slots_hbm, h2_ref, xs_in_hbm, xs_hbm, idx_smem, zbuf, sem_idx,
                            sem_rows, sem_zero):
    del xs_in_hbm
    TS = h2_ref.shape[0]
    idx_cp = pltpu.make_async_copy(slots_hbm.at[pl.ds(0, IDX_CHUNK)], idx_smem, sem_idx)
    idx_cp.start()
    zbuf[...] = jnp.zeros(zbuf.shape, zbuf.dtype)

    def zero_copy(e, j):
        return pltpu.make_async_copy(zbuf, xs_hbm.at[pl.ds(zstart_ref[e] + j * ZERO_CHUNK, ZERO_CHUNK)], sem_zero)

    for e in range(N_EXPERTS):
        lax.fori_loop(0, zchunks_ref[e], lambda j, c, e=e: (zero_copy(e, j).start(), c)[1], 0)
    idx_cp.wait()

    def issue(r, carry):
        for kk in range(2):
            pltpu.make_async_copy(h2_ref.at[r], xs_hbm.at[idx_smem[2 * r + kk]], sem_rows).start(priority=kk)
        return carry

    lax.fori_loop(0, TS, issue, 0, unroll=ISSUE_UNROLL)
    for e in range(N_EXPERTS):
        lax.fori_loop(0, zchunks_ref[e], lambda j, c, e=e: (zero_copy(e, j).wait(), c)[1], 0)
    for kk in range(2):
        pltpu.make_async_copy(h2_ref, xs_hbm.at[pl.ds(0, TS)], sem_rows).wait()


def _finish_dispatch(xs, h2s, slots, zstart, zchunks):
    Bs = h2s.shape[0]
    assert 2 * Bs <= IDX_CHUNK and slots.shape[0] == IDX_CHUNK and MOE_BLOCK % ZERO_CHUNK == 0
    any_spec = pl.BlockSpec(memory_space=pl.ANY)
    grid_spec = pltpu.PrefetchScalarGridSpec(
        num_scalar_prefetch=2,
        grid=(1,),
        in_specs=[any_spec, pl.BlockSpec((Bs,) + ROW_TILE, lambda t, z, n: (0, 0, 0)), any_spec],
        out_specs=any_spec,
        scratch_shapes=[pltpu.SMEM((IDX_CHUNK,), jnp.int32), pltpu.VMEM((ZERO_CHUNK,) + ROW_TILE, _F32),
                        pltpu.SemaphoreType.DMA, pltpu.SemaphoreType.DMA, pltpu.SemaphoreType.DMA],
    )
    return pl.pallas_call(
        _finish_dispatch_kernel,
        grid_spec=grid_spec,
        out_shape=jax.ShapeDtypeStruct(xs.shape, _F32),
        input_output_aliases={4: 0},
        compiler_params=pltpu.CompilerParams(dimension_semantics=("arbitrary",), has_side_effects=True),
        name="moe_finish_dispatch",
    )(zstart, zchunks, slots, h2s, xs)


def _ffn_kernel(bexp_ref, brow_ref, nused_ref, first_ref, slot_ref, next_ref,
                xs_ref, w1_hbm, w3_hbm, w2_hbm, ys_ref, w13_scr, w2_scr, st1, st3, st2, sem_w):
    del brow_ref
    i = pl.program_id(0)

    def weight_copies(e, s):
        return (pltpu.make_async_copy(w1_hbm.at[e], st1.at[s], sem_w.at[s]),
                pltpu.make_async_copy(w3_hbm.at[e], st3.at[s], sem_w.at[s]),
                pltpu.make_async_copy(w2_hbm.at[e], st2.at[s], sem_w.at[s]))

    @pl.when(i < nused_ref[0])
    def _():
        @pl.when(first_ref[i] == 1)
        def _():
            e = bexp_ref[i]
            s = slot_ref[i]

            @pl.when(i == 0)
            def _():
                for cp in weight_copies(e, s):
                    cp.start()

            for cp in weight_copies(e, s):
                cp.wait()
            @pl.when(next_ref[i] >= 0)
            def _():
                for cp in weight_copies(next_ref[i], 1 - s):
                    cp.start()

            w13_scr[:, 0:D_EXPERT] = st1[s].astype(_BF)
            w13_scr[:, D_EXPERT:2 * D_EXPERT] = st3[s].astype(_BF)
            w2_scr[...] = st2[s].astype(_BF)

        xb = xs_ref[...].reshape(MOE_BLOCK, D_MODEL).astype(_BF)
        ab = _dot(xb, w13_scr[...])
        hid = jax.nn.silu(ab[:, 0:D_EXPERT]) * ab[:, D_EXPERT:2 * D_EXPERT]
        ys_ref[...] = _dot(hid.astype(_BF), w2_scr[...]).reshape(ys_ref.shape)


def _ffn(xs, block_exp, block_row, n_used, run_first, run_slot, run_next, w1, w3, w2, n_blocks):
    blk = lambda i, *_: (_[1][i], 0, 0)
    any_spec = pl.BlockSpec(memory_space=pl.ANY)
    grid_spec = pltpu.PrefetchScalarGridSpec(
        num_scalar_prefetch=6,
        grid=(n_blocks,),
        in_specs=[pl.BlockSpec((MOE_BLOCK,) + ROW_TILE, blk), any_spec, any_spec, any_spec],
        out_specs=pl.BlockSpec((MOE_BLOCK,) + ROW_TILE, blk),
        scratch_shapes=[pltpu.VMEM((D_MODEL, 2 * D_EXPERT), _BF), pltpu.VMEM((D_EXPERT, D_MODEL), _BF),
                        pltpu.VMEM((2, D_MODEL, D_EXPERT), _F32), pltpu.VMEM((2, D_MODEL, D_EXPERT), _F32),
                        pltpu.VMEM((2, D_EXPERT, D_MODEL), _F32), pltpu.SemaphoreType.DMA((2,))],
    )
    return pl.pallas_call(
        _ffn_kernel,
        grid_spec=grid_spec,
        out_shape=jax.ShapeDtypeStruct(xs.shape, _F32),
        compiler_params=pltpu.CompilerParams(dimension_semantics=("arbitrary",), vmem_limit_bytes=VMEM_LIMIT),
        name="moe_ffn",
    )(block_exp, block_row, n_used, run_first, run_slot, run_next, xs, w1, w3, w2)


def _combine_kernel(slots_hbm, x1_ref, rw_ref, g_ref, ys_hbm, y_ref, idx_smem, buf, sem_idx, sem_rows):
    t = pl.program_id(0)
    TC = x1_ref.shape[0]
    idx_cp = pltpu.make_async_copy(slots_hbm.at[pl.ds(t * IDX_CHUNK, IDX_CHUNK)], idx_smem, sem_idx)
    idx_cp.start()
    idx_cp.wait()

    def issue(r, carry):
        for kk in range(2):
            pltpu.make_async_copy(ys_hbm.at[idx_smem[2 * r + kk]], buf.at[kk, r], sem_rows).start(priority=kk)
        return carry

    lax.fori_loop(0, TC, issue, 0, unroll=ISSUE_UNROLL)
    for kk in range(2):
        pltpu.make_async_copy(ys_hbm.at[pl.ds(0, TC)], buf.at[kk], sem_rows).wait()

    w = rw_ref[...]
    y0 = buf[0].reshape(TC, D_MODEL)
    y1 = buf[1].reshape(TC, D_MODEL)
    x2 = x1_ref[...] + (w[:, 0:1] * y0 + w[:, 1:2] * y1)
    y_ref[...] = _rms(x2, g_ref[...])


TC_GATHER_SHARE = 16
SC_GATHER_CHUNKS = 3
SC_GATHER_BUFFERS = 3
SC_GATHER_MAX_WINDOW = 40


def _sc_gather(ys, rows):
    Tg = rows.shape[1]
    n_workers = SC_CORES * SC_SUBCORES
    per_w = Tg // n_workers
    assert per_w * n_workers == Tg and per_w % SUBLANES == 0
    W = max(w for w in range(SUBLANES, SC_GATHER_MAX_WINDOW + 1, SUBLANES) if per_w % w == 0)
    n_win = per_w // W
    NB = SC_GATHER_BUFFERS
    idx = rows.reshape(2, n_workers, n_win, W)
    jobs = [(kk, i) for i in range(n_win) for kk in range(2)]
    mesh = plsc.VectorSubcoreMesh(core_axis_name="c", subcore_axis_name="s")

    def body(ys_hbm, idx_hbm, out_hbm, i_v, rows_v, sem_g, sem_s):
        wid = lax.axis_index("s") * SC_CORES + lax.axis_index("c")
        base = wid * per_w
        for kk in range(2):
            pltpu.sync_copy(idx_hbm.at[kk, wid], i_v.at[kk])

        def gather(j):
            kk, i = jobs[j]
            return pltpu.make_async_copy(ys_hbm.at[i_v.at[kk, i]], rows_v.at[j % NB], sem_g.at[j % NB])

        def store(j):
            kk, i = jobs[j]
            return pltpu.make_async_copy(rows_v.at[j % NB], out_hbm.at[kk, pl.ds(base + i * W, W)], sem_s.at[j % NB])

        gather(0).start()
        for j in range(len(jobs)):
            gather(j).wait()
            store(j).start()
            if j + 1 < len(jobs):
                if j + 1 >= NB:
                    store(j + 1 - NB).wait()
                gather(j + 1).start()
        for j in range(max(0, len(jobs) - NB), len(jobs)):
            store(j).wait()

    return pl.kernel(
        body, mesh=mesh,
        out_type=jax.ShapeDtypeStruct((2, Tg) + ROW_TILE, _F32),
        scratch_types=[pltpu.VMEM((2, n_win, W), jnp.int32), pltpu.VMEM((NB, W) + ROW_TILE, _F32),
                       pltpu.SemaphoreType.DMA((NB,)), pltpu.SemaphoreType.DMA((NB,))],
        name="moe_sc_gather",
    )(ys, idx)


def _combine_dense_kernel(x1_ref, rw_ref, g_ref, rows_ref, y_in_hbm, y_ref):
    del y_in_hbm
    TC = x1_ref.shape[0]
    w = rw_ref[...]
    y0 = rows_ref[0].reshape(TC, D_MODEL)
    y1 = rows_ref[1].reshape(TC, D_MODEL)
    x2 = x1_ref[...] + (w[:, 0:1] * y0 + w[:, 1:2] * y1)
    y_ref[...] = _rms(x2, g_ref[...])


def _combine_dense(x1, rw, rows, y, g_final, tile, first_tile):
    Tg = rows.shape[1]
    assert Tg % tile == 0
    tok = lambda t: (first_tile + t, 0)
    return pl.pallas_call(
        _combine_dense_kernel,
        grid=(Tg // tile,),
        in_specs=[pl.BlockSpec((tile, D_MODEL), tok), pl.BlockSpec((tile, LANES), tok), _full((1, D_MODEL)),
                  pl.BlockSpec((2, tile) + ROW_TILE, lambda t: (0, t, 0, 0)), pl.BlockSpec(memory_space=pl.ANY)],
        out_specs=pl.BlockSpec((tile, D_MODEL), tok),
        out_shape=jax.ShapeDtypeStruct(y.shape, _F32),
        input_output_aliases={4: 0},
        compiler_params=pltpu.CompilerParams(dimension_semantics=("arbitrary",), vmem_limit_bytes=VMEM_LIMIT),
        name="moe_combine_dense",
    )(x1, rw, g_final.reshape(1, D_MODEL), rows, y)


def _combine(x1, rw, slots, ys, g_final, tile, n_tiles):
    T = x1.shape[0]
    assert T % tile == 0 and 2 * tile <= IDX_CHUNK and slots.shape[0] == n_tiles * IDX_CHUNK
    any_spec = pl.BlockSpec(memory_space=pl.ANY)
    return pl.pallas_call(
        _combine_kernel,
        grid=(n_tiles,),
        in_specs=[any_spec, pl.BlockSpec((tile, D_MODEL), lambda t: (t, 0)),
                  pl.BlockSpec((tile, LANES), lambda t: (t, 0)), _full((1, D_MODEL)), any_spec],
        out_specs=pl.BlockSpec((tile, D_MODEL), lambda t: (t, 0)),
        out_shape=jax.ShapeDtypeStruct((T, D_MODEL), _F32),
        scratch_shapes=[pltpu.SMEM((IDX_CHUNK,), jnp.int32), pltpu.VMEM((2, tile) + ROW_TILE, _F32),
                        pltpu.SemaphoreType.DMA, pltpu.SemaphoreType.DMA],
        compiler_params=pltpu.CompilerParams(dimension_semantics=("arbitrary",), vmem_limit_bytes=VMEM_LIMIT),
        name="moe_combine",
    )(slots, x1, rw, g_final.reshape(1, D_MODEL), ys)


def _tile_for(n_tokens):
    tile = IDX_CHUNK // 2
    return tile if n_tokens % tile == 0 else n_tokens


def _chunked_slots(slot, tile):
    n_tiles = slot.shape[0] // tile
    s = slot.reshape(n_tiles, 2 * tile)
    return jnp.pad(s, ((0, 0), (0, IDX_CHUNK - 2 * tile))).reshape(-1)


def kernel(x_prompt, x_sample, state_pool, state_ret, g_mix, w_in, w_pool, pool_scale, ret_gn, w_out, g_ffn, w_grp, w_exp, w1, w3, w2, g_final):
    Bp, Lp, _ = x_prompt.shape
    Bs = x_sample.shape[0]
    Tp = Bp * Lp
    wts = _prep_weights(g_mix[0], w_in[0], w_pool[0], pool_scale[0], ret_gn[0], w_out[0], g_ffn[0], w_grp[0], w_exp[0])

    T_all = Tp + Bs
    cap = (-(-T_all // MOE_BLOCK) + 1) * MOE_BLOCK
    cap_blocks = cap // MOE_BLOCK

    x1p, dst_p, dst_rows_p, rwp, npool_p, nret_p, counts_p, h2p = _mixer_prompt(x_prompt, wts, cap)
    xs = _sc_scatter(h2p, dst_rows_p, N_EXPERTS * cap)
    x1s, dst_s, rws, npool_s, nret_s, counts, h2s = _mixer_sample(
        x_sample.reshape(Bs, D_MODEL), jnp.swapaxes(state_pool[0], 0, 1), state_ret[0], wts, PAST_LEN, counts_p, cap)

    counts = counts[0, :N_EXPERTS].astype(jnp.int32)
    tile_p, tile_s = _tile_for(Tp), _tile_for(Bs)
    slots_p = _chunked_slots(dst_p[:, :2], tile_p)
    slots_s = _chunked_slots(dst_s[:, :2], tile_s)
    zstart = jnp.arange(N_EXPERTS, dtype=jnp.int32) * cap + counts
    zchunks = ((-counts) % MOE_BLOCK + ZERO_CHUNK - 1) // ZERO_CHUNK
    xs = _finish_dispatch(xs, h2s, slots_s, zstart, zchunks.astype(jnp.int32))
    nblk = (counts + MOE_BLOCK - 1) // MOE_BLOCK
    blk_end = jnp.cumsum(nblk)
    n_used = blk_end[-1:].astype(jnp.int32)
    n_blocks = -(-2 * T_all // MOE_BLOCK) + N_EXPERTS
    step = jnp.minimum(jnp.arange(n_blocks, dtype=jnp.int32), n_used - 1)
    block_exp = jnp.minimum(jnp.sum((blk_end[None, :] <= step[:, None]).astype(jnp.int32), axis=1), N_EXPERTS - 1)
    blk_start = jnp.sum(jnp.where(block_exp[:, None] == jnp.arange(N_EXPERTS, dtype=jnp.int32), (blk_end - nblk)[None, :], 0),
                        axis=1)
    block_row = (block_exp * cap_blocks + (step - blk_start)).astype(jnp.int32)
    block_exp = block_exp.astype(jnp.int32)
    experts = jnp.arange(N_EXPERTS, dtype=jnp.int32)
    nonempty = nblk > 0
    run_first = (step == blk_start).astype(jnp.int32)
    run_slot = (jnp.sum((nonempty[None, :] & (experts[None, :] < block_exp[:, None])).astype(jnp.int32), axis=1) % 2)
    later = nonempty[None, :] & (experts[None, :] > block_exp[:, None])
    run_next = jnp.min(jnp.where(later, experts[None, :], N_EXPERTS), axis=1)
    run_next = jnp.where(run_next == N_EXPERTS, -1, run_next).astype(jnp.int32)

    ys = _ffn(xs, block_exp, block_row, n_used, run_first, run_slot.astype(jnp.int32), run_next,
              w1[0], w3[0], w2[0], n_blocks)
    n_tiles_p = Tp // tile_p
    n_tc = max(1, n_tiles_p // TC_GATHER_SHARE)
    n_sc = n_tiles_p - n_tc
    chunk_tiles = [n_sc // SC_GATHER_CHUNKS + (k < n_sc % SC_GATHER_CHUNKS) for k in range(SC_GATHER_CHUNKS)]
    gathered, first = [], n_tc
    for nt in chunk_tiles:
        if nt:
            gathered.append((first, _sc_gather(ys, dst_rows_p[:, first * tile_p:(first + nt) * tile_p])))
            first += nt
    y_s = _combine(x1s, rws, slots_s, ys, g_final, tile_s, 1)
    y_p = _combine(x1p, rwp, slots_p[:n_tc * IDX_CHUNK], ys, g_final, tile_p, n_tc)
    for first, rows in gathered:
        y_p = _combine_dense(x1p, rwp, rows, y_p, g_final, tile_p, first)

    return (y_p.reshape(Bp, Lp, D_MODEL), y_s.reshape(Bs, 1, D_MODEL),
            npool_p[None], nret_p[None], jnp.swapaxes(npool_s, 0, 1)[None], nret_s[None])
```

```python
import functools

import jax
import jax.numpy as jnp
from jax import lax
from jax.experimental import pallas as pl
from jax.experimental.pallas import tpu as pltpu
from jax.experimental.pallas import tpu_sc as plsc

D_MODEL = 1024
EPS = 1e-6
POOL_GROUPS = 4
POOL_IN = D_MODEL // 2
POOL_GC = POOL_IN // POOL_GROUPS
POOL_OUT_GC = D_MODEL // POOL_GROUPS
POOL_WINDOWS = (2, 4, 8, 16)
POOL_HIST = max(POOL_WINDOWS) - 1
RET_HEADS = 4
RET_DK = D_MODEL // 8
RET_DV = D_MODEL // RET_HEADS
ROPE_BASE = 10000.0
PAST_LEN = 16384
N_GROUPS = 4
EXPERTS_PER_GROUP = 8
N_EXPERTS = N_GROUPS * EXPERTS_PER_GROUP
D_EXPERT = D_MODEL // 4
QK_W = RET_HEADS * RET_DK
V_W = RET_HEADS * RET_DV
OFF_U = 0
OFF_Q = POOL_IN
OFF_K = OFF_Q + QK_W
OFF_V = OFF_K + QK_W
OFF_GA = OFF_V + V_W
OFF_GB = OFF_GA + D_MODEL
IN_WIDTH = OFF_GB + D_MODEL

LANES = 128
SUBLANES = 8
ROW_WORDS = D_MODEL // 2
ROW_TILE = (ROW_WORDS // LANES, LANES)
ROW_DTYPE = jnp.uint32
HALO = 16
RET_CHUNK = 256
MIXER_SEQS = 2
MOE_BLOCK = 512
ROUTER_W = LANES
VMEM_LIMIT = 56 * 1024 * 1024

_BF = jnp.bfloat16
_F32 = jnp.float32


def _rms(x, g):
    inv = lax.rsqrt(jnp.mean(x * x, axis=-1, keepdims=True) + EPS)
    return x * inv * g


def _dot(a, b):
    return jnp.dot(a, b, preferred_element_type=_F32)


def _round_bf16(x):
    return x.astype(_BF).astype(_F32)


def _pack_rows(x):
    lo = lax.bitcast_convert_type(x[:, :ROW_WORDS].astype(_BF).astype(_F32), jnp.uint32)
    hi = lax.bitcast_convert_type(x[:, ROW_WORDS:].astype(_BF).astype(_F32), jnp.uint32)
    return ((lo >> 16) | hi).reshape((x.shape[0],) + ROW_TILE)


def _unpack_rows(w):
    w = w.reshape(w.shape[0], ROW_WORDS)
    lo = lax.bitcast_convert_type(w << 16, _F32)
    hi = lax.bitcast_convert_type(w & jnp.uint32(0xFFFF0000), _F32)
    return jnp.concatenate([lo, hi], axis=1)


def _rotary(x, cos2, sin2):
    return x * cos2 + pltpu.roll(x, RET_DK // 2, 1) * sin2


def _route(logits):
    lane = lax.broadcasted_iota(jnp.int32, logits.shape, 1).astype(_F32)
    neg = jnp.float32(-jnp.inf)
    big = jnp.float32(1 << 20)
    lg = jnp.where(lane < N_GROUPS, logits, neg)
    mg = jnp.max(lg, axis=-1, keepdims=True)
    g_idx = jnp.min(jnp.where(lg == mg, lane, big), axis=-1, keepdims=True)
    p_g = 1.0 / jnp.sum(jnp.exp(lg - mg), axis=-1, keepdims=True)
    lo = N_GROUPS + g_idx * EXPERTS_PER_GROUP
    in_grp = (lane >= lo) & (lane < lo + EXPERTS_PER_GROUP)
    le = jnp.where(in_grp, logits, neg)
    m1 = jnp.max(le, axis=-1, keepdims=True)
    i1 = jnp.min(jnp.where(le == m1, lane, big), axis=-1, keepdims=True)
    le2 = jnp.where(lane == i1, neg, le)
    m2 = jnp.max(le2, axis=-1, keepdims=True)
    i2 = jnp.min(jnp.where(le2 == m2, lane, big), axis=-1, keepdims=True)
    t = jnp.exp(m2 - m1)
    den = 1.0 + t
    e0 = (i1 - N_GROUPS).astype(jnp.int32)
    e1 = (i2 - N_GROUPS).astype(jnp.int32)
    return e0, e1, p_g * (1.0 / den), p_g * (t / den)


def _post_mix(x, mixed_ref, w_out_ref, g_ffn_ref, w_rt_ref, x1_ref, rw_ref):
    x1 = x + _dot(mixed_ref[...], w_out_ref[...])
    x1_ref[...] = x1.reshape(x1_ref.shape)
    h2 = _rms(x1, g_ffn_ref[...])
    e0, e1, w0, w1 = _route(_dot(h2.astype(_BF), w_rt_ref[...]))
    lane = lax.broadcasted_iota(jnp.int32, (x.shape[0], LANES), 1)
    rw_ref[...] = jnp.where(lane == 0, w0, jnp.where(lane == 1, w1, 0.0)).reshape(rw_ref.shape)
    return h2, e0, e1


def _rank_rows(e0, e1, cap, carry_scr):
    R = e0.shape[0]
    lane = lax.broadcasted_iota(jnp.int32, (R, LANES), 1)
    m0 = lane == e0
    m1 = lane == e1
    onehot = jnp.where(m0 | m1, 1.0, 0.0)
    r_i = lax.broadcasted_iota(jnp.int32, (R, R), 0)
    c_i = lax.broadcasted_iota(jnp.int32, (R, R), 1)
    tri = jnp.where(c_i < r_i, 1.0, 0.0).astype(_BF)
    before = _dot(tri, onehot.astype(_BF)) + carry_scr[...]
    d0 = e0.astype(_F32) * cap + jnp.sum(jnp.where(m0, before, 0.0), axis=-1, keepdims=True)
    d1 = e1.astype(_F32) * cap + jnp.sum(jnp.where(m1, before, 0.0), axis=-1, keepdims=True)
    carry_scr[...] += jnp.sum(onehot, axis=0, keepdims=True)
    return jnp.where(lane == 0, d0, jnp.where(lane == 1, d1, 0.0))


def _group_norm(o, gain):
    mu = jnp.mean(o, axis=-1, keepdims=True)
    d = o - mu
    var = jnp.mean(d * d, axis=-1, keepdims=True)
    return d * lax.rsqrt(var + EPS) * gain


def _mixer_prompt_kernel(x_ref, g_mix_ref, w_in_ref, w_pool_ref, pscale_ref, gn_ref, w_out_ref, g_ffn_ref,
                         w_rt_ref, cos_ref, sin_ref, dmask_ref, xi_ref, zeta_ref, gc_ref,
                         x1_ref, dst_ref, dst_t_ref, rw_ref, npool_ref, nret_ref, cnt_ref, h2_ref,
                         u_scr, s_scr, mixed_scr, carry_scr, *, cap):
    c = pl.program_id(1)
    n_c = pl.num_programs(1)
    @pl.when((pl.program_id(0) == 0) & (c == 0))
    def _():
        carry_scr[...] = jnp.zeros(carry_scr.shape, _F32)
    NSEQ, C, _ = x_ref.shape
    R = NSEQ * C

    @pl.when(c == 0)
    def _():
        u_scr[:, 0:HALO, :] = jnp.zeros((NSEQ, HALO, POOL_IN), _F32)
        s_scr[...] = jnp.zeros(s_scr.shape, _F32)

    x = x_ref[...].reshape(R, D_MODEL)
    h = _rms(x, g_mix_ref[...]).astype(_BF)

    u_scr[:, HALO:HALO + C, :] = _dot(h, w_in_ref[:, OFF_U:OFF_U + POOL_IN]).reshape(NSEQ, C, POOL_IN)
    q = _dot(h, w_in_ref[:, OFF_Q:OFF_Q + QK_W])
    k = _dot(h, w_in_ref[:, OFF_K:OFF_K + QK_W])
    cos2 = jnp.concatenate([cos_ref[...]] * NSEQ, axis=0)
    sin2 = jnp.concatenate([sin_ref[...]] * NSEQ, axis=0)
    pos1 = (c * C + 1 + lax.broadcasted_iota(jnp.int32, (C, POOL_GC), 0)).astype(_F32)

    for j in range(RET_HEADS):
        win = POOL_WINDOWS[j]
        cs = slice(j * POOL_GC, (j + 1) * POOL_GC)
        n_rows = jnp.minimum(pos1, jnp.float32(win))
        pooled = []
        for sq in range(NSEQ):
            u_j = u_scr[sq, HALO:HALO + C, cs]
            s = u_j
            for d in range(1, win):
                s = s + u_scr[sq, HALO - d:HALO - d + C, cs]
            pooled.append(s / n_rows - u_j)
        osl = slice(j * POOL_OUT_GC, (j + 1) * POOL_OUT_GC)
        pool_out = _dot(jnp.concatenate(pooled, axis=0).astype(_BF), w_pool_ref[j]) * pscale_ref[:, osl]

        qs = slice(j * RET_DK, (j + 1) * RET_DK)
        qb = _rotary(q[:, qs], cos2, sin2).astype(_BF)
        kf = _rotary(k[:, qs], cos2, sin2) * (RET_DK ** -0.5)
        kb = kf.astype(_BF)
        vb = _dot(h, w_in_ref[:, OFF_V + j * RET_DV:OFF_V + (j + 1) * RET_DV]).astype(_BF)
        ret = []
        for sq in range(NSEQ):
            rows = slice(sq * C, (sq + 1) * C)
            scores = lax.dot_general(qb[rows], kb[rows], (((1,), (1,)), ((), ())),
                                     preferred_element_type=_F32) * dmask_ref[j]
            s_old = s_scr[sq, j]
            o = _dot(scores.astype(_BF), vb[rows]) + _dot(qb[rows], s_old.astype(_BF)) * xi_ref[j]
            kz = (kf[rows] * zeta_ref[j]).astype(_BF)
            s_scr[sq, j] = gc_ref[j] * s_old + lax.dot_general(kz, vb[rows], (((0,), (0,)), ((), ())),
                                                               preferred_element_type=_F32)
            ret.append(_group_norm(o, gn_ref[:, osl]))
        ret_out = jnp.concatenate(ret, axis=0)

        ga = _dot(h, w_in_ref[:, OFF_GA + j * RET_DV:OFF_GA + (j + 1) * RET_DV])
        gb = _dot(h, w_in_ref[:, OFF_GB + j * RET_DV:OFF_GB + (j + 1) * RET_DV])
        mixed_scr[:, osl] = (jax.nn.sigmoid(ga) * pool_out + jax.nn.sigmoid(gb) * ret_out).astype(_BF)

    h2, e0, e1 = _post_mix(x, mixed_scr, w_out_ref, g_ffn_ref, w_rt_ref, x1_ref, rw_ref)

    h2_ref[...] = _pack_rows(h2).reshape(h2_ref.shape)
    dst = _rank_rows(e0, e1, cap, carry_scr)
    dst_ref[...] = dst.astype(jnp.int32).reshape(dst_ref.shape)
    dst_t = dst.T[0:SUBLANES, :].astype(jnp.int32)
    for sq in range(NSEQ):
        dst_t_ref[sq] = dst_t[:, sq * C:(sq + 1) * C]
    cnt_ref[...] = carry_scr[...]

    u_scr[:, 0:HALO, :] = u_scr[:, C:C + HALO, :]

    @pl.when(c == n_c - 1)
    def _():
        npool_ref[...] = u_scr[:, HALO + C - POOL_HIST:HALO + C, :]
        nret_ref[...] = s_scr[...]


def _decay_tables(C):
    log_g = jnp.log(1.0 - jnp.exp2(-5.0 - jnp.arange(RET_HEADS, dtype=_F32)))
    i = jnp.arange(C, dtype=_F32)
    diff = i[:, None] - i[None, :]
    dmask = jnp.where(diff >= 0, jnp.exp(jnp.maximum(diff, 0.0)[None] * log_g[:, None, None]), 0.0)
    xi = jnp.exp((i[None, :] + 1.0) * log_g[:, None])
    zeta = jnp.exp((C - 1.0 - i)[None, :] * log_g[:, None])
    g_chunk = jnp.exp(C * log_g)
    return dmask, xi, zeta, g_chunk


def _rope_tables(pos):
    half = RET_DK // 2
    freqs = ROPE_BASE ** (-jnp.arange(half, dtype=_F32) / half)
    ang = pos[:, None] * freqs[None, :]
    cos, sin = jnp.cos(ang), jnp.sin(ang)
    return jnp.concatenate([cos, cos], axis=-1), jnp.concatenate([-sin, sin], axis=-1)


def _full(shape):
    n = len(shape)
    return pl.BlockSpec(shape, lambda *_: (0,) * n)


def _mixer_prompt(x, wts, cap):
    B, L, _ = x.shape
    C = RET_CHUNK if L % RET_CHUNK == 0 else L
    n_c = L // C
    T = B * L
    dmask, xi, zeta, g_chunk = _decay_tables(C)
    xi_b = jnp.broadcast_to(xi[:, :, None], (RET_HEADS, C, RET_DV))
    zeta_b = jnp.broadcast_to(zeta[:, :, None], (RET_HEADS, C, RET_DK))
    gc_b = jnp.broadcast_to(g_chunk[:, None, None], (RET_HEADS, 1, RET_DV))
    cos2, sin2 = _rope_tables(jnp.arange(L).astype(_F32))

    NSEQ = MIXER_SEQS if B % MIXER_SEQS == 0 else 1
    B2 = B // NSEQ
    T2 = T // NSEQ
    x = x.reshape(NSEQ, B2, L, D_MODEL)
    tok = lambda b, c: (0, b * n_c + c, 0)
    in_specs = [
        pl.BlockSpec((NSEQ, None, C, D_MODEL), lambda b, c: (0, b, c, 0)),
        _full((1, D_MODEL)), _full((D_MODEL, IN_WIDTH)), _full((POOL_GROUPS, POOL_GC, POOL_OUT_GC)),
        _full((1, D_MODEL)), _full((1, D_MODEL)), _full((D_MODEL, D_MODEL)), _full((1, D_MODEL)),
        _full((D_MODEL, ROUTER_W)),
        pl.BlockSpec((C, RET_DK), lambda b, c: (c, 0)), pl.BlockSpec((C, RET_DK), lambda b, c: (c, 0)),
        _full((RET_HEADS, C, C)), _full((RET_HEADS, C, RET_DV)), _full((RET_HEADS, C, RET_DK)),
        _full((RET_HEADS, 1, RET_DV)),
    ]
    R = NSEQ * C
    out_shape = (
        jax.ShapeDtypeStruct((NSEQ, T2, D_MODEL), _F32),
        jax.ShapeDtypeStruct((NSEQ, T2, LANES), jnp.int32),
        jax.ShapeDtypeStruct((NSEQ, SUBLANES, T2), jnp.int32),
        jax.ShapeDtypeStruct((NSEQ, T2, LANES), _F32),
        jax.ShapeDtypeStruct((NSEQ, B2, POOL_HIST, POOL_IN), _F32),
        jax.ShapeDtypeStruct((NSEQ, B2, RET_HEADS, RET_DK, RET_DV), _F32),
        jax.ShapeDtypeStruct((1, LANES), _F32),
        jax.ShapeDtypeStruct((NSEQ, T2) + ROW_TILE, ROW_DTYPE),
    )
    out_specs = (
        pl.BlockSpec((NSEQ, C, D_MODEL), tok),
        pl.BlockSpec((NSEQ, C, LANES), tok),
        pl.BlockSpec((NSEQ, SUBLANES, C), lambda b, c: (0, 0, b * n_c + c)),
        pl.BlockSpec((NSEQ, C, LANES), tok),
        pl.BlockSpec((NSEQ, None, POOL_HIST, POOL_IN), lambda b, c: (0, b, 0, 0)),
        pl.BlockSpec((NSEQ, None, RET_HEADS, RET_DK, RET_DV), lambda b, c: (0, b, 0, 0, 0)),
        _full((1, LANES)),
        pl.BlockSpec((NSEQ, C) + ROW_TILE, lambda b, c: (0, b * n_c + c, 0, 0)),
    )
    x1, dst, dst_t, rw, npool, nret, counts, h2 = pl.pallas_call(
        functools.partial(_mixer_prompt_kernel, cap=cap),
        grid=(B2, n_c),
        in_specs=in_specs,
        out_specs=out_specs,
        out_shape=out_shape,
        scratch_shapes=[pltpu.VMEM((NSEQ, HALO + C, POOL_IN), _F32),
                        pltpu.VMEM((NSEQ, RET_HEADS, RET_DK, RET_DV), _F32),
                        pltpu.VMEM((R, D_MODEL), _BF),
                        pltpu.VMEM((1, LANES), _F32)],
        compiler_params=pltpu.CompilerParams(dimension_semantics=("arbitrary", "arbitrary"),
                                             vmem_limit_bytes=VMEM_LIMIT),
        name="mixer_prompt",
    )(x, wts["g_mix"], wts["w_in"], wts["w_pool"], wts["pool_scale"], wts["ret_gn"], wts["w_out"],
      wts["g_ffn"], wts["w_rt"], cos2, sin2, dmask, xi_b, zeta_b, gc_b)
    dst_rows = jnp.moveaxis(dst_t[:, 0:2, :], 1, 0).reshape(2, T)
    return (x1.reshape(T, D_MODEL), dst.reshape(T, LANES), dst_rows, rw.reshape(T, LANES),
            npool.reshape(B, POOL_HIST, POOL_IN), nret.reshape(B, RET_HEADS, RET_DK, RET_DV), counts,
            h2.reshape((T,) + ROW_TILE))


def _prep_weights(g_mix, w_in, w_pool, pool_scale, ret_gn, w_out, g_ffn, w_grp, w_exp):
    w_rt = jnp.concatenate([w_grp, w_exp.reshape(D_MODEL, N_EXPERTS)], axis=1)
    w_rt = jnp.pad(w_rt, ((0, 0), (0, ROUTER_W - w_rt.shape[1])))
    row = lambda v: v.reshape(1, D_MODEL)
    return dict(g_mix=row(g_mix), w_in=w_in.astype(_BF), w_pool=w_pool.astype(_BF), pool_scale=row(pool_scale),
                ret_gn=row(ret_gn), w_out=w_out.astype(_BF), g_ffn=row(g_ffn), w_rt=w_rt.astype(_BF))


SAMPLE_TB = 8


def _mixer_sample_kernel(x_ref, spool_ref, sret_ref, g_mix_ref, w_in_ref, w_pool_ref, pscale_ref, gn_ref,
                         w_out_ref, g_ffn_ref, w_rt_ref, cos_ref, sin_ref, dm_ref, xi_ref, zeta_ref, gc_ref,
                         carry_in_ref,
                         x1_ref, dst_ref, rw_ref, npool_ref, nret_ref, cnt_ref, h2_ref,
                         u_scr, q_scr, k_scr, qt_scr, kt_scr, v_scr, ga_scr, gb_scr, pooled_scr, o_scr, mixed_scr,
                         carry_scr, *, cap):
    t = pl.program_id(0)
    n_t = pl.num_programs(0)
    Bs = x_ref.shape[0]
    TB = sret_ref.shape[0]

    @pl.when(t == 0)
    def _():
        h = _rms(x_ref[...], g_mix_ref[...]).astype(_BF)
        u_scr[...] = _dot(h, w_in_ref[:, OFF_U:OFF_U + POOL_IN])
        q = _dot(h, w_in_ref[:, OFF_Q:OFF_Q + QK_W])
        k = _dot(h, w_in_ref[:, OFF_K:OFF_K + QK_W])
        for j in range(RET_HEADS):
            qs = slice(j * RET_DK, (j + 1) * RET_DK)
            qf = _rotary(q[:, qs], cos_ref[...], sin_ref[...])
            kf = _rotary(k[:, qs], cos_ref[...], sin_ref[...]) * (RET_DK ** -0.5)
            q_scr[:, qs] = _round_bf16(qf)
            k_scr[:, qs] = _round_bf16(kf)
            qt_scr[j] = _round_bf16(qf).T
            kt_scr[j] = kf.T
        v_scr[...] = _round_bf16(_dot(h, w_in_ref[:, OFF_V:OFF_V + V_W]))
        ga_scr[...] = _dot(h, w_in_ref[:, OFF_GA:OFF_GA + D_MODEL])
        gb_scr[...] = _dot(h, w_in_ref[:, OFF_GB:OFF_GB + D_MODEL])

    shift = (Bs - t * TB) % Bs
    qt = [pltpu.roll(qt_scr[j], shift, 1) for j in range(RET_HEADS)]
    kt = [pltpu.roll(kt_scr[j], shift, 1) for j in range(RET_HEADS)]

    blk = pl.ds(pl.multiple_of(t * TB, TB), TB)
    u_blk = u_scr[blk, :]
    q_blk = q_scr[blk, :]
    k_blk = k_scr[blk, :]
    v_blk = v_scr[blk, :]
    score = [_round_bf16(jnp.sum(q_blk[:, j * RET_DK:(j + 1) * RET_DK] * k_blk[:, j * RET_DK:(j + 1) * RET_DK],
                                 axis=-1, keepdims=True) * dm_ref[j]) for j in range(RET_HEADS)]

    groups = []
    for g, win in enumerate(POOL_WINDOWS):
        cs = slice(g * POOL_GC, (g + 1) * POOL_GC)
        s = u_blk[:, cs]
        for r in range(POOL_HIST - (win - 1), POOL_HIST):
            s = s + spool_ref[r, :, cs]
        groups.append(s / jnp.float32(win) - u_blk[:, cs])
    pooled_scr[blk, :] = jnp.concatenate(groups, axis=1)
    npool_ref[0:POOL_HIST - 1] = spool_ref[1:POOL_HIST]
    npool_ref[POOL_HIST - 1] = u_blk

    o_rows = []
    for i in range(TB):
        heads = []
        for j in range(RET_HEADS):
            s_old = sret_ref[i, j]
            v_row = v_blk[i:i + 1, j * RET_DV:(j + 1) * RET_DV]
            qcol = qt[j][:, i:i + 1]
            kcol = kt[j][:, i:i + 1]
            qs_old = jnp.sum(qcol * _round_bf16(s_old), axis=0, keepdims=True)
            heads.append(score[j][i:i + 1, :] * v_row + qs_old * xi_ref[j])
            nret_ref[i, j] = gc_ref[j] * s_old + _round_bf16(kcol * zeta_ref[j]) * v_row
        o_rows.append(jnp.concatenate(heads, axis=1))
    o_scr[blk, :] = jnp.concatenate(o_rows, axis=0)

    @pl.when(t == n_t - 1)
    def _():
        for j in range(RET_HEADS):
            cs = slice(j * POOL_GC, (j + 1) * POOL_GC)
            osl = slice(j * POOL_OUT_GC, (j + 1) * POOL_OUT_GC)
            pool_out = _dot(pooled_scr[:, cs].astype(_BF), w_pool_ref[j]) * pscale_ref[:, osl]
            ret_out = _group_norm(o_scr[:, osl], gn_ref[:, osl])
            mixed_scr[:, osl] = (jax.nn.sigmoid(ga_scr[:, osl]) * pool_out
                                 + jax.nn.sigmoid(gb_scr[:, osl]) * ret_out).astype(_BF)
        h2, e0, e1 = _post_mix(x_ref[...], mixed_scr, w_out_ref, g_ffn_ref, w_rt_ref, x1_ref, rw_ref)
        h2_ref[...] = _pack_rows(h2).reshape(h2_ref.shape)
        carry_scr[...] = carry_in_ref[...]
        dst_ref[...] = _rank_rows(e0, e1, cap, carry_scr).astype(jnp.int32)
        cnt_ref[...] = carry_scr[...]


def _mixer_sample(x, state_pool, state_ret, wts, pos0, carry_in, cap):
    Bs = x.shape[0]
    assert Bs == LANES and Bs % SAMPLE_TB == 0
    TB = SAMPLE_TB
    dmask, xi, zeta, g_chunk = _decay_tables(1)
    dm_b = jnp.broadcast_to(dmask, (RET_HEADS, 1, 1))
    xi_b = jnp.broadcast_to(xi[:, :, None], (RET_HEADS, 1, RET_DV))
    zeta_b = jnp.broadcast_to(zeta[:, :, None], (RET_HEADS, 1, 1))
    gc_b = jnp.broadcast_to(g_chunk[:, None, None], (RET_HEADS, 1, RET_DV))
    cos2, sin2 = _rope_tables((pos0 + jnp.arange(1)).astype(_F32))

    in_specs = [
        _full((Bs, D_MODEL)),
        pl.BlockSpec((POOL_HIST, TB, POOL_IN), lambda t: (0, t, 0)),
        pl.BlockSpec((TB, RET_HEADS, RET_DK, RET_DV), lambda t: (t, 0, 0, 0)),
        _full((1, D_MODEL)), _full((D_MODEL, IN_WIDTH)), _full((POOL_GROUPS, POOL_GC, POOL_OUT_GC)),
        _full((1, D_MODEL)), _full((1, D_MODEL)), _full((D_MODEL, D_MODEL)), _full((1, D_MODEL)),
        _full((D_MODEL, ROUTER_W)),
        _full((1, RET_DK)), _full((1, RET_DK)),
        _full((RET_HEADS, 1, 1)), _full((RET_HEADS, 1, RET_DV)), _full((RET_HEADS, 1, 1)),
        _full((RET_HEADS, 1, RET_DV)),
        _full((1, LANES)),
    ]
    out_shape = (
        jax.ShapeDtypeStruct((Bs, D_MODEL), _F32),
        jax.ShapeDtypeStruct((Bs, LANES), jnp.int32),
        jax.ShapeDtypeStruct((Bs, LANES), _F32),
        jax.ShapeDtypeStruct((POOL_HIST, Bs, POOL_IN), _F32),
        jax.ShapeDtypeStruct((Bs, RET_HEADS, RET_DK, RET_DV), _F32),
        jax.ShapeDtypeStruct((1, LANES), _F32),
        jax.ShapeDtypeStruct((Bs,) + ROW_TILE, ROW_DTYPE),
    )
    out_specs = (
        _full((Bs, D_MODEL)), _full((Bs, LANES)), _full((Bs, LANES)),
        pl.BlockSpec((POOL_HIST, TB, POOL_IN), lambda t: (0, t, 0)),
        pl.BlockSpec((TB, RET_HEADS, RET_DK, RET_DV), lambda t: (t, 0, 0, 0)),
        _full((1, LANES)), _full((Bs,) + ROW_TILE),
    )
    f32 = lambda *s: pltpu.VMEM(s, _F32)
    return pl.pallas_call(
        functools.partial(_mixer_sample_kernel, cap=cap),
        grid=(Bs // TB,),
        in_specs=in_specs,
        out_specs=out_specs,
        out_shape=out_shape,
        scratch_shapes=[f32(Bs, POOL_IN), f32(Bs, QK_W), f32(Bs, QK_W), f32(RET_HEADS, RET_DK, Bs),
                        f32(RET_HEADS, RET_DK, Bs), f32(Bs, V_W), f32(Bs, D_MODEL), f32(Bs, D_MODEL),
                        f32(Bs, POOL_IN), f32(Bs, V_W), pltpu.VMEM((Bs, D_MODEL), _BF), f32(1, LANES)],
        compiler_params=pltpu.CompilerParams(dimension_semantics=("arbitrary",), vmem_limit_bytes=VMEM_LIMIT),
        name="mixer_sample",
    )(x, state_pool, state_ret, wts["g_mix"], wts["w_in"], wts["w_pool"], wts["pool_scale"], wts["ret_gn"],
      wts["w_out"], wts["g_ffn"], wts["w_rt"], cos2, sin2, dm_b, xi_b, zeta_b, gc_b, carry_in)


IDX_CHUNK = 1024
ISSUE_UNROLL = 8
SC_CORES = 2
SC_SUBCORES = 16
SC_WINDOW = 64


def _sc_scatter(h2, dst_rows, n_rows):
    T = h2.shape[0]
    n_workers = SC_CORES * SC_SUBCORES
    W = SC_WINDOW
    assert T % (n_workers * W) == 0
    per_w = T // n_workers
    n_win = per_w // W
    d0 = dst_rows[0].reshape(T // W, W)
    d1 = dst_rows[1].reshape(T // W, W)
    mesh = plsc.VectorSubcoreMesh(core_axis_name="c", subcore_axis_name="s")

    def body(h2_hbm, d0_hbm, d1_hbm, xs_hbm, i0_v, i1_v, rows_v, sem_load, sem_store):
        wid = lax.axis_index("s") * SC_CORES + lax.axis_index("c")
        base = wid * per_w
        pltpu.sync_copy(d0_hbm.at[pl.ds(wid * n_win, n_win)], i0_v)
        pltpu.sync_copy(d1_hbm.at[pl.ds(wid * n_win, n_win)], i1_v)
        pltpu.async_copy(h2_hbm.at[pl.ds(base, W)], rows_v.at[0], sem_load)
        for i in range(n_win):
            b = i % 2
            pltpu.make_async_copy(h2_hbm.at[pl.ds(base, W)], rows_v.at[b], sem_load).wait()
            if i + 1 < n_win:
                pltpu.async_copy(h2_hbm.at[pl.ds(base + (i + 1) * W, W)], rows_v.at[1 - b], sem_load)
            c0 = pltpu.async_copy(rows_v.at[b], xs_hbm.at[i0_v.at[i]], sem_store)
            c1 = pltpu.async_copy(rows_v.at[b], xs_hbm.at[i1_v.at[i]], sem_store)
            c0.wait()
            c1.wait()

    return pl.kernel(
        body, mesh=mesh,
        out_type=jax.ShapeDtypeStruct((n_rows,) + ROW_TILE, ROW_DTYPE),
        scratch_types=[pltpu.VMEM((n_win, W), jnp.int32), pltpu.VMEM((n_win, W), jnp.int32),
                       pltpu.VMEM((2, W) + ROW_TILE, ROW_DTYPE),
                       pltpu.SemaphoreType.DMA, pltpu.SemaphoreType.DMA],
        name="moe_sc_scatter",
    )(h2, d0, d1)


ZERO_CHUNK = 64


def _finish_dispatch_kernel(zstart_ref, zchunks_ref, slots_hbm, h2_ref, xs_in_hbm, xs_hbm, idx_smem, zbuf, sem_idx,
                            sem_rows, sem_zero):
    del xs_in_hbm
    TS = h2_ref.shape[0]
    idx_cp = pltpu.make_async_copy(slots_hbm.at[pl.ds(0, IDX_CHUNK)], idx_smem, sem_idx)
    idx_cp.start()
    zbuf[...] = jnp.zeros(zbuf.shape, zbuf.dtype)

    def zero_copy(e, j):
        return pltpu.make_async_copy(zbuf, xs_hbm.at[pl.ds(zstart_ref[e] + j * ZERO_CHUNK, ZERO_CHUNK)], sem_zero)

    for e in range(N_EXPERTS):
        lax.fori_loop(0, zchunks_ref[e], lambda j, c, e=e: (zero_copy(e, j).start(), c)[1], 0)
    idx_cp.wait()

    def issue(r, carry):
        for kk in range(2):
            pltpu.make_async_copy(h2_ref.at[r], xs_hbm.at[idx_smem[2 * r + kk]], sem_rows).start(priority=kk)
        return carry

    lax.fori_loop(0, TS, issue, 0, unroll=ISSUE_UNROLL)
    for e in range(N_EXPERTS):
        lax.fori_loop(0, zchunks_ref[e], lambda j, c, e=e: (zero_copy(e, j).wait(), c)[1], 0)
    for kk in range(2):
        pltpu.make_async_copy(h2_ref, xs_hbm.at[pl.ds(0, TS)], sem_rows).wait()


def _finish_dispatch(xs, h2s, slots, zstart, zchunks):
    Bs = h2s.shape[0]
    assert 2 * Bs <= IDX_CHUNK and slots.shape[0] == IDX_CHUNK and MOE_BLOCK % ZERO_CHUNK == 0
    any_spec = pl.BlockSpec(memory_space=pl.ANY)
    grid_spec = pltpu.PrefetchScalarGridSpec(
        num_scalar_prefetch=2,
        grid=(1,),
        in_specs=[any_spec, pl.BlockSpec((Bs,) + ROW_TILE, lambda t, z, n: (0, 0, 0)), any_spec],
        out_specs=any_spec,
        scratch_shapes=[pltpu.SMEM((IDX_CHUNK,), jnp.int32), pltpu.VMEM((ZERO_CHUNK,) + ROW_TILE, ROW_DTYPE),
                        pltpu.SemaphoreType.DMA, pltpu.SemaphoreType.DMA, pltpu.SemaphoreType.DMA],
    )
    return pl.pallas_call(
        _finish_dispatch_kernel,
        grid_spec=grid_spec,
        out_shape=jax.ShapeDtypeStruct(xs.shape, ROW_DTYPE),
        input_output_aliases={4: 0},
        compiler_params=pltpu.CompilerParams(dimension_semantics=("arbitrary",), has_side_effects=True),
        name="moe_finish_dispatch",
    )(zstart, zchunks, slots, h2s, xs)


def _ffn_kernel(bexp_ref, brow_ref, nused_ref, first_ref, slot_ref, next_ref,
                xs_ref, w1_hbm, w3_hbm, w2_hbm, ys_ref, w13_scr, w2_scr, st1, st3, st2, sem_w):
    del brow_ref
    i = pl.program_id(0)

    def weight_copies(e, s):
        return (pltpu.make_async_copy(w1_hbm.at[e], st1.at[s], sem_w.at[s]),
                pltpu.make_async_copy(w3_hbm.at[e], st3.at[s], sem_w.at[s]),
                pltpu.make_async_copy(w2_hbm.at[e], st2.at[s], sem_w.at[s]))

    @pl.when(i < nused_ref[0])
    def _():
        @pl.when(first_ref[i] == 1)
        def _():
            e = bexp_ref[i]
            s = slot_ref[i]

            @pl.when(i == 0)
            def _():
                for cp in weight_copies(e, s):
                    cp.start()

            for cp in weight_copies(e, s):
                cp.wait()
            @pl.when(next_ref[i] >= 0)
            def _():
                for cp in weight_copies(next_ref[i], 1 - s):
                    cp.start()

            w13_scr[:, 0:D_EXPERT] = st1[s].astype(_BF)
            w13_scr[:, D_EXPERT:2 * D_EXPERT] = st3[s].astype(_BF)
            w2_scr[...] = st2[s].astype(_BF)

        xb = _unpack_rows(xs_ref[...]).astype(_BF)
        ab = _dot(xb, w13_scr[...])
        hid = jax.nn.silu(ab[:, 0:D_EXPERT]) * ab[:, D_EXPERT:2 * D_EXPERT]
        ys_ref[...] = _pack_rows(_dot(hid.astype(_BF), w2_scr[...]))


def _ffn(xs, block_exp, block_row, n_used, run_first, run_slot, run_next, w1, w3, w2, n_blocks):
    blk = lambda i, *_: (_[1][i], 0, 0)
    any_spec = pl.BlockSpec(memory_space=pl.ANY)
    grid_spec = pltpu.PrefetchScalarGridSpec(
        num_scalar_prefetch=6,
        grid=(n_blocks,),
        in_specs=[pl.BlockSpec((MOE_BLOCK,) + ROW_TILE, blk), any_spec, any_spec, any_spec],
        out_specs=pl.BlockSpec((MOE_BLOCK,) + ROW_TILE, blk),
        scratch_shapes=[pltpu.VMEM((D_MODEL, 2 * D_EXPERT), _BF), pltpu.VMEM((D_EXPERT, D_MODEL), _BF),
                        pltpu.VMEM((2, D_MODEL, D_EXPERT), _F32), pltpu.VMEM((2, D_MODEL, D_EXPERT), _F32),
                        pltpu.VMEM((2, D_EXPERT, D_MODEL), _F32), pltpu.SemaphoreType.DMA((2,))],
    )
    return pl.pallas_call(
        _ffn_kernel,
        grid_spec=grid_spec,
        out_shape=jax.ShapeDtypeStruct(xs.shape, ROW_DTYPE),
        compiler_params=pltpu.CompilerParams(dimension_semantics=("arbitrary",), vmem_limit_bytes=VMEM_LIMIT),
        name="moe_ffn",
    )(block_exp, block_row, n_used, run_first, run_slot, run_next, xs, w1, w3, w2)


def _combine_kernel(slots_hbm, x1_ref, rw_ref, g_ref, ys_hbm, y_ref, idx_smem, buf, sem_idx, sem_rows):
    t = pl.program_id(0)
    TC = x1_ref.shape[0]
    idx_cp = pltpu.make_async_copy(slots_hbm.at[pl.ds(t * IDX_CHUNK, IDX_CHUNK)], idx_smem, sem_idx)
    idx_cp.start()
    idx_cp.wait()

    def issue(r, carry):
        for kk in range(2):
            pltpu.make_async_copy(ys_hbm.at[idx_smem[2 * r + kk]], buf.at[kk, r], sem_rows).start(priority=kk)
        return carry

    lax.fori_loop(0, TC, issue, 0, unroll=ISSUE_UNROLL)
    for kk in range(2):
        pltpu.make_async_copy(ys_hbm.at[pl.ds(0, TC)], buf.at[kk], sem_rows).wait()

    w = rw_ref[...]
    y0 = _unpack_rows(buf[0])
    y1 = _unpack_rows(buf[1])
    x2 = x1_ref[...] + (w[:, 0:1] * y0 + w[:, 1:2] * y1)
    y_ref[...] = _rms(x2, g_ref[...])


TC_GATHER_SHARE = 16
SC_GATHER_CHUNKS = 3
SC_GATHER_BUFFERS = 3
SC_GATHER_MAX_WINDOW = 80


def _sc_gather(ys, rows):
    Tg = rows.shape[1]
    n_workers = SC_CORES * SC_SUBCORES
    per_w = Tg // n_workers
    assert per_w * n_workers == Tg and per_w % SUBLANES == 0
    W = max(w for w in range(SUBLANES, SC_GATHER_MAX_WINDOW + 1, SUBLANES) if per_w % w == 0)
    n_win = per_w // W
    NB = SC_GATHER_BUFFERS
    idx = rows.reshape(2, n_workers, n_win, W)
    jobs = [(kk, i) for i in range(n_win) for kk in range(2)]
    mesh = plsc.VectorSubcoreMesh(core_axis_name="c", subcore_axis_name="s")

    def body(ys_hbm, idx_hbm, out_hbm, i_v, rows_v, sem_g, sem_s):
        wid = lax.axis_index("s") * SC_CORES + lax.axis_index("c")
        base = wid * per_w
        for kk in range(2):
            pltpu.sync_copy(idx_hbm.at[kk, wid], i_v.at[kk])

        def gather(j):
            kk, i = jobs[j]
            return pltpu.make_async_copy(ys_hbm.at[i_v.at[kk, i]], rows_v.at[j % NB], sem_g.at[j % NB])

        def store(j):
            kk, i = jobs[j]
            return pltpu.make_async_copy(rows_v.at[j % NB], out_hbm.at[kk, pl.ds(base + i * W, W)], sem_s.at[j % NB])

        gather(0).start()
        for j in range(len(jobs)):
            gather(j).wait()
            store(j).start()
            if j + 1 < len(jobs):
                if j + 1 >= NB:
                    store(j + 1 - NB).wait()
                gather(j + 1).start()
        for j in range(max(0, len(jobs) - NB), len(jobs)):
            store(j).wait()

    return pl.kernel(
        body, mesh=mesh,
        out_type=jax.ShapeDtypeStruct((2, Tg) + ROW_TILE, ROW_DTYPE),
        scratch_types=[pltpu.VMEM((2, n_win, W), jnp.int32), pltpu.VMEM((NB, W) + ROW_TILE, ROW_DTYPE),
                       pltpu.SemaphoreType.DMA((NB,)), pltpu.SemaphoreType.DMA((NB,))],
        name="moe_sc_gather",
    )(ys, idx)


def _combine_dense_kernel(x1_ref, rw_ref, g_ref, rows_ref, y_in_hbm, y_ref):
    del y_in_hbm
    TC = x1_ref.shape[0]
    w = rw_ref[...]
    y0 = _unpack_rows(rows_ref[0])
    y1 = _unpack_rows(rows_ref[1])
    x2 = x1_ref[...] + (w[:, 0:1] * y0 + w[:, 1:2] * y1)
    y_ref[...] = _rms(x2, g_ref[...])


def _combine_dense(x1, rw, rows, y, g_final, tile, first_tile):
    Tg = rows.shape[1]
    assert Tg % tile == 0
    tok = lambda t: (first_tile + t, 0)
    return pl.pallas_call(
        _combine_dense_kernel,
        grid=(Tg // tile,),
        in_specs=[pl.BlockSpec((tile, D_MODEL), tok), pl.BlockSpec((tile, LANES), tok), _full((1, D_MODEL)),
                  pl.BlockSpec((2, tile) + ROW_TILE, lambda t: (0, t, 0, 0)), pl.BlockSpec(memory_space=pl.ANY)],
        out_specs=pl.BlockSpec((tile, D_MODEL), tok),
        out_shape=jax.ShapeDtypeStruct(y.shape, _F32),
        input_output_aliases={4: 0},
        compiler_params=pltpu.CompilerParams(dimension_semantics=("arbitrary",), vmem_limit_bytes=VMEM_LIMIT),
        name="moe_combine_dense",
    )(x1, rw, g_final.reshape(1, D_MODEL), rows, y)


def _combine(x1, rw, slots, ys, g_final, tile, n_tiles):
    T = x1.shape[0]
    assert T % tile == 0 and 2 * tile <= IDX_CHUNK and slots.shape[0] == n_tiles * IDX_CHUNK
    any_spec = pl.BlockSpec(memory_space=pl.ANY)
    return pl.pallas_call(
        _combine_kernel,
        grid=(n_tiles,),
        in_specs=[any_spec, pl.BlockSpec((tile, D_MODEL), lambda t: (t, 0)),
                  pl.BlockSpec((tile, LANES), lambda t: (t, 0)), _full((1, D_MODEL)), any_spec],
        out_specs=pl.BlockSpec((tile, D_MODEL), lambda t: (t, 0)),
        out_shape=jax.ShapeDtypeStruct((T, D_MODEL), _F32),
        scratch_shapes=[pltpu.SMEM((IDX_CHUNK,), jnp.int32), pltpu.VMEM((2, tile) + ROW_TILE, ROW_DTYPE),
                        pltpu.SemaphoreType.DMA, pltpu.SemaphoreType.DMA],
        compiler_params=pltpu.CompilerParams(dimension_semantics=("arbitrary",), vmem_limit_bytes=VMEM_LIMIT),
        name="moe_combine",
    )(slots, x1, rw, g_final.reshape(1, D_MODEL), ys)


def _tile_for(n_tokens):
    tile = IDX_CHUNK // 2
    return tile if n_tokens % tile == 0 else n_tokens


def _chunked_slots(slot, tile):
    n_tiles = slot.shape[0] // tile
    s = slot.reshape(n_tiles, 2 * tile)
    return jnp.pad(s, ((0, 0), (0, IDX_CHUNK - 2 * tile))).reshape(-1)


def kernel(x_prompt, x_sample, state_pool, state_ret, g_mix, w_in, w_pool, pool_scale, ret_gn, w_out, g_ffn, w_grp, w_exp, w1, w3, w2, g_final):
    Bp, Lp, _ = x_prompt.shape
    Bs = x_sample.shape[0]
    Tp = Bp * Lp
    wts = _prep_weights(g_mix[0], w_in[0], w_pool[0], pool_scale[0], ret_gn[0], w_out[0], g_ffn[0], w_grp[0], w_exp[0])

    T_all = Tp + Bs
    cap = (-(-T_all // MOE_BLOCK) + 1) * MOE_BLOCK
    cap_blocks = cap // MOE_BLOCK

    x1p, dst_p, dst_rows_p, rwp, npool_p, nret_p, counts_p, h2p = _mixer_prompt(x_prompt, wts, cap)
    xs = _sc_scatter(h2p, dst_rows_p, N_EXPERTS * cap)
    x1s, dst_s, rws, npool_s, nret_s, counts, h2s = _mixer_sample(
        x_sample.reshape(Bs, D_MODEL), jnp.swapaxes(state_pool[0], 0, 1), state_ret[0], wts, PAST_LEN, counts_p, cap)

    counts = counts[0, :N_EXPERTS].astype(jnp.int32)
    tile_p, tile_s = _tile_for(Tp), _tile_for(Bs)
    slots_p = _chunked_slots(dst_p[:, :2], tile_p)
    slots_s = _chunked_slots(dst_s[:, :2], tile_s)
    zstart = jnp.arange(N_EXPERTS, dtype=jnp.int32) * cap + counts
    zchunks = ((-counts) % MOE_BLOCK + ZERO_CHUNK - 1) // ZERO_CHUNK
    xs = _finish_dispatch(xs, h2s, slots_s, zstart, zchunks.astype(jnp.int32))
    nblk = (counts + MOE_BLOCK - 1) // MOE_BLOCK
    blk_end = jnp.cumsum(nblk)
    n_used = blk_end[-1:].astype(jnp.int32)
    n_blocks = -(-2 * T_all // MOE_BLOCK) + N_EXPERTS
    step = jnp.minimum(jnp.arange(n_blocks, dtype=jnp.int32), n_used - 1)
    block_exp = jnp.minimum(jnp.sum((blk_end[None, :] <= step[:, None]).astype(jnp.int32), axis=1), N_EXPERTS - 1)
    blk_start = jnp.sum(jnp.where(block_exp[:, None] == jnp.arange(N_EXPERTS, dtype=jnp.int32), (blk_end - nblk)[None, :], 0),
                        axis=1)
    block_row = (block_exp * cap_blocks + (step - blk_start)).astype(jnp.int32)
    block_exp = block_exp.astype(jnp.int32)
    experts = jnp.arange(N_EXPERTS, dtype=jnp.int32)
    nonempty = nblk > 0
    run_first = (step == blk_start).astype(jnp.int32)
    run_slot = (jnp.sum((nonempty[None, :] & (experts[None, :] < block_exp[:, None])).astype(jnp.int32), axis=1) % 2)
    later = nonempty[None, :] & (experts[None, :] > block_exp[:, None])
    run_next = jnp.min(jnp.where(later, experts[None, :], N_EXPERTS), axis=1)
    run_next = jnp.where(run_next == N_EXPERTS, -1, run_next).astype(jnp.int32)

    ys = _ffn(xs, block_exp, block_row, n_used, run_first, run_slot.astype(jnp.int32), run_next,
              w1[0], w3[0], w2[0], n_blocks)
    n_tiles_p = Tp // tile_p
    n_tc = max(1, n_tiles_p // TC_GATHER_SHARE)
    n_sc = n_tiles_p - n_tc
    chunk_tiles = [n_sc // SC_GATHER_CHUNKS + (k < n_sc % SC_GATHER_CHUNKS) for k in range(SC_GATHER_CHUNKS)]
    gathered, first = [], n_tc
    for nt in chunk_tiles:
        if nt:
            gathered.append((first, _sc_gather(ys, dst_rows_p[:, first * tile_p:(first + nt) * tile_p])))
            first += nt
    y_s = _combine(x1s, rws, slots_s, ys, g_final, tile_s, 1)
    y_p = _combine(x1p, rwp, slots_p[:n_tc * IDX_CHUNK], ys, g_final, tile_p, n_tc)
    for first, rows in gathered:
        y_p = _combine_dense(x1p, rwp, rows, y_p, g_final, tile_p, first)

    return (y_p.reshape(Bp, Lp, D_MODEL), y_s.reshape(Bs, 1, D_MODEL),
            npool_p[None], nret_p[None], jnp.swapaxes(npool_s, 0, 1)[None], nret_s[None])
```

```python
import functools

import jax
import jax.numpy as jnp
import numpy as np
from jax import lax
from jax.experimental import pallas as pl
from jax.experimental.pallas import tpu as pltpu
from jax.experimental.pallas import tpu_sc as plsc

D_MODEL = 1024
EPS = 1e-6
POOL_GROUPS = 4
POOL_IN = D_MODEL // 2
POOL_GC = POOL_IN // POOL_GROUPS
POOL_OUT_GC = D_MODEL // POOL_GROUPS
POOL_WINDOWS = (2, 4, 8, 16)
POOL_HIST = max(POOL_WINDOWS) - 1
RET_HEADS = 4
RET_DK = D_MODEL // 8
RET_DV = D_MODEL // RET_HEADS
ROPE_BASE = 10000.0
PAST_LEN = 16384
N_GROUPS = 4
EXPERTS_PER_GROUP = 8
N_EXPERTS = N_GROUPS * EXPERTS_PER_GROUP
D_EXPERT = D_MODEL // 4
QK_W = RET_HEADS * RET_DK
V_W = RET_HEADS * RET_DV
OFF_U = 0
OFF_Q = POOL_IN
OFF_K = OFF_Q + QK_W
OFF_V = OFF_K + QK_W
OFF_GA = OFF_V + V_W
OFF_GB = OFF_GA + D_MODEL
IN_WIDTH = OFF_GB + D_MODEL

LANES = 128
SUBLANES = 8
ROW_WORDS = D_MODEL // 2
ROW_TILE = (ROW_WORDS // LANES, LANES)
ROW_DTYPE = jnp.uint32
HALO = 16
RET_CHUNK = 256
MIXER_SEQS = 2
MOE_BLOCK = 512
ROUTER_W = LANES
VMEM_LIMIT = 56 * 1024 * 1024

_BF = jnp.bfloat16
_F32 = jnp.float32


def _rms(x, g):
    inv = lax.rsqrt(jnp.mean(x * x, axis=-1, keepdims=True) + EPS)
    return x * inv * g


def _dot(a, b):
    return jnp.dot(a, b, preferred_element_type=_F32)


def _round_bf16(x):
    return x.astype(_BF).astype(_F32)


def _pack_rows(x):
    lo = lax.bitcast_convert_type(x[:, :ROW_WORDS].astype(_BF).astype(_F32), jnp.uint32)
    hi = lax.bitcast_convert_type(x[:, ROW_WORDS:].astype(_BF).astype(_F32), jnp.uint32)
    return ((lo >> 16) | hi).reshape((x.shape[0],) + ROW_TILE)


def _unpack_rows(w):
    w = w.reshape(w.shape[0], ROW_WORDS)
    lo = lax.bitcast_convert_type(w << 16, _F32)
    hi = lax.bitcast_convert_type(w & jnp.uint32(0xFFFF0000), _F32)
    return jnp.concatenate([lo, hi], axis=1)


def _rotary(x, cos2, sin2):
    return x * cos2 + pltpu.roll(x, RET_DK // 2, 1) * sin2


def _route(logits):
    lane = lax.broadcasted_iota(jnp.int32, logits.shape, 1).astype(_F32)
    neg = jnp.float32(-jnp.inf)
    big = jnp.float32(1 << 20)
    lg = jnp.where(lane < N_GROUPS, logits, neg)
    mg = jnp.max(lg, axis=-1, keepdims=True)
    g_idx = jnp.min(jnp.where(lg == mg, lane, big), axis=-1, keepdims=True)
    p_g = 1.0 / jnp.sum(jnp.exp(lg - mg), axis=-1, keepdims=True)
    lo = N_GROUPS + g_idx * EXPERTS_PER_GROUP
    in_grp = (lane >= lo) & (lane < lo + EXPERTS_PER_GROUP)
    le = jnp.where(in_grp, logits, neg)
    m1 = jnp.max(le, axis=-1, keepdims=True)
    i1 = jnp.min(jnp.where(le == m1, lane, big), axis=-1, keepdims=True)
    le2 = jnp.where(lane == i1, neg, le)
    m2 = jnp.max(le2, axis=-1, keepdims=True)
    i2 = jnp.min(jnp.where(le2 == m2, lane, big), axis=-1, keepdims=True)
    t = jnp.exp(m2 - m1)
    den = 1.0 + t
    e0 = (i1 - N_GROUPS).astype(jnp.int32)
    e1 = (i2 - N_GROUPS).astype(jnp.int32)
    return e0, e1, p_g * (1.0 / den), p_g * (t / den)


def _post_mix(x, mixed_ref, w_out_ref, g_ffn_ref, w_rt_ref, x1_ref, rw_ref):
    x1 = x + _dot(mixed_ref[...], w_out_ref[...])
    x1_ref[...] = x1.reshape(x1_ref.shape)
    h2 = _rms(x1, g_ffn_ref[...])
    e0, e1, w0, w1 = _route(_dot(h2.astype(_BF), w_rt_ref[...]))
    lane = lax.broadcasted_iota(jnp.int32, (x.shape[0], LANES), 1)
    rw_ref[...] = jnp.where(lane == 0, w0, jnp.where(lane == 1, w1, 0.0)).reshape(rw_ref.shape)
    return h2, e0, e1


def _rank_rows(e0, e1, cap, carry_scr):
    R = e0.shape[0]
    lane = lax.broadcasted_iota(jnp.int32, (R, LANES), 1)
    m0 = lane == e0
    m1 = lane == e1
    onehot = jnp.where(m0 | m1, 1.0, 0.0)
    r_i = lax.broadcasted_iota(jnp.int32, (R, R), 0)
    c_i = lax.broadcasted_iota(jnp.int32, (R, R), 1)
    tri = jnp.where(c_i < r_i, 1.0, 0.0).astype(_BF)
    before = _dot(tri, onehot.astype(_BF)) + carry_scr[...]
    d0 = e0.astype(_F32) * cap + jnp.sum(jnp.where(m0, before, 0.0), axis=-1, keepdims=True)
    d1 = e1.astype(_F32) * cap + jnp.sum(jnp.where(m1, before, 0.0), axis=-1, keepdims=True)
    carry_scr[...] += jnp.sum(onehot, axis=0, keepdims=True)
    return jnp.where(lane == 0, d0, jnp.where(lane == 1, d1, 0.0))


def _group_norm(o, gain):
    mu = jnp.mean(o, axis=-1, keepdims=True)
    d = o - mu
    var = jnp.mean(d * d, axis=-1, keepdims=True)
    return d * lax.rsqrt(var + EPS) * gain


def _mixer_prompt_kernel(x_ref, g_mix_ref, w_in_ref, w_pool_ref, pscale_ref, gn_ref, w_out_ref, g_ffn_ref,
                         w_rt_ref, cos_ref, sin_ref, dmask_ref, xi_ref, zeta_ref, gc_ref,
                         x1_ref, dst_t_ref, rw_ref, npool_ref, nret_ref, cnt_ref, h2_ref,
                         u_scr, s_scr, mixed_scr, carry_scr, *, cap):
    c = pl.program_id(1)
    n_c = pl.num_programs(1)
    @pl.when((pl.program_id(0) == 0) & (c == 0))
    def _():
        carry_scr[...] = jnp.zeros(carry_scr.shape, _F32)
    NSEQ, C, _ = x_ref.shape
    R = NSEQ * C

    @pl.when(c == 0)
    def _():
        u_scr[:, 0:HALO, :] = jnp.zeros((NSEQ, HALO, POOL_IN), _F32)
        s_scr[...] = jnp.zeros(s_scr.shape, _F32)

    x = x_ref[...].reshape(R, D_MODEL)
    h = _rms(x, g_mix_ref[...]).astype(_BF)

    u_scr[:, HALO:HALO + C, :] = _dot(h, w_in_ref[:, OFF_U:OFF_U + POOL_IN]).reshape(NSEQ, C, POOL_IN)
    q = _dot(h, w_in_ref[:, OFF_Q:OFF_Q + QK_W])
    k = _dot(h, w_in_ref[:, OFF_K:OFF_K + QK_W])
    cos2 = jnp.concatenate([cos_ref[...]] * NSEQ, axis=0)
    sin2 = jnp.concatenate([sin_ref[...]] * NSEQ, axis=0)
    pos1 = (c * C + 1 + lax.broadcasted_iota(jnp.int32, (C, POOL_GC), 0)).astype(_F32)

    for j in range(RET_HEADS):
        win = POOL_WINDOWS[j]
        cs = slice(j * POOL_GC, (j + 1) * POOL_GC)
        n_rows = jnp.minimum(pos1, jnp.float32(win))
        pooled = []
        for sq in range(NSEQ):
            u_j = u_scr[sq, HALO:HALO + C, cs]
            s = u_j
            for d in range(1, win):
                s = s + u_scr[sq, HALO - d:HALO - d + C, cs]
            pooled.append(s / n_rows - u_j)
        osl = slice(j * POOL_OUT_GC, (j + 1) * POOL_OUT_GC)
        pool_out = _dot(jnp.concatenate(pooled, axis=0).astype(_BF), w_pool_ref[j]) * pscale_ref[:, osl]

        qs = slice(j * RET_DK, (j + 1) * RET_DK)
        qb = _rotary(q[:, qs], cos2, sin2).astype(_BF)
        kf = _rotary(k[:, qs], cos2, sin2) * (RET_DK ** -0.5)
        kb = kf.astype(_BF)
        vb = _dot(h, w_in_ref[:, OFF_V + j * RET_DV:OFF_V + (j + 1) * RET_DV]).astype(_BF)
        ret = []
        for sq in range(NSEQ):
            rows = slice(sq * C, (sq + 1) * C)
            scores = lax.dot_general(qb[rows], kb[rows], (((1,), (1,)), ((), ())),
                                     preferred_element_type=_F32) * dmask_ref[j]
            s_old = s_scr[sq, j]
            o = _dot(scores.astype(_BF), vb[rows]) + _dot(qb[rows], s_old.astype(_BF)) * xi_ref[j]
            kz = (kf[rows] * zeta_ref[j]).astype(_BF)
            s_scr[sq, j] = gc_ref[j] * s_old + lax.dot_general(kz, vb[rows], (((0,), (0,)), ((), ())),
                                                               preferred_element_type=_F32)
            ret.append(_group_norm(o, gn_ref[:, osl]))
        ret_out = jnp.concatenate(ret, axis=0)

        ga = _dot(h, w_in_ref[:, OFF_GA + j * RET_DV:OFF_GA + (j + 1) * RET_DV])
        gb = _dot(h, w_in_ref[:, OFF_GB + j * RET_DV:OFF_GB + (j + 1) * RET_DV])
        mixed_scr[:, osl] = (jax.nn.sigmoid(ga) * pool_out + jax.nn.sigmoid(gb) * ret_out).astype(_BF)

    h2, e0, e1 = _post_mix(x, mixed_scr, w_out_ref, g_ffn_ref, w_rt_ref, x1_ref, rw_ref)

    h2_ref[...] = _pack_rows(h2).reshape(h2_ref.shape)
    dst = _rank_rows(e0, e1, cap, carry_scr)
    dst_t =dst.T[0:SUBLANES, :].astype(jnp.int32)
    for sq in range(NSEQ):
        dst_t_ref[sq] = dst_t[:, sq * C:(sq + 1) * C]
    cnt_ref[...] = carry_scr[...]

    u_scr[:, 0:HALO, :] = u_scr[:, C:C + HALO, :]

    @pl.when(c == n_c - 1)
    def _():
        npool_ref[...] = u_scr[:, HALO + C - POOL_HIST:HALO + C, :]
        nret_ref[...] = s_scr[...]


def _decay_tables(C):
    f32 = np.float32
    log_g = np.log(f32(1.0) - np.exp2(f32(-5.0) - np.arange(RET_HEADS, dtype=f32)))
    i = np.arange(C, dtype=f32)
    diff = i[:, None] - i[None, :]
    dmask = np.where(diff >= 0, np.exp(np.maximum(diff, f32(0.0))[None] * log_g[:, None, None]), f32(0.0))
    xi = np.exp((i[None, :] + f32(1.0)) * log_g[:, None])
    zeta = np.exp((f32(C) - f32(1.0) - i)[None, :] * log_g[:, None])
    g_chunk = np.exp(f32(C) * log_g)
    return dmask.astype(f32), xi.astype(f32), zeta.astype(f32), g_chunk.astype(f32)


def _rope_tables(pos):
    f32 = np.float32
    half = RET_DK // 2
    freqs = np.power(f32(ROPE_BASE), -np.arange(half, dtype=f32) / f32(half)).astype(f32)
    ang = (pos[:, None] * freqs[None, :]).astype(f32)
    cos, sin = np.cos(ang).astype(f32), np.sin(ang).astype(f32)
    return np.concatenate([cos, cos], axis=-1), np.concatenate([-sin, sin], axis=-1)


def _const(a, shape=None):
    if shape is not None:
        a = np.ascontiguousarray(np.broadcast_to(a, shape))
    return jnp.asarray(a)


def _full(shape):
    n = len(shape)
    return pl.BlockSpec(shape, lambda *_: (0,) * n)


def _mixer_prompt(x, wts, cap):
    B, L, _ = x.shape
    C = RET_CHUNK if L % RET_CHUNK == 0 else L
    n_c = L // C
    T = B * L
    dmask, xi, zeta, g_chunk = _decay_tables(C)
    dmask = _const(dmask)
    xi_b = _const(xi[:, :, None], (RET_HEADS, C, RET_DV))
    zeta_b = _const(zeta[:, :, None], (RET_HEADS, C, RET_DK))
    gc_b = _const(g_chunk[:, None, None], (RET_HEADS, 1, RET_DV))
    cos2, sin2 = (_const(t) for t in _rope_tables(np.arange(L).astype(np.float32)))

    NSEQ = MIXER_SEQS if B % MIXER_SEQS == 0 else 1
    B2 = B // NSEQ
    T2 = T // NSEQ
    x = x.reshape(NSEQ, B2, L, D_MODEL)
    tok = lambda b, c: (0, b * n_c + c, 0)
    in_specs = [
        pl.BlockSpec((NSEQ, None, C, D_MODEL), lambda b, c: (0, b, c, 0)),
        _full((1, D_MODEL)), _full((D_MODEL, IN_WIDTH)), _full((POOL_GROUPS, POOL_GC, POOL_OUT_GC)),
        _full((1, D_MODEL)), _full((1, D_MODEL)), _full((D_MODEL, D_MODEL)), _full((1, D_MODEL)),
        _full((D_MODEL, ROUTER_W)),
        pl.BlockSpec((C, RET_DK), lambda b, c: (c, 0)), pl.BlockSpec((C, RET_DK), lambda b, c: (c, 0)),
        _full((RET_HEADS, C, C)), _full((RET_HEADS, C, RET_DV)), _full((RET_HEADS, C, RET_DK)),
        _full((RET_HEADS, 1, RET_DV)),
    ]
    R = NSEQ * C
    out_shape = (
        jax.ShapeDtypeStruct((NSEQ, T2, D_MODEL), _F32),
        jax.ShapeDtypeStruct((NSEQ, SUBLANES, T2), jnp.int32),
        jax.ShapeDtypeStruct((NSEQ, T2, LANES), _F32),
        jax.ShapeDtypeStruct((NSEQ, B2, POOL_HIST, POOL_IN), _F32),
        jax.ShapeDtypeStruct((NSEQ, B2, RET_HEADS, RET_DK, RET_DV), _F32),
        jax.ShapeDtypeStruct((1, LANES), _F32),
        jax.ShapeDtypeStruct((NSEQ, T2) + ROW_TILE, ROW_DTYPE),
    )
    out_specs = (
        pl.BlockSpec((NSEQ, C, D_MODEL), tok),
        pl.BlockSpec((NSEQ, SUBLANES, C), lambda b, c: (0, 0, b * n_c + c)),
        pl.BlockSpec((NSEQ, C, LANES), tok),
        pl.BlockSpec((NSEQ, None, POOL_HIST, POOL_IN), lambda b, c: (0, b, 0, 0)),
        pl.BlockSpec((NSEQ, None, RET_HEADS, RET_DK, RET_DV), lambda b, c: (0, b, 0, 0, 0)),
        _full((1, LANES)),
        pl.BlockSpec((NSEQ, C) + ROW_TILE, lambda b, c: (0, b * n_c + c, 0, 0)),
    )
    x1, dst_t, rw, npool, nret, counts, h2 = pl.pallas_call(
        functools.partial(_mixer_prompt_kernel, cap=cap),
        grid=(B2, n_c),
        in_specs=in_specs,
        out_specs=out_specs,
        out_shape=out_shape,
        scratch_shapes=[pltpu.VMEM((NSEQ, HALO + C, POOL_IN), _F32),
                        pltpu.VMEM((NSEQ, RET_HEADS, RET_DK, RET_DV), _F32),
                        pltpu.VMEM((R, D_MODEL), _BF),
                        pltpu.VMEM((1, LANES), _F32)],
        compiler_params=pltpu.CompilerParams(dimension_semantics=("arbitrary", "arbitrary"),
                                             vmem_limit_bytes=VMEM_LIMIT),
        name="mixer_prompt",
    )(x, wts["g_mix"], wts["w_in"], wts["w_pool"], wts["pool_scale"], wts["ret_gn"], wts["w_out"],
      wts["g_ffn"], wts["w_rt"], cos2, sin2, dmask, xi_b, zeta_b, gc_b)
    dst_rows = jnp.moveaxis(dst_t[:, 0:2, :], 1, 0).reshape(2, T)
    return (x1.reshape(T, D_MODEL), dst_rows, rw.reshape(T, LANES),
            npool.reshape(B, POOL_HIST, POOL_IN), nret.reshape(B, RET_HEADS, RET_DK, RET_DV), counts,
            h2.reshape((T,) + ROW_TILE))


def _prep_weights(g_mix, w_in, w_pool, pool_scale, ret_gn, w_out, g_ffn, w_grp, w_exp):
    w_rt = jnp.concatenate([w_grp, w_exp.reshape(D_MODEL, N_EXPERTS)], axis=1)
    w_rt = jnp.pad(w_rt, ((0, 0), (0, ROUTER_W - w_rt.shape[1])))
    row = lambda v: v.reshape(1, D_MODEL)
    return dict(g_mix=row(g_mix), w_in=w_in.astype(_BF), w_pool=w_pool.astype(_BF), pool_scale=row(pool_scale),
                ret_gn=row(ret_gn), w_out=w_out.astype(_BF), g_ffn=row(g_ffn), w_rt=w_rt.astype(_BF))


SAMPLE_TB = 8


def _mixer_sample_kernel(x_ref, spool_ref, sret_ref, g_mix_ref, w_in_ref, w_pool_ref, pscale_ref, gn_ref,
                         w_out_ref, g_ffn_ref, w_rt_ref, cos_ref, sin_ref, dm_ref, xi_ref, zeta_ref, gc_ref,
                         carry_in_ref,
                         x1_ref, dst_ref, rw_ref, npool_ref, nret_ref, cnt_ref, h2_ref,
                         u_scr, q_scr, k_scr, qt_scr, kt_scr, v_scr, ga_scr, gb_scr, pooled_scr, o_scr, mixed_scr,
                         carry_scr, *, cap):
    t = pl.program_id(0)
    n_t = pl.num_programs(0)
    Bs = x_ref.shape[0]
    TB = sret_ref.shape[0]

    @pl.when(t == 0)
    def _():
        h = _rms(x_ref[...], g_mix_ref[...]).astype(_BF)
        u_scr[...] = _dot(h, w_in_ref[:, OFF_U:OFF_U + POOL_IN])
        q = _dot(h, w_in_ref[:, OFF_Q:OFF_Q + QK_W])
        k = _dot(h, w_in_ref[:, OFF_K:OFF_K + QK_W])
        for j in range(RET_HEADS):
            qs = slice(j * RET_DK, (j + 1) * RET_DK)
            qf = _rotary(q[:, qs], cos_ref[...], sin_ref[...])
            kf = _rotary(k[:, qs], cos_ref[...], sin_ref[...]) * (RET_DK ** -0.5)
            q_scr[:, qs] = _round_bf16(qf)
            k_scr[:, qs] = _round_bf16(kf)
            qt_scr[j] = _round_bf16(qf).T
            kt_scr[j] = kf.T
        v_scr[...] = _round_bf16(_dot(h, w_in_ref[:, OFF_V:OFF_V + V_W]))
        ga_scr[...] = _dot(h, w_in_ref[:, OFF_GA:OFF_GA + D_MODEL])
        gb_scr[...] = _dot(h, w_in_ref[:, OFF_GB:OFF_GB + D_MODEL])

    shift = (Bs - t * TB) % Bs
    qt = [pltpu.roll(qt_scr[j], shift, 1) for j in range(RET_HEADS)]
    kt = [pltpu.roll(kt_scr[j], shift, 1) for j in range(RET_HEADS)]

    blk = pl.ds(pl.multiple_of(t * TB, TB), TB)
    u_blk = u_scr[blk, :]
    q_blk = q_scr[blk, :]
    k_blk = k_scr[blk, :]
    v_blk = v_scr[blk, :]
    score = [_round_bf16(jnp.sum(q_blk[:, j * RET_DK:(j + 1) * RET_DK] * k_blk[:, j * RET_DK:(j + 1) * RET_DK],
                                 axis=-1, keepdims=True) * dm_ref[j]) for j in range(RET_HEADS)]

    groups = []
    for g, win in enumerate(POOL_WINDOWS):
        cs = slice(g * POOL_GC, (g + 1) * POOL_GC)
        s = u_blk[:, cs]
        for r in range(POOL_HIST - (win - 1), POOL_HIST):
            s = s + spool_ref[r, :, cs]
        groups.append(s / jnp.float32(win) - u_blk[:, cs])
    pooled_scr[blk, :] = jnp.concatenate(groups, axis=1)
    npool_ref[0:POOL_HIST - 1] = spool_ref[1:POOL_HIST]
    npool_ref[POOL_HIST - 1] = u_blk

    o_rows = []
    for i in range(TB):
        heads = []
        for j in range(RET_HEADS):
            s_old = sret_ref[i, j]
            v_row = v_blk[i:i + 1, j * RET_DV:(j + 1) * RET_DV]
            qcol = qt[j][:, i:i + 1]
            kcol = kt[j][:, i:i + 1]
            qs_old = jnp.sum(qcol * _round_bf16(s_old), axis=0, keepdims=True)
            heads.append(score[j][i:i + 1, :] * v_row + qs_old * xi_ref[j])
            nret_ref[i, j] = gc_ref[j] * s_old + _round_bf16(kcol * zeta_ref[j]) * v_row
        o_rows.append(jnp.concatenate(heads, axis=1))
    o_scr[blk, :] = jnp.concatenate(o_rows, axis=0)

    @pl.when(t == n_t - 1)
    def _():
        for j in range(RET_HEADS):
            cs = slice(j * POOL_GC, (j + 1) * POOL_GC)
            osl = slice(j * POOL_OUT_GC, (j + 1) * POOL_OUT_GC)
            pool_out = _dot(pooled_scr[:, cs].astype(_BF), w_pool_ref[j]) * pscale_ref[:, osl]
            ret_out = _group_norm(o_scr[:, osl], gn_ref[:, osl])
            mixed_scr[:, osl] = (jax.nn.sigmoid(ga_scr[:, osl]) * pool_out
                                 + jax.nn.sigmoid(gb_scr[:, osl]) * ret_out).astype(_BF)
        h2, e0, e1 = _post_mix(x_ref[...], mixed_scr, w_out_ref, g_ffn_ref, w_rt_ref, x1_ref, rw_ref)
        h2_ref[...] = _pack_rows(h2).reshape(h2_ref.shape)
        carry_scr[...] = carry_in_ref[...]
        dst_ref[...] = _rank_rows(e0, e1, cap, carry_scr).astype(jnp.int32)
        cnt_ref[...] = carry_scr[...]


def _mixer_sample(x, state_pool, state_ret, wts, pos0, carry_in, cap):
    Bs = x.shape[0]
    assert Bs == LANES and Bs % SAMPLE_TB == 0
    TB = SAMPLE_TB
    dmask, xi, zeta, g_chunk = _decay_tables(1)
    dm_b = _const(dmask, (RET_HEADS, 1, 1))
    xi_b = _const(xi[:, :, None], (RET_HEADS, 1, RET_DV))
    zeta_b = _const(zeta[:, :, None], (RET_HEADS, 1, 1))
    gc_b = _const(g_chunk[:, None, None], (RET_HEADS, 1, RET_DV))
    cos2, sin2 = (_const(t) for t in _rope_tables((pos0 + np.arange(1)).astype(np.float32)))

    in_specs = [
        _full((Bs, D_MODEL)),
        pl.BlockSpec((POOL_HIST, TB, POOL_IN), lambda t: (0, t, 0)),
        pl.BlockSpec((TB, RET_HEADS, RET_DK, RET_DV), lambda t: (t, 0, 0, 0)),
        _full((1, D_MODEL)), _full((D_MODEL, IN_WIDTH)), _full((POOL_GROUPS, POOL_GC, POOL_OUT_GC)),
        _full((1, D_MODEL)), _full((1, D_MODEL)), _full((D_MODEL, D_MODEL)), _full((1, D_MODEL)),
        _full((D_MODEL, ROUTER_W)),
        _full((1, RET_DK)), _full((1, RET_DK)),
        _full((RET_HEADS, 1, 1)), _full((RET_HEADS, 1, RET_DV)), _full((RET_HEADS, 1, 1)),
        _full((RET_HEADS, 1, RET_DV)),
        _full((1, LANES)),
    ]
    out_shape = (
        jax.ShapeDtypeStruct((Bs, D_MODEL), _F32),
        jax.ShapeDtypeStruct((Bs, LANES), jnp.int32),
        jax.ShapeDtypeStruct((Bs, LANES), _F32),
        jax.ShapeDtypeStruct((POOL_HIST, Bs, POOL_IN), _F32),
        jax.ShapeDtypeStruct((Bs, RET_HEADS, RET_DK, RET_DV), _F32),
        jax.ShapeDtypeStruct((1, LANES), _F32),
        jax.ShapeDtypeStruct((Bs,) + ROW_TILE, ROW_DTYPE),
    )
    out_specs = (
        _full((Bs, D_MODEL)), _full((Bs, LANES)), _full((Bs, LANES)),
        pl.BlockSpec((POOL_HIST, TB, POOL_IN), lambda t: (0, t, 0)),
        pl.BlockSpec((TB, RET_HEADS, RET_DK, RET_DV), lambda t: (t, 0, 0, 0)),
        _full((1, LANES)), _full((Bs,) + ROW_TILE),
    )
    f32 = lambda *s: pltpu.VMEM(s, _F32)
    return pl.pallas_call(
        functools.partial(_mixer_sample_kernel, cap=cap),
        grid=(Bs // TB,),
        in_specs=in_specs,
        out_specs=out_specs,
        out_shape=out_shape,
        scratch_shapes=[f32(Bs, POOL_IN), f32(Bs, QK_W), f32(Bs, QK_W), f32(RET_HEADS, RET_DK, Bs),
                        f32(RET_HEADS, RET_DK, Bs), f32(Bs, V_W), f32(Bs, D_MODEL), f32(Bs, D_MODEL),
                        f32(Bs, POOL_IN), f32(Bs, V_W), pltpu.VMEM((Bs, D_MODEL), _BF), f32(1, LANES)],
        compiler_params=pltpu.CompilerParams(dimension_semantics=("arbitrary",), vmem_limit_bytes=VMEM_LIMIT),
        name="mixer_sample",
    )(x, state_pool, state_ret, wts["g_mix"], wts["w_in"], wts["w_pool"], wts["pool_scale"], wts["ret_gn"],
      wts["w_out"], wts["g_ffn"], wts["w_rt"], cos2, sin2, dm_b, xi_b, zeta_b, gc_b, carry_in)


IDX_CHUNK = 1024
ISSUE_UNROLL = 8
SC_CORES = 2
SC_SUBCORES = 16
SC_WINDOW = 64


def _sc_scatter(h2, dst_rows, n_rows):
    T = h2.shape[0]
    n_workers = SC_CORES * SC_SUBCORES
    W = SC_WINDOW
    assert T % (n_workers * W) == 0
    per_w = T // n_workers
    n_win = per_w // W
    d0 = dst_rows[0].reshape(T // W, W)
    d1 = dst_rows[1].reshape(T // W, W)
    mesh = plsc.VectorSubcoreMesh(core_axis_name="c", subcore_axis_name="s")

    def body(h2_hbm, d0_hbm, d1_hbm, xs_hbm, i0_v, i1_v, rows_v, sem_load, sem_store):
        wid = lax.axis_index("s") * SC_CORES + lax.axis_index("c")
        base = wid * per_w
        pltpu.sync_copy(d0_hbm.at[pl.ds(wid * n_win, n_win)], i0_v)
        pltpu.sync_copy(d1_hbm.at[pl.ds(wid * n_win, n_win)], i1_v)
        pltpu.async_copy(h2_hbm.at[pl.ds(base, W)], rows_v.at[0], sem_load)
        for i in range(n_win):
            b = i % 2
            pltpu.make_async_copy(h2_hbm.at[pl.ds(base, W)], rows_v.at[b], sem_load).wait()
            if i + 1 < n_win:
                pltpu.async_copy(h2_hbm.at[pl.ds(base + (i + 1) * W, W)], rows_v.at[1 - b], sem_load)
            c0 = pltpu.async_copy(rows_v.at[b], xs_hbm.at[i0_v.at[i]], sem_store)
            c1 = pltpu.async_copy(rows_v.at[b], xs_hbm.at[i1_v.at[i]], sem_store)
            c0.wait()
            c1.wait()

    return pl.kernel(
        body, mesh=mesh,
        out_type=jax.ShapeDtypeStruct((n_rows,) + ROW_TILE, ROW_DTYPE),
        scratch_types=[pltpu.VMEM((n_win, W), jnp.int32), pltpu.VMEM((n_win, W), jnp.int32),
                       pltpu.VMEM((2, W) + ROW_TILE, ROW_DTYPE),
                       pltpu.SemaphoreType.DMA, pltpu.SemaphoreType.DMA],
        name="moe_sc_scatter",
    )(h2, d0, d1)


ZERO_CHUNK = 64


def _finish_dispatch_kernel(zstart_ref, zchunks_ref, slots_hbm, h2_ref, xs_in_hbm, xs_hbm, idx_smem, zbuf, sem_idx,
                            sem_rows, sem_zero):
    del xs_in_hbm
    TS = h2_ref.shape[0]
    idx_cp = pltpu.make_async_copy(slots_hbm.at[pl.ds(0, IDX_CHUNK)], idx_smem, sem_idx)
    idx_cp.start()
    zbuf[...] = jnp.zeros(zbuf.shape, zbuf.dtype)

    def zero_copy(e, j):
        return pltpu.make_async_copy(zbuf, xs_hbm.at[pl.ds(zstart_ref[e] + j * ZERO_CHUNK, ZERO_CHUNK)], sem_zero)

    for e in range(N_EXPERTS):
        lax.fori_loop(0, zchunks_ref[e], lambda j, c, e=e: (zero_copy(e, j).start(), c)[1], 0)
    idx_cp.wait()

    def issue(r, carry):
        for kk in range(2):
            pltpu.make_async_copy(h2_ref.at[r], xs_hbm.at[idx_smem[2 * r + kk]], sem_rows).start(priority=kk)
        return carry

    lax.fori_loop(0, TS, issue, 0, unroll=ISSUE_UNROLL)
    for e in range(N_EXPERTS):
        lax.fori_loop(0, zchunks_ref[e], lambda j, c, e=e: (zero_copy(e, j).wait(), c)[1], 0)
    for kk in range(2):
        pltpu.make_async_copy(h2_ref, xs_hbm.at[pl.ds(0, TS)], sem_rows).wait()


def _finish_dispatch(xs, h2s, slots, zstart, zchunks):
    Bs = h2s.shape[0]
    assert 2 * Bs <= IDX_CHUNK and slots.shape[0] == IDX_CHUNK and MOE_BLOCK % ZERO_CHUNK == 0
    any_spec = pl.BlockSpec(memory_space=pl.ANY)
    grid_spec = pltpu.PrefetchScalarGridSpec(
        num_scalar_prefetch=2,
        grid=(1,),
        in_specs=[any_spec, pl.BlockSpec((Bs,) + ROW_TILE, lambda t, z, n: (0, 0, 0)), any_spec],
        out_specs=any_spec,
        scratch_shapes=[pltpu.SMEM((IDX_CHUNK,), jnp.int32), pltpu.VMEM((ZERO_CHUNK,) + ROW_TILE, ROW_DTYPE),
                        pltpu.SemaphoreType.DMA, pltpu.SemaphoreType.DMA, pltpu.SemaphoreType.DMA],
    )
    return pl.pallas_call(
        _finish_dispatch_kernel,
        grid_spec=grid_spec,
        out_shape=jax.ShapeDtypeStruct(xs.shape, ROW_DTYPE),
        input_output_aliases={4: 0},
        compiler_params=pltpu.CompilerParams(dimension_semantics=("arbitrary",), has_side_effects=True),
        name="moe_finish_dispatch",
    )(zstart, zchunks, slots, h2s, xs)


def _ffn_kernel(bexp_ref, brow_ref, nused_ref, first_ref, slot_ref, next_ref,
                xs_ref, w1_hbm, w3_hbm, w2_hbm, ys_ref, w13_scr, w2_scr, st1, st3, st2, sem_w):
    del brow_ref
    i = pl.program_id(0)

    def weight_copies(e, s):
        return (pltpu.make_async_copy(w1_hbm.at[e], st1.at[s], sem_w.at[s]),
                pltpu.make_async_copy(w3_hbm.at[e], st3.at[s], sem_w.at[s]),
                pltpu.make_async_copy(w2_hbm.at[e], st2.at[s], sem_w.at[s]))

    @pl.when(i < nused_ref[0])
    def _():
        @pl.when(first_ref[i] == 1)
        def _():
            e = bexp_ref[i]
            s = slot_ref[i]

            @pl.when(i == 0)
            def _():
                for cp in weight_copies(e, s):
                    cp.start()

            for cp in weight_copies(e, s):
                cp.wait()
            @pl.when(next_ref[i] >= 0)
            def _():
                for cp in weight_copies(next_ref[i], 1 - s):
                    cp.start()

            w13_scr[:, 0:D_EXPERT] = st1[s].astype(_BF)
            w13_scr[:, D_EXPERT:2 * D_EXPERT] = st3[s].astype(_BF)
            w2_scr[...] = st2[s].astype(_BF)

        xb = _unpack_rows(xs_ref[...]).astype(_BF)
        ab = _dot(xb, w13_scr[...])
        hid = jax.nn.silu(ab[:, 0:D_EXPERT]) * ab[:, D_EXPERT:2 * D_EXPERT]
        ys_ref[...] = _pack_rows(_dot(hid.astype(_BF), w2_scr[...]))


def _ffn(xs, block_exp, block_row, n_used, run_first, run_slot, run_next, w1, w3, w2, n_blocks):
    blk = lambda i, *_: (_[1][i], 0, 0)
    any_spec = pl.BlockSpec(memory_space=pl.ANY)
    grid_spec = pltpu.PrefetchScalarGridSpec(
        num_scalar_prefetch=6,
        grid=(n_blocks,),
        in_specs=[pl.BlockSpec((MOE_BLOCK,) + ROW_TILE, blk), any_spec, any_spec, any_spec],
        out_specs=pl.BlockSpec((MOE_BLOCK,) + ROW_TILE, blk),
        scratch_shapes=[pltpu.VMEM((D_MODEL, 2 * D_EXPERT), _BF), pltpu.VMEM((D_EXPERT, D_MODEL), _BF),
                        pltpu.VMEM((2, D_MODEL, D_EXPERT), _F32), pltpu.VMEM((2, D_MODEL, D_EXPERT), _F32),
                        pltpu.VMEM((2, D_EXPERT, D_MODEL), _F32), pltpu.SemaphoreType.DMA((2,))],
    )
    return pl.pallas_call(
        _ffn_kernel,
        grid_spec=grid_spec,
        out_shape=jax.ShapeDtypeStruct(xs.shape, ROW_DTYPE),
        compiler_params=pltpu.CompilerParams(dimension_semantics=("arbitrary",), vmem_limit_bytes=VMEM_LIMIT),
        name="moe_ffn",
    )(block_exp, block_row, n_used, run_first, run_slot, run_next, xs, w1, w3, w2)


def _combine_kernel(slots_hbm, x1_ref, rw_ref, g_ref, ys_hbm, y_ref, idx_smem, buf, sem_idx, sem_rows):
    t = pl.program_id(0)
    TC = x1_ref.shape[0]
    idx_cp = pltpu.make_async_copy(slots_hbm.at[pl.ds(t * IDX_CHUNK, IDX_CHUNK)], idx_smem, sem_idx)
    idx_cp.start()
    idx_cp.wait()

    def issue(r, carry):
        for kk in range(2):
            pltpu.make_async_copy(ys_hbm.at[idx_smem[2 * r + kk]], buf.at[kk, r], sem_rows).start(priority=kk)
        return carry

    lax.fori_loop(0, TC, issue, 0, unroll=ISSUE_UNROLL)
    for kk in range(2):
        pltpu.make_async_copy(ys_hbm.at[pl.ds(0, TC)], buf.at[kk], sem_rows).wait()

    w = rw_ref[...]
    y0 = _unpack_rows(buf[0])
    y1 = _unpack_rows(buf[1])
    x2 = x1_ref[...] + (w[:, 0:1] * y0 + w[:, 1:2] * y1)
    y_ref[...] = _rms(x2, g_ref[...])


SC_GATHER_CHUNKS = 4
SC_GATHER_BUFFERS = 3
SC_GATHER_MAX_WINDOW = 80


def _sc_gather(ys, rows):
    Tg = rows.shape[1]
    n_workers = SC_CORES * SC_SUBCORES
    per_w = Tg // n_workers
    assert per_w * n_workers == Tg and per_w % SUBLANES == 0
    W = max(w for w in range(SUBLANES, SC_GATHER_MAX_WINDOW + 1, SUBLANES) if per_w % w == 0)
    n_win = per_w // W
    NB = SC_GATHER_BUFFERS
    idx = rows.reshape(2, n_workers, n_win, W)
    jobs = [(kk, i) for i in range(n_win) for kk in range(2)]
    mesh = plsc.VectorSubcoreMesh(core_axis_name="c", subcore_axis_name="s")

    def body(ys_hbm, idx_hbm, out_hbm, i_v, rows_v, sem_g, sem_s):
        wid = lax.axis_index("s") * SC_CORES + lax.axis_index("c")
        base = wid * per_w
        for kk in range(2):
            pltpu.sync_copy(idx_hbm.at[kk, wid], i_v.at[kk])

        def gather(j):
            kk, i = jobs[j]
            return pltpu.make_async_copy(ys_hbm.at[i_v.at[kk, i]], rows_v.at[j % NB], sem_g.at[j % NB])

        def store(j):
            kk, i = jobs[j]
            return pltpu.make_async_copy(rows_v.at[j % NB], out_hbm.at[kk, pl.ds(base + i * W, W)], sem_s.at[j % NB])

        gather(0).start()
        for j in range(len(jobs)):
            gather(j).wait()
            store(j).start()
            if j + 1 < len(jobs):
                if j + 1 >= NB:
                    store(j + 1 - NB).wait()
                gather(j + 1).start()
        for j in range(max(0, len(jobs) - NB), len(jobs)):
            store(j).wait()

    return pl.kernel(
        body, mesh=mesh,
        out_type=jax.ShapeDtypeStruct((2, Tg) + ROW_TILE, ROW_DTYPE),
        scratch_types=[pltpu.VMEM((2, n_win, W), jnp.int32), pltpu.VMEM((NB, W) + ROW_TILE, ROW_DTYPE),
                       pltpu.SemaphoreType.DMA((NB,)), pltpu.SemaphoreType.DMA((NB,))],
        name="moe_sc_gather",
    )(ys, idx)


def _combine_dense_kernel(x1_ref, rw_ref, g_ref, rows_ref, *rest):
    y_ref = rest[-1]
    w = rw_ref[...]
    y0 = _unpack_rows(rows_ref[0])
    y1 = _unpack_rows(rows_ref[1])
    x2 = x1_ref[...] + (w[:, 0:1] * y0 + w[:, 1:2] * y1)
    y_ref[...] = _rms(x2, g_ref[...])


def _combine_dense(x1, rw, rows, y, g_final, tile, first_tile):
    Tg = rows.shape[1]
    assert Tg % tile == 0
    tok = lambda t: (first_tile + t, 0)
    in_specs = [pl.BlockSpec((tile, D_MODEL), tok), pl.BlockSpec((tile, LANES), tok), _full((1, D_MODEL)),
                pl.BlockSpec((2, tile) + ROW_TILE, lambda t: (0, t, 0, 0))]
    args = [x1, rw, g_final.reshape(1, D_MODEL), rows]
    aliases = {}
    if y is not None:
        in_specs.append(pl.BlockSpec(memory_space=pl.ANY))
        args.append(y)
        aliases = {4: 0}
    return pl.pallas_call(
        _combine_dense_kernel,
        grid=(Tg // tile,),
        in_specs=in_specs,
        out_specs=pl.BlockSpec((tile, D_MODEL), tok),
        out_shape=jax.ShapeDtypeStruct(x1.shape, _F32),
        input_output_aliases=aliases,
        compiler_params=pltpu.CompilerParams(dimension_semantics=("arbitrary",), vmem_limit_bytes=VMEM_LIMIT),
        name="moe_combine_dense",
    )(*args)


def _combine(x1, rw, slots, ys, g_final, tile, n_tiles):
    T = x1.shape[0]
    assert T % tile == 0 and 2 * tile <= IDX_CHUNK and slots.shape[0] == n_tiles * IDX_CHUNK
    any_spec = pl.BlockSpec(memory_space=pl.ANY)
    return pl.pallas_call(
        _combine_kernel,
        grid=(n_tiles,),
        in_specs=[any_spec, pl.BlockSpec((tile, D_MODEL), lambda t: (t, 0)),
                  pl.BlockSpec((tile, LANES), lambda t: (t, 0)), _full((1, D_MODEL)), any_spec],
        out_specs=pl.BlockSpec((tile, D_MODEL), lambda t: (t, 0)),
        out_shape=jax.ShapeDtypeStruct((T, D_MODEL), _F32),
        scratch_shapes=[pltpu.SMEM((IDX_CHUNK,), jnp.int32), pltpu.VMEM((2, tile) + ROW_TILE, ROW_DTYPE),
                        pltpu.SemaphoreType.DMA, pltpu.SemaphoreType.DMA],
        compiler_params=pltpu.CompilerParams(dimension_semantics=("arbitrary",), vmem_limit_bytes=VMEM_LIMIT),
        name="moe_combine",
    )(slots, x1, rw, g_final.reshape(1, D_MODEL), ys)


def _tile_for(n_tokens):
    tile = IDX_CHUNK // 2
    return tile if n_tokens % tile == 0 else n_tokens


def _chunked_slots(slot, tile):
    n_tiles = slot.shape[0] // tile
    s = slot.reshape(n_tiles, 2 * tile)
    return jnp.pad(s, ((0, 0), (0, IDX_CHUNK - 2 * tile))).reshape(-1)


def kernel(x_prompt, x_sample, state_pool, state_ret, g_mix, w_in, w_pool, pool_scale, ret_gn, w_out, g_ffn, w_grp, w_exp, w1, w3, w2, g_final):
    Bp, Lp, _ = x_prompt.shape
    Bs = x_sample.shape[0]
    Tp = Bp * Lp
    wts = _prep_weights(g_mix[0], w_in[0], w_pool[0], pool_scale[0], ret_gn[0], w_out[0], g_ffn[0], w_grp[0], w_exp[0])

    T_all = Tp + Bs
    cap = (-(-T_all // MOE_BLOCK) + 1) * MOE_BLOCK
    cap_blocks = cap // MOE_BLOCK

    x1p, dst_rows_p, rwp, npool_p, nret_p, counts_p, h2p = _mixer_prompt(x_prompt, wts, cap)
    xs = _sc_scatter(h2p, dst_rows_p, N_EXPERTS * cap)
    x1s, dst_s, rws, npool_s, nret_s, counts, h2s = _mixer_sample(
        x_sample.reshape(Bs, D_MODEL), jnp.swapaxes(state_pool[0], 0, 1), state_ret[0], wts, PAST_LEN, counts_p, cap)

    counts = counts[0, :N_EXPERTS].astype(jnp.int32)
    tile_p, tile_s = _tile_for(Tp), _tile_for(Bs)
    slots_s = _chunked_slots(dst_s[:, :2], tile_s)
    zstart = jnp.arange(N_EXPERTS, dtype=jnp.int32) * cap + counts
    zchunks = ((-counts) % MOE_BLOCK + ZERO_CHUNK - 1) // ZERO_CHUNK
    xs = _finish_dispatch(xs, h2s, slots_s, zstart, zchunks.astype(jnp.int32))
    nblk = (counts + MOE_BLOCK - 1) // MOE_BLOCK
    blk_end = jnp.cumsum(nblk)
    n_used = blk_end[-1:].astype(jnp.int32)
    n_blocks = -(-2 * T_all // MOE_BLOCK) + N_EXPERTS
    step = jnp.minimum(jnp.arange(n_blocks, dtype=jnp.int32), n_used - 1)
    block_exp = jnp.minimum(jnp.sum((blk_end[None, :] <= step[:, None]).astype(jnp.int32), axis=1), N_EXPERTS - 1)
    blk_start = jnp.sum(jnp.where(block_exp[:, None] == jnp.arange(N_EXPERTS, dtype=jnp.int32), (blk_end - nblk)[None, :], 0),
                        axis=1)
    block_row = (block_exp * cap_blocks + (step - blk_start)).astype(jnp.int32)
    block_exp = block_exp.astype(jnp.int32)
    experts = jnp.arange(N_EXPERTS, dtype=jnp.int32)
    nonempty = nblk > 0
    run_first = (step == blk_start).astype(jnp.int32)
    run_slot = (jnp.sum((nonempty[None, :] & (experts[None, :] < block_exp[:, None])).astype(jnp.int32), axis=1) % 2)
    later = nonempty[None, :] & (experts[None, :] > block_exp[:, None])
    run_next = jnp.min(jnp.where(later, experts[None, :], N_EXPERTS), axis=1)
    run_next = jnp.where(run_next == N_EXPERTS, -1, run_next).astype(jnp.int32)

    ys = _ffn(xs, block_exp, block_row, n_used, run_first, run_slot.astype(jnp.int32), run_next,
              w1[0], w3[0], w2[0], n_blocks)
    n_tiles_p = Tp // tile_p
    chunk_tiles = [n_tiles_p // SC_GATHER_CHUNKS + (k < n_tiles_p % SC_GATHER_CHUNKS)
                   for k in range(SC_GATHER_CHUNKS)]
    gathered, first = [], 0
    for nt in chunk_tiles:
        if nt:
            gathered.append((first, _sc_gather(ys, dst_rows_p[:, first * tile_p:(first + nt) * tile_p])))
            first += nt
    y_s = _combine(x1s, rws, slots_s, ys, g_final, tile_s, 1)
    y_p = None
    for first, rows in gathered:
        y_p = _combine_dense(x1p, rwp, rows, y_p, g_final, tile_p, first)

    return (y_p.reshape(Bp, Lp, D_MODEL), y_s.reshape(Bs, 1, D_MODEL),
            npool_p[None], nret_p[None], jnp.swapaxes(npool_s, 0, 1)[None], nret_s[None])
```

```python
import functools

import jax
import jax.numpy as jnp
import numpy as np
from jax import lax
from jax.experimental import pallas as pl
from jax.experimental.pallas import tpu as pltpu
from jax.experimental.pallas import tpu_sc as plsc

D_MODEL = 1024
EPS = 1e-6
POOL_GROUPS = 4
POOL_IN = D_MODEL // 2
POOL_GC = POOL_IN // POOL_GROUPS
POOL_OUT_GC = D_MODEL // POOL_GROUPS
POOL_WINDOWS = (2, 4, 8, 16)
POOL_HIST = max(POOL_WINDOWS) - 1
RET_HEADS = 4
RET_DK = D_MODEL // 8
RET_DV = D_MODEL // RET_HEADS
ROPE_BASE = 10000.0
PAST_LEN = 16384
N_GROUPS = 4
EXPERTS_PER_GROUP = 8
N_EXPERTS = N_GROUPS * EXPERTS_PER_GROUP
D_EXPERT = D_MODEL // 4
QK_W = RET_HEADS * RET_DK
V_W = RET_HEADS * RET_DV
OFF_U = 0
OFF_Q = POOL_IN
OFF_K = OFF_Q + QK_W
OFF_V = OFF_K + QK_W
OFF_GA = OFF_V + V_W
OFF_GB = OFF_GA + D_MODEL
IN_WIDTH = OFF_GB + D_MODEL

LANES = 128
SUBLANES = 8
ROW_WORDS = D_MODEL // 2
ROW_TILE = (ROW_WORDS // LANES, LANES)
ROW_DTYPE = jnp.uint32
HALO = 16
RET_CHUNK = 256
MIXER_SEQS = 2
MOE_BLOCK = 512
ROUTER_W = LANES
VMEM_LIMIT = 56 * 1024 * 1024

_BF = jnp.bfloat16
_F32 = jnp.float32


def _rms(x, g):
    inv = lax.rsqrt(jnp.mean(x * x, axis=-1, keepdims=True) + EPS)
    return x * inv * g


def _dot(a, b):
    return jnp.dot(a, b, preferred_element_type=_F32)


def _round_bf16(x):
    return x.astype(_BF).astype(_F32)


def _pack_rows(x):
    lo = lax.bitcast_convert_type(x[:, :ROW_WORDS].astype(_BF).astype(_F32), jnp.uint32)
    hi = lax.bitcast_convert_type(x[:, ROW_WORDS:].astype(_BF).astype(_F32), jnp.uint32)
    return ((lo >> 16) | hi).reshape((x.shape[0],) + ROW_TILE)


def _unpack_rows(w):
    w = w.reshape(w.shape[0], ROW_WORDS)
    lo = lax.bitcast_convert_type(w << 16, _F32)
    hi = lax.bitcast_convert_type(w & jnp.uint32(0xFFFF0000), _F32)
    return jnp.concatenate([lo, hi], axis=1)


def _rotary(x, cos2, sin2):
    return x * cos2 + pltpu.roll(x, RET_DK // 2, 1) * sin2


def _route(logits):
    lane = lax.broadcasted_iota(jnp.int32, logits.shape, 1).astype(_F32)
    neg = jnp.float32(-jnp.inf)
    big = jnp.float32(1 << 20)
    lg = jnp.where(lane < N_GROUPS, logits, neg)
    mg = jnp.max(lg, axis=-1, keepdims=True)
    g_idx = jnp.min(jnp.where(lg == mg, lane, big), axis=-1, keepdims=True)
    p_g = 1.0 / jnp.sum(jnp.exp(lg - mg), axis=-1, keepdims=True)
    lo = N_GROUPS + g_idx * EXPERTS_PER_GROUP
    in_grp = (lane >= lo) & (lane < lo + EXPERTS_PER_GROUP)
    le = jnp.where(in_grp, logits, neg)
    m1 = jnp.max(le, axis=-1, keepdims=True)
    i1 = jnp.min(jnp.where(le == m1, lane, big), axis=-1, keepdims=True)
    le2 = jnp.where(lane == i1, neg, le)
    m2 = jnp.max(le2, axis=-1, keepdims=True)
    i2 = jnp.min(jnp.where(le2 == m2, lane, big), axis=-1, keepdims=True)
    t = jnp.exp(m2 - m1)
    den = 1.0 + t
    e0 = (i1 - N_GROUPS).astype(jnp.int32)
    e1 = (i2 - N_GROUPS).astype(jnp.int32)
    return e0, e1, p_g * (1.0 / den), p_g * (t / den)


def _post_mix(x, mixed_ref, w_out_ref, g_ffn_ref, w_rt_ref, x1_ref, rw_ref):
    x1 = x + _dot(mixed_ref[...], w_out_ref[...])
    x1_ref[...] = x1.reshape(x1_ref.shape)
    h2 = _rms(x1, g_ffn_ref[...])
    e0, e1, w0, w1 = _route(_dot(h2.astype(_BF), w_rt_ref[...]))
    lane = lax.broadcasted_iota(jnp.int32, (x.shape[0], LANES), 1)
    rw_ref[...] = jnp.where(lane == 0, w0, jnp.where(lane == 1, w1, 0.0)).reshape(rw_ref.shape)
    return h2, e0, e1


def _rank_rows(e0, e1, cap, carry_scr):
    R = e0.shape[0]
    lane = lax.broadcasted_iota(jnp.int32, (R, LANES), 1)
    m0 = lane == e0
    m1 = lane == e1
    onehot = jnp.where(m0 | m1, 1.0, 0.0)
    r_i = lax.broadcasted_iota(jnp.int32, (R, R), 0)
    c_i = lax.broadcasted_iota(jnp.int32, (R, R), 1)
    tri = jnp.where(c_i < r_i, 1.0, 0.0).astype(_BF)
    before = _dot(tri, onehot.astype(_BF)) + carry_scr[...]
    d0 = e0.astype(_F32) * cap + jnp.sum(jnp.where(m0, before, 0.0), axis=-1, keepdims=True)
    d1 = e1.astype(_F32) * cap + jnp.sum(jnp.where(m1, before, 0.0), axis=-1, keepdims=True)
    carry_scr[...] += jnp.sum(onehot, axis=0, keepdims=True)
    return jnp.where(lane == 0, d0, jnp.where(lane == 1, d1, 0.0))


def _group_norm(o, gain):
    mu = jnp.mean(o, axis=-1, keepdims=True)
    d = o - mu
    var = jnp.mean(d * d, axis=-1, keepdims=True)
    return d * lax.rsqrt(var + EPS) * gain


def _mixer_prompt_kernel(x_ref, g_mix_ref, w_in_ref, w_pool_ref, pscale_ref, gn_ref, w_out_ref, g_ffn_ref,
                         w_rt_ref, cos_ref, sin_ref, dmask_ref, xi_ref, zeta_ref, gc_ref,
                         x1_ref, dst_t_ref, rw_ref, npool_ref, nret_ref, cnt_ref, h2_ref,
                         u_scr, s_scr, mixed_scr, carry_scr, *, cap):
    c = pl.program_id(1)
    n_c = pl.num_programs(1)
    @pl.when((pl.program_id(0) == 0) & (c == 0))
    def _():
        carry_scr[...] = jnp.zeros(carry_scr.shape, _F32)
    NSEQ, C, _ = x_ref.shape
    R = NSEQ * C

    @pl.when(c == 0)
    def _():
        u_scr[:, 0:HALO, :] = jnp.zeros((NSEQ, HALO, POOL_IN), _F32)
        s_scr[...] = jnp.zeros(s_scr.shape, _F32)

    x = x_ref[...].reshape(R, D_MODEL)
    h = _rms(x, g_mix_ref[...]).astype(_BF)

    u_scr[:, HALO:HALO + C, :] = _dot(h, w_in_ref[:, OFF_U:OFF_U + POOL_IN]).reshape(NSEQ, C, POOL_IN)
    q = _dot(h, w_in_ref[:, OFF_Q:OFF_Q + QK_W])
    k = _dot(h, w_in_ref[:, OFF_K:OFF_K + QK_W])
    cos2 = jnp.concatenate([cos_ref[...]] * NSEQ, axis=0)
    sin2 = jnp.concatenate([sin_ref[...]] * NSEQ, axis=0)
    pos1 = (c * C + 1 + lax.broadcasted_iota(jnp.int32, (C, POOL_GC), 0)).astype(_F32)

    for j in range(RET_HEADS):
        win = POOL_WINDOWS[j]
        cs = slice(j * POOL_GC, (j + 1) * POOL_GC)
        n_rows = jnp.minimum(pos1, jnp.float32(win))
        pooled = []
        for sq in range(NSEQ):
            u_j = u_scr[sq, HALO:HALO + C, cs]
            s = u_j
            for d in range(1, win):
                s = s + u_scr[sq, HALO - d:HALO - d + C, cs]
            pooled.append(s / n_rows - u_j)
        osl = slice(j * POOL_OUT_GC, (j + 1) * POOL_OUT_GC)
        pool_out = _dot(jnp.concatenate(pooled, axis=0).astype(_BF), w_pool_ref[j]) * pscale_ref[:, osl]

        qs = slice(j * RET_DK, (j + 1) * RET_DK)
        qb = _rotary(q[:, qs], cos2, sin2).astype(_BF)
        kf = _rotary(k[:, qs], cos2, sin2) * (RET_DK ** -0.5)
        kb = kf.astype(_BF)
        vb = _dot(h, w_in_ref[:, OFF_V + j * RET_DV:OFF_V + (j + 1) * RET_DV]).astype(_BF)
        ret = []
        for sq in range(NSEQ):
            rows = slice(sq * C, (sq + 1) * C)
            scores = lax.dot_general(qb[rows], kb[rows], (((1,), (1,)), ((), ())),
                                     preferred_element_type=_F32) * dmask_ref[j]
            s_old = s_scr[sq, j]
            o = _dot(scores.astype(_BF), vb[rows]) + _dot(qb[rows], s_old.astype(_BF)) * xi_ref[j]
            kz = (kf[rows] * zeta_ref[j]).astype(_BF)
            s_scr[sq, j] = gc_ref[j] * s_old + lax.dot_general(kz, vb[rows], (((0,), (0,)), ((), ())),
                                                               preferred_element_type=_F32)
            ret.append(_group_norm(o, gn_ref[:, osl]))
        ret_out = jnp.concatenate(ret, axis=0)

        ga = _dot(h, w_in_ref[:, OFF_GA + j * RET_DV:OFF_GA + (j + 1) * RET_DV])
        gb = _dot(h, w_in_ref[:, OFF_GB + j * RET_DV:OFF_GB + (j + 1) * RET_DV])
        mixed_scr[:, osl] = (jax.nn.sigmoid(ga) * pool_out + jax.nn.sigmoid(gb) * ret_out).astype(_BF)

    h2, e0, e1 = _post_mix(x, mixed_scr, w_out_ref, g_ffn_ref, w_rt_ref, x1_ref, rw_ref)

    h2_ref[...] = _pack_rows(h2).reshape(h2_ref.shape)
    dst = _rank_rows(e0, e1, cap, carry_scr)
    dst_t =dst.T[0:SUBLANES, :].astype(jnp.int32)
    for sq in range(NSEQ):
        dst_t_ref[sq] = dst_t[:, sq * C:(sq + 1) * C]
    cnt_ref[...] = carry_scr[...]

    u_scr[:, 0:HALO, :] = u_scr[:, C:C + HALO, :]

    @pl.when(c == n_c - 1)
    def _():
        npool_ref[...] = u_scr[:, HALO + C - POOL_HIST:HALO + C, :]
        nret_ref[...] = s_scr[...]


def _decay_tables(C):
    f32 = np.float32
    log_g = np.log(f32(1.0) - np.exp2(f32(-5.0) - np.arange(RET_HEADS, dtype=f32)))
    i = np.arange(C, dtype=f32)
    diff = i[:, None] - i[None, :]
    dmask = np.where(diff >= 0, np.exp(np.maximum(diff, f32(0.0))[None] * log_g[:, None, None]), f32(0.0))
    xi = np.exp((i[None, :] + f32(1.0)) * log_g[:, None])
    zeta = np.exp((f32(C) - f32(1.0) - i)[None, :] * log_g[:, None])
    g_chunk = np.exp(f32(C) * log_g)
    return dmask.astype(f32), xi.astype(f32), zeta.astype(f32), g_chunk.astype(f32)


def _rope_tables(pos):
    half = RET_DK // 2
    freqs = ROPE_BASE ** (-jnp.arange(half, dtype=_F32) / half)
    ang = pos[:, None] * freqs[None, :]
    cos, sin = jnp.cos(ang), jnp.sin(ang)
    return jnp.concatenate([cos, cos], axis=-1), jnp.concatenate([-sin, sin], axis=-1)


def _const(a, shape=None):
    if shape is not None:
        a = np.ascontiguousarray(np.broadcast_to(a, shape))
    return jnp.asarray(a)


def _full(shape):
    n = len(shape)
    return pl.BlockSpec(shape, lambda *_: (0,) * n)


def _mixer_prompt(x, wts, cap):
    B, L, _ = x.shape
    C = RET_CHUNK if L % RET_CHUNK == 0 else L
    n_c = L // C
    T = B * L
    dmask, xi, zeta, g_chunk = _decay_tables(C)
    dmask = _const(dmask)
    xi_b = _const(xi[:, :, None], (RET_HEADS, C, RET_DV))
    zeta_b = _const(zeta[:, :, None], (RET_HEADS, C, RET_DK))
    gc_b = _const(g_chunk[:, None, None], (RET_HEADS, 1, RET_DV))
    cos2, sin2 = _rope_tables(jnp.arange(L).astype(_F32))

    NSEQ = MIXER_SEQS if B % MIXER_SEQS == 0 else 1
    B2 = B // NSEQ
    T2 = T // NSEQ
    x = x.reshape(NSEQ, B2, L, D_MODEL)
    tok = lambda b, c: (0, b * n_c + c, 0)
    in_specs = [
        pl.BlockSpec((NSEQ, None, C, D_MODEL), lambda b, c: (0, b, c, 0)),
        _full((1, D_MODEL)), _full((D_MODEL, IN_WIDTH)), _full((POOL_GROUPS, POOL_GC, POOL_OUT_GC)),
        _full((1, D_MODEL)), _full((1, D_MODEL)), _full((D_MODEL, D_MODEL)), _full((1, D_MODEL)),
        _full((D_MODEL, ROUTER_W)),
        pl.BlockSpec((C, RET_DK), lambda b, c: (c, 0)), pl.BlockSpec((C, RET_DK), lambda b, c: (c, 0)),
        _full((RET_HEADS, C, C)), _full((RET_HEADS, C, RET_DV)), _full((RET_HEADS, C, RET_DK)),
        _full((RET_HEADS, 1, RET_DV)),
    ]
    R = NSEQ * C
    out_shape = (
        jax.ShapeDtypeStruct((NSEQ, T2, D_MODEL), _F32),
        jax.ShapeDtypeStruct((NSEQ, SUBLANES, T2), jnp.int32),
        jax.ShapeDtypeStruct((NSEQ, T2, LANES), _F32),
        jax.ShapeDtypeStruct((NSEQ, B2, POOL_HIST, POOL_IN), _F32),
        jax.ShapeDtypeStruct((NSEQ, B2, RET_HEADS, RET_DK, RET_DV), _F32),
        jax.ShapeDtypeStruct((1, LANES), _F32),
        jax.ShapeDtypeStruct((NSEQ, T2) + ROW_TILE, ROW_DTYPE),
    )
    out_specs = (
        pl.BlockSpec((NSEQ, C, D_MODEL), tok),
        pl.BlockSpec((NSEQ, SUBLANES, C), lambda b, c: (0, 0, b * n_c + c)),
        pl.BlockSpec((NSEQ, C, LANES), tok),
        pl.BlockSpec((NSEQ, None, POOL_HIST, POOL_IN), lambda b, c: (0, b, 0, 0)),
        pl.BlockSpec((NSEQ, None, RET_HEADS, RET_DK, RET_DV), lambda b, c: (0, b, 0, 0, 0)),
        _full((1, LANES)),
        pl.BlockSpec((NSEQ, C) + ROW_TILE, lambda b, c: (0, b * n_c + c, 0, 0)),
    )
    x1, dst_t, rw, npool, nret, counts, h2 = pl.pallas_call(
        functools.partial(_mixer_prompt_kernel, cap=cap),
        grid=(B2, n_c),
        in_specs=in_specs,
        out_specs=out_specs,
        out_shape=out_shape,
        scratch_shapes=[pltpu.VMEM((NSEQ, HALO + C, POOL_IN), _F32),
                        pltpu.VMEM((NSEQ, RET_HEADS, RET_DK, RET_DV), _F32),
                        pltpu.VMEM((R, D_MODEL), _BF),
                        pltpu.VMEM((1, LANES), _F32)],
        compiler_params=pltpu.CompilerParams(dimension_semantics=("arbitrary", "arbitrary"),
                                             vmem_limit_bytes=VMEM_LIMIT),
        name="mixer_prompt",
    )(x, wts["g_mix"], wts["w_in"], wts["w_pool"], wts["pool_scale"], wts["ret_gn"], wts["w_out"],
      wts["g_ffn"], wts["w_rt"], cos2, sin2, dmask, xi_b, zeta_b, gc_b)
    dst_rows = jnp.moveaxis(dst_t[:, 0:2, :], 1, 0).reshape(2, T)
    return (x1.reshape(T, D_MODEL), dst_rows, rw.reshape(T, LANES),
            npool.reshape(B, POOL_HIST, POOL_IN), nret.reshape(B, RET_HEADS, RET_DK, RET_DV), counts,
            h2.reshape((T,) + ROW_TILE))


def _prep_weights(g_mix, w_in, w_pool, pool_scale, ret_gn, w_out, g_ffn, w_grp, w_exp):
    w_rt = jnp.concatenate([w_grp, w_exp.reshape(D_MODEL, N_EXPERTS)], axis=1)
    w_rt = jnp.pad(w_rt, ((0, 0), (0, ROUTER_W - w_rt.shape[1])))
    row = lambda v: v.reshape(1, D_MODEL)
    return dict(g_mix=row(g_mix), w_in=w_in.astype(_BF), w_pool=w_pool.astype(_BF), pool_scale=row(pool_scale),
                ret_gn=row(ret_gn), w_out=w_out.astype(_BF), g_ffn=row(g_ffn), w_rt=w_rt.astype(_BF))


SAMPLE_TB = 8


def _mixer_sample_kernel(x_ref, spool_ref, sret_ref, g_mix_ref, w_in_ref, w_pool_ref, pscale_ref, gn_ref,
                         w_out_ref, g_ffn_ref, w_rt_ref, cos_ref, sin_ref, dm_ref, xi_ref, zeta_ref, gc_ref,
                         carry_in_ref,
                         x1_ref, dst_ref, rw_ref, npool_ref, nret_ref, cnt_ref, h2_ref,
                         u_scr, q_scr, k_scr, qt_scr, kt_scr, v_scr, ga_scr, gb_scr, pooled_scr, o_scr, mixed_scr,
                         carry_scr, *, cap):
    t = pl.program_id(0)
    n_t = pl.num_programs(0)
    Bs = x_ref.shape[0]
    TB = sret_ref.shape[0]

    @pl.when(t == 0)
    def _():
        h = _rms(x_ref[...], g_mix_ref[...]).astype(_BF)
        u_scr[...] = _dot(h, w_in_ref[:, OFF_U:OFF_U + POOL_IN])
        q = _dot(h, w_in_ref[:, OFF_Q:OFF_Q + QK_W])
        k = _dot(h, w_in_ref[:, OFF_K:OFF_K + QK_W])
        for j in range(RET_HEADS):
            qs = slice(j * RET_DK, (j + 1) * RET_DK)
            qf = _rotary(q[:, qs], cos_ref[...], sin_ref[...])
            kf = _rotary(k[:, qs], cos_ref[...], sin_ref[...]) * (RET_DK ** -0.5)
            q_scr[:, qs] = _round_bf16(qf)
            k_scr[:, qs] = _round_bf16(kf)
            qt_scr[j] = _round_bf16(qf).T
            kt_scr[j] = kf.T
        v_scr[...] = _round_bf16(_dot(h, w_in_ref[:, OFF_V:OFF_V + V_W]))
        ga_scr[...] = _dot(h, w_in_ref[:, OFF_GA:OFF_GA + D_MODEL])
        gb_scr[...] = _dot(h, w_in_ref[:, OFF_GB:OFF_GB + D_MODEL])

    shift = (Bs - t * TB) % Bs
    qt = [pltpu.roll(qt_scr[j], shift, 1) for j in range(RET_HEADS)]
    kt = [pltpu.roll(kt_scr[j], shift, 1) for j in range(RET_HEADS)]

    blk = pl.ds(pl.multiple_of(t * TB, TB), TB)
    u_blk = u_scr[blk, :]
    q_blk = q_scr[blk, :]
    k_blk = k_scr[blk, :]
    v_blk = v_scr[blk, :]
    score = [_round_bf16(jnp.sum(q_blk[:, j * RET_DK:(j + 1) * RET_DK] * k_blk[:, j * RET_DK:(j + 1) * RET_DK],
                                 axis=-1, keepdims=True) * dm_ref[j]) for j in range(RET_HEADS)]

    groups = []
    for g, win in enumerate(POOL_WINDOWS):
        cs = slice(g * POOL_GC, (g + 1) * POOL_GC)
        s = u_blk[:, cs]
        for r in range(POOL_HIST - (win - 1), POOL_HIST):
            s = s + spool_ref[r, :, cs]
        groups.append(s / jnp.float32(win) - u_blk[:, cs])
    pooled_scr[blk, :] = jnp.concatenate(groups, axis=1)
    npool_ref[0:POOL_HIST - 1] = spool_ref[1:POOL_HIST]
    npool_ref[POOL_HIST - 1] = u_blk

    o_rows = []
    for i in range(TB):
        heads = []
        for j in range(RET_HEADS):
            s_old = sret_ref[i, j]
            v_row = v_blk[i:i + 1, j * RET_DV:(j + 1) * RET_DV]
            qcol = qt[j][:, i:i + 1]
            kcol = kt[j][:, i:i + 1]
            qs_old = jnp.sum(qcol * _round_bf16(s_old), axis=0, keepdims=True)
            heads.append(score[j][i:i + 1, :] * v_row + qs_old * xi_ref[j])
            nret_ref[i, j] = gc_ref[j] * s_old + _round_bf16(kcol * zeta_ref[j]) * v_row
        o_rows.append(jnp.concatenate(heads, axis=1))
    o_scr[blk, :] = jnp.concatenate(o_rows, axis=0)

    @pl.when(t == n_t - 1)
    def _():
        for j in range(RET_HEADS):
            cs = slice(j * POOL_GC, (j + 1) * POOL_GC)
            osl = slice(j * POOL_OUT_GC, (j + 1) * POOL_OUT_GC)
            pool_out = _dot(pooled_scr[:, cs].astype(_BF), w_pool_ref[j]) * pscale_ref[:, osl]
            ret_out = _group_norm(o_scr[:, osl], gn_ref[:, osl])
            mixed_scr[:, osl] = (jax.nn.sigmoid(ga_scr[:, osl]) * pool_out
                                 + jax.nn.sigmoid(gb_scr[:, osl]) * ret_out).astype(_BF)
        h2, e0, e1 = _post_mix(x_ref[...], mixed_scr, w_out_ref, g_ffn_ref, w_rt_ref, x1_ref, rw_ref)
        h2_ref[...] = _pack_rows(h2).reshape(h2_ref.shape)
        carry_scr[...] = carry_in_ref[...]
        dst_ref[...] = _rank_rows(e0, e1, cap, carry_scr).astype(jnp.int32)
        cnt_ref[...] = carry_scr[...]


def _mixer_sample(x, state_pool, state_ret, wts, pos0, carry_in, cap):
    Bs = x.shape[0]
    assert Bs == LANES and Bs % SAMPLE_TB == 0
    TB = SAMPLE_TB
    dmask, xi, zeta, g_chunk = _decay_tables(1)
    dm_b = _const(dmask, (RET_HEADS, 1, 1))
    xi_b = _const(xi[:, :, None], (RET_HEADS, 1, RET_DV))
    zeta_b = _const(zeta[:, :, None], (RET_HEADS, 1, 1))
    gc_b = _const(g_chunk[:, None, None], (RET_HEADS, 1, RET_DV))
    cos2, sin2 = _rope_tables((pos0 + jnp.arange(1)).astype(_F32))

    in_specs = [
        _full((Bs, D_MODEL)),
        pl.BlockSpec((POOL_HIST, TB, POOL_IN), lambda t: (0, t, 0)),
        pl.BlockSpec((TB, RET_HEADS, RET_DK, RET_DV), lambda t: (t, 0, 0, 0)),
        _full((1, D_MODEL)), _full((D_MODEL, IN_WIDTH)), _full((POOL_GROUPS, POOL_GC, POOL_OUT_GC)),
        _full((1, D_MODEL)), _full((1, D_MODEL)), _full((D_MODEL, D_MODEL)), _full((1, D_MODEL)),
        _full((D_MODEL, ROUTER_W)),
        _full((1, RET_DK)), _full((1, RET_DK)),
        _full((RET_HEADS, 1, 1)), _full((RET_HEADS, 1, RET_DV)), _full((RET_HEADS, 1, 1)),
        _full((RET_HEADS, 1, RET_DV)),
        _full((1, LANES)),
    ]
    out_shape = (
        jax.ShapeDtypeStruct((Bs, D_MODEL), _F32),
        jax.ShapeDtypeStruct((Bs, LANES), jnp.int32),
        jax.ShapeDtypeStruct((Bs, LANES), _F32),
        jax.ShapeDtypeStruct((POOL_HIST, Bs, POOL_IN), _F32),
        jax.ShapeDtypeStruct((Bs, RET_HEADS, RET_DK, RET_DV), _F32),
        jax.ShapeDtypeStruct((1, LANES), _F32),
        jax.ShapeDtypeStruct((Bs,) + ROW_TILE, ROW_DTYPE),
    )
    out_specs = (
        _full((Bs, D_MODEL)), _full((Bs, LANES)), _full((Bs, LANES)),
        pl.BlockSpec((POOL_HIST, TB, POOL_IN), lambda t: (0, t, 0)),
        pl.BlockSpec((TB, RET_HEADS, RET_DK, RET_DV), lambda t: (t, 0, 0, 0)),
        _full((1, LANES)), _full((Bs,) + ROW_TILE),
    )
    f32 = lambda *s: pltpu.VMEM(s, _F32)
    return pl.pallas_call(
        functools.partial(_mixer_sample_kernel, cap=cap),
        grid=(Bs // TB,),
        in_specs=in_specs,
        out_specs=out_specs,
        out_shape=out_shape,
        scratch_shapes=[f32(Bs, POOL_IN), f32(Bs, QK_W), f32(Bs, QK_W), f32(RET_HEADS, RET_DK, Bs),
                        f32(RET_HEADS, RET_DK, Bs), f32(Bs, V_W), f32(Bs, D_MODEL), f32(Bs, D_MODEL),
                        f32(Bs, POOL_IN), f32(Bs, V_W), pltpu.VMEM((Bs, D_MODEL), _BF), f32(1, LANES)],
        compiler_params=pltpu.CompilerParams(dimension_semantics=("arbitrary",), vmem_limit_bytes=VMEM_LIMIT),
        name="mixer_sample",
    )(x, state_pool, state_ret, wts["g_mix"], wts["w_in"], wts["w_pool"], wts["pool_scale"], wts["ret_gn"],
      wts["w_out"], wts["g_ffn"], wts["w_rt"], cos2, sin2, dm_b, xi_b, zeta_b, gc_b, carry_in)


IDX_CHUNK = 1024
ISSUE_UNROLL = 8
SC_CORES = 2
SC_SUBCORES = 16
SC_WINDOW = 64


def _sc_scatter(h2, dst_rows, n_rows):
    T = h2.shape[0]
    n_workers = SC_CORES * SC_SUBCORES
    W = SC_WINDOW
    assert T % (n_workers * W) == 0
    per_w = T // n_workers
    n_win = per_w // W
    d0 = dst_rows[0].reshape(T // W, W)
    d1 = dst_rows[1].reshape(T // W, W)
    mesh = plsc.VectorSubcoreMesh(core_axis_name="c", subcore_axis_name="s")

    def body(h2_hbm, d0_hbm, d1_hbm, xs_hbm, i0_v, i1_v, rows_v, sem_load, sem_store):
        wid = lax.axis_index("s") * SC_CORES + lax.axis_index("c")
        base = wid * per_w
        pltpu.sync_copy(d0_hbm.at[pl.ds(wid * n_win, n_win)], i0_v)
        pltpu.sync_copy(d1_hbm.at[pl.ds(wid * n_win, n_win)], i1_v)
        pltpu.async_copy(h2_hbm.at[pl.ds(base, W)], rows_v.at[0], sem_load)
        for i in range(n_win):
            b = i % 2
            pltpu.make_async_copy(h2_hbm.at[pl.ds(base, W)], rows_v.at[b], sem_load).wait()
            if i + 1 < n_win:
                pltpu.async_copy(h2_hbm.at[pl.ds(base + (i + 1) * W, W)], rows_v.at[1 - b], sem_load)
            c0 = pltpu.async_copy(rows_v.at[b], xs_hbm.at[i0_v.at[i]], sem_store)
            c1 = pltpu.async_copy(rows_v.at[b], xs_hbm.at[i1_v.at[i]], sem_store)
            c0.wait()
            c1.wait()

    return pl.kernel(
        body, mesh=mesh,
        out_type=jax.ShapeDtypeStruct((n_rows,) + ROW_TILE, ROW_DTYPE),
        scratch_types=[pltpu.VMEM((n_win, W), jnp.int32), pltpu.VMEM((n_win, W), jnp.int32),
                       pltpu.VMEM((2, W) + ROW_TILE, ROW_DTYPE),
                       pltpu.SemaphoreType.DMA, pltpu.SemaphoreType.DMA],
        name="moe_sc_scatter",
    )(h2, d0, d1)


ZERO_CHUNK = 64


def _finish_dispatch_kernel(zstart_ref, zchunks_ref, slots_hbm, h2_ref, xs_in_hbm, xs_hbm, idx_smem, zbuf, sem_idx,
                            sem_rows, sem_zero):
    del xs_in_hbm
    TS = h2_ref.shape[0]
    idx_cp = pltpu.make_async_copy(slots_hbm.at[pl.ds(0, IDX_CHUNK)], idx_smem, sem_idx)
    idx_cp.start()
    zbuf[...] = jnp.zeros(zbuf.shape, zbuf.dtype)

    def zero_copy(e, j):
        return pltpu.make_async_copy(zbuf, xs_hbm.at[pl.ds(zstart_ref[e] + j * ZERO_CHUNK, ZERO_CHUNK)], sem_zero)

    for e in range(N_EXPERTS):
        lax.fori_loop(0, zchunks_ref[e], lambda j, c, e=e: (zero_copy(e, j).start(), c)[1], 0)
    idx_cp.wait()

    def issue(r, carry):
        for kk in range(2):
            pltpu.make_async_copy(h2_ref.at[r], xs_hbm.at[idx_smem[2 * r + kk]], sem_rows).start(priority=kk)
        return carry

    lax.fori_loop(0, TS, issue, 0, unroll=ISSUE_UNROLL)
    for e in range(N_EXPERTS):
        lax.fori_loop(0, zchunks_ref[e], lambda j, c, e=e: (zero_copy(e, j).wait(), c)[1], 0)
    for kk in range(2):
        pltpu.make_async_copy(h2_ref, xs_hbm.at[pl.ds(0, TS)], sem_rows).wait()


def _finish_dispatch(xs, h2s, slots, zstart, zchunks):
    Bs = h2s.shape[0]
    assert 2 * Bs <= IDX_CHUNK and slots.shape[0] == IDX_CHUNK and MOE_BLOCK % ZERO_CHUNK == 0
    any_spec = pl.BlockSpec(memory_space=pl.ANY)
    grid_spec = pltpu.PrefetchScalarGridSpec(
        num_scalar_prefetch=2,
        grid=(1,),
        in_specs=[any_spec, pl.BlockSpec((Bs,) + ROW_TILE, lambda t, z, n: (0, 0, 0)), any_spec],
        out_specs=any_spec,
        scratch_shapes=[pltpu.SMEM((IDX_CHUNK,), jnp.int32), pltpu.VMEM((ZERO_CHUNK,) + ROW_TILE, ROW_DTYPE),
                        pltpu.SemaphoreType.DMA, pltpu.SemaphoreType.DMA, pltpu.SemaphoreType.DMA],
    )
    return pl.pallas_call(
        _finish_dispatch_kernel,
        grid_spec=grid_spec,
        out_shape=jax.ShapeDtypeStruct(xs.shape, ROW_DTYPE),
        input_output_aliases={4: 0},
        compiler_params=pltpu.CompilerParams(dimension_semantics=("arbitrary",), has_side_effects=True),
        name="moe_finish_dispatch",
    )(zstart, zchunks, slots, h2s, xs)


def _ffn_kernel(bexp_ref, brow_ref, nused_ref, first_ref, slot_ref, next_ref,
                xs_ref, w1_hbm, w3_hbm, w2_hbm, ys_ref, w13_scr, w2_scr, st1, st3, st2, sem_w):
    del brow_ref
    i = pl.program_id(0)

    def weight_copies(e, s):
        return (pltpu.make_async_copy(w1_hbm.at[e], st1.at[s], sem_w.at[s]),
                pltpu.make_async_copy(w3_hbm.at[e], st3.at[s], sem_w.at[s]),
                pltpu.make_async_copy(w2_hbm.at[e], st2.at[s], sem_w.at[s]))

    @pl.when(i < nused_ref[0])
    def _():
        @pl.when(first_ref[i] == 1)
        def _():
            e = bexp_ref[i]
            s = slot_ref[i]

            @pl.when(i == 0)
            def _():
                for cp in weight_copies(e, s):
                    cp.start()

            for cp in weight_copies(e, s):
                cp.wait()
            @pl.when(next_ref[i] >= 0)
            def _():
                for cp in weight_copies(next_ref[i], 1 - s):
                    cp.start()

            w13_scr[:, 0:D_EXPERT] = st1[s].astype(_BF)
            w13_scr[:, D_EXPERT:2 * D_EXPERT] = st3[s].astype(_BF)
            w2_scr[...] = st2[s].astype(_BF)

        xb = _unpack_rows(xs_ref[...]).astype(_BF)
        ab = _dot(xb, w13_scr[...])
        hid = jax.nn.silu(ab[:, 0:D_EXPERT]) * ab[:, D_EXPERT:2 * D_EXPERT]
        ys_ref[...] = _pack_rows(_dot(hid.astype(_BF), w2_scr[...]))


def _ffn(xs, block_exp, block_row, n_used, run_first, run_slot, run_next, w1, w3, w2, n_blocks):
    blk = lambda i, *_: (_[1][i], 0, 0)
    any_spec = pl.BlockSpec(memory_space=pl.ANY)
    grid_spec = pltpu.PrefetchScalarGridSpec(
        num_scalar_prefetch=6,
        grid=(n_blocks,),
        in_specs=[pl.BlockSpec((MOE_BLOCK,) + ROW_TILE, blk), any_spec, any_spec, any_spec],
        out_specs=pl.BlockSpec((MOE_BLOCK,) + ROW_TILE, blk),
        scratch_shapes=[pltpu.VMEM((D_MODEL, 2 * D_EXPERT), _BF), pltpu.VMEM((D_EXPERT, D_MODEL), _BF),
                        pltpu.VMEM((2, D_MODEL, D_EXPERT), _F32), pltpu.VMEM((2, D_MODEL, D_EXPERT), _F32),
                        pltpu.VMEM((2, D_EXPERT, D_MODEL), _F32), pltpu.SemaphoreType.DMA((2,))],
    )
    return pl.pallas_call(
        _ffn_kernel,
        grid_spec=grid_spec,
        out_shape=jax.ShapeDtypeStruct(xs.shape, ROW_DTYPE),
        compiler_params=pltpu.CompilerParams(dimension_semantics=("arbitrary",), vmem_limit_bytes=VMEM_LIMIT),
        name="moe_ffn",
    )(block_exp, block_row, n_used, run_first, run_slot, run_next, xs, w1, w3, w2)


def _combine_kernel(slots_hbm, x1_ref, rw_ref, g_ref, ys_hbm, y_ref, idx_smem, buf, sem_idx, sem_rows):
    t = pl.program_id(0)
    TC = x1_ref.shape[0]
    idx_cp = pltpu.make_async_copy(slots_hbm.at[pl.ds(t * IDX_CHUNK, IDX_CHUNK)], idx_smem, sem_idx)
    idx_cp.start()
    idx_cp.wait()

    def issue(r, carry):
        for kk in range(2):
            pltpu.make_async_copy(ys_hbm.at[idx_smem[2 * r + kk]], buf.at[kk, r], sem_rows).start(priority=kk)
        return carry

    lax.fori_loop(0, TC, issue, 0, unroll=ISSUE_UNROLL)
    for kk in range(2):
        pltpu.make_async_copy(ys_hbm.at[pl.ds(0, TC)], buf.at[kk], sem_rows).wait()

    w = rw_ref[...]
    y0 = _unpack_rows(buf[0])
    y1 = _unpack_rows(buf[1])
    x2 = x1_ref[...] + (w[:, 0:1] * y0 + w[:, 1:2] * y1)
    y_ref[...] = _rms(x2, g_ref[...])


SC_GATHER_CHUNKS = 4
SC_GATHER_BUFFERS = 3
SC_GATHER_MAX_WINDOW = 80


def _sc_gather(ys, rows):
    Tg = rows.shape[1]
    n_workers = SC_CORES * SC_SUBCORES
    per_w = Tg // n_workers
    assert per_w * n_workers == Tg and per_w % SUBLANES == 0
    W = max(w for w in range(SUBLANES, SC_GATHER_MAX_WINDOW + 1, SUBLANES) if per_w % w == 0)
    n_win = per_w // W
    NB = SC_GATHER_BUFFERS
    idx = rows.reshape(2, n_workers, n_win, W)
    jobs = [(kk, i) for i in range(n_win) for kk in range(2)]
    mesh = plsc.VectorSubcoreMesh(core_axis_name="c", subcore_axis_name="s")

    def body(ys_hbm, idx_hbm, out_hbm, i_v, rows_v, sem_g, sem_s):
        wid = lax.axis_index("s") * SC_CORES + lax.axis_index("c")
        base = wid * per_w
        for kk in range(2):
            pltpu.sync_copy(idx_hbm.at[kk, wid], i_v.at[kk])

        def gather(j):
            kk, i = jobs[j]
            return pltpu.make_async_copy(ys_hbm.at[i_v.at[kk, i]], rows_v.at[j % NB], sem_g.at[j % NB])

        def store(j):
            kk, i = jobs[j]
            return pltpu.make_async_copy(rows_v.at[j % NB], out_hbm.at[kk, pl.ds(base + i * W, W)], sem_s.at[j % NB])

        gather(0).start()
        for j in range(len(jobs)):
            gather(j).wait()
            store(j).start()
            if j + 1 < len(jobs):
                if j + 1 >= NB:
                    store(j + 1 - NB).wait()
                gather(j + 1).start()
        for j in range(max(0, len(jobs) - NB), len(jobs)):
            store(j).wait()

    return pl.kernel(
        body, mesh=mesh,
        out_type=jax.ShapeDtypeStruct((2, Tg) + ROW_TILE, ROW_DTYPE),
        scratch_types=[pltpu.VMEM((2, n_win, W), jnp.int32), pltpu.VMEM((NB, W) + ROW_TILE, ROW_DTYPE),
                       pltpu.SemaphoreType.DMA((NB,)), pltpu.SemaphoreType.DMA((NB,))],
        name="moe_sc_gather",
    )(ys, idx)


def _combine_dense_kernel(x1_ref, rw_ref, g_ref, rows_ref, *rest):
    y_ref = rest[-1]
    w = rw_ref[...]
    y0 = _unpack_rows(rows_ref[0])
    y1 = _unpack_rows(rows_ref[1])
    x2 = x1_ref[...] + (w[:, 0:1] * y0 + w[:, 1:2] * y1)
    y_ref[...] = _rms(x2, g_ref[...])


def _combine_dense(x1, rw, rows, y, g_final, tile, first_tile):
    Tg = rows.shape[1]
    assert Tg % tile == 0
    tok = lambda t: (first_tile + t, 0)
    in_specs = [pl.BlockSpec((tile, D_MODEL), tok), pl.BlockSpec((tile, LANES), tok), _full((1, D_MODEL)),
                pl.BlockSpec((2, tile) + ROW_TILE, lambda t: (0, t, 0, 0))]
    args = [x1, rw, g_final.reshape(1, D_MODEL), rows]
    aliases = {}
    if y is not None:
        in_specs.append(pl.BlockSpec(memory_space=pl.ANY))
        args.append(y)
        aliases = {4: 0}
    return pl.pallas_call(
        _combine_dense_kernel,
        grid=(Tg // tile,),
        in_specs=in_specs,
        out_specs=pl.BlockSpec((tile, D_MODEL), tok),
        out_shape=jax.ShapeDtypeStruct(x1.shape, _F32),
        input_output_aliases=aliases,
        compiler_params=pltpu.CompilerParams(dimension_semantics=("arbitrary",), vmem_limit_bytes=VMEM_LIMIT),
        name="moe_combine_dense",
    )(*args)


def _combine(x1, rw, slots, ys, g_final, tile, n_tiles):
    T = x1.shape[0]
    assert T % tile == 0 and 2 * tile <= IDX_CHUNK and slots.shape[0] == n_tiles * IDX_CHUNK
    any_spec = pl.BlockSpec(memory_space=pl.ANY)
    return pl.pallas_call(
        _combine_kernel,
        grid=(n_tiles,),
        in_specs=[any_spec, pl.BlockSpec((tile, D_MODEL), lambda t: (t, 0)),
                  pl.BlockSpec((tile, LANES), lambda t: (t, 0)), _full((1, D_MODEL)), any_spec],
        out_specs=pl.BlockSpec((tile, D_MODEL), lambda t: (t, 0)),
        out_shape=jax.ShapeDtypeStruct((T, D_MODEL), _F32),
        scratch_shapes=[pltpu.SMEM((IDX_CHUNK,), jnp.int32), pltpu.VMEM((2, tile) + ROW_TILE, ROW_DTYPE),
                        pltpu.SemaphoreType.DMA, pltpu.SemaphoreType.DMA],
        compiler_params=pltpu.CompilerParams(dimension_semantics=("arbitrary",), vmem_limit_bytes=VMEM_LIMIT),
        name="moe_combine",
    )(slots, x1, rw, g_final.reshape(1, D_MODEL), ys)


def _tile_for(n_tokens):
    tile = IDX_CHUNK // 2
    return tile if n_tokens % tile == 0 else n_tokens


def _chunked_slots(slot, tile):
    n_tiles = slot.shape[0] // tile
    s = slot.reshape(n_tiles, 2 * tile)
    return jnp.pad(s, ((0, 0), (0, IDX_CHUNK - 2 * tile))).reshape(-1)


def kernel(x_prompt, x_sample, state_pool, state_ret, g_mix, w_in, w_pool, pool_scale, ret_gn, w_out, g_ffn, w_grp, w_exp, w1, w3, w2, g_final):
    Bp, Lp, _ = x_prompt.shape
    Bs = x_sample.shape[0]
    Tp = Bp * Lp
    wts = _prep_weights(g_mix[0], w_in[0], w_pool[0], pool_scale[0], ret_gn[0], w_out[0], g_ffn[0], w_grp[0], w_exp[0])

    T_all = Tp + Bs
    cap = (-(-T_all // MOE_BLOCK) + 1) * MOE_BLOCK
    cap_blocks = cap // MOE_BLOCK

    x1p, dst_rows_p, rwp, npool_p, nret_p, counts_p, h2p = _mixer_prompt(x_prompt, wts, cap)
    xs = _sc_scatter(h2p, dst_rows_p, N_EXPERTS * cap)
    x1s, dst_s, rws, npool_s, nret_s, counts, h2s = _mixer_sample(
        x_sample.reshape(Bs, D_MODEL), jnp.swapaxes(state_pool[0], 0, 1), state_ret[0], wts, PAST_LEN, counts_p, cap)

    counts = counts[0, :N_EXPERTS].astype(jnp.int32)
    tile_p, tile_s = _tile_for(Tp), _tile_for(Bs)
    slots_s = _chunked_slots(dst_s[:, :2], tile_s)
    zstart = jnp.arange(N_EXPERTS, dtype=jnp.int32) * cap + counts
    zchunks = ((-counts) % MOE_BLOCK + ZERO_CHUNK - 1) // ZERO_CHUNK
    xs = _finish_dispatch(xs, h2s, slots_s, zstart, zchunks.astype(jnp.int32))
    nblk = (counts + MOE_BLOCK - 1) // MOE_BLOCK
    blk_end = jnp.cumsum(nblk)
    n_used = blk_end[-1:].astype(jnp.int32)
    n_blocks = -(-2 * T_all // MOE_BLOCK) + N_EXPERTS
    step = jnp.minimum(jnp.arange(n_blocks, dtype=jnp.int32), n_used - 1)
    block_exp = jnp.minimum(jnp.sum((blk_end[None, :] <= step[:, None]).astype(jnp.int32), axis=1), N_EXPERTS - 1)
    blk_start = jnp.sum(jnp.where(block_exp[:, None] == jnp.arange(N_EXPERTS, dtype=jnp.int32), (blk_end - nblk)[None, :], 0),
                        axis=1)
    block_row = (block_exp * cap_blocks + (step - blk_start)).astype(jnp.int32)
    block_exp = block_exp.astype(jnp.int32)
    experts = jnp.arange(N_EXPERTS, dtype=jnp.int32)
    nonempty = nblk > 0
    run_first = (step == blk_start).astype(jnp.int32)
    run_slot = (jnp.sum((nonempty[None, :] & (experts[None, :] < block_exp[:, None])).astype(jnp.int32), axis=1) % 2)
    later = nonempty[None, :] & (experts[None, :] > block_exp[:, None])
    run_next = jnp.min(jnp.where(later, experts[None, :], N_EXPERTS), axis=1)
    run_next = jnp.where(run_next == N_EXPERTS, -1, run_next).astype(jnp.int32)

    ys = _ffn(xs, block_exp, block_row, n_used, run_first, run_slot.astype(jnp.int32), run_next,
              w1[0], w3[0], w2[0], n_blocks)
    n_tiles_p = Tp // tile_p
    chunk_tiles = [n_tiles_p // SC_GATHER_CHUNKS + (k < n_tiles_p % SC_GATHER_CHUNKS)
                   for k in range(SC_GATHER_CHUNKS)]
    gathered, first = [], 0
    for nt in chunk_tiles:
        if nt:
            gathered.append((first, _sc_gather(ys, dst_rows_p[:, first * tile_p:(first + nt) * tile_p])))
            first += nt
    y_s = _combine(x1s, rws, slots_s, ys, g_final, tile_s, 1)
    y_p = None
    for first, rows in gathered:
        y_p = _combine_dense(x1p, rwp, rows, y_p, g_final, tile_p, first)

    return (y_p.reshape(Bp, Lp, D_MODEL), y_s.reshape(Bs, 1, D_MODEL),
            npool_p[None], nret_p[None], jnp.swapaxes(npool_s, 0, 1)[None], nret_s[None])
```

```python
import functools

import jax
import jax.numpy as jnp
import numpy as np
from jax import lax
from jax.experimental import pallas as pl
from jax.experimental.pallas import tpu as pltpu
from jax.experimental.pallas import tpu_sc as plsc

D_MODEL = 1024
EPS = 1e-6
POOL_GROUPS = 4
POOL_IN = D_MODEL // 2
POOL_GC = POOL_IN // POOL_GROUPS
POOL_OUT_GC = D_MODEL // POOL_GROUPS
POOL_WINDOWS = (2, 4, 8, 16)
POOL_HIST = max(POOL_WINDOWS) - 1
RET_HEADS = 4
RET_DK = D_MODEL // 8
RET_DV = D_MODEL // RET_HEADS
ROPE_BASE = 10000.0
PAST_LEN = 16384
N_GROUPS = 4
EXPERTS_PER_GROUP = 8
N_EXPERTS = N_GROUPS * EXPERTS_PER_GROUP
D_EXPERT = D_MODEL // 4
QK_W = RET_HEADS * RET_DK
V_W = RET_HEADS * RET_DV
OFF_U = 0
OFF_Q = POOL_IN
OFF_K = OFF_Q + QK_W
OFF_V = OFF_K + QK_W
OFF_GA = OFF_V + V_W
OFF_GB = OFF_GA + D_MODEL
IN_WIDTH = OFF_GB + D_MODEL

LANES = 128
SUBLANES = 8
ROW_WORDS = D_MODEL // 2
ROW_TILE = (ROW_WORDS // LANES, LANES)
ROW_DTYPE = jnp.uint32
HALO = 16
RET_CHUNK = 256
MIXER_SEQS = 2
MOE_BLOCK = 512
ROUTER_W = LANES
VMEM_LIMIT = 56 * 1024 * 1024

_BF = jnp.bfloat16
_F32 = jnp.float32


def _rms(x, g):
    inv = lax.rsqrt(jnp.mean(x * x, axis=-1, keepdims=True) + EPS)
    return x * inv * g


def _dot(a, b):
    return jnp.dot(a, b, preferred_element_type=_F32)


def _round_bf16(x):
    return x.astype(_BF).astype(_F32)


def _pack_rows(x):
    lo = lax.bitcast_convert_type(x[:, :ROW_WORDS].astype(_BF).astype(_F32), jnp.uint32)
    hi = lax.bitcast_convert_type(x[:, ROW_WORDS:].astype(_BF).astype(_F32), jnp.uint32)
    return ((lo >> 16) | hi).reshape((x.shape[0],) + ROW_TILE)


def _unpack_rows(w):
    w = w.reshape(w.shape[0], ROW_WORDS)
    lo = lax.bitcast_convert_type(w << 16, _F32)
    hi = lax.bitcast_convert_type(w & jnp.uint32(0xFFFF0000), _F32)
    return jnp.concatenate([lo, hi], axis=1)


def _rotary(x, cos2, sin2):
    return x * cos2 + pltpu.roll(x, RET_DK // 2, 1) * sin2


def _route(logits):
    lane = lax.broadcasted_iota(jnp.int32, logits.shape, 1).astype(_F32)
    neg = jnp.float32(-jnp.inf)
    big = jnp.float32(1 << 20)
    lg = jnp.where(lane < N_GROUPS, logits, neg)
    mg = jnp.max(lg, axis=-1, keepdims=True)
    g_idx = jnp.min(jnp.where(lg == mg, lane, big), axis=-1, keepdims=True)
    p_g = 1.0 / jnp.sum(jnp.exp(lg - mg), axis=-1, keepdims=True)
    lo = N_GROUPS + g_idx * EXPERTS_PER_GROUP
    in_grp = (lane >= lo) & (lane < lo + EXPERTS_PER_GROUP)
    le = jnp.where(in_grp, logits, neg)
    m1 = jnp.max(le, axis=-1, keepdims=True)
    i1 = jnp.min(jnp.where(le == m1, lane, big), axis=-1, keepdims=True)
    le2 = jnp.where(lane == i1, neg, le)
    m2 = jnp.max(le2, axis=-1, keepdims=True)
    i2 = jnp.min(jnp.where(le2 == m2, lane, big), axis=-1, keepdims=True)
    t = jnp.exp(m2 - m1)
    den = 1.0 + t
    e0 = (i1 - N_GROUPS).astype(jnp.int32)
    e1 = (i2 - N_GROUPS).astype(jnp.int32)
    return e0, e1, p_g * (1.0 / den), p_g * (t / den)


def _post_mix(x, mixed_ref, w_out_ref, g_ffn_ref, w_rt_ref, x1_ref, rw_ref):
    x1 = x + _dot(mixed_ref[...], w_out_ref[...])
    x1_ref[...] = x1.reshape(x1_ref.shape)
    h2 = _rms(x1, g_ffn_ref[...])
    e0, e1, w0, w1 = _route(_dot(h2.astype(_BF), w_rt_ref[...]))
    lane = lax.broadcasted_iota(jnp.int32, (x.shape[0], LANES), 1)
    rw_ref[...] = jnp.where(lane == 0, w0, jnp.where(lane == 1, w1, 0.0)).reshape(rw_ref.shape)
    return h2, e0, e1


def _route_block(x1, g_ffn_ref, w_rt_ref, carry_scr, cap, valid, h2_ref, rw_ref, dst_t_ref):
    for _ in _route_stages(x1, g_ffn_ref, w_rt_ref, carry_scr, cap, valid, h2_ref, rw_ref, dst_t_ref):
        pass


def _route_stages(x1, g_ffn_ref, w_rt_ref, carry_scr, cap, valid, h2_ref, rw_ref, dst_t_ref):
    h2 = _rms(x1, g_ffn_ref[...])
    logits = _dot(h2.astype(_BF), w_rt_ref[...])
    h2_ref[...] = _pack_rows(h2).reshape(h2_ref.shape)
    yield
    e0, e1, w0, w1 = _route(logits)
    lane = lax.broadcasted_iota(jnp.int32, (x1.shape[0], LANES), 1)
    rw_ref[...] = jnp.where(lane == 0, w0, jnp.where(lane == 1, w1, 0.0)).reshape(rw_ref.shape)
    yield
    dst = _rank_rows(e0, e1, cap, carry_scr, valid)
    dst_t = dst.T[0:SUBLANES, :].astype(jnp.int32)
    n_parts, _, width = dst_t_ref.shape
    for part in range(n_parts):
        dst_t_ref[part] = dst_t[:, part * width:(part + 1) * width]
    yield


def _rank_rows(e0, e1, cap, carry_scr, valid=None):
    R = e0.shape[0]
    lane = lax.broadcasted_iota(jnp.int32, (R, LANES), 1)
    m0 = lane == e0
    m1 = lane == e1
    onehot = jnp.where(m0 | m1, 1.0, 0.0)
    r_i = lax.broadcasted_iota(jnp.int32, (R, R), 0)
    c_i = lax.broadcasted_iota(jnp.int32, (R, R), 1)
    tri = jnp.where(c_i < r_i, 1.0, 0.0).astype(_BF)
    before = _dot(tri, onehot.astype(_BF)) + carry_scr[...]
    d0 = e0.astype(_F32) * cap + jnp.sum(jnp.where(m0, before, 0.0), axis=-1, keepdims=True)
    d1 = e1.astype(_F32) * cap + jnp.sum(jnp.where(m1, before, 0.0), axis=-1, keepdims=True)
    added = jnp.sum(onehot, axis=0, keepdims=True)
    carry_scr[...] += added if valid is None else added * valid
    return jnp.where(lane == 0, d0, jnp.where(lane == 1, d1, 0.0))


def _group_norm(o, gain):
    mu = jnp.mean(o, axis=-1, keepdims=True)
    d = o - mu
    var = jnp.mean(d * d, axis=-1, keepdims=True)
    return d * lax.rsqrt(var + EPS) * gain


def _mixer_prompt_kernel(x_ref, g_mix_ref, w_in_ref, w_pool_ref, pscale_ref, gn_ref, w_out_ref, g_ffn_ref,
                         w_rt_ref, cos_ref, sin_ref, dmask_ref, xi_ref, zeta_ref, gc_ref,
                         x1_ref, dst_t_ref, rw_ref, npool_ref, nret_ref, cnt_ref, h2_ref,
                         u_scr, s_scr, mixed_scr, carry_scr, z_scr, x1_prev_scr, *, cap):
    c = pl.program_id(1)
    n_c = pl.num_programs(1)
    step = pl.program_id(0) * n_c + c
    NSEQ, C, _ = x_ref.shape
    R = NSEQ * C

    @pl.when(step == 0)
    def _():
        carry_scr[...] = jnp.zeros(carry_scr.shape, _F32)
        x1_prev_scr[...] = jnp.zeros(x1_prev_scr.shape, _F32)

    @pl.when(c == 0)
    def _():
        u_scr[:, 0:HALO, :] = jnp.zeros((NSEQ, HALO, POOL_IN), _F32)
        s_scr[...] = jnp.zeros(s_scr.shape, _F32)

    x = x_ref[...].reshape(R, D_MODEL)
    h = _rms(x, g_mix_ref[...]).astype(_BF)

    stages = _route_stages(x1_prev_scr[...], g_ffn_ref, w_rt_ref, carry_scr, cap, (step > 0).astype(_F32),
                           h2_ref, rw_ref, dst_t_ref)
    for lo, hi in ((0, OFF_V), (OFF_V, OFF_GA), (OFF_GA, OFF_GB), (OFF_GB, IN_WIDTH)):
        z_scr[:, lo:hi] = _dot(h, w_in_ref[:, lo:hi])
        next(stages, None)
    cnt_ref[...] = carry_scr[...]

    u_scr[:, HALO:HALO + C, :] = z_scr[:, OFF_U:OFF_U + POOL_IN].reshape(NSEQ, C, POOL_IN)
    q = z_scr[:, OFF_Q:OFF_Q + QK_W]
    k = z_scr[:, OFF_K:OFF_K + QK_W]
    cos2 = jnp.concatenate([cos_ref[...]] * NSEQ, axis=0)
    sin2 = jnp.concatenate([sin_ref[...]] * NSEQ, axis=0)
    pos1 = (c * C + 1 + lax.broadcasted_iota(jnp.int32, (C, POOL_GC), 0)).astype(_F32)

    for j in range(RET_HEADS):
        win = POOL_WINDOWS[j]
        cs = slice(j * POOL_GC, (j + 1) * POOL_GC)
        n_rows = jnp.minimum(pos1, jnp.float32(win))
        pooled = []
        for sq in range(NSEQ):
            u_j = u_scr[sq, HALO:HALO + C, cs]
            s = u_j
            for d in range(1, win):
                s = s + u_scr[sq, HALO - d:HALO - d + C, cs]
            pooled.append(s / n_rows - u_j)
        osl = slice(j * POOL_OUT_GC, (j + 1) * POOL_OUT_GC)
        pool_out = _dot(jnp.concatenate(pooled, axis=0).astype(_BF), w_pool_ref[j]) * pscale_ref[:, osl]

        qs = slice(j * RET_DK, (j + 1) * RET_DK)
        qb = _rotary(q[:, qs], cos2, sin2).astype(_BF)
        kf = _rotary(k[:, qs], cos2, sin2) * (RET_DK ** -0.5)
        kb = kf.astype(_BF)
        vb = z_scr[:, OFF_V + j * RET_DV:OFF_V + (j + 1) * RET_DV].astype(_BF)
        ret = []
        for sq in range(NSEQ):
            rows = slice(sq * C, (sq + 1) * C)
            scores = lax.dot_general(qb[rows], kb[rows], (((1,), (1,)), ((), ())),
                                     preferred_element_type=_F32) * dmask_ref[j]
            s_old = s_scr[sq, j]
            o = _dot(scores.astype(_BF), vb[rows]) + _dot(qb[rows], s_old.astype(_BF)) * xi_ref[j]
            kz = (kf[rows] * zeta_ref[j]).astype(_BF)
            s_scr[sq, j] = gc_ref[j] * s_old + lax.dot_general(kz, vb[rows], (((0,), (0,)), ((), ())),
                                                               preferred_element_type=_F32)
            ret.append(_group_norm(o, gn_ref[:, osl]))
        ret_out = jnp.concatenate(ret, axis=0)

        ga = z_scr[:, OFF_GA + j * RET_DV:OFF_GA + (j + 1) * RET_DV]
        gb = z_scr[:, OFF_GB + j * RET_DV:OFF_GB + (j + 1) * RET_DV]
        mixed_scr[:, osl] = (jax.nn.sigmoid(ga) * pool_out + jax.nn.sigmoid(gb) * ret_out).astype(_BF)

    x1 = x + _dot(mixed_scr[...], w_out_ref[...])
    x1_ref[...] = x1.reshape(x1_ref.shape)
    x1_prev_scr[...] = x1

    u_scr[:, 0:HALO, :] = u_scr[:, C:C + HALO, :]

    @pl.when(c == n_c - 1)
    def _():
        npool_ref[...] = u_scr[:, HALO + C - POOL_HIST:HALO + C, :]
        nret_ref[...] = s_scr[...]


def _decay_tables(C):
    f32 = np.float32
    log_g = np.log(f32(1.0) - np.exp2(f32(-5.0) - np.arange(RET_HEADS, dtype=f32)))
    i = np.arange(C, dtype=f32)
    diff = i[:, None] - i[None, :]
    dmask = np.where(diff >= 0, np.exp(np.maximum(diff, f32(0.0))[None] * log_g[:, None, None]), f32(0.0))
    xi = np.exp((i[None, :] + f32(1.0)) * log_g[:, None])
    zeta = np.exp((f32(C) - f32(1.0) - i)[None, :] * log_g[:, None])
    g_chunk = np.exp(f32(C) * log_g)
    return dmask.astype(f32), xi.astype(f32), zeta.astype(f32), g_chunk.astype(f32)


def _rope_tables(pos):
    half = RET_DK // 2
    freqs = ROPE_BASE ** (-jnp.arange(half, dtype=_F32) / half)
    ang = pos[:, None] * freqs[None, :]
    cos, sin = jnp.cos(ang), jnp.sin(ang)
    return jnp.concatenate([cos, cos], axis=-1), jnp.concatenate([-sin, sin], axis=-1)


def _const(a, shape=None):
    if shape is not None:
        a = np.ascontiguousarray(np.broadcast_to(a, shape))
    return jnp.asarray(a)


def _full(shape):
    n = len(shape)
    return pl.BlockSpec(shape, lambda *_: (0,) * n)


def _route_tail_kernel(x1_ref, g_ffn_ref, w_rt_ref, cnt_in_ref, dst_t_in, rw_in, h2_in,
                       dst_t_ref, rw_ref, cnt_ref, h2_ref, carry_scr, *, cap):
    del dst_t_in, rw_in, h2_in
    carry_scr[...] = cnt_in_ref[...]
    x1 = x1_ref[...]
    _route_block(x1.reshape(x1.shape[0] * x1.shape[1], D_MODEL), g_ffn_ref, w_rt_ref, carry_scr, cap, None,
                 h2_ref, rw_ref, dst_t_ref)
    cnt_ref[...] = carry_scr[...]


def _route_tail(x1, dst_t, rw, counts, h2, wts, cap, C, last):
    NSEQ = x1.shape[0]
    any_spec = pl.BlockSpec(memory_space=pl.ANY)
    return pl.pallas_call(
        functools.partial(_route_tail_kernel, cap=cap),
        grid=(1,),
        in_specs=[pl.BlockSpec((NSEQ, C, D_MODEL), lambda i: (0, last, 0)), _full((1, D_MODEL)),
                  _full((D_MODEL, ROUTER_W)), _full((1, LANES)), any_spec, any_spec, any_spec],
        out_specs=(pl.BlockSpec((NSEQ, SUBLANES, C), lambda i: (0, 0, last)),
                   pl.BlockSpec((NSEQ, C, LANES), lambda i: (0, last, 0)),
                   _full((1, LANES)),
                   pl.BlockSpec((NSEQ, C) + ROW_TILE, lambda i: (0, last, 0, 0))),
        out_shape=(jax.ShapeDtypeStruct(dst_t.shape, dst_t.dtype), jax.ShapeDtypeStruct(rw.shape, rw.dtype),
                   jax.ShapeDtypeStruct((1, LANES), _F32), jax.ShapeDtypeStruct(h2.shape, h2.dtype)),
        input_output_aliases={4: 0, 5: 1, 6: 3},
        scratch_shapes=[pltpu.VMEM((1, LANES), _F32)],
        compiler_params=pltpu.CompilerParams(dimension_semantics=("arbitrary",), vmem_limit_bytes=VMEM_LIMIT),
        name="mixer_route_tail",
    )(x1, wts["g_ffn"], wts["w_rt"], counts, dst_t, rw, h2)


def _mixer_prompt(x, wts, cap):
    B, L, _ = x.shape
    C = RET_CHUNK if L % RET_CHUNK == 0 else L
    n_c = L // C
    T = B * L
    dmask, xi, zeta, g_chunk = _decay_tables(C)
    dmask = _const(dmask)
    xi_b = _const(xi[:, :, None], (RET_HEADS, C, RET_DV))
    zeta_b = _const(zeta[:, :, None], (RET_HEADS, C, RET_DK))
    gc_b = _const(g_chunk[:, None, None], (RET_HEADS, 1, RET_DV))
    cos2, sin2 = _rope_tables(jnp.arange(L).astype(_F32))

    NSEQ = MIXER_SEQS if B % MIXER_SEQS == 0 else 1
    B2 = B // NSEQ
    T2 = T // NSEQ
    x = x.reshape(NSEQ, B2, L, D_MODEL)
    tok = lambda b, c: (0, b * n_c + c, 0)
    in_specs = [
        pl.BlockSpec((NSEQ, None, C, D_MODEL), lambda b, c: (0, b, c, 0)),
        _full((1, D_MODEL)), _full((D_MODEL, IN_WIDTH)), _full((POOL_GROUPS, POOL_GC, POOL_OUT_GC)),
        _full((1, D_MODEL)), _full((1, D_MODEL)), _full((D_MODEL, D_MODEL)), _full((1, D_MODEL)),
        _full((D_MODEL, ROUTER_W)),
        pl.BlockSpec((C, RET_DK), lambda b, c: (c, 0)), pl.BlockSpec((C, RET_DK), lambda b, c: (c, 0)),
        _full((RET_HEADS, C, C)), _full((RET_HEADS, C, RET_DV)), _full((RET_HEADS, C, RET_DK)),
        _full((RET_HEADS, 1, RET_DV)),
    ]
    R = NSEQ * C
    out_shape = (
        jax.ShapeDtypeStruct((NSEQ, T2, D_MODEL), _F32),
        jax.ShapeDtypeStruct((NSEQ, SUBLANES, T2), jnp.int32),
        jax.ShapeDtypeStruct((NSEQ, T2, LANES), _F32),
        jax.ShapeDtypeStruct((NSEQ, B2, POOL_HIST, POOL_IN), _F32),
        jax.ShapeDtypeStruct((NSEQ, B2, RET_HEADS, RET_DK, RET_DV), _F32),
        jax.ShapeDtypeStruct((1, LANES), _F32),
        jax.ShapeDtypeStruct((NSEQ, T2) + ROW_TILE, ROW_DTYPE),
    )
    prev = lambda b, c: jnp.maximum(b * n_c + c - 1, 0)
    out_specs = (
        pl.BlockSpec((NSEQ, C, D_MODEL), tok),
        pl.BlockSpec((NSEQ, SUBLANES, C), lambda b, c: (0, 0, prev(b, c))),
        pl.BlockSpec((NSEQ, C, LANES), lambda b, c: (0, prev(b, c), 0)),
        pl.BlockSpec((NSEQ, None, POOL_HIST, POOL_IN), lambda b, c: (0, b, 0, 0)),
        pl.BlockSpec((NSEQ, None, RET_HEADS, RET_DK, RET_DV), lambda b, c: (0, b, 0, 0, 0)),
        _full((1, LANES)),
        pl.BlockSpec((NSEQ, C) + ROW_TILE, lambda b, c: (0, prev(b, c), 0, 0)),
    )
    x1, dst_t, rw, npool, nret, counts, h2 = pl.pallas_call(
        functools.partial(_mixer_prompt_kernel, cap=cap),
        grid=(B2, n_c),
        in_specs=in_specs,
        out_specs=out_specs,
        out_shape=out_shape,
        scratch_shapes=[pltpu.VMEM((NSEQ, HALO + C, POOL_IN), _F32),
                        pltpu.VMEM((NSEQ, RET_HEADS, RET_DK, RET_DV), _F32),
                        pltpu.VMEM((R, D_MODEL), _BF),
                        pltpu.VMEM((1, LANES), _F32),
                        pltpu.VMEM((R, IN_WIDTH), _F32),
                        pltpu.VMEM((R, D_MODEL), _F32)],
        compiler_params=pltpu.CompilerParams(dimension_semantics=("arbitrary", "arbitrary"),
                                             vmem_limit_bytes=VMEM_LIMIT),
        name="mixer_prompt",
    )(x, wts["g_mix"], wts["w_in"], wts["w_pool"], wts["pool_scale"], wts["ret_gn"], wts["w_out"],
      wts["g_ffn"], wts["w_rt"], cos2, sin2, dmask, xi_b, zeta_b, gc_b)
    dst_t, rw, counts, h2 = _route_tail(x1, dst_t, rw, counts, h2, wts, cap, C, B2 * n_c - 1)
    dst_rows = jnp.moveaxis(dst_t[:, 0:2, :], 1, 0).reshape(2, T)
    return (x1.reshape(T, D_MODEL), dst_rows, rw.reshape(T, LANES),
            npool.reshape(B, POOL_HIST, POOL_IN), nret.reshape(B, RET_HEADS, RET_DK, RET_DV), counts,
            h2.reshape((T,) + ROW_TILE))


def _prep_weights(g_mix, w_in, w_pool, pool_scale, ret_gn, w_out, g_ffn, w_grp, w_exp):
    w_rt = jnp.concatenate([w_grp, w_exp.reshape(D_MODEL, N_EXPERTS)], axis=1)
    w_rt = jnp.pad(w_rt, ((0, 0), (0, ROUTER_W - w_rt.shape[1])))
    row = lambda v: v.reshape(1, D_MODEL)
    return dict(g_mix=row(g_mix), w_in=w_in.astype(_BF), w_pool=w_pool.astype(_BF), pool_scale=row(pool_scale),
                ret_gn=row(ret_gn), w_out=w_out.astype(_BF), g_ffn=row(g_ffn), w_rt=w_rt.astype(_BF))


SAMPLE_TB = 8


def _mixer_sample_kernel(x_ref, spool_ref, sret_ref, g_mix_ref, w_in_ref, w_pool_ref, pscale_ref, gn_ref,
                         w_out_ref, g_ffn_ref, w_rt_ref, cos_ref, sin_ref, dm_ref, xi_ref, zeta_ref, gc_ref,
                         carry_in_ref,
                         x1_ref, dst_ref, rw_ref, npool_ref, nret_ref, cnt_ref, h2_ref,
                         u_scr, q_scr, k_scr, qt_scr, kt_scr, v_scr, ga_scr, gb_scr, pooled_scr, o_scr, mixed_scr,
                         carry_scr, *, cap):
    t = pl.program_id(0)
    n_t = pl.num_programs(0)
    Bs = x_ref.shape[0]
    TB = sret_ref.shape[0]

    @pl.when(t == 0)
    def _():
        h = _rms(x_ref[...], g_mix_ref[...]).astype(_BF)
        u_scr[...] = _dot(h, w_in_ref[:, OFF_U:OFF_U + POOL_IN])
        q = _dot(h, w_in_ref[:, OFF_Q:OFF_Q + QK_W])
        k = _dot(h, w_in_ref[:, OFF_K:OFF_K + QK_W])
        for j in range(RET_HEADS):
            qs = slice(j * RET_DK, (j + 1) * RET_DK)
            qf = _rotary(q[:, qs], cos_ref[...], sin_ref[...])
            kf = _rotary(k[:, qs], cos_ref[...], sin_ref[...]) * (RET_DK ** -0.5)
            q_scr[:, qs] = qf
            k_scr[:, qs] = kf
            qt_scr[j] = _round_bf16(qf).T
            kt_scr[j] = kf.T
        v_scr[...] = _dot(h, w_in_ref[:, OFF_V:OFF_V + V_W])
        ga_scr[...] = _dot(h, w_in_ref[:, OFF_GA:OFF_GA + D_MODEL])
        gb_scr[...] = _dot(h, w_in_ref[:, OFF_GB:OFF_GB + D_MODEL])

    shift = (Bs - t * TB) % Bs
    qt = [pltpu.roll(qt_scr[j], shift, 1) for j in range(RET_HEADS)]
    kt = [pltpu.roll(kt_scr[j], shift, 1) for j in range(RET_HEADS)]

    blk = pl.ds(pl.multiple_of(t * TB, TB), TB)
    u_blk = u_scr[blk, :]
    q_blk = q_scr[blk, :]
    k_blk = k_scr[blk, :]
    v_blk = v_scr[blk, :]
    score = [jnp.sum(q_blk[:, j * RET_DK:(j + 1) * RET_DK] * k_blk[:, j * RET_DK:(j + 1) * RET_DK],
                     axis=-1, keepdims=True) * dm_ref[j] for j in range(RET_HEADS)]

    groups = []
    for g, win in enumerate(POOL_WINDOWS):
        cs = slice(g * POOL_GC, (g + 1) * POOL_GC)
        s = u_blk[:, cs]
        for r in range(POOL_HIST - (win - 1), POOL_HIST):
            s = s + spool_ref[r, :, cs]
        groups.append(s / jnp.float32(win) - u_blk[:, cs])
    pooled_scr[blk, :] = jnp.concatenate(groups, axis=1)
    npool_ref[0:POOL_HIST - 1] = spool_ref[1:POOL_HIST]
    npool_ref[POOL_HIST - 1] = u_blk

    o_rows = []
    for i in range(TB):
        heads = []
        for j in range(RET_HEADS):
            s_old = sret_ref[i, j]
            v_row = v_blk[i:i + 1, j * RET_DV:(j + 1) * RET_DV]
            qcol = qt[j][:, i:i + 1]
            kcol = kt[j][:, i:i + 1]
            qs_old = jnp.sum(qcol * _round_bf16(s_old), axis=0, keepdims=True)
            heads.append(score[j][i:i + 1, :] * v_row + qs_old * xi_ref[j])
            nret_ref[i, j] = gc_ref[j] * s_old + (kcol * zeta_ref[j]) * v_row
        o_rows.append(jnp.concatenate(heads, axis=1))
    o_scr[blk, :] = jnp.concatenate(o_rows, axis=0)

    @pl.when(t == n_t - 1)
    def _():
        for j in range(RET_HEADS):
            cs = slice(j * POOL_GC, (j + 1) * POOL_GC)
            osl = slice(j * POOL_OUT_GC, (j + 1) * POOL_OUT_GC)
            pool_out = _dot(pooled_scr[:, cs].astype(_BF), w_pool_ref[j]) * pscale_ref[:, osl]
            ret_out = _group_norm(o_scr[:, osl], gn_ref[:, osl])
            mixed_scr[:, osl] = (jax.nn.sigmoid(ga_scr[:, osl]) * pool_out
                                 + jax.nn.sigmoid(gb_scr[:, osl]) * ret_out).astype(_BF)
        h2, e0, e1 = _post_mix(x_ref[...], mixed_scr, w_out_ref, g_ffn_ref, w_rt_ref, x1_ref, rw_ref)
        h2_ref[...] = _pack_rows(h2).reshape(h2_ref.shape)
        carry_scr[...] = carry_in_ref[...]
        dst_ref[...] = _rank_rows(e0, e1, cap, carry_scr).astype(jnp.int32)
        cnt_ref[...] = carry_scr[...]


def _mixer_sample(x, state_pool, state_ret, wts, pos0, carry_in, cap):
    Bs = x.shape[0]
    assert Bs == LANES and Bs % SAMPLE_TB == 0
    TB = SAMPLE_TB
    dmask, xi, zeta, g_chunk = _decay_tables(1)
    dm_b = _const(dmask, (RET_HEADS, 1, 1))
    xi_b = _const(xi[:, :, None], (RET_HEADS, 1, RET_DV))
    zeta_b = _const(zeta[:, :, None], (RET_HEADS, 1, 1))
    gc_b = _const(g_chunk[:, None, None], (RET_HEADS, 1, RET_DV))
    cos2, sin2 = _rope_tables((pos0 + jnp.arange(1)).astype(_F32))

    in_specs = [
        _full((Bs, D_MODEL)),
        pl.BlockSpec((POOL_HIST, TB, POOL_IN), lambda t: (0, t, 0)),
        pl.BlockSpec((TB, RET_HEADS, RET_DK, RET_DV), lambda t: (t, 0, 0, 0)),
        _full((1, D_MODEL)), _full((D_MODEL, IN_WIDTH)), _full((POOL_GROUPS, POOL_GC, POOL_OUT_GC)),
        _full((1, D_MODEL)), _full((1, D_MODEL)), _full((D_MODEL, D_MODEL)), _full((1, D_MODEL)),
        _full((D_MODEL, ROUTER_W)),
        _full((1, RET_DK)), _full((1, RET_DK)),
        _full((RET_HEADS, 1, 1)), _full((RET_HEADS, 1, RET_DV)), _full((RET_HEADS, 1, 1)),
        _full((RET_HEADS, 1, RET_DV)),
        _full((1, LANES)),
    ]
    out_shape = (
        jax.ShapeDtypeStruct((Bs, D_MODEL), _F32),
        jax.ShapeDtypeStruct((Bs, LANES), jnp.int32),
        jax.ShapeDtypeStruct((Bs, LANES), _F32),
        jax.ShapeDtypeStruct((POOL_HIST, Bs, POOL_IN), _F32),
        jax.ShapeDtypeStruct((Bs, RET_HEADS, RET_DK, RET_DV), _F32),
        jax.ShapeDtypeStruct((1, LANES), _F32),
        jax.ShapeDtypeStruct((Bs,) + ROW_TILE, ROW_DTYPE),
    )
    out_specs = (
        _full((Bs, D_MODEL)), _full((Bs, LANES)), _full((Bs, LANES)),
        pl.BlockSpec((POOL_HIST, TB, POOL_IN), lambda t: (0, t, 0)),
        pl.BlockSpec((TB, RET_HEADS, RET_DK, RET_DV), lambda t: (t, 0, 0, 0)),
        _full((1, LANES)), _full((Bs,) + ROW_TILE),
    )
    f32 = lambda *s: pltpu.VMEM(s, _F32)
    return pl.pallas_call(
        functools.partial(_mixer_sample_kernel, cap=cap),
        grid=(Bs // TB,),
        in_specs=in_specs,
        out_specs=out_specs,
        out_shape=out_shape,
        scratch_shapes=[f32(Bs, POOL_IN), f32(Bs, QK_W), f32(Bs, QK_W), f32(RET_HEADS, RET_DK, Bs),
                        f32(RET_HEADS, RET_DK, Bs), f32(Bs, V_W), f32(Bs, D_MODEL), f32(Bs, D_MODEL),
                        f32(Bs, POOL_IN), f32(Bs, V_W), pltpu.VMEM((Bs, D_MODEL), _BF), f32(1, LANES)],
        compiler_params=pltpu.CompilerParams(dimension_semantics=("arbitrary",), vmem_limit_bytes=VMEM_LIMIT),
        name="mixer_sample",
    )(x, state_pool, state_ret, wts["g_mix"], wts["w_in"], wts["w_pool"], wts["pool_scale"], wts["ret_gn"],
      wts["w_out"], wts["g_ffn"], wts["w_rt"], cos2, sin2, dm_b, xi_b, zeta_b, gc_b, carry_in)


IDX_CHUNK = 1024
ISSUE_UNROLL = 8
SC_CORES = 2
SC_SUBCORES = 16
SC_WINDOW = 64


def _sc_scatter(h2, dst_rows, n_rows):
    T = h2.shape[0]
    n_workers = SC_CORES * SC_SUBCORES
    W = SC_WINDOW
    assert T % (n_workers * W) == 0
    per_w = T // n_workers
    n_win = per_w // W
    d0 = dst_rows[0].reshape(T // W, W)
    d1 = dst_rows[1].reshape(T // W, W)
    mesh = plsc.VectorSubcoreMesh(core_axis_name="c", subcore_axis_name="s")

    def body(h2_hbm, d0_hbm, d1_hbm, xs_hbm, i0_v, i1_v, rows_v, sem_load, sem_store):
        wid = lax.axis_index("s") * SC_CORES + lax.axis_index("c")
        base = wid * per_w
        pltpu.sync_copy(d0_hbm.at[pl.ds(wid * n_win, n_win)], i0_v)
        pltpu.sync_copy(d1_hbm.at[pl.ds(wid * n_win, n_win)], i1_v)
        pltpu.async_copy(h2_hbm.at[pl.ds(base, W)], rows_v.at[0], sem_load)
        for i in range(n_win):
            b = i % 2
            pltpu.make_async_copy(h2_hbm.at[pl.ds(base, W)], rows_v.at[b], sem_load).wait()
            if i + 1 < n_win:
                pltpu.async_copy(h2_hbm.at[pl.ds(base + (i + 1) * W, W)], rows_v.at[1 - b], sem_load)
            c0 = pltpu.async_copy(rows_v.at[b], xs_hbm.at[i0_v.at[i]], sem_store)
            c1 = pltpu.async_copy(rows_v.at[b], xs_hbm.at[i1_v.at[i]], sem_store)
            c0.wait()
            c1.wait()

    return pl.kernel(
        body, mesh=mesh,
        out_type=jax.ShapeDtypeStruct((n_rows,) + ROW_TILE, ROW_DTYPE),
        scratch_types=[pltpu.VMEM((n_win, W), jnp.int32), pltpu.VMEM((n_win, W), jnp.int32),
                       pltpu.VMEM((2, W) + ROW_TILE, ROW_DTYPE),
                       pltpu.SemaphoreType.DMA, pltpu.SemaphoreType.DMA],
        name="moe_sc_scatter",
    )(h2, d0, d1)


ZERO_CHUNK = 64


def _finish_dispatch_kernel(zstart_ref, zchunks_ref, slots_hbm, h2_ref, xs_in_hbm, xs_hbm, idx_smem, zbuf, sem_idx,
                            sem_rows, sem_zero):
    del xs_in_hbm
    TS = h2_ref.shape[0]
    idx_cp = pltpu.make_async_copy(slots_hbm.at[pl.ds(0, IDX_CHUNK)], idx_smem, sem_idx)
    idx_cp.start()
    zbuf[...] = jnp.zeros(zbuf.shape, zbuf.dtype)

    def zero_copy(e, j):
        return pltpu.make_async_copy(zbuf, xs_hbm.at[pl.ds(zstart_ref[e] + j * ZERO_CHUNK, ZERO_CHUNK)], sem_zero)

    for e in range(N_EXPERTS):
        lax.fori_loop(0, zchunks_ref[e], lambda j, c, e=e: (zero_copy(e, j).start(), c)[1], 0)
    idx_cp.wait()

    def issue(r, carry):
        for kk in range(2):
            pltpu.make_async_copy(h2_ref.at[r], xs_hbm.at[idx_smem[2 * r + kk]], sem_rows).start(priority=kk)
        return carry

    lax.fori_loop(0, TS, issue, 0, unroll=ISSUE_UNROLL)
    for e in range(N_EXPERTS):
        lax.fori_loop(0, zchunks_ref[e], lambda j, c, e=e: (zero_copy(e, j).wait(), c)[1], 0)
    for kk in range(2):
        pltpu.make_async_copy(h2_ref, xs_hbm.at[pl.ds(0, TS)], sem_rows).wait()


def _finish_dispatch(xs, h2s, slots, zstart, zchunks):
    Bs = h2s.shape[0]
    assert 2 * Bs <= IDX_CHUNK and slots.shape[0] == IDX_CHUNK and MOE_BLOCK % ZERO_CHUNK == 0
    any_spec = pl.BlockSpec(memory_space=pl.ANY)
    grid_spec = pltpu.PrefetchScalarGridSpec(
        num_scalar_prefetch=2,
        grid=(1,),
        in_specs=[any_spec, pl.BlockSpec((Bs,) + ROW_TILE, lambda t, z, n: (0, 0, 0)), any_spec],
        out_specs=any_spec,
        scratch_shapes=[pltpu.SMEM((IDX_CHUNK,), jnp.int32), pltpu.VMEM((ZERO_CHUNK,) + ROW_TILE, ROW_DTYPE),
                        pltpu.SemaphoreType.DMA, pltpu.SemaphoreType.DMA, pltpu.SemaphoreType.DMA],
    )
    return pl.pallas_call(
        _finish_dispatch_kernel,
        grid_spec=grid_spec,
        out_shape=jax.ShapeDtypeStruct(xs.shape, ROW_DTYPE),
        input_output_aliases={4: 0},
        compiler_params=pltpu.CompilerParams(dimension_semantics=("arbitrary",), has_side_effects=True),
        name="moe_finish_dispatch",
    )(zstart, zchunks, slots, h2s, xs)


FFN_IN_BUFFERS = 3
FFN_OUT_BUFFERS = 2


def _ffn_kernel(cnt_ref, xs_hbm, w1_hbm, w3_hbm, w2_hbm, ys_hbm,
                xs_buf, ys_buf, st1, st3, st2, w13_scr, w2_scr, t_row, t_exp, t_len, sem_in, sem_out, sem_w,
                *, cap_blocks):
    B = MOE_BLOCK

    def fill_expert(e, g):
        nb = (cnt_ref[e] + (B - 1)) // B

        def fill_block(j, carry):
            t_row[g + j] = e * cap_blocks + j
            t_exp[g + j] = e
            t_len[g + j] = nb
            return carry

        lax.fori_loop(0, nb, fill_block, 0)
        return g + nb

    n_used = lax.fori_loop(0, N_EXPERTS, fill_expert, 0)

    def in_copy(g):
        s = g % FFN_IN_BUFFERS
        return pltpu.make_async_copy(xs_hbm.at[pl.ds(t_row[g] * B, B)], xs_buf.at[s], sem_in.at[s])

    def out_copy(g):
        s = g % FFN_OUT_BUFFERS
        return pltpu.make_async_copy(ys_buf.at[s], ys_hbm.at[pl.ds(t_row[g] * B, B)], sem_out.at[s])

    def weight_copies(e, s):
        return (pltpu.make_async_copy(w1_hbm.at[e], st1.at[s], sem_w.at[s]),
                pltpu.make_async_copy(w3_hbm.at[e], st3.at[s], sem_w.at[s]),
                pltpu.make_async_copy(w2_hbm.at[e], st2.at[s], sem_w.at[s]))

    @pl.when(n_used > 0)
    def _():
        for cp in weight_copies(t_exp[0], 0):
            cp.start()

    for g0 in range(FFN_IN_BUFFERS - 1):
        @pl.when(g0 < n_used)
        def _(g0=g0):
            in_copy(g0).start()

    def block(g, wslot):
        first = (g == 0) | (t_exp[g] != t_exp[jnp.maximum(g - 1, 0)])
        wslot = jnp.where(first & (g > 0), 1 - wslot, wslot)

        @pl.when(first)
        def _():
            for cp in weight_copies(t_exp[g], wslot):
                cp.wait()
            nxt = g + t_len[g]

            @pl.when(nxt < n_used)
            def _():
                for cp in weight_copies(t_exp[jnp.minimum(nxt, n_used - 1)], 1 - wslot):
                    cp.start()

            w13_scr[:, 0:D_EXPERT] = st1[wslot].astype(_BF)
            w13_scr[:, D_EXPERT:2 * D_EXPERT] = st3[wslot].astype(_BF)
            w2_scr[...] = st2[wslot].astype(_BF)

        in_copy(g).wait()

        @pl.when(g + FFN_IN_BUFFERS - 1 < n_used)
        def _():
            in_copy(g + FFN_IN_BUFFERS - 1).start()

        @pl.when(g >= FFN_OUT_BUFFERS)
        def _():
            out_copy(g - FFN_OUT_BUFFERS).wait()

        xb = _unpack_rows(xs_buf[g % FFN_IN_BUFFERS]).astype(_BF)
        ab = _dot(xb, w13_scr[...])
        hid = jax.nn.silu(ab[:, 0:D_EXPERT]) * ab[:, D_EXPERT:2 * D_EXPERT]
        ys_buf[g % FFN_OUT_BUFFERS] = _pack_rows(_dot(hid.astype(_BF), w2_scr[...]))
        out_copy(g).start()
        return wslot

    lax.fori_loop(0, n_used, block, 0)

    for back in range(FFN_OUT_BUFFERS, 0, -1):
        @pl.when(n_used >= back)
        def _(back=back):
            out_copy(n_used - back).wait()


def _ffn(xs, counts, w1, w3, w2, cap_blocks, n_blocks):
    any_spec = pl.BlockSpec(memory_space=pl.ANY)
    row_buf = lambda n: pltpu.VMEM((n, MOE_BLOCK) + ROW_TILE, ROW_DTYPE)
    grid_spec = pltpu.PrefetchScalarGridSpec(
        num_scalar_prefetch=1,
        grid=(1,),
        in_specs=[any_spec, any_spec, any_spec, any_spec],
        out_specs=any_spec,
        scratch_shapes=[row_buf(FFN_IN_BUFFERS), row_buf(FFN_OUT_BUFFERS),
                        pltpu.VMEM((2, D_MODEL, D_EXPERT), _F32), pltpu.VMEM((2, D_MODEL, D_EXPERT), _F32),
                        pltpu.VMEM((2, D_EXPERT, D_MODEL), _F32),
                        pltpu.VMEM((D_MODEL, 2 * D_EXPERT), _BF), pltpu.VMEM((D_EXPERT, D_MODEL), _BF),
                        pltpu.SMEM((n_blocks,), jnp.int32), pltpu.SMEM((n_blocks,), jnp.int32),
                        pltpu.SMEM((n_blocks,), jnp.int32),
                        pltpu.SemaphoreType.DMA((FFN_IN_BUFFERS,)), pltpu.SemaphoreType.DMA((FFN_OUT_BUFFERS,)),
                        pltpu.SemaphoreType.DMA((2,))],
    )
    return pl.pallas_call(
        functools.partial(_ffn_kernel, cap_blocks=cap_blocks),
        grid_spec=grid_spec,
        out_shape=jax.ShapeDtypeStruct(xs.shape, ROW_DTYPE),
        compiler_params=pltpu.CompilerParams(dimension_semantics=("arbitrary",), vmem_limit_bytes=VMEM_LIMIT,
                                             has_side_effects=True),
        name="moe_ffn",
    )(counts, xs, w1, w3, w2)


def _combine_kernel(slots_hbm, x1_ref, rw_ref, g_ref, ys_hbm, y_ref, idx_smem, buf, sem_idx, sem_rows):
    t = pl.program_id(0)
    TC = x1_ref.shape[0]
    idx_cp = pltpu.make_async_copy(slots_hbm.at[pl.ds(t * IDX_CHUNK, IDX_CHUNK)], idx_smem, sem_idx)
    idx_cp.start()
    idx_cp.wait()

    def issue(r, carry):
        for kk in range(2):
            pltpu.make_async_copy(ys_hbm.at[idx_smem[2 * r + kk]], buf.at[kk, r], sem_rows).start(priority=kk)
        return carry

    lax.fori_loop(0, TC, issue, 0, unroll=ISSUE_UNROLL)
    for kk in range(2):
        pltpu.make_async_copy(ys_hbm.at[pl.ds(0, TC)], buf.at[kk], sem_rows).wait()

    w = rw_ref[...]
    y0 = _unpack_rows(buf[0])
    y1 = _unpack_rows(buf[1])
    x2 = x1_ref[...] + (w[:, 0:1] * y0 + w[:, 1:2] * y1)
    y_ref[...] = _rms(x2, g_ref[...])


SC_GATHER_CHUNKS = 4
DENSE_TILE = 1024
SC_GATHER_BUFFERS = 3
SC_GATHER_MAX_WINDOW = 80


def _sc_gather(ys, rows):
    Tg = rows.shape[1]
    n_workers = SC_CORES * SC_SUBCORES
    per_w = Tg // n_workers
    assert per_w * n_workers == Tg and per_w % SUBLANES == 0
    W = max(w for w in range(SUBLANES, SC_GATHER_MAX_WINDOW + 1, SUBLANES) if per_w % w == 0)
    n_win = per_w // W
    NB = SC_GATHER_BUFFERS
    idx = rows.reshape(2, n_workers, n_win, W)
    jobs = [(kk, i) for i in range(n_win) for kk in range(2)]
    mesh = plsc.VectorSubcoreMesh(core_axis_name="c", subcore_axis_name="s")

    def body(ys_hbm, idx_hbm, out_hbm, i_v, rows_v, sem_g, sem_s):
        wid = lax.axis_index("s") * SC_CORES + lax.axis_index("c")
        base = wid * per_w
        for kk in range(2):
            pltpu.sync_copy(idx_hbm.at[kk, wid], i_v.at[kk])

        def gather(j):
            kk, i = jobs[j]
            return pltpu.make_async_copy(ys_hbm.at[i_v.at[kk, i]], rows_v.at[j % NB], sem_g.at[j % NB])

        def store(j):
            kk, i = jobs[j]
            return pltpu.make_async_copy(rows_v.at[j % NB], out_hbm.at[kk, pl.ds(base + i * W, W)], sem_s.at[j % NB])

        gather(0).start()
        for j in range(len(jobs)):
            gather(j).wait()
            store(j).start()
            if j + 1 < len(jobs):
                if j + 1 >= NB:
                    store(j + 1 - NB).wait()
                gather(j + 1).start()
        for j in range(max(0, len(jobs) - NB), len(jobs)):
            store(j).wait()

    return pl.kernel(
        body, mesh=mesh,
        out_type=jax.ShapeDtypeStruct((2, Tg) + ROW_TILE, ROW_DTYPE),
        scratch_types=[pltpu.VMEM((2, n_win, W), jnp.int32), pltpu.VMEM((NB, W) + ROW_TILE, ROW_DTYPE),
                       pltpu.SemaphoreType.DMA((NB,)), pltpu.SemaphoreType.DMA((NB,))],
        name="moe_sc_gather",
    )(ys, idx)


def _combine_dense_kernel(x1_ref, rw_ref, g_ref, rows_ref, *rest):
    y_ref = rest[-1]
    w = rw_ref[...]
    y0 = _unpack_rows(rows_ref[0])
    y1 = _unpack_rows(rows_ref[1])
    x2 = x1_ref[...] + (w[:, 0:1] * y0 + w[:, 1:2] * y1)
    y_ref[...] = _rms(x2, g_ref[...])


def _combine_dense(x1, rw, rows, y, g_final, tile, first_tile):
    Tg = rows.shape[1]
    assert Tg % tile == 0
    tok = lambda t: (first_tile + t, 0)
    in_specs = [pl.BlockSpec((tile, D_MODEL), tok), pl.BlockSpec((tile, LANES), tok), _full((1, D_MODEL)),
                pl.BlockSpec((2, tile) + ROW_TILE, lambda t: (0, t, 0, 0))]
    args = [x1, rw, g_final.reshape(1, D_MODEL), rows]
    aliases = {}
    if y is not None:
        in_specs.append(pl.BlockSpec(memory_space=pl.ANY))
        args.append(y)
        aliases = {4: 0}
    return pl.pallas_call(
        _combine_dense_kernel,
        grid=(Tg // tile,),
        in_specs=in_specs,
        out_specs=pl.BlockSpec((tile, D_MODEL), tok),
        out_shape=jax.ShapeDtypeStruct(x1.shape, _F32),
        input_output_aliases=aliases,
        compiler_params=pltpu.CompilerParams(dimension_semantics=("arbitrary",), vmem_limit_bytes=VMEM_LIMIT),
        name="moe_combine_dense",
    )(*args)


def _combine(x1, rw, slots, ys, g_final, tile, n_tiles):
    T = x1.shape[0]
    assert T % tile == 0 and 2 * tile <= IDX_CHUNK and slots.shape[0] == n_tiles * IDX_CHUNK
    any_spec = pl.BlockSpec(memory_space=pl.ANY)
    return pl.pallas_call(
        _combine_kernel,
        grid=(n_tiles,),
        in_specs=[any_spec, pl.BlockSpec((tile, D_MODEL), lambda t: (t, 0)),
                  pl.BlockSpec((tile, LANES), lambda t: (t, 0)), _full((1, D_MODEL)), any_spec],
        out_specs=pl.BlockSpec((tile, D_MODEL), lambda t: (t, 0)),
        out_shape=jax.ShapeDtypeStruct((T, D_MODEL), _F32),
        scratch_shapes=[pltpu.SMEM((IDX_CHUNK,), jnp.int32), pltpu.VMEM((2, tile) + ROW_TILE, ROW_DTYPE),
                        pltpu.SemaphoreType.DMA, pltpu.SemaphoreType.DMA],
        compiler_params=pltpu.CompilerParams(dimension_semantics=("arbitrary",), vmem_limit_bytes=VMEM_LIMIT),
        name="moe_combine",
    )(slots, x1, rw, g_final.reshape(1, D_MODEL), ys)


def _tile_for(n_tokens):
    tile = IDX_CHUNK // 2
    return tile if n_tokens % tile == 0 else n_tokens


def _chunked_slots(slot, tile):
    n_tiles = slot.shape[0] // tile
    s = slot.reshape(n_tiles, 2 * tile)
    return jnp.pad(s, ((0, 0), (0, IDX_CHUNK - 2 * tile))).reshape(-1)


def kernel(x_prompt, x_sample, state_pool, state_ret, g_mix, w_in, w_pool, pool_scale, ret_gn, w_out, g_ffn, w_grp, w_exp, w1, w3, w2, g_final):
    Bp, Lp, _ = x_prompt.shape
    Bs = x_sample.shape[0]
    Tp = Bp * Lp
    wts = _prep_weights(g_mix[0], w_in[0], w_pool[0], pool_scale[0], ret_gn[0], w_out[0], g_ffn[0], w_grp[0], w_exp[0])

    T_all = Tp + Bs
    cap = (-(-T_all // MOE_BLOCK) + 1) * MOE_BLOCK
    cap_blocks = cap // MOE_BLOCK

    x1p, dst_rows_p, rwp, npool_p, nret_p, counts_p, h2p = _mixer_prompt(x_prompt, wts, cap)
    xs = _sc_scatter(h2p, dst_rows_p, N_EXPERTS * cap)
    x1s, dst_s, rws, npool_s, nret_s, counts, h2s = _mixer_sample(
        x_sample.reshape(Bs, D_MODEL), jnp.swapaxes(state_pool[0], 0, 1), state_ret[0], wts, PAST_LEN, counts_p, cap)

    counts = counts[0, :N_EXPERTS].astype(jnp.int32)
    tile_p, tile_s = _tile_for(Tp), _tile_for(Bs)
    slots_s = _chunked_slots(dst_s[:, :2], tile_s)
    zstart = jnp.arange(N_EXPERTS, dtype=jnp.int32) * cap + counts
    zchunks = ((-counts) % MOE_BLOCK + ZERO_CHUNK - 1) // ZERO_CHUNK
    xs = _finish_dispatch(xs, h2s, slots_s, zstart, zchunks.astype(jnp.int32))
    n_blocks = -(-2 * T_all // MOE_BLOCK) + N_EXPERTS
    ys = _ffn(xs, counts, w1[0], w3[0], w2[0], cap_blocks, n_blocks)
    tile_d = DENSE_TILE if Tp % DENSE_TILE == 0 else tile_p
    n_tiles_d = Tp // tile_d
    chunk_tiles = [n_tiles_d // SC_GATHER_CHUNKS + (k < n_tiles_d % SC_GATHER_CHUNKS)
                   for k in range(SC_GATHER_CHUNKS)]
    gathered, first = [], 0
    for nt in chunk_tiles:
        if nt:
            gathered.append((first, _sc_gather(ys, dst_rows_p[:, first * tile_d:(first + nt) * tile_d])))
            first += nt
    y_s = _combine(x1s, rws, slots_s, ys, g_final, tile_s, 1)
    y_p = None
    for first, rows in gathered:
        y_p = _combine_dense(x1p, rwp, rows, y_p, g_final, tile_d, first)

    return (y_p.reshape(Bp, Lp, D_MODEL), y_s.reshape(Bs, 1, D_MODEL),
            npool_p[None], nret_p[None], jnp.swapaxes(npool_s, 0, 1)[None], nret_s[None])
```

```python
import functools

import jax
import jax.numpy as jnp
import numpy as np
from jax import lax
from jax.experimental import pallas as pl
from jax.experimental.pallas import tpu as pltpu
from jax.experimental.pallas import tpu_sc as plsc

D_MODEL = 1024
EPS = 1e-6
POOL_GROUPS = 4
POOL_IN = D_MODEL // 2
POOL_GC = POOL_IN // POOL_GROUPS
POOL_OUT_GC = D_MODEL // POOL_GROUPS
POOL_WINDOWS = (2, 4, 8, 16)
POOL_HIST = max(POOL_WINDOWS) - 1
RET_HEADS = 4
RET_DK = D_MODEL // 8
RET_DV = D_MODEL // RET_HEADS
ROPE_BASE = 10000.0
PAST_LEN = 16384
N_GROUPS = 4
EXPERTS_PER_GROUP = 8
N_EXPERTS = N_GROUPS * EXPERTS_PER_GROUP
D_EXPERT = D_MODEL // 4
QK_W = RET_HEADS * RET_DK
V_W = RET_HEADS * RET_DV
OFF_U = 0
OFF_Q = POOL_IN
OFF_K = OFF_Q + QK_W
OFF_V = OFF_K + QK_W
OFF_GA = OFF_V + V_W
OFF_GB = OFF_GA + D_MODEL
IN_WIDTH = OFF_GB + D_MODEL

LANES = 128
SUBLANES = 8
ROW_WORDS = D_MODEL // 2
ROW_TILE = (ROW_WORDS // LANES, LANES)
ROW_DTYPE = jnp.uint32
HALO = 16
RET_CHUNK = 256
MIXER_SEQS = 2
MOE_BLOCK = 512
ROUTER_W = LANES
RANK_GROUP = 128
VMEM_LIMIT = 56 * 1024 * 1024

_BF = jnp.bfloat16
_F32 = jnp.float32


def _rms(x, g):
    inv = lax.rsqrt(jnp.mean(x * x, axis=-1, keepdims=True) + EPS)
    return x * inv * g


def _dot(a, b):
    return jnp.dot(a, b, preferred_element_type=_F32)


def _round_bf16(x):
    return x.astype(_BF).astype(_F32)


def _pack_rows(x):
    lo = lax.bitcast_convert_type(x[:, :ROW_WORDS].astype(_BF).astype(_F32), jnp.uint32)
    hi = lax.bitcast_convert_type(x[:, ROW_WORDS:].astype(_BF).astype(_F32), jnp.uint32)
    return ((lo >> 16) | hi).reshape((x.shape[0],) + ROW_TILE)


def _unpack_rows(w):
    w = w.reshape(w.shape[0], ROW_WORDS)
    lo = lax.bitcast_convert_type(w << 16, _F32)
    hi = lax.bitcast_convert_type(w & jnp.uint32(0xFFFF0000), _F32)
    return jnp.concatenate([lo, hi], axis=1)


def _rotary(x, cos2, sin2):
    return x * cos2 + pltpu.roll(x, RET_DK // 2, 1) * sin2


def _route(logits):
    lane = lax.broadcasted_iota(jnp.int32, logits.shape, 1).astype(_F32)
    neg = jnp.float32(-jnp.inf)
    big = jnp.float32(1 << 20)
    lg = jnp.where(lane < N_GROUPS, logits, neg)
    mg = jnp.max(lg, axis=-1, keepdims=True)
    g_idx = jnp.min(jnp.where(lg == mg, lane, big), axis=-1, keepdims=True)
    p_g = 1.0 / jnp.sum(jnp.exp(lg - mg), axis=-1, keepdims=True)
    lo = N_GROUPS + g_idx * EXPERTS_PER_GROUP
    in_grp = (lane >= lo) & (lane < lo + EXPERTS_PER_GROUP)
    le = jnp.where(in_grp, logits, neg)
    m1 = jnp.max(le, axis=-1, keepdims=True)
    i1 = jnp.min(jnp.where(le == m1, lane, big), axis=-1, keepdims=True)
    le2 = jnp.where(lane == i1, neg, le)
    m2 = jnp.max(le2, axis=-1, keepdims=True)
    i2 = jnp.min(jnp.where(le2 == m2, lane, big), axis=-1, keepdims=True)
    t = jnp.exp(m2 - m1)
    den = 1.0 + t
    e0 = (i1 - N_GROUPS).astype(jnp.int32)
    e1 = (i2 - N_GROUPS).astype(jnp.int32)
    return e0, e1, p_g * (1.0 / den), p_g * (t / den)


def _post_mix(x, mixed_ref, w_out_ref, g_ffn_ref, w_rt_ref, x1_ref, rw_ref):
    x1 = x + _dot(mixed_ref[...], w_out_ref[...])
    x1_ref[...] = x1.reshape(x1_ref.shape)
    h2 = _rms(x1, g_ffn_ref[...])
    e0, e1, w0, w1 = _route(_dot(h2.astype(_BF), w_rt_ref[...]))
    lane = lax.broadcasted_iota(jnp.int32, (x.shape[0], LANES), 1)
    rw_ref[...] = jnp.where(lane == 0, w0, jnp.where(lane == 1, w1, 0.0)).reshape(rw_ref.shape)
    return h2, e0, e1


def _route_block(x1, g_ffn_ref, w_rt_ref, carry_scr, cap, valid, h2_ref, rw_ref, dst_t_ref):
    for _ in _route_stages(x1, g_ffn_ref, w_rt_ref, carry_scr, cap, valid, h2_ref, rw_ref, dst_t_ref):
        pass


def _route_stages(x1, g_ffn_ref, w_rt_ref, carry_scr, cap, valid, h2_ref, rw_ref, dst_t_ref):
    h2 = _rms(x1, g_ffn_ref[...])
    logits = _dot(h2.astype(_BF), w_rt_ref[...])
    h2_ref[...] = _pack_rows(h2).reshape(h2_ref.shape)
    yield
    e0, e1, w0, w1 = _route(logits)
    lane = lax.broadcasted_iota(jnp.int32, (x1.shape[0], LANES), 1)
    rw_ref[...] = jnp.where(lane == 0, w0, jnp.where(lane == 1, w1, 0.0)).reshape(rw_ref.shape)
    yield
    dst = _rank_rows(e0, e1, cap, carry_scr, valid)
    dst_t = dst.T[0:SUBLANES, :].astype(jnp.int32)
    n_parts, _, width = dst_t_ref.shape
    for part in range(n_parts):
        dst_t_ref[part] = dst_t[:, part * width:(part + 1) * width]
    yield


def _rank_rows(e0, e1, cap, carry_scr, valid=None):
    R = e0.shape[0]
    lane = lax.broadcasted_iota(jnp.int32, (R, LANES), 1)
    m0 = lane == e0
    m1 = lane == e1
    onehot = jnp.where(m0 | m1, 1.0, 0.0)
    G = min(RANK_GROUP, R)
    r_i = lax.broadcasted_iota(jnp.int32, (G, G), 0)
    c_i = lax.broadcasted_iota(jnp.int32, (G, G), 1)
    tri = jnp.where(c_i < r_i, 1.0, 0.0).astype(_BF)
    running = carry_scr[...]
    parts = []
    for g in range(R // G):
        grp = onehot[g * G:(g + 1) * G]
        parts.append(_dot(tri, grp.astype(_BF)) + running)
        running = running + jnp.sum(grp, axis=0, keepdims=True)
    before = jnp.concatenate(parts, axis=0)
    d0 = e0.astype(_F32) * cap + jnp.sum(jnp.where(m0, before, 0.0), axis=-1, keepdims=True)
    d1 = e1.astype(_F32) * cap + jnp.sum(jnp.where(m1, before, 0.0), axis=-1, keepdims=True)
    added = running - carry_scr[...]
    carry_scr[...] += added if valid is None else added * valid
    return jnp.where(lane == 0, d0, jnp.where(lane == 1, d1, 0.0))


def _group_norm(o, gain):
    mu = jnp.mean(o, axis=-1, keepdims=True)
    d = o - mu
    var = jnp.mean(d * d, axis=-1, keepdims=True)
    return d * lax.rsqrt(var + EPS) * gain


def _mixer_prompt_kernel(x_ref, g_mix_ref, w_in_ref, w_pool_ref, pscale_ref, gn_ref, w_out_ref, g_ffn_ref,
                         w_rt_ref, cos_ref, sin_ref, dmask_ref, xi_ref, zeta_ref, gc_ref,
                         x1_ref, dst_t_ref, rw_ref, npool_ref, nret_ref, cnt_ref, h2_ref,
                         u_scr, s_scr, mixed_scr, carry_scr, z_scr, x1_prev_scr, *, cap):
    c = pl.program_id(1)
    n_c = pl.num_programs(1)
    step = pl.program_id(0) * n_c + c
    NSEQ, C, _ = x_ref.shape
    R = NSEQ * C

    @pl.when(step == 0)
    def _():
        carry_scr[...] = jnp.zeros(carry_scr.shape, _F32)
        x1_prev_scr[...] = jnp.zeros(x1_prev_scr.shape, _F32)

    @pl.when(c == 0)
    def _():
        u_scr[:, 0:HALO, :] = jnp.zeros((NSEQ, HALO, POOL_IN), _F32)
        s_scr[...] = jnp.zeros(s_scr.shape, _F32)

    x = x_ref[...].reshape(R, D_MODEL)
    h = _rms(x, g_mix_ref[...]).astype(_BF)

    stages = _route_stages(x1_prev_scr[...], g_ffn_ref, w_rt_ref, carry_scr, cap, (step > 0).astype(_F32),
                           h2_ref, rw_ref, dst_t_ref)
    for lo, hi in ((0, OFF_V), (OFF_V, OFF_GA), (OFF_GA, OFF_GB), (OFF_GB, IN_WIDTH)):
        z_scr[:, lo:hi] = _dot(h, w_in_ref[:, lo:hi])
        next(stages, None)
    cnt_ref[...] = carry_scr[...]

    u_scr[:, HALO:HALO + C, :] = z_scr[:, OFF_U:OFF_U + POOL_IN].reshape(NSEQ, C, POOL_IN)
    q = z_scr[:, OFF_Q:OFF_Q + QK_W]
    k = z_scr[:, OFF_K:OFF_K + QK_W]
    cos2 = jnp.concatenate([cos_ref[...]] * NSEQ, axis=0)
    sin2 = jnp.concatenate([sin_ref[...]] * NSEQ, axis=0)
    pos1 = (c * C + 1 + lax.broadcasted_iota(jnp.int32, (C, POOL_GC), 0)).astype(_F32)

    for j in range(RET_HEADS):
        win = POOL_WINDOWS[j]
        cs = slice(j * POOL_GC, (j + 1) * POOL_GC)
        n_rows = jnp.minimum(pos1, jnp.float32(win))
        pooled = []
        for sq in range(NSEQ):
            u_j = u_scr[sq, HALO:HALO + C, cs]
            s = u_j
            for d in range(1, win):
                s = s + u_scr[sq, HALO - d:HALO - d + C, cs]
            pooled.append(s / n_rows - u_j)
        osl = slice(j * POOL_OUT_GC, (j + 1) * POOL_OUT_GC)
        pool_out = _dot(jnp.concatenate(pooled, axis=0).astype(_BF), w_pool_ref[j]) * pscale_ref[:, osl]

        qs = slice(j * RET_DK, (j + 1) * RET_DK)
        qb = _rotary(q[:, qs], cos2, sin2).astype(_BF)
        kf = _rotary(k[:, qs], cos2, sin2) * (RET_DK ** -0.5)
        kb = kf.astype(_BF)
        vb = z_scr[:, OFF_V + j * RET_DV:OFF_V + (j + 1) * RET_DV].astype(_BF)
        ret = []
        for sq in range(NSEQ):
            rows = slice(sq * C, (sq + 1) * C)
            scores = lax.dot_general(qb[rows], kb[rows], (((1,), (1,)), ((), ())),
                                     preferred_element_type=_F32) * dmask_ref[j]
            s_old = s_scr[sq, j]
            o = _dot(scores.astype(_BF), vb[rows]) + _dot(qb[rows], s_old.astype(_BF)) * xi_ref[j]
            kz = (kf[rows] * zeta_ref[j]).astype(_BF)
            s_scr[sq, j] = gc_ref[j] * s_old + lax.dot_general(kz, vb[rows], (((0,), (0,)), ((), ())),
                                                               preferred_element_type=_F32)
            ret.append(_group_norm(o, gn_ref[:, osl]))
        ret_out = jnp.concatenate(ret, axis=0)

        ga = z_scr[:, OFF_GA + j * RET_DV:OFF_GA + (j + 1) * RET_DV]
        gb = z_scr[:, OFF_GB + j * RET_DV:OFF_GB + (j + 1) * RET_DV]
        mixed_scr[:, osl] = (jax.nn.sigmoid(ga) * pool_out + jax.nn.sigmoid(gb) * ret_out).astype(_BF)

    x1 = x + _dot(mixed_scr[...], w_out_ref[...])
    x1_ref[...] = x1.reshape(x1_ref.shape)
    x1_prev_scr[...] = x1

    u_scr[:, 0:HALO, :] = u_scr[:, C:C + HALO, :]

    @pl.when(c == n_c - 1)
    def _():
        npool_ref[...] = u_scr[:, HALO + C - POOL_HIST:HALO + C, :]
        nret_ref[...] = s_scr[...]


def _decay_tables(C):
    f32 = np.float32
    log_g = np.log(f32(1.0) - np.exp2(f32(-5.0) - np.arange(RET_HEADS, dtype=f32)))
    i = np.arange(C, dtype=f32)
    diff = i[:, None] - i[None, :]
    dmask = np.where(diff >= 0, np.exp(np.maximum(diff, f32(0.0))[None] * log_g[:, None, None]), f32(0.0))
    xi = np.exp((i[None, :] + f32(1.0)) * log_g[:, None])
    zeta = np.exp((f32(C) - f32(1.0) - i)[None, :] * log_g[:, None])
    g_chunk = np.exp(f32(C) * log_g)
    return dmask.astype(f32), xi.astype(f32), zeta.astype(f32), g_chunk.astype(f32)


def _rope_tables(pos):
    half = RET_DK // 2
    freqs = ROPE_BASE ** (-jnp.arange(half, dtype=_F32) / half)
    ang = pos[:, None] * freqs[None, :]
    cos, sin = jnp.cos(ang), jnp.sin(ang)
    return jnp.concatenate([cos, cos], axis=-1), jnp.concatenate([-sin, sin], axis=-1)


def _const(a, shape=None):
    if shape is not None:
        a = np.ascontiguousarray(np.broadcast_to(a, shape))
    return jnp.asarray(a)


def _full(shape):
    n = len(shape)
    return pl.BlockSpec(shape, lambda *_: (0,) * n)


def _route_tail_kernel(x1_ref, g_ffn_ref, w_rt_ref, cnt_in_ref, dst_t_in, rw_in, h2_in,
                       dst_t_ref, rw_ref, cnt_ref, h2_ref, carry_scr, *, cap):
    del dst_t_in, rw_in, h2_in
    carry_scr[...] = cnt_in_ref[...]
    x1 = x1_ref[...]
    _route_block(x1.reshape(x1.shape[0] * x1.shape[1], D_MODEL), g_ffn_ref, w_rt_ref, carry_scr, cap, None,
                 h2_ref, rw_ref, dst_t_ref)
    cnt_ref[...] = carry_scr[...]


def _route_tail(x1, dst_t, rw, counts, h2, wts, cap, C, last):
    NSEQ = x1.shape[0]
    any_spec = pl.BlockSpec(memory_space=pl.ANY)
    return pl.pallas_call(
        functools.partial(_route_tail_kernel, cap=cap),
        grid=(1,),
        in_specs=[pl.BlockSpec((NSEQ, C, D_MODEL), lambda i: (0, last, 0)), _full((1, D_MODEL)),
                  _full((D_MODEL, ROUTER_W)), _full((1, LANES)), any_spec, any_spec, any_spec],
        out_specs=(pl.BlockSpec((NSEQ, SUBLANES, C), lambda i: (0, 0, last)),
                   pl.BlockSpec((NSEQ, C, LANES), lambda i: (0, last, 0)),
                   _full((1, LANES)),
                   pl.BlockSpec((NSEQ, C) + ROW_TILE, lambda i: (0, last, 0, 0))),
        out_shape=(jax.ShapeDtypeStruct(dst_t.shape, dst_t.dtype), jax.ShapeDtypeStruct(rw.shape, rw.dtype),
                   jax.ShapeDtypeStruct((1, LANES), _F32), jax.ShapeDtypeStruct(h2.shape, h2.dtype)),
        input_output_aliases={4: 0, 5: 1, 6: 3},
        scratch_shapes=[pltpu.VMEM((1, LANES), _F32)],
        compiler_params=pltpu.CompilerParams(dimension_semantics=("arbitrary",), vmem_limit_bytes=VMEM_LIMIT),
        name="mixer_route_tail",
    )(x1, wts["g_ffn"], wts["w_rt"], counts, dst_t, rw, h2)


def _mixer_prompt(x, wts, cap):
    B, L, _ = x.shape
    C = RET_CHUNK if L % RET_CHUNK == 0 else L
    n_c = L // C
    T = B * L
    dmask, xi, zeta, g_chunk = _decay_tables(C)
    dmask = _const(dmask)
    xi_b = _const(xi[:, :, None], (RET_HEADS, C, RET_DV))
    zeta_b = _const(zeta[:, :, None], (RET_HEADS, C, RET_DK))
    gc_b = _const(g_chunk[:, None, None], (RET_HEADS, 1, RET_DV))
    cos2, sin2 = _rope_tables(jnp.arange(L).astype(_F32))

    NSEQ = MIXER_SEQS if B % MIXER_SEQS == 0 else 1
    B2 = B // NSEQ
    T2 = T // NSEQ
    x = x.reshape(NSEQ, B2, L, D_MODEL)
    tok = lambda b, c: (0, b * n_c + c, 0)
    in_specs = [
        pl.BlockSpec((NSEQ, None, C, D_MODEL), lambda b, c: (0, b, c, 0)),
        _full((1, D_MODEL)), _full((D_MODEL, IN_WIDTH)), _full((POOL_GROUPS, POOL_GC, POOL_OUT_GC)),
        _full((1, D_MODEL)), _full((1, D_MODEL)), _full((D_MODEL, D_MODEL)), _full((1, D_MODEL)),
        _full((D_MODEL, ROUTER_W)),
        pl.BlockSpec((C, RET_DK), lambda b, c: (c, 0)), pl.BlockSpec((C, RET_DK), lambda b, c: (c, 0)),
        _full((RET_HEADS, C, C)), _full((RET_HEADS, C, RET_DV)), _full((RET_HEADS, C, RET_DK)),
        _full((RET_HEADS, 1, RET_DV)),
    ]
    R = NSEQ * C
    out_shape = (
        jax.ShapeDtypeStruct((NSEQ, T2, D_MODEL), _F32),
        jax.ShapeDtypeStruct((NSEQ, SUBLANES, T2), jnp.int32),
        jax.ShapeDtypeStruct((NSEQ, T2, LANES), _F32),
        jax.ShapeDtypeStruct((NSEQ, B2, POOL_HIST, POOL_IN), _F32),
        jax.ShapeDtypeStruct((NSEQ, B2, RET_HEADS, RET_DK, RET_DV), _F32),
        jax.ShapeDtypeStruct((1, LANES), _F32),
        jax.ShapeDtypeStruct((NSEQ, T2) + ROW_TILE, ROW_DTYPE),
    )
    prev = lambda b, c: jnp.maximum(b * n_c + c - 1, 0)
    out_specs = (
        pl.BlockSpec((NSEQ, C, D_MODEL), tok),
        pl.BlockSpec((NSEQ, SUBLANES, C), lambda b, c: (0, 0, prev(b, c))),
        pl.BlockSpec((NSEQ, C, LANES), lambda b, c: (0, prev(b, c), 0)),
        pl.BlockSpec((NSEQ, None, POOL_HIST, POOL_IN), lambda b, c: (0, b, 0, 0)),
        pl.BlockSpec((NSEQ, None, RET_HEADS, RET_DK, RET_DV), lambda b, c: (0, b, 0, 0, 0)),
        _full((1, LANES)),
        pl.BlockSpec((NSEQ, C) + ROW_TILE, lambda b, c: (0, prev(b, c), 0, 0)),
    )
    x1, dst_t, rw, npool, nret, counts, h2 = pl.pallas_call(
        functools.partial(_mixer_prompt_kernel, cap=cap),
        grid=(B2, n_c),
        in_specs=in_specs,
        out_specs=out_specs,
        out_shape=out_shape,
        scratch_shapes=[pltpu.VMEM((NSEQ, HALO + C, POOL_IN), _F32),
                        pltpu.VMEM((NSEQ, RET_HEADS, RET_DK, RET_DV), _F32),
                        pltpu.VMEM((R, D_MODEL), _BF),
                        pltpu.VMEM((1, LANES), _F32),
                        pltpu.VMEM((R, IN_WIDTH), _F32),
                        pltpu.VMEM((R, D_MODEL), _F32)],
        compiler_params=pltpu.CompilerParams(dimension_semantics=("arbitrary", "arbitrary"),
                                             vmem_limit_bytes=VMEM_LIMIT),
        name="mixer_prompt",
    )(x, wts["g_mix"], wts["w_in"], wts["w_pool"], wts["pool_scale"], wts["ret_gn"], wts["w_out"],
      wts["g_ffn"], wts["w_rt"], cos2, sin2, dmask, xi_b, zeta_b, gc_b)
    dst_t, rw, counts, h2 = _route_tail(x1, dst_t, rw, counts, h2, wts, cap, C, B2 * n_c - 1)
    dst_rows = jnp.moveaxis(dst_t[:, 0:2, :], 1, 0).reshape(2, T)
    return (x1.reshape(T, D_MODEL), dst_rows, rw.reshape(T, LANES),
            npool.reshape(B, POOL_HIST, POOL_IN), nret.reshape(B, RET_HEADS, RET_DK, RET_DV), counts,
            h2.reshape((T,) + ROW_TILE))


def _prep_weights(g_mix, w_in, w_pool, pool_scale, ret_gn, w_out, g_ffn, w_grp, w_exp):
    w_rt = jnp.concatenate([w_grp, w_exp.reshape(D_MODEL, N_EXPERTS)], axis=1)
    w_rt = jnp.pad(w_rt, ((0, 0), (0, ROUTER_W - w_rt.shape[1])))
    row = lambda v: v.reshape(1, D_MODEL)
    return dict(g_mix=row(g_mix), w_in=w_in.astype(_BF), w_pool=w_pool.astype(_BF), pool_scale=row(pool_scale),
                ret_gn=row(ret_gn), w_out=w_out.astype(_BF), g_ffn=row(g_ffn), w_rt=w_rt.astype(_BF))


SAMPLE_TB = 8


def _mixer_sample_kernel(x_ref, spool_ref, sret_ref, g_mix_ref, w_in_ref, w_pool_ref, pscale_ref, gn_ref,
                         w_out_ref, g_ffn_ref, w_rt_ref, cos_ref, sin_ref, dm_ref, xi_ref, zeta_ref, gc_ref,
                         carry_in_ref,
                         x1_ref, dst_ref, rw_ref, npool_ref, nret_ref, cnt_ref, h2_ref,
                         u_scr, q_scr, k_scr, qt_scr, kt_scr, v_scr, ga_scr, gb_scr, pooled_scr, o_scr, mixed_scr,
                         carry_scr, *, cap):
    t = pl.program_id(0)
    n_t = pl.num_programs(0)
    Bs = x_ref.shape[0]
    TB = sret_ref.shape[0]

    @pl.when(t == 0)
    def _():
        h = _rms(x_ref[...], g_mix_ref[...]).astype(_BF)
        u_scr[...] = _dot(h, w_in_ref[:, OFF_U:OFF_U + POOL_IN])
        q = _dot(h, w_in_ref[:, OFF_Q:OFF_Q + QK_W])
        k = _dot(h, w_in_ref[:, OFF_K:OFF_K + QK_W])
        for j in range(RET_HEADS):
            qs = slice(j * RET_DK, (j + 1) * RET_DK)
            qf = _rotary(q[:, qs], cos_ref[...], sin_ref[...])
            kf = _rotary(k[:, qs], cos_ref[...], sin_ref[...]) * (RET_DK ** -0.5)
            q_scr[:, qs] = qf
            k_scr[:, qs] = kf
            qt_scr[j] = _round_bf16(qf).T
            kt_scr[j] = kf.T
        v_scr[...] = _dot(h, w_in_ref[:, OFF_V:OFF_V + V_W])
        ga_scr[...] = _dot(h, w_in_ref[:, OFF_GA:OFF_GA + D_MODEL])
        gb_scr[...] = _dot(h, w_in_ref[:, OFF_GB:OFF_GB + D_MODEL])

    shift = (Bs - t * TB) % Bs
    qt = [pltpu.roll(qt_scr[j], shift, 1) for j in range(RET_HEADS)]
    kt = [pltpu.roll(kt_scr[j], shift, 1) for j in range(RET_HEADS)]

    blk = pl.ds(pl.multiple_of(t * TB, TB), TB)
    u_blk = u_scr[blk, :]
    q_blk = q_scr[blk, :]
    k_blk = k_scr[blk, :]
    v_blk = v_scr[blk, :]
    score = [jnp.sum(q_blk[:, j * RET_DK:(j + 1) * RET_DK] * k_blk[:, j * RET_DK:(j + 1) * RET_DK],
                     axis=-1, keepdims=True) * dm_ref[j] for j in range(RET_HEADS)]

    groups = []
    for g, win in enumerate(POOL_WINDOWS):
        cs = slice(g * POOL_GC, (g + 1) * POOL_GC)
        s = u_blk[:, cs]
        for r in range(POOL_HIST - (win - 1), POOL_HIST):
            s = s + spool_ref[r, :, cs]
        groups.append(s / jnp.float32(win) - u_blk[:, cs])
    pooled_scr[blk, :] = jnp.concatenate(groups, axis=1)
    npool_ref[0:POOL_HIST - 1] = spool_ref[1:POOL_HIST]
    npool_ref[POOL_HIST - 1] = u_blk

    o_rows = []
    for i in range(TB):
        heads = []
        for j in range(RET_HEADS):
            s_old = sret_ref[i, j]
            v_row = v_blk[i:i + 1, j * RET_DV:(j + 1) * RET_DV]
            qcol = qt[j][:, i:i + 1]
            kcol = kt[j][:, i:i + 1]
            qs_old = jnp.sum(qcol * _round_bf16(s_old), axis=0, keepdims=True)
            heads.append(score[j][i:i + 1, :] * v_row + qs_old * xi_ref[j])
            nret_ref[i, j] = gc_ref[j] * s_old + (kcol * zeta_ref[j]) * v_row
        o_rows.append(jnp.concatenate(heads, axis=1))
    o_scr[blk, :] = jnp.concatenate(o_rows, axis=0)

    @pl.when(t == n_t - 1)
    def _():
        for j in range(RET_HEADS):
            cs = slice(j * POOL_GC, (j + 1) * POOL_GC)
            osl = slice(j * POOL_OUT_GC, (j + 1) * POOL_OUT_GC)
            pool_out = _dot(pooled_scr[:, cs].astype(_BF), w_pool_ref[j]) * pscale_ref[:, osl]
            ret_out = _group_norm(o_scr[:, osl], gn_ref[:, osl])
            mixed_scr[:, osl] = (jax.nn.sigmoid(ga_scr[:, osl]) * pool_out
                                 + jax.nn.sigmoid(gb_scr[:, osl]) * ret_out).astype(_BF)
        h2, e0, e1 = _post_mix(x_ref[...], mixed_scr, w_out_ref, g_ffn_ref, w_rt_ref, x1_ref, rw_ref)
        h2_ref[...] = _pack_rows(h2).reshape(h2_ref.shape)
        carry_scr[...] = carry_in_ref[...]
        dst_ref[...] = _rank_rows(e0, e1, cap, carry_scr).astype(jnp.int32)
        cnt_ref[...] = carry_scr[...]


def _mixer_sample(x, state_pool, state_ret, wts, pos0, carry_in, cap):
    Bs = x.shape[0]
    assert Bs == LANES and Bs % SAMPLE_TB == 0
    TB = SAMPLE_TB
    dmask, xi, zeta, g_chunk = _decay_tables(1)
    dm_b = _const(dmask, (RET_HEADS, 1, 1))
    xi_b = _const(xi[:, :, None], (RET_HEADS, 1, RET_DV))
    zeta_b = _const(zeta[:, :, None], (RET_HEADS, 1, 1))
    gc_b = _const(g_chunk[:, None, None], (RET_HEADS, 1, RET_DV))
    cos2, sin2 = _rope_tables((pos0 + jnp.arange(1)).astype(_F32))

    in_specs = [
        _full((Bs, D_MODEL)),
        pl.BlockSpec((POOL_HIST, TB, POOL_IN), lambda t: (0, t, 0)),
        pl.BlockSpec((TB, RET_HEADS, RET_DK, RET_DV), lambda t: (t, 0, 0, 0)),
        _full((1, D_MODEL)), _full((D_MODEL, IN_WIDTH)), _full((POOL_GROUPS, POOL_GC, POOL_OUT_GC)),
        _full((1, D_MODEL)), _full((1, D_MODEL)), _full((D_MODEL, D_MODEL)), _full((1, D_MODEL)),
        _full((D_MODEL, ROUTER_W)),
        _full((1, RET_DK)), _full((1, RET_DK)),
        _full((RET_HEADS, 1, 1)), _full((RET_HEADS, 1, RET_DV)), _full((RET_HEADS, 1, 1)),
        _full((RET_HEADS, 1, RET_DV)),
        _full((1, LANES)),
    ]
    out_shape = (
        jax.ShapeDtypeStruct((Bs, D_MODEL), _F32),
        jax.ShapeDtypeStruct((Bs, LANES), jnp.int32),
        jax.ShapeDtypeStruct((Bs, LANES), _F32),
        jax.ShapeDtypeStruct((POOL_HIST, Bs, POOL_IN), _F32),
        jax.ShapeDtypeStruct((Bs, RET_HEADS, RET_DK, RET_DV), _F32),
        jax.ShapeDtypeStruct((1, LANES), _F32),
        jax.ShapeDtypeStruct((Bs,) + ROW_TILE, ROW_DTYPE),
    )
    out_specs = (
        _full((Bs, D_MODEL)), _full((Bs, LANES)), _full((Bs, LANES)),
        pl.BlockSpec((POOL_HIST, TB, POOL_IN), lambda t: (0, t, 0)),
        pl.BlockSpec((TB, RET_HEADS, RET_DK, RET_DV), lambda t: (t, 0, 0, 0)),
        _full((1, LANES)), _full((Bs,) + ROW_TILE),
    )
    f32 = lambda *s: pltpu.VMEM(s, _F32)
    return pl.pallas_call(
        functools.partial(_mixer_sample_kernel, cap=cap),
        grid=(Bs // TB,),
        in_specs=in_specs,
        out_specs=out_specs,
        out_shape=out_shape,
        scratch_shapes=[f32(Bs, POOL_IN), f32(Bs, QK_W), f32(Bs, QK_W), f32(RET_HEADS, RET_DK, Bs),
                        f32(RET_HEADS, RET_DK, Bs), f32(Bs, V_W), f32(Bs, D_MODEL), f32(Bs, D_MODEL),
                        f32(Bs, POOL_IN), f32(Bs, V_W), pltpu.VMEM((Bs, D_MODEL), _BF), f32(1, LANES)],
        compiler_params=pltpu.CompilerParams(dimension_semantics=("arbitrary",), vmem_limit_bytes=VMEM_LIMIT),
        name="mixer_sample",
    )(x, state_pool, state_ret, wts["g_mix"], wts["w_in"], wts["w_pool"], wts["pool_scale"], wts["ret_gn"],
      wts["w_out"], wts["g_ffn"], wts["w_rt"], cos2, sin2, dm_b, xi_b, zeta_b, gc_b, carry_in)


IDX_CHUNK = 1024
ISSUE_UNROLL = 8
SC_CORES = 2
SC_SUBCORES = 16
SC_WINDOW = 64


def _sc_scatter(h2, dst_rows, n_rows):
    T = h2.shape[0]
    n_workers = SC_CORES * SC_SUBCORES
    W = SC_WINDOW
    assert T % (n_workers * W) == 0
    per_w = T // n_workers
    n_win = per_w // W
    d0 = dst_rows[0].reshape(T // W, W)
    d1 = dst_rows[1].reshape(T // W, W)
    mesh = plsc.VectorSubcoreMesh(core_axis_name="c", subcore_axis_name="s")

    def body(h2_hbm, d0_hbm, d1_hbm, xs_hbm, i0_v, i1_v, rows_v, sem_load, sem_store):
        wid = lax.axis_index("s") * SC_CORES + lax.axis_index("c")
        base = wid * per_w
        pltpu.sync_copy(d0_hbm.at[pl.ds(wid * n_win, n_win)], i0_v)
        pltpu.sync_copy(d1_hbm.at[pl.ds(wid * n_win, n_win)], i1_v)
        pltpu.async_copy(h2_hbm.at[pl.ds(base, W)], rows_v.at[0], sem_load)
        for i in range(n_win):
            b = i % 2
            pltpu.make_async_copy(h2_hbm.at[pl.ds(base, W)], rows_v.at[b], sem_load).wait()
            if i + 1 < n_win:
                pltpu.async_copy(h2_hbm.at[pl.ds(base + (i + 1) * W, W)], rows_v.at[1 - b], sem_load)
            c0 = pltpu.async_copy(rows_v.at[b], xs_hbm.at[i0_v.at[i]], sem_store)
            c1 = pltpu.async_copy(rows_v.at[b], xs_hbm.at[i1_v.at[i]], sem_store)
            c0.wait()
            c1.wait()

    return pl.kernel(
        body, mesh=mesh,
        out_type=jax.ShapeDtypeStruct((n_rows,) + ROW_TILE, ROW_DTYPE),
        scratch_types=[pltpu.VMEM((n_win, W), jnp.int32), pltpu.VMEM((n_win, W), jnp.int32),
                       pltpu.VMEM((2, W) + ROW_TILE, ROW_DTYPE),
                       pltpu.SemaphoreType.DMA, pltpu.SemaphoreType.DMA],
        name="moe_sc_scatter",
    )(h2, d0, d1)


ZERO_CHUNK = 64


def _finish_dispatch_kernel(zstart_ref, zchunks_ref, slots_hbm, h2_ref, xs_in_hbm, xs_hbm, idx_smem, zbuf, sem_idx,
                            sem_rows, sem_zero):
    del xs_in_hbm
    TS = h2_ref.shape[0]
    idx_cp = pltpu.make_async_copy(slots_hbm.at[pl.ds(0, IDX_CHUNK)], idx_smem, sem_idx)
    idx_cp.start()
    zbuf[...] = jnp.zeros(zbuf.shape, zbuf.dtype)

    def zero_copy(e, j):
        return pltpu.make_async_copy(zbuf, xs_hbm.at[pl.ds(zstart_ref[e] + j * ZERO_CHUNK, ZERO_CHUNK)], sem_zero)

    for e in range(N_EXPERTS):
        lax.fori_loop(0, zchunks_ref[e], lambda j, c, e=e: (zero_copy(e, j).start(), c)[1], 0)
    idx_cp.wait()

    def issue(r, carry):
        for kk in range(2):
            pltpu.make_async_copy(h2_ref.at[r], xs_hbm.at[idx_smem[2 * r + kk]], sem_rows).start(priority=kk)
        return carry

    lax.fori_loop(0, TS, issue, 0, unroll=ISSUE_UNROLL)
    for e in range(N_EXPERTS):
        lax.fori_loop(0, zchunks_ref[e], lambda j, c, e=e: (zero_copy(e, j).wait(), c)[1], 0)
    for kk in range(2):
        pltpu.make_async_copy(h2_ref, xs_hbm.at[pl.ds(0, TS)], sem_rows).wait()


def _finish_dispatch(xs, h2s, slots, zstart, zchunks):
    Bs = h2s.shape[0]
    assert 2 * Bs <= IDX_CHUNK and slots.shape[0] == IDX_CHUNK and MOE_BLOCK % ZERO_CHUNK == 0
    any_spec = pl.BlockSpec(memory_space=pl.ANY)
    grid_spec = pltpu.PrefetchScalarGridSpec(
        num_scalar_prefetch=2,
        grid=(1,),
        in_specs=[any_spec, pl.BlockSpec((Bs,) + ROW_TILE, lambda t, z, n: (0, 0, 0)), any_spec],
        out_specs=any_spec,
        scratch_shapes=[pltpu.SMEM((IDX_CHUNK,), jnp.int32), pltpu.VMEM((ZERO_CHUNK,) + ROW_TILE, ROW_DTYPE),
                        pltpu.SemaphoreType.DMA, pltpu.SemaphoreType.DMA, pltpu.SemaphoreType.DMA],
    )
    return pl.pallas_call(
        _finish_dispatch_kernel,
        grid_spec=grid_spec,
        out_shape=jax.ShapeDtypeStruct(xs.shape, ROW_DTYPE),
        input_output_aliases={4: 0},
        compiler_params=pltpu.CompilerParams(dimension_semantics=("arbitrary",), has_side_effects=True),
        name="moe_finish_dispatch",
    )(zstart, zchunks, slots, h2s, xs)


FFN_IN_BUFFERS = 4
FFN_OUT_BUFFERS = 3


def _ffn_kernel(cnt_ref, xs_hbm, w1_hbm, w3_hbm, w2_hbm, ys_hbm,
                xs_buf, ys_buf, st1, st3, st2, w13_scr, w2_scr, t_row, t_exp, t_len, sem_in, sem_out, sem_w,
                *, cap_blocks):
    B = MOE_BLOCK

    def fill_expert(e, g):
        nb = (cnt_ref[e] + (B - 1)) // B

        def fill_block(j, carry):
            t_row[g + j] = e * cap_blocks + j
            t_exp[g + j] = e
            t_len[g + j] = nb
            return carry

        lax.fori_loop(0, nb, fill_block, 0)
        return g + nb

    n_used = lax.fori_loop(0, N_EXPERTS, fill_expert, 0)

    def in_copy(g):
        s = g % FFN_IN_BUFFERS
        return pltpu.make_async_copy(xs_hbm.at[pl.ds(t_row[g] * B, B)], xs_buf.at[s], sem_in.at[s])

    def out_copy(g):
        s = g % FFN_OUT_BUFFERS
        return pltpu.make_async_copy(ys_buf.at[s], ys_hbm.at[pl.ds(t_row[g] * B, B)], sem_out.at[s])

    def weight_copies(e, s):
        return (pltpu.make_async_copy(w1_hbm.at[e], st1.at[s], sem_w.at[s]),
                pltpu.make_async_copy(w3_hbm.at[e], st3.at[s], sem_w.at[s]),
                pltpu.make_async_copy(w2_hbm.at[e], st2.at[s], sem_w.at[s]))

    @pl.when(n_used > 0)
    def _():
        for cp in weight_copies(t_exp[0], 0):
            cp.start()

    for g0 in range(FFN_IN_BUFFERS - 1):
        @pl.when(g0 < n_used)
        def _(g0=g0):
            in_copy(g0).start()

    def block(g, wslot):
        first = (g == 0) | (t_exp[g] != t_exp[jnp.maximum(g - 1, 0)])
        wslot = jnp.where(first & (g > 0), 1 - wslot, wslot)

        @pl.when(first)
        def _():
            for cp in weight_copies(t_exp[g], wslot):
                cp.wait()
            nxt = g + t_len[g]

            @pl.when(nxt < n_used)
            def _():
                for cp in weight_copies(t_exp[jnp.minimum(nxt, n_used - 1)], 1 - wslot):
                    cp.start()

            w13_scr[:, 0:D_EXPERT] = st1[wslot].astype(_BF)
            w13_scr[:, D_EXPERT:2 * D_EXPERT] = st3[wslot].astype(_BF)
            w2_scr[...] = st2[wslot].astype(_BF)

        in_copy(g).wait()

        @pl.when(g + FFN_IN_BUFFERS - 1 < n_used)
        def _():
            in_copy(g + FFN_IN_BUFFERS - 1).start()

        @pl.when(g >= FFN_OUT_BUFFERS)
        def _():
            out_copy(g - FFN_OUT_BUFFERS).wait()

        xb = _unpack_rows(xs_buf[g % FFN_IN_BUFFERS]).astype(_BF)
        ab = _dot(xb, w13_scr[...])
        hid = jax.nn.silu(ab[:, 0:D_EXPERT]) * ab[:, D_EXPERT:2 * D_EXPERT]
        ys_buf[g % FFN_OUT_BUFFERS] = _pack_rows(_dot(hid.astype(_BF), w2_scr[...]))
        out_copy(g).start()
        return wslot

    lax.fori_loop(0, n_used, block, 0)

    for back in range(FFN_OUT_BUFFERS, 0, -1):
        @pl.when(n_used >= back)
        def _(back=back):
            out_copy(n_used - back).wait()


def _ffn(xs, counts, w1, w3, w2, cap_blocks, n_blocks):
    any_spec = pl.BlockSpec(memory_space=pl.ANY)
    row_buf = lambda n: pltpu.VMEM((n, MOE_BLOCK) + ROW_TILE, ROW_DTYPE)
    grid_spec = pltpu.PrefetchScalarGridSpec(
        num_scalar_prefetch=1,
        grid=(1,),
        in_specs=[any_spec, any_spec, any_spec, any_spec],
        out_specs=any_spec,
        scratch_shapes=[row_buf(FFN_IN_BUFFERS), row_buf(FFN_OUT_BUFFERS),
                        pltpu.VMEM((2, D_MODEL, D_EXPERT), _F32), pltpu.VMEM((2, D_MODEL, D_EXPERT), _F32),
                        pltpu.VMEM((2, D_EXPERT, D_MODEL), _F32),
                        pltpu.VMEM((D_MODEL, 2 * D_EXPERT), _BF), pltpu.VMEM((D_EXPERT, D_MODEL), _BF),
                        pltpu.SMEM((n_blocks,), jnp.int32), pltpu.SMEM((n_blocks,), jnp.int32),
                        pltpu.SMEM((n_blocks,), jnp.int32),
                        pltpu.SemaphoreType.DMA((FFN_IN_BUFFERS,)), pltpu.SemaphoreType.DMA((FFN_OUT_BUFFERS,)),
                        pltpu.SemaphoreType.DMA((2,))],
    )
    return pl.pallas_call(
        functools.partial(_ffn_kernel, cap_blocks=cap_blocks),
        grid_spec=grid_spec,
        out_shape=jax.ShapeDtypeStruct(xs.shape, ROW_DTYPE),
        compiler_params=pltpu.CompilerParams(dimension_semantics=("arbitrary",), vmem_limit_bytes=VMEM_LIMIT,
                                             has_side_effects=True),
        name="moe_ffn",
    )(counts, xs, w1, w3, w2)


def _combine_kernel(slots_hbm, x1_ref, rw_ref, g_ref, ys_hbm, y_ref, idx_smem, buf, sem_idx, sem_rows):
    t = pl.program_id(0)
    TC = x1_ref.shape[0]
    idx_cp = pltpu.make_async_copy(slots_hbm.at[pl.ds(t * IDX_CHUNK, IDX_CHUNK)], idx_smem, sem_idx)
    idx_cp.start()
    idx_cp.wait()

    def issue(r, carry):
        for kk in range(2):
            pltpu.make_async_copy(ys_hbm.at[idx_smem[2 * r + kk]], buf.at[kk, r], sem_rows).start(priority=kk)
        return carry

    lax.fori_loop(0, TC, issue, 0, unroll=ISSUE_UNROLL)
    for kk in range(2):
        pltpu.make_async_copy(ys_hbm.at[pl.ds(0, TC)], buf.at[kk], sem_rows).wait()

    w = rw_ref[...]
    y0 = _unpack_rows(buf[0])
    y1 = _unpack_rows(buf[1])
    x2 = x1_ref[...] + (w[:, 0:1] * y0 + w[:, 1:2] * y1)
    y_ref[...] = _rms(x2, g_ref[...])


SC_GATHER_CHUNKS = 4
DENSE_TILE = 1024
SC_GATHER_BUFFERS = 3
SC_GATHER_MAX_WINDOW = 80


def _sc_gather(ys, rows):
    Tg = rows.shape[1]
    n_workers = SC_CORES * SC_SUBCORES
    per_w = Tg // n_workers
    assert per_w * n_workers == Tg and per_w % SUBLANES == 0
    W = max(w for w in range(SUBLANES, SC_GATHER_MAX_WINDOW + 1, SUBLANES) if per_w % w == 0)
    n_win = per_w // W
    NB = SC_GATHER_BUFFERS
    idx = rows.reshape(2, n_workers, n_win, W)
    jobs = [(kk, i) for i in range(n_win) for kk in range(2)]
    mesh = plsc.VectorSubcoreMesh(core_axis_name="c", subcore_axis_name="s")

    def body(ys_hbm, idx_hbm, out_hbm, i_v, rows_v, sem_g, sem_s):
        wid = lax.axis_index("s") * SC_CORES + lax.axis_index("c")
        base = wid * per_w
        for kk in range(2):
            pltpu.sync_copy(idx_hbm.at[kk, wid], i_v.at[kk])

        def gather(j):
            kk, i = jobs[j]
            return pltpu.make_async_copy(ys_hbm.at[i_v.at[kk, i]], rows_v.at[j % NB], sem_g.at[j % NB])

        def store(j):
            kk, i = jobs[j]
            return pltpu.make_async_copy(rows_v.at[j % NB], out_hbm.at[kk, pl.ds(base + i * W, W)], sem_s.at[j % NB])

        gather(0).start()
        for j in range(len(jobs)):
            gather(j).wait()
            store(j).start()
            if j + 1 < len(jobs):
                if j + 1 >= NB:
                    store(j + 1 - NB).wait()
                gather(j + 1).start()
        for j in range(max(0, len(jobs) - NB), len(jobs)):
            store(j).wait()

    return pl.kernel(
        body, mesh=mesh,
        out_type=jax.ShapeDtypeStruct((2, Tg) + ROW_TILE, ROW_DTYPE),
        scratch_types=[pltpu.VMEM((2, n_win, W), jnp.int32), pltpu.VMEM((NB, W) + ROW_TILE, ROW_DTYPE),
                       pltpu.SemaphoreType.DMA((NB,)), pltpu.SemaphoreType.DMA((NB,))],
        name="moe_sc_gather",
    )(ys, idx)


def _combine_dense_kernel(x1_ref, rw_ref, g_ref, rows_ref, *rest):
    y_ref = rest[-1]
    w = rw_ref[...]
    y0 = _unpack_rows(rows_ref[0])
    y1 = _unpack_rows(rows_ref[1])
    x2 = x1_ref[...] + (w[:, 0:1] * y0 + w[:, 1:2] * y1)
    y_ref[...] = _rms(x2, g_ref[...])


def _combine_dense(x1, rw, rows, y, g_final, tile, first_tile):
    Tg = rows.shape[1]
    assert Tg % tile == 0
    tok = lambda t: (first_tile + t, 0)
    in_specs = [pl.BlockSpec((tile, D_MODEL), tok), pl.BlockSpec((tile, LANES), tok), _full((1, D_MODEL)),
                pl.BlockSpec((2, tile) + ROW_TILE, lambda t: (0, t, 0, 0))]
    args = [x1, rw, g_final.reshape(1, D_MODEL), rows]
    aliases = {}
    if y is not None:
        in_specs.append(pl.BlockSpec(memory_space=pl.ANY))
        args.append(y)
        aliases = {4: 0}
    return pl.pallas_call(
        _combine_dense_kernel,
        grid=(Tg // tile,),
        in_specs=in_specs,
        out_specs=pl.BlockSpec((tile, D_MODEL), tok),
        out_shape=jax.ShapeDtypeStruct(x1.shape, _F32),
        input_output_aliases=aliases,
        compiler_params=pltpu.CompilerParams(dimension_semantics=("arbitrary",), vmem_limit_bytes=VMEM_LIMIT),
        name="moe_combine_dense",
    )(*args)


def _combine(x1, rw, slots, ys, g_final, tile, n_tiles):
    T = x1.shape[0]
    assert T % tile == 0 and 2 * tile <= IDX_CHUNK and slots.shape[0] == n_tiles * IDX_CHUNK
    any_spec = pl.BlockSpec(memory_space=pl.ANY)
    return pl.pallas_call(
        _combine_kernel,
        grid=(n_tiles,),
        in_specs=[any_spec, pl.BlockSpec((tile, D_MODEL), lambda t: (t, 0)),
                  pl.BlockSpec((tile, LANES), lambda t: (t, 0)), _full((1, D_MODEL)), any_spec],
        out_specs=pl.BlockSpec((tile, D_MODEL), lambda t: (t, 0)),
        out_shape=jax.ShapeDtypeStruct((T, D_MODEL), _F32),
        scratch_shapes=[pltpu.SMEM((IDX_CHUNK,), jnp.int32), pltpu.VMEM((2, tile) + ROW_TILE, ROW_DTYPE),
                        pltpu.SemaphoreType.DMA, pltpu.SemaphoreType.DMA],
        compiler_params=pltpu.CompilerParams(dimension_semantics=("arbitrary",), vmem_limit_bytes=VMEM_LIMIT),
        name="moe_combine",
    )(slots, x1, rw, g_final.reshape(1, D_MODEL), ys)


def _tile_for(n_tokens):
    tile = IDX_CHUNK // 2
    return tile if n_tokens % tile == 0 else n_tokens


def _chunked_slots(slot, tile):
    n_tiles = slot.shape[0] // tile
    s = slot.reshape(n_tiles, 2 * tile)
    return jnp.pad(s, ((0, 0), (0, IDX_CHUNK - 2 * tile))).reshape(-1)


def kernel(x_prompt, x_sample, state_pool, state_ret, g_mix, w_in, w_pool, pool_scale, ret_gn, w_out, g_ffn, w_grp, w_exp, w1, w3, w2, g_final):
    Bp, Lp, _ = x_prompt.shape
    Bs = x_sample.shape[0]
    Tp = Bp * Lp
    wts = _prep_weights(g_mix[0], w_in[0], w_pool[0], pool_scale[0], ret_gn[0], w_out[0], g_ffn[0], w_grp[0], w_exp[0])

    T_all = Tp + Bs
    cap = (-(-T_all // MOE_BLOCK) + 1) * MOE_BLOCK
    cap_blocks = cap // MOE_BLOCK

    x1p, dst_rows_p, rwp, npool_p, nret_p, counts_p, h2p = _mixer_prompt(x_prompt, wts, cap)
    xs = _sc_scatter(h2p, dst_rows_p, N_EXPERTS * cap)
    x1s, dst_s, rws, npool_s, nret_s, counts, h2s = _mixer_sample(
        x_sample.reshape(Bs, D_MODEL), jnp.swapaxes(state_pool[0], 0, 1), state_ret[0], wts, PAST_LEN, counts_p, cap)

    counts = counts[0, :N_EXPERTS].astype(jnp.int32)
    tile_p, tile_s = _tile_for(Tp), _tile_for(Bs)
    slots_s = _chunked_slots(dst_s[:, :2], tile_s)
    zstart = jnp.arange(N_EXPERTS, dtype=jnp.int32) * cap + counts
    zchunks = ((-counts) % MOE_BLOCK + ZERO_CHUNK - 1) // ZERO_CHUNK
    xs = _finish_dispatch(xs, h2s, slots_s, zstart, zchunks.astype(jnp.int32))
    n_blocks = -(-2 * T_all // MOE_BLOCK) + N_EXPERTS
    ys = _ffn(xs, counts, w1[0], w3[0], w2[0], cap_blocks, n_blocks)
    tile_d = DENSE_TILE if Tp % DENSE_TILE == 0 else tile_p
    n_tiles_d = Tp // tile_d
    chunk_tiles = [n_tiles_d // SC_GATHER_CHUNKS + (k < n_tiles_d % SC_GATHER_CHUNKS)
                   for k in range(SC_GATHER_CHUNKS)]
    gathered, first = [], 0
    for nt in chunk_tiles:
        if nt:
            gathered.append((first, _sc_gather(ys, dst_rows_p[:, first * tile_d:(first + nt) * tile_d])))
            first += nt
    y_s = _combine(x1s, rws, slots_s, ys, g_final, tile_s, 1)
    y_p = None
    for first, rows in gathered:
        y_p = _combine_dense(x1p, rwp, rows, y_p, g_final, tile_d, first)

    return (y_p.reshape(Bp, Lp, D_MODEL), y_s.reshape(Bs, 1, D_MODEL),
            npool_p[None], nret_p[None], jnp.swapaxes(npool_s, 0, 1)[None], nret_s[None])
```

```python
import dataclasses
import functools

import jax
import jax.numpy as jnp
import numpy as np
from jax import lax
from jax.experimental import pallas as pl
from jax.experimental.pallas import tpu as pltpu
from jax.experimental.pallas import tpu_sc as plsc

D_MODEL = 1024
EPS = 1e-6
POOL_GROUPS = 4
POOL_IN = D_MODEL // 2
POOL_GC = POOL_IN // POOL_GROUPS
POOL_OUT_GC = D_MODEL // POOL_GROUPS
POOL_WINDOWS = (2, 4, 8, 16)
POOL_HIST = max(POOL_WINDOWS) - 1
RET_HEADS = 4
RET_DK = D_MODEL // 8
RET_DV = D_MODEL // RET_HEADS
ROPE_BASE = 10000.0
PAST_LEN = 16384
N_GROUPS = 4
EXPERTS_PER_GROUP = 8
N_EXPERTS = N_GROUPS * EXPERTS_PER_GROUP
D_EXPERT = D_MODEL // 4
QK_W = RET_HEADS * RET_DK
V_W = RET_HEADS * RET_DV
OFF_U = 0
OFF_Q = POOL_IN
OFF_K = OFF_Q + QK_W
OFF_V = OFF_K + QK_W
OFF_GA = OFF_V + V_W
OFF_GB = OFF_GA + D_MODEL
IN_WIDTH = OFF_GB + D_MODEL

LANES = 128
SUBLANES = 8
ROW_WORDS = D_MODEL // 2
ROW_TILE = (ROW_WORDS // LANES, LANES)
ROW_DTYPE = jnp.uint32
HALO = 16
RET_CHUNK = 256
MIXER_SEQS = 2
MOE_BLOCK = 512
ROUTER_W = LANES
RANK_GROUP = 128
VMEM_LIMIT = 56 * 1024 * 1024

_BF = jnp.bfloat16
_F32 = jnp.float32


def _rms(x, g):
    inv = lax.rsqrt(jnp.mean(x * x, axis=-1, keepdims=True) + EPS)
    return x * inv * g


def _dot(a, b):
    return jnp.dot(a, b, preferred_element_type=_F32)


def _round_bf16(x):
    return x.astype(_BF).astype(_F32)


def _pack_rows(x):
    lo = lax.bitcast_convert_type(x[:, :ROW_WORDS].astype(_BF).astype(_F32), jnp.uint32)
    hi = lax.bitcast_convert_type(x[:, ROW_WORDS:].astype(_BF).astype(_F32), jnp.uint32)
    return ((lo >> 16) | hi).reshape((x.shape[0],) + ROW_TILE)


def _unpack_rows(w):
    w = w.reshape(w.shape[0], ROW_WORDS)
    lo = lax.bitcast_convert_type(w << 16, _F32)
    hi = lax.bitcast_convert_type(w & jnp.uint32(0xFFFF0000), _F32)
    return jnp.concatenate([lo, hi], axis=1)


def _rotary(x, cos2, sin2):
    return x * cos2 + pltpu.roll(x, RET_DK // 2, 1) * sin2


def _route(logits):
    lane = lax.broadcasted_iota(jnp.int32, logits.shape, 1).astype(_F32)
    neg = jnp.float32(-jnp.inf)
    big = jnp.float32(1 << 20)
    lg = jnp.where(lane < N_GROUPS, logits, neg)
    mg = jnp.max(lg, axis=-1, keepdims=True)
    g_idx = jnp.min(jnp.where(lg == mg, lane, big), axis=-1, keepdims=True)
    p_g = 1.0 / jnp.sum(jnp.exp(lg - mg), axis=-1, keepdims=True)
    lo = N_GROUPS + g_idx * EXPERTS_PER_GROUP
    in_grp = (lane >= lo) & (lane < lo + EXPERTS_PER_GROUP)
    le = jnp.where(in_grp, logits, neg)
    m1 = jnp.max(le, axis=-1, keepdims=True)
    i1 = jnp.min(jnp.where(le == m1, lane, big), axis=-1, keepdims=True)
    le2 = jnp.where(lane == i1, neg, le)
    m2 = jnp.max(le2, axis=-1, keepdims=True)
    i2 = jnp.min(jnp.where(le2 == m2, lane, big), axis=-1, keepdims=True)
    t = jnp.exp(m2 - m1)
    den = 1.0 + t
    e0 = (i1 - N_GROUPS).astype(jnp.int32)
    e1 = (i2 - N_GROUPS).astype(jnp.int32)
    return e0, e1, p_g * (1.0 / den), p_g * (t / den)


def _post_mix(x, mixed_ref, w_out_ref, g_ffn_ref, w_rt_ref, x1_ref, rw_ref):
    x1 = x + _dot(mixed_ref[...], w_out_ref[...])
    x1_ref[...] = x1.reshape(x1_ref.shape)
    h2 = _rms(x1, g_ffn_ref[...])
    e0, e1, w0, w1 = _route(_dot(h2.astype(_BF), w_rt_ref[...]))
    lane = lax.broadcasted_iota(jnp.int32, (x.shape[0], LANES), 1)
    rw_ref[...] = jnp.where(lane == 0, w0, jnp.where(lane == 1, w1, 0.0)).reshape(rw_ref.shape)
    return h2, e0, e1


def _route_block(x1, g_ffn_ref, w_rt_ref, carry_scr, cap, valid, h2_ref, rw_ref, dst_t_ref):
    for _ in _route_stages(x1, g_ffn_ref, w_rt_ref, carry_scr, cap, valid, h2_ref, rw_ref, dst_t_ref):
        pass


def _route_stages(x1, g_ffn_ref, w_rt_ref, carry_scr, cap, valid, h2_ref, rw_ref, dst_t_ref):
    h2 = _rms(x1, g_ffn_ref[...])
    logits = _dot(h2.astype(_BF), w_rt_ref[...])
    h2_ref[...] = _pack_rows(h2).reshape(h2_ref.shape)
    yield
    e0, e1, w0, w1 = _route(logits)
    lane = lax.broadcasted_iota(jnp.int32, (x1.shape[0], LANES), 1)
    rw_ref[...] = jnp.where(lane == 0, w0, jnp.where(lane == 1, w1, 0.0)).reshape(rw_ref.shape)
    yield
    dst = _rank_rows(e0, e1, cap, carry_scr, valid)
    dst_t = dst.T[0:SUBLANES, :].astype(jnp.int32)
    n_parts, _, width = dst_t_ref.shape
    for part in range(n_parts):
        dst_t_ref[part] = dst_t[:, part * width:(part + 1) * width]
    yield


def _rank_rows(e0, e1, cap, carry_scr, valid=None):
    R = e0.shape[0]
    lane = lax.broadcasted_iota(jnp.int32, (R, LANES), 1)
    m0 = lane == e0
    m1 = lane == e1
    onehot = jnp.where(m0 | m1, 1.0, 0.0)
    G = min(RANK_GROUP, R)
    r_i = lax.broadcasted_iota(jnp.int32, (G, G), 0)
    c_i = lax.broadcasted_iota(jnp.int32, (G, G), 1)
    tri = jnp.where(c_i < r_i, 1.0, 0.0).astype(_BF)
    running = carry_scr[...]
    parts = []
    for g in range(R // G):
        grp = onehot[g * G:(g + 1) * G]
        parts.append(_dot(tri, grp.astype(_BF)) + running)
        running = running + jnp.sum(grp, axis=0, keepdims=True)
    before = jnp.concatenate(parts, axis=0)
    d0 = e0.astype(_F32) * cap + jnp.sum(jnp.where(m0, before, 0.0), axis=-1, keepdims=True)
    d1 = e1.astype(_F32) * cap + jnp.sum(jnp.where(m1, before, 0.0), axis=-1, keepdims=True)
    added = running - carry_scr[...]
    carry_scr[...] += added if valid is None else added * valid
    return jnp.where(lane == 0, d0, jnp.where(lane == 1, d1, 0.0))


def _group_norm(o, gain):
    mu = jnp.mean(o, axis=-1, keepdims=True)
    d = o - mu
    var = jnp.mean(d * d, axis=-1, keepdims=True)
    return d * lax.rsqrt(var + EPS) * gain


def _mixer_prompt_kernel(x_ref, g_mix_ref, w_in_ref, w_pool_ref, pscale_ref, gn_ref, w_out_ref, g_ffn_ref,
                         w_rt_ref, cos_ref, sin_ref, dmask_ref, xi_ref, zeta_ref, gc_ref,
                         x1_ref, dst_t_ref, rw_ref, npool_ref, nret_ref, cnt_ref, h2_ref,
                         u_scr, s_scr, mixed_scr, carry_scr, z_scr, x1_prev_scr, *, cap):
    c = pl.program_id(1)
    n_c = pl.num_programs(1)
    step = pl.program_id(0) * n_c + c
    NSEQ, C, _ = x_ref.shape
    R = NSEQ * C

    @pl.when(step == 0)
    def _():
        carry_scr[...] = jnp.zeros(carry_scr.shape, _F32)
        x1_prev_scr[...] = jnp.zeros(x1_prev_scr.shape, _F32)

    @pl.when(c == 0)
    def _():
        u_scr[:, 0:HALO, :] = jnp.zeros((NSEQ, HALO, POOL_IN), _F32)
        s_scr[...] = jnp.zeros(s_scr.shape, _F32)

    x = x_ref[...].reshape(R, D_MODEL)
    h = _rms(x, g_mix_ref[...]).astype(_BF)

    stages = _route_stages(x1_prev_scr[...], g_ffn_ref, w_rt_ref, carry_scr, cap, (step > 0).astype(_F32),
                           h2_ref, rw_ref, dst_t_ref)
    for lo, hi in ((0, OFF_V), (OFF_V, OFF_GA), (OFF_GA, OFF_GB), (OFF_GB, IN_WIDTH)):
        z_scr[:, lo:hi] = _dot(h, w_in_ref[:, lo:hi])
        next(stages, None)
    cnt_ref[...] = carry_scr[...]

    u_scr[:, HALO:HALO + C, :] = z_scr[:, OFF_U:OFF_U + POOL_IN].reshape(NSEQ, C, POOL_IN)
    q = z_scr[:, OFF_Q:OFF_Q + QK_W]
    k = z_scr[:, OFF_K:OFF_K + QK_W]
    cos2 = jnp.concatenate([cos_ref[...]] * NSEQ, axis=0)
    sin2 = jnp.concatenate([sin_ref[...]] * NSEQ, axis=0)
    pos1 = (c * C + 1 + lax.broadcasted_iota(jnp.int32, (C, POOL_GC), 0)).astype(_F32)

    for j in range(RET_HEADS):
        win = POOL_WINDOWS[j]
        cs = slice(j * POOL_GC, (j + 1) * POOL_GC)
        n_rows = jnp.minimum(pos1, jnp.float32(win))
        pooled = []
        for sq in range(NSEQ):
            u_j = u_scr[sq, HALO:HALO + C, cs]
            s = u_j
            for d in range(1, win):
                s = s + u_scr[sq, HALO - d:HALO - d + C, cs]
            pooled.append(s / n_rows - u_j)
        osl = slice(j * POOL_OUT_GC, (j + 1) * POOL_OUT_GC)
        pool_out = _dot(jnp.concatenate(pooled, axis=0).astype(_BF), w_pool_ref[j]) * pscale_ref[:, osl]

        qs = slice(j * RET_DK, (j + 1) * RET_DK)
        qb = _rotary(q[:, qs], cos2, sin2).astype(_BF)
        kf = _rotary(k[:, qs], cos2, sin2) * (RET_DK ** -0.5)
        kb = kf.astype(_BF)
        vb = z_scr[:, OFF_V + j * RET_DV:OFF_V + (j + 1) * RET_DV].astype(_BF)
        ret = []
        for sq in range(NSEQ):
            rows = slice(sq * C, (sq + 1) * C)
            scores = lax.dot_general(qb[rows], kb[rows], (((1,), (1,)), ((), ())),
                                     preferred_element_type=_F32) * dmask_ref[j]
            s_old = s_scr[sq, j]
            o = _dot(scores.astype(_BF), vb[rows]) + _dot(qb[rows], s_old.astype(_BF)) * xi_ref[j]
            kz = (kf[rows] * zeta_ref[j]).astype(_BF)
            s_scr[sq, j] = gc_ref[j] * s_old + lax.dot_general(kz, vb[rows], (((0,), (0,)), ((), ())),
                                                               preferred_element_type=_F32)
            ret.append(_group_norm(o, gn_ref[:, osl]))
        ret_out = jnp.concatenate(ret, axis=0)

        ga = z_scr[:, OFF_GA + j * RET_DV:OFF_GA + (j + 1) * RET_DV]
        gb = z_scr[:, OFF_GB + j * RET_DV:OFF_GB + (j + 1) * RET_DV]
        mixed_scr[:, osl] = (jax.nn.sigmoid(ga) * pool_out + jax.nn.sigmoid(gb) * ret_out).astype(_BF)

    x1 = x + _dot(mixed_scr[...], w_out_ref[...])
    x1_ref[...] = x1.reshape(x1_ref.shape)
    x1_prev_scr[...] = x1

    u_scr[:, 0:HALO, :] = u_scr[:, C:C + HALO, :]

    @pl.when(c == n_c - 1)
    def _():
        npool_ref[...] = u_scr[:, HALO + C - POOL_HIST:HALO + C, :]
        nret_ref[...] = s_scr[...]


def _decay_tables(C):
    f32 = np.float32
    log_g = np.log(f32(1.0) - np.exp2(f32(-5.0) - np.arange(RET_HEADS, dtype=f32)))
    i = np.arange(C, dtype=f32)
    diff = i[:, None] - i[None, :]
    dmask = np.where(diff >= 0, np.exp(np.maximum(diff, f32(0.0))[None] * log_g[:, None, None]), f32(0.0))
    xi = np.exp((i[None, :] + f32(1.0)) * log_g[:, None])
    zeta = np.exp((f32(C) - f32(1.0) - i)[None, :] * log_g[:, None])
    g_chunk = np.exp(f32(C) * log_g)
    return dmask.astype(f32), xi.astype(f32), zeta.astype(f32), g_chunk.astype(f32)


def _rope_tables(pos):
    half = RET_DK // 2
    freqs = ROPE_BASE ** (-jnp.arange(half, dtype=_F32) / half)
    ang = pos[:, None] * freqs[None, :]
    cos, sin = jnp.cos(ang), jnp.sin(ang)
    return jnp.concatenate([cos, cos], axis=-1), jnp.concatenate([-sin, sin], axis=-1)


def _const(a, shape=None):
    if shape is not None:
        a = np.ascontiguousarray(np.broadcast_to(a, shape))
    return jnp.asarray(a)


def _full(shape):
    n = len(shape)
    return pl.BlockSpec(shape, lambda *_: (0,) * n)


def _route_tail_kernel(x1_ref, g_ffn_ref, w_rt_ref, cnt_in_ref, dst_t_in, rw_in, h2_in,
                       dst_t_ref, rw_ref, cnt_ref, h2_ref, carry_scr, *, cap):
    del dst_t_in, rw_in, h2_in
    carry_scr[...] = cnt_in_ref[...]
    x1 = x1_ref[...]
    _route_block(x1.reshape(x1.shape[0] * x1.shape[1], D_MODEL), g_ffn_ref, w_rt_ref, carry_scr, cap, None,
                 h2_ref, rw_ref, dst_t_ref)
    cnt_ref[...] = carry_scr[...]


def _route_tail(x1, dst_t, rw, counts, h2, wts, cap, C, last):
    NSEQ = x1.shape[0]
    any_spec = pl.BlockSpec(memory_space=pl.ANY)
    return pl.pallas_call(
        functools.partial(_route_tail_kernel, cap=cap),
        grid=(1,),
        in_specs=[pl.BlockSpec((NSEQ, C, D_MODEL), lambda i: (0, last, 0)), _full((1, D_MODEL)),
                  _full((D_MODEL, ROUTER_W)), _full((1, LANES)), any_spec, any_spec, any_spec],
        out_specs=(pl.BlockSpec((NSEQ, SUBLANES, C), lambda i: (0, 0, last)),
                   pl.BlockSpec((NSEQ, C, LANES), lambda i: (0, last, 0)),
                   _full((1, LANES)),
                   pl.BlockSpec((NSEQ, C) + ROW_TILE, lambda i: (0, last, 0, 0))),
        out_shape=(jax.ShapeDtypeStruct(dst_t.shape, dst_t.dtype), jax.ShapeDtypeStruct(rw.shape, rw.dtype),
                   jax.ShapeDtypeStruct((1, LANES), _F32), jax.ShapeDtypeStruct(h2.shape, h2.dtype)),
        input_output_aliases={4: 0, 5: 1, 6: 3},
        scratch_shapes=[pltpu.VMEM((1, LANES), _F32)],
        compiler_params=pltpu.CompilerParams(dimension_semantics=("arbitrary",), vmem_limit_bytes=VMEM_LIMIT),
        name="mixer_route_tail",
    )(x1, wts["g_ffn"], wts["w_rt"], counts, dst_t, rw, h2)


def _mixer_prompt(x, wts, cap):
    B, L, _ = x.shape
    C = RET_CHUNK if L % RET_CHUNK == 0 else L
    n_c = L // C
    T = B * L
    dmask, xi, zeta, g_chunk = _decay_tables(C)
    dmask = _const(dmask)
    xi_b = _const(xi[:, :, None], (RET_HEADS, C, RET_DV))
    zeta_b = _const(zeta[:, :, None], (RET_HEADS, C, RET_DK))
    gc_b = _const(g_chunk[:, None, None], (RET_HEADS, 1, RET_DV))
    cos2, sin2 = _rope_tables(jnp.arange(L).astype(_F32))

    NSEQ = MIXER_SEQS if B % MIXER_SEQS == 0 else 1
    B2 = B // NSEQ
    T2 = T // NSEQ
    x = x.reshape(NSEQ, B2, L, D_MODEL)
    tok = lambda b, c: (0, b * n_c + c, 0)
    in_specs = [
        pl.BlockSpec((NSEQ, None, C, D_MODEL), lambda b, c: (0, b, c, 0)),
        _full((1, D_MODEL)), _full((D_MODEL, IN_WIDTH)), _full((POOL_GROUPS, POOL_GC, POOL_OUT_GC)),
        _full((1, D_MODEL)), _full((1, D_MODEL)), _full((D_MODEL, D_MODEL)), _full((1, D_MODEL)),
        _full((D_MODEL, ROUTER_W)),
        pl.BlockSpec((C, RET_DK), lambda b, c: (c, 0)), pl.BlockSpec((C, RET_DK), lambda b, c: (c, 0)),
        _full((RET_HEADS, C, C)), _full((RET_HEADS, C, RET_DV)), _full((RET_HEADS, C, RET_DK)),
        _full((RET_HEADS, 1, RET_DV)),
    ]
    R = NSEQ * C
    out_shape = (
        jax.ShapeDtypeStruct((NSEQ, T2, D_MODEL), _F32),
        jax.ShapeDtypeStruct((NSEQ, SUBLANES, T2), jnp.int32),
        jax.ShapeDtypeStruct((NSEQ, T2, LANES), _F32),
        jax.ShapeDtypeStruct((NSEQ, B2, POOL_HIST, POOL_IN), _F32),
        jax.ShapeDtypeStruct((NSEQ, B2, RET_HEADS, RET_DK, RET_DV), _F32),
        jax.ShapeDtypeStruct((1, LANES), _F32),
        jax.ShapeDtypeStruct((NSEQ, T2) + ROW_TILE, ROW_DTYPE),
    )
    prev = lambda b, c: jnp.maximum(b * n_c + c - 1, 0)
    out_specs = (
        pl.BlockSpec((NSEQ, C, D_MODEL), tok),
        pl.BlockSpec((NSEQ, SUBLANES, C), lambda b, c: (0, 0, prev(b, c))),
        pl.BlockSpec((NSEQ, C, LANES), lambda b, c: (0, prev(b, c), 0)),
        pl.BlockSpec((NSEQ, None, POOL_HIST, POOL_IN), lambda b, c: (0, b, 0, 0)),
        pl.BlockSpec((NSEQ, None, RET_HEADS, RET_DK, RET_DV), lambda b, c: (0, b, 0, 0, 0)),
        _full((1, LANES)),
        pl.BlockSpec((NSEQ, C) + ROW_TILE, lambda b, c: (0, prev(b, c), 0, 0)),
    )
    x1, dst_t, rw, npool, nret, counts, h2 = pl.pallas_call(
        functools.partial(_mixer_prompt_kernel, cap=cap),
        grid=(B2, n_c),
        in_specs=in_specs,
        out_specs=out_specs,
        out_shape=out_shape,
        scratch_shapes=[pltpu.VMEM((NSEQ, HALO + C, POOL_IN), _F32),
                        pltpu.VMEM((NSEQ, RET_HEADS, RET_DK, RET_DV), _F32),
                        pltpu.VMEM((R, D_MODEL), _BF),
                        pltpu.VMEM((1, LANES), _F32),
                        pltpu.VMEM((R, IN_WIDTH), _F32),
                        pltpu.VMEM((R, D_MODEL), _F32)],
        compiler_params=pltpu.CompilerParams(dimension_semantics=("arbitrary", "arbitrary"),
                                             vmem_limit_bytes=VMEM_LIMIT),
        name="mixer_prompt",
    )(x, wts["g_mix"], wts["w_in"], wts["w_pool"], wts["pool_scale"], wts["ret_gn"], wts["w_out"],
      wts["g_ffn"], wts["w_rt"], cos2, sin2, dmask, xi_b, zeta_b, gc_b)
    dst_t, rw, counts, h2 = _route_tail(x1, dst_t, rw, counts, h2, wts, cap, C, B2 * n_c - 1)
    dst_rows = jnp.moveaxis(dst_t[:, 0:2, :], 1, 0).reshape(2, T)
    return (x1.reshape(T, D_MODEL), dst_rows, rw.reshape(T, LANES),
            npool.reshape(B, POOL_HIST, POOL_IN), nret.reshape(B, RET_HEADS, RET_DK, RET_DV), counts,
            h2.reshape((T,) + ROW_TILE))


def _prep_weights(g_mix, w_in, w_pool, pool_scale, ret_gn, w_out, g_ffn, w_grp, w_exp):
    w_rt = jnp.concatenate([w_grp, w_exp.reshape(D_MODEL, N_EXPERTS)], axis=1)
    w_rt = jnp.pad(w_rt, ((0, 0), (0, ROUTER_W - w_rt.shape[1])))
    row = lambda v: v.reshape(1, D_MODEL)
    return dict(g_mix=row(g_mix), w_in=w_in.astype(_BF), w_pool=w_pool.astype(_BF), pool_scale=row(pool_scale),
                ret_gn=row(ret_gn), w_out=w_out.astype(_BF), g_ffn=row(g_ffn), w_rt=w_rt.astype(_BF))


SAMPLE_TB = 8


def _sample_proj_kernel(x_ref, g_mix_ref, w_in_ref, cos_ref, sin_ref, zeta_ref,
                        u_ref, q_ref, k_ref, kz_ref, v_ref, ga_ref, gb_ref):
    h = _rms(x_ref[...], g_mix_ref[...]).astype(_BF)
    u_ref[...] = _dot(h, w_in_ref[:, OFF_U:OFF_U + POOL_IN])
    q = _dot(h, w_in_ref[:, OFF_Q:OFF_Q + QK_W])
    k = _dot(h, w_in_ref[:, OFF_K:OFF_K + QK_W])
    for j in range(RET_HEADS):
        qs = slice(j * RET_DK, (j + 1) * RET_DK)
        kf = _rotary(k[:, qs], cos_ref[...], sin_ref[...]) * (RET_DK ** -0.5)
        q_ref[:, qs] = _rotary(q[:, qs], cos_ref[...], sin_ref[...])
        k_ref[:, qs] = kf
        kz_ref[:, qs] = kf * zeta_ref[j]
    v_ref[...] = _dot(h, w_in_ref[:, OFF_V:OFF_V + V_W])
    ga_ref[...] = _dot(h, w_in_ref[:, OFF_GA:OFF_GA + D_MODEL])
    gb_ref[...] = _dot(h, w_in_ref[:, OFF_GB:OFF_GB + D_MODEL])


def _sample_proj(x, wts, pos0):
    Bs = x.shape[0]
    _, _, zeta, _ = _decay_tables(1)
    cos2, sin2 = _rope_tables((pos0 + jnp.arange(1)).astype(_F32))
    widths = (POOL_IN, QK_W, QK_W, QK_W, V_W, D_MODEL, D_MODEL)
    return pl.pallas_call(
        _sample_proj_kernel,
        grid=(1,),
        in_specs=[_full((Bs, D_MODEL)), _full((1, D_MODEL)), _full((D_MODEL, IN_WIDTH)),
                  _full((1, RET_DK)), _full((1, RET_DK)), _full((RET_HEADS, 1, 1))],
        out_specs=tuple(_full((Bs, w)) for w in widths),
        out_shape=tuple(jax.ShapeDtypeStruct((Bs, w), _F32) for w in widths),
        compiler_params=pltpu.CompilerParams(dimension_semantics=("arbitrary",), vmem_limit_bytes=VMEM_LIMIT),
        name="sample_proj",
    )(x, wts["g_mix"], wts["w_in"], cos2, sin2, _const(zeta[:, :, None], (RET_HEADS, 1, 1)))


def _mixer_sample_kernel(x_ref, u_ref, q_ref, k_ref, v_ref, ga_ref, gb_ref, spool_ref, sret_ref,
                         w_pool_ref, pscale_ref, gn_ref, w_out_ref, g_ffn_ref, w_rt_ref, dm_ref, xi_ref,
                         carry_in_ref,
                         x1_ref, dst_ref, rw_ref, npool_ref, cnt_ref, h2_ref,
                         qt_scr, pooled_scr, o_scr, mixed_scr, carry_scr, *, cap):
    t = pl.program_id(0)
    n_t = pl.num_programs(0)
    Bs = x_ref.shape[0]
    TB = sret_ref.shape[0]

    @pl.when(t == 0)
    def _():
        for j in range(RET_HEADS):
            qt_scr[j] = _round_bf16(q_ref[:, j * RET_DK:(j + 1) * RET_DK]).T

    shift = (Bs - t * TB) % Bs
    qt = [pltpu.roll(qt_scr[j], shift, 1) for j in range(RET_HEADS)]

    blk = pl.ds(pl.multiple_of(t * TB, TB), TB)
    u_blk = u_ref[blk, :]
    q_blk = q_ref[blk, :]
    k_blk = k_ref[blk, :]
    v_blk = v_ref[blk, :]
    score = [jnp.sum(q_blk[:, j * RET_DK:(j + 1) * RET_DK] * k_blk[:, j * RET_DK:(j + 1) * RET_DK],
                     axis=-1, keepdims=True) * dm_ref[j] for j in range(RET_HEADS)]

    groups = []
    for g, win in enumerate(POOL_WINDOWS):
        cs = slice(g * POOL_GC, (g + 1) * POOL_GC)
        s = u_blk[:, cs]
        for r in range(POOL_HIST - (win - 1), POOL_HIST):
            s = s + spool_ref[r, :, cs]
        groups.append(s / jnp.float32(win) - u_blk[:, cs])
    pooled_scr[blk, :] = jnp.concatenate(groups, axis=1)
    npool_ref[0:POOL_HIST - 1] = spool_ref[1:POOL_HIST]
    npool_ref[POOL_HIST - 1] = u_blk

    o_rows = []
    for i in range(TB):
        heads = []
        for j in range(RET_HEADS):
            v_row = v_blk[i:i + 1, j * RET_DV:(j + 1) * RET_DV]
            qcol = qt[j][:, i:i + 1]
            qs_old = jnp.sum(qcol * _round_bf16(sret_ref[i, j]), axis=0, keepdims=True)
            heads.append(score[j][i:i + 1, :] * v_row + qs_old * xi_ref[j])
        o_rows.append(jnp.concatenate(heads, axis=1))
    o_scr[blk, :] = jnp.concatenate(o_rows, axis=0)

    @pl.when(t == n_t - 1)
    def _():
        for j in range(RET_HEADS):
            cs = slice(j * POOL_GC, (j + 1) * POOL_GC)
            osl = slice(j * POOL_OUT_GC, (j + 1) * POOL_OUT_GC)
            pool_out = _dot(pooled_scr[:, cs].astype(_BF), w_pool_ref[j]) * pscale_ref[:, osl]
            ret_out = _group_norm(o_scr[:, osl], gn_ref[:, osl])
            mixed_scr[:, osl] = (jax.nn.sigmoid(ga_ref[:, osl]) * pool_out
                                 + jax.nn.sigmoid(gb_ref[:, osl]) * ret_out).astype(_BF)
        h2, e0, e1 = _post_mix(x_ref[...], mixed_scr, w_out_ref, g_ffn_ref, w_rt_ref, x1_ref, rw_ref)
        h2_ref[...] = _pack_rows(h2).reshape(h2_ref.shape)
        carry_scr[...] = carry_in_ref[...]
        dst_ref[...] = _rank_rows(e0, e1, cap, carry_scr).astype(jnp.int32)
        cnt_ref[...] = carry_scr[...]


def _mixer_sample(x, proj, state_pool, state_ret, wts, carry_in, cap):
    Bs = x.shape[0]
    assert Bs == LANES and Bs % SAMPLE_TB == 0
    TB = SAMPLE_TB
    u, q, k, v, ga, gb = proj
    dmask, xi, _, _ = _decay_tables(1)
    dm_b = _const(dmask, (RET_HEADS, 1, 1))
    xi_b = _const(xi[:, :, None], (RET_HEADS, 1, RET_DV))

    in_specs = [
        _full((Bs, D_MODEL)),
        _full((Bs, POOL_IN)), _full((Bs, QK_W)), _full((Bs, QK_W)), _full((Bs, V_W)),
        _full((Bs, D_MODEL)), _full((Bs, D_MODEL)),
        pl.BlockSpec((POOL_HIST, TB, POOL_IN), lambda t: (0, t, 0)),
        pl.BlockSpec((TB, RET_HEADS, RET_DK, RET_DV), lambda t: (t, 0, 0, 0)),
        _full((POOL_GROUPS, POOL_GC, POOL_OUT_GC)),
        _full((1, D_MODEL)), _full((1, D_MODEL)), _full((D_MODEL, D_MODEL)), _full((1, D_MODEL)),
        _full((D_MODEL, ROUTER_W)),
        _full((RET_HEADS, 1, 1)), _full((RET_HEADS, 1, RET_DV)),
        _full((1, LANES)),
    ]
    out_shape = (
        jax.ShapeDtypeStruct((Bs, D_MODEL), _F32),
        jax.ShapeDtypeStruct((Bs, LANES), jnp.int32),
        jax.ShapeDtypeStruct((Bs, LANES), _F32),
        jax.ShapeDtypeStruct((POOL_HIST, Bs, POOL_IN), _F32),
        jax.ShapeDtypeStruct((1, LANES), _F32),
        jax.ShapeDtypeStruct((Bs,) + ROW_TILE, ROW_DTYPE),
    )
    out_specs = (
        _full((Bs, D_MODEL)), _full((Bs, LANES)), _full((Bs, LANES)),
        pl.BlockSpec((POOL_HIST, TB, POOL_IN), lambda t: (0, t, 0)),
        _full((1, LANES)), _full((Bs,) + ROW_TILE),
    )
    f32 = lambda *s: pltpu.VMEM(s, _F32)
    return pl.pallas_call(
        functools.partial(_mixer_sample_kernel, cap=cap),
        grid=(Bs // TB,),
        in_specs=in_specs,
        out_specs=out_specs,
        out_shape=out_shape,
        scratch_shapes=[f32(RET_HEADS, RET_DK, Bs), f32(Bs, POOL_IN), f32(Bs, V_W),
                        pltpu.VMEM((Bs, D_MODEL), _BF), f32(1, LANES)],
        compiler_params=pltpu.CompilerParams(dimension_semantics=("arbitrary",), vmem_limit_bytes=VMEM_LIMIT),
        name="mixer_sample",
    )(x, u, q, k, v, ga, gb, state_pool, state_ret, wts["w_pool"], wts["pool_scale"], wts["ret_gn"],
      wts["w_out"], wts["g_ffn"], wts["w_rt"], dm_b, xi_b, carry_in)


SC_CORES = 2
SC_SUBCORES = 16
SC_LANES = 16
STATE_ROWS = 64
STATE_UNROLL = LANES // SC_LANES


def _sc_state_update(s0, kz, v):
    P = s0.shape[0]
    n_workers = SC_CORES * SC_SUBCORES
    assert P % n_workers == 0 and RET_DK % (2 * STATE_ROWS) == 0 and STATE_ROWS % STATE_UNROLL == 0
    per_w = P // n_workers
    R = STATE_ROWS
    n_parts = RET_DK // R
    n_vc = RET_DV // SC_LANES
    _, _, _, g_chunk = _decay_tables(1)
    decay = _const(g_chunk[:, None], (RET_HEADS, SC_LANES))
    kz_rep = jnp.broadcast_to(kz.reshape(P, RET_DK, 1), (P, RET_DK, SC_LANES)).reshape(P, RET_DK // STATE_UNROLL, LANES)
    mesh = plsc.VectorSubcoreMesh(core_axis_name="c", subcore_axis_name="s")

    def body(s0_hbm, k_hbm, v_hbm, g_hbm, out_hbm, in_v, out_v, k_v, v_v, g_v, sem_in, sem_out):
        wid = lax.axis_index("s") * SC_CORES + lax.axis_index("c")
        base = wid * per_w
        pltpu.sync_copy(k_hbm.at[pl.ds(base, per_w)], k_v)
        pltpu.sync_copy(v_hbm.at[pl.ds(base, per_w)], v_v)
        pltpu.sync_copy(g_hbm, g_v)

        def load(p, part, slot):
            return pltpu.make_async_copy(s0_hbm.at[p, pl.ds(part * R, R)], in_v.at[slot], sem_in.at[slot])

        def store(p, part, slot):
            return pltpu.make_async_copy(out_v.at[slot], out_hbm.at[p, pl.ds(part * R, R)], sem_out.at[slot])

        load(base, 0, 0).start()

        def pair(pp, carry):
            p = base + pp
            g = g_v[p % RET_HEADS, :]
            vs = [v_v[pp, pl.ds(c * SC_LANES, SC_LANES)] for c in range(n_vc)]
            for part in range(n_parts):
                slot = part % 2
                load(p, part, slot).wait()
                if part + 1 < n_parts:
                    load(p, part + 1, 1 - slot).start()
                else:
                    @pl.when(pp + 1 < per_w)
                    def _():
                        load(p + 1, 0, 1 - slot).start()

                @pl.when(pp > 0)
                def _():
                    store(p, part, slot).wait()

                def rows(i8, c2):
                    for j in range(STATE_UNROLL):
                        i = i8 * STATE_UNROLL + j
                        ki = k_v[pp, part * (R // STATE_UNROLL) + i8, pl.ds(j * SC_LANES, SC_LANES)]
                        for c in range(n_vc):
                            cs = pl.ds(c * SC_LANES, SC_LANES)
                            out_v[slot, i, cs] = g * in_v[slot, i, cs] + ki * vs[c]
                    return c2

                lax.fori_loop(0, R // STATE_UNROLL, rows, 0)
                store(p, part, slot).start()
            return carry

        lax.fori_loop(0, per_w, pair, 0)
        for slot in range(2):
            store(base, slot, slot).wait()

    return pl.kernel(
        body, mesh=mesh,
        out_type=jax.ShapeDtypeStruct(s0.shape, _F32),
        scratch_types=[pltpu.VMEM((2, R, RET_DV), _F32), pltpu.VMEM((2, R, RET_DV), _F32),
                       pltpu.VMEM((per_w, RET_DK // STATE_UNROLL, LANES), _F32), pltpu.VMEM((per_w, RET_DV), _F32),
                       pltpu.VMEM((RET_HEADS, SC_LANES), _F32),
                       pltpu.SemaphoreType.DMA((2,)), pltpu.SemaphoreType.DMA((2,))],
        compiler_params=dataclasses.replace(pltpu.CompilerParams(), needs_layout_passes=False),
        name="sample_state_update",
    )(s0, kz_rep, v, decay)


IDX_CHUNK = 1024
ISSUE_UNROLL = 8
SC_WINDOW = 64


def _sc_scatter(h2, dst_rows, n_rows):
    T = h2.shape[0]
    n_workers = SC_CORES * SC_SUBCORES
    W = SC_WINDOW
    assert T % (n_workers * W) == 0
    per_w = T // n_workers
    n_win = per_w // W
    d0 = dst_rows[0].reshape(T // W, W)
    d1 = dst_rows[1].reshape(T // W, W)
    mesh = plsc.VectorSubcoreMesh(core_axis_name="c", subcore_axis_name="s")

    def body(h2_hbm, d0_hbm, d1_hbm, xs_hbm, i0_v, i1_v, rows_v, sem_load, sem_store):
        wid = lax.axis_index("s") * SC_CORES + lax.axis_index("c")
        base = wid * per_w
        pltpu.sync_copy(d0_hbm.at[pl.ds(wid * n_win, n_win)], i0_v)
        pltpu.sync_copy(d1_hbm.at[pl.ds(wid * n_win, n_win)], i1_v)
        pltpu.async_copy(h2_hbm.at[pl.ds(base, W)], rows_v.at[0], sem_load)
        for i in range(n_win):
            b = i % 2
            pltpu.make_async_copy(h2_hbm.at[pl.ds(base, W)], rows_v.at[b], sem_load).wait()
            if i + 1 < n_win:
                pltpu.async_copy(h2_hbm.at[pl.ds(base + (i + 1) * W, W)], rows_v.at[1 - b], sem_load)
            c0 = pltpu.async_copy(rows_v.at[b], xs_hbm.at[i0_v.at[i]], sem_store)
            c1 = pltpu.async_copy(rows_v.at[b], xs_hbm.at[i1_v.at[i]], sem_store)
            c0.wait()
            c1.wait()

    return pl.kernel(
        body, mesh=mesh,
        out_type=jax.ShapeDtypeStruct((n_rows,) + ROW_TILE, ROW_DTYPE),
        scratch_types=[pltpu.VMEM((n_win, W), jnp.int32), pltpu.VMEM((n_win, W), jnp.int32),
                       pltpu.VMEM((2, W) + ROW_TILE, ROW_DTYPE),
                       pltpu.SemaphoreType.DMA, pltpu.SemaphoreType.DMA],
        name="moe_sc_scatter",
    )(h2, d0, d1)


ZERO_CHUNK = 64


def _finish_dispatch_kernel(zstart_ref, zchunks_ref, slots_hbm, h2_ref, xs_in_hbm, xs_hbm, idx_smem, zbuf, sem_idx,
                            sem_rows, sem_zero):
    del xs_in_hbm
    TS = h2_ref.shape[0]
    idx_cp = pltpu.make_async_copy(slots_hbm.at[pl.ds(0, IDX_CHUNK)], idx_smem, sem_idx)
    idx_cp.start()
    zbuf[...] = jnp.zeros(zbuf.shape, zbuf.dtype)

    def zero_copy(e, j):
        return pltpu.make_async_copy(zbuf, xs_hbm.at[pl.ds(zstart_ref[e] + j * ZERO_CHUNK, ZERO_CHUNK)], sem_zero)

    for e in range(N_EXPERTS):
        lax.fori_loop(0, zchunks_ref[e], lambda j, c, e=e: (zero_copy(e, j).start(), c)[1], 0)
    idx_cp.wait()

    def issue(r, carry):
        for kk in range(2):
            pltpu.make_async_copy(h2_ref.at[r], xs_hbm.at[idx_smem[2 * r + kk]], sem_rows).start(priority=kk)
        return carry

    lax.fori_loop(0, TS, issue, 0, unroll=ISSUE_UNROLL)
    for e in range(N_EXPERTS):
        lax.fori_loop(0, zchunks_ref[e], lambda j, c, e=e: (zero_copy(e, j).wait(), c)[1], 0)
    for kk in range(2):
        pltpu.make_async_copy(h2_ref, xs_hbm.at[pl.ds(0, TS)], sem_rows).wait()


def _finish_dispatch(xs, h2s, slots, zstart, zchunks):
    Bs = h2s.shape[0]
    assert 2 * Bs <= IDX_CHUNK and slots.shape[0] == IDX_CHUNK and MOE_BLOCK % ZERO_CHUNK == 0
    any_spec = pl.BlockSpec(memory_space=pl.ANY)
    grid_spec = pltpu.PrefetchScalarGridSpec(
        num_scalar_prefetch=2,
        grid=(1,),
        in_specs=[any_spec, pl.BlockSpec((Bs,) + ROW_TILE, lambda t, z, n: (0, 0, 0)), any_spec],
        out_specs=any_spec,
        scratch_shapes=[pltpu.SMEM((IDX_CHUNK,), jnp.int32), pltpu.VMEM((ZERO_CHUNK,) + ROW_TILE, ROW_DTYPE),
                        pltpu.SemaphoreType.DMA, pltpu.SemaphoreType.DMA, pltpu.SemaphoreType.DMA],
    )
    return pl.pallas_call(
        _finish_dispatch_kernel,
        grid_spec=grid_spec,
        out_shape=jax.ShapeDtypeStruct(xs.shape, ROW_DTYPE),
        input_output_aliases={4: 0},
        compiler_params=pltpu.CompilerParams(dimension_semantics=("arbitrary",), has_side_effects=True),
        name="moe_finish_dispatch",
    )(zstart, zchunks, slots, h2s, xs)


FFN_IN_BUFFERS = 4
FFN_OUT_BUFFERS = 3


def _ffn_kernel(cnt_ref, xs_hbm, w1_hbm, w3_hbm, w2_hbm, ys_hbm,
                xs_buf, ys_buf, st1, st3, st2, w13_scr, w2_scr, t_row, t_exp, t_len, sem_in, sem_out, sem_w,
                *, cap_blocks):
    B = MOE_BLOCK

    def fill_expert(e, g):
        nb = (cnt_ref[e] + (B - 1)) // B

        def fill_block(j, carry):
            t_row[g + j] = e * cap_blocks + j
            t_exp[g + j] = e
            t_len[g + j] = nb
            return carry

        lax.fori_loop(0, nb, fill_block, 0)
        return g + nb

    n_used = lax.fori_loop(0, N_EXPERTS, fill_expert, 0)

    def in_copy(g):
        s = g % FFN_IN_BUFFERS
        return pltpu.make_async_copy(xs_hbm.at[pl.ds(t_row[g] * B, B)], xs_buf.at[s], sem_in.at[s])

    def out_copy(g):
        s = g % FFN_OUT_BUFFERS
        return pltpu.make_async_copy(ys_buf.at[s], ys_hbm.at[pl.ds(t_row[g] * B, B)], sem_out.at[s])

    def weight_copies(e, s):
        return (pltpu.make_async_copy(w1_hbm.at[e], st1.at[s], sem_w.at[s]),
                pltpu.make_async_copy(w3_hbm.at[e], st3.at[s], sem_w.at[s]),
                pltpu.make_async_copy(w2_hbm.at[e], st2.at[s], sem_w.at[s]))

    @pl.when(n_used > 0)
    def _():
        for cp in weight_copies(t_exp[0], 0):
            cp.start()

    for g0 in range(FFN_IN_BUFFERS - 1):
        @pl.when(g0 < n_used)
        def _(g0=g0):
            in_copy(g0).start()

    def block(g, wslot):
        first = (g == 0) | (t_exp[g] != t_exp[jnp.maximum(g - 1, 0)])
        wslot = jnp.where(first & (g > 0), 1 - wslot, wslot)

        @pl.when(first)
        def _():
            for cp in weight_copies(t_exp[g], wslot):
                cp.wait()
            nxt = g + t_len[g]

            @pl.when(nxt < n_used)
            def _():
                for cp in weight_copies(t_exp[jnp.minimum(nxt, n_used - 1)], 1 - wslot):
                    cp.start()

            w13_scr[:, 0:D_EXPERT] = st1[wslot].astype(_BF)
            w13_scr[:, D_EXPERT:2 * D_EXPERT] = st3[wslot].astype(_BF)
            w2_scr[...] = st2[wslot].astype(_BF)

        in_copy(g).wait()

        @pl.when(g + FFN_IN_BUFFERS - 1 < n_used)
        def _():
            in_copy(g + FFN_IN_BUFFERS - 1).start()

        @pl.when(g >= FFN_OUT_BUFFERS)
        def _():
            out_copy(g - FFN_OUT_BUFFERS).wait()

        xb = _unpack_rows(xs_buf[g % FFN_IN_BUFFERS]).astype(_BF)
        ab = _dot(xb, w13_scr[...])
        hid = jax.nn.silu(ab[:, 0:D_EXPERT]) * ab[:, D_EXPERT:2 * D_EXPERT]
        ys_buf[g % FFN_OUT_BUFFERS] = _pack_rows(_dot(hid.astype(_BF), w2_scr[...]))
        out_copy(g).start()
        return wslot

    lax.fori_loop(0, n_used, block, 0)

    for back in range(FFN_OUT_BUFFERS, 0, -1):
        @pl.when(n_used >= back)
        def _(back=back):
            out_copy(n_used - back).wait()


def _ffn(xs, counts, w1, w3, w2, cap_blocks, n_blocks):
    any_spec = pl.BlockSpec(memory_space=pl.ANY)
    row_buf = lambda n: pltpu.VMEM((n, MOE_BLOCK) + ROW_TILE, ROW_DTYPE)
    grid_spec = pltpu.PrefetchScalarGridSpec(
        num_scalar_prefetch=1,
        grid=(1,),
        in_specs=[any_spec, any_spec, any_spec, any_spec],
        out_specs=any_spec,
        scratch_shapes=[row_buf(FFN_IN_BUFFERS), row_buf(FFN_OUT_BUFFERS),
                        pltpu.VMEM((2, D_MODEL, D_EXPERT), _F32), pltpu.VMEM((2, D_MODEL, D_EXPERT), _F32),
                        pltpu.VMEM((2, D_EXPERT, D_MODEL), _F32),
                        pltpu.VMEM((D_MODEL, 2 * D_EXPERT), _BF), pltpu.VMEM((D_EXPERT, D_MODEL), _BF),
                        pltpu.SMEM((n_blocks,), jnp.int32), pltpu.SMEM((n_blocks,), jnp.int32),
                        pltpu.SMEM((n_blocks,), jnp.int32),
                        pltpu.SemaphoreType.DMA((FFN_IN_BUFFERS,)), pltpu.SemaphoreType.DMA((FFN_OUT_BUFFERS,)),
                        pltpu.SemaphoreType.DMA((2,))],
    )
    return pl.pallas_call(
        functools.partial(_ffn_kernel, cap_blocks=cap_blocks),
        grid_spec=grid_spec,
        out_shape=jax.ShapeDtypeStruct(xs.shape, ROW_DTYPE),
        compiler_params=pltpu.CompilerParams(dimension_semantics=("arbitrary",), vmem_limit_bytes=VMEM_LIMIT,
                                             has_side_effects=True),
        name="moe_ffn",
    )(counts, xs, w1, w3, w2)


def _combine_kernel(slots_hbm, x1_ref, rw_ref, g_ref, ys_hbm, y_ref, idx_smem, buf, sem_idx, sem_rows):
    t = pl.program_id(0)
    TC = x1_ref.shape[0]
    idx_cp = pltpu.make_async_copy(slots_hbm.at[pl.ds(t * IDX_CHUNK, IDX_CHUNK)], idx_smem, sem_idx)
    idx_cp.start()
    idx_cp.wait()

    def issue(r, carry):
        for kk in range(2):
            pltpu.make_async_copy(ys_hbm.at[idx_smem[2 * r + kk]], buf.at[kk, r], sem_rows).start(priority=kk)
        return carry

    lax.fori_loop(0, TC, issue, 0, unroll=ISSUE_UNROLL)
    for kk in range(2):
        pltpu.make_async_copy(ys_hbm.at[pl.ds(0, TC)], buf.at[kk], sem_rows).wait()

    w = rw_ref[...]
    y0 = _unpack_rows(buf[0])
    y1 = _unpack_rows(buf[1])
    x2 = x1_ref[...] + (w[:, 0:1] * y0 + w[:, 1:2] * y1)
    y_ref[...] = _rms(x2, g_ref[...])


SC_GATHER_CHUNKS = 4
DENSE_TILE = 1024
SC_GATHER_BUFFERS = 3
SC_GATHER_MAX_WINDOW = 80


def _sc_gather(ys, rows):
    Tg = rows.shape[1]
    n_workers = SC_CORES * SC_SUBCORES
    per_w = Tg // n_workers
    assert per_w * n_workers == Tg and per_w % SUBLANES == 0
    W = max(w for w in range(SUBLANES, SC_GATHER_MAX_WINDOW + 1, SUBLANES) if per_w % w == 0)
    n_win = per_w // W
    NB = SC_GATHER_BUFFERS
    idx = rows.reshape(2, n_workers, n_win, W)
    jobs = [(kk, i) for i in range(n_win) for kk in range(2)]
    mesh = plsc.VectorSubcoreMesh(core_axis_name="c", subcore_axis_name="s")

    def body(ys_hbm, idx_hbm, out_hbm, i_v, rows_v, sem_g, sem_s):
        wid = lax.axis_index("s") * SC_CORES + lax.axis_index("c")
        base = wid * per_w
        for kk in range(2):
            pltpu.sync_copy(idx_hbm.at[kk, wid], i_v.at[kk])

        def gather(j):
            kk, i = jobs[j]
            return pltpu.make_async_copy(ys_hbm.at[i_v.at[kk, i]], rows_v.at[j % NB], sem_g.at[j % NB])

        def store(j):
            kk, i = jobs[j]
            return pltpu.make_async_copy(rows_v.at[j % NB], out_hbm.at[kk, pl.ds(base + i * W, W)], sem_s.at[j % NB])

        gather(0).start()
        for j in range(len(jobs)):
            gather(j).wait()
            store(j).start()
            if j + 1 < len(jobs):
                if j + 1 >= NB:
                    store(j + 1 - NB).wait()
                gather(j + 1).start()
        for j in range(max(0, len(jobs) - NB), len(jobs)):
            store(j).wait()

    return pl.kernel(
        body, mesh=mesh,
        out_type=jax.ShapeDtypeStruct((2, Tg) + ROW_TILE, ROW_DTYPE),
        scratch_types=[pltpu.VMEM((2, n_win, W), jnp.int32), pltpu.VMEM((NB, W) + ROW_TILE, ROW_DTYPE),
                       pltpu.SemaphoreType.DMA((NB,)), pltpu.SemaphoreType.DMA((NB,))],
        name="moe_sc_gather",
    )(ys, idx)


def _combine_dense_kernel(x1_ref, rw_ref, g_ref, rows_ref, *rest):
    y_ref = rest[-1]
    w = rw_ref[...]
    y0 = _unpack_rows(rows_ref[0])
    y1 = _unpack_rows(rows_ref[1])
    x2 = x1_ref[...] + (w[:, 0:1] * y0 + w[:, 1:2] * y1)
    y_ref[...] = _rms(x2, g_ref[...])


def _combine_dense(x1, rw, rows, y, g_final, tile, first_tile):
    Tg = rows.shape[1]
    assert Tg % tile == 0
    tok = lambda t: (first_tile + t, 0)
    in_specs = [pl.BlockSpec((tile, D_MODEL), tok), pl.BlockSpec((tile, LANES), tok), _full((1, D_MODEL)),
                pl.BlockSpec((2, tile) + ROW_TILE, lambda t: (0, t, 0, 0))]
    args = [x1, rw, g_final.reshape(1, D_MODEL), rows]
    aliases = {}
    if y is not None:
        in_specs.append(pl.BlockSpec(memory_space=pl.ANY))
        args.append(y)
        aliases = {4: 0}
    return pl.pallas_call(
        _combine_dense_kernel,
        grid=(Tg // tile,),
        in_specs=in_specs,
        out_specs=pl.BlockSpec((tile, D_MODEL), tok),
        out_shape=jax.ShapeDtypeStruct(x1.shape, _F32),
        input_output_aliases=aliases,
        compiler_params=pltpu.CompilerParams(dimension_semantics=("arbitrary",), vmem_limit_bytes=VMEM_LIMIT),
        name="moe_combine_dense",
    )(*args)


def _combine(x1, rw, slots, ys, g_final, tile, n_tiles):
    T = x1.shape[0]
    assert T % tile == 0 and 2 * tile <= IDX_CHUNK and slots.shape[0] == n_tiles * IDX_CHUNK
    any_spec = pl.BlockSpec(memory_space=pl.ANY)
    return pl.pallas_call(
        _combine_kernel,
        grid=(n_tiles,),
        in_specs=[any_spec, pl.BlockSpec((tile, D_MODEL), lambda t: (t, 0)),
                  pl.BlockSpec((tile, LANES), lambda t: (t, 0)), _full((1, D_MODEL)), any_spec],
        out_specs=pl.BlockSpec((tile, D_MODEL), lambda t: (t, 0)),
        out_shape=jax.ShapeDtypeStruct((T, D_MODEL), _F32),
        scratch_shapes=[pltpu.SMEM((IDX_CHUNK,), jnp.int32), pltpu.VMEM((2, tile) + ROW_TILE, ROW_DTYPE),
                        pltpu.SemaphoreType.DMA, pltpu.SemaphoreType.DMA],
        compiler_params=pltpu.CompilerParams(dimension_semantics=("arbitrary",), vmem_limit_bytes=VMEM_LIMIT),
        name="moe_combine",
    )(slots, x1, rw, g_final.reshape(1, D_MODEL), ys)


def _tile_for(n_tokens):
    tile = IDX_CHUNK // 2
    return tile if n_tokens % tile == 0 else n_tokens


def _chunked_slots(slot, tile):
    n_tiles = slot.shape[0] // tile
    s = slot.reshape(n_tiles, 2 * tile)
    return jnp.pad(s, ((0, 0), (0, IDX_CHUNK - 2 * tile))).reshape(-1)


def kernel(x_prompt, x_sample, state_pool, state_ret, g_mix, w_in, w_pool, pool_scale, ret_gn, w_out, g_ffn, w_grp, w_exp, w1, w3, w2, g_final):
    Bp, Lp, _ = x_prompt.shape
    Bs = x_sample.shape[0]
    Tp = Bp * Lp
    wts = _prep_weights(g_mix[0], w_in[0], w_pool[0], pool_scale[0], ret_gn[0], w_out[0], g_ffn[0], w_grp[0], w_exp[0])

    T_all = Tp + Bs
    cap = (-(-T_all // MOE_BLOCK) + 1) * MOE_BLOCK
    cap_blocks = cap // MOE_BLOCK

    xs_tok = x_sample.reshape(Bs, D_MODEL)
    u_s, q_s, k_s, kz_s, v_s, ga_s, gb_s = _sample_proj(xs_tok, wts, PAST_LEN)
    n_pairs = Bs * RET_HEADS
    nret_s = _sc_state_update(state_ret[0].reshape(n_pairs, RET_DK, RET_DV), kz_s.reshape(n_pairs, RET_DK),
                              v_s.reshape(n_pairs, RET_DV)).reshape(state_ret.shape[1:])
    x1p, dst_rows_p, rwp, npool_p, nret_p, counts_p, h2p = _mixer_prompt(x_prompt, wts, cap)
    xs = _sc_scatter(h2p, dst_rows_p, N_EXPERTS * cap)
    x1s, dst_s, rws, npool_s, counts, h2s = _mixer_sample(
        xs_tok, (u_s, q_s, k_s, v_s, ga_s, gb_s), jnp.swapaxes(state_pool[0], 0, 1), state_ret[0], wts, counts_p, cap)

    counts = counts[0, :N_EXPERTS].astype(jnp.int32)
    tile_p, tile_s = _tile_for(Tp), _tile_for(Bs)
    slots_s = _chunked_slots(dst_s[:, :2], tile_s)
    zstart = jnp.arange(N_EXPERTS, dtype=jnp.int32) * cap + counts
    zchunks = ((-counts) % MOE_BLOCK + ZERO_CHUNK - 1) // ZERO_CHUNK
    xs = _finish_dispatch(xs, h2s, slots_s, zstart, zchunks.astype(jnp.int32))
    n_blocks = -(-2 * T_all // MOE_BLOCK) + N_EXPERTS
    ys = _ffn(xs, counts, w1[0], w3[0], w2[0], cap_blocks, n_blocks)
    tile_d = DENSE_TILE if Tp % DENSE_TILE == 0 else tile_p
    n_tiles_d = Tp // tile_d
    chunk_tiles = [n_tiles_d // SC_GATHER_CHUNKS + (k < n_tiles_d % SC_GATHER_CHUNKS)
                   for k in range(SC_GATHER_CHUNKS)]
    gathered, first = [], 0
    for nt in chunk_tiles:
        if nt:
            gathered.append((first, _sc_gather(ys, dst_rows_p[:, first * tile_d:(first + nt) * tile_d])))
            first += nt
    y_s = _combine(x1s, rws, slots_s, ys, g_final, tile_s, 1)
    y_p = None
    for first, rows in gathered:
        y_p = _combine_dense(x1p, rwp, rows, y_p, g_final, tile_d, first)

    return (y_p.reshape(Bp, Lp, D_MODEL), y_s.reshape(Bs, 1, D_MODEL),
            npool_p[None], nret_p[None], jnp.swapaxes(npool_s, 0, 1)[None], nret_s[None])
```

```python
import dataclasses
import functools

import jax
import jax.numpy as jnp
import numpy as np
from jax import lax
from jax.experimental import pallas as pl
from jax.experimental.pallas import tpu as pltpu
from jax.experimental.pallas import tpu_sc as plsc

D_MODEL = 1024
EPS = 1e-6
POOL_GROUPS = 4
POOL_IN = D_MODEL // 2
POOL_GC = POOL_IN // POOL_GROUPS
POOL_OUT_GC = D_MODEL // POOL_GROUPS
POOL_WINDOWS = (2, 4, 8, 16)
POOL_HIST = max(POOL_WINDOWS) - 1
RET_HEADS = 4
RET_DK = D_MODEL // 8
RET_DV = D_MODEL // RET_HEADS
ROPE_BASE = 10000.0
PAST_LEN = 16384
N_GROUPS = 4
EXPERTS_PER_GROUP = 8
N_EXPERTS = N_GROUPS * EXPERTS_PER_GROUP
D_EXPERT = D_MODEL // 4
QK_W = RET_HEADS * RET_DK
V_W = RET_HEADS * RET_DV
OFF_U = 0
OFF_Q = POOL_IN
OFF_K = OFF_Q + QK_W
OFF_V = OFF_K + QK_W
OFF_GA = OFF_V + V_W
OFF_GB = OFF_GA + D_MODEL
IN_WIDTH = OFF_GB + D_MODEL

LANES = 128
SUBLANES = 8
ROW_WORDS = D_MODEL // 2
ROW_TILE = (ROW_WORDS // LANES, LANES)
ROW_DTYPE = jnp.uint32
HALO = 16
RET_CHUNK = 256
MIXER_SEQS = 2
MOE_BLOCK = 512
ROUTER_W = LANES
RANK_GROUP = 128
VMEM_LIMIT = 56 * 1024 * 1024

_BF = jnp.bfloat16
_F32 = jnp.float32


def _rms(x, g):
    inv = lax.rsqrt(jnp.mean(x * x, axis=-1, keepdims=True) + EPS)
    return x * inv * g


def _dot(a, b):
    return jnp.dot(a, b, preferred_element_type=_F32)


def _round_bf16(x):
    return x.astype(_BF).astype(_F32)


def _pack_rows(x):
    lo = lax.bitcast_convert_type(x[:, :ROW_WORDS].astype(_BF).astype(_F32), jnp.uint32)
    hi = lax.bitcast_convert_type(x[:, ROW_WORDS:].astype(_BF).astype(_F32), jnp.uint32)
    return ((lo >> 16) | hi).reshape((x.shape[0],) + ROW_TILE)


def _unpack_rows(w):
    w = w.reshape(w.shape[0], ROW_WORDS)
    lo = lax.bitcast_convert_type(w << 16, _F32)
    hi = lax.bitcast_convert_type(w & jnp.uint32(0xFFFF0000), _F32)
    return jnp.concatenate([lo, hi], axis=1)


def _rotary(x, cos2, sin2):
    return x * cos2 + pltpu.roll(x, RET_DK // 2, 1) * sin2


def _route(logits):
    lane = lax.broadcasted_iota(jnp.int32, logits.shape, 1).astype(_F32)
    neg = jnp.float32(-jnp.inf)
    big = jnp.float32(1 << 20)
    lg = jnp.where(lane < N_GROUPS, logits, neg)
    mg = jnp.max(lg, axis=-1, keepdims=True)
    g_idx = jnp.min(jnp.where(lg == mg, lane, big), axis=-1, keepdims=True)
    p_g = 1.0 / jnp.sum(jnp.exp(lg - mg), axis=-1, keepdims=True)
    lo = N_GROUPS + g_idx * EXPERTS_PER_GROUP
    in_grp = (lane >= lo) & (lane < lo + EXPERTS_PER_GROUP)
    le = jnp.where(in_grp, logits, neg)
    m1 = jnp.max(le, axis=-1, keepdims=True)
    i1 = jnp.min(jnp.where(le == m1, lane, big), axis=-1, keepdims=True)
    le2 = jnp.where(lane == i1, neg, le)
    m2 = jnp.max(le2, axis=-1, keepdims=True)
    i2 = jnp.min(jnp.where(le2 == m2, lane, big), axis=-1, keepdims=True)
    t = jnp.exp(m2 - m1)
    den = 1.0 + t
    e0 = (i1 - N_GROUPS).astype(jnp.int32)
    e1 = (i2 - N_GROUPS).astype(jnp.int32)
    return e0, e1, p_g * (1.0 / den), p_g * (t / den)


def _post_mix(x, mixed_ref, w_out_ref, g_ffn_ref, w_rt_ref, x1_ref, rw_ref):
    x1 = x + _dot(mixed_ref[...], w_out_ref[...])
    x1_ref[...] = x1.reshape(x1_ref.shape)
    h2 = _rms(x1, g_ffn_ref[...])
    e0, e1, w0, w1 = _route(_dot(h2.astype(_BF), w_rt_ref[...]))
    lane = lax.broadcasted_iota(jnp.int32, (x.shape[0], LANES), 1)
    rw_ref[...] = jnp.where(lane == 0, w0, jnp.where(lane == 1, w1, 0.0)).reshape(rw_ref.shape)
    return h2, e0, e1


def _route_block(x1, g_ffn_ref, w_rt_ref, carry_scr, cap, valid, h2_ref, rw_ref, dst_t_ref):
    for _ in _route_stages(x1, g_ffn_ref, w_rt_ref, carry_scr, cap, valid, h2_ref, rw_ref, dst_t_ref):
        pass


def _route_stages(x1, g_ffn_ref, w_rt_ref, carry_scr, cap, valid, h2_ref, rw_ref, dst_t_ref):
    h2 = _rms(x1, g_ffn_ref[...])
    logits = _dot(h2.astype(_BF), w_rt_ref[...])
    h2_ref[...] = _pack_rows(h2).reshape(h2_ref.shape)
    yield
    e0, e1, w0, w1 = _route(logits)
    lane = lax.broadcasted_iota(jnp.int32, (x1.shape[0], LANES), 1)
    rw_ref[...] = jnp.where(lane == 0, w0, jnp.where(lane == 1, w1, 0.0)).reshape(rw_ref.shape)
    yield
    dst = _rank_rows(e0, e1, cap, carry_scr, valid)
    dst_t = dst.T[0:SUBLANES, :].astype(jnp.int32)
    n_parts, _, width = dst_t_ref.shape
    for part in range(n_parts):
        dst_t_ref[part] = dst_t[:, part * width:(part + 1) * width]
    yield


def _rank_rows(e0, e1, cap, carry_scr, valid=None):
    R = e0.shape[0]
    lane = lax.broadcasted_iota(jnp.int32, (R, LANES), 1)
    m0 = lane == e0
    m1 = lane == e1
    onehot = jnp.where(m0 | m1, 1.0, 0.0)
    G = min(RANK_GROUP, R)
    r_i = lax.broadcasted_iota(jnp.int32, (G, G), 0)
    c_i = lax.broadcasted_iota(jnp.int32, (G, G), 1)
    tri = jnp.where(c_i < r_i, 1.0, 0.0).astype(_BF)
    running = carry_scr[...]
    parts = []
    for g in range(R // G):
        grp = onehot[g * G:(g + 1) * G]
        parts.append(_dot(tri, grp.astype(_BF)) + running)
        running = running + jnp.sum(grp, axis=0, keepdims=True)
    before = jnp.concatenate(parts, axis=0)
    d0 = e0.astype(_F32) * cap + jnp.sum(jnp.where(m0, before, 0.0), axis=-1, keepdims=True)
    d1 = e1.astype(_F32) * cap + jnp.sum(jnp.where(m1, before, 0.0), axis=-1, keepdims=True)
    added = running - carry_scr[...]
    carry_scr[...] += added if valid is None else added * valid
    return jnp.where(lane == 0, d0, jnp.where(lane == 1, d1, 0.0))


def _group_norm(o, gain):
    mu = jnp.mean(o, axis=-1, keepdims=True)
    d = o - mu
    var = jnp.mean(d * d, axis=-1, keepdims=True)
    return d * lax.rsqrt(var + EPS) * gain


def _mixer_prompt_kernel(x_ref, g_mix_ref, w_in_ref, w_pool_ref, pscale_ref, gn_ref, w_out_ref, g_ffn_ref,
                         w_rt_ref, cos_ref, sin_ref, dmask_ref, xi_ref, zeta_ref, gc_ref,
                         x1_ref, dst_t_ref, rw_ref, npool_ref, nret_ref, cnt_ref, h2_ref,
                         u_scr, s_scr, mixed_scr, carry_scr, z_scr, x1_prev_scr, *, cap):
    c = pl.program_id(1)
    n_c = pl.num_programs(1)
    step = pl.program_id(0) * n_c + c
    NSEQ, C, _ = x_ref.shape
    R = NSEQ * C

    @pl.when(step == 0)
    def _():
        carry_scr[...] = jnp.zeros(carry_scr.shape, _F32)
        x1_prev_scr[...] = jnp.zeros(x1_prev_scr.shape, _F32)

    @pl.when(c == 0)
    def _():
        u_scr[:, 0:HALO, :] = jnp.zeros((NSEQ, HALO, POOL_IN), _F32)
        s_scr[...] = jnp.zeros(s_scr.shape, _F32)

    x = x_ref[...].reshape(R, D_MODEL)
    h = _rms(x, g_mix_ref[...]).astype(_BF)

    stages = _route_stages(x1_prev_scr[...], g_ffn_ref, w_rt_ref, carry_scr, cap, (step > 0).astype(_F32),
                           h2_ref, rw_ref, dst_t_ref)
    for lo, hi in ((0, OFF_V), (OFF_V, OFF_GA), (OFF_GA, OFF_GB), (OFF_GB, IN_WIDTH)):
        z_scr[:, lo:hi] = _dot(h, w_in_ref[:, lo:hi])
        next(stages, None)
    cnt_ref[...] = carry_scr[...]

    u_scr[:, HALO:HALO + C, :] = z_scr[:, OFF_U:OFF_U + POOL_IN].reshape(NSEQ, C, POOL_IN)
    q = z_scr[:, OFF_Q:OFF_Q + QK_W]
    k = z_scr[:, OFF_K:OFF_K + QK_W]
    cos2 = jnp.concatenate([cos_ref[...]] * NSEQ, axis=0)
    sin2 = jnp.concatenate([sin_ref[...]] * NSEQ, axis=0)
    pos1 = (c * C + 1 + lax.broadcasted_iota(jnp.int32, (C, POOL_GC), 0)).astype(_F32)

    for j in range(RET_HEADS):
        win = POOL_WINDOWS[j]
        cs = slice(j * POOL_GC, (j + 1) * POOL_GC)
        n_rows = jnp.minimum(pos1, jnp.float32(win))
        pooled = []
        for sq in range(NSEQ):
            u_j = u_scr[sq, HALO:HALO + C, cs]
            s = u_j
            for d in range(1, win):
                s = s + u_scr[sq, HALO - d:HALO - d + C, cs]
            pooled.append(s / n_rows - u_j)
        osl = slice(j * POOL_OUT_GC, (j + 1) * POOL_OUT_GC)
        pool_out = _dot(jnp.concatenate(pooled, axis=0).astype(_BF), w_pool_ref[j]) * pscale_ref[:, osl]

        qs = slice(j * RET_DK, (j + 1) * RET_DK)
        qb = _rotary(q[:, qs], cos2, sin2).astype(_BF)
        kf = _rotary(k[:, qs], cos2, sin2) * (RET_DK ** -0.5)
        kb = kf.astype(_BF)
        vb = z_scr[:, OFF_V + j * RET_DV:OFF_V + (j + 1) * RET_DV].astype(_BF)
        ret = []
        for sq in range(NSEQ):
            rows = slice(sq * C, (sq + 1) * C)
            scores = lax.dot_general(qb[rows], kb[rows], (((1,), (1,)), ((), ())),
                                     preferred_element_type=_F32) * dmask_ref[j]
            s_old = s_scr[sq, j]
            o = _dot(scores.astype(_BF), vb[rows]) + _dot(qb[rows], s_old.astype(_BF)) * xi_ref[j]
            kz = (kf[rows] * zeta_ref[j]).astype(_BF)
            s_scr[sq, j] = gc_ref[j] * s_old + lax.dot_general(kz, vb[rows], (((0,), (0,)), ((), ())),
                                                               preferred_element_type=_F32)
            ret.append(_group_norm(o, gn_ref[:, osl]))
        ret_out = jnp.concatenate(ret, axis=0)

        ga = z_scr[:, OFF_GA + j * RET_DV:OFF_GA + (j + 1) * RET_DV]
        gb = z_scr[:, OFF_GB + j * RET_DV:OFF_GB + (j + 1) * RET_DV]
        mixed_scr[:, osl] = (jax.nn.sigmoid(ga) * pool_out + jax.nn.sigmoid(gb) * ret_out).astype(_BF)

    x1 = x + _dot(mixed_scr[...], w_out_ref[...])
    x1_ref[...] = x1.reshape(x1_ref.shape)
    x1_prev_scr[...] = x1

    u_scr[:, 0:HALO, :] = u_scr[:, C:C + HALO, :]

    @pl.when(c == n_c - 1)
    def _():
        npool_ref[...] = u_scr[:, HALO + C - POOL_HIST:HALO + C, :]
        nret_ref[...] = s_scr[...]


def _decay_tables(C):
    f32 = np.float32
    log_g = np.log(f32(1.0) - np.exp2(f32(-5.0) - np.arange(RET_HEADS, dtype=f32)))
    i = np.arange(C, dtype=f32)
    diff = i[:, None] - i[None, :]
    dmask = np.where(diff >= 0, np.exp(np.maximum(diff, f32(0.0))[None] * log_g[:, None, None]), f32(0.0))
    xi = np.exp((i[None, :] + f32(1.0)) * log_g[:, None])
    zeta = np.exp((f32(C) - f32(1.0) - i)[None, :] * log_g[:, None])
    g_chunk = np.exp(f32(C) * log_g)
    return dmask.astype(f32), xi.astype(f32), zeta.astype(f32), g_chunk.astype(f32)


def _rope_tables(pos):
    half = RET_DK // 2
    freqs = ROPE_BASE ** (-jnp.arange(half, dtype=_F32) / half)
    ang = pos[:, None] * freqs[None, :]
    cos, sin = jnp.cos(ang), jnp.sin(ang)
    return jnp.concatenate([cos, cos], axis=-1), jnp.concatenate([-sin, sin], axis=-1)


def _const(a, shape=None):
    if shape is not None:
        a = np.ascontiguousarray(np.broadcast_to(a, shape))
    return jnp.asarray(a)


def _full(shape):
    n = len(shape)
    return pl.BlockSpec(shape, lambda *_: (0,) * n)


def _route_tail_kernel(x1_ref, g_ffn_ref, w_rt_ref, cnt_in_ref, dst_t_in, rw_in, h2_in,
                       dst_t_ref, rw_ref, cnt_ref, h2_ref, carry_scr, *, cap):
    del dst_t_in, rw_in, h2_in
    carry_scr[...] = cnt_in_ref[...]
    x1 = x1_ref[...]
    _route_block(x1.reshape(x1.shape[0] * x1.shape[1], D_MODEL), g_ffn_ref, w_rt_ref, carry_scr, cap, None,
                 h2_ref, rw_ref, dst_t_ref)
    cnt_ref[...] = carry_scr[...]


def _route_tail(x1, dst_t, rw, counts, h2, wts, cap, C, last):
    NSEQ = x1.shape[0]
    any_spec = pl.BlockSpec(memory_space=pl.ANY)
    return pl.pallas_call(
        functools.partial(_route_tail_kernel, cap=cap),
        grid=(1,),
        in_specs=[pl.BlockSpec((NSEQ, C, D_MODEL), lambda i: (0, last, 0)), _full((1, D_MODEL)),
                  _full((D_MODEL, ROUTER_W)), _full((1, LANES)), any_spec, any_spec, any_spec],
        out_specs=(pl.BlockSpec((NSEQ, SUBLANES, C), lambda i: (0, 0, last)),
                   pl.BlockSpec((NSEQ, C, LANES), lambda i: (0, last, 0)),
                   _full((1, LANES)),
                   pl.BlockSpec((NSEQ, C) + ROW_TILE, lambda i: (0, last, 0, 0))),
        out_shape=(jax.ShapeDtypeStruct(dst_t.shape, dst_t.dtype), jax.ShapeDtypeStruct(rw.shape, rw.dtype),
                   jax.ShapeDtypeStruct((1, LANES), _F32), jax.ShapeDtypeStruct(h2.shape, h2.dtype)),
        input_output_aliases={4: 0, 5: 1, 6: 3},
        scratch_shapes=[pltpu.VMEM((1, LANES), _F32)],
        compiler_params=pltpu.CompilerParams(dimension_semantics=("arbitrary",), vmem_limit_bytes=VMEM_LIMIT),
        name="mixer_route_tail",
    )(x1, wts["g_ffn"], wts["w_rt"], counts, dst_t, rw, h2)


def _mixer_prompt(x, wts, cap):
    B, L, _ = x.shape
    C = RET_CHUNK if L % RET_CHUNK == 0 else L
    n_c = L // C
    T = B * L
    dmask, xi, zeta, g_chunk = _decay_tables(C)
    dmask = _const(dmask)
    xi_b = _const(xi[:, :, None], (RET_HEADS, C, RET_DV))
    zeta_b = _const(zeta[:, :, None], (RET_HEADS, C, RET_DK))
    gc_b = _const(g_chunk[:, None, None], (RET_HEADS, 1, RET_DV))
    cos2, sin2 = _rope_tables(jnp.arange(L).astype(_F32))

    NSEQ = MIXER_SEQS if B % MIXER_SEQS == 0 else 1
    B2 = B // NSEQ
    T2 = T // NSEQ
    x = x.reshape(NSEQ, B2, L, D_MODEL)
    tok = lambda b, c: (0, b * n_c + c, 0)
    in_specs = [
        pl.BlockSpec((NSEQ, None, C, D_MODEL), lambda b, c: (0, b, c, 0)),
        _full((1, D_MODEL)), _full((D_MODEL, IN_WIDTH)), _full((POOL_GROUPS, POOL_GC, POOL_OUT_GC)),
        _full((1, D_MODEL)), _full((1, D_MODEL)), _full((D_MODEL, D_MODEL)), _full((1, D_MODEL)),
        _full((D_MODEL, ROUTER_W)),
        pl.BlockSpec((C, RET_DK), lambda b, c: (c, 0)), pl.BlockSpec((C, RET_DK), lambda b, c: (c, 0)),
        _full((RET_HEADS, C, C)), _full((RET_HEADS, C, RET_DV)), _full((RET_HEADS, C, RET_DK)),
        _full((RET_HEADS, 1, RET_DV)),
    ]
    R = NSEQ * C
    out_shape = (
        jax.ShapeDtypeStruct((NSEQ, T2, D_MODEL), _F32),
        jax.ShapeDtypeStruct((NSEQ, SUBLANES, T2), jnp.int32),
        jax.ShapeDtypeStruct((NSEQ, T2, LANES), _F32),
        jax.ShapeDtypeStruct((NSEQ, B2, POOL_HIST, POOL_IN), _F32),
        jax.ShapeDtypeStruct((NSEQ, B2, RET_HEADS, RET_DK, RET_DV), _F32),
        jax.ShapeDtypeStruct((1, LANES), _F32),
        jax.ShapeDtypeStruct((NSEQ, T2) + ROW_TILE, ROW_DTYPE),
    )
    prev = lambda b, c: jnp.maximum(b * n_c + c - 1, 0)
    out_specs = (
        pl.BlockSpec((NSEQ, C, D_MODEL), tok),
        pl.BlockSpec((NSEQ, SUBLANES, C), lambda b, c: (0, 0, prev(b, c))),
        pl.BlockSpec((NSEQ, C, LANES), lambda b, c: (0, prev(b, c), 0)),
        pl.BlockSpec((NSEQ, None, POOL_HIST, POOL_IN), lambda b, c: (0, b, 0, 0)),
        pl.BlockSpec((NSEQ, None, RET_HEADS, RET_DK, RET_DV), lambda b, c: (0, b, 0, 0, 0)),
        _full((1, LANES)),
        pl.BlockSpec((NSEQ, C) + ROW_TILE, lambda b, c: (0, prev(b, c), 0, 0)),
    )
    x1, dst_t, rw, npool, nret, counts, h2 = pl.pallas_call(
        functools.partial(_mixer_prompt_kernel, cap=cap),
        grid=(B2, n_c),
        in_specs=in_specs,
        out_specs=out_specs,
        out_shape=out_shape,
        scratch_shapes=[pltpu.VMEM((NSEQ, HALO + C, POOL_IN), _F32),
                        pltpu.VMEM((NSEQ, RET_HEADS, RET_DK, RET_DV), _F32),
                        pltpu.VMEM((R, D_MODEL), _BF),
                        pltpu.VMEM((1, LANES), _F32),
                        pltpu.VMEM((R, IN_WIDTH), _F32),
                        pltpu.VMEM((R, D_MODEL), _F32)],
        compiler_params=pltpu.CompilerParams(dimension_semantics=("arbitrary", "arbitrary"),
                                             vmem_limit_bytes=VMEM_LIMIT),
        name="mixer_prompt",
    )(x, wts["g_mix"], wts["w_in"], wts["w_pool"], wts["pool_scale"], wts["ret_gn"], wts["w_out"],
      wts["g_ffn"], wts["w_rt"], cos2, sin2, dmask, xi_b, zeta_b, gc_b)
    dst_t, rw, counts, h2 = _route_tail(x1, dst_t, rw, counts, h2, wts, cap, C, B2 * n_c - 1)
    dst_rows = jnp.moveaxis(dst_t[:, 0:2, :], 1, 0).reshape(2, T)
    return (x1.reshape(T, D_MODEL), dst_rows, rw.reshape(T, LANES),
            npool.reshape(B, POOL_HIST, POOL_IN), nret.reshape(B, RET_HEADS, RET_DK, RET_DV), counts,
            h2.reshape((T,) + ROW_TILE))


def _prep_weights(g_mix, w_pool, pool_scale, ret_gn, w_out, g_ffn, w_grp, w_exp):
    w_rt = jnp.concatenate([w_grp, w_exp.reshape(D_MODEL, N_EXPERTS)], axis=1)
    w_rt = jnp.pad(w_rt, ((0, 0), (0, ROUTER_W - w_rt.shape[1])))
    row = lambda v: v.reshape(1, D_MODEL)
    return dict(g_mix=row(g_mix), w_pool=w_pool.astype(_BF), pool_scale=row(pool_scale),
                ret_gn=row(ret_gn), w_out=w_out.astype(_BF), g_ffn=row(g_ffn), w_rt=w_rt.astype(_BF))


SAMPLE_TB = 8


PROJ_COLS = 512


def _sample_proj_kernel(x_ref, g_mix_ref, w_in_ref, cos_ref, sin_ref, zeta_ref,
                        w_bf_ref, u_ref, q_ref, k_ref, kz_ref, v_ref, ga_ref, gb_ref, h_scr, z_scr):
    c = pl.program_id(0)

    @pl.when(c == 0)
    def _():
        h_scr[...] = _rms(x_ref[...], g_mix_ref[...]).astype(_BF)

    w = w_in_ref[...].astype(_BF)
    w_bf_ref[...] = w
    z_scr[c] = _dot(h_scr[...], w)

    @pl.when(c == pl.num_programs(0) - 1)
    def _():
        def cols(off, width):
            parts = [z_scr[i] for i in range(off // PROJ_COLS, (off + width) // PROJ_COLS)]
            return parts[0] if len(parts) == 1 else jnp.concatenate(parts, axis=1)

        u_ref[...] = cols(OFF_U, POOL_IN)
        q = cols(OFF_Q, QK_W)
        k = cols(OFF_K, QK_W)
        for j in range(RET_HEADS):
            qs = slice(j * RET_DK, (j + 1) * RET_DK)
            kf = _rotary(k[:, qs], cos_ref[...], sin_ref[...]) * (RET_DK ** -0.5)
            q_ref[:, qs] = _rotary(q[:, qs], cos_ref[...], sin_ref[...])
            k_ref[:, qs] = kf
            kz_ref[:, qs] = kf * zeta_ref[j]
        v_ref[...] = cols(OFF_V, V_W)
        ga_ref[...] = cols(OFF_GA, D_MODEL)
        gb_ref[...] = cols(OFF_GB, D_MODEL)


def _sample_proj(x, g_mix, w_in, pos0):
    Bs = x.shape[0]
    offsets = (OFF_U, OFF_Q, OFF_K, OFF_V, OFF_GA, OFF_GB, IN_WIDTH)
    assert all(o % PROJ_COLS == 0 for o in offsets)
    n_c = IN_WIDTH // PROJ_COLS
    _, _, zeta, _ = _decay_tables(1)
    cos2, sin2 = _rope_tables((pos0 + jnp.arange(1)).astype(_F32))
    widths = (POOL_IN, QK_W, QK_W, QK_W, V_W, D_MODEL, D_MODEL)
    return pl.pallas_call(
        _sample_proj_kernel,
        grid=(n_c,),
        in_specs=[_full((Bs, D_MODEL)), _full((1, D_MODEL)), pl.BlockSpec((D_MODEL, PROJ_COLS), lambda c: (0, c)),
                  _full((1, RET_DK)), _full((1, RET_DK)), _full((RET_HEADS, 1, 1))],
        out_specs=(pl.BlockSpec((D_MODEL, PROJ_COLS), lambda c: (0, c)),) + tuple(_full((Bs, w)) for w in widths),
        out_shape=(jax.ShapeDtypeStruct((D_MODEL, IN_WIDTH), _BF),)
        + tuple(jax.ShapeDtypeStruct((Bs, w), _F32) for w in widths),
        scratch_shapes=[pltpu.VMEM((Bs, D_MODEL), _BF), pltpu.VMEM((n_c, Bs, PROJ_COLS), _F32)],
        compiler_params=pltpu.CompilerParams(dimension_semantics=("arbitrary",), vmem_limit_bytes=VMEM_LIMIT),
        name="sample_proj",
    )(x, g_mix, w_in, cos2, sin2, _const(zeta[:, :, None], (RET_HEADS, 1, 1)))


def _mixer_sample_kernel(x_ref, u_ref, q_ref, k_ref, v_ref, ga_ref, gb_ref, spool_ref, sret_ref,
                         w_pool_ref, pscale_ref, gn_ref, w_out_ref, g_ffn_ref, w_rt_ref, dm_ref, xi_ref,
                         carry_in_ref, nret_in,
                         x1_ref, dst_ref, rw_ref, npool_ref, cnt_ref, h2_ref, nret_out,
                         qt_scr, pooled_scr, o_scr, mixed_scr, carry_scr, *, cap):
    del nret_in, nret_out
    t = pl.program_id(0)
    n_t = pl.num_programs(0)
    Bs = x_ref.shape[0]
    TB = sret_ref.shape[0]

    @pl.when(t == 0)
    def _():
        for j in range(RET_HEADS):
            qt_scr[j] = _round_bf16(q_ref[:, j * RET_DK:(j + 1) * RET_DK]).T

    shift = (Bs - t * TB) % Bs
    qt = [pltpu.roll(qt_scr[j], shift, 1) for j in range(RET_HEADS)]

    blk = pl.ds(pl.multiple_of(t * TB, TB), TB)
    u_blk = u_ref[blk, :]
    q_blk = q_ref[blk, :]
    k_blk = k_ref[blk, :]
    v_blk = v_ref[blk, :]
    score = [jnp.sum(q_blk[:, j * RET_DK:(j + 1) * RET_DK] * k_blk[:, j * RET_DK:(j + 1) * RET_DK],
                     axis=-1, keepdims=True) * dm_ref[j] for j in range(RET_HEADS)]

    groups = []
    for g, win in enumerate(POOL_WINDOWS):
        cs = slice(g * POOL_GC, (g + 1) * POOL_GC)
        s = u_blk[:, cs]
        for r in range(POOL_HIST - (win - 1), POOL_HIST):
            s = s + spool_ref[r, :, cs]
        groups.append(s / jnp.float32(win) - u_blk[:, cs])
    pooled_scr[blk, :] = jnp.concatenate(groups, axis=1)
    npool_ref[0:POOL_HIST - 1] = spool_ref[1:POOL_HIST]
    npool_ref[POOL_HIST - 1] = u_blk

    o_rows = []
    for i in range(TB):
        heads = []
        for j in range(RET_HEADS):
            v_row = v_blk[i:i + 1, j * RET_DV:(j + 1) * RET_DV]
            qcol = qt[j][:, i:i + 1]
            qs_old = jnp.sum(qcol * _round_bf16(sret_ref[i, j]), axis=0, keepdims=True)
            heads.append(score[j][i:i + 1, :] * v_row + qs_old * xi_ref[j])
        o_rows.append(jnp.concatenate(heads, axis=1))
    o_scr[blk, :] = jnp.concatenate(o_rows, axis=0)

    @pl.when(t == n_t - 1)
    def _():
        for j in range(RET_HEADS):
            cs = slice(j * POOL_GC, (j + 1) * POOL_GC)
            osl = slice(j * POOL_OUT_GC, (j + 1) * POOL_OUT_GC)
            pool_out = _dot(pooled_scr[:, cs].astype(_BF), w_pool_ref[j]) * pscale_ref[:, osl]
            ret_out = _group_norm(o_scr[:, osl], gn_ref[:, osl])
            mixed_scr[:, osl] = (jax.nn.sigmoid(ga_ref[:, osl]) * pool_out
                                 + jax.nn.sigmoid(gb_ref[:, osl]) * ret_out).astype(_BF)
        h2, e0, e1 = _post_mix(x_ref[...], mixed_scr, w_out_ref, g_ffn_ref, w_rt_ref, x1_ref, rw_ref)
        h2_ref[...] = _pack_rows(h2).reshape(h2_ref.shape)
        carry_scr[...] = carry_in_ref[...]
        dst_ref[...] = _rank_rows(e0, e1, cap, carry_scr).astype(jnp.int32)
        cnt_ref[...] = carry_scr[...]


def _mixer_sample(x, proj, state_pool, state_ret, new_state_ret, wts, carry_in, cap):
    Bs = x.shape[0]
    assert Bs == LANES and Bs % SAMPLE_TB == 0
    TB = SAMPLE_TB
    u, q, k, v, ga, gb = proj
    dmask, xi, _, _ = _decay_tables(1)
    dm_b = _const(dmask, (RET_HEADS, 1, 1))
    xi_b = _const(xi[:, :, None], (RET_HEADS, 1, RET_DV))

    in_specs = [
        _full((Bs, D_MODEL)),
        _full((Bs, POOL_IN)), _full((Bs, QK_W)), _full((Bs, QK_W)), _full((Bs, V_W)),
        _full((Bs, D_MODEL)), _full((Bs, D_MODEL)),
        pl.BlockSpec((POOL_HIST, TB, POOL_IN), lambda t: (0, t, 0)),
        pl.BlockSpec((TB, RET_HEADS, RET_DK, RET_DV), lambda t: (t, 0, 0, 0)),
        _full((POOL_GROUPS, POOL_GC, POOL_OUT_GC)),
        _full((1, D_MODEL)), _full((1, D_MODEL)), _full((D_MODEL, D_MODEL)), _full((1, D_MODEL)),
        _full((D_MODEL, ROUTER_W)),
        _full((RET_HEADS, 1, 1)), _full((RET_HEADS, 1, RET_DV)),
        _full((1, LANES)),
        pl.BlockSpec(memory_space=pl.ANY),
    ]
    out_shape = (
        jax.ShapeDtypeStruct((Bs, D_MODEL), _F32),
        jax.ShapeDtypeStruct((Bs, LANES), jnp.int32),
        jax.ShapeDtypeStruct((Bs, LANES), _F32),
        jax.ShapeDtypeStruct((POOL_HIST, Bs, POOL_IN), _F32),
        jax.ShapeDtypeStruct((1, LANES), _F32),
        jax.ShapeDtypeStruct((Bs,) + ROW_TILE, ROW_DTYPE),
        jax.ShapeDtypeStruct(new_state_ret.shape, new_state_ret.dtype),
    )
    out_specs = (
        _full((Bs, D_MODEL)), _full((Bs, LANES)), _full((Bs, LANES)),
        pl.BlockSpec((POOL_HIST, TB, POOL_IN), lambda t: (0, t, 0)),
        _full((1, LANES)), _full((Bs,) + ROW_TILE),
        pl.BlockSpec(memory_space=pl.ANY),
    )
    f32 = lambda *s: pltpu.VMEM(s, _F32)
    return pl.pallas_call(
        functools.partial(_mixer_sample_kernel, cap=cap),
        grid=(Bs // TB,),
        in_specs=in_specs,
        out_specs=out_specs,
        out_shape=out_shape,
        scratch_shapes=[f32(RET_HEADS, RET_DK, Bs), f32(Bs, POOL_IN), f32(Bs, V_W),
                        pltpu.VMEM((Bs, D_MODEL), _BF), f32(1, LANES)],
        input_output_aliases={len(in_specs) - 1: len(out_shape) - 1},
        compiler_params=pltpu.CompilerParams(dimension_semantics=("arbitrary",), vmem_limit_bytes=VMEM_LIMIT),
        name="mixer_sample",
    )(x, u, q, k, v, ga, gb, state_pool, state_ret, wts["w_pool"], wts["pool_scale"], wts["ret_gn"],
      wts["w_out"], wts["g_ffn"], wts["w_rt"], dm_b, xi_b, carry_in, new_state_ret)


SC_CORES = 2
SC_SUBCORES = 16
SC_LANES = 16
STATE_ROWS = 64
STATE_UNROLL = LANES // SC_LANES


def _sc_state_update(s0, kz, v):
    P = s0.shape[0]
    n_workers = SC_CORES * SC_SUBCORES
    assert P % n_workers == 0 and RET_DK % (2 * STATE_ROWS) == 0 and STATE_ROWS % STATE_UNROLL == 0
    per_w = P // n_workers
    R = STATE_ROWS
    n_parts = RET_DK // R
    n_vc = RET_DV // SC_LANES
    _, _, _, g_chunk = _decay_tables(1)
    decay = _const(g_chunk[:, None], (RET_HEADS, SC_LANES))
    kz_rep = jnp.broadcast_to(kz.reshape(P, RET_DK, 1), (P, RET_DK, SC_LANES)).reshape(P, RET_DK // STATE_UNROLL, LANES)
    mesh = plsc.VectorSubcoreMesh(core_axis_name="c", subcore_axis_name="s")

    def body(s0_hbm, k_hbm, v_hbm, g_hbm, out_hbm, in_v, out_v, k_v, v_v, g_v, sem_in, sem_out):
        wid = lax.axis_index("s") * SC_CORES + lax.axis_index("c")
        base = wid * per_w
        pltpu.sync_copy(k_hbm.at[pl.ds(base, per_w)], k_v)
        pltpu.sync_copy(v_hbm.at[pl.ds(base, per_w)], v_v)
        pltpu.sync_copy(g_hbm, g_v)

        def load(p, part, slot):
            return pltpu.make_async_copy(s0_hbm.at[p, pl.ds(part * R, R)], in_v.at[slot], sem_in.at[slot])

        def store(p, part, slot):
            return pltpu.make_async_copy(out_v.at[slot], out_hbm.at[p, pl.ds(part * R, R)], sem_out.at[slot])

        load(base, 0, 0).start()

        def pair(pp, carry):
            p = base + pp
            g = g_v[p % RET_HEADS, :]
            vs = [v_v[pp, pl.ds(c * SC_LANES, SC_LANES)] for c in range(n_vc)]
            for part in range(n_parts):
                slot = part % 2
                load(p, part, slot).wait()
                if part + 1 < n_parts:
                    load(p, part + 1, 1 - slot).start()
                else:
                    @pl.when(pp + 1 < per_w)
                    def _():
                        load(p + 1, 0, 1 - slot).start()

                @pl.when(pp > 0)
                def _():
                    store(p, part, slot).wait()

                def rows(i8, c2):
                    for j in range(STATE_UNROLL):
                        i = i8 * STATE_UNROLL + j
                        ki = k_v[pp, part * (R // STATE_UNROLL) + i8, pl.ds(j * SC_LANES, SC_LANES)]
                        for c in range(n_vc):
                            cs = pl.ds(c * SC_LANES, SC_LANES)
                            out_v[slot, i, cs] = g * in_v[slot, i, cs] + ki * vs[c]
                    return c2

                lax.fori_loop(0, R // STATE_UNROLL, rows, 0)
                store(p, part, slot).start()
            return carry

        lax.fori_loop(0, per_w, pair, 0)
        for slot in range(2):
            store(base, slot, slot).wait()

    return pl.kernel(
        body, mesh=mesh,
        out_type=jax.ShapeDtypeStruct(s0.shape, _F32),
        scratch_types=[pltpu.VMEM((2, R, RET_DV), _F32), pltpu.VMEM((2, R, RET_DV), _F32),
                       pltpu.VMEM((per_w, RET_DK // STATE_UNROLL, LANES), _F32), pltpu.VMEM((per_w, RET_DV), _F32),
                       pltpu.VMEM((RET_HEADS, SC_LANES), _F32),
                       pltpu.SemaphoreType.DMA((2,)), pltpu.SemaphoreType.DMA((2,))],
        compiler_params=dataclasses.replace(pltpu.CompilerParams(), needs_layout_passes=False),
        name="sample_state_update",
    )(s0, kz_rep, v, decay)


IDX_CHUNK = 1024
ISSUE_UNROLL = 8
SC_WINDOW = 64


def _sc_scatter(h2, dst_rows, n_rows):
    T = h2.shape[0]
    n_workers = SC_CORES * SC_SUBCORES
    W = SC_WINDOW
    assert T % (n_workers * W) == 0
    per_w = T // n_workers
    n_win = per_w // W
    d0 = dst_rows[0].reshape(T // W, W)
    d1 = dst_rows[1].reshape(T // W, W)
    mesh = plsc.VectorSubcoreMesh(core_axis_name="c", subcore_axis_name="s")

    def body(h2_hbm, d0_hbm, d1_hbm, xs_hbm, i0_v, i1_v, rows_v, sem_load, sem_store):
        wid = lax.axis_index("s") * SC_CORES + lax.axis_index("c")
        base = wid * per_w
        pltpu.sync_copy(d0_hbm.at[pl.ds(wid * n_win, n_win)], i0_v)
        pltpu.sync_copy(d1_hbm.at[pl.ds(wid * n_win, n_win)], i1_v)
        pltpu.async_copy(h2_hbm.at[pl.ds(base, W)], rows_v.at[0], sem_load)
        for i in range(n_win):
            b = i % 2
            pltpu.make_async_copy(h2_hbm.at[pl.ds(base, W)], rows_v.at[b], sem_load).wait()
            if i + 1 < n_win:
                pltpu.async_copy(h2_hbm.at[pl.ds(base + (i + 1) * W, W)], rows_v.at[1 - b], sem_load)
            c0 = pltpu.async_copy(rows_v.at[b], xs_hbm.at[i0_v.at[i]], sem_store)
            c1 = pltpu.async_copy(rows_v.at[b], xs_hbm.at[i1_v.at[i]], sem_store)
            c0.wait()
            c1.wait()

    return pl.kernel(
        body, mesh=mesh,
        out_type=jax.ShapeDtypeStruct((n_rows,) + ROW_TILE, ROW_DTYPE),
        scratch_types=[pltpu.VMEM((n_win, W), jnp.int32), pltpu.VMEM((n_win, W), jnp.int32),
                       pltpu.VMEM((2, W) + ROW_TILE, ROW_DTYPE),
                       pltpu.SemaphoreType.DMA, pltpu.SemaphoreType.DMA],
        name="moe_sc_scatter",
    )(h2, d0, d1)


ZERO_CHUNK = 64


def _finish_dispatch_kernel(zstart_ref, zchunks_ref, slots_hbm, h2_ref, xs_in_hbm, xs_hbm, idx_smem, zbuf, sem_idx,
                            sem_rows, sem_zero):
    del xs_in_hbm
    TS = h2_ref.shape[0]
    idx_cp = pltpu.make_async_copy(slots_hbm.at[pl.ds(0, IDX_CHUNK)], idx_smem, sem_idx)
    idx_cp.start()
    zbuf[...] = jnp.zeros(zbuf.shape, zbuf.dtype)

    def zero_copy(e, j):
        return pltpu.make_async_copy(zbuf, xs_hbm.at[pl.ds(zstart_ref[e] + j * ZERO_CHUNK, ZERO_CHUNK)], sem_zero)

    for e in range(N_EXPERTS):
        lax.fori_loop(0, zchunks_ref[e], lambda j, c, e=e: (zero_copy(e, j).start(), c)[1], 0)
    idx_cp.wait()

    def issue(r, carry):
        for kk in range(2):
            pltpu.make_async_copy(h2_ref.at[r], xs_hbm.at[idx_smem[2 * r + kk]], sem_rows).start(priority=kk)
        return carry

    lax.fori_loop(0, TS, issue, 0, unroll=ISSUE_UNROLL)
    for e in range(N_EXPERTS):
        lax.fori_loop(0, zchunks_ref[e], lambda j, c, e=e: (zero_copy(e, j).wait(), c)[1], 0)
    for kk in range(2):
        pltpu.make_async_copy(h2_ref, xs_hbm.at[pl.ds(0, TS)], sem_rows).wait()


def _finish_dispatch(xs, h2s, slots, zstart, zchunks):
    Bs = h2s.shape[0]
    assert 2 * Bs <= IDX_CHUNK and slots.shape[0] == IDX_CHUNK and MOE_BLOCK % ZERO_CHUNK == 0
    any_spec = pl.BlockSpec(memory_space=pl.ANY)
    grid_spec = pltpu.PrefetchScalarGridSpec(
        num_scalar_prefetch=2,
        grid=(1,),
        in_specs=[any_spec, pl.BlockSpec((Bs,) + ROW_TILE, lambda t, z, n: (0, 0, 0)), any_spec],
        out_specs=any_spec,
        scratch_shapes=[pltpu.SMEM((IDX_CHUNK,), jnp.int32), pltpu.VMEM((ZERO_CHUNK,) + ROW_TILE, ROW_DTYPE),
                        pltpu.SemaphoreType.DMA, pltpu.SemaphoreType.DMA, pltpu.SemaphoreType.DMA],
    )
    return pl.pallas_call(
        _finish_dispatch_kernel,
        grid_spec=grid_spec,
        out_shape=jax.ShapeDtypeStruct(xs.shape, ROW_DTYPE),
        input_output_aliases={4: 0},
        compiler_params=pltpu.CompilerParams(dimension_semantics=("arbitrary",), has_side_effects=True),
        name="moe_finish_dispatch",
    )(zstart, zchunks, slots, h2s, xs)


FFN_IN_BUFFERS = 4
FFN_OUT_BUFFERS = 3


def _ffn_kernel(cnt_ref, xs_hbm, w1_hbm, w3_hbm, w2_hbm, ys_hbm,
                xs_buf, ys_buf, st1, st3, st2, w13_scr, w2_scr, t_row, t_exp, t_len, sem_in, sem_out, sem_w,
                *, cap_blocks):
    B = MOE_BLOCK

    def fill_expert(e, g):
        nb = (cnt_ref[e] + (B - 1)) // B

        def fill_block(j, carry):
            t_row[g + j] = e * cap_blocks + j
            t_exp[g + j] = e
            t_len[g + j] = nb
            return carry

        lax.fori_loop(0, nb, fill_block, 0)
        return g + nb

    n_used = lax.fori_loop(0, N_EXPERTS, fill_expert, 0)

    def in_copy(g):
        s = g % FFN_IN_BUFFERS
        return pltpu.make_async_copy(xs_hbm.at[pl.ds(t_row[g] * B, B)], xs_buf.at[s], sem_in.at[s])

    def out_copy(g):
        s = g % FFN_OUT_BUFFERS
        return pltpu.make_async_copy(ys_buf.at[s], ys_hbm.at[pl.ds(t_row[g] * B, B)], sem_out.at[s])

    def weight_copies(e, s):
        return (pltpu.make_async_copy(w1_hbm.at[e], st1.at[s], sem_w.at[s]),
                pltpu.make_async_copy(w3_hbm.at[e], st3.at[s], sem_w.at[s]),
                pltpu.make_async_copy(w2_hbm.at[e], st2.at[s], sem_w.at[s]))

    @pl.when(n_used > 0)
    def _():
        for cp in weight_copies(t_exp[0], 0):
            cp.start()

    for g0 in range(FFN_IN_BUFFERS - 1):
        @pl.when(g0 < n_used)
        def _(g0=g0):
            in_copy(g0).start()

    def block(g, wslot):
        first = (g == 0) | (t_exp[g] != t_exp[jnp.maximum(g - 1, 0)])
        wslot = jnp.where(first & (g > 0), 1 - wslot, wslot)

        @pl.when(first)
        def _():
            for cp in weight_copies(t_exp[g], wslot):
                cp.wait()
            nxt = g + t_len[g]

            @pl.when(nxt < n_used)
            def _():
                for cp in weight_copies(t_exp[jnp.minimum(nxt, n_used - 1)], 1 - wslot):
                    cp.start()

            w13_scr[:, 0:D_EXPERT] = st1[wslot].astype(_BF)
            w13_scr[:, D_EXPERT:2 * D_EXPERT] = st3[wslot].astype(_BF)
            w2_scr[...] = st2[wslot].astype(_BF)

        in_copy(g).wait()

        @pl.when(g + FFN_IN_BUFFERS - 1 < n_used)
        def _():
            in_copy(g + FFN_IN_BUFFERS - 1).start()

        @pl.when(g >= FFN_OUT_BUFFERS)
        def _():
            out_copy(g - FFN_OUT_BUFFERS).wait()

        xb = _unpack_rows(xs_buf[g % FFN_IN_BUFFERS]).astype(_BF)
        ab = _dot(xb, w13_scr[...])
        hid = jax.nn.silu(ab[:, 0:D_EXPERT]) * ab[:, D_EXPERT:2 * D_EXPERT]
        ys_buf[g % FFN_OUT_BUFFERS] = _pack_rows(_dot(hid.astype(_BF), w2_scr[...]))
        out_copy(g).start()
        return wslot

    lax.fori_loop(0, n_used, block, 0)

    for back in range(FFN_OUT_BUFFERS, 0, -1):
        @pl.when(n_used >= back)
        def _(back=back):
            out_copy(n_used - back).wait()


def _ffn(xs, counts, w1, w3, w2, cap_blocks, n_blocks):
    any_spec = pl.BlockSpec(memory_space=pl.ANY)
    row_buf = lambda n: pltpu.VMEM((n, MOE_BLOCK) + ROW_TILE, ROW_DTYPE)
    grid_spec = pltpu.PrefetchScalarGridSpec(
        num_scalar_prefetch=1,
        grid=(1,),
        in_specs=[any_spec, any_spec, any_spec, any_spec],
        out_specs=any_spec,
        scratch_shapes=[row_buf(FFN_IN_BUFFERS), row_buf(FFN_OUT_BUFFERS),
                        pltpu.VMEM((2, D_MODEL, D_EXPERT), _F32), pltpu.VMEM((2, D_MODEL, D_EXPERT), _F32),
                        pltpu.VMEM((2, D_EXPERT, D_MODEL), _F32),
                        pltpu.VMEM((D_MODEL, 2 * D_EXPERT), _BF), pltpu.VMEM((D_EXPERT, D_MODEL), _BF),
                        pltpu.SMEM((n_blocks,), jnp.int32), pltpu.SMEM((n_blocks,), jnp.int32),
                        pltpu.SMEM((n_blocks,), jnp.int32),
                        pltpu.SemaphoreType.DMA((FFN_IN_BUFFERS,)), pltpu.SemaphoreType.DMA((FFN_OUT_BUFFERS,)),
                        pltpu.SemaphoreType.DMA((2,))],
    )
    return pl.pallas_call(
        functools.partial(_ffn_kernel, cap_blocks=cap_blocks),
        grid_spec=grid_spec,
        out_shape=jax.ShapeDtypeStruct(xs.shape, ROW_DTYPE),
        compiler_params=pltpu.CompilerParams(dimension_semantics=("arbitrary",), vmem_limit_bytes=VMEM_LIMIT,
                                             has_side_effects=True),
        name="moe_ffn",
    )(counts, xs, w1, w3, w2)


def _combine_kernel(slots_hbm, x1_ref, rw_ref, g_ref, ys_hbm, y_ref, idx_smem, buf, sem_idx, sem_rows):
    t = pl.program_id(0)
    TC = x1_ref.shape[0]
    idx_cp = pltpu.make_async_copy(slots_hbm.at[pl.ds(t * IDX_CHUNK, IDX_CHUNK)], idx_smem, sem_idx)
    idx_cp.start()
    idx_cp.wait()

    def issue(r, carry):
        for kk in range(2):
            pltpu.make_async_copy(ys_hbm.at[idx_smem[2 * r + kk]], buf.at[kk, r], sem_rows).start(priority=kk)
        return carry

    lax.fori_loop(0, TC, issue, 0, unroll=ISSUE_UNROLL)
    for kk in range(2):
        pltpu.make_async_copy(ys_hbm.at[pl.ds(0, TC)], buf.at[kk], sem_rows).wait()

    w = rw_ref[...]
    y0 = _unpack_rows(buf[0])
    y1 = _unpack_rows(buf[1])
    x2 = x1_ref[...] + (w[:, 0:1] * y0 + w[:, 1:2] * y1)
    y_ref[...] = _rms(x2, g_ref[...])


SC_GATHER_CHUNKS = 4
DENSE_TILE = 1024
SC_GATHER_BUFFERS = 3
SC_GATHER_MAX_WINDOW = 80


def _sc_gather(ys, rows):
    Tg = rows.shape[1]
    n_workers = SC_CORES * SC_SUBCORES
    per_w = Tg // n_workers
    assert per_w * n_workers == Tg and per_w % SUBLANES == 0
    W = max(w for w in range(SUBLANES, SC_GATHER_MAX_WINDOW + 1, SUBLANES) if per_w % w == 0)
    n_win = per_w // W
    NB = SC_GATHER_BUFFERS
    idx = rows.reshape(2, n_workers, n_win, W)
    jobs = [(kk, i) for i in range(n_win) for kk in range(2)]
    mesh = plsc.VectorSubcoreMesh(core_axis_name="c", subcore_axis_name="s")

    def body(ys_hbm, idx_hbm, out_hbm, i_v, rows_v, sem_g, sem_s):
        wid = lax.axis_index("s") * SC_CORES + lax.axis_index("c")
        base = wid * per_w
        for kk in range(2):
            pltpu.sync_copy(idx_hbm.at[kk, wid], i_v.at[kk])

        def gather(j):
            kk, i = jobs[j]
            return pltpu.make_async_copy(ys_hbm.at[i_v.at[kk, i]], rows_v.at[j % NB], sem_g.at[j % NB])

        def store(j):
            kk, i = jobs[j]
            return pltpu.make_async_copy(rows_v.at[j % NB], out_hbm.at[kk, pl.ds(base + i * W, W)], sem_s.at[j % NB])

        gather(0).start()
        for j in range(len(jobs)):
            gather(j).wait()
            store(j).start()
            if j + 1 < len(jobs):
                if j + 1 >= NB:
                    store(j + 1 - NB).wait()
                gather(j + 1).start()
        for j in range(max(0, len(jobs) - NB), len(jobs)):
            store(j).wait()

    return pl.kernel(
        body, mesh=mesh,
        out_type=jax.ShapeDtypeStruct((2, Tg) + ROW_TILE, ROW_DTYPE),
        scratch_types=[pltpu.VMEM((2, n_win, W), jnp.int32), pltpu.VMEM((NB, W) + ROW_TILE, ROW_DTYPE),
                       pltpu.SemaphoreType.DMA((NB,)), pltpu.SemaphoreType.DMA((NB,))],
        name="moe_sc_gather",
    )(ys, idx)


def _combine_dense_kernel(x1_ref, rw_ref, g_ref, rows_ref, *rest):
    y_ref = rest[-1]
    w = rw_ref[...]
    y0 = _unpack_rows(rows_ref[0])
    y1 = _unpack_rows(rows_ref[1])
    x2 = x1_ref[...] + (w[:, 0:1] * y0 + w[:, 1:2] * y1)
    y_ref[...] = _rms(x2, g_ref[...])


def _combine_dense(x1, rw, rows, y, g_final, tile, first_tile):
    Tg = rows.shape[1]
    assert Tg % tile == 0
    tok = lambda t: (first_tile + t, 0)
    in_specs = [pl.BlockSpec((tile, D_MODEL), tok), pl.BlockSpec((tile, LANES), tok), _full((1, D_MODEL)),
                pl.BlockSpec((2, tile) + ROW_TILE, lambda t: (0, t, 0, 0))]
    args = [x1, rw, g_final.reshape(1, D_MODEL), rows]
    aliases = {}
    if y is not None:
        in_specs.append(pl.BlockSpec(memory_space=pl.ANY))
        args.append(y)
        aliases = {4: 0}
    return pl.pallas_call(
        _combine_dense_kernel,
        grid=(Tg // tile,),
        in_specs=in_specs,
        out_specs=pl.BlockSpec((tile, D_MODEL), tok),
        out_shape=jax.ShapeDtypeStruct(x1.shape, _F32),
        input_output_aliases=aliases,
        compiler_params=pltpu.CompilerParams(dimension_semantics=("arbitrary",), vmem_limit_bytes=VMEM_LIMIT),
        name="moe_combine_dense",
    )(*args)


def _combine(x1, rw, slots, ys, g_final, tile, n_tiles):
    T = x1.shape[0]
    assert T % tile == 0 and 2 * tile <= IDX_CHUNK and slots.shape[0] == n_tiles * IDX_CHUNK
    any_spec = pl.BlockSpec(memory_space=pl.ANY)
    return pl.pallas_call(
        _combine_kernel,
        grid=(n_tiles,),
        in_specs=[any_spec, pl.BlockSpec((tile, D_MODEL), lambda t: (t, 0)),
                  pl.BlockSpec((tile, LANES), lambda t: (t, 0)), _full((1, D_MODEL)), any_spec],
        out_specs=pl.BlockSpec((tile, D_MODEL), lambda t: (t, 0)),
        out_shape=jax.ShapeDtypeStruct((T, D_MODEL), _F32),
        scratch_shapes=[pltpu.SMEM((IDX_CHUNK,), jnp.int32), pltpu.VMEM((2, tile) + ROW_TILE, ROW_DTYPE),
                        pltpu.SemaphoreType.DMA, pltpu.SemaphoreType.DMA],
        compiler_params=pltpu.CompilerParams(dimension_semantics=("arbitrary",), vmem_limit_bytes=VMEM_LIMIT),
        name="moe_combine",
    )(slots, x1, rw, g_final.reshape(1, D_MODEL), ys)


def _tile_for(n_tokens):
    tile = IDX_CHUNK // 2
    return tile if n_tokens % tile == 0 else n_tokens


def _chunked_slots(slot, tile):
    n_tiles = slot.shape[0] // tile
    s = slot.reshape(n_tiles, 2 * tile)
    return jnp.pad(s, ((0, 0), (0, IDX_CHUNK - 2 * tile))).reshape(-1)


def kernel(x_prompt, x_sample, state_pool, state_ret, g_mix, w_in, w_pool, pool_scale, ret_gn, w_out, g_ffn, w_grp, w_exp, w1, w3, w2, g_final):
    Bp, Lp, _ = x_prompt.shape
    Bs = x_sample.shape[0]
    Tp = Bp * Lp
    wts = _prep_weights(g_mix[0], w_pool[0], pool_scale[0], ret_gn[0], w_out[0], g_ffn[0], w_grp[0], w_exp[0])

    T_all = Tp + Bs
    cap = (-(-T_all // MOE_BLOCK) + 1) * MOE_BLOCK
    cap_blocks = cap // MOE_BLOCK

    xs_tok = x_sample.reshape(Bs, D_MODEL)
    wts["w_in"], u_s, q_s, k_s, kz_s, v_s, ga_s, gb_s = _sample_proj(xs_tok, wts["g_mix"], w_in[0], PAST_LEN)
    n_pairs = Bs * RET_HEADS
    nret_s = _sc_state_update(state_ret[0].reshape(n_pairs, RET_DK, RET_DV), kz_s.reshape(n_pairs, RET_DK),
                              v_s.reshape(n_pairs, RET_DV)).reshape(state_ret.shape[1:])
    x1p, dst_rows_p, rwp, npool_p, nret_p, counts_p, h2p = _mixer_prompt(x_prompt, wts, cap)
    xs = _sc_scatter(h2p, dst_rows_p, N_EXPERTS * cap)
    x1s, dst_s, rws, npool_s, counts, h2s, nret_s = _mixer_sample(
        xs_tok, (u_s, q_s, k_s, v_s, ga_s, gb_s), jnp.swapaxes(state_pool[0], 0, 1), state_ret[0], nret_s, wts,
        counts_p, cap)

    counts = counts[0, :N_EXPERTS].astype(jnp.int32)
    tile_p, tile_s = _tile_for(Tp), _tile_for(Bs)
    slots_s = _chunked_slots(dst_s[:, :2], tile_s)
    zstart = jnp.arange(N_EXPERTS, dtype=jnp.int32) * cap + counts
    zchunks = ((-counts) % MOE_BLOCK + ZERO_CHUNK - 1) // ZERO_CHUNK
    xs = _finish_dispatch(xs, h2s, slots_s, zstart, zchunks.astype(jnp.int32))
    n_blocks = -(-2 * T_all // MOE_BLOCK) + N_EXPERTS
    ys = _ffn(xs, counts, w1[0], w3[0], w2[0], cap_blocks, n_blocks)
    tile_d = DENSE_TILE if Tp % DENSE_TILE == 0 else tile_p
    n_tiles_d = Tp // tile_d
    chunk_tiles = [n_tiles_d // SC_GATHER_CHUNKS + (k < n_tiles_d % SC_GATHER_CHUNKS)
                   for k in range(SC_GATHER_CHUNKS)]
    gathered, first = [], 0
    for nt in chunk_tiles:
        if nt:
            gathered.append((first, _sc_gather(ys, dst_rows_p[:, first * tile_d:(first + nt) * tile_d])))
            first += nt
    y_s = _combine(x1s, rws, slots_s, ys, g_final, tile_s, 1)
    y_p = None
    for first, rows in gathered:
        y_p = _combine_dense(x1p, rwp, rows, y_p, g_final, tile_d, first)

    return (y_p.reshape(Bp, Lp, D_MODEL), y_s.reshape(Bs, 1, D_MODEL),
            npool_p[None], nret_p[None], jnp.swapaxes(npool_s, 0, 1)[None], nret_s[None])
```

```python
import dataclasses
import functools

import jax
import jax.numpy as jnp
import numpy as np
from jax import lax
from jax.experimental import pallas as pl
from jax.experimental.pallas import tpu as pltpu
from jax.experimental.pallas import tpu_sc as plsc

D_MODEL = 1024
EPS = 1e-6
POOL_GROUPS = 4
POOL_IN = D_MODEL // 2
POOL_GC = POOL_IN // POOL_GROUPS
POOL_OUT_GC = D_MODEL // POOL_GROUPS
POOL_WINDOWS = (2, 4, 8, 16)
POOL_HIST = max(POOL_WINDOWS) - 1
RET_HEADS = 4
RET_DK = D_MODEL // 8
RET_DV = D_MODEL // RET_HEADS
ROPE_BASE = 10000.0
PAST_LEN = 16384
N_GROUPS = 4
EXPERTS_PER_GROUP = 8
N_EXPERTS = N_GROUPS * EXPERTS_PER_GROUP
D_EXPERT = D_MODEL // 4
QK_W = RET_HEADS * RET_DK
V_W = RET_HEADS * RET_DV
OFF_U = 0
OFF_Q = POOL_IN
OFF_K = OFF_Q + QK_W
OFF_V = OFF_K + QK_W
OFF_GA = OFF_V + V_W
OFF_GB = OFF_GA + D_MODEL
IN_WIDTH = OFF_GB + D_MODEL

LANES = 128
SUBLANES = 8
ROW_WORDS = D_MODEL // 2
ROW_TILE = (ROW_WORDS // LANES, LANES)
ROW_DTYPE = jnp.uint32
HALO = 16
RET_CHUNK = 256
MIXER_SEQS = 2
MOE_BLOCK = 512
ROUTER_W = LANES
RANK_GROUP = 128
VMEM_LIMIT = 56 * 1024 * 1024

_BF = jnp.bfloat16
_F32 = jnp.float32


def _rms(x, g):
    inv = lax.rsqrt(jnp.mean(x * x, axis=-1, keepdims=True) + EPS)
    return x * inv * g


def _dot(a, b):
    return jnp.dot(a, b, preferred_element_type=_F32)


def _round_bf16(x):
    return x.astype(_BF).astype(_F32)


def _pack_rows(x):
    lo = lax.bitcast_convert_type(x[:, :ROW_WORDS].astype(_BF).astype(_F32), jnp.uint32)
    hi = lax.bitcast_convert_type(x[:, ROW_WORDS:].astype(_BF).astype(_F32), jnp.uint32)
    return ((lo >> 16) | hi).reshape((x.shape[0],) + ROW_TILE)


def _unpack_rows(w):
    w = w.reshape(w.shape[0], ROW_WORDS)
    lo = lax.bitcast_convert_type(w << 16, _F32)
    hi = lax.bitcast_convert_type(w & jnp.uint32(0xFFFF0000), _F32)
    return jnp.concatenate([lo, hi], axis=1)


def _rotary(x, cos2, sin2):
    return x * cos2 + pltpu.roll(x, RET_DK // 2, 1) * sin2


def _route(logits):
    lane = lax.broadcasted_iota(jnp.int32, logits.shape, 1).astype(_F32)
    neg = jnp.float32(-jnp.inf)
    big = jnp.float32(1 << 20)
    lg = jnp.where(lane < N_GROUPS, logits, neg)
    mg = jnp.max(lg, axis=-1, keepdims=True)
    g_idx = jnp.min(jnp.where(lg == mg, lane, big), axis=-1, keepdims=True)
    p_g = 1.0 / jnp.sum(jnp.exp(lg - mg), axis=-1, keepdims=True)
    lo = N_GROUPS + g_idx * EXPERTS_PER_GROUP
    in_grp = (lane >= lo) & (lane < lo + EXPERTS_PER_GROUP)
    le = jnp.where(in_grp, logits, neg)
    m1 = jnp.max(le, axis=-1, keepdims=True)
    i1 = jnp.min(jnp.where(le == m1, lane, big), axis=-1, keepdims=True)
    le2 = jnp.where(lane == i1, neg, le)
    m2 = jnp.max(le2, axis=-1, keepdims=True)
    i2 = jnp.min(jnp.where(le2 == m2, lane, big), axis=-1, keepdims=True)
    t = jnp.exp(m2 - m1)
    den = 1.0 + t
    e0 = (i1 - N_GROUPS).astype(jnp.int32)
    e1 = (i2 - N_GROUPS).astype(jnp.int32)
    return e0, e1, p_g * (1.0 / den), p_g * (t / den)


def _post_mix(x, mixed_ref, w_out_ref, g_ffn_ref, w_rt_ref, x1_ref, rw_ref):
    x1 = x + _dot(mixed_ref[...], w_out_ref[...])
    x1_ref[...] = x1.reshape(x1_ref.shape)
    h2 = _rms(x1, g_ffn_ref[...])
    e0, e1, w0, w1 = _route(_dot(h2.astype(_BF), w_rt_ref[...]))
    lane = lax.broadcasted_iota(jnp.int32, (x.shape[0], LANES), 1)
    rw_ref[...] = jnp.where(lane == 0, w0, jnp.where(lane == 1, w1, 0.0)).reshape(rw_ref.shape)
    return h2, e0, e1


def _route_block(x1, g_ffn_ref, w_rt_ref, carry_scr, cap, valid, h2_ref, rw_ref, dst_t_ref):
    for _ in _route_stages(x1, g_ffn_ref, w_rt_ref, carry_scr, cap, valid, h2_ref, rw_ref, dst_t_ref):
        pass


def _route_stages(x1, g_ffn_ref, w_rt_ref, carry_scr, cap, valid, h2_ref, rw_ref, dst_t_ref):
    h2 = _rms(x1, g_ffn_ref[...])
    logits = _dot(h2.astype(_BF), w_rt_ref[...])
    h2_ref[...] = _pack_rows(h2).reshape(h2_ref.shape)
    yield
    e0, e1, w0, w1 = _route(logits)
    lane = lax.broadcasted_iota(jnp.int32, (x1.shape[0], LANES), 1)
    rw_ref[...] = jnp.where(lane == 0, w0, jnp.where(lane == 1, w1, 0.0)).reshape(rw_ref.shape)
    yield
    dst = _rank_rows(e0, e1, cap, carry_scr, valid)
    dst_t = dst.T[0:SUBLANES, :].astype(jnp.int32)
    n_parts, _, width = dst_t_ref.shape
    for part in range(n_parts):
        dst_t_ref[part] = dst_t[:, part * width:(part + 1) * width]
    yield


def _rank_rows(e0, e1, cap, carry_scr, valid=None):
    R = e0.shape[0]
    lane = lax.broadcasted_iota(jnp.int32, (R, LANES), 1)
    m0 = lane == e0
    m1 = lane == e1
    onehot = jnp.where(m0 | m1, 1.0, 0.0)
    G = min(RANK_GROUP, R)
    r_i = lax.broadcasted_iota(jnp.int32, (G, G), 0)
    c_i = lax.broadcasted_iota(jnp.int32, (G, G), 1)
    tri = jnp.where(c_i < r_i, 1.0, 0.0).astype(_BF)
    running = carry_scr[...]
    parts = []
    for g in range(R // G):
        grp = onehot[g * G:(g + 1) * G]
        parts.append(_dot(tri, grp.astype(_BF)) + running)
        running = running + jnp.sum(grp, axis=0, keepdims=True)
    before = jnp.concatenate(parts, axis=0)
    d0 = e0.astype(_F32) * cap + jnp.sum(jnp.where(m0, before, 0.0), axis=-1, keepdims=True)
    d1 = e1.astype(_F32) * cap + jnp.sum(jnp.where(m1, before, 0.0), axis=-1, keepdims=True)
    added = running - carry_scr[...]
    carry_scr[...] += added if valid is None else added * valid
    return jnp.where(lane == 0, d0, jnp.where(lane == 1, d1, 0.0))


def _group_norm(o, gain):
    mu = jnp.mean(o, axis=-1, keepdims=True)
    d = o - mu
    var = jnp.mean(d * d, axis=-1, keepdims=True)
    return d * lax.rsqrt(var + EPS) * gain


def _mixer_prompt_kernel(x_ref, g_mix_ref, w_in_ref, w_pool_ref, pscale_ref, gn_ref, w_out_ref, g_ffn_ref,
                         w_rt_ref, cos_ref, sin_ref, dmask_ref, xi_ref, zeta_ref, gc_ref,
                         x1_ref, dst_t_ref, rw_ref, npool_ref, nret_ref, cnt_ref, h2_ref,
                         u_scr, s_scr, mixed_scr, carry_scr, z_scr, x1_prev_scr, *, cap):
    c = pl.program_id(1)
    n_c = pl.num_programs(1)
    step = pl.program_id(0) * n_c + c
    NSEQ, C, _ = x_ref.shape
    R = NSEQ * C

    @pl.when(step == 0)
    def _():
        carry_scr[...] = jnp.zeros(carry_scr.shape, _F32)
        x1_prev_scr[...] = jnp.zeros(x1_prev_scr.shape, _F32)

    @pl.when(c == 0)
    def _():
        u_scr[:, 0:HALO, :] = jnp.zeros((NSEQ, HALO, POOL_IN), _F32)
        s_scr[...] = jnp.zeros(s_scr.shape, _F32)

    x = x_ref[...].reshape(R, D_MODEL)
    h = _rms(x, g_mix_ref[...]).astype(_BF)

    stages = _route_stages(x1_prev_scr[...], g_ffn_ref, w_rt_ref, carry_scr, cap, (step > 0).astype(_F32),
                           h2_ref, rw_ref, dst_t_ref)
    for lo, hi in ((0, OFF_V), (OFF_V, OFF_GA), (OFF_GA, OFF_GB), (OFF_GB, IN_WIDTH)):
        z_scr[:, lo:hi] = _dot(h, w_in_ref[:, lo:hi])
        next(stages, None)
    cnt_ref[...] = carry_scr[...]

    u_scr[:, HALO:HALO + C, :] = z_scr[:, OFF_U:OFF_U + POOL_IN].reshape(NSEQ, C, POOL_IN)
    q = z_scr[:, OFF_Q:OFF_Q + QK_W]
    k = z_scr[:, OFF_K:OFF_K + QK_W]
    cos2 = jnp.concatenate([cos_ref[...]] * NSEQ, axis=0)
    sin2 = jnp.concatenate([sin_ref[...]] * NSEQ, axis=0)
    pos1 = (c * C + 1 + lax.broadcasted_iota(jnp.int32, (C, POOL_GC), 0)).astype(_F32)

    for j in range(RET_HEADS):
        win = POOL_WINDOWS[j]
        cs = slice(j * POOL_GC, (j + 1) * POOL_GC)
        n_rows = jnp.minimum(pos1, jnp.float32(win))
        pooled = []
        for sq in range(NSEQ):
            u_j = u_scr[sq, HALO:HALO + C, cs]
            s = u_j
            for d in range(1, win):
                s = s + u_scr[sq, HALO - d:HALO - d + C, cs]
            pooled.append(s / n_rows - u_j)
        osl = slice(j * POOL_OUT_GC, (j + 1) * POOL_OUT_GC)
        pool_out = _dot(jnp.concatenate(pooled, axis=0).astype(_BF), w_pool_ref[j]) * pscale_ref[:, osl]

        qs = slice(j * RET_DK, (j + 1) * RET_DK)
        qb = _rotary(q[:, qs], cos2, sin2).astype(_BF)
        kf = _rotary(k[:, qs], cos2, sin2) * (RET_DK ** -0.5)
        kb = kf.astype(_BF)
        vb = z_scr[:, OFF_V + j * RET_DV:OFF_V + (j + 1) * RET_DV].astype(_BF)
        ret = []
        for sq in range(NSEQ):
            rows = slice(sq * C, (sq + 1) * C)
            scores = lax.dot_general(qb[rows], kb[rows], (((1,), (1,)), ((), ())),
                                     preferred_element_type=_F32) * dmask_ref[j]
            s_old = s_scr[sq, j]
            o = _dot(scores.astype(_BF), vb[rows]) + _dot(qb[rows], s_old.astype(_BF)) * xi_ref[j]
            kz = (kf[rows] * zeta_ref[j]).astype(_BF)
            s_scr[sq, j] = gc_ref[j] * s_old + lax.dot_general(kz, vb[rows], (((0,), (0,)), ((), ())),
                                                               preferred_element_type=_F32)
            ret.append(_group_norm(o, gn_ref[:, osl]))
        ret_out = jnp.concatenate(ret, axis=0)

        ga = z_scr[:, OFF_GA + j * RET_DV:OFF_GA + (j + 1) * RET_DV]
        gb = z_scr[:, OFF_GB + j * RET_DV:OFF_GB + (j + 1) * RET_DV]
        mixed_scr[:, osl] = (jax.nn.sigmoid(ga) * pool_out + jax.nn.sigmoid(gb) * ret_out).astype(_BF)

    x1 = x + _dot(mixed_scr[...], w_out_ref[...])
    x1_ref[...] = x1.reshape(x1_ref.shape)
    x1_prev_scr[...] = x1

    u_scr[:, 0:HALO, :] = u_scr[:, C:C + HALO, :]

    @pl.when(c == n_c - 1)
    def _():
        npool_ref[...] = u_scr[:, HALO + C - POOL_HIST:HALO + C, :]
        nret_ref[...] = s_scr[...]


def _decay_tables(C):
    f32 = np.float32
    log_g = np.log(f32(1.0) - np.exp2(f32(-5.0) - np.arange(RET_HEADS, dtype=f32)))
    i = np.arange(C, dtype=f32)
    diff = i[:, None] - i[None, :]
    dmask = np.where(diff >= 0, np.exp(np.maximum(diff, f32(0.0))[None] * log_g[:, None, None]), f32(0.0))
    xi = np.exp((i[None, :] + f32(1.0)) * log_g[:, None])
    zeta = np.exp((f32(C) - f32(1.0) - i)[None, :] * log_g[:, None])
    g_chunk = np.exp(f32(C) * log_g)
    return dmask.astype(f32), xi.astype(f32), zeta.astype(f32), g_chunk.astype(f32)


def _rope_tables(pos):
    half = RET_DK // 2
    freqs = ROPE_BASE ** (-jnp.arange(half, dtype=_F32) / half)
    ang = pos[:, None] * freqs[None, :]
    cos, sin = jnp.cos(ang), jnp.sin(ang)
    return jnp.concatenate([cos, cos], axis=-1), jnp.concatenate([-sin, sin], axis=-1)


def _const(a, shape=None):
    if shape is not None:
        a = np.ascontiguousarray(np.broadcast_to(a, shape))
    return jnp.asarray(a)


def _full(shape):
    n = len(shape)
    return pl.BlockSpec(shape, lambda *_: (0,) * n)


def _route_tail_kernel(x1_ref, g_ffn_ref, w_rt_ref, cnt_in_ref, dst_t_in, rw_in, h2_in,
                       dst_t_ref, rw_ref, cnt_ref, h2_ref, carry_scr, *, cap):
    del dst_t_in, rw_in, h2_in
    carry_scr[...] = cnt_in_ref[...]
    x1 = x1_ref[...]
    _route_block(x1.reshape(x1.shape[0] * x1.shape[1], D_MODEL), g_ffn_ref, w_rt_ref, carry_scr, cap, None,
                 h2_ref, rw_ref, dst_t_ref)
    cnt_ref[...] = carry_scr[...]


def _route_tail(x1, dst_t, rw, counts, h2, wts, cap, C, last):
    NSEQ = x1.shape[0]
    any_spec = pl.BlockSpec(memory_space=pl.ANY)
    return pl.pallas_call(
        functools.partial(_route_tail_kernel, cap=cap),
        grid=(1,),
        in_specs=[pl.BlockSpec((NSEQ, C, D_MODEL), lambda i: (0, last, 0)), _full((1, D_MODEL)),
                  _full((D_MODEL, ROUTER_W)), _full((1, LANES)), any_spec, any_spec, any_spec],
        out_specs=(pl.BlockSpec((NSEQ, SUBLANES, C), lambda i: (0, 0, last)),
                   pl.BlockSpec((NSEQ, C, LANES), lambda i: (0, last, 0)),
                   _full((1, LANES)),
                   pl.BlockSpec((NSEQ, C) + ROW_TILE, lambda i: (0, last, 0, 0))),
        out_shape=(jax.ShapeDtypeStruct(dst_t.shape, dst_t.dtype), jax.ShapeDtypeStruct(rw.shape, rw.dtype),
                   jax.ShapeDtypeStruct((1, LANES), _F32), jax.ShapeDtypeStruct(h2.shape, h2.dtype)),
        input_output_aliases={4: 0, 5: 1, 6: 3},
        scratch_shapes=[pltpu.VMEM((1, LANES), _F32)],
        compiler_params=pltpu.CompilerParams(dimension_semantics=("arbitrary",), vmem_limit_bytes=VMEM_LIMIT),
        name="mixer_route_tail",
    )(x1, wts["g_ffn"], wts["w_rt"], counts, dst_t, rw, h2)


def _mixer_prompt(x, wts, cap):
    B, L, _ = x.shape
    C = RET_CHUNK if L % RET_CHUNK == 0 else L
    n_c = L // C
    T = B * L
    dmask, xi, zeta, g_chunk = _decay_tables(C)
    dmask = _const(dmask)
    xi_b = _const(xi[:, :, None], (RET_HEADS, C, RET_DV))
    zeta_b = _const(zeta[:, :, None], (RET_HEADS, C, RET_DK))
    gc_b = _const(g_chunk[:, None, None], (RET_HEADS, 1, RET_DV))
    cos2, sin2 = _rope_tables(jnp.arange(L).astype(_F32))

    NSEQ = MIXER_SEQS if B % MIXER_SEQS == 0 else 1
    B2 = B // NSEQ
    T2 = T // NSEQ
    x = x.reshape(NSEQ, B2, L, D_MODEL)
    tok = lambda b, c: (0, b * n_c + c, 0)
    in_specs = [
        pl.BlockSpec((NSEQ, None, C, D_MODEL), lambda b, c: (0, b, c, 0)),
        _full((1, D_MODEL)), _full((D_MODEL, IN_WIDTH)), _full((POOL_GROUPS, POOL_GC, POOL_OUT_GC)),
        _full((1, D_MODEL)), _full((1, D_MODEL)), _full((D_MODEL, D_MODEL)), _full((1, D_MODEL)),
        _full((D_MODEL, ROUTER_W)),
        pl.BlockSpec((C, RET_DK), lambda b, c: (c, 0)), pl.BlockSpec((C, RET_DK), lambda b, c: (c, 0)),
        _full((RET_HEADS, C, C)), _full((RET_HEADS, C, RET_DV)), _full((RET_HEADS, C, RET_DK)),
        _full((RET_HEADS, 1, RET_DV)),
    ]
    R = NSEQ * C
    out_shape = (
        jax.ShapeDtypeStruct((NSEQ, T2, D_MODEL), _F32),
        jax.ShapeDtypeStruct((NSEQ, SUBLANES, T2), jnp.int32),
        jax.ShapeDtypeStruct((NSEQ, T2, LANES), _F32),
        jax.ShapeDtypeStruct((NSEQ, B2, POOL_HIST, POOL_IN), _F32),
        jax.ShapeDtypeStruct((NSEQ, B2, RET_HEADS, RET_DK, RET_DV), _F32),
        jax.ShapeDtypeStruct((1, LANES), _F32),
        jax.ShapeDtypeStruct((NSEQ, T2) + ROW_TILE, ROW_DTYPE),
    )
    prev = lambda b, c: jnp.maximum(b * n_c + c - 1, 0)
    out_specs = (
        pl.BlockSpec((NSEQ, C, D_MODEL), tok),
        pl.BlockSpec((NSEQ, SUBLANES, C), lambda b, c: (0, 0, prev(b, c))),
        pl.BlockSpec((NSEQ, C, LANES), lambda b, c: (0, prev(b, c), 0)),
        pl.BlockSpec((NSEQ, None, POOL_HIST, POOL_IN), lambda b, c: (0, b, 0, 0)),
        pl.BlockSpec((NSEQ, None, RET_HEADS, RET_DK, RET_DV), lambda b, c: (0, b, 0, 0, 0)),
        _full((1, LANES)),
        pl.BlockSpec((NSEQ, C) + ROW_TILE, lambda b, c: (0, prev(b, c), 0, 0)),
    )
    x1, dst_t, rw, npool, nret, counts, h2 = pl.pallas_call(
        functools.partial(_mixer_prompt_kernel, cap=cap),
        grid=(B2, n_c),
        in_specs=in_specs,
        out_specs=out_specs,
        out_shape=out_shape,
        scratch_shapes=[pltpu.VMEM((NSEQ, HALO + C, POOL_IN), _F32),
                        pltpu.VMEM((NSEQ, RET_HEADS, RET_DK, RET_DV), _F32),
                        pltpu.VMEM((R, D_MODEL), _BF),
                        pltpu.VMEM((1, LANES), _F32),
                        pltpu.VMEM((R, IN_WIDTH), _F32),
                        pltpu.VMEM((R, D_MODEL), _F32)],
        compiler_params=pltpu.CompilerParams(dimension_semantics=("arbitrary", "arbitrary"),
                                             vmem_limit_bytes=VMEM_LIMIT),
        name="mixer_prompt",
    )(x, wts["g_mix"], wts["w_in"], wts["w_pool"], wts["pool_scale"], wts["ret_gn"], wts["w_out"],
      wts["g_ffn"], wts["w_rt"], cos2, sin2, dmask, xi_b, zeta_b, gc_b)
    dst_t, rw, counts, h2 = _route_tail(x1, dst_t, rw, counts, h2, wts, cap, C, B2 * n_c - 1)
    dst_rows = jnp.moveaxis(dst_t[:, 0:2, :], 1, 0).reshape(2, T)
    return (x1.reshape(T, D_MODEL), dst_rows, rw.reshape(T, LANES),
            npool.reshape(B, POOL_HIST, POOL_IN), nret.reshape(B, RET_HEADS, RET_DK, RET_DV), counts,
            h2.reshape((T,) + ROW_TILE))


def _prep_weights(g_mix, w_pool, pool_scale, ret_gn, w_out, g_ffn, w_grp, w_exp):
    w_rt = jnp.concatenate([w_grp, w_exp.reshape(D_MODEL, N_EXPERTS)], axis=1)
    w_rt = jnp.pad(w_rt, ((0, 0), (0, ROUTER_W - w_rt.shape[1])))
    row = lambda v: v.reshape(1, D_MODEL)
    return dict(g_mix=row(g_mix), w_pool=w_pool.astype(_BF), pool_scale=row(pool_scale),
                ret_gn=row(ret_gn), w_out=w_out.astype(_BF), g_ffn=row(g_ffn), w_rt=w_rt.astype(_BF))


PROJ_COLS = 512


def _sample_proj_kernel(x_ref, g_mix_ref, w_in_ref, cos_ref, sin_ref, zeta_ref,
                        w_bf_ref, u_ref, q_ref, k_ref, kq_ref, v_ref, ga_ref, gb_ref, h_scr, z_scr):
    c = pl.program_id(0)

    @pl.when(c == 0)
    def _():
        h_scr[...] = _rms(x_ref[...], g_mix_ref[...]).astype(_BF)

    w = w_in_ref[...].astype(_BF)
    w_bf_ref[...] = w
    z_scr[c] = _dot(h_scr[...], w)

    @pl.when(c == pl.num_programs(0) - 1)
    def _():
        def cols(off, width):
            parts = [z_scr[i] for i in range(off // PROJ_COLS, (off + width) // PROJ_COLS)]
            return parts[0] if len(parts) == 1 else jnp.concatenate(parts, axis=1)

        u_ref[...] = cols(OFF_U, POOL_IN)
        q = cols(OFF_Q, QK_W)
        k = cols(OFF_K, QK_W)
        for j in range(RET_HEADS):
            qs = slice(j * RET_DK, (j + 1) * RET_DK)
            qf = _rotary(q[:, qs], cos_ref[...], sin_ref[...])
            kf = _rotary(k[:, qs], cos_ref[...], sin_ref[...]) * (RET_DK ** -0.5)
            q_ref[:, qs] = qf
            k_ref[:, qs] = kf
            kq_ref[0, :, qs] = kf * zeta_ref[j]
            kq_ref[1, :, qs] = _round_bf16(qf)
        v_ref[...] = cols(OFF_V, V_W)
        ga_ref[...] = cols(OFF_GA, D_MODEL)
        gb_ref[...] = cols(OFF_GB, D_MODEL)


def _sample_proj(x, g_mix, w_in, pos0):
    Bs = x.shape[0]
    offsets = (OFF_U, OFF_Q, OFF_K, OFF_V, OFF_GA, OFF_GB, IN_WIDTH)
    assert all(o % PROJ_COLS == 0 for o in offsets)
    n_c = IN_WIDTH // PROJ_COLS
    _, _, zeta, _ = _decay_tables(1)
    cos2, sin2 = _rope_tables((pos0 + jnp.arange(1)).astype(_F32))
    shapes = ((Bs, POOL_IN), (Bs, QK_W), (Bs, QK_W), (2, Bs, QK_W), (Bs, V_W), (Bs, D_MODEL), (Bs, D_MODEL))
    return pl.pallas_call(
        _sample_proj_kernel,
        grid=(n_c,),
        in_specs=[_full((Bs, D_MODEL)), _full((1, D_MODEL)), pl.BlockSpec((D_MODEL, PROJ_COLS), lambda c: (0, c)),
                  _full((1, RET_DK)), _full((1, RET_DK)), _full((RET_HEADS, 1, 1))],
        out_specs=(pl.BlockSpec((D_MODEL, PROJ_COLS), lambda c: (0, c)),) + tuple(_full(sh) for sh in shapes),
        out_shape=(jax.ShapeDtypeStruct((D_MODEL, IN_WIDTH), _BF),)
        + tuple(jax.ShapeDtypeStruct(sh, _F32) for sh in shapes),
        scratch_shapes=[pltpu.VMEM((Bs, D_MODEL), _BF), pltpu.VMEM((n_c, Bs, PROJ_COLS), _F32)],
        compiler_params=pltpu.CompilerParams(dimension_semantics=("arbitrary",), vmem_limit_bytes=VMEM_LIMIT),
        name="sample_proj",
    )(x, g_mix, w_in, cos2, sin2, _const(zeta[:, :, None], (RET_HEADS, 1, 1)))


def _mixer_sample_kernel(x_ref, u_ref, q_ref, k_ref, v_ref, ga_ref, gb_ref, spool_ref, qs_ref,
                         w_pool_ref, pscale_ref, gn_ref, w_out_ref, g_ffn_ref, w_rt_ref, dm_ref, xi_ref,
                         carry_in_ref,
                         x1_ref, dst_ref, rw_ref, npool_ref, cnt_ref, h2_ref,
                         mixed_scr, carry_scr, *, cap):
    u = u_ref[...]
    npool_ref[0:POOL_HIST - 1] = spool_ref[1:POOL_HIST]
    npool_ref[POOL_HIST - 1] = u
    for j in range(RET_HEADS):
        cs = slice(j * POOL_GC, (j + 1) * POOL_GC)
        qs = slice(j * RET_DK, (j + 1) * RET_DK)
        osl = slice(j * POOL_OUT_GC, (j + 1) * POOL_OUT_GC)
        win = POOL_WINDOWS[j]
        s = u[:, cs]
        for r in range(POOL_HIST - (win - 1), POOL_HIST):
            s = s + spool_ref[r, :, cs]
        pooled = s / jnp.float32(win) - u[:, cs]
        pool_out = _dot(pooled.astype(_BF), w_pool_ref[j]) * pscale_ref[:, osl]
        score = jnp.sum(q_ref[:, qs] * k_ref[:, qs], axis=-1, keepdims=True) * dm_ref[j]
        o = score * v_ref[:, osl] + qs_ref[:, osl] * xi_ref[j]
        ret_out = _group_norm(o, gn_ref[:, osl])
        mixed_scr[:, osl] = (jax.nn.sigmoid(ga_ref[:, osl]) * pool_out
                             + jax.nn.sigmoid(gb_ref[:, osl]) * ret_out).astype(_BF)
    h2, e0, e1 = _post_mix(x_ref[...], mixed_scr, w_out_ref, g_ffn_ref, w_rt_ref, x1_ref, rw_ref)
    h2_ref[...] = _pack_rows(h2).reshape(h2_ref.shape)
    carry_scr[...] = carry_in_ref[...]
    dst_ref[...] = _rank_rows(e0, e1, cap, carry_scr).astype(jnp.int32)
    cnt_ref[...] = carry_scr[...]


def _mixer_sample(x, proj, state_pool, q_state, wts, carry_in, cap):
    Bs = x.shape[0]
    assert Bs == LANES and POOL_GROUPS == RET_HEADS
    u, q, k, v, ga, gb = proj
    dmask, xi, _, _ = _decay_tables(1)
    dm_b = _const(dmask, (RET_HEADS, 1, 1))
    xi_b = _const(xi[:, :, None], (RET_HEADS, 1, RET_DV))

    in_specs = [
        _full((Bs, D_MODEL)),
        _full((Bs, POOL_IN)), _full((Bs, QK_W)), _full((Bs, QK_W)), _full((Bs, V_W)),
        _full((Bs, D_MODEL)), _full((Bs, D_MODEL)),
        _full((POOL_HIST, Bs, POOL_IN)), _full((Bs, V_W)),
        _full((POOL_GROUPS, POOL_GC, POOL_OUT_GC)),
        _full((1, D_MODEL)), _full((1, D_MODEL)), _full((D_MODEL, D_MODEL)), _full((1, D_MODEL)),
        _full((D_MODEL, ROUTER_W)),
        _full((RET_HEADS, 1, 1)), _full((RET_HEADS, 1, RET_DV)),
        _full((1, LANES)),
    ]
    out_shape = (
        jax.ShapeDtypeStruct((Bs, D_MODEL), _F32),
        jax.ShapeDtypeStruct((Bs, LANES), jnp.int32),
        jax.ShapeDtypeStruct((Bs, LANES), _F32),
        jax.ShapeDtypeStruct((POOL_HIST, Bs, POOL_IN), _F32),
        jax.ShapeDtypeStruct((1, LANES), _F32),
        jax.ShapeDtypeStruct((Bs,) + ROW_TILE, ROW_DTYPE),
    )
    out_specs = (
        _full((Bs, D_MODEL)), _full((Bs, LANES)), _full((Bs, LANES)), _full((POOL_HIST, Bs, POOL_IN)),
        _full((1, LANES)), _full((Bs,) + ROW_TILE),
    )
    return pl.pallas_call(
        functools.partial(_mixer_sample_kernel, cap=cap),
        grid=(1,),
        in_specs=in_specs,
        out_specs=out_specs,
        out_shape=out_shape,
        scratch_shapes=[pltpu.VMEM((Bs, D_MODEL), _BF), pltpu.VMEM((1, LANES), _F32)],
        compiler_params=pltpu.CompilerParams(dimension_semantics=("arbitrary",), vmem_limit_bytes=VMEM_LIMIT),
        name="mixer_sample",
    )(x, u, q, k, v, ga, gb, state_pool, q_state, wts["w_pool"], wts["pool_scale"], wts["ret_gn"],
      wts["w_out"], wts["g_ffn"], wts["w_rt"], dm_b, xi_b, carry_in)


SC_CORES = 2
SC_SUBCORES = 16
SC_LANES = 16
STATE_ROWS = 32
STATE_UNROLL = LANES // SC_LANES
STATE_COLS = 8


def _sc_round_bf16(x):
    b = lax.bitcast_convert_type(x, jnp.uint32)
    r = b + jnp.uint32(0x7FFF) + ((b >> 16) & jnp.uint32(1))
    return lax.bitcast_convert_type(r & jnp.uint32(0xFFFF0000), _F32)


def _sc_state_update(s0, kq, v):
    P = s0.shape[0]
    n_workers = SC_CORES * SC_SUBCORES
    R = STATE_ROWS
    n_parts = RET_DK // R
    n_vc = RET_DV // SC_LANES
    assert P % n_workers == 0 and RET_DK % R == 0 and n_parts % 2 == 0 and n_parts >= 2
    assert R % STATE_UNROLL == 0 and n_vc % STATE_COLS == 0
    per_w = P // n_workers
    _, _, _, g_chunk = _decay_tables(1)
    decay = _const(g_chunk[:, None], (RET_HEADS, SC_LANES))
    kq_rep = jnp.broadcast_to(kq[..., None], kq.shape + (SC_LANES,)).reshape(2, P, RET_DK // STATE_UNROLL, LANES)
    mesh = plsc.VectorSubcoreMesh(core_axis_name="c", subcore_axis_name="s")

    def body(s0_hbm, kq_hbm, v_hbm, g_hbm, out_hbm, qs_hbm, in_v, out_v, k_v, q_v, v_v, g_v, o_v, sem_in, sem_out):
        wid = lax.axis_index("s") * SC_CORES + lax.axis_index("c")
        base = wid * per_w
        pltpu.sync_copy(kq_hbm.at[0, pl.ds(base, per_w)], k_v)
        pltpu.sync_copy(kq_hbm.at[1, pl.ds(base, per_w)], q_v)
        pltpu.sync_copy(v_hbm.at[pl.ds(base, per_w)], v_v)
        pltpu.sync_copy(g_hbm, g_v)

        def load(p, part, slot):
            return pltpu.make_async_copy(s0_hbm.at[p, pl.ds(part * R, R)], in_v.at[slot], sem_in.at[slot])

        def store(p, part, slot):
            return pltpu.make_async_copy(out_v.at[slot], out_hbm.at[p, pl.ds(part * R, R)], sem_out.at[slot])

        load(base, 0, 0).start()

        def pair(pp, carry):
            p = base + pp
            g = g_v[p % RET_HEADS, :]
            for c in range(n_vc):
                o_v[pp, pl.ds(c * SC_LANES, SC_LANES)] = jnp.zeros((SC_LANES,), _F32)
            for part in range(n_parts):
                slot = part % 2
                load(p, part, slot).wait()
                if part + 1 < n_parts:
                    load(p, part + 1, 1 - slot).start()
                else:
                    @pl.when(pp + 1 < per_w)
                    def _():
                        load(p + 1, 0, 1 - slot).start()

                if part < 2:
                    @pl.when(pp > 0)
                    def _():
                        store(p, part, slot).wait()
                else:
                    store(p, part, slot).wait()

                def rows(i8, c2):
                    krow = part * (R // STATE_UNROLL) + i8
                    for c0 in range(0, n_vc, STATE_COLS):
                        cols = [pl.ds((c0 + c) * SC_LANES, SC_LANES) for c in range(STATE_COLS)]
                        vs = [v_v[pp, cs] for cs in cols]
                        acc = [None] * STATE_COLS
                        for j in range(STATE_UNROLL):
                            i = i8 * STATE_UNROLL + j
                            ki = k_v[pp, krow, pl.ds(j * SC_LANES, SC_LANES)]
                            qi = q_v[pp, krow, pl.ds(j * SC_LANES, SC_LANES)]
                            ss = [in_v[slot, i, cs] for cs in cols]
                            for c, cs in enumerate(cols):
                                out_v[slot, i, cs] = g * ss[c] + ki * vs[c]
                            for c in range(STATE_COLS):
                                t = qi * _sc_round_bf16(ss[c])
                                acc[c] = t if acc[c] is None else acc[c] + t
                        for c, cs in enumerate(cols):
                            o_v[pp, cs] = o_v[pp, cs] + acc[c]
                    return c2

                lax.fori_loop(0, R // STATE_UNROLL, rows, 0)
                store(p, part, slot).start()
            return carry

        lax.fori_loop(0, per_w, pair, 0)
        for slot in range(2):
            store(base, slot, slot).wait()
        pltpu.sync_copy(o_v, qs_hbm.at[pl.ds(base, per_w)])

    keys = pltpu.VMEM((per_w, RET_DK // STATE_UNROLL, LANES), _F32)
    return pl.kernel(
        body, mesh=mesh,
        out_type=(jax.ShapeDtypeStruct(s0.shape, _F32), jax.ShapeDtypeStruct((P, RET_DV), _F32)),
        scratch_types=[pltpu.VMEM((2, R, RET_DV), _F32), pltpu.VMEM((2, R, RET_DV), _F32), keys, keys,
                       pltpu.VMEM((per_w, RET_DV), _F32), pltpu.VMEM((RET_HEADS, SC_LANES), _F32),
                       pltpu.VMEM((per_w, RET_DV), _F32),
                       pltpu.SemaphoreType.DMA((2,)), pltpu.SemaphoreType.DMA((2,))],
        compiler_params=dataclasses.replace(pltpu.CompilerParams(), needs_layout_passes=False),
        name="sample_state_update",
    )(s0, kq_rep, v, decay)


IDX_CHUNK = 1024
ISSUE_UNROLL = 8
SC_WINDOW = 64


def _sc_scatter(h2, dst_rows, n_rows):
    T = h2.shape[0]
    n_workers = SC_CORES * SC_SUBCORES
    W = SC_WINDOW
    assert T % (n_workers * W) == 0
    per_w = T // n_workers
    n_win = per_w // W
    d0 = dst_rows[0].reshape(T // W, W)
    d1 = dst_rows[1].reshape(T // W, W)
    mesh = plsc.VectorSubcoreMesh(core_axis_name="c", subcore_axis_name="s")

    def body(h2_hbm, d0_hbm, d1_hbm, xs_hbm, i0_v, i1_v, rows_v, sem_load, sem_store):
        wid = lax.axis_index("s") * SC_CORES + lax.axis_index("c")
        base = wid * per_w
        pltpu.sync_copy(d0_hbm.at[pl.ds(wid * n_win, n_win)], i0_v)
        pltpu.sync_copy(d1_hbm.at[pl.ds(wid * n_win, n_win)], i1_v)
        pltpu.async_copy(h2_hbm.at[pl.ds(base, W)], rows_v.at[0], sem_load)
        for i in range(n_win):
            b = i % 2
            pltpu.make_async_copy(h2_hbm.at[pl.ds(base, W)], rows_v.at[b], sem_load).wait()
            if i + 1 < n_win:
                pltpu.async_copy(h2_hbm.at[pl.ds(base + (i + 1) * W, W)], rows_v.at[1 - b], sem_load)
            c0 = pltpu.async_copy(rows_v.at[b], xs_hbm.at[i0_v.at[i]], sem_store)
            c1 = pltpu.async_copy(rows_v.at[b], xs_hbm.at[i1_v.at[i]], sem_store)
            c0.wait()
            c1.wait()

    return pl.kernel(
        body, mesh=mesh,
        out_type=jax.ShapeDtypeStruct((n_rows,) + ROW_TILE, ROW_DTYPE),
        scratch_types=[pltpu.VMEM((n_win, W), jnp.int32), pltpu.VMEM((n_win, W), jnp.int32),
                       pltpu.VMEM((2, W) + ROW_TILE, ROW_DTYPE),
                       pltpu.SemaphoreType.DMA, pltpu.SemaphoreType.DMA],
        name="moe_sc_scatter",
    )(h2, d0, d1)


ZERO_CHUNK = 64


def _finish_dispatch_kernel(zstart_ref, zchunks_ref, slots_hbm, h2_ref, xs_in_hbm, xs_hbm, idx_smem, zbuf, sem_idx,
                            sem_rows, sem_zero):
    del xs_in_hbm
    TS = h2_ref.shape[0]
    idx_cp = pltpu.make_async_copy(slots_hbm.at[pl.ds(0, IDX_CHUNK)], idx_smem, sem_idx)
    idx_cp.start()
    zbuf[...] = jnp.zeros(zbuf.shape, zbuf.dtype)

    def zero_copy(e, j):
        return pltpu.make_async_copy(zbuf, xs_hbm.at[pl.ds(zstart_ref[e] + j * ZERO_CHUNK, ZERO_CHUNK)], sem_zero)

    for e in range(N_EXPERTS):
        lax.fori_loop(0, zchunks_ref[e], lambda j, c, e=e: (zero_copy(e, j).start(), c)[1], 0)
    idx_cp.wait()

    def issue(r, carry):
        for kk in range(2):
            pltpu.make_async_copy(h2_ref.at[r], xs_hbm.at[idx_smem[2 * r + kk]], sem_rows).start(priority=kk)
        return carry

    lax.fori_loop(0, TS, issue, 0, unroll=ISSUE_UNROLL)
    for e in range(N_EXPERTS):
        lax.fori_loop(0, zchunks_ref[e], lambda j, c, e=e: (zero_copy(e, j).wait(), c)[1], 0)
    for kk in range(2):
        pltpu.make_async_copy(h2_ref, xs_hbm.at[pl.ds(0, TS)], sem_rows).wait()


def _finish_dispatch(xs, h2s, slots, zstart, zchunks):
    Bs = h2s.shape[0]
    assert 2 * Bs <= IDX_CHUNK and slots.shape[0] == IDX_CHUNK and MOE_BLOCK % ZERO_CHUNK == 0
    any_spec = pl.BlockSpec(memory_space=pl.ANY)
    grid_spec = pltpu.PrefetchScalarGridSpec(
        num_scalar_prefetch=2,
        grid=(1,),
        in_specs=[any_spec, pl.BlockSpec((Bs,) + ROW_TILE, lambda t, z, n: (0, 0, 0)), any_spec],
        out_specs=any_spec,
        scratch_shapes=[pltpu.SMEM((IDX_CHUNK,), jnp.int32), pltpu.VMEM((ZERO_CHUNK,) + ROW_TILE, ROW_DTYPE),
                        pltpu.SemaphoreType.DMA, pltpu.SemaphoreType.DMA, pltpu.SemaphoreType.DMA],
    )
    return pl.pallas_call(
        _finish_dispatch_kernel,
        grid_spec=grid_spec,
        out_shape=jax.ShapeDtypeStruct(xs.shape, ROW_DTYPE),
        input_output_aliases={4: 0},
        compiler_params=pltpu.CompilerParams(dimension_semantics=("arbitrary",), has_side_effects=True),
        name="moe_finish_dispatch",
    )(zstart, zchunks, slots, h2s, xs)


FFN_IN_BUFFERS = 4
FFN_OUT_BUFFERS = 3


def _ffn_kernel(cnt_ref, xs_hbm, w1_hbm, w3_hbm, w2_hbm, ys_hbm,
                xs_buf, ys_buf, st1, st3, st2, w13_scr, w2_scr, t_row, t_exp, t_len, sem_in, sem_out, sem_w,
                *, cap_blocks):
    B = MOE_BLOCK

    def fill_expert(e, g):
        nb = (cnt_ref[e] + (B - 1)) // B

        def fill_block(j, carry):
            t_row[g + j] = e * cap_blocks + j
            t_exp[g + j] = e
            t_len[g + j] = nb
            return carry

        lax.fori_loop(0, nb, fill_block, 0)
        return g + nb

    n_used = lax.fori_loop(0, N_EXPERTS, fill_expert, 0)

    def in_copy(g):
        s = g % FFN_IN_BUFFERS
        return pltpu.make_async_copy(xs_hbm.at[pl.ds(t_row[g] * B, B)], xs_buf.at[s], sem_in.at[s])

    def out_copy(g):
        s = g % FFN_OUT_BUFFERS
        return pltpu.make_async_copy(ys_buf.at[s], ys_hbm.at[pl.ds(t_row[g] * B, B)], sem_out.at[s])

    def weight_copies(e, s):
        return (pltpu.make_async_copy(w1_hbm.at[e], st1.at[s], sem_w.at[s]),
                pltpu.make_async_copy(w3_hbm.at[e], st3.at[s], sem_w.at[s]),
                pltpu.make_async_copy(w2_hbm.at[e], st2.at[s], sem_w.at[s]))

    @pl.when(n_used > 0)
    def _():
        for cp in weight_copies(t_exp[0], 0):
            cp.start()

    for g0 in range(FFN_IN_BUFFERS - 1):
        @pl.when(g0 < n_used)
        def _(g0=g0):
            in_copy(g0).start()

    def block(g, wslot):
        first = (g == 0) | (t_exp[g] != t_exp[jnp.maximum(g - 1, 0)])
        wslot = jnp.where(first & (g > 0), 1 - wslot, wslot)

        @pl.when(first)
        def _():
            for cp in weight_copies(t_exp[g], wslot):
                cp.wait()
            nxt = g + t_len[g]

            @pl.when(nxt < n_used)
            def _():
                for cp in weight_copies(t_exp[jnp.minimum(nxt, n_used - 1)], 1 - wslot):
                    cp.start()

            w13_scr[:, 0:D_EXPERT] = st1[wslot].astype(_BF)
            w13_scr[:, D_EXPERT:2 * D_EXPERT] = st3[wslot].astype(_BF)
            w2_scr[...] = st2[wslot].astype(_BF)

        in_copy(g).wait()

        @pl.when(g + FFN_IN_BUFFERS - 1 < n_used)
        def _():
            in_copy(g + FFN_IN_BUFFERS - 1).start()

        @pl.when(g >= FFN_OUT_BUFFERS)
        def _():
            out_copy(g - FFN_OUT_BUFFERS).wait()

        xb = _unpack_rows(xs_buf[g % FFN_IN_BUFFERS]).astype(_BF)
        ab = _dot(xb, w13_scr[...])
        hid = jax.nn.silu(ab[:, 0:D_EXPERT]) * ab[:, D_EXPERT:2 * D_EXPERT]
        ys_buf[g % FFN_OUT_BUFFERS] = _pack_rows(_dot(hid.astype(_BF), w2_scr[...]))
        out_copy(g).start()
        return wslot

    lax.fori_loop(0, n_used, block, 0)

    for back in range(FFN_OUT_BUFFERS, 0, -1):
        @pl.when(n_used >= back)
        def _(back=back):
            out_copy(n_used - back).wait()


def _ffn(xs, counts, w1, w3, w2, cap_blocks, n_blocks):
    any_spec = pl.BlockSpec(memory_space=pl.ANY)
    row_buf = lambda n: pltpu.VMEM((n, MOE_BLOCK) + ROW_TILE, ROW_DTYPE)
    grid_spec = pltpu.PrefetchScalarGridSpec(
        num_scalar_prefetch=1,
        grid=(1,),
        in_specs=[any_spec, any_spec, any_spec, any_spec],
        out_specs=any_spec,
        scratch_shapes=[row_buf(FFN_IN_BUFFERS), row_buf(FFN_OUT_BUFFERS),
                        pltpu.VMEM((2, D_MODEL, D_EXPERT), _F32), pltpu.VMEM((2, D_MODEL, D_EXPERT), _F32),
                        pltpu.VMEM((2, D_EXPERT, D_MODEL), _F32),
                        pltpu.VMEM((D_MODEL, 2 * D_EXPERT), _BF), pltpu.VMEM((D_EXPERT, D_MODEL), _BF),
                        pltpu.SMEM((n_blocks,), jnp.int32), pltpu.SMEM((n_blocks,), jnp.int32),
                        pltpu.SMEM((n_blocks,), jnp.int32),
                        pltpu.SemaphoreType.DMA((FFN_IN_BUFFERS,)), pltpu.SemaphoreType.DMA((FFN_OUT_BUFFERS,)),
                        pltpu.SemaphoreType.DMA((2,))],
    )
    return pl.pallas_call(
        functools.partial(_ffn_kernel, cap_blocks=cap_blocks),
        grid_spec=grid_spec,
        out_shape=jax.ShapeDtypeStruct(xs.shape, ROW_DTYPE),
        compiler_params=pltpu.CompilerParams(dimension_semantics=("arbitrary",), vmem_limit_bytes=VMEM_LIMIT,
                                             has_side_effects=True),
        name="moe_ffn",
    )(counts, xs, w1, w3, w2)


def _combine_kernel(slots_hbm, x1_ref, rw_ref, g_ref, ys_hbm, y_ref, idx_smem, buf, sem_idx, sem_rows):
    t = pl.program_id(0)
    TC = x1_ref.shape[0]
    idx_cp = pltpu.make_async_copy(slots_hbm.at[pl.ds(t * IDX_CHUNK, IDX_CHUNK)], idx_smem, sem_idx)
    idx_cp.start()
    idx_cp.wait()

    def issue(r, carry):
        for kk in range(2):
            pltpu.make_async_copy(ys_hbm.at[idx_smem[2 * r + kk]], buf.at[kk, r], sem_rows).start(priority=kk)
        return carry

    lax.fori_loop(0, TC, issue, 0, unroll=ISSUE_UNROLL)
    for kk in range(2):
        pltpu.make_async_copy(ys_hbm.at[pl.ds(0, TC)], buf.at[kk], sem_rows).wait()

    w = rw_ref[...]
    y0 = _unpack_rows(buf[0])
    y1 = _unpack_rows(buf[1])
    x2 = x1_ref[...] + (w[:, 0:1] * y0 + w[:, 1:2] * y1)
    y_ref[...] = _rms(x2, g_ref[...])


SC_GATHER_CHUNKS = 4
DENSE_TILE = 1024
SC_GATHER_BUFFERS = 3
SC_GATHER_MAX_WINDOW = 80


def _sc_gather(ys, rows):
    Tg = rows.shape[1]
    n_workers = SC_CORES * SC_SUBCORES
    per_w = Tg // n_workers
    assert per_w * n_workers == Tg and per_w % SUBLANES == 0
    W = max(w for w in range(SUBLANES, SC_GATHER_MAX_WINDOW + 1, SUBLANES) if per_w % w == 0)
    n_win = per_w // W
    NB = SC_GATHER_BUFFERS
    idx = rows.reshape(2, n_workers, n_win, W)
    jobs = [(kk, i) for i in range(n_win) for kk in range(2)]
    mesh = plsc.VectorSubcoreMesh(core_axis_name="c", subcore_axis_name="s")

    def body(ys_hbm, idx_hbm, out_hbm, i_v, rows_v, sem_g, sem_s):
        wid = lax.axis_index("s") * SC_CORES + lax.axis_index("c")
        base = wid * per_w
        for kk in range(2):
            pltpu.sync_copy(idx_hbm.at[kk, wid], i_v.at[kk])

        def gather(j):
            kk, i = jobs[j]
            return pltpu.make_async_copy(ys_hbm.at[i_v.at[kk, i]], rows_v.at[j % NB], sem_g.at[j % NB])

        def store(j):
            kk, i = jobs[j]
            return pltpu.make_async_copy(rows_v.at[j % NB], out_hbm.at[kk, pl.ds(base + i * W, W)], sem_s.at[j % NB])

        gather(0).start()
        for j in range(len(jobs)):
            gather(j).wait()
            store(j).start()
            if j + 1 < len(jobs):
                if j + 1 >= NB:
                    store(j + 1 - NB).wait()
                gather(j + 1).start()
        for j in range(max(0, len(jobs) - NB), len(jobs)):
            store(j).wait()

    return pl.kernel(
        body, mesh=mesh,
        out_type=jax.ShapeDtypeStruct((2, Tg) + ROW_TILE, ROW_DTYPE),
        scratch_types=[pltpu.VMEM((2, n_win, W), jnp.int32), pltpu.VMEM((NB, W) + ROW_TILE, ROW_DTYPE),
                       pltpu.SemaphoreType.DMA((NB,)), pltpu.SemaphoreType.DMA((NB,))],
        name="moe_sc_gather",
    )(ys, idx)


def _combine_dense_kernel(x1_ref, rw_ref, g_ref, rows_ref, *rest):
    y_ref = rest[-1]
    w = rw_ref[...]
    y0 = _unpack_rows(rows_ref[0])
    y1 = _unpack_rows(rows_ref[1])
    x2 = x1_ref[...] + (w[:, 0:1] * y0 + w[:, 1:2] * y1)
    y_ref[...] = _rms(x2, g_ref[...])


def _combine_dense(x1, rw, rows, y, g_final, tile, first_tile):
    Tg = rows.shape[1]
    assert Tg % tile == 0
    tok = lambda t: (first_tile + t, 0)
    in_specs = [pl.BlockSpec((tile, D_MODEL), tok), pl.BlockSpec((tile, LANES), tok), _full((1, D_MODEL)),
                pl.BlockSpec((2, tile) + ROW_TILE, lambda t: (0, t, 0, 0))]
    args = [x1, rw, g_final.reshape(1, D_MODEL), rows]
    aliases = {}
    if y is not None:
        in_specs.append(pl.BlockSpec(memory_space=pl.ANY))
        args.append(y)
        aliases = {4: 0}
    return pl.pallas_call(
        _combine_dense_kernel,
        grid=(Tg // tile,),
        in_specs=in_specs,
        out_specs=pl.BlockSpec((tile, D_MODEL), tok),
        out_shape=jax.ShapeDtypeStruct(x1.shape, _F32),
        input_output_aliases=aliases,
        compiler_params=pltpu.CompilerParams(dimension_semantics=("arbitrary",), vmem_limit_bytes=VMEM_LIMIT),
        name="moe_combine_dense",
    )(*args)


def _combine(x1, rw, slots, ys, g_final, tile, n_tiles):
    T = x1.shape[0]
    assert T % tile == 0 and 2 * tile <= IDX_CHUNK and slots.shape[0] == n_tiles * IDX_CHUNK
    any_spec = pl.BlockSpec(memory_space=pl.ANY)
    return pl.pallas_call(
        _combine_kernel,
        grid=(n_tiles,),
        in_specs=[any_spec, pl.BlockSpec((tile, D_MODEL), lambda t: (t, 0)),
                  pl.BlockSpec((tile, LANES), lambda t: (t, 0)), _full((1, D_MODEL)), any_spec],
        out_specs=pl.BlockSpec((tile, D_MODEL), lambda t: (t, 0)),
        out_shape=jax.ShapeDtypeStruct((T, D_MODEL), _F32),
        scratch_shapes=[pltpu.SMEM((IDX_CHUNK,), jnp.int32), pltpu.VMEM((2, tile) + ROW_TILE, ROW_DTYPE),
                        pltpu.SemaphoreType.DMA, pltpu.SemaphoreType.DMA],
        compiler_params=pltpu.CompilerParams(dimension_semantics=("arbitrary",), vmem_limit_bytes=VMEM_LIMIT),
        name="moe_combine",
    )(slots, x1, rw, g_final.reshape(1, D_MODEL), ys)


def _tile_for(n_tokens):
    tile = IDX_CHUNK // 2
    return tile if n_tokens % tile == 0 else n_tokens


def _chunked_slots(slot, tile):
    n_tiles = slot.shape[0] // tile
    s = slot.reshape(n_tiles, 2 * tile)
    return jnp.pad(s, ((0, 0), (0, IDX_CHUNK - 2 * tile))).reshape(-1)


def kernel(x_prompt, x_sample, state_pool, state_ret, g_mix, w_in, w_pool, pool_scale, ret_gn, w_out, g_ffn, w_grp, w_exp, w1, w3, w2, g_final):
    Bp, Lp, _ = x_prompt.shape
    Bs = x_sample.shape[0]
    Tp = Bp * Lp
    wts = _prep_weights(g_mix[0], w_pool[0], pool_scale[0], ret_gn[0], w_out[0], g_ffn[0], w_grp[0], w_exp[0])

    T_all = Tp + Bs
    cap = (-(-T_all // MOE_BLOCK) + 1) * MOE_BLOCK
    cap_blocks = cap // MOE_BLOCK

    xs_tok = x_sample.reshape(Bs, D_MODEL)
    wts["w_in"], u_s, q_s, k_s, kq_s, v_s, ga_s, gb_s = _sample_proj(xs_tok, wts["g_mix"], w_in[0], PAST_LEN)
    n_pairs = Bs * RET_HEADS
    nret_s, q_state = _sc_state_update(state_ret[0].reshape(n_pairs, RET_DK, RET_DV),
                                       kq_s.reshape(2, n_pairs, RET_DK), v_s.reshape(n_pairs, RET_DV))
    x1p, dst_rows_p, rwp, npool_p, nret_p, counts_p, h2p = _mixer_prompt(x_prompt, wts, cap)
    xs = _sc_scatter(h2p, dst_rows_p, N_EXPERTS * cap)
    x1s, dst_s, rws, npool_s, counts, h2s = _mixer_sample(
        xs_tok, (u_s, q_s, k_s, v_s, ga_s, gb_s), jnp.swapaxes(state_pool[0], 0, 1), q_state.reshape(Bs, V_W), wts,
        counts_p, cap)

    counts = counts[0, :N_EXPERTS].astype(jnp.int32)
    tile_p, tile_s = _tile_for(Tp), _tile_for(Bs)
    slots_s = _chunked_slots(dst_s[:, :2], tile_s)
    zstart = jnp.arange(N_EXPERTS, dtype=jnp.int32) * cap + counts
    zchunks = ((-counts) % MOE_BLOCK + ZERO_CHUNK - 1) // ZERO_CHUNK
    xs = _finish_dispatch(xs, h2s, slots_s, zstart, zchunks.astype(jnp.int32))
    n_blocks = -(-2 * T_all // MOE_BLOCK) + N_EXPERTS
    ys = _ffn(xs, counts, w1[0], w3[0], w2[0], cap_blocks, n_blocks)
    tile_d = DENSE_TILE if Tp % DENSE_TILE == 0 else tile_p
    n_tiles_d = Tp // tile_d
    chunk_tiles = [n_tiles_d // SC_GATHER_CHUNKS + (k < n_tiles_d % SC_GATHER_CHUNKS)
                   for k in range(SC_GATHER_CHUNKS)]
    gathered, first = [], 0
    for nt in chunk_tiles:
        if nt:
            gathered.append((first, _sc_gather(ys, dst_rows_p[:, first * tile_d:(first + nt) * tile_d])))
            first += nt
    y_s = _combine(x1s, rws, slots_s, ys, g_final, tile_s, 1)
    y_p = None
    for first, rows in gathered:
        y_p = _combine_dense(x1p, rwp, rows, y_p, g_final, tile_d, first)

    return (y_p.reshape(Bp, Lp, D_MODEL), y_s.reshape(Bs, 1, D_MODEL),
            npool_p[None], nret_p[None], jnp.swapaxes(npool_s, 0, 1)[None], nret_s.reshape(state_ret.shape))
```

```python
import dataclasses
import functools

import jax
import jax.numpy as jnp
import numpy as np
from jax import lax
from jax.experimental import pallas as pl
from jax.experimental.pallas import tpu as pltpu
from jax.experimental.pallas import tpu_sc as plsc

D_MODEL = 1024
EPS = 1e-6
POOL_GROUPS = 4
POOL_IN = D_MODEL // 2
POOL_GC = POOL_IN // POOL_GROUPS
POOL_OUT_GC = D_MODEL // POOL_GROUPS
POOL_WINDOWS = (2, 4, 8, 16)
POOL_HIST = max(POOL_WINDOWS) - 1
RET_HEADS = 4
RET_DK = D_MODEL // 8
RET_DV = D_MODEL // RET_HEADS
ROPE_BASE = 10000.0
PAST_LEN = 16384
N_GROUPS = 4
EXPERTS_PER_GROUP = 8
N_EXPERTS = N_GROUPS * EXPERTS_PER_GROUP
D_EXPERT = D_MODEL // 4
QK_W = RET_HEADS * RET_DK
V_W = RET_HEADS * RET_DV
OFF_U = 0
OFF_Q = POOL_IN
OFF_K = OFF_Q + QK_W
OFF_V = OFF_K + QK_W
OFF_GA = OFF_V + V_W
OFF_GB = OFF_GA + D_MODEL
IN_WIDTH = OFF_GB + D_MODEL

LANES = 128
SUBLANES = 8
ROW_WORDS = D_MODEL // 2
ROW_TILE = (ROW_WORDS // LANES, LANES)
ROW_DTYPE = jnp.uint32
HALO = 16
RET_CHUNK = 256
MIXER_SEQS = 2
MOE_BLOCK = 512
ROUTER_W = LANES
RANK_GROUP = 128
VMEM_LIMIT = 56 * 1024 * 1024

_BF = jnp.bfloat16
_F32 = jnp.float32


def _rms(x, g):
    inv = lax.rsqrt(jnp.mean(x * x, axis=-1, keepdims=True) + EPS)
    return x * inv * g


def _dot(a, b):
    return jnp.dot(a, b, preferred_element_type=_F32)


def _round_bf16(x):
    return x.astype(_BF).astype(_F32)


def _pack_rows(x):
    lo = lax.bitcast_convert_type(x[:, :ROW_WORDS].astype(_BF).astype(_F32), jnp.uint32)
    hi = lax.bitcast_convert_type(x[:, ROW_WORDS:].astype(_BF).astype(_F32), jnp.uint32)
    return ((lo >> 16) | hi).reshape((x.shape[0],) + ROW_TILE)


def _unpack_rows(w):
    w = w.reshape(w.shape[0], ROW_WORDS)
    lo = lax.bitcast_convert_type(w << 16, _F32)
    hi = lax.bitcast_convert_type(w & jnp.uint32(0xFFFF0000), _F32)
    return jnp.concatenate([lo, hi], axis=1)


def _rotary(x, cos2, sin2):
    return x * cos2 + pltpu.roll(x, RET_DK // 2, 1) * sin2


def _route(logits):
    lane = lax.broadcasted_iota(jnp.int32, logits.shape, 1).astype(_F32)
    neg = jnp.float32(-jnp.inf)
    big = jnp.float32(1 << 20)
    lg = jnp.where(lane < N_GROUPS, logits, neg)
    mg = jnp.max(lg, axis=-1, keepdims=True)
    g_idx = jnp.min(jnp.where(lg == mg, lane, big), axis=-1, keepdims=True)
    p_g = 1.0 / jnp.sum(jnp.exp(lg - mg), axis=-1, keepdims=True)
    lo = N_GROUPS + g_idx * EXPERTS_PER_GROUP
    in_grp = (lane >= lo) & (lane < lo + EXPERTS_PER_GROUP)
    le = jnp.where(in_grp, logits, neg)
    m1 = jnp.max(le, axis=-1, keepdims=True)
    i1 = jnp.min(jnp.where(le == m1, lane, big), axis=-1, keepdims=True)
    le2 = jnp.where(lane == i1, neg, le)
    m2 = jnp.max(le2, axis=-1, keepdims=True)
    i2 = jnp.min(jnp.where(le2 == m2, lane, big), axis=-1, keepdims=True)
    t = jnp.exp(m2 - m1)
    den = 1.0 + t
    e0 = (i1 - N_GROUPS).astype(jnp.int32)
    e1 = (i2 - N_GROUPS).astype(jnp.int32)
    return e0, e1, p_g * (1.0 / den), p_g * (t / den)


def _post_mix(x, mixed_ref, w_out_ref, g_ffn_ref, w_rt_ref, x1_ref, rw_ref):
    x1 = x + _dot(mixed_ref[...], w_out_ref[...])
    x1_ref[...] = x1.reshape(x1_ref.shape)
    h2 = _rms(x1, g_ffn_ref[...])
    e0, e1, w0, w1 = _route(_dot(h2.astype(_BF), w_rt_ref[...]))
    lane = lax.broadcasted_iota(jnp.int32, (x.shape[0], LANES), 1)
    rw_ref[...] = jnp.where(lane == 0, w0, jnp.where(lane == 1, w1, 0.0)).reshape(rw_ref.shape)
    return h2, e0, e1


def _route_block(x1, g_ffn_ref, w_rt_ref, carry_scr, cap, valid, h2_ref, rw_ref, dst_t_ref):
    for _ in _route_stages(x1, g_ffn_ref, w_rt_ref, carry_scr, cap, valid, h2_ref, rw_ref, dst_t_ref):
        pass


def _route_stages(x1, g_ffn_ref, w_rt_ref, carry_scr, cap, valid, h2_ref, rw_ref, dst_t_ref):
    h2 = _rms(x1, g_ffn_ref[...])
    logits = _dot(h2.astype(_BF), w_rt_ref[...])
    h2_ref[...] = _pack_rows(h2).reshape(h2_ref.shape)
    yield
    e0, e1, w0, w1 = _route(logits)
    lane = lax.broadcasted_iota(jnp.int32, (x1.shape[0], LANES), 1)
    rw_ref[...] = jnp.where(lane == 0, w0, jnp.where(lane == 1, w1, 0.0)).reshape(rw_ref.shape)
    yield
    dst = _rank_rows(e0, e1, cap, carry_scr, valid)
    dst_t = dst.T[0:SUBLANES, :].astype(jnp.int32)
    n_parts, _, width = dst_t_ref.shape
    for part in range(n_parts):
        dst_t_ref[part] = dst_t[:, part * width:(part + 1) * width]
    yield


def _rank_rows(e0, e1, cap, carry_scr, valid=None):
    R = e0.shape[0]
    lane = lax.broadcasted_iota(jnp.int32, (R, LANES), 1)
    m0 = lane == e0
    m1 = lane == e1
    onehot = jnp.where(m0 | m1, 1.0, 0.0)
    G = min(RANK_GROUP, R)
    r_i = lax.broadcasted_iota(jnp.int32, (G, G), 0)
    c_i = lax.broadcasted_iota(jnp.int32, (G, G), 1)
    tri = jnp.where(c_i < r_i, 1.0, 0.0).astype(_BF)
    running = carry_scr[...]
    parts = []
    for g in range(R // G):
        grp = onehot[g * G:(g + 1) * G]
        parts.append(_dot(tri, grp.astype(_BF)) + running)
        running = running + jnp.sum(grp, axis=0, keepdims=True)
    before = jnp.concatenate(parts, axis=0)
    d0 = e0.astype(_F32) * cap + jnp.sum(jnp.where(m0, before, 0.0), axis=-1, keepdims=True)
    d1 = e1.astype(_F32) * cap + jnp.sum(jnp.where(m1, before, 0.0), axis=-1, keepdims=True)
    added = running - carry_scr[...]
    carry_scr[...] += added if valid is None else added * valid
    return jnp.where(lane == 0, d0, jnp.where(lane == 1, d1, 0.0))


def _group_norm(o, gain):
    mu = jnp.mean(o, axis=-1, keepdims=True)
    d = o - mu
    var = jnp.mean(d * d, axis=-1, keepdims=True)
    return d * lax.rsqrt(var + EPS) * gain


def _mixer_prompt_kernel(x_ref, g_mix_ref, w_in_ref, w_pool_ref, pscale_ref, gn_ref, w_out_ref, g_ffn_ref,
                         w_rt_ref, cos_ref, sin_ref, dmask_ref, xi_ref, zeta_ref, gc_ref,
                         x1_ref, dst_t_ref, rw_ref, npool_ref, nret_ref, cnt_ref, h2_ref,
                         u_scr, s_scr, mixed_scr, carry_scr, z_scr, x1_prev_scr, *, cap):
    c = pl.program_id(1)
    n_c = pl.num_programs(1)
    step = pl.program_id(0) * n_c + c
    NSEQ, C, _ = x_ref.shape
    R = NSEQ * C

    @pl.when(step == 0)
    def _():
        carry_scr[...] = jnp.zeros(carry_scr.shape, _F32)
        x1_prev_scr[...] = jnp.zeros(x1_prev_scr.shape, _F32)

    @pl.when(c == 0)
    def _():
        u_scr[:, 0:HALO, :] = jnp.zeros((NSEQ, HALO, POOL_IN), _F32)
        s_scr[...] = jnp.zeros(s_scr.shape, _F32)

    x = x_ref[...].reshape(R, D_MODEL)
    h = _rms(x, g_mix_ref[...]).astype(_BF)

    stages = _route_stages(x1_prev_scr[...], g_ffn_ref, w_rt_ref, carry_scr, cap, (step > 0).astype(_F32),
                           h2_ref, rw_ref, dst_t_ref)
    for lo, hi in ((0, OFF_V), (OFF_V, OFF_GA), (OFF_GA, OFF_GB), (OFF_GB, IN_WIDTH)):
        z_scr[:, lo:hi] = _dot(h, w_in_ref[:, lo:hi])
        next(stages, None)
    cnt_ref[...] = carry_scr[...]

    u_scr[:, HALO:HALO + C, :] = z_scr[:, OFF_U:OFF_U + POOL_IN].reshape(NSEQ, C, POOL_IN)
    q = z_scr[:, OFF_Q:OFF_Q + QK_W]
    k = z_scr[:, OFF_K:OFF_K + QK_W]
    cos2 = jnp.concatenate([cos_ref[...]] * NSEQ, axis=0)
    sin2 = jnp.concatenate([sin_ref[...]] * NSEQ, axis=0)
    pos1 = (c * C + 1 + lax.broadcasted_iota(jnp.int32, (C, POOL_GC), 0)).astype(_F32)

    for j in range(RET_HEADS):
        win = POOL_WINDOWS[j]
        cs = slice(j * POOL_GC, (j + 1) * POOL_GC)
        n_rows = jnp.minimum(pos1, jnp.float32(win))
        pooled = []
        for sq in range(NSEQ):
            u_j = u_scr[sq, HALO:HALO + C, cs]
            s = u_j
            for d in range(1, win):
                s = s + u_scr[sq, HALO - d:HALO - d + C, cs]
            pooled.append(s / n_rows - u_j)
        osl = slice(j * POOL_OUT_GC, (j + 1) * POOL_OUT_GC)
        pool_out = _dot(jnp.concatenate(pooled, axis=0).astype(_BF), w_pool_ref[j]) * pscale_ref[:, osl]

        qs = slice(j * RET_DK, (j + 1) * RET_DK)
        qb = _rotary(q[:, qs], cos2, sin2).astype(_BF)
        kf = _rotary(k[:, qs], cos2, sin2) * (RET_DK ** -0.5)
        kb = kf.astype(_BF)
        vb = z_scr[:, OFF_V + j * RET_DV:OFF_V + (j + 1) * RET_DV].astype(_BF)
        ret = []
        for sq in range(NSEQ):
            rows = slice(sq * C, (sq + 1) * C)
            scores = lax.dot_general(qb[rows], kb[rows], (((1,), (1,)), ((), ())),
                                     preferred_element_type=_F32) * dmask_ref[j]
            s_old = s_scr[sq, j]
            o = _dot(scores.astype(_BF), vb[rows]) + _dot(qb[rows], s_old.astype(_BF)) * xi_ref[j]
            kz = (kf[rows] * zeta_ref[j]).astype(_BF)
            s_scr[sq, j] = gc_ref[j] * s_old + lax.dot_general(kz, vb[rows], (((0,), (0,)), ((), ())),
                                                               preferred_element_type=_F32)
            ret.append(_group_norm(o, gn_ref[:, osl]))
        ret_out = jnp.concatenate(ret, axis=0)

        ga = z_scr[:, OFF_GA + j * RET_DV:OFF_GA + (j + 1) * RET_DV]
        gb = z_scr[:, OFF_GB + j * RET_DV:OFF_GB + (j + 1) * RET_DV]
        mixed_scr[:, osl] = (jax.nn.sigmoid(ga) * pool_out + jax.nn.sigmoid(gb) * ret_out).astype(_BF)

    x1 = x + _dot(mixed_scr[...], w_out_ref[...])
    x1_ref[...] = x1.reshape(x1_ref.shape)
    x1_prev_scr[...] = x1

    u_scr[:, 0:HALO, :] = u_scr[:, C:C + HALO, :]

    @pl.when(c == n_c - 1)
    def _():
        npool_ref[...] = u_scr[:, HALO + C - POOL_HIST:HALO + C, :]
        nret_ref[...] = s_scr[...]


def _decay_tables(C):
    f32 = np.float32
    log_g = np.log(f32(1.0) - np.exp2(f32(-5.0) - np.arange(RET_HEADS, dtype=f32)))
    i = np.arange(C, dtype=f32)
    diff = i[:, None] - i[None, :]
    dmask = np.where(diff >= 0, np.exp(np.maximum(diff, f32(0.0))[None] * log_g[:, None, None]), f32(0.0))
    xi = np.exp((i[None, :] + f32(1.0)) * log_g[:, None])
    zeta = np.exp((f32(C) - f32(1.0) - i)[None, :] * log_g[:, None])
    g_chunk = np.exp(f32(C) * log_g)
    return dmask.astype(f32), xi.astype(f32), zeta.astype(f32), g_chunk.astype(f32)


def _rope_tables(pos):
    half = RET_DK // 2
    freqs = ROPE_BASE ** (-jnp.arange(half, dtype=_F32) / half)
    ang = pos[:, None] * freqs[None, :]
    cos, sin = jnp.cos(ang), jnp.sin(ang)
    return jnp.concatenate([cos, cos], axis=-1), jnp.concatenate([-sin, sin], axis=-1)


def _const(a, shape=None):
    if shape is not None:
        a = np.ascontiguousarray(np.broadcast_to(a, shape))
    return jnp.asarray(a)


def _full(shape):
    n = len(shape)
    return pl.BlockSpec(shape, lambda *_: (0,) * n)


def _route_tail_kernel(x1_ref, g_ffn_ref, w_rt_ref, cnt_in_ref, dst_t_in, rw_in, h2_in,
                       dst_t_ref, rw_ref, cnt_ref, h2_ref, carry_scr, *, cap):
    del dst_t_in, rw_in, h2_in
    carry_scr[...] = cnt_in_ref[...]
    x1 = x1_ref[...]
    _route_block(x1.reshape(x1.shape[0] * x1.shape[1], D_MODEL), g_ffn_ref, w_rt_ref, carry_scr, cap, None,
                 h2_ref, rw_ref, dst_t_ref)
    cnt_ref[...] = carry_scr[...]


def _route_tail(x1, dst_t, rw, counts, h2, wts, cap, C, last):
    NSEQ = x1.shape[0]
    any_spec = pl.BlockSpec(memory_space=pl.ANY)
    return pl.pallas_call(
        functools.partial(_route_tail_kernel, cap=cap),
        grid=(1,),
        in_specs=[pl.BlockSpec((NSEQ, C, D_MODEL), lambda i: (0, last, 0)), _full((1, D_MODEL)),
                  _full((D_MODEL, ROUTER_W)), _full((1, LANES)), any_spec, any_spec, any_spec],
        out_specs=(pl.BlockSpec((NSEQ, SUBLANES, C), lambda i: (0, 0, last)),
                   pl.BlockSpec((NSEQ, C, LANES), lambda i: (0, last, 0)),
                   _full((1, LANES)),
                   pl.BlockSpec((NSEQ, C) + ROW_TILE, lambda i: (0, last, 0, 0))),
        out_shape=(jax.ShapeDtypeStruct(dst_t.shape, dst_t.dtype), jax.ShapeDtypeStruct(rw.shape, rw.dtype),
                   jax.ShapeDtypeStruct((1, LANES), _F32), jax.ShapeDtypeStruct(h2.shape, h2.dtype)),
        input_output_aliases={4: 0, 5: 1, 6: 3},
        scratch_shapes=[pltpu.VMEM((1, LANES), _F32)],
        compiler_params=pltpu.CompilerParams(dimension_semantics=("arbitrary",), vmem_limit_bytes=VMEM_LIMIT),
        name="mixer_route_tail",
    )(x1, wts["g_ffn"], wts["w_rt"], counts, dst_t, rw, h2)


def _mixer_prompt(x, wts, cap):
    B, L, _ = x.shape
    C = RET_CHUNK if L % RET_CHUNK == 0 else L
    n_c = L // C
    T = B * L
    dmask, xi, zeta, g_chunk = _decay_tables(C)
    dmask = _const(dmask)
    xi_b = _const(xi[:, :, None], (RET_HEADS, C, RET_DV))
    zeta_b = _const(zeta[:, :, None], (RET_HEADS, C, RET_DK))
    gc_b = _const(g_chunk[:, None, None], (RET_HEADS, 1, RET_DV))
    cos2, sin2 = _rope_tables(jnp.arange(L).astype(_F32))

    NSEQ = MIXER_SEQS if B % MIXER_SEQS == 0 else 1
    B2 = B // NSEQ
    T2 = T // NSEQ
    x = x.reshape(NSEQ, B2, L, D_MODEL)
    tok = lambda b, c: (0, b * n_c + c, 0)
    in_specs = [
        pl.BlockSpec((NSEQ, None, C, D_MODEL), lambda b, c: (0, b, c, 0)),
        _full((1, D_MODEL)), _full((D_MODEL, IN_WIDTH)), _full((POOL_GROUPS, POOL_GC, POOL_OUT_GC)),
        _full((1, D_MODEL)), _full((1, D_MODEL)), _full((D_MODEL, D_MODEL)), _full((1, D_MODEL)),
        _full((D_MODEL, ROUTER_W)),
        pl.BlockSpec((C, RET_DK), lambda b, c: (c, 0)), pl.BlockSpec((C, RET_DK), lambda b, c: (c, 0)),
        _full((RET_HEADS, C, C)), _full((RET_HEADS, C, RET_DV)), _full((RET_HEADS, C, RET_DK)),
        _full((RET_HEADS, 1, RET_DV)),
    ]
    R = NSEQ * C
    out_shape = (
        jax.ShapeDtypeStruct((NSEQ, T2, D_MODEL), _F32),
        jax.ShapeDtypeStruct((NSEQ, SUBLANES, T2), jnp.int32),
        jax.ShapeDtypeStruct((NSEQ, T2, LANES), _F32),
        jax.ShapeDtypeStruct((NSEQ, B2, POOL_HIST, POOL_IN), _F32),
        jax.ShapeDtypeStruct((NSEQ, B2, RET_HEADS, RET_DK, RET_DV), _F32),
        jax.ShapeDtypeStruct((1, LANES), _F32),
        jax.ShapeDtypeStruct((NSEQ, T2) + ROW_TILE, ROW_DTYPE),
    )
    prev = lambda b, c: jnp.maximum(b * n_c + c - 1, 0)
    out_specs = (
        pl.BlockSpec((NSEQ, C, D_MODEL), tok),
        pl.BlockSpec((NSEQ, SUBLANES, C), lambda b, c: (0, 0, prev(b, c))),
        pl.BlockSpec((NSEQ, C, LANES), lambda b, c: (0, prev(b, c), 0)),
        pl.BlockSpec((NSEQ, None, POOL_HIST, POOL_IN), lambda b, c: (0, b, 0, 0)),
        pl.BlockSpec((NSEQ, None, RET_HEADS, RET_DK, RET_DV), lambda b, c: (0, b, 0, 0, 0)),
        _full((1, LANES)),
        pl.BlockSpec((NSEQ, C) + ROW_TILE, lambda b, c: (0, prev(b, c), 0, 0)),
    )
    x1, dst_t, rw, npool, nret, counts, h2 = pl.pallas_call(
        functools.partial(_mixer_prompt_kernel, cap=cap),
        grid=(B2, n_c),
        in_specs=in_specs,
        out_specs=out_specs,
        out_shape=out_shape,
        scratch_shapes=[pltpu.VMEM((NSEQ, HALO + C, POOL_IN), _F32),
                        pltpu.VMEM((NSEQ, RET_HEADS, RET_DK, RET_DV), _F32),
                        pltpu.VMEM((R, D_MODEL), _BF),
                        pltpu.VMEM((1, LANES), _F32),
                        pltpu.VMEM((R, IN_WIDTH), _F32),
                        pltpu.VMEM((R, D_MODEL), _F32)],
        compiler_params=pltpu.CompilerParams(dimension_semantics=("arbitrary", "arbitrary"),
                                             vmem_limit_bytes=VMEM_LIMIT),
        name="mixer_prompt",
    )(x, wts["g_mix"], wts["w_in"], wts["w_pool"], wts["pool_scale"], wts["ret_gn"], wts["w_out"],
      wts["g_ffn"], wts["w_rt"], cos2, sin2, dmask, xi_b, zeta_b, gc_b)
    dst_t, rw, counts, h2 = _route_tail(x1, dst_t, rw, counts, h2, wts, cap, C, B2 * n_c - 1)
    dst_rows = jnp.moveaxis(dst_t[:, 0:2, :], 1, 0).reshape(2, T)
    return (x1.reshape(T, D_MODEL), dst_rows, rw.reshape(T, LANES),
            npool.reshape(B, POOL_HIST, POOL_IN), nret.reshape(B, RET_HEADS, RET_DK, RET_DV), counts,
            h2.reshape((T,) + ROW_TILE))


def _prep_weights(g_mix, w_pool, pool_scale, ret_gn, w_out, g_ffn, w_grp, w_exp):
    w_rt = jnp.concatenate([w_grp, w_exp.reshape(D_MODEL, N_EXPERTS)], axis=1)
    w_rt = jnp.pad(w_rt, ((0, 0), (0, ROUTER_W - w_rt.shape[1])))
    row = lambda v: v.reshape(1, D_MODEL)
    return dict(g_mix=row(g_mix), w_pool=w_pool.astype(_BF), pool_scale=row(pool_scale),
                ret_gn=row(ret_gn), w_out=w_out.astype(_BF), g_ffn=row(g_ffn), w_rt=w_rt.astype(_BF))


PROJ_COLS = 512


def _sample_proj_kernel(x_ref, g_mix_ref, w_in_ref, cos_ref, sin_ref, zeta_ref,
                        w_bf_ref, u_ref, q_ref, k_ref, kq_ref, v_ref, ga_ref, gb_ref, h_scr, z_scr):
    c = pl.program_id(0)

    @pl.when(c == 0)
    def _():
        h_scr[...] = _rms(x_ref[...], g_mix_ref[...]).astype(_BF)

    w = w_in_ref[...].astype(_BF)
    w_bf_ref[...] = w
    z_scr[c] = _dot(h_scr[...], w)

    @pl.when(c == pl.num_programs(0) - 1)
    def _():
        def cols(off, width):
            parts = [z_scr[i] for i in range(off // PROJ_COLS, (off + width) // PROJ_COLS)]
            return parts[0] if len(parts) == 1 else jnp.concatenate(parts, axis=1)

        u_ref[...] = cols(OFF_U, POOL_IN)
        q = cols(OFF_Q, QK_W)
        k = cols(OFF_K, QK_W)
        for j in range(RET_HEADS):
            qs = slice(j * RET_DK, (j + 1) * RET_DK)
            qf = _rotary(q[:, qs], cos_ref[...], sin_ref[...])
            kf = _rotary(k[:, qs], cos_ref[...], sin_ref[...]) * (RET_DK ** -0.5)
            q_ref[:, qs] = qf
            k_ref[:, qs] = kf
            kq_ref[0, :, qs] = kf * zeta_ref[j]
            kq_ref[1, :, qs] = _round_bf16(qf)
        v_ref[...] = cols(OFF_V, V_W)
        ga_ref[...] = cols(OFF_GA, D_MODEL)
        gb_ref[...] = cols(OFF_GB, D_MODEL)


def _sample_proj(x, g_mix, w_in, pos0):
    Bs = x.shape[0]
    offsets = (OFF_U, OFF_Q, OFF_K, OFF_V, OFF_GA, OFF_GB, IN_WIDTH)
    assert all(o % PROJ_COLS == 0 for o in offsets)
    n_c = IN_WIDTH // PROJ_COLS
    _, _, zeta, _ = _decay_tables(1)
    cos2, sin2 = _rope_tables((pos0 + jnp.arange(1)).astype(_F32))
    shapes = ((Bs, POOL_IN), (Bs, QK_W), (Bs, QK_W), (2, Bs, QK_W), (Bs, V_W), (Bs, D_MODEL), (Bs, D_MODEL))
    return pl.pallas_call(
        _sample_proj_kernel,
        grid=(n_c,),
        in_specs=[_full((Bs, D_MODEL)), _full((1, D_MODEL)), pl.BlockSpec((D_MODEL, PROJ_COLS), lambda c: (0, c)),
                  _full((1, RET_DK)), _full((1, RET_DK)), _full((RET_HEADS, 1, 1))],
        out_specs=(pl.BlockSpec((D_MODEL, PROJ_COLS), lambda c: (0, c)),) + tuple(_full(sh) for sh in shapes),
        out_shape=(jax.ShapeDtypeStruct((D_MODEL, IN_WIDTH), _BF),)
        + tuple(jax.ShapeDtypeStruct(sh, _F32) for sh in shapes),
        scratch_shapes=[pltpu.VMEM((Bs, D_MODEL), _BF), pltpu.VMEM((n_c, Bs, PROJ_COLS), _F32)],
        compiler_params=pltpu.CompilerParams(dimension_semantics=("arbitrary",), vmem_limit_bytes=VMEM_LIMIT),
        name="sample_proj",
    )(x, g_mix, w_in, cos2, sin2, _const(zeta[:, :, None], (RET_HEADS, 1, 1)))


def _mixer_sample_kernel(x_ref, u_ref, q_ref, k_ref, v_ref, ga_ref, gb_ref, spool_ref, qs_ref,
                         w_pool_ref, pscale_ref, gn_ref, w_out_ref, g_ffn_ref, w_rt_ref, dm_ref, xi_ref,
                         carry_in_ref,
                         x1_ref, dst_ref, rw_ref, npool_ref, cnt_ref, h2_ref,
                         mixed_scr, carry_scr, *, cap):
    u = u_ref[...]
    npool_ref[0:POOL_HIST - 1] = spool_ref[1:POOL_HIST]
    npool_ref[POOL_HIST - 1] = u
    for j in range(RET_HEADS):
        cs = slice(j * POOL_GC, (j + 1) * POOL_GC)
        qs = slice(j * RET_DK, (j + 1) * RET_DK)
        osl = slice(j * POOL_OUT_GC, (j + 1) * POOL_OUT_GC)
        win = POOL_WINDOWS[j]
        s = u[:, cs]
        for r in range(POOL_HIST - (win - 1), POOL_HIST):
            s = s + spool_ref[r, :, cs]
        pooled = s / jnp.float32(win) - u[:, cs]
        pool_out = _dot(pooled.astype(_BF), w_pool_ref[j]) * pscale_ref[:, osl]
        score = jnp.sum(q_ref[:, qs] * k_ref[:, qs], axis=-1, keepdims=True) * dm_ref[j]
        q_state = jnp.concatenate([qs_ref[hf, pl.ds(j, u.shape[0], stride=RET_HEADS), :]
                                   for hf in range(qs_ref.shape[0])], axis=1)
        o = score * v_ref[:, osl] + q_state * xi_ref[j]
        ret_out = _group_norm(o, gn_ref[:, osl])
        mixed_scr[:, osl] = (jax.nn.sigmoid(ga_ref[:, osl]) * pool_out
                             + jax.nn.sigmoid(gb_ref[:, osl]) * ret_out).astype(_BF)
    h2, e0, e1 = _post_mix(x_ref[...], mixed_scr, w_out_ref, g_ffn_ref, w_rt_ref, x1_ref, rw_ref)
    h2_ref[...] = _pack_rows(h2).reshape(h2_ref.shape)
    carry_scr[...] = carry_in_ref[...]
    dst_ref[...] = _rank_rows(e0, e1, cap, carry_scr).astype(jnp.int32)
    cnt_ref[...] = carry_scr[...]


def _mixer_sample(x, proj, state_pool, q_state, wts, carry_in, cap):
    Bs = x.shape[0]
    assert Bs == LANES and POOL_GROUPS == RET_HEADS
    u, q, k, v, ga, gb = proj
    dmask, xi, _, _ = _decay_tables(1)
    dm_b = _const(dmask, (RET_HEADS, 1, 1))
    xi_b = _const(xi[:, :, None], (RET_HEADS, 1, RET_DV))

    in_specs = [
        _full((Bs, D_MODEL)),
        _full((Bs, POOL_IN)), _full((Bs, QK_W)), _full((Bs, QK_W)), _full((Bs, V_W)),
        _full((Bs, D_MODEL)), _full((Bs, D_MODEL)),
        _full((POOL_HIST, Bs, POOL_IN)), _full(q_state.shape),
        _full((POOL_GROUPS, POOL_GC, POOL_OUT_GC)),
        _full((1, D_MODEL)), _full((1, D_MODEL)), _full((D_MODEL, D_MODEL)), _full((1, D_MODEL)),
        _full((D_MODEL, ROUTER_W)),
        _full((RET_HEADS, 1, 1)), _full((RET_HEADS, 1, RET_DV)),
        _full((1, LANES)),
    ]
    out_shape = (
        jax.ShapeDtypeStruct((Bs, D_MODEL), _F32),
        jax.ShapeDtypeStruct((Bs, LANES), jnp.int32),
        jax.ShapeDtypeStruct((Bs, LANES), _F32),
        jax.ShapeDtypeStruct((POOL_HIST, Bs, POOL_IN), _F32),
        jax.ShapeDtypeStruct((1, LANES), _F32),
        jax.ShapeDtypeStruct((Bs,) + ROW_TILE, ROW_DTYPE),
    )
    out_specs = (
        _full((Bs, D_MODEL)), _full((Bs, LANES)), _full((Bs, LANES)), _full((POOL_HIST, Bs, POOL_IN)),
        _full((1, LANES)), _full((Bs,) + ROW_TILE),
    )
    return pl.pallas_call(
        functools.partial(_mixer_sample_kernel, cap=cap),
        grid=(1,),
        in_specs=in_specs,
        out_specs=out_specs,
        out_shape=out_shape,
        scratch_shapes=[pltpu.VMEM((Bs, D_MODEL), _BF), pltpu.VMEM((1, LANES), _F32)],
        compiler_params=pltpu.CompilerParams(dimension_semantics=("arbitrary",), vmem_limit_bytes=VMEM_LIMIT),
        name="mixer_sample",
    )(x, u, q, k, v, ga, gb, state_pool, q_state, wts["w_pool"], wts["pool_scale"], wts["ret_gn"],
      wts["w_out"], wts["g_ffn"], wts["w_rt"], dm_b, xi_b, carry_in)


SC_CORES = 2
SC_SUBCORES = 16
SC_LANES = 16
STATE_ROWS = 32
STATE_UNROLL = 8
STATE_COLS = 8


def _sc_round_bf16(x):
    b = lax.bitcast_convert_type(x, jnp.uint32)
    r = b + jnp.uint32(0x7FFF) + ((b >> 16) & jnp.uint32(1))
    return lax.bitcast_convert_type(r & jnp.uint32(0xFFFF0000), _F32)


def _sc_state_update(s0, kq, v):
    P = s0.shape[0]
    n_workers = SC_CORES * SC_SUBCORES
    R = STATE_ROWS
    n_parts = RET_DK // R
    n_vc = RET_DV // SC_LANES
    assert P % n_workers == 0 and RET_DK % R == 0 and n_parts % 2 == 0 and n_parts >= 2
    assert R % STATE_UNROLL == 0 and n_vc % STATE_COLS == 0
    per_w = P // n_workers
    _, _, _, g_chunk = _decay_tables(1)
    decay = _const(g_chunk[:, None], (RET_HEADS, SC_LANES))
    mesh = plsc.VectorSubcoreMesh(core_axis_name="c", subcore_axis_name="s")

    def body(s0_hbm, kq_hbm, v_hbm, g_hbm, out_hbm, qs_hbm, in_v, out_v, k_v, q_v, v_v, g_v, o_v, sem_in, sem_out):
        wid = lax.axis_index("s") * SC_CORES + lax.axis_index("c")
        base = wid * per_w
        pltpu.sync_copy(kq_hbm.at[0, pl.ds(base, per_w)], k_v)
        pltpu.sync_copy(kq_hbm.at[1, pl.ds(base, per_w)], q_v)
        pltpu.sync_copy(v_hbm.at[pl.ds(base, per_w)], v_v)
        pltpu.sync_copy(g_hbm, g_v)

        def load(p, part, slot):
            return pltpu.make_async_copy(s0_hbm.at[p, pl.ds(part * R, R)], in_v.at[slot], sem_in.at[slot])

        def store(p, part, slot):
            return pltpu.make_async_copy(out_v.at[slot], out_hbm.at[p, pl.ds(part * R, R)], sem_out.at[slot])

        def o_at(pp, c):
            return (c * SC_LANES // LANES, pp, pl.ds(c * SC_LANES % LANES, SC_LANES))

        load(base, 0, 0).start()

        def pair(pp, carry):
            p = base + pp
            g = g_v[p % RET_HEADS, :]
            pidx = jnp.full((SC_LANES,), pp, jnp.int32)
            for c in range(n_vc):
                o_v[o_at(pp, c)] = jnp.zeros((SC_LANES,), _F32)
            for part in range(n_parts):
                slot = part % 2
                load(p, part, slot).wait()
                if part + 1 < n_parts:
                    load(p, part + 1, 1 - slot).start()
                else:
                    @pl.when(pp + 1 < per_w)
                    def _():
                        load(p + 1, 0, 1 - slot).start()

                if part < 2:
                    @pl.when(pp > 0)
                    def _():
                        store(p, part, slot).wait()
                else:
                    store(p, part, slot).wait()

                def rows(i8, c2):
                    for c0 in range(0, n_vc, STATE_COLS):
                        cols = [pl.ds((c0 + c) * SC_LANES, SC_LANES) for c in range(STATE_COLS)]
                        vs = [v_v[pp, cs] for cs in cols]
                        acc = [None] * STATE_COLS
                        for j in range(STATE_UNROLL):
                            i = i8 * STATE_UNROLL + j
                            ridx = jnp.full((SC_LANES,), part * R + i, jnp.int32)
                            ki = plsc.load_gather(k_v, [pidx, ridx])
                            qi = plsc.load_gather(q_v, [pidx, ridx])
                            ss = [in_v[slot, i, cs] for cs in cols]
                            for c, cs in enumerate(cols):
                                out_v[slot, i, cs] = g * ss[c] + ki * vs[c]
                            for c in range(STATE_COLS):
                                t = qi * _sc_round_bf16(ss[c])
                                acc[c] = t if acc[c] is None else acc[c] + t
                        for c in range(STATE_COLS):
                            o_v[o_at(pp, c0 + c)] = o_v[o_at(pp, c0 + c)] + acc[c]
                    return c2

                lax.fori_loop(0, R // STATE_UNROLL, rows, 0)
                store(p, part, slot).start()
            return carry

        lax.fori_loop(0, per_w, pair, 0)
        for slot in range(2):
            store(base, slot, slot).wait()
        for hf in range(RET_DV // LANES):
            pltpu.sync_copy(o_v.at[hf], qs_hbm.at[hf, pl.ds(base, per_w)])

    keys = pltpu.VMEM((per_w, RET_DK), _F32)
    return pl.kernel(
        body, mesh=mesh,
        out_type=(jax.ShapeDtypeStruct(s0.shape, _F32), jax.ShapeDtypeStruct((RET_DV // LANES, P, LANES), _F32)),
        scratch_types=[pltpu.VMEM((2, R, RET_DV), _F32), pltpu.VMEM((2, R, RET_DV), _F32), keys, keys,
                       pltpu.VMEM((per_w, RET_DV), _F32), pltpu.VMEM((RET_HEADS, SC_LANES), _F32),
                       pltpu.VMEM((RET_DV // LANES, per_w, LANES), _F32),
                       pltpu.SemaphoreType.DMA((2,)), pltpu.SemaphoreType.DMA((2,))],
        compiler_params=dataclasses.replace(pltpu.CompilerParams(), needs_layout_passes=False),
        name="sample_state_update",
    )(s0, kq, v, decay)


IDX_CHUNK = 1024
ISSUE_UNROLL = 8
SC_WINDOW = 64


def _sc_scatter(h2, dst_rows, n_rows):
    T = h2.shape[0]
    n_workers = SC_CORES * SC_SUBCORES
    W = SC_WINDOW
    assert T % (n_workers * W) == 0
    per_w = T // n_workers
    n_win = per_w // W
    d0 = dst_rows[0].reshape(T // W, W)
    d1 = dst_rows[1].reshape(T // W, W)
    mesh = plsc.VectorSubcoreMesh(core_axis_name="c", subcore_axis_name="s")

    def body(h2_hbm, d0_hbm, d1_hbm, xs_hbm, i0_v, i1_v, rows_v, sem_load, sem_store):
        wid = lax.axis_index("s") * SC_CORES + lax.axis_index("c")
        base = wid * per_w
        pltpu.sync_copy(d0_hbm.at[pl.ds(wid * n_win, n_win)], i0_v)
        pltpu.sync_copy(d1_hbm.at[pl.ds(wid * n_win, n_win)], i1_v)
        pltpu.async_copy(h2_hbm.at[pl.ds(base, W)], rows_v.at[0], sem_load)
        for i in range(n_win):
            b = i % 2
            pltpu.make_async_copy(h2_hbm.at[pl.ds(base, W)], rows_v.at[b], sem_load).wait()
            if i + 1 < n_win:
                pltpu.async_copy(h2_hbm.at[pl.ds(base + (i + 1) * W, W)], rows_v.at[1 - b], sem_load)
            c0 = pltpu.async_copy(rows_v.at[b], xs_hbm.at[i0_v.at[i]], sem_store)
            c1 = pltpu.async_copy(rows_v.at[b], xs_hbm.at[i1_v.at[i]], sem_store)
            c0.wait()
            c1.wait()

    return pl.kernel(
        body, mesh=mesh,
        out_type=jax.ShapeDtypeStruct((n_rows,) + ROW_TILE, ROW_DTYPE),
        scratch_types=[pltpu.VMEM((n_win, W), jnp.int32), pltpu.VMEM((n_win, W), jnp.int32),
                       pltpu.VMEM((2, W) + ROW_TILE, ROW_DTYPE),
                       pltpu.SemaphoreType.DMA, pltpu.SemaphoreType.DMA],
        name="moe_sc_scatter",
    )(h2, d0, d1)


ZERO_CHUNK = 64


def _finish_dispatch_kernel(zstart_ref, zchunks_ref, slots_hbm, h2_ref, xs_in_hbm, xs_hbm, idx_smem, zbuf, sem_idx,
                            sem_rows, sem_zero):
    del xs_in_hbm
    TS = h2_ref.shape[0]
    idx_cp = pltpu.make_async_copy(slots_hbm.at[pl.ds(0, IDX_CHUNK)], idx_smem, sem_idx)
    idx_cp.start()
    zbuf[...] = jnp.zeros(zbuf.shape, zbuf.dtype)

    def zero_copy(e, j):
        return pltpu.make_async_copy(zbuf, xs_hbm.at[pl.ds(zstart_ref[e] + j * ZERO_CHUNK, ZERO_CHUNK)], sem_zero)

    for e in range(N_EXPERTS):
        lax.fori_loop(0, zchunks_ref[e], lambda j, c, e=e: (zero_copy(e, j).start(), c)[1], 0)
    idx_cp.wait()

    def issue(r, carry):
        for kk in range(2):
            pltpu.make_async_copy(h2_ref.at[r], xs_hbm.at[idx_smem[2 * r + kk]], sem_rows).start(priority=kk)
        return carry

    lax.fori_loop(0, TS, issue, 0, unroll=ISSUE_UNROLL)
    for e in range(N_EXPERTS):
        lax.fori_loop(0, zchunks_ref[e], lambda j, c, e=e: (zero_copy(e, j).wait(), c)[1], 0)
    for kk in range(2):
        pltpu.make_async_copy(h2_ref, xs_hbm.at[pl.ds(0, TS)], sem_rows).wait()


def _finish_dispatch(xs, h2s, slots, zstart, zchunks):
    Bs = h2s.shape[0]
    assert 2 * Bs <= IDX_CHUNK and slots.shape[0] == IDX_CHUNK and MOE_BLOCK % ZERO_CHUNK == 0
    any_spec = pl.BlockSpec(memory_space=pl.ANY)
    grid_spec = pltpu.PrefetchScalarGridSpec(
        num_scalar_prefetch=2,
        grid=(1,),
        in_specs=[any_spec, pl.BlockSpec((Bs,) + ROW_TILE, lambda t, z, n: (0, 0, 0)), any_spec],
        out_specs=any_spec,
        scratch_shapes=[pltpu.SMEM((IDX_CHUNK,), jnp.int32), pltpu.VMEM((ZERO_CHUNK,) + ROW_TILE, ROW_DTYPE),
                        pltpu.SemaphoreType.DMA, pltpu.SemaphoreType.DMA, pltpu.SemaphoreType.DMA],
    )
    return pl.pallas_call(
        _finish_dispatch_kernel,
        grid_spec=grid_spec,
        out_shape=jax.ShapeDtypeStruct(xs.shape, ROW_DTYPE),
        input_output_aliases={4: 0},
        compiler_params=pltpu.CompilerParams(dimension_semantics=("arbitrary",), has_side_effects=True),
        name="moe_finish_dispatch",
    )(zstart, zchunks, slots, h2s, xs)


FFN_IN_BUFFERS = 4
FFN_OUT_BUFFERS = 3


def _ffn_kernel(cnt_ref, xs_hbm, w1_hbm, w3_hbm, w2_hbm, ys_hbm,
                xs_buf, ys_buf, st1, st3, st2, w13_scr, w2_scr, t_row, t_exp, t_len, sem_in, sem_out, sem_w,
                *, cap_blocks):
    B = MOE_BLOCK

    def fill_expert(e, g):
        nb = (cnt_ref[e] + (B - 1)) // B

        def fill_block(j, carry):
            t_row[g + j] = e * cap_blocks + j
            t_exp[g + j] = e
            t_len[g + j] = nb
            return carry

        lax.fori_loop(0, nb, fill_block, 0)
        return g + nb

    n_used = lax.fori_loop(0, N_EXPERTS, fill_expert, 0)

    def in_copy(g):
        s = g % FFN_IN_BUFFERS
        return pltpu.make_async_copy(xs_hbm.at[pl.ds(t_row[g] * B, B)], xs_buf.at[s], sem_in.at[s])

    def out_copy(g):
        s = g % FFN_OUT_BUFFERS
        return pltpu.make_async_copy(ys_buf.at[s], ys_hbm.at[pl.ds(t_row[g] * B, B)], sem_out.at[s])

    def weight_copies(e, s):
        return (pltpu.make_async_copy(w1_hbm.at[e], st1.at[s], sem_w.at[s]),
                pltpu.make_async_copy(w3_hbm.at[e], st3.at[s], sem_w.at[s]),
                pltpu.make_async_copy(w2_hbm.at[e], st2.at[s], sem_w.at[s]))

    @pl.when(n_used > 0)
    def _():
        for cp in weight_copies(t_exp[0], 0):
            cp.start()

    for g0 in range(FFN_IN_BUFFERS - 1):
        @pl.when(g0 < n_used)
        def _(g0=g0):
            in_copy(g0).start()

    def block(g, wslot):
        first = (g == 0) | (t_exp[g] != t_exp[jnp.maximum(g - 1, 0)])
        wslot = jnp.where(first & (g > 0), 1 - wslot, wslot)

        @pl.when(first)
        def _():
            for cp in weight_copies(t_exp[g], wslot):
                cp.wait()
            nxt = g + t_len[g]

            @pl.when(nxt < n_used)
            def _():
                for cp in weight_copies(t_exp[jnp.minimum(nxt, n_used - 1)], 1 - wslot):
                    cp.start()

            w13_scr[:, 0:D_EXPERT] = st1[wslot].astype(_BF)
            w13_scr[:, D_EXPERT:2 * D_EXPERT] = st3[wslot].astype(_BF)
            w2_scr[...] = st2[wslot].astype(_BF)

        in_copy(g).wait()

        @pl.when(g + FFN_IN_BUFFERS - 1 < n_used)
        def _():
            in_copy(g + FFN_IN_BUFFERS - 1).start()

        @pl.when(g >= FFN_OUT_BUFFERS)
        def _():
            out_copy(g - FFN_OUT_BUFFERS).wait()

        xb = _unpack_rows(xs_buf[g % FFN_IN_BUFFERS]).astype(_BF)
        ab = _dot(xb, w13_scr[...])
        hid = jax.nn.silu(ab[:, 0:D_EXPERT]) * ab[:, D_EXPERT:2 * D_EXPERT]
        ys_buf[g % FFN_OUT_BUFFERS] = _pack_rows(_dot(hid.astype(_BF), w2_scr[...]))
        out_copy(g).start()
        return wslot

    lax.fori_loop(0, n_used, block, 0)

    for back in range(FFN_OUT_BUFFERS, 0, -1):
        @pl.when(n_used >= back)
        def _(back=back):
            out_copy(n_used - back).wait()


def _ffn(xs, counts, w1, w3, w2, cap_blocks, n_blocks):
    any_spec = pl.BlockSpec(memory_space=pl.ANY)
    row_buf = lambda n: pltpu.VMEM((n, MOE_BLOCK) + ROW_TILE, ROW_DTYPE)
    grid_spec = pltpu.PrefetchScalarGridSpec(
        num_scalar_prefetch=1,
        grid=(1,),
        in_specs=[any_spec, any_spec, any_spec, any_spec],
        out_specs=any_spec,
        scratch_shapes=[row_buf(FFN_IN_BUFFERS), row_buf(FFN_OUT_BUFFERS),
                        pltpu.VMEM((2, D_MODEL, D_EXPERT), _F32), pltpu.VMEM((2, D_MODEL, D_EXPERT), _F32),
                        pltpu.VMEM((2, D_EXPERT, D_MODEL), _F32),
                        pltpu.VMEM((D_MODEL, 2 * D_EXPERT), _BF), pltpu.VMEM((D_EXPERT, D_MODEL), _BF),
                        pltpu.SMEM((n_blocks,), jnp.int32), pltpu.SMEM((n_blocks,), jnp.int32),
                        pltpu.SMEM((n_blocks,), jnp.int32),
                        pltpu.SemaphoreType.DMA((FFN_IN_BUFFERS,)), pltpu.SemaphoreType.DMA((FFN_OUT_BUFFERS,)),
                        pltpu.SemaphoreType.DMA((2,))],
    )
    return pl.pallas_call(
        functools.partial(_ffn_kernel, cap_blocks=cap_blocks),
        grid_spec=grid_spec,
        out_shape=jax.ShapeDtypeStruct(xs.shape, ROW_DTYPE),
        compiler_params=pltpu.CompilerParams(dimension_semantics=("arbitrary",), vmem_limit_bytes=VMEM_LIMIT,
                                             has_side_effects=True),
        name="moe_ffn",
    )(counts, xs, w1, w3, w2)


def _combine_kernel(slots_hbm, x1_ref, rw_ref, g_ref, ys_hbm, y_ref, idx_smem, buf, sem_idx, sem_rows):
    t = pl.program_id(0)
    TC = x1_ref.shape[0]
    idx_cp = pltpu.make_async_copy(slots_hbm.at[pl.ds(t * IDX_CHUNK, IDX_CHUNK)], idx_smem, sem_idx)
    idx_cp.start()
    idx_cp.wait()

    def issue(r, carry):
        for kk in range(2):
            pltpu.make_async_copy(ys_hbm.at[idx_smem[2 * r + kk]], buf.at[kk, r], sem_rows).start(priority=kk)
        return carry

    lax.fori_loop(0, TC, issue, 0, unroll=ISSUE_UNROLL)
    for kk in range(2):
        pltpu.make_async_copy(ys_hbm.at[pl.ds(0, TC)], buf.at[kk], sem_rows).wait()

    w = rw_ref[...]
    y0 = _unpack_rows(buf[0])
    y1 = _unpack_rows(buf[1])
    x2 = x1_ref[...] + (w[:, 0:1] * y0 + w[:, 1:2] * y1)
    y_ref[...] = _rms(x2, g_ref[...])


SC_GATHER_CHUNKS = 4
DENSE_TILE = 1024
SC_GATHER_BUFFERS = 3
SC_GATHER_MAX_WINDOW = 80


def _sc_gather(ys, rows):
    Tg = rows.shape[1]
    n_workers = SC_CORES * SC_SUBCORES
    per_w = Tg // n_workers
    assert per_w * n_workers == Tg and per_w % SUBLANES == 0
    W = max(w for w in range(SUBLANES, SC_GATHER_MAX_WINDOW + 1, SUBLANES) if per_w % w == 0)
    n_win = per_w // W
    NB = SC_GATHER_BUFFERS
    idx = rows.reshape(2, n_workers, n_win, W)
    jobs = [(kk, i) for i in range(n_win) for kk in range(2)]
    mesh = plsc.VectorSubcoreMesh(core_axis_name="c", subcore_axis_name="s")

    def body(ys_hbm, idx_hbm, out_hbm, i_v, rows_v, sem_g, sem_s):
        wid = lax.axis_index("s") * SC_CORES + lax.axis_index("c")
        base = wid * per_w
        for kk in range(2):
            pltpu.sync_copy(idx_hbm.at[kk, wid], i_v.at[kk])

        def gather(j):
            kk, i = jobs[j]
            return pltpu.make_async_copy(ys_hbm.at[i_v.at[kk, i]], rows_v.at[j % NB], sem_g.at[j % NB])

        def store(j):
            kk, i = jobs[j]
            return pltpu.make_async_copy(rows_v.at[j % NB], out_hbm.at[kk, pl.ds(base + i * W, W)], sem_s.at[j % NB])

        gather(0).start()
        for j in range(len(jobs)):
            gather(j).wait()
            store(j).start()
            if j + 1 < len(jobs):
                if j + 1 >= NB:
                    store(j + 1 - NB).wait()
                gather(j + 1).start()
        for j in range(max(0, len(jobs) - NB), len(jobs)):
            store(j).wait()

    return pl.kernel(
        body, mesh=mesh,
        out_type=jax.ShapeDtypeStruct((2, Tg) + ROW_TILE, ROW_DTYPE),
        scratch_types=[pltpu.VMEM((2, n_win, W), jnp.int32), pltpu.VMEM((NB, W) + ROW_TILE, ROW_DTYPE),
                       pltpu.SemaphoreType.DMA((NB,)), pltpu.SemaphoreType.DMA((NB,))],
        name="moe_sc_gather",
    )(ys, idx)


def _combine_dense_kernel(x1_ref, rw_ref, g_ref, rows_ref, *rest):
    y_ref = rest[-1]
    w = rw_ref[...]
    y0 = _unpack_rows(rows_ref[0])
    y1 = _unpack_rows(rows_ref[1])
    x2 = x1_ref[...] + (w[:, 0:1] * y0 + w[:, 1:2] * y1)
    y_ref[...] = _rms(x2, g_ref[...])


def _combine_dense(x1, rw, rows, y, g_final, tile, first_tile):
    Tg = rows.shape[1]
    assert Tg % tile == 0
    tok = lambda t: (first_tile + t, 0)
    in_specs = [pl.BlockSpec((tile, D_MODEL), tok), pl.BlockSpec((tile, LANES), tok), _full((1, D_MODEL)),
                pl.BlockSpec((2, tile) + ROW_TILE, lambda t: (0, t, 0, 0))]
    args = [x1, rw, g_final.reshape(1, D_MODEL), rows]
    aliases = {}
    if y is not None:
        in_specs.append(pl.BlockSpec(memory_space=pl.ANY))
        args.append(y)
        aliases = {4: 0}
    return pl.pallas_call(
        _combine_dense_kernel,
        grid=(Tg // tile,),
        in_specs=in_specs,
        out_specs=pl.BlockSpec((tile, D_MODEL), tok),
        out_shape=jax.ShapeDtypeStruct(x1.shape, _F32),
        input_output_aliases=aliases,
        compiler_params=pltpu.CompilerParams(dimension_semantics=("arbitrary",), vmem_limit_bytes=VMEM_LIMIT),
        name="moe_combine_dense",
    )(*args)


def _combine(x1, rw, slots, ys, g_final, tile, n_tiles):
    T = x1.shape[0]
    assert T % tile == 0 and 2 * tile <= IDX_CHUNK and slots.shape[0] == n_tiles * IDX_CHUNK
    any_spec = pl.BlockSpec(memory_space=pl.ANY)
    return pl.pallas_call(
        _combine_kernel,
        grid=(n_tiles,),
        in_specs=[any_spec, pl.BlockSpec((tile, D_MODEL), lambda t: (t, 0)),
                  pl.BlockSpec((tile, LANES), lambda t: (t, 0)), _full((1, D_MODEL)), any_spec],
        out_specs=pl.BlockSpec((tile, D_MODEL), lambda t: (t, 0)),
        out_shape=jax.ShapeDtypeStruct((T, D_MODEL), _F32),
        scratch_shapes=[pltpu.SMEM((IDX_CHUNK,), jnp.int32), pltpu.VMEM((2, tile) + ROW_TILE, ROW_DTYPE),
                        pltpu.SemaphoreType.DMA, pltpu.SemaphoreType.DMA],
        compiler_params=pltpu.CompilerParams(dimension_semantics=("arbitrary",), vmem_limit_bytes=VMEM_LIMIT),
        name="moe_combine",
    )(slots, x1, rw, g_final.reshape(1, D_MODEL), ys)


def _tile_for(n_tokens):
    tile = IDX_CHUNK // 2
    return tile if n_tokens % tile == 0 else n_tokens


def _chunked_slots(slot, tile):
    n_tiles = slot.shape[0] // tile
    s = slot.reshape(n_tiles, 2 * tile)
    return jnp.pad(s, ((0, 0), (0, IDX_CHUNK - 2 * tile))).reshape(-1)


def kernel(x_prompt, x_sample, state_pool, state_ret, g_mix, w_in, w_pool, pool_scale, ret_gn, w_out, g_ffn, w_grp, w_exp, w1, w3, w2, g_final):
    Bp, Lp, _ = x_prompt.shape
    Bs = x_sample.shape[0]
    Tp = Bp * Lp
    wts = _prep_weights(g_mix[0], w_pool[0], pool_scale[0], ret_gn[0], w_out[0], g_ffn[0], w_grp[0], w_exp[0])

    T_all = Tp + Bs
    cap = (-(-T_all // MOE_BLOCK) + 1) * MOE_BLOCK
    cap_blocks = cap // MOE_BLOCK

    xs_tok = x_sample.reshape(Bs, D_MODEL)
    wts["w_in"], u_s, q_s, k_s, kq_s, v_s, ga_s, gb_s = _sample_proj(xs_tok, wts["g_mix"], w_in[0], PAST_LEN)
    n_pairs = Bs * RET_HEADS
    nret_s, q_state = _sc_state_update(state_ret[0].reshape(n_pairs, RET_DK, RET_DV),
                                       kq_s.reshape(2, n_pairs, RET_DK), v_s.reshape(n_pairs, RET_DV))
    x1p, dst_rows_p, rwp, npool_p, nret_p, counts_p, h2p = _mixer_prompt(x_prompt, wts, cap)
    xs = _sc_scatter(h2p, dst_rows_p, N_EXPERTS * cap)
    x1s, dst_s, rws, npool_s, counts, h2s = _mixer_sample(
        xs_tok, (u_s, q_s, k_s, v_s, ga_s, gb_s), jnp.swapaxes(state_pool[0], 0, 1), q_state, wts, counts_p, cap)

    counts = counts[0, :N_EXPERTS].astype(jnp.int32)
    tile_p, tile_s = _tile_for(Tp), _tile_for(Bs)
    slots_s = _chunked_slots(dst_s[:, :2], tile_s)
    zstart = jnp.arange(N_EXPERTS, dtype=jnp.int32) * cap + counts
    zchunks = ((-counts) % MOE_BLOCK + ZERO_CHUNK - 1) // ZERO_CHUNK
    xs = _finish_dispatch(xs, h2s, slots_s, zstart, zchunks.astype(jnp.int32))
    n_blocks = -(-2 * T_all // MOE_BLOCK) + N_EXPERTS
    ys = _ffn(xs, counts, w1[0], w3[0], w2[0], cap_blocks, n_blocks)
    tile_d = DENSE_TILE if Tp % DENSE_TILE == 0 else tile_p
    n_tiles_d = Tp // tile_d
    chunk_tiles = [n_tiles_d // SC_GATHER_CHUNKS + (k < n_tiles_d % SC_GATHER_CHUNKS)
                   for k in range(SC_GATHER_CHUNKS)]
    gathered, first = [], 0
    for nt in chunk_tiles:
        if nt:
            gathered.append((first, _sc_gather(ys, dst_rows_p[:, first * tile_d:(first + nt) * tile_d])))
            first += nt
    y_s = _combine(x1s, rws, slots_s, ys, g_final, tile_s, 1)
    y_p = None
    for first, rows in gathered:
        y_p = _combine_dense(x1p, rwp, rows, y_p, g_final, tile_d, first)

    return (y_p.reshape(Bp, Lp, D_MODEL), y_s.reshape(Bs, 1, D_MODEL),
            npool_p[None], nret_p[None], jnp.swapaxes(npool_s, 0, 1)[None], nret_s.reshape(state_ret.shape))
```

```python
import dataclasses
import functools

import jax
import jax.numpy as jnp
import numpy as np
from jax import lax
from jax.experimental import pallas as pl
from jax.experimental.pallas import tpu as pltpu
from jax.experimental.pallas import tpu_sc as plsc

D_MODEL = 1024
EPS = 1e-6
POOL_GROUPS = 4
POOL_IN = D_MODEL // 2
POOL_GC = POOL_IN // POOL_GROUPS
POOL_OUT_GC = D_MODEL // POOL_GROUPS
POOL_WINDOWS = (2, 4, 8, 16)
POOL_HIST = max(POOL_WINDOWS) - 1
RET_HEADS = 4
RET_DK = D_MODEL // 8
RET_DV = D_MODEL // RET_HEADS
ROPE_BASE = 10000.0
PAST_LEN = 16384
N_GROUPS = 4
EXPERTS_PER_GROUP = 8
N_EXPERTS = N_GROUPS * EXPERTS_PER_GROUP
D_EXPERT = D_MODEL // 4
QK_W = RET_HEADS * RET_DK
V_W = RET_HEADS * RET_DV
OFF_U = 0
OFF_Q = POOL_IN
OFF_K = OFF_Q + QK_W
OFF_V = OFF_K + QK_W
OFF_GA = OFF_V + V_W
OFF_GB = OFF_GA + D_MODEL
IN_WIDTH = OFF_GB + D_MODEL

LANES = 128
SUBLANES = 8
ROW_WORDS = D_MODEL // 2
ROW_TILE = (ROW_WORDS // LANES, LANES)
ROW_DTYPE = jnp.uint32
HALO = 16
RET_CHUNK = 256
MIXER_SEQS = 2
MOE_BLOCK = 512
ROUTER_W = LANES
RANK_GROUP = 128
VMEM_LIMIT = 56 * 1024 * 1024

_BF = jnp.bfloat16
_F32 = jnp.float32


def _rms(x, g):
    inv = lax.rsqrt(jnp.mean(x * x, axis=-1, keepdims=True) + EPS)
    return x * inv * g


def _dot(a, b):
    return jnp.dot(a, b, preferred_element_type=_F32)


def _round_bf16(x):
    return x.astype(_BF).astype(_F32)


def _pack_rows(x):
    lo = lax.bitcast_convert_type(x[:, :ROW_WORDS].astype(_BF).astype(_F32), jnp.uint32)
    hi = lax.bitcast_convert_type(x[:, ROW_WORDS:].astype(_BF).astype(_F32), jnp.uint32)
    return ((lo >> 16) | hi).reshape((x.shape[0],) + ROW_TILE)


def _unpack_rows(w):
    w = w.reshape(w.shape[0], ROW_WORDS)
    lo = lax.bitcast_convert_type(w << 16, _F32)
    hi = lax.bitcast_convert_type(w & jnp.uint32(0xFFFF0000), _F32)
    return jnp.concatenate([lo, hi], axis=1)


def _rotary(x, cos2, sin2):
    return x * cos2 + pltpu.roll(x, RET_DK // 2, 1) * sin2


def _route(logits):
    lane = lax.broadcasted_iota(jnp.int32, logits.shape, 1).astype(_F32)
    neg = jnp.float32(-jnp.inf)
    big = jnp.float32(1 << 20)
    lg = jnp.where(lane < N_GROUPS, logits, neg)
    mg = jnp.max(lg, axis=-1, keepdims=True)
    g_idx = jnp.min(jnp.where(lg == mg, lane, big), axis=-1, keepdims=True)
    p_g = 1.0 / jnp.sum(jnp.exp(lg - mg), axis=-1, keepdims=True)
    lo = N_GROUPS + g_idx * EXPERTS_PER_GROUP
    in_grp = (lane >= lo) & (lane < lo + EXPERTS_PER_GROUP)
    le = jnp.where(in_grp, logits, neg)
    m1 = jnp.max(le, axis=-1, keepdims=True)
    i1 = jnp.min(jnp.where(le == m1, lane, big), axis=-1, keepdims=True)
    le2 = jnp.where(lane == i1, neg, le)
    m2 = jnp.max(le2, axis=-1, keepdims=True)
    i2 = jnp.min(jnp.where(le2 == m2, lane, big), axis=-1, keepdims=True)
    t = jnp.exp(m2 - m1)
    den = 1.0 + t
    e0 = (i1 - N_GROUPS).astype(jnp.int32)
    e1 = (i2 - N_GROUPS).astype(jnp.int32)
    return e0, e1, p_g * (1.0 / den), p_g * (t / den)


def _post_mix(x, mixed_ref, w_out_ref, g_ffn_ref, w_rt_ref, x1_ref, rw_ref):
    x1 = x + _dot(mixed_ref[...], w_out_ref[...])
    x1_ref[...] = x1.reshape(x1_ref.shape)
    h2 = _rms(x1, g_ffn_ref[...])
    e0, e1, w0, w1 = _route(_dot(h2.astype(_BF), w_rt_ref[...]))
    lane = lax.broadcasted_iota(jnp.int32, (x.shape[0], LANES), 1)
    rw_ref[...] = jnp.where(lane == 0, w0, jnp.where(lane == 1, w1, 0.0)).reshape(rw_ref.shape)
    return h2, e0, e1


def _route_block(x1, g_ffn_ref, w_rt_ref, carry_scr, cap, valid, h2_ref, rw_ref, dst_t_ref):
    for _ in _route_stages(x1, g_ffn_ref, w_rt_ref, carry_scr, cap, valid, h2_ref, rw_ref, dst_t_ref):
        pass


def _route_stages(x1, g_ffn_ref, w_rt_ref, carry_scr, cap, valid, h2_ref, rw_ref, dst_t_ref):
    h2 = _rms(x1, g_ffn_ref[...])
    logits = _dot(h2.astype(_BF), w_rt_ref[...])
    h2_ref[...] = _pack_rows(h2).reshape(h2_ref.shape)
    yield
    e0, e1, w0, w1 = _route(logits)
    lane = lax.broadcasted_iota(jnp.int32, (x1.shape[0], LANES), 1)
    rw_ref[...] = jnp.where(lane == 0, w0, jnp.where(lane == 1, w1, 0.0)).reshape(rw_ref.shape)
    yield
    dst = _rank_rows(e0, e1, cap, carry_scr, valid)
    dst_t = dst.T[0:SUBLANES, :].astype(jnp.int32)
    n_parts, _, width = dst_t_ref.shape
    for part in range(n_parts):
        dst_t_ref[part] = dst_t[:, part * width:(part + 1) * width]
    yield


def _rank_rows(e0, e1, cap, carry_scr, valid=None):
    R = e0.shape[0]
    lane = lax.broadcasted_iota(jnp.int32, (R, LANES), 1)
    m0 = lane == e0
    m1 = lane == e1
    onehot = jnp.where(m0 | m1, 1.0, 0.0)
    G = min(RANK_GROUP, R)
    r_i = lax.broadcasted_iota(jnp.int32, (G, G), 0)
    c_i = lax.broadcasted_iota(jnp.int32, (G, G), 1)
    tri = jnp.where(c_i < r_i, 1.0, 0.0).astype(_BF)
    running = carry_scr[...]
    parts = []
    for g in range(R // G):
        grp = onehot[g * G:(g + 1) * G]
        parts.append(_dot(tri, grp.astype(_BF)) + running)
        running = running + jnp.sum(grp, axis=0, keepdims=True)
    before = jnp.concatenate(parts, axis=0)
    d0 = e0.astype(_F32) * cap + jnp.sum(jnp.where(m0, before, 0.0), axis=-1, keepdims=True)
    d1 = e1.astype(_F32) * cap + jnp.sum(jnp.where(m1, before, 0.0), axis=-1, keepdims=True)
    added = running - carry_scr[...]
    carry_scr[...] += added if valid is None else added * valid
    return jnp.where(lane == 0, d0, jnp.where(lane == 1, d1, 0.0))


def _group_norm(o, gain):
    mu = jnp.mean(o, axis=-1, keepdims=True)
    d = o - mu
    var = jnp.mean(d * d, axis=-1, keepdims=True)
    return d * lax.rsqrt(var + EPS) * gain


def _mixer_prompt_kernel(x_ref, g_mix_ref, w_in_ref, w_pool_ref, pscale_ref, gn_ref, w_out_ref, g_ffn_ref,
                         w_rt_ref, cos_ref, sin_ref, dmask_ref, xi_ref, zeta_ref, gc_ref,
                         x1_ref, dst_t_ref, rw_ref, npool_ref, nret_ref, cnt_ref, h2_ref,
                         u_scr, s_scr, mixed_scr, carry_scr, z_scr, x1_prev_scr, *, cap):
    c = pl.program_id(1)
    n_c = pl.num_programs(1)
    step = pl.program_id(0) * n_c + c
    NSEQ, C, _ = x_ref.shape
    R = NSEQ * C

    @pl.when(step == 0)
    def _():
        carry_scr[...] = jnp.zeros(carry_scr.shape, _F32)
        x1_prev_scr[...] = jnp.zeros(x1_prev_scr.shape, _F32)

    @pl.when(c == 0)
    def _():
        u_scr[:, 0:HALO, :] = jnp.zeros((NSEQ, HALO, POOL_IN), _F32)
        s_scr[...] = jnp.zeros(s_scr.shape, _F32)

    x = x_ref[...].reshape(R, D_MODEL)
    h = _rms(x, g_mix_ref[...]).astype(_BF)

    stages = _route_stages(x1_prev_scr[...], g_ffn_ref, w_rt_ref, carry_scr, cap, (step > 0).astype(_F32),
                           h2_ref, rw_ref, dst_t_ref)
    for lo, hi in ((0, OFF_V), (OFF_V, OFF_GA), (OFF_GA, OFF_GB), (OFF_GB, IN_WIDTH)):
        z_scr[:, lo:hi] = _dot(h, w_in_ref[:, lo:hi])
        next(stages, None)
    cnt_ref[...] = carry_scr[...]

    u_scr[:, HALO:HALO + C, :] = z_scr[:, OFF_U:OFF_U + POOL_IN].reshape(NSEQ, C, POOL_IN)
    q = z_scr[:, OFF_Q:OFF_Q + QK_W]
    k = z_scr[:, OFF_K:OFF_K + QK_W]
    cos2 = jnp.concatenate([cos_ref[...]] * NSEQ, axis=0)
    sin2 = jnp.concatenate([sin_ref[...]] * NSEQ, axis=0)
    pos1 = (c * C + 1 + lax.broadcasted_iota(jnp.int32, (C, POOL_GC), 0)).astype(_F32)

    for j in range(RET_HEADS):
        win = POOL_WINDOWS[j]
        cs = slice(j * POOL_GC, (j + 1) * POOL_GC)
        n_rows = jnp.minimum(pos1, jnp.float32(win))
        pooled = []
        for sq in range(NSEQ):
            u_j = u_scr[sq, HALO:HALO + C, cs]
            s = u_j
            for d in range(1, win):
                s = s + u_scr[sq, HALO - d:HALO - d + C, cs]
            pooled.append(s / n_rows - u_j)
        osl = slice(j * POOL_OUT_GC, (j + 1) * POOL_OUT_GC)
        pool_out = _dot(jnp.concatenate(pooled, axis=0).astype(_BF), w_pool_ref[j]) * pscale_ref[:, osl]

        qs = slice(j * RET_DK, (j + 1) * RET_DK)
        qb = _rotary(q[:, qs], cos2, sin2).astype(_BF)
        kf = _rotary(k[:, qs], cos2, sin2) * (RET_DK ** -0.5)
        kb = kf.astype(_BF)
        vb = z_scr[:, OFF_V + j * RET_DV:OFF_V + (j + 1) * RET_DV].astype(_BF)
        ret = []
        for sq in range(NSEQ):
            rows = slice(sq * C, (sq + 1) * C)
            scores = lax.dot_general(qb[rows], kb[rows], (((1,), (1,)), ((), ())),
                                     preferred_element_type=_F32) * dmask_ref[j]
            s_old = s_scr[sq, j]
            o = _dot(scores.astype(_BF), vb[rows]) + _dot(qb[rows], s_old.astype(_BF)) * xi_ref[j]
            kz = (kf[rows] * zeta_ref[j]).astype(_BF)
            s_scr[sq, j] = gc_ref[j] * s_old + lax.dot_general(kz, vb[rows], (((0,), (0,)), ((), ())),
                                                               preferred_element_type=_F32)
            ret.append(_group_norm(o, gn_ref[:, osl]))
        ret_out = jnp.concatenate(ret, axis=0)

        ga = z_scr[:, OFF_GA + j * RET_DV:OFF_GA + (j + 1) * RET_DV]
        gb = z_scr[:, OFF_GB + j * RET_DV:OFF_GB + (j + 1) * RET_DV]
        mixed_scr[:, osl] = (jax.nn.sigmoid(ga) * pool_out + jax.nn.sigmoid(gb) * ret_out).astype(_BF)

    x1 = x + _dot(mixed_scr[...], w_out_ref[...])
    x1_ref[...] = x1.reshape(x1_ref.shape)
    x1_prev_scr[...] = x1

    u_scr[:, 0:HALO, :] = u_scr[:, C:C + HALO, :]

    @pl.when(c == n_c - 1)
    def _():
        npool_ref[...] = u_scr[:, HALO + C - POOL_HIST:HALO + C, :]
        nret_ref[...] = s_scr[...]


def _decay_tables(C):
    f32 = np.float32
    log_g = np.log(f32(1.0) - np.exp2(f32(-5.0) - np.arange(RET_HEADS, dtype=f32)))
    i = np.arange(C, dtype=f32)
    diff = i[:, None] - i[None, :]
    dmask = np.where(diff >= 0, np.exp(np.maximum(diff, f32(0.0))[None] * log_g[:, None, None]), f32(0.0))
    xi = np.exp((i[None, :] + f32(1.0)) * log_g[:, None])
    zeta = np.exp((f32(C) - f32(1.0) - i)[None, :] * log_g[:, None])
    g_chunk = np.exp(f32(C) * log_g)
    return dmask.astype(f32), xi.astype(f32), zeta.astype(f32), g_chunk.astype(f32)


def _rope_tables(pos):
    half = RET_DK // 2
    freqs = ROPE_BASE ** (-jnp.arange(half, dtype=_F32) / half)
    ang = pos[:, None] * freqs[None, :]
    cos, sin = jnp.cos(ang), jnp.sin(ang)
    return jnp.concatenate([cos, cos], axis=-1), jnp.concatenate([-sin, sin], axis=-1)


def _const(a, shape=None):
    if shape is not None:
        a = np.ascontiguousarray(np.broadcast_to(a, shape))
    return jnp.asarray(a)


def _full(shape):
    n = len(shape)
    return pl.BlockSpec(shape, lambda *_: (0,) * n)


def _route_tail_kernel(x1_ref, g_ffn_ref, w_rt_ref, cnt_in_ref, dst_t_in, rw_in, h2_in,
                       dst_t_ref, rw_ref, cnt_ref, h2_ref, carry_scr, *, cap):
    del dst_t_in, rw_in, h2_in
    carry_scr[...] = cnt_in_ref[...]
    x1 = x1_ref[...]
    _route_block(x1.reshape(x1.shape[0] * x1.shape[1], D_MODEL), g_ffn_ref, w_rt_ref, carry_scr, cap, None,
                 h2_ref, rw_ref, dst_t_ref)
    cnt_ref[...] = carry_scr[...]


def _route_tail(x1, dst_t, rw, counts, h2, wts, cap, C, last):
    NSEQ = x1.shape[0]
    any_spec = pl.BlockSpec(memory_space=pl.ANY)
    return pl.pallas_call(
        functools.partial(_route_tail_kernel, cap=cap),
        grid=(1,),
        in_specs=[pl.BlockSpec((NSEQ, C, D_MODEL), lambda i: (0, last, 0)), _full((1, D_MODEL)),
                  _full((D_MODEL, ROUTER_W)), _full((1, LANES)), any_spec, any_spec, any_spec],
        out_specs=(pl.BlockSpec((NSEQ, SUBLANES, C), lambda i: (0, 0, last)),
                   pl.BlockSpec((NSEQ, C, LANES), lambda i: (0, last, 0)),
                   _full((1, LANES)),
                   pl.BlockSpec((NSEQ, C) + ROW_TILE, lambda i: (0, last, 0, 0))),
        out_shape=(jax.ShapeDtypeStruct(dst_t.shape, dst_t.dtype), jax.ShapeDtypeStruct(rw.shape, rw.dtype),
                   jax.ShapeDtypeStruct((1, LANES), _F32), jax.ShapeDtypeStruct(h2.shape, h2.dtype)),
        input_output_aliases={4: 0, 5: 1, 6: 3},
        scratch_shapes=[pltpu.VMEM((1, LANES), _F32)],
        compiler_params=pltpu.CompilerParams(dimension_semantics=("arbitrary",), vmem_limit_bytes=VMEM_LIMIT),
        name="mixer_route_tail",
    )(x1, wts["g_ffn"], wts["w_rt"], counts, dst_t, rw, h2)


def _mixer_prompt(x, wts, cap):
    B, L, _ = x.shape
    C = RET_CHUNK if L % RET_CHUNK == 0 else L
    n_c = L // C
    T = B * L
    dmask, xi, zeta, g_chunk = _decay_tables(C)
    dmask = _const(dmask)
    xi_b = _const(xi[:, :, None], (RET_HEADS, C, RET_DV))
    zeta_b = _const(zeta[:, :, None], (RET_HEADS, C, RET_DK))
    gc_b = _const(g_chunk[:, None, None], (RET_HEADS, 1, RET_DV))
    cos2, sin2 = _rope_tables(jnp.arange(L).astype(_F32))

    NSEQ = MIXER_SEQS if B % MIXER_SEQS == 0 else 1
    B2 = B // NSEQ
    T2 = T // NSEQ
    x = x.reshape(NSEQ, B2, L, D_MODEL)
    tok = lambda b, c: (0, b * n_c + c, 0)
    in_specs = [
        pl.BlockSpec((NSEQ, None, C, D_MODEL), lambda b, c: (0, b, c, 0)),
        _full((1, D_MODEL)), _full((D_MODEL, IN_WIDTH)), _full((POOL_GROUPS, POOL_GC, POOL_OUT_GC)),
        _full((1, D_MODEL)), _full((1, D_MODEL)), _full((D_MODEL, D_MODEL)), _full((1, D_MODEL)),
        _full((D_MODEL, ROUTER_W)),
        pl.BlockSpec((C, RET_DK), lambda b, c: (c, 0)), pl.BlockSpec((C, RET_DK), lambda b, c: (c, 0)),
        _full((RET_HEADS, C, C)), _full((RET_HEADS, C, RET_DV)), _full((RET_HEADS, C, RET_DK)),
        _full((RET_HEADS, 1, RET_DV)),
    ]
    R = NSEQ * C
    out_shape = (
        jax.ShapeDtypeStruct((NSEQ, T2, D_MODEL), _F32),
        jax.ShapeDtypeStruct((NSEQ, SUBLANES, T2), jnp.int32),
        jax.ShapeDtypeStruct((NSEQ, T2, LANES), _F32),
        jax.ShapeDtypeStruct((NSEQ, B2, POOL_HIST, POOL_IN), _F32),
        jax.ShapeDtypeStruct((NSEQ, B2, RET_HEADS, RET_DK, RET_DV), _F32),
        jax.ShapeDtypeStruct((1, LANES), _F32),
        jax.ShapeDtypeStruct((NSEQ, T2) + ROW_TILE, ROW_DTYPE),
    )
    prev = lambda b, c: jnp.maximum(b * n_c + c - 1, 0)
    out_specs = (
        pl.BlockSpec((NSEQ, C, D_MODEL), tok),
        pl.BlockSpec((NSEQ, SUBLANES, C), lambda b, c: (0, 0, prev(b, c))),
        pl.BlockSpec((NSEQ, C, LANES), lambda b, c: (0, prev(b, c), 0)),
        pl.BlockSpec((NSEQ, None, POOL_HIST, POOL_IN), lambda b, c: (0, b, 0, 0)),
        pl.BlockSpec((NSEQ, None, RET_HEADS, RET_DK, RET_DV), lambda b, c: (0, b, 0, 0, 0)),
        _full((1, LANES)),
        pl.BlockSpec((NSEQ, C) + ROW_TILE, lambda b, c: (0, prev(b, c), 0, 0)),
    )
    x1, dst_t, rw, npool, nret, counts, h2 = pl.pallas_call(
        functools.partial(_mixer_prompt_kernel, cap=cap),
        grid=(B2, n_c),
        in_specs=in_specs,
        out_specs=out_specs,
        out_shape=out_shape,
        scratch_shapes=[pltpu.VMEM((NSEQ, HALO + C, POOL_IN), _F32),
                        pltpu.VMEM((NSEQ, RET_HEADS, RET_DK, RET_DV), _F32),
                        pltpu.VMEM((R, D_MODEL), _BF),
                        pltpu.VMEM((1, LANES), _F32),
                        pltpu.VMEM((R, IN_WIDTH), _F32),
                        pltpu.VMEM((R, D_MODEL), _F32)],
        compiler_params=pltpu.CompilerParams(dimension_semantics=("arbitrary", "arbitrary"),
                                             vmem_limit_bytes=VMEM_LIMIT),
        name="mixer_prompt",
    )(x, wts["g_mix"], wts["w_in"], wts["w_pool"], wts["pool_scale"], wts["ret_gn"], wts["w_out"],
      wts["g_ffn"], wts["w_rt"], cos2, sin2, dmask, xi_b, zeta_b, gc_b)
    dst_t, rw, counts, h2 = _route_tail(x1, dst_t, rw, counts, h2, wts, cap, C, B2 * n_c - 1)
    dst_rows = jnp.moveaxis(dst_t[:, 0:2, :], 1, 0).reshape(2, T)
    return (x1.reshape(T, D_MODEL), dst_rows, rw.reshape(T, LANES),
            npool.reshape(B, POOL_HIST, POOL_IN), nret.reshape(B, RET_HEADS, RET_DK, RET_DV), counts,
            h2.reshape((T,) + ROW_TILE))


def _prep_weights(g_mix, w_pool, pool_scale, ret_gn, w_out, g_ffn, w_grp, w_exp):
    w_rt = jnp.concatenate([w_grp, w_exp.reshape(D_MODEL, N_EXPERTS)], axis=1)
    w_rt = jnp.pad(w_rt, ((0, 0), (0, ROUTER_W - w_rt.shape[1])))
    row = lambda v: v.reshape(1, D_MODEL)
    return dict(g_mix=row(g_mix), w_pool=w_pool.astype(_BF), pool_scale=row(pool_scale),
                ret_gn=row(ret_gn), w_out=w_out.astype(_BF), g_ffn=row(g_ffn), w_rt=w_rt.astype(_BF))


PROJ_COLS = 512


def _sample_proj_kernel(x_ref, g_mix_ref, w_in_ref, cos_ref, sin_ref, zeta_ref,
                        w_bf_ref, u_ref, q_ref, k_ref, kq_ref, v_ref, ga_ref, gb_ref, h_scr, z_scr):
    c = pl.program_id(0)

    @pl.when(c == 0)
    def _():
        h_scr[...] = _rms(x_ref[...], g_mix_ref[...]).astype(_BF)

    w = w_in_ref[...].astype(_BF)
    w_bf_ref[...] = w
    z_scr[c] = _dot(h_scr[...], w)

    @pl.when(c == pl.num_programs(0) - 1)
    def _():
        def cols(off, width):
            parts = [z_scr[i] for i in range(off // PROJ_COLS, (off + width) // PROJ_COLS)]
            return parts[0] if len(parts) == 1 else jnp.concatenate(parts, axis=1)

        u_ref[...] = cols(OFF_U, POOL_IN)
        q = cols(OFF_Q, QK_W)
        k = cols(OFF_K, QK_W)
        for j in range(RET_HEADS):
            qs = slice(j * RET_DK, (j + 1) * RET_DK)
            qf = _rotary(q[:, qs], cos_ref[...], sin_ref[...])
            kf = _rotary(k[:, qs], cos_ref[...], sin_ref[...]) * (RET_DK ** -0.5)
            q_ref[:, qs] = qf
            k_ref[:, qs] = kf
            kq_ref[0, :, qs] = kf * zeta_ref[j]
            kq_ref[1, :, qs] = _round_bf16(qf)
        v_ref[...] = cols(OFF_V, V_W)
        ga_ref[...] = cols(OFF_GA, D_MODEL)
        gb_ref[...] = cols(OFF_GB, D_MODEL)


def _sample_proj(x, g_mix, w_in, pos0):
    Bs = x.shape[0]
    offsets = (OFF_U, OFF_Q, OFF_K, OFF_V, OFF_GA, OFF_GB, IN_WIDTH)
    assert all(o % PROJ_COLS == 0 for o in offsets)
    n_c = IN_WIDTH // PROJ_COLS
    _, _, zeta, _ = _decay_tables(1)
    cos2, sin2 = _rope_tables((pos0 + jnp.arange(1)).astype(_F32))
    shapes = ((Bs, POOL_IN), (Bs, QK_W), (Bs, QK_W), (2, Bs, QK_W), (Bs, V_W), (Bs, D_MODEL), (Bs, D_MODEL))
    return pl.pallas_call(
        _sample_proj_kernel,
        grid=(n_c,),
        in_specs=[_full((Bs, D_MODEL)), _full((1, D_MODEL)), pl.BlockSpec((D_MODEL, PROJ_COLS), lambda c: (0, c)),
                  _full((1, RET_DK)), _full((1, RET_DK)), _full((RET_HEADS, 1, 1))],
        out_specs=(pl.BlockSpec((D_MODEL, PROJ_COLS), lambda c: (0, c)),) + tuple(_full(sh) for sh in shapes),
        out_shape=(jax.ShapeDtypeStruct((D_MODEL, IN_WIDTH), _BF),)
        + tuple(jax.ShapeDtypeStruct(sh, _F32) for sh in shapes),
        scratch_shapes=[pltpu.VMEM((Bs, D_MODEL), _BF), pltpu.VMEM((n_c, Bs, PROJ_COLS), _F32)],
        compiler_params=pltpu.CompilerParams(dimension_semantics=("arbitrary",), vmem_limit_bytes=VMEM_LIMIT),
        name="sample_proj",
    )(x, g_mix, w_in, cos2, sin2, _const(zeta[:, :, None], (RET_HEADS, 1, 1)))


def _mixer_sample_kernel(x_ref, u_ref, q_ref, k_ref, v_ref, ga_ref, gb_ref, spool_ref, qs_ref,
                         w_pool_ref, pscale_ref, gn_ref, w_out_ref, g_ffn_ref, w_rt_ref, dm_ref, xi_ref,
                         carry_in_ref,
                         x1_ref, dst_ref, rw_ref, npool_ref, cnt_ref, h2_ref,
                         mixed_scr, carry_scr, *, cap):
    u = u_ref[...]
    npool_ref[0:POOL_HIST - 1] = spool_ref[1:POOL_HIST]
    npool_ref[POOL_HIST - 1] = u
    for j in range(RET_HEADS):
        cs = slice(j * POOL_GC, (j + 1) * POOL_GC)
        qs = slice(j * RET_DK, (j + 1) * RET_DK)
        osl = slice(j * POOL_OUT_GC, (j + 1) * POOL_OUT_GC)
        win = POOL_WINDOWS[j]
        s = u[:, cs]
        for r in range(POOL_HIST - (win - 1), POOL_HIST):
            s = s + spool_ref[r, :, cs]
        pooled = s / jnp.float32(win) - u[:, cs]
        pool_out = _dot(pooled.astype(_BF), w_pool_ref[j]) * pscale_ref[:, osl]
        score = jnp.sum(q_ref[:, qs] * k_ref[:, qs], axis=-1, keepdims=True) * dm_ref[j]
        q_state = jnp.concatenate([qs_ref[hf, pl.ds(j, u.shape[0], stride=RET_HEADS), :]
                                   for hf in range(qs_ref.shape[0])], axis=1)
        o = score * v_ref[:, osl] + q_state * xi_ref[j]
        ret_out = _group_norm(o, gn_ref[:, osl])
        mixed_scr[:, osl] = (jax.nn.sigmoid(ga_ref[:, osl]) * pool_out
                             + jax.nn.sigmoid(gb_ref[:, osl]) * ret_out).astype(_BF)
    h2, e0, e1 = _post_mix(x_ref[...], mixed_scr, w_out_ref, g_ffn_ref, w_rt_ref, x1_ref, rw_ref)
    h2_ref[...] = _pack_rows(h2).reshape(h2_ref.shape)
    carry_scr[...] = carry_in_ref[...]
    dst_ref[...] = _rank_rows(e0, e1, cap, carry_scr).astype(jnp.int32)
    cnt_ref[...] = carry_scr[...]


def _mixer_sample(x, proj, state_pool, q_state, wts, carry_in, cap):
    Bs = x.shape[0]
    assert Bs == LANES and POOL_GROUPS == RET_HEADS
    u, q, k, v, ga, gb = proj
    dmask, xi, _, _ = _decay_tables(1)
    dm_b = _const(dmask, (RET_HEADS, 1, 1))
    xi_b = _const(xi[:, :, None], (RET_HEADS, 1, RET_DV))

    in_specs = [
        _full((Bs, D_MODEL)),
        _full((Bs, POOL_IN)), _full((Bs, QK_W)), _full((Bs, QK_W)), _full((Bs, V_W)),
        _full((Bs, D_MODEL)), _full((Bs, D_MODEL)),
        _full((POOL_HIST, Bs, POOL_IN)), _full(q_state.shape),
        _full((POOL_GROUPS, POOL_GC, POOL_OUT_GC)),
        _full((1, D_MODEL)), _full((1, D_MODEL)), _full((D_MODEL, D_MODEL)), _full((1, D_MODEL)),
        _full((D_MODEL, ROUTER_W)),
        _full((RET_HEADS, 1, 1)), _full((RET_HEADS, 1, RET_DV)),
        _full((1, LANES)),
    ]
    out_shape = (
        jax.ShapeDtypeStruct((Bs, D_MODEL), _F32),
        jax.ShapeDtypeStruct((Bs, LANES), jnp.int32),
        jax.ShapeDtypeStruct((Bs, LANES), _F32),
        jax.ShapeDtypeStruct((POOL_HIST, Bs, POOL_IN), _F32),
        jax.ShapeDtypeStruct((1, LANES), _F32),
        jax.ShapeDtypeStruct((Bs,) + ROW_TILE, ROW_DTYPE),
    )
    out_specs = (
        _full((Bs, D_MODEL)), _full((Bs, LANES)), _full((Bs, LANES)), _full((POOL_HIST, Bs, POOL_IN)),
        _full((1, LANES)), _full((Bs,) + ROW_TILE),
    )
    return pl.pallas_call(
        functools.partial(_mixer_sample_kernel, cap=cap),
        grid=(1,),
        in_specs=in_specs,
        out_specs=out_specs,
        out_shape=out_shape,
        scratch_shapes=[pltpu.VMEM((Bs, D_MODEL), _BF), pltpu.VMEM((1, LANES), _F32)],
        compiler_params=pltpu.CompilerParams(dimension_semantics=("arbitrary",), vmem_limit_bytes=VMEM_LIMIT),
        name="mixer_sample",
    )(x, u, q, k, v, ga, gb, state_pool, q_state, wts["w_pool"], wts["pool_scale"], wts["ret_gn"],
      wts["w_out"], wts["g_ffn"], wts["w_rt"], dm_b, xi_b, carry_in)


SC_CORES = 2
SC_SUBCORES = 16
SC_LANES = 16
STATE_ROWS = 32
STATE_UNROLL = 8
STATE_COLS = 8


def _sc_round_bf16(x):
    b = lax.bitcast_convert_type(x, jnp.uint32)
    r = b + jnp.uint32(0x7FFF) + ((b >> 16) & jnp.uint32(1))
    return lax.bitcast_convert_type(r & jnp.uint32(0xFFFF0000), _F32)


def _sc_state_update(s0, kq, v):
    P = s0.shape[0]
    n_workers = SC_CORES * SC_SUBCORES
    R = STATE_ROWS
    n_parts = RET_DK // R
    n_vc = RET_DV // SC_LANES
    assert P % (n_workers * RET_HEADS) == 0 and RET_DK % R == 0 and n_parts % 2 == 0 and n_parts >= 2
    assert R % STATE_UNROLL == 0 and n_vc % STATE_COLS == 0
    per_w = P // n_workers
    seqs_w = per_w // RET_HEADS
    _, _, _, g_chunk = _decay_tables(1)
    decay = _const(g_chunk[:, None], (RET_HEADS, SC_LANES))
    mesh = plsc.VectorSubcoreMesh(core_axis_name="c", subcore_axis_name="s")

    def body(s0_hbm, kq_hbm, v_hbm, g_hbm, out_hbm, qs_hbm, in_v, out_v, k_v, q_v, v_v, g_v, o_v, sem_in, sem_out):
        wid = lax.axis_index("s") * SC_CORES + lax.axis_index("c")
        base = wid * per_w
        seqs = pl.ds(wid * seqs_w, seqs_w)
        pltpu.sync_copy(kq_hbm.at[0, seqs], k_v)
        pltpu.sync_copy(kq_hbm.at[1, seqs], q_v)
        pltpu.sync_copy(v_hbm.at[seqs], v_v)
        pltpu.sync_copy(g_hbm, g_v)

        def load(p, part, slot):
            return pltpu.make_async_copy(s0_hbm.at[p, pl.ds(part * R, R)], in_v.at[slot], sem_in.at[slot])

        def store(p, part, slot):
            return pltpu.make_async_copy(out_v.at[slot], out_hbm.at[p, pl.ds(part * R, R)], sem_out.at[slot])

        def o_at(pp, c):
            return (c * SC_LANES // LANES, pp, pl.ds(c * SC_LANES % LANES, SC_LANES))

        load(base, 0, 0).start()

        def pair(pp, carry):
            p = base + pp
            seq, head = pp // RET_HEADS, pp % RET_HEADS
            g = g_v[head, :]
            sidx = jnp.full((SC_LANES,), seq, jnp.int32)
            for c in range(n_vc):
                o_v[o_at(pp, c)] = jnp.zeros((SC_LANES,), _F32)
            for part in range(n_parts):
                slot = part % 2
                load(p, part, slot).wait()
                if part + 1 < n_parts:
                    load(p, part + 1, 1 - slot).start()
                else:
                    @pl.when(pp + 1 < per_w)
                    def _():
                        load(p + 1, 0, 1 - slot).start()

                if part < 2:
                    @pl.when(pp > 0)
                    def _():
                        store(p, part, slot).wait()
                else:
                    store(p, part, slot).wait()

                def rows(i8, c2):
                    for c0 in range(0, n_vc, STATE_COLS):
                        cols = [pl.ds((c0 + c) * SC_LANES, SC_LANES) for c in range(STATE_COLS)]
                        vs = [v_v[seq, pl.ds(head * RET_DV + (c0 + c) * SC_LANES, SC_LANES)]
                              for c in range(STATE_COLS)]
                        acc = [None] * STATE_COLS
                        for j in range(STATE_UNROLL):
                            i = i8 * STATE_UNROLL + j
                            ridx = jnp.full((SC_LANES,), head * RET_DK + part * R + i, jnp.int32)
                            ki = plsc.load_gather(k_v, [sidx, ridx])
                            qi = plsc.load_gather(q_v, [sidx, ridx])
                            ss = [in_v[slot, i, cs] for cs in cols]
                            for c, cs in enumerate(cols):
                                out_v[slot, i, cs] = g * ss[c] + ki * vs[c]
                            for c in range(STATE_COLS):
                                t = qi * _sc_round_bf16(ss[c])
                                acc[c] = t if acc[c] is None else acc[c] + t
                        for c in range(STATE_COLS):
                            o_v[o_at(pp, c0 + c)] = o_v[o_at(pp, c0 + c)] + acc[c]
                    return c2

                lax.fori_loop(0, R // STATE_UNROLL, rows, 0)
                store(p, part, slot).start()
            return carry

        lax.fori_loop(0, per_w, pair, 0)
        for slot in range(2):
            store(base, slot, slot).wait()
        for hf in range(RET_DV // LANES):
            pltpu.sync_copy(o_v.at[hf], qs_hbm.at[hf, pl.ds(base, per_w)])

    keys = pltpu.VMEM((seqs_w, RET_HEADS * RET_DK), _F32)
    return pl.kernel(
        body, mesh=mesh,
        out_type=(jax.ShapeDtypeStruct(s0.shape, _F32), jax.ShapeDtypeStruct((RET_DV // LANES, P, LANES), _F32)),
        scratch_types=[pltpu.VMEM((2, R, RET_DV), _F32), pltpu.VMEM((2, R, RET_DV), _F32), keys, keys,
                       pltpu.VMEM((seqs_w, RET_HEADS * RET_DV), _F32), pltpu.VMEM((RET_HEADS, SC_LANES), _F32),
                       pltpu.VMEM((RET_DV // LANES, per_w, LANES), _F32),
                       pltpu.SemaphoreType.DMA((2,)), pltpu.SemaphoreType.DMA((2,))],
        compiler_params=dataclasses.replace(pltpu.CompilerParams(), needs_layout_passes=False),
        name="sample_state_update",
    )(s0, kq, v, decay)


IDX_CHUNK = 1024
ISSUE_UNROLL = 8
SC_WINDOW = 64


def _sc_scatter(h2, dst_rows, n_rows):
    T = h2.shape[0]
    n_workers = SC_CORES * SC_SUBCORES
    W = SC_WINDOW
    assert T % (n_workers * W) == 0
    per_w = T // n_workers
    n_win = per_w // W
    d0 = dst_rows[0].reshape(T // W, W)
    d1 = dst_rows[1].reshape(T // W, W)
    mesh = plsc.VectorSubcoreMesh(core_axis_name="c", subcore_axis_name="s")

    def body(h2_hbm, d0_hbm, d1_hbm, xs_hbm, i0_v, i1_v, rows_v, sem_load, sem_store):
        wid = lax.axis_index("s") * SC_CORES + lax.axis_index("c")
        base = wid * per_w
        pltpu.sync_copy(d0_hbm.at[pl.ds(wid * n_win, n_win)], i0_v)
        pltpu.sync_copy(d1_hbm.at[pl.ds(wid * n_win, n_win)], i1_v)
        pltpu.async_copy(h2_hbm.at[pl.ds(base, W)], rows_v.at[0], sem_load)
        for i in range(n_win):
            b = i % 2
            pltpu.make_async_copy(h2_hbm.at[pl.ds(base, W)], rows_v.at[b], sem_load).wait()
            if i + 1 < n_win:
                pltpu.async_copy(h2_hbm.at[pl.ds(base + (i + 1) * W, W)], rows_v.at[1 - b], sem_load)
            c0 = pltpu.async_copy(rows_v.at[b], xs_hbm.at[i0_v.at[i]], sem_store)
            c1 = pltpu.async_copy(rows_v.at[b], xs_hbm.at[i1_v.at[i]], sem_store)
            c0.wait()
            c1.wait()

    return pl.kernel(
        body, mesh=mesh,
        out_type=jax.ShapeDtypeStruct((n_rows,) + ROW_TILE, ROW_DTYPE),
        scratch_types=[pltpu.VMEM((n_win, W), jnp.int32), pltpu.VMEM((n_win, W), jnp.int32),
                       pltpu.VMEM((2, W) + ROW_TILE, ROW_DTYPE),
                       pltpu.SemaphoreType.DMA, pltpu.SemaphoreType.DMA],
        name="moe_sc_scatter",
    )(h2, d0, d1)


ZERO_CHUNK = 64


def _finish_dispatch_kernel(zstart_ref, zchunks_ref, slots_hbm, h2_ref, xs_in_hbm, xs_hbm, idx_smem, zbuf, sem_idx,
                            sem_rows, sem_zero):
    del xs_in_hbm
    TS = h2_ref.shape[0]
    idx_cp = pltpu.make_async_copy(slots_hbm.at[pl.ds(0, IDX_CHUNK)], idx_smem, sem_idx)
    idx_cp.start()
    zbuf[...] = jnp.zeros(zbuf.shape, zbuf.dtype)

    def zero_copy(e, j):
        return pltpu.make_async_copy(zbuf, xs_hbm.at[pl.ds(zstart_ref[e] + j * ZERO_CHUNK, ZERO_CHUNK)], sem_zero)

    for e in range(N_EXPERTS):
        lax.fori_loop(0, zchunks_ref[e], lambda j, c, e=e: (zero_copy(e, j).start(), c)[1], 0)
    idx_cp.wait()

    def issue(r, carry):
        for kk in range(2):
            pltpu.make_async_copy(h2_ref.at[r], xs_hbm.at[idx_smem[2 * r + kk]], sem_rows).start(priority=kk)
        return carry

    lax.fori_loop(0, TS, issue, 0, unroll=ISSUE_UNROLL)
    for e in range(N_EXPERTS):
        lax.fori_loop(0, zchunks_ref[e], lambda j, c, e=e: (zero_copy(e, j).wait(), c)[1], 0)
    for kk in range(2):
        pltpu.make_async_copy(h2_ref, xs_hbm.at[pl.ds(0, TS)], sem_rows).wait()


def _finish_dispatch(xs, h2s, slots, zstart, zchunks):
    Bs = h2s.shape[0]
    assert 2 * Bs <= IDX_CHUNK and slots.shape[0] == IDX_CHUNK and MOE_BLOCK % ZERO_CHUNK == 0
    any_spec = pl.BlockSpec(memory_space=pl.ANY)
    grid_spec = pltpu.PrefetchScalarGridSpec(
        num_scalar_prefetch=2,
        grid=(1,),
        in_specs=[any_spec, pl.BlockSpec((Bs,) + ROW_TILE, lambda t, z, n: (0, 0, 0)), any_spec],
        out_specs=any_spec,
        scratch_shapes=[pltpu.SMEM((IDX_CHUNK,), jnp.int32), pltpu.VMEM((ZERO_CHUNK,) + ROW_TILE, ROW_DTYPE),
                        pltpu.SemaphoreType.DMA, pltpu.SemaphoreType.DMA, pltpu.SemaphoreType.DMA],
    )
    return pl.pallas_call(
        _finish_dispatch_kernel,
        grid_spec=grid_spec,
        out_shape=jax.ShapeDtypeStruct(xs.shape, ROW_DTYPE),
        input_output_aliases={4: 0},
        compiler_params=pltpu.CompilerParams(dimension_semantics=("arbitrary",), has_side_effects=True),
        name="moe_finish_dispatch",
    )(zstart, zchunks, slots, h2s, xs)


FFN_IN_BUFFERS = 4
FFN_OUT_BUFFERS = 3


def _ffn_kernel(cnt_ref, xs_hbm, w1_hbm, w3_hbm, w2_hbm, ys_hbm,
                xs_buf, ys_buf, st1, st3, st2, w13_scr, w2_scr, t_row, t_exp, t_len, sem_in, sem_out, sem_w,
                *, cap_blocks):
    B = MOE_BLOCK

    def fill_expert(e, g):
        nb = (cnt_ref[e] + (B - 1)) // B

        def fill_block(j, carry):
            t_row[g + j] = e * cap_blocks + j
            t_exp[g + j] = e
            t_len[g + j] = nb
            return carry

        lax.fori_loop(0, nb, fill_block, 0)
        return g + nb

    n_used = lax.fori_loop(0, N_EXPERTS, fill_expert, 0)

    def in_copy(g):
        s = g % FFN_IN_BUFFERS
        return pltpu.make_async_copy(xs_hbm.at[pl.ds(t_row[g] * B, B)], xs_buf.at[s], sem_in.at[s])

    def out_copy(g):
        s = g % FFN_OUT_BUFFERS
        return pltpu.make_async_copy(ys_buf.at[s], ys_hbm.at[pl.ds(t_row[g] * B, B)], sem_out.at[s])

    def weight_copies(e, s):
        return (pltpu.make_async_copy(w1_hbm.at[e], st1.at[s], sem_w.at[s]),
                pltpu.make_async_copy(w3_hbm.at[e], st3.at[s], sem_w.at[s]),
                pltpu.make_async_copy(w2_hbm.at[e], st2.at[s], sem_w.at[s]))

    @pl.when(n_used > 0)
    def _():
        for cp in weight_copies(t_exp[0], 0):
            cp.start()

    for g0 in range(FFN_IN_BUFFERS - 1):
        @pl.when(g0 < n_used)
        def _(g0=g0):
            in_copy(g0).start()

    def block(g, wslot):
        first = (g == 0) | (t_exp[g] != t_exp[jnp.maximum(g - 1, 0)])
        wslot = jnp.where(first & (g > 0), 1 - wslot, wslot)

        @pl.when(first)
        def _():
            for cp in weight_copies(t_exp[g], wslot):
                cp.wait()
            nxt = g + t_len[g]

            @pl.when(nxt < n_used)
            def _():
                for cp in weight_copies(t_exp[jnp.minimum(nxt, n_used - 1)], 1 - wslot):
                    cp.start()

            w13_scr[:, 0:D_EXPERT] = st1[wslot].astype(_BF)
            w13_scr[:, D_EXPERT:2 * D_EXPERT] = st3[wslot].astype(_BF)
            w2_scr[...] = st2[wslot].astype(_BF)

        in_copy(g).wait()

        @pl.when(g + FFN_IN_BUFFERS - 1 < n_used)
        def _():
            in_copy(g + FFN_IN_BUFFERS - 1).start()

        @pl.when(g >= FFN_OUT_BUFFERS)
        def _():
            out_copy(g - FFN_OUT_BUFFERS).wait()

        xb = _unpack_rows(xs_buf[g % FFN_IN_BUFFERS]).astype(_BF)
        ab = _dot(xb, w13_scr[...])
        hid = jax.nn.silu(ab[:, 0:D_EXPERT]) * ab[:, D_EXPERT:2 * D_EXPERT]
        ys_buf[g % FFN_OUT_BUFFERS] = _pack_rows(_dot(hid.astype(_BF), w2_scr[...]))
        out_copy(g).start()
        return wslot

    lax.fori_loop(0, n_used, block, 0)

    for back in range(FFN_OUT_BUFFERS, 0, -1):
        @pl.when(n_used >= back)
        def _(back=back):
            out_copy(n_used - back).wait()


def _ffn(xs, counts, w1, w3, w2, cap_blocks, n_blocks):
    any_spec = pl.BlockSpec(memory_space=pl.ANY)
    row_buf = lambda n: pltpu.VMEM((n, MOE_BLOCK) + ROW_TILE, ROW_DTYPE)
    grid_spec = pltpu.PrefetchScalarGridSpec(
        num_scalar_prefetch=1,
        grid=(1,),
        in_specs=[any_spec, any_spec, any_spec, any_spec],
        out_specs=any_spec,
        scratch_shapes=[row_buf(FFN_IN_BUFFERS), row_buf(FFN_OUT_BUFFERS),
                        pltpu.VMEM((2, D_MODEL, D_EXPERT), _F32), pltpu.VMEM((2, D_MODEL, D_EXPERT), _F32),
                        pltpu.VMEM((2, D_EXPERT, D_MODEL), _F32),
                        pltpu.VMEM((D_MODEL, 2 * D_EXPERT), _BF), pltpu.VMEM((D_EXPERT, D_MODEL), _BF),
                        pltpu.SMEM((n_blocks,), jnp.int32), pltpu.SMEM((n_blocks,), jnp.int32),
                        pltpu.SMEM((n_blocks,), jnp.int32),
                        pltpu.SemaphoreType.DMA((FFN_IN_BUFFERS,)), pltpu.SemaphoreType.DMA((FFN_OUT_BUFFERS,)),
                        pltpu.SemaphoreType.DMA((2,))],
    )
    return pl.pallas_call(
        functools.partial(_ffn_kernel, cap_blocks=cap_blocks),
        grid_spec=grid_spec,
        out_shape=jax.ShapeDtypeStruct(xs.shape, ROW_DTYPE),
        compiler_params=pltpu.CompilerParams(dimension_semantics=("arbitrary",), vmem_limit_bytes=VMEM_LIMIT,
                                             has_side_effects=True),
        name="moe_ffn",
    )(counts, xs, w1, w3, w2)


def _combine_kernel(slots_hbm, x1_ref, rw_ref, g_ref, ys_hbm, y_ref, idx_smem, buf, sem_idx, sem_rows):
    t = pl.program_id(0)
    TC = x1_ref.shape[0]
    idx_cp = pltpu.make_async_copy(slots_hbm.at[pl.ds(t * IDX_CHUNK, IDX_CHUNK)], idx_smem, sem_idx)
    idx_cp.start()
    idx_cp.wait()

    def issue(r, carry):
        for kk in range(2):
            pltpu.make_async_copy(ys_hbm.at[idx_smem[2 * r + kk]], buf.at[kk, r], sem_rows).start(priority=kk)
        return carry

    lax.fori_loop(0, TC, issue, 0, unroll=ISSUE_UNROLL)
    for kk in range(2):
        pltpu.make_async_copy(ys_hbm.at[pl.ds(0, TC)], buf.at[kk], sem_rows).wait()

    w = rw_ref[...]
    y0 = _unpack_rows(buf[0])
    y1 = _unpack_rows(buf[1])
    x2 = x1_ref[...] + (w[:, 0:1] * y0 + w[:, 1:2] * y1)
    y_ref[...] = _rms(x2, g_ref[...])


SC_GATHER_CHUNKS = 4
DENSE_TILE = 1024
SC_GATHER_BUFFERS = 3
SC_GATHER_MAX_WINDOW = 80


def _sc_gather(ys, rows):
    Tg = rows.shape[1]
    n_workers = SC_CORES * SC_SUBCORES
    per_w = Tg // n_workers
    assert per_w * n_workers == Tg and per_w % SUBLANES == 0
    W = max(w for w in range(SUBLANES, SC_GATHER_MAX_WINDOW + 1, SUBLANES) if per_w % w == 0)
    n_win = per_w // W
    NB = SC_GATHER_BUFFERS
    idx = rows.reshape(2, n_workers, n_win, W)
    jobs = [(kk, i) for i in range(n_win) for kk in range(2)]
    mesh = plsc.VectorSubcoreMesh(core_axis_name="c", subcore_axis_name="s")

    def body(ys_hbm, idx_hbm, out_hbm, i_v, rows_v, sem_g, sem_s):
        wid = lax.axis_index("s") * SC_CORES + lax.axis_index("c")
        base = wid * per_w
        for kk in range(2):
            pltpu.sync_copy(idx_hbm.at[kk, wid], i_v.at[kk])

        def gather(j):
            kk, i = jobs[j]
            return pltpu.make_async_copy(ys_hbm.at[i_v.at[kk, i]], rows_v.at[j % NB], sem_g.at[j % NB])

        def store(j):
            kk, i = jobs[j]
            return pltpu.make_async_copy(rows_v.at[j % NB], out_hbm.at[kk, pl.ds(base + i * W, W)], sem_s.at[j % NB])

        gather(0).start()
        for j in range(len(jobs)):
            gather(j).wait()
            store(j).start()
            if j + 1 < len(jobs):
                if j + 1 >= NB:
                    store(j + 1 - NB).wait()
                gather(j + 1).start()
        for j in range(max(0, len(jobs) - NB), len(jobs)):
            store(j).wait()

    return pl.kernel(
        body, mesh=mesh,
        out_type=jax.ShapeDtypeStruct((2, Tg) + ROW_TILE, ROW_DTYPE),
        scratch_types=[pltpu.VMEM((2, n_win, W), jnp.int32), pltpu.VMEM((NB, W) + ROW_TILE, ROW_DTYPE),
                       pltpu.SemaphoreType.DMA((NB,)), pltpu.SemaphoreType.DMA((NB,))],
        name="moe_sc_gather",
    )(ys, idx)


def _combine_dense_kernel(x1_ref, rw_ref, g_ref, rows_ref, *rest):
    y_ref = rest[-1]
    w = rw_ref[...]
    y0 = _unpack_rows(rows_ref[0])
    y1 = _unpack_rows(rows_ref[1])
    x2 = x1_ref[...] + (w[:, 0:1] * y0 + w[:, 1:2] * y1)
    y_ref[...] = _rms(x2, g_ref[...])


def _combine_dense(x1, rw, rows, y, g_final, tile, first_tile):
    Tg = rows.shape[1]
    assert Tg % tile == 0
    tok = lambda t: (first_tile + t, 0)
    in_specs = [pl.BlockSpec((tile, D_MODEL), tok), pl.BlockSpec((tile, LANES), tok), _full((1, D_MODEL)),
                pl.BlockSpec((2, tile) + ROW_TILE, lambda t: (0, t, 0, 0))]
    args = [x1, rw, g_final.reshape(1, D_MODEL), rows]
    aliases = {}
    if y is not None:
        in_specs.append(pl.BlockSpec(memory_space=pl.ANY))
        args.append(y)
        aliases = {4: 0}
    return pl.pallas_call(
        _combine_dense_kernel,
        grid=(Tg // tile,),
        in_specs=in_specs,
        out_specs=pl.BlockSpec((tile, D_MODEL), tok),
        out_shape=jax.ShapeDtypeStruct(x1.shape, _F32),
        input_output_aliases=aliases,
        compiler_params=pltpu.CompilerParams(dimension_semantics=("arbitrary",), vmem_limit_bytes=VMEM_LIMIT),
        name="moe_combine_dense",
    )(*args)


def _combine(x1, rw, slots, ys, g_final, tile, n_tiles):
    T = x1.shape[0]
    assert T % tile == 0 and 2 * tile <= IDX_CHUNK and slots.shape[0] == n_tiles * IDX_CHUNK
    any_spec = pl.BlockSpec(memory_space=pl.ANY)
    return pl.pallas_call(
        _combine_kernel,
        grid=(n_tiles,),
        in_specs=[any_spec, pl.BlockSpec((tile, D_MODEL), lambda t: (t, 0)),
                  pl.BlockSpec((tile, LANES), lambda t: (t, 0)), _full((1, D_MODEL)), any_spec],
        out_specs=pl.BlockSpec((tile, D_MODEL), lambda t: (t, 0)),
        out_shape=jax.ShapeDtypeStruct((T, D_MODEL), _F32),
        scratch_shapes=[pltpu.SMEM((IDX_CHUNK,), jnp.int32), pltpu.VMEM((2, tile) + ROW_TILE, ROW_DTYPE),
                        pltpu.SemaphoreType.DMA, pltpu.SemaphoreType.DMA],
        compiler_params=pltpu.CompilerParams(dimension_semantics=("arbitrary",), vmem_limit_bytes=VMEM_LIMIT),
        name="moe_combine",
    )(slots, x1, rw, g_final.reshape(1, D_MODEL), ys)


def _tile_for(n_tokens):
    tile = IDX_CHUNK // 2
    return tile if n_tokens % tile == 0 else n_tokens


def _chunked_slots(slot, tile):
    n_tiles = slot.shape[0] // tile
    s = slot.reshape(n_tiles, 2 * tile)
    return jnp.pad(s, ((0, 0), (0, IDX_CHUNK - 2 * tile))).reshape(-1)


def kernel(x_prompt, x_sample, state_pool, state_ret, g_mix, w_in, w_pool, pool_scale, ret_gn, w_out, g_ffn, w_grp, w_exp, w1, w3, w2, g_final):
    Bp, Lp, _ = x_prompt.shape
    Bs = x_sample.shape[0]
    Tp = Bp * Lp
    wts = _prep_weights(g_mix[0], w_pool[0], pool_scale[0], ret_gn[0], w_out[0], g_ffn[0], w_grp[0], w_exp[0])

    T_all = Tp + Bs
    cap = (-(-T_all // MOE_BLOCK) + 1) * MOE_BLOCK
    cap_blocks = cap // MOE_BLOCK

    xs_tok = x_sample.reshape(Bs, D_MODEL)
    wts["w_in"], u_s, q_s, k_s, kq_s, v_s, ga_s, gb_s = _sample_proj(xs_tok, wts["g_mix"], w_in[0], PAST_LEN)
    nret_s, q_state = _sc_state_update(state_ret[0].reshape(Bs * RET_HEADS, RET_DK, RET_DV), kq_s, v_s)
    x1p, dst_rows_p, rwp, npool_p, nret_p, counts_p, h2p = _mixer_prompt(x_prompt, wts, cap)
    xs = _sc_scatter(h2p, dst_rows_p, N_EXPERTS * cap)
    x1s, dst_s, rws, npool_s, counts, h2s = _mixer_sample(
        xs_tok, (u_s, q_s, k_s, v_s, ga_s, gb_s), jnp.swapaxes(state_pool[0], 0, 1), q_state, wts, counts_p, cap)

    counts = counts[0, :N_EXPERTS].astype(jnp.int32)
    tile_p, tile_s = _tile_for(Tp), _tile_for(Bs)
    slots_s = _chunked_slots(dst_s[:, :2], tile_s)
    zstart = jnp.arange(N_EXPERTS, dtype=jnp.int32) * cap + counts
    zchunks = ((-counts) % MOE_BLOCK + ZERO_CHUNK - 1) // ZERO_CHUNK
    xs = _finish_dispatch(xs, h2s, slots_s, zstart, zchunks.astype(jnp.int32))
    n_blocks = -(-2 * T_all // MOE_BLOCK) + N_EXPERTS
    ys = _ffn(xs, counts, w1[0], w3[0], w2[0], cap_blocks, n_blocks)
    tile_d = DENSE_TILE if Tp % DENSE_TILE == 0 else tile_p
    n_tiles_d = Tp // tile_d
    chunk_tiles = [n_tiles_d // SC_GATHER_CHUNKS + (k < n_tiles_d % SC_GATHER_CHUNKS)
                   for k in range(SC_GATHER_CHUNKS)]
    gathered, first = [], 0
    for nt in chunk_tiles:
        if nt:
            gathered.append((first, _sc_gather(ys, dst_rows_p[:, first * tile_d:(first + nt) * tile_d])))
            first += nt
    y_s = _combine(x1s, rws, slots_s, ys, g_final, tile_s, 1)
    y_p = None
    for first, rows in gathered:
        y_p = _combine_dense(x1p, rwp, rows, y_p, g_final, tile_d, first)

    return (y_p.reshape(Bp, Lp, D_MODEL), y_s.reshape(Bs, 1, D_MODEL),
            npool_p[None], nret_p[None], jnp.swapaxes(npool_s, 0, 1)[None], nret_s.reshape(state_ret.shape))
```

```python
import dataclasses
import functools

import jax
import jax.numpy as jnp
import numpy as np
from jax import lax
from jax.experimental import pallas as pl
from jax.experimental.pallas import tpu as pltpu
from jax.experimental.pallas import tpu_sc as plsc

D_MODEL = 1024
EPS = 1e-6
POOL_GROUPS = 4
POOL_IN = D_MODEL // 2
POOL_GC = POOL_IN // POOL_GROUPS
POOL_OUT_GC = D_MODEL // POOL_GROUPS
POOL_WINDOWS = (2, 4, 8, 16)
POOL_HIST = max(POOL_WINDOWS) - 1
RET_HEADS = 4
RET_DK = D_MODEL // 8
RET_DV = D_MODEL // RET_HEADS
ROPE_BASE = 10000.0
PAST_LEN = 16384
N_GROUPS = 4
EXPERTS_PER_GROUP = 8
N_EXPERTS = N_GROUPS * EXPERTS_PER_GROUP
D_EXPERT = D_MODEL // 4
QK_W = RET_HEADS * RET_DK
V_W = RET_HEADS * RET_DV
OFF_U = 0
OFF_Q = POOL_IN
OFF_K = OFF_Q + QK_W
OFF_V = OFF_K + QK_W
OFF_GA = OFF_V + V_W
OFF_GB = OFF_GA + D_MODEL
IN_WIDTH = OFF_GB + D_MODEL

LANES = 128
SUBLANES = 8
ROW_WORDS = D_MODEL // 2
ROW_TILE = (ROW_WORDS // LANES, LANES)
ROW_DTYPE = jnp.uint32
HALO = 16
RET_CHUNK = 256
MIXER_SEQS = 2
MOE_BLOCK = 512
ROUTER_W = LANES
RANK_GROUP = 128
VMEM_LIMIT = 56 * 1024 * 1024

_BF = jnp.bfloat16
_F32 = jnp.float32


def _rms(x, g):
    inv = lax.rsqrt(jnp.mean(x * x, axis=-1, keepdims=True) + EPS)
    return x * inv * g


def _dot(a, b):
    return jnp.dot(a, b, preferred_element_type=_F32)


def _round_bf16(x):
    return x.astype(_BF).astype(_F32)


def _pack_rows(x):
    lo = lax.bitcast_convert_type(x[:, :ROW_WORDS].astype(_BF).astype(_F32), jnp.uint32)
    hi = lax.bitcast_convert_type(x[:, ROW_WORDS:].astype(_BF).astype(_F32), jnp.uint32)
    return ((lo >> 16) | hi).reshape((x.shape[0],) + ROW_TILE)


def _unpack_rows(w):
    w = w.reshape(w.shape[0], ROW_WORDS)
    lo = lax.bitcast_convert_type(w << 16, _F32)
    hi = lax.bitcast_convert_type(w & jnp.uint32(0xFFFF0000), _F32)
    return jnp.concatenate([lo, hi], axis=1)


def _rotary(x, cos2, sin2):
    return x * cos2 + pltpu.roll(x, RET_DK // 2, 1) * sin2


def _route(logits):
    lane = lax.broadcasted_iota(jnp.int32, logits.shape, 1).astype(_F32)
    neg = jnp.float32(-jnp.inf)
    big = jnp.float32(1 << 20)
    lg = jnp.where(lane < N_GROUPS, logits, neg)
    mg = jnp.max(lg, axis=-1, keepdims=True)
    g_idx = jnp.min(jnp.where(lg == mg, lane, big), axis=-1, keepdims=True)
    p_g = 1.0 / jnp.sum(jnp.exp(lg - mg), axis=-1, keepdims=True)
    lo = N_GROUPS + g_idx * EXPERTS_PER_GROUP
    in_grp = (lane >= lo) & (lane < lo + EXPERTS_PER_GROUP)
    le = jnp.where(in_grp, logits, neg)
    m1 = jnp.max(le, axis=-1, keepdims=True)
    i1 = jnp.min(jnp.where(le == m1, lane, big), axis=-1, keepdims=True)
    le2 = jnp.where(lane == i1, neg, le)
    m2 = jnp.max(le2, axis=-1, keepdims=True)
    i2 = jnp.min(jnp.where(le2 == m2, lane, big), axis=-1, keepdims=True)
    t = jnp.exp(m2 - m1)
    den = 1.0 + t
    e0 = (i1 - N_GROUPS).astype(jnp.int32)
    e1 = (i2 - N_GROUPS).astype(jnp.int32)
    return e0, e1, p_g * (1.0 / den), p_g * (t / den)


def _post_mix(x, mixed_ref, w_out_ref, g_ffn_ref, w_rt_ref, x1_ref, rw_ref):
    x1 = x + _dot(mixed_ref[...], w_out_ref[...])
    x1_ref[...] = x1.reshape(x1_ref.shape)
    h2 = _rms(x1, g_ffn_ref[...])
    e0, e1, w0, w1 = _route(_dot(h2.astype(_BF), w_rt_ref[...]))
    lane = lax.broadcasted_iota(jnp.int32, (x.shape[0], LANES), 1)
    rw_ref[...] = jnp.where(lane == 0, w0, jnp.where(lane == 1, w1, 0.0)).reshape(rw_ref.shape)
    return h2, e0, e1


def _route_block(x1, g_ffn_ref, w_rt_ref, carry_scr, cap, valid, h2_ref, rw_ref, dst_t_ref):
    for _ in _route_stages(x1, g_ffn_ref, w_rt_ref, carry_scr, cap, valid, h2_ref, rw_ref, dst_t_ref):
        pass


def _route_stages(x1, g_ffn_ref, w_rt_ref, carry_scr, cap, valid, h2_ref, rw_ref, dst_t_ref):
    h2 = _rms(x1, g_ffn_ref[...])
    logits = _dot(h2.astype(_BF), w_rt_ref[...])
    h2_ref[...] = _pack_rows(h2).reshape(h2_ref.shape)
    yield
    e0, e1, w0, w1 = _route(logits)
    lane = lax.broadcasted_iota(jnp.int32, (x1.shape[0], LANES), 1)
    rw_ref[...] = jnp.where(lane == 0, w0, jnp.where(lane == 1, w1, 0.0)).reshape(rw_ref.shape)
    yield
    dst = _rank_rows(e0, e1, cap, carry_scr, valid)
    dst_t = dst.T[0:SUBLANES, :].astype(jnp.int32)
    n_parts, _, width = dst_t_ref.shape
    for part in range(n_parts):
        dst_t_ref[part] = dst_t[:, part * width:(part + 1) * width]
    yield


def _rank_rows(e0, e1, cap, carry_scr, valid=None):
    R = e0.shape[0]
    lane = lax.broadcasted_iota(jnp.int32, (R, LANES), 1)
    m0 = lane == e0
    m1 = lane == e1
    onehot = jnp.where(m0 | m1, 1.0, 0.0)
    G = min(RANK_GROUP, R)
    r_i = lax.broadcasted_iota(jnp.int32, (G, G), 0)
    c_i = lax.broadcasted_iota(jnp.int32, (G, G), 1)
    tri = jnp.where(c_i < r_i, 1.0, 0.0).astype(_BF)
    running = carry_scr[...]
    parts = []
    for g in range(R // G):
        grp = onehot[g * G:(g + 1) * G]
        parts.append(_dot(tri, grp.astype(_BF)) + running)
        running = running + jnp.sum(grp, axis=0, keepdims=True)
    before = jnp.concatenate(parts, axis=0)
    d0 = e0.astype(_F32) * cap + jnp.sum(jnp.where(m0, before, 0.0), axis=-1, keepdims=True)
    d1 = e1.astype(_F32) * cap + jnp.sum(jnp.where(m1, before, 0.0), axis=-1, keepdims=True)
    added = running - carry_scr[...]
    carry_scr[...] += added if valid is None else added * valid
    return jnp.where(lane == 0, d0, jnp.where(lane == 1, d1, 0.0))


def _group_norm(o, gain):
    mu = jnp.mean(o, axis=-1, keepdims=True)
    d = o - mu
    var = jnp.mean(d * d, axis=-1, keepdims=True)
    return d * lax.rsqrt(var + EPS) * gain


def _mixer_prompt_kernel(x_ref, g_mix_ref, w_in_ref, w_pool_ref, pscale_ref, gn_ref, w_out_ref, g_ffn_ref,
                         w_rt_ref, cos_ref, sin_ref, dmask_ref, xi_ref, zeta_ref, gc_ref, w1_ref, w3_ref, w2_ref,
                         x1_ref, dst_t_ref, rw_ref, npool_ref, nret_ref, cnt_ref, h2_ref, w13_ref, w2b_ref,
                         u_scr, s_scr, mixed_scr, carry_scr, z_scr, x1_prev_scr, *, cap):
    c = pl.program_id(1)
    n_c = pl.num_programs(1)
    step = pl.program_id(0) * n_c + c
    NSEQ, C, _ = x_ref.shape
    R = NSEQ * C

    @pl.when(step == 0)
    def _():
        carry_scr[...] = jnp.zeros(carry_scr.shape, _F32)
        x1_prev_scr[...] = jnp.zeros(x1_prev_scr.shape, _F32)

    @pl.when(c == 0)
    def _():
        u_scr[:, 0:HALO, :] = jnp.zeros((NSEQ, HALO, POOL_IN), _F32)
        s_scr[...] = jnp.zeros(s_scr.shape, _F32)

    x = x_ref[...].reshape(R, D_MODEL)
    h = _rms(x, g_mix_ref[...]).astype(_BF)

    stages = _route_stages(x1_prev_scr[...], g_ffn_ref, w_rt_ref, carry_scr, cap, (step > 0).astype(_F32),
                           h2_ref, rw_ref, dst_t_ref)
    def cast_expert_weights(i):
        if i == 0:
            w13_ref[:, :, 0:D_EXPERT] = w1_ref[...].astype(_BF)
        elif i == 1:
            w13_ref[:, :, D_EXPERT:2 * D_EXPERT] = w3_ref[...].astype(_BF)
        elif i == 2:
            w2b_ref[...] = w2_ref[...].astype(_BF)

    for i, (lo, hi) in enumerate(((0, OFF_V), (OFF_V, OFF_GA), (OFF_GA, OFF_GB), (OFF_GB, IN_WIDTH))):
        z_scr[:, lo:hi] = _dot(h, w_in_ref[:, lo:hi])
        next(stages, None)
        cast_expert_weights(i)
    cnt_ref[...] = carry_scr[...]

    u_scr[:, HALO:HALO + C, :] = z_scr[:, OFF_U:OFF_U + POOL_IN].reshape(NSEQ, C, POOL_IN)
    q = z_scr[:, OFF_Q:OFF_Q + QK_W]
    k = z_scr[:, OFF_K:OFF_K + QK_W]
    cos2 = jnp.concatenate([cos_ref[...]] * NSEQ, axis=0)
    sin2 = jnp.concatenate([sin_ref[...]] * NSEQ, axis=0)
    pos1 = (c * C + 1 + lax.broadcasted_iota(jnp.int32, (C, POOL_GC), 0)).astype(_F32)

    for j in range(RET_HEADS):
        win = POOL_WINDOWS[j]
        cs = slice(j * POOL_GC, (j + 1) * POOL_GC)
        n_rows = jnp.minimum(pos1, jnp.float32(win))
        pooled = []
        for sq in range(NSEQ):
            u_j = u_scr[sq, HALO:HALO + C, cs]
            s = u_j
            for d in range(1, win):
                s = s + u_scr[sq, HALO - d:HALO - d + C, cs]
            pooled.append(s / n_rows - u_j)
        osl = slice(j * POOL_OUT_GC, (j + 1) * POOL_OUT_GC)
        pool_out = _dot(jnp.concatenate(pooled, axis=0).astype(_BF), w_pool_ref[j]) * pscale_ref[:, osl]

        qs = slice(j * RET_DK, (j + 1) * RET_DK)
        qb = _rotary(q[:, qs], cos2, sin2).astype(_BF)
        kf = _rotary(k[:, qs], cos2, sin2) * (RET_DK ** -0.5)
        kb = kf.astype(_BF)
        vb = z_scr[:, OFF_V + j * RET_DV:OFF_V + (j + 1) * RET_DV].astype(_BF)
        ret = []
        for sq in range(NSEQ):
            rows = slice(sq * C, (sq + 1) * C)
            scores = lax.dot_general(qb[rows], kb[rows], (((1,), (1,)), ((), ())),
                                     preferred_element_type=_F32) * dmask_ref[j]
            s_old = s_scr[sq, j]
            o = _dot(scores.astype(_BF), vb[rows]) + _dot(qb[rows], s_old.astype(_BF)) * xi_ref[j]
            kz = (kf[rows] * zeta_ref[j]).astype(_BF)
            s_scr[sq, j] = gc_ref[j] * s_old + lax.dot_general(kz, vb[rows], (((0,), (0,)), ((), ())),
                                                               preferred_element_type=_F32)
            ret.append(_group_norm(o, gn_ref[:, osl]))
        ret_out = jnp.concatenate(ret, axis=0)

        ga = z_scr[:, OFF_GA + j * RET_DV:OFF_GA + (j + 1) * RET_DV]
        gb = z_scr[:, OFF_GB + j * RET_DV:OFF_GB + (j + 1) * RET_DV]
        mixed_scr[:, osl] = (jax.nn.sigmoid(ga) * pool_out + jax.nn.sigmoid(gb) * ret_out).astype(_BF)

    x1 = x + _dot(mixed_scr[...], w_out_ref[...])
    x1_ref[...] = x1.reshape(x1_ref.shape)
    x1_prev_scr[...] = x1

    u_scr[:, 0:HALO, :] = u_scr[:, C:C + HALO, :]

    @pl.when(c == n_c - 1)
    def _():
        npool_ref[...] = u_scr[:, HALO + C - POOL_HIST:HALO + C, :]
        nret_ref[...] = s_scr[...]


def _decay_tables(C):
    f32 = np.float32
    log_g = np.log(f32(1.0) - np.exp2(f32(-5.0) - np.arange(RET_HEADS, dtype=f32)))
    i = np.arange(C, dtype=f32)
    diff = i[:, None] - i[None, :]
    dmask = np.where(diff >= 0, np.exp(np.maximum(diff, f32(0.0))[None] * log_g[:, None, None]), f32(0.0))
    xi = np.exp((i[None, :] + f32(1.0)) * log_g[:, None])
    zeta = np.exp((f32(C) - f32(1.0) - i)[None, :] * log_g[:, None])
    g_chunk = np.exp(f32(C) * log_g)
    return dmask.astype(f32), xi.astype(f32), zeta.astype(f32), g_chunk.astype(f32)


def _rope_tables(pos):
    half = RET_DK // 2
    freqs = ROPE_BASE ** (-jnp.arange(half, dtype=_F32) / half)
    ang = pos[:, None] * freqs[None, :]
    cos, sin = jnp.cos(ang), jnp.sin(ang)
    return jnp.concatenate([cos, cos], axis=-1), jnp.concatenate([-sin, sin], axis=-1)


def _const(a, shape=None):
    if shape is not None:
        a = np.ascontiguousarray(np.broadcast_to(a, shape))
    return jnp.asarray(a)


def _full(shape):
    n = len(shape)
    return pl.BlockSpec(shape, lambda *_: (0,) * n)


def _route_tail_kernel(x1_ref, g_ffn_ref, w_rt_ref, cnt_in_ref, dst_t_in, rw_in, h2_in,
                       dst_t_ref, rw_ref, cnt_ref, h2_ref, carry_scr, *, cap):
    del dst_t_in, rw_in, h2_in
    carry_scr[...] = cnt_in_ref[...]
    x1 = x1_ref[...]
    _route_block(x1.reshape(x1.shape[0] * x1.shape[1], D_MODEL), g_ffn_ref, w_rt_ref, carry_scr, cap, None,
                 h2_ref, rw_ref, dst_t_ref)
    cnt_ref[...] = carry_scr[...]


def _route_tail(x1, dst_t, rw, counts, h2, wts, cap, C, last):
    NSEQ = x1.shape[0]
    any_spec = pl.BlockSpec(memory_space=pl.ANY)
    return pl.pallas_call(
        functools.partial(_route_tail_kernel, cap=cap),
        grid=(1,),
        in_specs=[pl.BlockSpec((NSEQ, C, D_MODEL), lambda i: (0, last, 0)), _full((1, D_MODEL)),
                  _full((D_MODEL, ROUTER_W)), _full((1, LANES)), any_spec, any_spec, any_spec],
        out_specs=(pl.BlockSpec((NSEQ, SUBLANES, C), lambda i: (0, 0, last)),
                   pl.BlockSpec((NSEQ, C, LANES), lambda i: (0, last, 0)),
                   _full((1, LANES)),
                   pl.BlockSpec((NSEQ, C) + ROW_TILE, lambda i: (0, last, 0, 0))),
        out_shape=(jax.ShapeDtypeStruct(dst_t.shape, dst_t.dtype), jax.ShapeDtypeStruct(rw.shape, rw.dtype),
                   jax.ShapeDtypeStruct((1, LANES), _F32), jax.ShapeDtypeStruct(h2.shape, h2.dtype)),
        input_output_aliases={4: 0, 5: 1, 6: 3},
        scratch_shapes=[pltpu.VMEM((1, LANES), _F32)],
        compiler_params=pltpu.CompilerParams(dimension_semantics=("arbitrary",), vmem_limit_bytes=VMEM_LIMIT),
        name="mixer_route_tail",
    )(x1, wts["g_ffn"], wts["w_rt"], counts, dst_t, rw, h2)


def _mixer_prompt(x, wts, w1, w3, w2, cap):
    B, L, _ = x.shape
    C = RET_CHUNK if L % RET_CHUNK == 0 else L
    n_c = L // C
    T = B * L
    dmask, xi, zeta, g_chunk = _decay_tables(C)
    dmask = _const(dmask)
    xi_b = _const(xi[:, :, None], (RET_HEADS, C, RET_DV))
    zeta_b = _const(zeta[:, :, None], (RET_HEADS, C, RET_DK))
    gc_b = _const(g_chunk[:, None, None], (RET_HEADS, 1, RET_DV))
    cos2, sin2 = _rope_tables(jnp.arange(L).astype(_F32))

    NSEQ = MIXER_SEQS if B % MIXER_SEQS == 0 else 1
    B2 = B // NSEQ
    T2 = T // NSEQ
    x = x.reshape(NSEQ, B2, L, D_MODEL)
    tok = lambda b, c: (0, b * n_c + c, 0)
    n_e = w1.shape[0]
    eps = -(-n_e // (B2 * n_c))
    assert n_e % eps == 0
    w_blk = lambda b, c: (jnp.minimum(b * n_c + c, n_e // eps - 1), 0, 0)
    in_specs = [
        pl.BlockSpec((NSEQ, None, C, D_MODEL), lambda b, c: (0, b, c, 0)),
        _full((1, D_MODEL)), _full((D_MODEL, IN_WIDTH)), _full((POOL_GROUPS, POOL_GC, POOL_OUT_GC)),
        _full((1, D_MODEL)), _full((1, D_MODEL)), _full((D_MODEL, D_MODEL)), _full((1, D_MODEL)),
        _full((D_MODEL, ROUTER_W)),
        pl.BlockSpec((C, RET_DK), lambda b, c: (c, 0)), pl.BlockSpec((C, RET_DK), lambda b, c: (c, 0)),
        _full((RET_HEADS, C, C)), _full((RET_HEADS, C, RET_DV)), _full((RET_HEADS, C, RET_DK)),
        _full((RET_HEADS, 1, RET_DV)),
        pl.BlockSpec((eps, D_MODEL, D_EXPERT), w_blk), pl.BlockSpec((eps, D_MODEL, D_EXPERT), w_blk),
        pl.BlockSpec((eps, D_EXPERT, D_MODEL), w_blk),
    ]
    R = NSEQ * C
    out_shape = (
        jax.ShapeDtypeStruct((NSEQ, T2, D_MODEL), _F32),
        jax.ShapeDtypeStruct((NSEQ, SUBLANES, T2), jnp.int32),
        jax.ShapeDtypeStruct((NSEQ, T2, LANES), _F32),
        jax.ShapeDtypeStruct((NSEQ, B2, POOL_HIST, POOL_IN), _F32),
        jax.ShapeDtypeStruct((NSEQ, B2, RET_HEADS, RET_DK, RET_DV), _F32),
        jax.ShapeDtypeStruct((1, LANES), _F32),
        jax.ShapeDtypeStruct((NSEQ, T2) + ROW_TILE, ROW_DTYPE),
        jax.ShapeDtypeStruct((n_e, D_MODEL, 2 * D_EXPERT), _BF),
        jax.ShapeDtypeStruct((n_e, D_EXPERT, D_MODEL), _BF),
    )
    prev = lambda b, c: jnp.maximum(b * n_c + c - 1, 0)
    out_specs = (
        pl.BlockSpec((NSEQ, C, D_MODEL), tok),
        pl.BlockSpec((NSEQ, SUBLANES, C), lambda b, c: (0, 0, prev(b, c))),
        pl.BlockSpec((NSEQ, C, LANES), lambda b, c: (0, prev(b, c), 0)),
        pl.BlockSpec((NSEQ, None, POOL_HIST, POOL_IN), lambda b, c: (0, b, 0, 0)),
        pl.BlockSpec((NSEQ, None, RET_HEADS, RET_DK, RET_DV), lambda b, c: (0, b, 0, 0, 0)),
        _full((1, LANES)),
        pl.BlockSpec((NSEQ, C) + ROW_TILE, lambda b, c: (0, prev(b, c), 0, 0)),
        pl.BlockSpec((eps, D_MODEL, 2 * D_EXPERT), w_blk), pl.BlockSpec((eps, D_EXPERT, D_MODEL), w_blk),
    )
    x1, dst_t, rw, npool, nret, counts, h2, w13, w2b = pl.pallas_call(
        functools.partial(_mixer_prompt_kernel, cap=cap),
        grid=(B2, n_c),
        in_specs=in_specs,
        out_specs=out_specs,
        out_shape=out_shape,
        scratch_shapes=[pltpu.VMEM((NSEQ, HALO + C, POOL_IN), _F32),
                        pltpu.VMEM((NSEQ, RET_HEADS, RET_DK, RET_DV), _F32),
                        pltpu.VMEM((R, D_MODEL), _BF),
                        pltpu.VMEM((1, LANES), _F32),
                        pltpu.VMEM((R, IN_WIDTH), _F32),
                        pltpu.VMEM((R, D_MODEL), _F32)],
        compiler_params=pltpu.CompilerParams(dimension_semantics=("arbitrary", "arbitrary"),
                                             vmem_limit_bytes=VMEM_LIMIT),
        name="mixer_prompt",
    )(x, wts["g_mix"], wts["w_in"], wts["w_pool"], wts["pool_scale"], wts["ret_gn"], wts["w_out"],
      wts["g_ffn"], wts["w_rt"], cos2, sin2, dmask, xi_b, zeta_b, gc_b, w1, w3, w2)
    dst_t, rw, counts, h2 = _route_tail(x1, dst_t, rw, counts, h2, wts, cap, C, B2 * n_c - 1)
    dst_rows = jnp.moveaxis(dst_t[:, 0:2, :], 1, 0).reshape(2, T)
    return (x1.reshape(T, D_MODEL), dst_rows, rw.reshape(T, LANES),
            npool.reshape(B, POOL_HIST, POOL_IN), nret.reshape(B, RET_HEADS, RET_DK, RET_DV), counts,
            h2.reshape((T,) + ROW_TILE), w13, w2b)


def _prep_weights(g_mix, w_pool, pool_scale, ret_gn, w_out, g_ffn, w_grp, w_exp):
    w_rt = jnp.concatenate([w_grp, w_exp.reshape(D_MODEL, N_EXPERTS)], axis=1)
    w_rt = jnp.pad(w_rt, ((0, 0), (0, ROUTER_W - w_rt.shape[1])))
    row = lambda v: v.reshape(1, D_MODEL)
    return dict(g_mix=row(g_mix), w_pool=w_pool.astype(_BF), pool_scale=row(pool_scale),
                ret_gn=row(ret_gn), w_out=w_out.astype(_BF), g_ffn=row(g_ffn), w_rt=w_rt.astype(_BF))


PROJ_COLS = 512


def _sample_proj_kernel(x_ref, g_mix_ref, w_in_ref, cos_ref, sin_ref, zeta_ref,
                        w_bf_ref, u_ref, q_ref, k_ref, kq_ref, v_ref, ga_ref, gb_ref, h_scr, z_scr):
    c = pl.program_id(0)

    @pl.when(c == 0)
    def _():
        h_scr[...] = _rms(x_ref[...], g_mix_ref[...]).astype(_BF)

    w = w_in_ref[...].astype(_BF)
    w_bf_ref[...] = w
    z_scr[c] = _dot(h_scr[...], w)

    @pl.when(c == pl.num_programs(0) - 1)
    def _():
        def cols(off, width):
            parts = [z_scr[i] for i in range(off // PROJ_COLS, (off + width) // PROJ_COLS)]
            return parts[0] if len(parts) == 1 else jnp.concatenate(parts, axis=1)

        u_ref[...] = cols(OFF_U, POOL_IN)
        q = cols(OFF_Q, QK_W)
        k = cols(OFF_K, QK_W)
        for j in range(RET_HEADS):
            qs = slice(j * RET_DK, (j + 1) * RET_DK)
            qf = _rotary(q[:, qs], cos_ref[...], sin_ref[...])
            kf = _rotary(k[:, qs], cos_ref[...], sin_ref[...]) * (RET_DK ** -0.5)
            q_ref[:, qs] = qf
            k_ref[:, qs] = kf
            kq_ref[0, :, qs] = kf * zeta_ref[j]
            kq_ref[1, :, qs] = _round_bf16(qf)
        v_ref[...] = cols(OFF_V, V_W)
        ga_ref[...] = cols(OFF_GA, D_MODEL)
        gb_ref[...] = cols(OFF_GB, D_MODEL)


def _sample_proj(x, g_mix, w_in, pos0):
    Bs = x.shape[0]
    offsets = (OFF_U, OFF_Q, OFF_K, OFF_V, OFF_GA, OFF_GB, IN_WIDTH)
    assert all(o % PROJ_COLS == 0 for o in offsets)
    n_c = IN_WIDTH // PROJ_COLS
    _, _, zeta, _ = _decay_tables(1)
    cos2, sin2 = _rope_tables((pos0 + jnp.arange(1)).astype(_F32))
    shapes = ((Bs, POOL_IN), (Bs, QK_W), (Bs, QK_W), (2, Bs, QK_W), (Bs, V_W), (Bs, D_MODEL), (Bs, D_MODEL))
    return pl.pallas_call(
        _sample_proj_kernel,
        grid=(n_c,),
        in_specs=[_full((Bs, D_MODEL)), _full((1, D_MODEL)), pl.BlockSpec((D_MODEL, PROJ_COLS), lambda c: (0, c)),
                  _full((1, RET_DK)), _full((1, RET_DK)), _full((RET_HEADS, 1, 1))],
        out_specs=(pl.BlockSpec((D_MODEL, PROJ_COLS), lambda c: (0, c)),) + tuple(_full(sh) for sh in shapes),
        out_shape=(jax.ShapeDtypeStruct((D_MODEL, IN_WIDTH), _BF),)
        + tuple(jax.ShapeDtypeStruct(sh, _F32) for sh in shapes),
        scratch_shapes=[pltpu.VMEM((Bs, D_MODEL), _BF), pltpu.VMEM((n_c, Bs, PROJ_COLS), _F32)],
        compiler_params=pltpu.CompilerParams(dimension_semantics=("arbitrary",), vmem_limit_bytes=VMEM_LIMIT),
        name="sample_proj",
    )(x, g_mix, w_in, cos2, sin2, _const(zeta[:, :, None], (RET_HEADS, 1, 1)))


def _mixer_sample_kernel(x_ref, u_ref, q_ref, k_ref, v_ref, ga_ref, gb_ref, spool_ref, qs_ref,
                         w_pool_ref, pscale_ref, gn_ref, w_out_ref, g_ffn_ref, w_rt_ref, dm_ref, xi_ref,
                         carry_in_ref,
                         x1_ref, dst_ref, rw_ref, npool_ref, cnt_ref, h2_ref,
                         mixed_scr, carry_scr, *, cap):
    u = u_ref[...]
    npool_ref[0:POOL_HIST - 1] = spool_ref[1:POOL_HIST]
    npool_ref[POOL_HIST - 1] = u
    for j in range(RET_HEADS):
        cs = slice(j * POOL_GC, (j + 1) * POOL_GC)
        qs = slice(j * RET_DK, (j + 1) * RET_DK)
        osl = slice(j * POOL_OUT_GC, (j + 1) * POOL_OUT_GC)
        win = POOL_WINDOWS[j]
        s = u[:, cs]
        for r in range(POOL_HIST - (win - 1), POOL_HIST):
            s = s + spool_ref[r, :, cs]
        pooled = s / jnp.float32(win) - u[:, cs]
        pool_out = _dot(pooled.astype(_BF), w_pool_ref[j]) * pscale_ref[:, osl]
        score = jnp.sum(q_ref[:, qs] * k_ref[:, qs], axis=-1, keepdims=True) * dm_ref[j]
        q_state = jnp.concatenate([qs_ref[hf, pl.ds(j, u.shape[0], stride=RET_HEADS), :]
                                   for hf in range(qs_ref.shape[0])], axis=1)
        o = score * v_ref[:, osl] + q_state * xi_ref[j]
        ret_out = _group_norm(o, gn_ref[:, osl])
        mixed_scr[:, osl] = (jax.nn.sigmoid(ga_ref[:, osl]) * pool_out
                             + jax.nn.sigmoid(gb_ref[:, osl]) * ret_out).astype(_BF)
    h2, e0, e1 = _post_mix(x_ref[...], mixed_scr, w_out_ref, g_ffn_ref, w_rt_ref, x1_ref, rw_ref)
    h2_ref[...] = _pack_rows(h2).reshape(h2_ref.shape)
    carry_scr[...] = carry_in_ref[...]
    dst_ref[...] = _rank_rows(e0, e1, cap, carry_scr).astype(jnp.int32)
    cnt_ref[...] = carry_scr[...]


def _mixer_sample(x, proj, state_pool, q_state, wts, carry_in, cap):
    Bs = x.shape[0]
    assert Bs == LANES and POOL_GROUPS == RET_HEADS
    u, q, k, v, ga, gb = proj
    dmask, xi, _, _ = _decay_tables(1)
    dm_b = _const(dmask, (RET_HEADS, 1, 1))
    xi_b = _const(xi[:, :, None], (RET_HEADS, 1, RET_DV))

    in_specs = [
        _full((Bs, D_MODEL)),
        _full((Bs, POOL_IN)), _full((Bs, QK_W)), _full((Bs, QK_W)), _full((Bs, V_W)),
        _full((Bs, D_MODEL)), _full((Bs, D_MODEL)),
        _full((POOL_HIST, Bs, POOL_IN)), _full(q_state.shape),
        _full((POOL_GROUPS, POOL_GC, POOL_OUT_GC)),
        _full((1, D_MODEL)), _full((1, D_MODEL)), _full((D_MODEL, D_MODEL)), _full((1, D_MODEL)),
        _full((D_MODEL, ROUTER_W)),
        _full((RET_HEADS, 1, 1)), _full((RET_HEADS, 1, RET_DV)),
        _full((1, LANES)),
    ]
    out_shape = (
        jax.ShapeDtypeStruct((Bs, D_MODEL), _F32),
        jax.ShapeDtypeStruct((Bs, LANES), jnp.int32),
        jax.ShapeDtypeStruct((Bs, LANES), _F32),
        jax.ShapeDtypeStruct((POOL_HIST, Bs, POOL_IN), _F32),
        jax.ShapeDtypeStruct((1, LANES), _F32),
        jax.ShapeDtypeStruct((Bs,) + ROW_TILE, ROW_DTYPE),
    )
    out_specs = (
        _full((Bs, D_MODEL)), _full((Bs, LANES)), _full((Bs, LANES)), _full((POOL_HIST, Bs, POOL_IN)),
        _full((1, LANES)), _full((Bs,) + ROW_TILE),
    )
    return pl.pallas_call(
        functools.partial(_mixer_sample_kernel, cap=cap),
        grid=(1,),
        in_specs=in_specs,
        out_specs=out_specs,
        out_shape=out_shape,
        scratch_shapes=[pltpu.VMEM((Bs, D_MODEL), _BF), pltpu.VMEM((1, LANES), _F32)],
        compiler_params=pltpu.CompilerParams(dimension_semantics=("arbitrary",), vmem_limit_bytes=VMEM_LIMIT),
        name="mixer_sample",
    )(x, u, q, k, v, ga, gb, state_pool, q_state, wts["w_pool"], wts["pool_scale"], wts["ret_gn"],
      wts["w_out"], wts["g_ffn"], wts["w_rt"], dm_b, xi_b, carry_in)


SC_CORES = 2
SC_SUBCORES = 16
SC_LANES = 16
STATE_ROWS = 32
STATE_UNROLL = 8
STATE_COLS = 8


def _sc_round_bf16(x):
    b = lax.bitcast_convert_type(x, jnp.uint32)
    r = b + jnp.uint32(0x7FFF) + ((b >> 16) & jnp.uint32(1))
    return lax.bitcast_convert_type(r & jnp.uint32(0xFFFF0000), _F32)


def _sc_state_update(s0, kq, v):
    P = s0.shape[0]
    n_workers = SC_CORES * SC_SUBCORES
    R = STATE_ROWS
    n_parts = RET_DK // R
    n_vc = RET_DV // SC_LANES
    assert P % (n_workers * RET_HEADS) == 0 and RET_DK % R == 0 and n_parts % 2 == 0 and n_parts >= 2
    assert R % STATE_UNROLL == 0 and n_vc % STATE_COLS == 0
    per_w = P // n_workers
    seqs_w = per_w // RET_HEADS
    _, _, _, g_chunk = _decay_tables(1)
    decay = _const(g_chunk[:, None], (RET_HEADS, SC_LANES))
    mesh = plsc.VectorSubcoreMesh(core_axis_name="c", subcore_axis_name="s")

    def body(s0_hbm, kq_hbm, v_hbm, g_hbm, out_hbm, qs_hbm, in_v, out_v, k_v, q_v, v_v, g_v, o_v, sem_in, sem_out):
        wid = lax.axis_index("s") * SC_CORES + lax.axis_index("c")
        base = wid * per_w
        seqs = pl.ds(wid * seqs_w, seqs_w)
        pltpu.sync_copy(kq_hbm.at[0, seqs], k_v)
        pltpu.sync_copy(kq_hbm.at[1, seqs], q_v)
        pltpu.sync_copy(v_hbm.at[seqs], v_v)
        pltpu.sync_copy(g_hbm, g_v)

        def load(p, part, slot):
            return pltpu.make_async_copy(s0_hbm.at[p, pl.ds(part * R, R)], in_v.at[slot], sem_in.at[slot])

        def store(p, part, slot):
            return pltpu.make_async_copy(out_v.at[slot], out_hbm.at[p, pl.ds(part * R, R)], sem_out.at[slot])

        def o_at(pp, c):
            return (c * SC_LANES // LANES, pp, pl.ds(c * SC_LANES % LANES, SC_LANES))

        load(base, 0, 0).start()

        def pair(pp, carry):
            p = base + pp
            seq, head = pp // RET_HEADS, pp % RET_HEADS
            g = g_v[head, :]
            sidx = jnp.full((SC_LANES,), seq, jnp.int32)
            for c in range(n_vc):
                o_v[o_at(pp, c)] = jnp.zeros((SC_LANES,), _F32)
            for part in range(n_parts):
                slot = part % 2
                load(p, part, slot).wait()
                if part + 1 < n_parts:
                    load(p, part + 1, 1 - slot).start()
                else:
                    @pl.when(pp + 1 < per_w)
                    def _():
                        load(p + 1, 0, 1 - slot).start()

                if part < 2:
                    @pl.when(pp > 0)
                    def _():
                        store(p, part, slot).wait()
                else:
                    store(p, part, slot).wait()

                def rows(i8, c2):
                    for c0 in range(0, n_vc, STATE_COLS):
                        cols = [pl.ds((c0 + c) * SC_LANES, SC_LANES) for c in range(STATE_COLS)]
                        vs = [v_v[seq, pl.ds(head * RET_DV + (c0 + c) * SC_LANES, SC_LANES)]
                              for c in range(STATE_COLS)]
                        acc = [None] * STATE_COLS
                        for j in range(STATE_UNROLL):
                            i = i8 * STATE_UNROLL + j
                            ridx = jnp.full((SC_LANES,), head * RET_DK + part * R + i, jnp.int32)
                            ki = plsc.load_gather(k_v, [sidx, ridx])
                            qi = plsc.load_gather(q_v, [sidx, ridx])
                            ss = [in_v[slot, i, cs] for cs in cols]
                            for c, cs in enumerate(cols):
                                out_v[slot, i, cs] = g * ss[c] + ki * vs[c]
                            for c in range(STATE_COLS):
                                t = qi * _sc_round_bf16(ss[c])
                                acc[c] = t if acc[c] is None else acc[c] + t
                        for c in range(STATE_COLS):
                            o_v[o_at(pp, c0 + c)] = o_v[o_at(pp, c0 + c)] + acc[c]
                    return c2

                lax.fori_loop(0, R // STATE_UNROLL, rows, 0)
                store(p, part, slot).start()
            return carry

        lax.fori_loop(0, per_w, pair, 0)
        for slot in range(2):
            store(base, slot, slot).wait()
        for hf in range(RET_DV // LANES):
            pltpu.sync_copy(o_v.at[hf], qs_hbm.at[hf, pl.ds(base, per_w)])

    keys = pltpu.VMEM((seqs_w, RET_HEADS * RET_DK), _F32)
    return pl.kernel(
        body, mesh=mesh,
        out_type=(jax.ShapeDtypeStruct(s0.shape, _F32), jax.ShapeDtypeStruct((RET_DV // LANES, P, LANES), _F32)),
        scratch_types=[pltpu.VMEM((2, R, RET_DV), _F32), pltpu.VMEM((2, R, RET_DV), _F32), keys, keys,
                       pltpu.VMEM((seqs_w, RET_HEADS * RET_DV), _F32), pltpu.VMEM((RET_HEADS, SC_LANES), _F32),
                       pltpu.VMEM((RET_DV // LANES, per_w, LANES), _F32),
                       pltpu.SemaphoreType.DMA((2,)), pltpu.SemaphoreType.DMA((2,))],
        compiler_params=dataclasses.replace(pltpu.CompilerParams(), needs_layout_passes=False),
        name="sample_state_update",
    )(s0, kq, v, decay)


IDX_CHUNK = 1024
ISSUE_UNROLL = 8
SC_WINDOW = 64


def _sc_scatter(h2, dst_rows, n_rows):
    T = h2.shape[0]
    n_workers = SC_CORES * SC_SUBCORES
    W = SC_WINDOW
    assert T % (n_workers * W) == 0
    per_w = T // n_workers
    n_win = per_w // W
    d0 = dst_rows[0].reshape(T // W, W)
    d1 = dst_rows[1].reshape(T // W, W)
    mesh = plsc.VectorSubcoreMesh(core_axis_name="c", subcore_axis_name="s")

    def body(h2_hbm, d0_hbm, d1_hbm, xs_hbm, i0_v, i1_v, rows_v, sem_load, sem_store):
        wid = lax.axis_index("s") * SC_CORES + lax.axis_index("c")
        base = wid * per_w
        pltpu.sync_copy(d0_hbm.at[pl.ds(wid * n_win, n_win)], i0_v)
        pltpu.sync_copy(d1_hbm.at[pl.ds(wid * n_win, n_win)], i1_v)
        pltpu.async_copy(h2_hbm.at[pl.ds(base, W)], rows_v.at[0], sem_load)
        for i in range(n_win):
            b = i % 2
            pltpu.make_async_copy(h2_hbm.at[pl.ds(base, W)], rows_v.at[b], sem_load).wait()
            if i + 1 < n_win:
                pltpu.async_copy(h2_hbm.at[pl.ds(base + (i + 1) * W, W)], rows_v.at[1 - b], sem_load)
            c0 = pltpu.async_copy(rows_v.at[b], xs_hbm.at[i0_v.at[i]], sem_store)
            c1 = pltpu.async_copy(rows_v.at[b], xs_hbm.at[i1_v.at[i]], sem_store)
            c0.wait()
            c1.wait()

    return pl.kernel(
        body, mesh=mesh,
        out_type=jax.ShapeDtypeStruct((n_rows,) + ROW_TILE, ROW_DTYPE),
        scratch_types=[pltpu.VMEM((n_win, W), jnp.int32), pltpu.VMEM((n_win, W), jnp.int32),
                       pltpu.VMEM((2, W) + ROW_TILE, ROW_DTYPE),
                       pltpu.SemaphoreType.DMA, pltpu.SemaphoreType.DMA],
        name="moe_sc_scatter",
    )(h2, d0, d1)


ZERO_CHUNK = 64


def _finish_dispatch_kernel(zstart_ref, zchunks_ref, slots_hbm, h2_ref, xs_in_hbm, xs_hbm, idx_smem, zbuf, sem_idx,
                            sem_rows, sem_zero):
    del xs_in_hbm
    TS = h2_ref.shape[0]
    idx_cp = pltpu.make_async_copy(slots_hbm.at[pl.ds(0, IDX_CHUNK)], idx_smem, sem_idx)
    idx_cp.start()
    zbuf[...] = jnp.zeros(zbuf.shape, zbuf.dtype)

    def zero_copy(e, j):
        return pltpu.make_async_copy(zbuf, xs_hbm.at[pl.ds(zstart_ref[e] + j * ZERO_CHUNK, ZERO_CHUNK)], sem_zero)

    for e in range(N_EXPERTS):
        lax.fori_loop(0, zchunks_ref[e], lambda j, c, e=e: (zero_copy(e, j).start(), c)[1], 0)
    idx_cp.wait()

    def issue(r, carry):
        for kk in range(2):
            pltpu.make_async_copy(h2_ref.at[r], xs_hbm.at[idx_smem[2 * r + kk]], sem_rows).start(priority=kk)
        return carry

    lax.fori_loop(0, TS, issue, 0, unroll=ISSUE_UNROLL)
    for e in range(N_EXPERTS):
        lax.fori_loop(0, zchunks_ref[e], lambda j, c, e=e: (zero_copy(e, j).wait(), c)[1], 0)
    for kk in range(2):
        pltpu.make_async_copy(h2_ref, xs_hbm.at[pl.ds(0, TS)], sem_rows).wait()


def _finish_dispatch(xs, h2s, slots, zstart, zchunks):
    Bs = h2s.shape[0]
    assert 2 * Bs <= IDX_CHUNK and slots.shape[0] == IDX_CHUNK and MOE_BLOCK % ZERO_CHUNK == 0
    any_spec = pl.BlockSpec(memory_space=pl.ANY)
    grid_spec = pltpu.PrefetchScalarGridSpec(
        num_scalar_prefetch=2,
        grid=(1,),
        in_specs=[any_spec, pl.BlockSpec((Bs,) + ROW_TILE, lambda t, z, n: (0, 0, 0)), any_spec],
        out_specs=any_spec,
        scratch_shapes=[pltpu.SMEM((IDX_CHUNK,), jnp.int32), pltpu.VMEM((ZERO_CHUNK,) + ROW_TILE, ROW_DTYPE),
                        pltpu.SemaphoreType.DMA, pltpu.SemaphoreType.DMA, pltpu.SemaphoreType.DMA],
    )
    return pl.pallas_call(
        _finish_dispatch_kernel,
        grid_spec=grid_spec,
        out_shape=jax.ShapeDtypeStruct(xs.shape, ROW_DTYPE),
        input_output_aliases={4: 0},
        compiler_params=pltpu.CompilerParams(dimension_semantics=("arbitrary",), has_side_effects=True),
        name="moe_finish_dispatch",
    )(zstart, zchunks, slots, h2s, xs)


FFN_IN_BUFFERS = 4
FFN_OUT_BUFFERS = 3


def _ffn_kernel(cnt_ref, xs_hbm, w13_hbm, w2_hbm, ys_hbm,
                xs_buf, ys_buf, w13_buf, w2_buf, t_row, t_exp, t_len, sem_in, sem_out, sem_w,
                *, cap_blocks):
    B = MOE_BLOCK

    def fill_expert(e, g):
        nb = (cnt_ref[e] + (B - 1)) // B

        def fill_block(j, carry):
            t_row[g + j] = e * cap_blocks + j
            t_exp[g + j] = e
            t_len[g + j] = nb
            return carry

        lax.fori_loop(0, nb, fill_block, 0)
        return g + nb

    n_used = lax.fori_loop(0, N_EXPERTS, fill_expert, 0)

    def in_copy(g):
        s = g % FFN_IN_BUFFERS
        return pltpu.make_async_copy(xs_hbm.at[pl.ds(t_row[g] * B, B)], xs_buf.at[s], sem_in.at[s])

    def out_copy(g):
        s = g % FFN_OUT_BUFFERS
        return pltpu.make_async_copy(ys_buf.at[s], ys_hbm.at[pl.ds(t_row[g] * B, B)], sem_out.at[s])

    def weight_copies(e, s):
        return (pltpu.make_async_copy(w13_hbm.at[e], w13_buf.at[s], sem_w.at[s]),
                pltpu.make_async_copy(w2_hbm.at[e], w2_buf.at[s], sem_w.at[s]))

    @pl.when(n_used > 0)
    def _():
        for cp in weight_copies(t_exp[0], 0):
            cp.start()

    for g0 in range(FFN_IN_BUFFERS - 1):
        @pl.when(g0 < n_used)
        def _(g0=g0):
            in_copy(g0).start()

    def block(g, wslot):
        first = (g == 0) | (t_exp[g] != t_exp[jnp.maximum(g - 1, 0)])
        wslot = jnp.where(first & (g > 0), 1 - wslot, wslot)

        @pl.when(first)
        def _():
            for cp in weight_copies(t_exp[g], wslot):
                cp.wait()
            nxt = g + t_len[g]

            @pl.when(nxt < n_used)
            def _():
                for cp in weight_copies(t_exp[jnp.minimum(nxt, n_used - 1)], 1 - wslot):
                    cp.start()

        in_copy(g).wait()

        @pl.when(g + FFN_IN_BUFFERS - 1 < n_used)
        def _():
            in_copy(g + FFN_IN_BUFFERS - 1).start()

        @pl.when(g >= FFN_OUT_BUFFERS)
        def _():
            out_copy(g - FFN_OUT_BUFFERS).wait()

        xb = _unpack_rows(xs_buf[g % FFN_IN_BUFFERS]).astype(_BF)
        ab = _dot(xb, w13_buf[wslot])
        hid = jax.nn.silu(ab[:, 0:D_EXPERT]) * ab[:, D_EXPERT:2 * D_EXPERT]
        ys_buf[g % FFN_OUT_BUFFERS] = _pack_rows(_dot(hid.astype(_BF), w2_buf[wslot]))
        out_copy(g).start()
        return wslot

    lax.fori_loop(0, n_used, block, 0)

    for back in range(FFN_OUT_BUFFERS, 0, -1):
        @pl.when(n_used >= back)
        def _(back=back):
            out_copy(n_used - back).wait()


def _ffn(xs, counts, w13, w2, cap_blocks, n_blocks):
    any_spec = pl.BlockSpec(memory_space=pl.ANY)
    row_buf = lambda n: pltpu.VMEM((n, MOE_BLOCK) + ROW_TILE, ROW_DTYPE)
    grid_spec = pltpu.PrefetchScalarGridSpec(
        num_scalar_prefetch=1,
        grid=(1,),
        in_specs=[any_spec, any_spec, any_spec],
        out_specs=any_spec,
        scratch_shapes=[row_buf(FFN_IN_BUFFERS), row_buf(FFN_OUT_BUFFERS),
                        pltpu.VMEM((2, D_MODEL, 2 * D_EXPERT), _BF), pltpu.VMEM((2, D_EXPERT, D_MODEL), _BF),
                        pltpu.SMEM((n_blocks,), jnp.int32), pltpu.SMEM((n_blocks,), jnp.int32),
                        pltpu.SMEM((n_blocks,), jnp.int32),
                        pltpu.SemaphoreType.DMA((FFN_IN_BUFFERS,)), pltpu.SemaphoreType.DMA((FFN_OUT_BUFFERS,)),
                        pltpu.SemaphoreType.DMA((2,))],
    )
    return pl.pallas_call(
        functools.partial(_ffn_kernel, cap_blocks=cap_blocks),
        grid_spec=grid_spec,
        out_shape=jax.ShapeDtypeStruct(xs.shape, ROW_DTYPE),
        compiler_params=pltpu.CompilerParams(dimension_semantics=("arbitrary",), vmem_limit_bytes=VMEM_LIMIT,
                                             has_side_effects=True),
        name="moe_ffn",
    )(counts, xs, w13, w2)


def _combine_kernel(slots_hbm, x1_ref, rw_ref, g_ref, ys_hbm, y_ref, idx_smem, buf, sem_idx, sem_rows):
    t = pl.program_id(0)
    TC = x1_ref.shape[0]
    idx_cp = pltpu.make_async_copy(slots_hbm.at[pl.ds(t * IDX_CHUNK, IDX_CHUNK)], idx_smem, sem_idx)
    idx_cp.start()
    idx_cp.wait()

    def issue(r, carry):
        for kk in range(2):
            pltpu.make_async_copy(ys_hbm.at[idx_smem[2 * r + kk]], buf.at[kk, r], sem_rows).start(priority=kk)
        return carry

    lax.fori_loop(0, TC, issue, 0, unroll=ISSUE_UNROLL)
    for kk in range(2):
        pltpu.make_async_copy(ys_hbm.at[pl.ds(0, TC)], buf.at[kk], sem_rows).wait()

    w = rw_ref[...]
    y0 = _unpack_rows(buf[0])
    y1 = _unpack_rows(buf[1])
    x2 = x1_ref[...] + (w[:, 0:1] * y0 + w[:, 1:2] * y1)
    y_ref[...] = _rms(x2, g_ref[...])


SC_GATHER_CHUNKS = 4
DENSE_TILE = 1024
SC_GATHER_BUFFERS = 3
SC_GATHER_MAX_WINDOW = 80


def _sc_gather(ys, rows):
    Tg = rows.shape[1]
    n_workers = SC_CORES * SC_SUBCORES
    per_w = Tg // n_workers
    assert per_w * n_workers == Tg and per_w % SUBLANES == 0
    W = max(w for w in range(SUBLANES, SC_GATHER_MAX_WINDOW + 1, SUBLANES) if per_w % w == 0)
    n_win = per_w // W
    NB = SC_GATHER_BUFFERS
    idx = rows.reshape(2, n_workers, n_win, W)
    jobs = [(kk, i) for i in range(n_win) for kk in range(2)]
    mesh = plsc.VectorSubcoreMesh(core_axis_name="c", subcore_axis_name="s")

    def body(ys_hbm, idx_hbm, out_hbm, i_v, rows_v, sem_g, sem_s):
        wid = lax.axis_index("s") * SC_CORES + lax.axis_index("c")
        base = wid * per_w
        for kk in range(2):
            pltpu.sync_copy(idx_hbm.at[kk, wid], i_v.at[kk])

        def gather(j):
            kk, i = jobs[j]
            return pltpu.make_async_copy(ys_hbm.at[i_v.at[kk, i]], rows_v.at[j % NB], sem_g.at[j % NB])

        def store(j):
            kk, i = jobs[j]
            return pltpu.make_async_copy(rows_v.at[j % NB], out_hbm.at[kk, pl.ds(base + i * W, W)], sem_s.at[j % NB])

        gather(0).start()
        for j in range(len(jobs)):
            gather(j).wait()
            store(j).start()
            if j + 1 < len(jobs):
                if j + 1 >= NB:
                    store(j + 1 - NB).wait()
                gather(j + 1).start()
        for j in range(max(0, len(jobs) - NB), len(jobs)):
            store(j).wait()

    return pl.kernel(
        body, mesh=mesh,
        out_type=jax.ShapeDtypeStruct((2, Tg) + ROW_TILE, ROW_DTYPE),
        scratch_types=[pltpu.VMEM((2, n_win, W), jnp.int32), pltpu.VMEM((NB, W) + ROW_TILE, ROW_DTYPE),
                       pltpu.SemaphoreType.DMA((NB,)), pltpu.SemaphoreType.DMA((NB,))],
        name="moe_sc_gather",
    )(ys, idx)


def _combine_dense_kernel(x1_ref, rw_ref, g_ref, rows_ref, *rest):
    y_ref = rest[-1]
    w = rw_ref[...]
    y0 = _unpack_rows(rows_ref[0])
    y1 = _unpack_rows(rows_ref[1])
    x2 = x1_ref[...] + (w[:, 0:1] * y0 + w[:, 1:2] * y1)
    y_ref[...] = _rms(x2, g_ref[...])


def _combine_dense(x1, rw, rows, y, g_final, tile, first_tile):
    Tg = rows.shape[1]
    assert Tg % tile == 0
    tok = lambda t: (first_tile + t, 0)
    in_specs = [pl.BlockSpec((tile, D_MODEL), tok), pl.BlockSpec((tile, LANES), tok), _full((1, D_MODEL)),
                pl.BlockSpec((2, tile) + ROW_TILE, lambda t: (0, t, 0, 0))]
    args = [x1, rw, g_final.reshape(1, D_MODEL), rows]
    aliases = {}
    if y is not None:
        in_specs.append(pl.BlockSpec(memory_space=pl.ANY))
        args.append(y)
        aliases = {4: 0}
    return pl.pallas_call(
        _combine_dense_kernel,
        grid=(Tg // tile,),
        in_specs=in_specs,
        out_specs=pl.BlockSpec((tile, D_MODEL), tok),
        out_shape=jax.ShapeDtypeStruct(x1.shape, _F32),
        input_output_aliases=aliases,
        compiler_params=pltpu.CompilerParams(dimension_semantics=("arbitrary",), vmem_limit_bytes=VMEM_LIMIT),
        name="moe_combine_dense",
    )(*args)


def _combine(x1, rw, slots, ys, g_final, tile, n_tiles):
    T = x1.shape[0]
    assert T % tile == 0 and 2 * tile <= IDX_CHUNK and slots.shape[0] == n_tiles * IDX_CHUNK
    any_spec = pl.BlockSpec(memory_space=pl.ANY)
    return pl.pallas_call(
        _combine_kernel,
        grid=(n_tiles,),
        in_specs=[any_spec, pl.BlockSpec((tile, D_MODEL), lambda t: (t, 0)),
                  pl.BlockSpec((tile, LANES), lambda t: (t, 0)), _full((1, D_MODEL)), any_spec],
        out_specs=pl.BlockSpec((tile, D_MODEL), lambda t: (t, 0)),
        out_shape=jax.ShapeDtypeStruct((T, D_MODEL), _F32),
        scratch_shapes=[pltpu.SMEM((IDX_CHUNK,), jnp.int32), pltpu.VMEM((2, tile) + ROW_TILE, ROW_DTYPE),
                        pltpu.SemaphoreType.DMA, pltpu.SemaphoreType.DMA],
        compiler_params=pltpu.CompilerParams(dimension_semantics=("arbitrary",), vmem_limit_bytes=VMEM_LIMIT),
        name="moe_combine",
    )(slots, x1, rw, g_final.reshape(1, D_MODEL), ys)


def _tile_for(n_tokens):
    tile = IDX_CHUNK // 2
    return tile if n_tokens % tile == 0 else n_tokens


def _chunked_slots(slot, tile):
    n_tiles = slot.shape[0] // tile
    s = slot.reshape(n_tiles, 2 * tile)
    return jnp.pad(s, ((0, 0), (0, IDX_CHUNK - 2 * tile))).reshape(-1)


def kernel(x_prompt, x_sample, state_pool, state_ret, g_mix, w_in, w_pool, pool_scale, ret_gn, w_out, g_ffn, w_grp, w_exp, w1, w3, w2, g_final):
    Bp, Lp, _ = x_prompt.shape
    Bs = x_sample.shape[0]
    Tp = Bp * Lp
    wts = _prep_weights(g_mix[0], w_pool[0], pool_scale[0], ret_gn[0], w_out[0], g_ffn[0], w_grp[0], w_exp[0])

    T_all = Tp + Bs
    cap = (-(-T_all // MOE_BLOCK) + 1) * MOE_BLOCK
    cap_blocks = cap // MOE_BLOCK

    xs_tok = x_sample.reshape(Bs, D_MODEL)
    wts["w_in"], u_s, q_s, k_s, kq_s, v_s, ga_s, gb_s = _sample_proj(xs_tok, wts["g_mix"], w_in[0], PAST_LEN)
    nret_s, q_state = _sc_state_update(state_ret[0].reshape(Bs * RET_HEADS, RET_DK, RET_DV), kq_s, v_s)
    x1p, dst_rows_p, rwp, npool_p, nret_p, counts_p, h2p, w13, w2b = _mixer_prompt(
        x_prompt, wts, w1[0], w3[0], w2[0], cap)
    xs = _sc_scatter(h2p, dst_rows_p, N_EXPERTS * cap)
    x1s, dst_s, rws, npool_s, counts, h2s = _mixer_sample(
        xs_tok, (u_s, q_s, k_s, v_s, ga_s, gb_s), jnp.swapaxes(state_pool[0], 0, 1), q_state, wts, counts_p, cap)

    counts = counts[0, :N_EXPERTS].astype(jnp.int32)
    tile_p, tile_s = _tile_for(Tp), _tile_for(Bs)
    slots_s = _chunked_slots(dst_s[:, :2], tile_s)
    zstart = jnp.arange(N_EXPERTS, dtype=jnp.int32) * cap + counts
    zchunks = ((-counts) % MOE_BLOCK + ZERO_CHUNK - 1) // ZERO_CHUNK
    xs = _finish_dispatch(xs, h2s, slots_s, zstart, zchunks.astype(jnp.int32))
    n_blocks = -(-2 * T_all // MOE_BLOCK) + N_EXPERTS
    ys = _ffn(xs, counts, w13, w2b, cap_blocks, n_blocks)
    tile_d = DENSE_TILE if Tp % DENSE_TILE == 0 else tile_p
    n_tiles_d = Tp // tile_d
    chunk_tiles = [n_tiles_d // SC_GATHER_CHUNKS + (k < n_tiles_d % SC_GATHER_CHUNKS)
                   for k in range(SC_GATHER_CHUNKS)]
    gathered, first = [], 0
    for nt in chunk_tiles:
        if nt:
            gathered.append((first, _sc_gather(ys, dst_rows_p[:, first * tile_d:(first + nt) * tile_d])))
            first += nt
    y_s = _combine(x1s, rws, slots_s, ys, g_final, tile_s, 1)
    y_p = None
    for first, rows in gathered:
        y_p = _combine_dense(x1p, rwp, rows, y_p, g_final, tile_d, first)

    return (y_p.reshape(Bp, Lp, D_MODEL), y_s.reshape(Bs, 1, D_MODEL),
            npool_p[None], nret_p[None], jnp.swapaxes(npool_s, 0, 1)[None], nret_s.reshape(state_ret.shape))
```

```python
import dataclasses
import functools

import jax
import jax.numpy as jnp
import numpy as np
from jax import lax
from jax.experimental import pallas as pl
from jax.experimental.pallas import tpu as pltpu
from jax.experimental.pallas import tpu_sc as plsc

D_MODEL = 1024
EPS = 1e-6
POOL_GROUPS = 4
POOL_IN = D_MODEL // 2
POOL_GC = POOL_IN // POOL_GROUPS
POOL_OUT_GC = D_MODEL // POOL_GROUPS
POOL_WINDOWS = (2, 4, 8, 16)
POOL_HIST = max(POOL_WINDOWS) - 1
RET_HEADS = 4
RET_DK = D_MODEL // 8
RET_DV = D_MODEL // RET_HEADS
ROPE_BASE = 10000.0
PAST_LEN = 16384
N_GROUPS = 4
EXPERTS_PER_GROUP = 8
N_EXPERTS = N_GROUPS * EXPERTS_PER_GROUP
D_EXPERT = D_MODEL // 4
QK_W = RET_HEADS * RET_DK
V_W = RET_HEADS * RET_DV
OFF_U = 0
OFF_Q = POOL_IN
OFF_K = OFF_Q + QK_W
OFF_V = OFF_K + QK_W
OFF_GA = OFF_V + V_W
OFF_GB = OFF_GA + D_MODEL
IN_WIDTH = OFF_GB + D_MODEL

LANES = 128
SUBLANES = 8
ROW_WORDS = D_MODEL // 2
ROW_TILE = (ROW_WORDS // LANES, LANES)
ROW_DTYPE = jnp.uint32
HALO = 16
RET_CHUNK = 256
MIXER_SEQS = 2
MOE_BLOCK = 512
ROUTER_W = LANES
RANK_GROUP = 128
VMEM_LIMIT = 56 * 1024 * 1024

_BF = jnp.bfloat16
_F32 = jnp.float32


def _rms(x, g):
    inv = lax.rsqrt(jnp.mean(x * x, axis=-1, keepdims=True) + EPS)
    return x * inv * g


def _dot(a, b):
    return jnp.dot(a, b, preferred_element_type=_F32)


def _round_bf16(x):
    return x.astype(_BF).astype(_F32)


def _pack_rows(x):
    lo = lax.bitcast_convert_type(x[:, :ROW_WORDS].astype(_BF).astype(_F32), jnp.uint32)
    hi = lax.bitcast_convert_type(x[:, ROW_WORDS:].astype(_BF).astype(_F32), jnp.uint32)
    return ((lo >> 16) | hi).reshape((x.shape[0],) + ROW_TILE)


def _unpack_rows(w):
    w = w.reshape(w.shape[0], ROW_WORDS)
    lo = lax.bitcast_convert_type(w << 16, _F32)
    hi = lax.bitcast_convert_type(w & jnp.uint32(0xFFFF0000), _F32)
    return jnp.concatenate([lo, hi], axis=1)


def _rotary(x, cos2, sin2):
    return x * cos2 + pltpu.roll(x, RET_DK // 2, 1) * sin2


def _route(logits):
    lane = lax.broadcasted_iota(jnp.int32, logits.shape, 1).astype(_F32)
    neg = jnp.float32(-jnp.inf)
    big = jnp.float32(1 << 20)
    lg = jnp.where(lane < N_GROUPS, logits, neg)
    mg = jnp.max(lg, axis=-1, keepdims=True)
    g_idx = jnp.min(jnp.where(lg == mg, lane, big), axis=-1, keepdims=True)
    p_g = 1.0 / jnp.sum(jnp.exp(lg - mg), axis=-1, keepdims=True)
    lo = N_GROUPS + g_idx * EXPERTS_PER_GROUP
    in_grp = (lane >= lo) & (lane < lo + EXPERTS_PER_GROUP)
    le = jnp.where(in_grp, logits, neg)
    m1 = jnp.max(le, axis=-1, keepdims=True)
    i1 = jnp.min(jnp.where(le == m1, lane, big), axis=-1, keepdims=True)
    le2 = jnp.where(lane == i1, neg, le)
    m2 = jnp.max(le2, axis=-1, keepdims=True)
    i2 = jnp.min(jnp.where(le2 == m2, lane, big), axis=-1, keepdims=True)
    t = jnp.exp(m2 - m1)
    den = 1.0 + t
    e0 = (i1 - N_GROUPS).astype(jnp.int32)
    e1 = (i2 - N_GROUPS).astype(jnp.int32)
    return e0, e1, p_g * (1.0 / den), p_g * (t / den)


def _post_mix(x, mixed_ref, w_out_ref, g_ffn_ref, w_rt_ref, x1_ref, rw_ref):
    x1 = x + _dot(mixed_ref[...], w_out_ref[...])
    x1_ref[...] = x1.reshape(x1_ref.shape)
    h2 = _rms(x1, g_ffn_ref[...])
    e0, e1, w0, w1 = _route(_dot(h2.astype(_BF), w_rt_ref[...]))
    lane = lax.broadcasted_iota(jnp.int32, (x.shape[0], LANES), 1)
    rw_ref[...] = jnp.where(lane == 0, w0, jnp.where(lane == 1, w1, 0.0)).reshape(rw_ref.shape)
    return h2, e0, e1


def _route_block(x1, g_ffn_ref, w_rt_ref, carry_scr, cap, valid, h2_ref, rw_ref, dst_t_ref):
    for _ in _route_stages(x1, g_ffn_ref, w_rt_ref, carry_scr, cap, valid, h2_ref, rw_ref, dst_t_ref):
        pass


def _route_stages(x1, g_ffn_ref, w_rt_ref, carry_scr, cap, valid, h2_ref, rw_ref, dst_t_ref):
    h2 = _rms(x1, g_ffn_ref[...])
    logits = _dot(h2.astype(_BF), w_rt_ref[...])
    h2_ref[...] = _pack_rows(h2).reshape(h2_ref.shape)
    yield
    e0, e1, w0, w1 = _route(logits)
    lane = lax.broadcasted_iota(jnp.int32, (x1.shape[0], LANES), 1)
    rw_ref[...] = jnp.where(lane == 0, w0, jnp.where(lane == 1, w1, 0.0)).reshape(rw_ref.shape)
    yield
    dst = _rank_rows(e0, e1, cap, carry_scr, valid)
    dst_t = dst.T[0:SUBLANES, :].astype(jnp.int32)
    n_parts, _, width = dst_t_ref.shape
    for part in range(n_parts):
        dst_t_ref[part] = dst_t[:, part * width:(part + 1) * width]
    yield


def _rank_rows(e0, e1, cap, carry_scr, valid=None):
    R = e0.shape[0]
    lane = lax.broadcasted_iota(jnp.int32, (R, LANES), 1)
    m0 = lane == e0
    m1 = lane == e1
    onehot = jnp.where(m0 | m1, 1.0, 0.0)
    G = min(RANK_GROUP, R)
    r_i = lax.broadcasted_iota(jnp.int32, (G, G), 0)
    c_i = lax.broadcasted_iota(jnp.int32, (G, G), 1)
    tri = jnp.where(c_i < r_i, 1.0, 0.0).astype(_BF)
    running = carry_scr[...]
    parts = []
    for g in range(R // G):
        grp = onehot[g * G:(g + 1) * G]
        parts.append(_dot(tri, grp.astype(_BF)) + running)
        running = running + jnp.sum(grp, axis=0, keepdims=True)
    before = jnp.concatenate(parts, axis=0)
    d0 = e0.astype(_F32) * cap + jnp.sum(jnp.where(m0, before, 0.0), axis=-1, keepdims=True)
    d1 = e1.astype(_F32) * cap + jnp.sum(jnp.where(m1, before, 0.0), axis=-1, keepdims=True)
    added = running - carry_scr[...]
    carry_scr[...] += added if valid is None else added * valid
    return jnp.where(lane == 0, d0, jnp.where(lane == 1, d1, 0.0))


def _group_norm(o, gain):
    mu = jnp.mean(o, axis=-1, keepdims=True)
    d = o - mu
    var = jnp.mean(d * d, axis=-1, keepdims=True)
    return d * lax.rsqrt(var + EPS) * gain


def _mixer_prompt_kernel(x_ref, g_mix_ref, w_in_ref, w_pool_ref, pscale_ref, gn_ref, w_out_ref, g_ffn_ref,
                         w_rt_ref, cos_ref, sin_ref, dmask_ref, xi_ref, zeta_ref, gc_ref,
                         x1_ref, dst_t_ref, rw_ref, npool_ref, nret_ref, cnt_ref, h2_ref,
                         u_scr, s_scr, mixed_scr, carry_scr, z_scr, x1_prev_scr, *, cap):
    c = pl.program_id(1)
    n_c = pl.num_programs(1)
    step = pl.program_id(0) * n_c + c
    NSEQ, C, _ = x_ref.shape
    R = NSEQ * C

    @pl.when(step == 0)
    def _():
        carry_scr[...] = jnp.zeros(carry_scr.shape, _F32)
        x1_prev_scr[...] = jnp.zeros(x1_prev_scr.shape, _F32)

    @pl.when(c == 0)
    def _():
        u_scr[:, 0:HALO, :] = jnp.zeros((NSEQ, HALO, POOL_IN), _F32)
        s_scr[...] = jnp.zeros(s_scr.shape, _F32)

    x = x_ref[...].reshape(R, D_MODEL)
    h = _rms(x, g_mix_ref[...]).astype(_BF)

    stages = _route_stages(x1_prev_scr[...], g_ffn_ref, w_rt_ref, carry_scr, cap, (step > 0).astype(_F32),
                           h2_ref, rw_ref, dst_t_ref)
    for lo, hi in ((0, OFF_V), (OFF_V, OFF_GA), (OFF_GA, OFF_GB), (OFF_GB, IN_WIDTH)):
        z_scr[:, lo:hi] = _dot(h, w_in_ref[:, lo:hi])
        next(stages, None)
    cnt_ref[...] = carry_scr[...]

    u_scr[:, HALO:HALO + C, :] = z_scr[:, OFF_U:OFF_U + POOL_IN].reshape(NSEQ, C, POOL_IN)
    q = z_scr[:, OFF_Q:OFF_Q + QK_W]
    k = z_scr[:, OFF_K:OFF_K + QK_W]
    cos2 = jnp.concatenate([cos_ref[...]] * NSEQ, axis=0)
    sin2 = jnp.concatenate([sin_ref[...]] * NSEQ, axis=0)
    pos1 = (c * C + 1 + lax.broadcasted_iota(jnp.int32, (C, POOL_GC), 0)).astype(_F32)

    for j in range(RET_HEADS):
        win = POOL_WINDOWS[j]
        cs = slice(j * POOL_GC, (j + 1) * POOL_GC)
        n_rows = jnp.minimum(pos1, jnp.float32(win))
        pooled = []
        for sq in range(NSEQ):
            u_j = u_scr[sq, HALO:HALO + C, cs]
            s = u_j
            for d in range(1, win):
                s = s + u_scr[sq, HALO - d:HALO - d + C, cs]
            pooled.append(s / n_rows - u_j)
        osl = slice(j * POOL_OUT_GC, (j + 1) * POOL_OUT_GC)
        pool_out = _dot(jnp.concatenate(pooled, axis=0).astype(_BF), w_pool_ref[j]) * pscale_ref[:, osl]

        qs = slice(j * RET_DK, (j + 1) * RET_DK)
        qb = _rotary(q[:, qs], cos2, sin2).astype(_BF)
        kf = _rotary(k[:, qs], cos2, sin2) * (RET_DK ** -0.5)
        kb = kf.astype(_BF)
        vb = z_scr[:, OFF_V + j * RET_DV:OFF_V + (j + 1) * RET_DV].astype(_BF)
        ret = []
        for sq in range(NSEQ):
            rows = slice(sq * C, (sq + 1) * C)
            scores = lax.dot_general(qb[rows], kb[rows], (((1,), (1,)), ((), ())),
                                     preferred_element_type=_F32) * dmask_ref[j]
            s_old = s_scr[sq, j]
            o = _dot(scores.astype(_BF), vb[rows]) + _dot(qb[rows], s_old.astype(_BF)) * xi_ref[j]
            kz = (kf[rows] * zeta_ref[j]).astype(_BF)
            s_scr[sq, j] = gc_ref[j] * s_old + lax.dot_general(kz, vb[rows], (((0,), (0,)), ((), ())),
                                                               preferred_element_type=_F32)
            ret.append(_group_norm(o, gn_ref[:, osl]))
        ret_out = jnp.concatenate(ret, axis=0)

        ga = z_scr[:, OFF_GA + j * RET_DV:OFF_GA + (j + 1) * RET_DV]
        gb = z_scr[:, OFF_GB + j * RET_DV:OFF_GB + (j + 1) * RET_DV]
        mixed_scr[:, osl] = (jax.nn.sigmoid(ga) * pool_out + jax.nn.sigmoid(gb) * ret_out).astype(_BF)

    x1 = x + _dot(mixed_scr[...], w_out_ref[...])
    x1_ref[...] = x1.reshape(x1_ref.shape)
    x1_prev_scr[...] = x1

    u_scr[:, 0:HALO, :] = u_scr[:, C:C + HALO, :]

    @pl.when(c == n_c - 1)
    def _():
        npool_ref[...] = u_scr[:, HALO + C - POOL_HIST:HALO + C, :]
        nret_ref[...] = s_scr[...]


def _decay_tables(C):
    f32 = np.float32
    log_g = np.log(f32(1.0) - np.exp2(f32(-5.0) - np.arange(RET_HEADS, dtype=f32)))
    i = np.arange(C, dtype=f32)
    diff = i[:, None] - i[None, :]
    dmask = np.where(diff >= 0, np.exp(np.maximum(diff, f32(0.0))[None] * log_g[:, None, None]), f32(0.0))
    xi = np.exp((i[None, :] + f32(1.0)) * log_g[:, None])
    zeta = np.exp((f32(C) - f32(1.0) - i)[None, :] * log_g[:, None])
    g_chunk = np.exp(f32(C) * log_g)
    return dmask.astype(f32), xi.astype(f32), zeta.astype(f32), g_chunk.astype(f32)


def _rope_tables(pos):
    half = RET_DK // 2
    freqs = ROPE_BASE ** (-jnp.arange(half, dtype=_F32) / half)
    ang = pos[:, None] * freqs[None, :]
    cos, sin = jnp.cos(ang), jnp.sin(ang)
    return jnp.concatenate([cos, cos], axis=-1), jnp.concatenate([-sin, sin], axis=-1)


def _const(a, shape=None):
    if shape is not None:
        a = np.ascontiguousarray(np.broadcast_to(a, shape))
    return jnp.asarray(a)


def _full(shape):
    n = len(shape)
    return pl.BlockSpec(shape, lambda *_: (0,) * n)


def _route_tail_kernel(x1_ref, g_ffn_ref, w_rt_ref, cnt_in_ref, dst_t_in, rw_in, h2_in,
                       dst_t_ref, rw_ref, cnt_ref, h2_ref, carry_scr, *, cap):
    del dst_t_in, rw_in, h2_in
    carry_scr[...] = cnt_in_ref[...]
    x1 = x1_ref[...]
    _route_block(x1.reshape(x1.shape[0] * x1.shape[1], D_MODEL), g_ffn_ref, w_rt_ref, carry_scr, cap, None,
                 h2_ref, rw_ref, dst_t_ref)
    cnt_ref[...] = carry_scr[...]


def _route_tail(x1, dst_t, rw, counts, h2, wts, cap, C, last):
    NSEQ = x1.shape[0]
    any_spec = pl.BlockSpec(memory_space=pl.ANY)
    return pl.pallas_call(
        functools.partial(_route_tail_kernel, cap=cap),
        grid=(1,),
        in_specs=[pl.BlockSpec((NSEQ, C, D_MODEL), lambda i: (0, last, 0)), _full((1, D_MODEL)),
                  _full((D_MODEL, ROUTER_W)), _full((1, LANES)), any_spec, any_spec, any_spec],
        out_specs=(pl.BlockSpec((NSEQ, SUBLANES, C), lambda i: (0, 0, last)),
                   pl.BlockSpec((NSEQ, C, LANES), lambda i: (0, last, 0)),
                   _full((1, LANES)),
                   pl.BlockSpec((NSEQ, C) + ROW_TILE, lambda i: (0, last, 0, 0))),
        out_shape=(jax.ShapeDtypeStruct(dst_t.shape, dst_t.dtype), jax.ShapeDtypeStruct(rw.shape, rw.dtype),
                   jax.ShapeDtypeStruct((1, LANES), _F32), jax.ShapeDtypeStruct(h2.shape, h2.dtype)),
        input_output_aliases={4: 0, 5: 1, 6: 3},
        scratch_shapes=[pltpu.VMEM((1, LANES), _F32)],
        compiler_params=pltpu.CompilerParams(dimension_semantics=("arbitrary",), vmem_limit_bytes=VMEM_LIMIT),
        name="mixer_route_tail",
    )(x1, wts["g_ffn"], wts["w_rt"], counts, dst_t, rw, h2)


def _mixer_prompt(x, wts, cap):
    B, L, _ = x.shape
    C = RET_CHUNK if L % RET_CHUNK == 0 else L
    n_c = L // C
    T = B * L
    dmask, xi, zeta, g_chunk = _decay_tables(C)
    dmask = _const(dmask)
    xi_b = _const(xi[:, :, None], (RET_HEADS, C, RET_DV))
    zeta_b = _const(zeta[:, :, None], (RET_HEADS, C, RET_DK))
    gc_b = _const(g_chunk[:, None, None], (RET_HEADS, 1, RET_DV))
    cos2, sin2 = _rope_tables(jnp.arange(L).astype(_F32))

    NSEQ = MIXER_SEQS if B % MIXER_SEQS == 0 else 1
    B2 = B // NSEQ
    T2 = T // NSEQ
    x = x.reshape(NSEQ, B2, L, D_MODEL)
    tok = lambda b, c: (0, b * n_c + c, 0)
    in_specs = [
        pl.BlockSpec((NSEQ, None, C, D_MODEL), lambda b, c: (0, b, c, 0)),
        _full((1, D_MODEL)), _full((D_MODEL, IN_WIDTH)), _full((POOL_GROUPS, POOL_GC, POOL_OUT_GC)),
        _full((1, D_MODEL)), _full((1, D_MODEL)), _full((D_MODEL, D_MODEL)), _full((1, D_MODEL)),
        _full((D_MODEL, ROUTER_W)),
        pl.BlockSpec((C, RET_DK), lambda b, c: (c, 0)), pl.BlockSpec((C, RET_DK), lambda b, c: (c, 0)),
        _full((RET_HEADS, C, C)), _full((RET_HEADS, C, RET_DV)), _full((RET_HEADS, C, RET_DK)),
        _full((RET_HEADS, 1, RET_DV)),
    ]
    R = NSEQ * C
    out_shape = (
        jax.ShapeDtypeStruct((NSEQ, T2, D_MODEL), _F32),
        jax.ShapeDtypeStruct((NSEQ, SUBLANES, T2), jnp.int32),
        jax.ShapeDtypeStruct((NSEQ, T2, LANES), _F32),
        jax.ShapeDtypeStruct((NSEQ, B2, POOL_HIST, POOL_IN), _F32),
        jax.ShapeDtypeStruct((NSEQ, B2, RET_HEADS, RET_DK, RET_DV), _F32),
        jax.ShapeDtypeStruct((1, LANES), _F32),
        jax.ShapeDtypeStruct((NSEQ, T2) + ROW_TILE, ROW_DTYPE),
    )
    prev = lambda b, c: jnp.maximum(b * n_c + c - 1, 0)
    out_specs = (
        pl.BlockSpec((NSEQ, C, D_MODEL), tok),
        pl.BlockSpec((NSEQ, SUBLANES, C), lambda b, c: (0, 0, prev(b, c))),
        pl.BlockSpec((NSEQ, C, LANES), lambda b, c: (0, prev(b, c), 0)),
        pl.BlockSpec((NSEQ, None, POOL_HIST, POOL_IN), lambda b, c: (0, b, 0, 0)),
        pl.BlockSpec((NSEQ, None, RET_HEADS, RET_DK, RET_DV), lambda b, c: (0, b, 0, 0, 0)),
        _full((1, LANES)),
        pl.BlockSpec((NSEQ, C) + ROW_TILE, lambda b, c: (0, prev(b, c), 0, 0)),
    )
    x1, dst_t, rw, npool, nret, counts, h2 = pl.pallas_call(
        functools.partial(_mixer_prompt_kernel, cap=cap),
        grid=(B2, n_c),
        in_specs=in_specs,
        out_specs=out_specs,
        out_shape=out_shape,
        scratch_shapes=[pltpu.VMEM((NSEQ, HALO + C, POOL_IN), _F32),
                        pltpu.VMEM((NSEQ, RET_HEADS, RET_DK, RET_DV), _F32),
                        pltpu.VMEM((R, D_MODEL), _BF),
                        pltpu.VMEM((1, LANES), _F32),
                        pltpu.VMEM((R, IN_WIDTH), _F32),
                        pltpu.VMEM((R, D_MODEL), _F32)],
        compiler_params=pltpu.CompilerParams(dimension_semantics=("arbitrary", "arbitrary"),
                                             vmem_limit_bytes=VMEM_LIMIT),
        name="mixer_prompt",
    )(x, wts["g_mix"], wts["w_in"], wts["w_pool"], wts["pool_scale"], wts["ret_gn"], wts["w_out"],
      wts["g_ffn"], wts["w_rt"], cos2, sin2, dmask, xi_b, zeta_b, gc_b)
    dst_t, rw, counts, h2 = _route_tail(x1, dst_t, rw, counts, h2, wts, cap, C, B2 * n_c - 1)
    dst_rows = jnp.moveaxis(dst_t[:, 0:2, :], 1, 0).reshape(2, T)
    return (x1.reshape(T, D_MODEL), dst_rows, rw.reshape(T, LANES),
            npool.reshape(B, POOL_HIST, POOL_IN), nret.reshape(B, RET_HEADS, RET_DK, RET_DV), counts,
            h2.reshape((T,) + ROW_TILE))


def _prep_weights(g_mix, w_pool, pool_scale, ret_gn, w_out, g_ffn, w_grp, w_exp):
    w_rt = jnp.concatenate([w_grp, w_exp.reshape(D_MODEL, N_EXPERTS)], axis=1)
    w_rt = jnp.pad(w_rt, ((0, 0), (0, ROUTER_W - w_rt.shape[1])))
    row = lambda v: v.reshape(1, D_MODEL)
    return dict(g_mix=row(g_mix), w_pool=w_pool.astype(_BF), pool_scale=row(pool_scale),
                ret_gn=row(ret_gn), w_out=w_out.astype(_BF), g_ffn=row(g_ffn), w_rt=w_rt.astype(_BF))


PROJ_COLS = 512


def _sample_proj_kernel(x_ref, g_mix_ref, w_in_ref, cos_ref, sin_ref, zeta_ref,
                        w_bf_ref, u_ref, q_ref, k_ref, kq_ref, v_ref, ga_ref, gb_ref, h_scr, z_scr):
    c = pl.program_id(0)

    @pl.when(c == 0)
    def _():
        h_scr[...] = _rms(x_ref[...], g_mix_ref[...]).astype(_BF)

    w = w_in_ref[...].astype(_BF)
    w_bf_ref[...] = w
    z_scr[c] = _dot(h_scr[...], w)

    @pl.when(c == pl.num_programs(0) - 1)
    def _():
        def cols(off, width):
            parts = [z_scr[i] for i in range(off // PROJ_COLS, (off + width) // PROJ_COLS)]
            return parts[0] if len(parts) == 1 else jnp.concatenate(parts, axis=1)

        u_ref[...] = cols(OFF_U, POOL_IN)
        q = cols(OFF_Q, QK_W)
        k = cols(OFF_K, QK_W)
        for j in range(RET_HEADS):
            qs = slice(j * RET_DK, (j + 1) * RET_DK)
            qf = _rotary(q[:, qs], cos_ref[...], sin_ref[...])
            kf = _rotary(k[:, qs], cos_ref[...], sin_ref[...]) * (RET_DK ** -0.5)
            q_ref[:, qs] = qf
            k_ref[:, qs] = kf
            kq_ref[0, :, qs] = kf * zeta_ref[j]
            kq_ref[1, :, qs] = _round_bf16(qf)
        v_ref[...] = cols(OFF_V, V_W)
        ga_ref[...] = cols(OFF_GA, D_MODEL)
        gb_ref[...] = cols(OFF_GB, D_MODEL)


def _sample_proj(x, g_mix, w_in, pos0):
    Bs = x.shape[0]
    offsets = (OFF_U, OFF_Q, OFF_K, OFF_V, OFF_GA, OFF_GB, IN_WIDTH)
    assert all(o % PROJ_COLS == 0 for o in offsets)
    n_c = IN_WIDTH // PROJ_COLS
    _, _, zeta, _ = _decay_tables(1)
    cos2, sin2 = _rope_tables((pos0 + jnp.arange(1)).astype(_F32))
    shapes = ((Bs, POOL_IN), (Bs, QK_W), (Bs, QK_W), (2, Bs, QK_W), (Bs, V_W), (Bs, D_MODEL), (Bs, D_MODEL))
    return pl.pallas_call(
        _sample_proj_kernel,
        grid=(n_c,),
        in_specs=[_full((Bs, D_MODEL)), _full((1, D_MODEL)), pl.BlockSpec((D_MODEL, PROJ_COLS), lambda c: (0, c)),
                  _full((1, RET_DK)), _full((1, RET_DK)), _full((RET_HEADS, 1, 1))],
        out_specs=(pl.BlockSpec((D_MODEL, PROJ_COLS), lambda c: (0, c)),) + tuple(_full(sh) for sh in shapes),
        out_shape=(jax.ShapeDtypeStruct((D_MODEL, IN_WIDTH), _BF),)
        + tuple(jax.ShapeDtypeStruct(sh, _F32) for sh in shapes),
        scratch_shapes=[pltpu.VMEM((Bs, D_MODEL), _BF), pltpu.VMEM((n_c, Bs, PROJ_COLS), _F32)],
        compiler_params=pltpu.CompilerParams(dimension_semantics=("arbitrary",), vmem_limit_bytes=VMEM_LIMIT),
        name="sample_proj",
    )(x, g_mix, w_in, cos2, sin2, _const(zeta[:, :, None], (RET_HEADS, 1, 1)))


def _mixer_sample_kernel(x_ref, u_ref, q_ref, k_ref, v_ref, ga_ref, gb_ref, spool_ref, qs_ref,
                         w_pool_ref, pscale_ref, gn_ref, w_out_ref, g_ffn_ref, w_rt_ref, dm_ref, xi_ref,
                         carry_in_ref,
                         x1_ref, dst_ref, rw_ref, npool_ref, cnt_ref, h2_ref,
                         mixed_scr, carry_scr, *, cap):
    u = u_ref[...]
    npool_ref[0:POOL_HIST - 1] = spool_ref[1:POOL_HIST]
    npool_ref[POOL_HIST - 1] = u
    for j in range(RET_HEADS):
        cs = slice(j * POOL_GC, (j + 1) * POOL_GC)
        qs = slice(j * RET_DK, (j + 1) * RET_DK)
        osl = slice(j * POOL_OUT_GC, (j + 1) * POOL_OUT_GC)
        win = POOL_WINDOWS[j]
        s = u[:, cs]
        for r in range(POOL_HIST - (win - 1), POOL_HIST):
            s = s + spool_ref[r, :, cs]
        pooled = s / jnp.float32(win) - u[:, cs]
        pool_out = _dot(pooled.astype(_BF), w_pool_ref[j]) * pscale_ref[:, osl]
        score = jnp.sum(q_ref[:, qs] * k_ref[:, qs], axis=-1, keepdims=True) * dm_ref[j]
        q_state = jnp.concatenate([qs_ref[hf, pl.ds(j, u.shape[0], stride=RET_HEADS), :]
                                   for hf in range(qs_ref.shape[0])], axis=1)
        o = score * v_ref[:, osl] + q_state * xi_ref[j]
        ret_out = _group_norm(o, gn_ref[:, osl])
        mixed_scr[:, osl] = (jax.nn.sigmoid(ga_ref[:, osl]) * pool_out
                             + jax.nn.sigmoid(gb_ref[:, osl]) * ret_out).astype(_BF)
    h2, e0, e1 = _post_mix(x_ref[...], mixed_scr, w_out_ref, g_ffn_ref, w_rt_ref, x1_ref, rw_ref)
    h2_ref[...] = _pack_rows(h2).reshape(h2_ref.shape)
    carry_scr[...] = carry_in_ref[...]
    dst_ref[...] = _rank_rows(e0, e1, cap, carry_scr).astype(jnp.int32)
    cnt_ref[...] = carry_scr[...]


def _mixer_sample(x, proj, state_pool, q_state, wts, carry_in, cap):
    Bs = x.shape[0]
    assert Bs == LANES and POOL_GROUPS == RET_HEADS
    u, q, k, v, ga, gb = proj
    dmask, xi, _, _ = _decay_tables(1)
    dm_b = _const(dmask, (RET_HEADS, 1, 1))
    xi_b = _const(xi[:, :, None], (RET_HEADS, 1, RET_DV))

    in_specs = [
        _full((Bs, D_MODEL)),
        _full((Bs, POOL_IN)), _full((Bs, QK_W)), _full((Bs, QK_W)), _full((Bs, V_W)),
        _full((Bs, D_MODEL)), _full((Bs, D_MODEL)),
        _full((POOL_HIST, Bs, POOL_IN)), _full(q_state.shape),
        _full((POOL_GROUPS, POOL_GC, POOL_OUT_GC)),
        _full((1, D_MODEL)), _full((1, D_MODEL)), _full((D_MODEL, D_MODEL)), _full((1, D_MODEL)),
        _full((D_MODEL, ROUTER_W)),
        _full((RET_HEADS, 1, 1)), _full((RET_HEADS, 1, RET_DV)),
        _full((1, LANES)),
    ]
    out_shape = (
        jax.ShapeDtypeStruct((Bs, D_MODEL), _F32),
        jax.ShapeDtypeStruct((Bs, LANES), jnp.int32),
        jax.ShapeDtypeStruct((Bs, LANES), _F32),
        jax.ShapeDtypeStruct((POOL_HIST, Bs, POOL_IN), _F32),
        jax.ShapeDtypeStruct((1, LANES), _F32),
        jax.ShapeDtypeStruct((Bs,) + ROW_TILE, ROW_DTYPE),
    )
    out_specs = (
        _full((Bs, D_MODEL)), _full((Bs, LANES)), _full((Bs, LANES)), _full((POOL_HIST, Bs, POOL_IN)),
        _full((1, LANES)), _full((Bs,) + ROW_TILE),
    )
    return pl.pallas_call(
        functools.partial(_mixer_sample_kernel, cap=cap),
        grid=(1,),
        in_specs=in_specs,
        out_specs=out_specs,
        out_shape=out_shape,
        scratch_shapes=[pltpu.VMEM((Bs, D_MODEL), _BF), pltpu.VMEM((1, LANES), _F32)],
        compiler_params=pltpu.CompilerParams(dimension_semantics=("arbitrary",), vmem_limit_bytes=VMEM_LIMIT),
        name="mixer_sample",
    )(x, u, q, k, v, ga, gb, state_pool, q_state, wts["w_pool"], wts["pool_scale"], wts["ret_gn"],
      wts["w_out"], wts["g_ffn"], wts["w_rt"], dm_b, xi_b, carry_in)


SC_CORES = 2
SC_SUBCORES = 16
SC_LANES = 16
STATE_ROWS = 32
STATE_UNROLL = 8
STATE_COLS = 8


def _sc_round_bf16(x):
    b = lax.bitcast_convert_type(x, jnp.uint32)
    r = b + jnp.uint32(0x7FFF) + ((b >> 16) & jnp.uint32(1))
    return lax.bitcast_convert_type(r & jnp.uint32(0xFFFF0000), _F32)


def _sc_state_update(s0, kq, v):
    P = s0.shape[0]
    n_workers = SC_CORES * SC_SUBCORES
    R = STATE_ROWS
    n_parts = RET_DK // R
    n_vc = RET_DV // SC_LANES
    assert P % (n_workers * RET_HEADS) == 0 and RET_DK % R == 0 and n_parts % 2 == 0 and n_parts >= 2
    assert R % STATE_UNROLL == 0 and n_vc % STATE_COLS == 0
    per_w = P // n_workers
    seqs_w = per_w // RET_HEADS
    _, _, _, g_chunk = _decay_tables(1)
    decay = _const(g_chunk[:, None], (RET_HEADS, SC_LANES))
    mesh = plsc.VectorSubcoreMesh(core_axis_name="c", subcore_axis_name="s")

    def body(s0_hbm, kq_hbm, v_hbm, g_hbm, out_hbm, qs_hbm, in_v, out_v, k_v, q_v, v_v, g_v, o_v, sem_in, sem_out):
        wid = lax.axis_index("s") * SC_CORES + lax.axis_index("c")
        base = wid * per_w
        seqs = pl.ds(wid * seqs_w, seqs_w)
        pltpu.sync_copy(kq_hbm.at[0, seqs], k_v)
        pltpu.sync_copy(kq_hbm.at[1, seqs], q_v)
        pltpu.sync_copy(v_hbm.at[seqs], v_v)
        pltpu.sync_copy(g_hbm, g_v)

        def load(p, part, slot):
            return pltpu.make_async_copy(s0_hbm.at[p, pl.ds(part * R, R)], in_v.at[slot], sem_in.at[slot])

        def store(p, part, slot):
            return pltpu.make_async_copy(out_v.at[slot], out_hbm.at[p, pl.ds(part * R, R)], sem_out.at[slot])

        def o_at(pp, c):
            return (c * SC_LANES // LANES, pp, pl.ds(c * SC_LANES % LANES, SC_LANES))

        load(base, 0, 0).start()

        def pair(pp, carry):
            p = base + pp
            seq, head = pp // RET_HEADS, pp % RET_HEADS
            g = g_v[head, :]
            sidx = jnp.full((SC_LANES,), seq, jnp.int32)
            for c in range(n_vc):
                o_v[o_at(pp, c)] = jnp.zeros((SC_LANES,), _F32)
            for part in range(n_parts):
                slot = part % 2
                load(p, part, slot).wait()
                if part + 1 < n_parts:
                    load(p, part + 1, 1 - slot).start()
                else:
                    @pl.when(pp + 1 < per_w)
                    def _():
                        load(p + 1, 0, 1 - slot).start()

                if part < 2:
                    @pl.when(pp > 0)
                    def _():
                        store(p, part, slot).wait()
                else:
                    store(p, part, slot).wait()

                def rows(i8, c2):
                    for c0 in range(0, n_vc, STATE_COLS):
                        cols = [pl.ds((c0 + c) * SC_LANES, SC_LANES) for c in range(STATE_COLS)]
                        vs = [v_v[seq, pl.ds(head * RET_DV + (c0 + c) * SC_LANES, SC_LANES)]
                              for c in range(STATE_COLS)]
                        acc = [None] * STATE_COLS
                        for j in range(STATE_UNROLL):
                            i = i8 * STATE_UNROLL + j
                            ridx = jnp.full((SC_LANES,), head * RET_DK + part * R + i, jnp.int32)
                            ki = plsc.load_gather(k_v, [sidx, ridx])
                            qi = plsc.load_gather(q_v, [sidx, ridx])
                            ss = [in_v[slot, i, cs] for cs in cols]
                            for c, cs in enumerate(cols):
                                out_v[slot, i, cs] = g * ss[c] + ki * vs[c]
                            for c in range(STATE_COLS):
                                t = qi * _sc_round_bf16(ss[c])
                                acc[c] = t if acc[c] is None else acc[c] + t
                        for c in range(STATE_COLS):
                            o_v[o_at(pp, c0 + c)] = o_v[o_at(pp, c0 + c)] + acc[c]
                    return c2

                lax.fori_loop(0, R // STATE_UNROLL, rows, 0)
                store(p, part, slot).start()
            return carry

        lax.fori_loop(0, per_w, pair, 0)
        for slot in range(2):
            store(base, slot, slot).wait()
        for hf in range(RET_DV // LANES):
            pltpu.sync_copy(o_v.at[hf], qs_hbm.at[hf, pl.ds(base, per_w)])

    keys = pltpu.VMEM((seqs_w, RET_HEADS * RET_DK), _F32)
    return pl.kernel(
        body, mesh=mesh,
        out_type=(jax.ShapeDtypeStruct(s0.shape, _F32), jax.ShapeDtypeStruct((RET_DV // LANES, P, LANES), _F32)),
        scratch_types=[pltpu.VMEM((2, R, RET_DV), _F32), pltpu.VMEM((2, R, RET_DV), _F32), keys, keys,
                       pltpu.VMEM((seqs_w, RET_HEADS * RET_DV), _F32), pltpu.VMEM((RET_HEADS, SC_LANES), _F32),
                       pltpu.VMEM((RET_DV // LANES, per_w, LANES), _F32),
                       pltpu.SemaphoreType.DMA((2,)), pltpu.SemaphoreType.DMA((2,))],
        compiler_params=dataclasses.replace(pltpu.CompilerParams(), needs_layout_passes=False),
        name="sample_state_update",
    )(s0, kq, v, decay)


IDX_CHUNK = 1024
ISSUE_UNROLL = 8
SC_WINDOW = 64
SC_SCATTER_BUFFERS = 3


def _sc_scatter(h2, dst_rows, n_rows):
    T = h2.shape[0]
    n_workers = SC_CORES * SC_SUBCORES
    W = SC_WINDOW
    assert T % (n_workers * W) == 0
    per_w = T // n_workers
    n_win = per_w // W
    d0 = dst_rows[0].reshape(T // W, W)
    d1 = dst_rows[1].reshape(T // W, W)
    mesh = plsc.VectorSubcoreMesh(core_axis_name="c", subcore_axis_name="s")

    def body(h2_hbm, d0_hbm, d1_hbm, xs_hbm, i0_v, i1_v, rows_v, sem_load, sem_store):
        wid = lax.axis_index("s") * SC_CORES + lax.axis_index("c")
        base = wid * per_w
        pltpu.sync_copy(d0_hbm.at[pl.ds(wid * n_win, n_win)], i0_v)
        pltpu.sync_copy(d1_hbm.at[pl.ds(wid * n_win, n_win)], i1_v)
        NB = SC_SCATTER_BUFFERS
        pltpu.async_copy(h2_hbm.at[pl.ds(base, W)], rows_v.at[0], sem_load)
        stores = {}
        for i in range(n_win):
            b = i % NB
            pltpu.make_async_copy(h2_hbm.at[pl.ds(base, W)], rows_v.at[b], sem_load).wait()
            for cp in stores.pop(i + 1 - NB, ()):
                cp.wait()
            if i + 1 < n_win:
                pltpu.async_copy(h2_hbm.at[pl.ds(base + (i + 1) * W, W)], rows_v.at[(i + 1) % NB], sem_load)
            stores[i] = (pltpu.async_copy(rows_v.at[b], xs_hbm.at[i0_v.at[i]], sem_store.at[b]),
                         pltpu.async_copy(rows_v.at[b], xs_hbm.at[i1_v.at[i]], sem_store.at[b]))
        for i in sorted(stores):
            for cp in stores[i]:
                cp.wait()

    return pl.kernel(
        body, mesh=mesh,
        out_type=jax.ShapeDtypeStruct((n_rows,) + ROW_TILE, ROW_DTYPE),
        scratch_types=[pltpu.VMEM((n_win, W), jnp.int32), pltpu.VMEM((n_win, W), jnp.int32),
                       pltpu.VMEM((SC_SCATTER_BUFFERS, W) + ROW_TILE, ROW_DTYPE),
                       pltpu.SemaphoreType.DMA, pltpu.SemaphoreType.DMA((SC_SCATTER_BUFFERS,))],
        name="moe_sc_scatter",
    )(h2, d0, d1)


ZERO_CHUNK = 64


def _finish_dispatch_kernel(zstart_ref, zchunks_ref, slots_hbm, h2_ref, xs_in_hbm, xs_hbm, idx_smem, zbuf, sem_idx,
                            sem_rows, sem_zero):
    del xs_in_hbm
    TS = h2_ref.shape[0]
    idx_cp = pltpu.make_async_copy(slots_hbm.at[pl.ds(0, IDX_CHUNK)], idx_smem, sem_idx)
    idx_cp.start()
    zbuf[...] = jnp.zeros(zbuf.shape, zbuf.dtype)

    def zero_copy(e, j):
        return pltpu.make_async_copy(zbuf, xs_hbm.at[pl.ds(zstart_ref[e] + j * ZERO_CHUNK, ZERO_CHUNK)], sem_zero)

    for e in range(N_EXPERTS):
        lax.fori_loop(0, zchunks_ref[e], lambda j, c, e=e: (zero_copy(e, j).start(), c)[1], 0)
    idx_cp.wait()

    def issue(r, carry):
        for kk in range(2):
            pltpu.make_async_copy(h2_ref.at[r], xs_hbm.at[idx_smem[2 * r + kk]], sem_rows).start(priority=kk)
        return carry

    lax.fori_loop(0, TS, issue, 0, unroll=ISSUE_UNROLL)
    for e in range(N_EXPERTS):
        lax.fori_loop(0, zchunks_ref[e], lambda j, c, e=e: (zero_copy(e, j).wait(), c)[1], 0)
    for kk in range(2):
        pltpu.make_async_copy(h2_ref, xs_hbm.at[pl.ds(0, TS)], sem_rows).wait()


def _finish_dispatch(xs, h2s, slots, zstart, zchunks):
    Bs = h2s.shape[0]
    assert 2 * Bs <= IDX_CHUNK and slots.shape[0] == IDX_CHUNK and MOE_BLOCK % ZERO_CHUNK == 0
    any_spec = pl.BlockSpec(memory_space=pl.ANY)
    grid_spec = pltpu.PrefetchScalarGridSpec(
        num_scalar_prefetch=2,
        grid=(1,),
        in_specs=[any_spec, pl.BlockSpec((Bs,) + ROW_TILE, lambda t, z, n: (0, 0, 0)), any_spec],
        out_specs=any_spec,
        scratch_shapes=[pltpu.SMEM((IDX_CHUNK,), jnp.int32), pltpu.VMEM((ZERO_CHUNK,) + ROW_TILE, ROW_DTYPE),
                        pltpu.SemaphoreType.DMA, pltpu.SemaphoreType.DMA, pltpu.SemaphoreType.DMA],
    )
    return pl.pallas_call(
        _finish_dispatch_kernel,
        grid_spec=grid_spec,
        out_shape=jax.ShapeDtypeStruct(xs.shape, ROW_DTYPE),
        input_output_aliases={4: 0},
        compiler_params=pltpu.CompilerParams(dimension_semantics=("arbitrary",), has_side_effects=True),
        name="moe_finish_dispatch",
    )(zstart, zchunks, slots, h2s, xs)


FFN_IN_BUFFERS = 4
FFN_OUT_BUFFERS = 3


def _ffn_kernel(cnt_ref, xs_hbm, w1_hbm, w3_hbm, w2_hbm, ys_hbm,
                xs_buf, ys_buf, st1, st3, st2, w13_scr, w2_scr, t_row, t_exp, t_len, sem_in, sem_out, sem_w,
                *, cap_blocks):
    B = MOE_BLOCK

    def fill_expert(e, g):
        nb = (cnt_ref[e] + (B - 1)) // B

        def fill_block(j, carry):
            t_row[g + j] = e * cap_blocks + j
            t_exp[g + j] = e
            t_len[g + j] = nb
            return carry

        lax.fori_loop(0, nb, fill_block, 0)
        return g + nb

    n_used = lax.fori_loop(0, N_EXPERTS, fill_expert, 0)

    def in_copy(g):
        s = g % FFN_IN_BUFFERS
        return pltpu.make_async_copy(xs_hbm.at[pl.ds(t_row[g] * B, B)], xs_buf.at[s], sem_in.at[s])

    def out_copy(g):
        s = g % FFN_OUT_BUFFERS
        return pltpu.make_async_copy(ys_buf.at[s], ys_hbm.at[pl.ds(t_row[g] * B, B)], sem_out.at[s])

    def weight_copies(e, s):
        return (pltpu.make_async_copy(w1_hbm.at[e], st1.at[s], sem_w.at[s]),
                pltpu.make_async_copy(w3_hbm.at[e], st3.at[s], sem_w.at[s]),
                pltpu.make_async_copy(w2_hbm.at[e], st2.at[s], sem_w.at[s]))

    @pl.when(n_used > 0)
    def _():
        for cp in weight_copies(t_exp[0], 0):
            cp.start()

    for g0 in range(FFN_IN_BUFFERS - 1):
        @pl.when(g0 < n_used)
        def _(g0=g0):
            in_copy(g0).start()

    def block(g, wslot):
        first = (g == 0) | (t_exp[g] != t_exp[jnp.maximum(g - 1, 0)])
        wslot = jnp.where(first & (g > 0), 1 - wslot, wslot)

        @pl.when(first)
        def _():
            for cp in weight_copies(t_exp[g], wslot):
                cp.wait()
            nxt = g + t_len[g]

            @pl.when(nxt < n_used)
            def _():
                for cp in weight_copies(t_exp[jnp.minimum(nxt, n_used - 1)], 1 - wslot):
                    cp.start()

            w13_scr[:, 0:D_EXPERT] = st1[wslot].astype(_BF)
            w13_scr[:, D_EXPERT:2 * D_EXPERT] = st3[wslot].astype(_BF)
            w2_scr[...] = st2[wslot].astype(_BF)

        in_copy(g).wait()

        @pl.when(g + FFN_IN_BUFFERS - 1 < n_used)
        def _():
            in_copy(g + FFN_IN_BUFFERS - 1).start()

        @pl.when(g >= FFN_OUT_BUFFERS)
        def _():
            out_copy(g - FFN_OUT_BUFFERS).wait()

        xb = _unpack_rows(xs_buf[g % FFN_IN_BUFFERS]).astype(_BF)
        ab = _dot(xb, w13_scr[...])
        hid = jax.nn.silu(ab[:, 0:D_EXPERT]) * ab[:, D_EXPERT:2 * D_EXPERT]
        ys_buf[g % FFN_OUT_BUFFERS] = _pack_rows(_dot(hid.astype(_BF), w2_scr[...]))
        out_copy(g).start()
        return wslot

    lax.fori_loop(0, n_used, block, 0)

    for back in range(FFN_OUT_BUFFERS, 0, -1):
        @pl.when(n_used >= back)
        def _(back=back):
            out_copy(n_used - back).wait()


def _ffn(xs, counts, w1, w3, w2, cap_blocks, n_blocks):
    any_spec = pl.BlockSpec(memory_space=pl.ANY)
    row_buf = lambda n: pltpu.VMEM((n, MOE_BLOCK) + ROW_TILE, ROW_DTYPE)
    grid_spec = pltpu.PrefetchScalarGridSpec(
        num_scalar_prefetch=1,
        grid=(1,),
        in_specs=[any_spec, any_spec, any_spec, any_spec],
        out_specs=any_spec,
        scratch_shapes=[row_buf(FFN_IN_BUFFERS), row_buf(FFN_OUT_BUFFERS),
                        pltpu.VMEM((2, D_MODEL, D_EXPERT), _F32), pltpu.VMEM((2, D_MODEL, D_EXPERT), _F32),
                        pltpu.VMEM((2, D_EXPERT, D_MODEL), _F32),
                        pltpu.VMEM((D_MODEL, 2 * D_EXPERT), _BF), pltpu.VMEM((D_EXPERT, D_MODEL), _BF),
                        pltpu.SMEM((n_blocks,), jnp.int32), pltpu.SMEM((n_blocks,), jnp.int32),
                        pltpu.SMEM((n_blocks,), jnp.int32),
                        pltpu.SemaphoreType.DMA((FFN_IN_BUFFERS,)), pltpu.SemaphoreType.DMA((FFN_OUT_BUFFERS,)),
                        pltpu.SemaphoreType.DMA((2,))],
    )
    return pl.pallas_call(
        functools.partial(_ffn_kernel, cap_blocks=cap_blocks),
        grid_spec=grid_spec,
        out_shape=jax.ShapeDtypeStruct(xs.shape, ROW_DTYPE),
        compiler_params=pltpu.CompilerParams(dimension_semantics=("arbitrary",), vmem_limit_bytes=VMEM_LIMIT,
                                             has_side_effects=True),
        name="moe_ffn",
    )(counts, xs, w1, w3, w2)


def _combine_kernel(slots_hbm, x1_ref, rw_ref, g_ref, ys_hbm, y_ref, idx_smem, buf, sem_idx, sem_rows):
    t = pl.program_id(0)
    TC = x1_ref.shape[0]
    idx_cp = pltpu.make_async_copy(slots_hbm.at[pl.ds(t * IDX_CHUNK, IDX_CHUNK)], idx_smem, sem_idx)
    idx_cp.start()
    idx_cp.wait()

    def issue(r, carry):
        for kk in range(2):
            pltpu.make_async_copy(ys_hbm.at[idx_smem[2 * r + kk]], buf.at[kk, r], sem_rows).start(priority=kk)
        return carry

    lax.fori_loop(0, TC, issue, 0, unroll=ISSUE_UNROLL)
    for kk in range(2):
        pltpu.make_async_copy(ys_hbm.at[pl.ds(0, TC)], buf.at[kk], sem_rows).wait()

    w = rw_ref[...]
    y0 = _unpack_rows(buf[0])
    y1 = _unpack_rows(buf[1])
    x2 = x1_ref[...] + (w[:, 0:1] * y0 + w[:, 1:2] * y1)
    y_ref[...] = _rms(x2, g_ref[...])


SC_GATHER_CHUNKS = 4
DENSE_TILE = 1024
SC_GATHER_BUFFERS = 3
SC_GATHER_MAX_WINDOW = 80


def _sc_gather(ys, rows):
    Tg = rows.shape[1]
    n_workers = SC_CORES * SC_SUBCORES
    per_w = Tg // n_workers
    assert per_w * n_workers == Tg and per_w % SUBLANES == 0
    W = max(w for w in range(SUBLANES, SC_GATHER_MAX_WINDOW + 1, SUBLANES) if per_w % w == 0)
    n_win = per_w // W
    NB = SC_GATHER_BUFFERS
    idx = rows.reshape(2, n_workers, n_win, W)
    jobs = [(kk, i) for i in range(n_win) for kk in range(2)]
    mesh = plsc.VectorSubcoreMesh(core_axis_name="c", subcore_axis_name="s")

    def body(ys_hbm, idx_hbm, out_hbm, i_v, rows_v, sem_g, sem_s):
        wid = lax.axis_index("s") * SC_CORES + lax.axis_index("c")
        base = wid * per_w
        for kk in range(2):
            pltpu.sync_copy(idx_hbm.at[kk, wid], i_v.at[kk])

        def gather(j):
            kk, i = jobs[j]
            return pltpu.make_async_copy(ys_hbm.at[i_v.at[kk, i]], rows_v.at[j % NB], sem_g.at[j % NB])

        def store(j):
            kk, i = jobs[j]
            return pltpu.make_async_copy(rows_v.at[j % NB], out_hbm.at[kk, pl.ds(base + i * W, W)], sem_s.at[j % NB])

        gather(0).start()
        for j in range(len(jobs)):
            gather(j).wait()
            store(j).start()
            if j + 1 < len(jobs):
                if j + 1 >= NB:
                    store(j + 1 - NB).wait()
                gather(j + 1).start()
        for j in range(max(0, len(jobs) - NB), len(jobs)):
            store(j).wait()

    return pl.kernel(
        body, mesh=mesh,
        out_type=jax.ShapeDtypeStruct((2, Tg) + ROW_TILE, ROW_DTYPE),
        scratch_types=[pltpu.VMEM((2, n_win, W), jnp.int32), pltpu.VMEM((NB, W) + ROW_TILE, ROW_DTYPE),
                       pltpu.SemaphoreType.DMA((NB,)), pltpu.SemaphoreType.DMA((NB,))],
        name="moe_sc_gather",
    )(ys, idx)


def _combine_dense_kernel(x1_ref, rw_ref, g_ref, rows_ref, *rest):
    y_ref = rest[-1]
    w = rw_ref[...]
    y0 = _unpack_rows(rows_ref[0])
    y1 = _unpack_rows(rows_ref[1])
    x2 = x1_ref[...] + (w[:, 0:1] * y0 + w[:, 1:2] * y1)
    y_ref[...] = _rms(x2, g_ref[...])


def _combine_dense(x1, rw, rows, y, g_final, tile, first_tile):
    Tg = rows.shape[1]
    assert Tg % tile == 0
    tok = lambda t: (first_tile + t, 0)
    in_specs = [pl.BlockSpec((tile, D_MODEL), tok), pl.BlockSpec((tile, LANES), tok), _full((1, D_MODEL)),
                pl.BlockSpec((2, tile) + ROW_TILE, lambda t: (0, t, 0, 0))]
    args = [x1, rw, g_final.reshape(1, D_MODEL), rows]
    aliases = {}
    if y is not None:
        in_specs.append(pl.BlockSpec(memory_space=pl.ANY))
        args.append(y)
        aliases = {4: 0}
    return pl.pallas_call(
        _combine_dense_kernel,
        grid=(Tg // tile,),
        in_specs=in_specs,
        out_specs=pl.BlockSpec((tile, D_MODEL), tok),
        out_shape=jax.ShapeDtypeStruct(x1.shape, _F32),
        input_output_aliases=aliases,
        compiler_params=pltpu.CompilerParams(dimension_semantics=("arbitrary",), vmem_limit_bytes=VMEM_LIMIT),
        name="moe_combine_dense",
    )(*args)


def _combine(x1, rw, slots, ys, g_final, tile, n_tiles):
    T = x1.shape[0]
    assert T % tile == 0 and 2 * tile <= IDX_CHUNK and slots.shape[0] == n_tiles * IDX_CHUNK
    any_spec = pl.BlockSpec(memory_space=pl.ANY)
    return pl.pallas_call(
        _combine_kernel,
        grid=(n_tiles,),
        in_specs=[any_spec, pl.BlockSpec((tile, D_MODEL), lambda t: (t, 0)),
                  pl.BlockSpec((tile, LANES), lambda t: (t, 0)), _full((1, D_MODEL)), any_spec],
        out_specs=pl.BlockSpec((tile, D_MODEL), lambda t: (t, 0)),
        out_shape=jax.ShapeDtypeStruct((T, D_MODEL), _F32),
        scratch_shapes=[pltpu.SMEM((IDX_CHUNK,), jnp.int32), pltpu.VMEM((2, tile) + ROW_TILE, ROW_DTYPE),
                        pltpu.SemaphoreType.DMA, pltpu.SemaphoreType.DMA],
        compiler_params=pltpu.CompilerParams(dimension_semantics=("arbitrary",), vmem_limit_bytes=VMEM_LIMIT),
        name="moe_combine",
    )(slots, x1, rw, g_final.reshape(1, D_MODEL), ys)


def _tile_for(n_tokens):
    tile = IDX_CHUNK // 2
    return tile if n_tokens % tile == 0 else n_tokens


def _chunked_slots(slot, tile):
    n_tiles = slot.shape[0] // tile
    s = slot.reshape(n_tiles, 2 * tile)
    return jnp.pad(s, ((0, 0), (0, IDX_CHUNK - 2 * tile))).reshape(-1)


def kernel(x_prompt, x_sample, state_pool, state_ret, g_mix, w_in, w_pool, pool_scale, ret_gn, w_out, g_ffn, w_grp, w_exp, w1, w3, w2, g_final):
    Bp, Lp, _ = x_prompt.shape
    Bs = x_sample.shape[0]
    Tp = Bp * Lp
    wts = _prep_weights(g_mix[0], w_pool[0], pool_scale[0], ret_gn[0], w_out[0], g_ffn[0], w_grp[0], w_exp[0])

    T_all = Tp + Bs
    cap = (-(-T_all // MOE_BLOCK) + 1) * MOE_BLOCK
    cap_blocks = cap // MOE_BLOCK

    xs_tok = x_sample.reshape(Bs, D_MODEL)
    wts["w_in"], u_s, q_s, k_s, kq_s, v_s, ga_s, gb_s = _sample_proj(xs_tok, wts["g_mix"], w_in[0], PAST_LEN)
    nret_s, q_state = _sc_state_update(state_ret[0].reshape(Bs * RET_HEADS, RET_DK, RET_DV), kq_s, v_s)
    x1p, dst_rows_p, rwp, npool_p, nret_p, counts_p, h2p = _mixer_prompt(x_prompt, wts, cap)
    xs = _sc_scatter(h2p, dst_rows_p, N_EXPERTS * cap)
    x1s, dst_s, rws, npool_s, counts, h2s = _mixer_sample(
        xs_tok, (u_s, q_s, k_s, v_s, ga_s, gb_s), jnp.swapaxes(state_pool[0], 0, 1), q_state, wts, counts_p, cap)

    counts = counts[0, :N_EXPERTS].astype(jnp.int32)
    tile_p, tile_s = _tile_for(Tp), _tile_for(Bs)
    slots_s = _chunked_slots(dst_s[:, :2], tile_s)
    zstart = jnp.arange(N_EXPERTS, dtype=jnp.int32) * cap + counts
    zchunks = ((-counts) % MOE_BLOCK + ZERO_CHUNK - 1) // ZERO_CHUNK
    xs = _finish_dispatch(xs, h2s, slots_s, zstart, zchunks.astype(jnp.int32))
    n_blocks = -(-2 * T_all // MOE_BLOCK) + N_EXPERTS
    ys = _ffn(xs, counts, w1[0], w3[0], w2[0], cap_blocks, n_blocks)
    tile_d = DENSE_TILE if Tp % DENSE_TILE == 0 else tile_p
    n_tiles_d = Tp // tile_d
    chunk_tiles = [n_tiles_d // SC_GATHER_CHUNKS + (k < n_tiles_d % SC_GATHER_CHUNKS)
                   for k in range(SC_GATHER_CHUNKS)]
    gathered, first = [], 0
    for nt in chunk_tiles:
        if nt:
            gathered.append((first, _sc_gather(ys, dst_rows_p[:, first * tile_d:(first + nt) * tile_d])))
            first += nt
    y_s = _combine(x1s, rws, slots_s, ys, g_final, tile_s, 1)
    y_p = None
    for first, rows in gathered:
        y_p = _combine_dense(x1p, rwp, rows, y_p, g_final, tile_d, first)

    return (y_p.reshape(Bp, Lp, D_MODEL), y_s.reshape(Bs, 1, D_MODEL),
            npool_p[None], nret_p[None], jnp.swapaxes(npool_s, 0, 1)[None], nret_s.reshape(state_ret.shape))
```

```python
import dataclasses
import functools

import jax
import jax.numpy as jnp
import numpy as np
from jax import lax
from jax.experimental import pallas as pl
from jax.experimental.pallas import tpu as pltpu
from jax.experimental.pallas import tpu_sc as plsc

D_MODEL = 1024
EPS = 1e-6
POOL_GROUPS = 4
POOL_IN = D_MODEL // 2
POOL_GC = POOL_IN // POOL_GROUPS
POOL_OUT_GC = D_MODEL // POOL_GROUPS
POOL_WINDOWS = (2, 4, 8, 16)
POOL_HIST = max(POOL_WINDOWS) - 1
RET_HEADS = 4
RET_DK = D_MODEL // 8
RET_DV = D_MODEL // RET_HEADS
ROPE_BASE = 10000.0
PAST_LEN = 16384
N_GROUPS = 4
EXPERTS_PER_GROUP = 8
N_EXPERTS = N_GROUPS * EXPERTS_PER_GROUP
D_EXPERT = D_MODEL // 4
QK_W = RET_HEADS * RET_DK
V_W = RET_HEADS * RET_DV
OFF_U = 0
OFF_Q = POOL_IN
OFF_K = OFF_Q + QK_W
OFF_V = OFF_K + QK_W
OFF_GA = OFF_V + V_W
OFF_GB = OFF_GA + D_MODEL
IN_WIDTH = OFF_GB + D_MODEL

LANES = 128
SUBLANES = 8
ROW_WORDS = D_MODEL // 2
ROW_TILE = (ROW_WORDS // LANES, LANES)
ROW_DTYPE = jnp.uint32
HALO = 16
RET_CHUNK = 256
MIXER_SEQS = 2
MOE_BLOCK = 512
ROUTER_W = LANES
RANK_GROUP = 128
VMEM_LIMIT = 56 * 1024 * 1024

_BF = jnp.bfloat16
_F32 = jnp.float32


def _rms(x, g):
    inv = lax.rsqrt(jnp.mean(x * x, axis=-1, keepdims=True) + EPS)
    return x * inv * g


def _dot(a, b):
    return jnp.dot(a, b, preferred_element_type=_F32)


def _round_bf16(x):
    return x.astype(_BF).astype(_F32)


def _pack_rows(x):
    lo = lax.bitcast_convert_type(x[:, :ROW_WORDS].astype(_BF).astype(_F32), jnp.uint32)
    hi = lax.bitcast_convert_type(x[:, ROW_WORDS:].astype(_BF).astype(_F32), jnp.uint32)
    return ((lo >> 16) | hi).reshape((x.shape[0],) + ROW_TILE)


def _unpack_rows(w):
    w = w.reshape(w.shape[0], ROW_WORDS)
    lo = lax.bitcast_convert_type(w << 16, _F32)
    hi = lax.bitcast_convert_type(w & jnp.uint32(0xFFFF0000), _F32)
    return jnp.concatenate([lo, hi], axis=1)


def _rotary(x, cos2, sin2):
    return x * cos2 + pltpu.roll(x, RET_DK // 2, 1) * sin2


def _route(logits):
    lane = lax.broadcasted_iota(jnp.int32, logits.shape, 1).astype(_F32)
    neg = jnp.float32(-jnp.inf)
    big = jnp.float32(1 << 20)
    lg = jnp.where(lane < N_GROUPS, logits, neg)
    mg = jnp.max(lg, axis=-1, keepdims=True)
    g_idx = jnp.min(jnp.where(lg == mg, lane, big), axis=-1, keepdims=True)
    p_g = 1.0 / jnp.sum(jnp.exp(lg - mg), axis=-1, keepdims=True)
    lo = N_GROUPS + g_idx * EXPERTS_PER_GROUP
    in_grp = (lane >= lo) & (lane < lo + EXPERTS_PER_GROUP)
    le = jnp.where(in_grp, logits, neg)
    m1 = jnp.max(le, axis=-1, keepdims=True)
    i1 = jnp.min(jnp.where(le == m1, lane, big), axis=-1, keepdims=True)
    le2 = jnp.where(lane == i1, neg, le)
    m2 = jnp.max(le2, axis=-1, keepdims=True)
    i2 = jnp.min(jnp.where(le2 == m2, lane, big), axis=-1, keepdims=True)
    t = jnp.exp(m2 - m1)
    den = 1.0 + t
    e0 = (i1 - N_GROUPS).astype(jnp.int32)
    e1 = (i2 - N_GROUPS).astype(jnp.int32)
    return e0, e1, p_g * (1.0 / den), p_g * (t / den)


def _post_mix(x, mixed_ref, w_out_ref, g_ffn_ref, w_rt_ref, x1_ref, rw_ref):
    x1 = x + _dot(mixed_ref[...], w_out_ref[...])
    x1_ref[...] = x1.reshape(x1_ref.shape)
    h2 = _rms(x1, g_ffn_ref[...])
    e0, e1, w0, w1 = _route(_dot(h2.astype(_BF), w_rt_ref[...]))
    lane = lax.broadcasted_iota(jnp.int32, (x.shape[0], LANES), 1)
    rw_ref[...] = jnp.where(lane == 0, w0, jnp.where(lane == 1, w1, 0.0)).reshape(rw_ref.shape)
    return h2, e0, e1


def _route_block(x1, g_ffn_ref, w_rt_ref, carry_scr, cap, valid, h2_ref, rw_ref, dst_t_ref):
    for _ in _route_stages(x1, g_ffn_ref, w_rt_ref, carry_scr, cap, valid, h2_ref, rw_ref, dst_t_ref):
        pass


def _route_stages(x1, g_ffn_ref, w_rt_ref, carry_scr, cap, valid, h2_ref, rw_ref, dst_t_ref):
    h2 = _rms(x1, g_ffn_ref[...])
    logits = _dot(h2.astype(_BF), w_rt_ref[...])
    h2_ref[...] = _pack_rows(h2).reshape(h2_ref.shape)
    yield
    e0, e1, w0, w1 = _route(logits)
    lane = lax.broadcasted_iota(jnp.int32, (x1.shape[0], LANES), 1)
    rw_ref[...] = jnp.where(lane == 0, w0, jnp.where(lane == 1, w1, 0.0)).reshape(rw_ref.shape)
    yield
    dst = _rank_rows(e0, e1, cap, carry_scr, valid)
    dst_t = dst.T[0:SUBLANES, :].astype(jnp.int32)
    n_parts, _, width = dst_t_ref.shape
    for part in range(n_parts):
        dst_t_ref[part] = dst_t[:, part * width:(part + 1) * width]
    yield


def _rank_rows(e0, e1, cap, carry_scr, valid=None):
    R = e0.shape[0]
    lane = lax.broadcasted_iota(jnp.int32, (R, LANES), 1)
    m0 = lane == e0
    m1 = lane == e1
    onehot = jnp.where(m0 | m1, 1.0, 0.0)
    G = min(RANK_GROUP, R)
    r_i = lax.broadcasted_iota(jnp.int32, (G, G), 0)
    c_i = lax.broadcasted_iota(jnp.int32, (G, G), 1)
    tri = jnp.where(c_i < r_i, 1.0, 0.0).astype(_BF)
    running = carry_scr[...]
    parts = []
    for g in range(R // G):
        grp = onehot[g * G:(g + 1) * G]
        parts.append(_dot(tri, grp.astype(_BF)) + running)
        running = running + jnp.sum(grp, axis=0, keepdims=True)
    before = jnp.concatenate(parts, axis=0)
    d0 = e0.astype(_F32) * cap + jnp.sum(jnp.where(m0, before, 0.0), axis=-1, keepdims=True)
    d1 = e1.astype(_F32) * cap + jnp.sum(jnp.where(m1, before, 0.0), axis=-1, keepdims=True)
    added = running - carry_scr[...]
    carry_scr[...] += added if valid is None else added * valid
    return jnp.where(lane == 0, d0, jnp.where(lane == 1, d1, 0.0))


def _group_norm(o, gain):
    mu = jnp.mean(o, axis=-1, keepdims=True)
    d = o - mu
    var = jnp.mean(d * d, axis=-1, keepdims=True)
    return d * lax.rsqrt(var + EPS) * gain


def _mixer_prompt_kernel(x_ref, g_mix_ref, w_in_ref, w_pool_ref, pscale_ref, gn_ref, w_out_ref, g_ffn_ref,
                         w_rt_ref, cos_ref, sin_ref, dmask_ref, xi_ref, zeta_ref, gc_ref,
                         x1_ref, dst_t_ref, rw_ref, npool_ref, nret_ref, cnt_ref, h2_ref,
                         u_scr, s_scr, mixed_scr, carry_scr, z_scr, x1_prev_scr, *, cap):
    c = pl.program_id(1)
    n_c = pl.num_programs(1)
    step = pl.program_id(0) * n_c + c
    NSEQ, C, _ = x_ref.shape
    R = NSEQ * C

    @pl.when(step == 0)
    def _():
        carry_scr[...] = jnp.zeros(carry_scr.shape, _F32)
        x1_prev_scr[...] = jnp.zeros(x1_prev_scr.shape, _F32)

    @pl.when(c == 0)
    def _():
        u_scr[:, 0:HALO, :] = jnp.zeros((NSEQ, HALO, POOL_IN), _F32)
        s_scr[...] = jnp.zeros(s_scr.shape, _F32)

    x = x_ref[...].reshape(R, D_MODEL)
    h = _rms(x, g_mix_ref[...]).astype(_BF)

    stages = _route_stages(x1_prev_scr[...], g_ffn_ref, w_rt_ref, carry_scr, cap, (step > 0).astype(_F32),
                           h2_ref, rw_ref, dst_t_ref)
    for lo, hi in ((0, OFF_V), (OFF_V, OFF_GA), (OFF_GA, OFF_GB), (OFF_GB, IN_WIDTH)):
        z_scr[:, lo:hi] = _dot(h, w_in_ref[:, lo:hi])
        next(stages, None)
    cnt_ref[...] = carry_scr[...]

    u_scr[:, HALO:HALO + C, :] = z_scr[:, OFF_U:OFF_U + POOL_IN].reshape(NSEQ, C, POOL_IN)
    q = z_scr[:, OFF_Q:OFF_Q + QK_W]
    k = z_scr[:, OFF_K:OFF_K + QK_W]
    cos2 = jnp.concatenate([cos_ref[...]] * NSEQ, axis=0)
    sin2 = jnp.concatenate([sin_ref[...]] * NSEQ, axis=0)
    pos1 = (c * C + 1 + lax.broadcasted_iota(jnp.int32, (C, POOL_GC), 0)).astype(_F32)

    for j in range(RET_HEADS):
        win = POOL_WINDOWS[j]
        cs = slice(j * POOL_GC, (j + 1) * POOL_GC)
        n_rows = jnp.minimum(pos1, jnp.float32(win))
        pooled = []
        for sq in range(NSEQ):
            u_j = u_scr[sq, HALO:HALO + C, cs]
            s = u_j
            for d in range(1, win):
                s = s + u_scr[sq, HALO - d:HALO - d + C, cs]
            pooled.append(s / n_rows - u_j)
        osl = slice(j * POOL_OUT_GC, (j + 1) * POOL_OUT_GC)
        pool_out = _dot(jnp.concatenate(pooled, axis=0).astype(_BF), w_pool_ref[j]) * pscale_ref[:, osl]

        qs = slice(j * RET_DK, (j + 1) * RET_DK)
        qb = _rotary(q[:, qs], cos2, sin2).astype(_BF)
        kf = _rotary(k[:, qs], cos2, sin2) * (RET_DK ** -0.5)
        kb = kf.astype(_BF)
        vb = z_scr[:, OFF_V + j * RET_DV:OFF_V + (j + 1) * RET_DV].astype(_BF)
        ret = []
        for sq in range(NSEQ):
            rows = slice(sq * C, (sq + 1) * C)
            scores = lax.dot_general(qb[rows], kb[rows], (((1,), (1,)), ((), ())),
                                     preferred_element_type=_F32) * dmask_ref[j]
            s_old = s_scr[sq, j]
            o = _dot(scores.astype(_BF), vb[rows]) + _dot(qb[rows], s_old.astype(_BF)) * xi_ref[j]
            kz = (kf[rows] * zeta_ref[j]).astype(_BF)
            s_scr[sq, j] = gc_ref[j] * s_old + lax.dot_general(kz, vb[rows], (((0,), (0,)), ((), ())),
                                                               preferred_element_type=_F32)
            ret.append(_group_norm(o, gn_ref[:, osl]))
        ret_out = jnp.concatenate(ret, axis=0)

        ga = z_scr[:, OFF_GA + j * RET_DV:OFF_GA + (j + 1) * RET_DV]
        gb = z_scr[:, OFF_GB + j * RET_DV:OFF_GB + (j + 1) * RET_DV]
        mixed_scr[:, osl] = (jax.nn.sigmoid(ga) * pool_out + jax.nn.sigmoid(gb) * ret_out).astype(_BF)

    x1 = x + _dot(mixed_scr[...], w_out_ref[...])
    x1_ref[...] = x1.reshape(x1_ref.shape)
    x1_prev_scr[...] = x1

    u_scr[:, 0:HALO, :] = u_scr[:, C:C + HALO, :]

    @pl.when(c == n_c - 1)
    def _():
        npool_ref[...] = u_scr[:, HALO + C - POOL_HIST:HALO + C, :]
        nret_ref[...] = s_scr[...]


def _decay_tables(C):
    f32 = np.float32
    log_g = np.log(f32(1.0) - np.exp2(f32(-5.0) - np.arange(RET_HEADS, dtype=f32)))
    i = np.arange(C, dtype=f32)
    diff = i[:, None] - i[None, :]
    dmask = np.where(diff >= 0, np.exp(np.maximum(diff, f32(0.0))[None] * log_g[:, None, None]), f32(0.0))
    xi = np.exp((i[None, :] + f32(1.0)) * log_g[:, None])
    zeta = np.exp((f32(C) - f32(1.0) - i)[None, :] * log_g[:, None])
    g_chunk = np.exp(f32(C) * log_g)
    return dmask.astype(f32), xi.astype(f32), zeta.astype(f32), g_chunk.astype(f32)


def _rope_tables(pos):
    half = RET_DK // 2
    freqs = ROPE_BASE ** (-jnp.arange(half, dtype=_F32) / half)
    ang = pos[:, None] * freqs[None, :]
    cos, sin = jnp.cos(ang), jnp.sin(ang)
    return jnp.concatenate([cos, cos], axis=-1), jnp.concatenate([-sin, sin], axis=-1)


def _const(a, shape=None):
    if shape is not None:
        a = np.ascontiguousarray(np.broadcast_to(a, shape))
    return jnp.asarray(a)


def _full(shape):
    n = len(shape)
    return pl.BlockSpec(shape, lambda *_: (0,) * n)


def _route_tail_kernel(x1_ref, g_ffn_ref, w_rt_ref, cnt_in_ref, dst_t_in, rw_in, h2_in,
                       dst_t_ref, rw_ref, cnt_ref, h2_ref, carry_scr, *, cap):
    del dst_t_in, rw_in, h2_in
    carry_scr[...] = cnt_in_ref[...]
    x1 = x1_ref[...]
    _route_block(x1.reshape(x1.shape[0] * x1.shape[1], D_MODEL), g_ffn_ref, w_rt_ref, carry_scr, cap, None,
                 h2_ref, rw_ref, dst_t_ref)
    cnt_ref[...] = carry_scr[...]


def _route_tail(x1, dst_t, rw, counts, h2, wts, cap, C, last):
    NSEQ = x1.shape[0]
    any_spec = pl.BlockSpec(memory_space=pl.ANY)
    return pl.pallas_call(
        functools.partial(_route_tail_kernel, cap=cap),
        grid=(1,),
        in_specs=[pl.BlockSpec((NSEQ, C, D_MODEL), lambda i: (0, last, 0)), _full((1, D_MODEL)),
                  _full((D_MODEL, ROUTER_W)), _full((1, LANES)), any_spec, any_spec, any_spec],
        out_specs=(pl.BlockSpec((NSEQ, SUBLANES, C), lambda i: (0, 0, last)),
                   pl.BlockSpec((NSEQ, C, LANES), lambda i: (0, last, 0)),
                   _full((1, LANES)),
                   pl.BlockSpec((NSEQ, C) + ROW_TILE, lambda i: (0, last, 0, 0))),
        out_shape=(jax.ShapeDtypeStruct(dst_t.shape, dst_t.dtype), jax.ShapeDtypeStruct(rw.shape, rw.dtype),
                   jax.ShapeDtypeStruct((1, LANES), _F32), jax.ShapeDtypeStruct(h2.shape, h2.dtype)),
        input_output_aliases={4: 0, 5: 1, 6: 3},
        scratch_shapes=[pltpu.VMEM((1, LANES), _F32)],
        compiler_params=pltpu.CompilerParams(dimension_semantics=("arbitrary",), vmem_limit_bytes=VMEM_LIMIT),
        name="mixer_route_tail",
    )(x1, wts["g_ffn"], wts["w_rt"], counts, dst_t, rw, h2)


def _mixer_prompt(x, wts, cap):
    B, L, _ = x.shape
    C = RET_CHUNK if L % RET_CHUNK == 0 else L
    n_c = L // C
    T = B * L
    dmask, xi, zeta, g_chunk = _decay_tables(C)
    dmask = _const(dmask)
    xi_b = _const(xi[:, :, None], (RET_HEADS, C, RET_DV))
    zeta_b = _const(zeta[:, :, None], (RET_HEADS, C, RET_DK))
    gc_b = _const(g_chunk[:, None, None], (RET_HEADS, 1, RET_DV))
    cos2, sin2 = _rope_tables(jnp.arange(L).astype(_F32))

    NSEQ = MIXER_SEQS if B % MIXER_SEQS == 0 else 1
    B2 = B // NSEQ
    T2 = T // NSEQ
    x = x.reshape(NSEQ, B2, L, D_MODEL)
    tok = lambda b, c: (0, b * n_c + c, 0)
    in_specs = [
        pl.BlockSpec((NSEQ, None, C, D_MODEL), lambda b, c: (0, b, c, 0)),
        _full((1, D_MODEL)), _full((D_MODEL, IN_WIDTH)), _full((POOL_GROUPS, POOL_GC, POOL_OUT_GC)),
        _full((1, D_MODEL)), _full((1, D_MODEL)), _full((D_MODEL, D_MODEL)), _full((1, D_MODEL)),
        _full((D_MODEL, ROUTER_W)),
        pl.BlockSpec((C, RET_DK), lambda b, c: (c, 0)), pl.BlockSpec((C, RET_DK), lambda b, c: (c, 0)),
        _full((RET_HEADS, C, C)), _full((RET_HEADS, C, RET_DV)), _full((RET_HEADS, C, RET_DK)),
        _full((RET_HEADS, 1, RET_DV)),
    ]
    R = NSEQ * C
    out_shape = (
        jax.ShapeDtypeStruct((NSEQ, T2, D_MODEL), _F32),
        jax.ShapeDtypeStruct((NSEQ, SUBLANES, T2), jnp.int32),
        jax.ShapeDtypeStruct((NSEQ, T2, LANES), _F32),
        jax.ShapeDtypeStruct((NSEQ, B2, POOL_HIST, POOL_IN), _F32),
        jax.ShapeDtypeStruct((NSEQ, B2, RET_HEADS, RET_DK, RET_DV), _F32),
        jax.ShapeDtypeStruct((1, LANES), _F32),
        jax.ShapeDtypeStruct((NSEQ, T2) + ROW_TILE, ROW_DTYPE),
    )
    prev = lambda b, c: jnp.maximum(b * n_c + c - 1, 0)
    out_specs = (
        pl.BlockSpec((NSEQ, C, D_MODEL), tok),
        pl.BlockSpec((NSEQ, SUBLANES, C), lambda b, c: (0, 0, prev(b, c))),
        pl.BlockSpec((NSEQ, C, LANES), lambda b, c: (0, prev(b, c), 0)),
        pl.BlockSpec((NSEQ, None, POOL_HIST, POOL_IN), lambda b, c: (0, b, 0, 0)),
        pl.BlockSpec((NSEQ, None, RET_HEADS, RET_DK, RET_DV), lambda b, c: (0, b, 0, 0, 0)),
        _full((1, LANES)),
        pl.BlockSpec((NSEQ, C) + ROW_TILE, lambda b, c: (0, prev(b, c), 0, 0)),
    )
    x1, dst_t, rw, npool, nret, counts, h2 = pl.pallas_call(
        functools.partial(_mixer_prompt_kernel, cap=cap),
        grid=(B2, n_c),
        in_specs=in_specs,
        out_specs=out_specs,
        out_shape=out_shape,
        scratch_shapes=[pltpu.VMEM((NSEQ, HALO + C, POOL_IN), _F32),
                        pltpu.VMEM((NSEQ, RET_HEADS, RET_DK, RET_DV), _F32),
                        pltpu.VMEM((R, D_MODEL), _BF),
                        pltpu.VMEM((1, LANES), _F32),
                        pltpu.VMEM((R, IN_WIDTH), _F32),
                        pltpu.VMEM((R, D_MODEL), _F32)],
        compiler_params=pltpu.CompilerParams(dimension_semantics=("arbitrary", "arbitrary"),
                                             vmem_limit_bytes=VMEM_LIMIT),
        name="mixer_prompt",
    )(x, wts["g_mix"], wts["w_in"], wts["w_pool"], wts["pool_scale"], wts["ret_gn"], wts["w_out"],
      wts["g_ffn"], wts["w_rt"], cos2, sin2, dmask, xi_b, zeta_b, gc_b)
    dst_t, rw, counts, h2 = _route_tail(x1, dst_t, rw, counts, h2, wts, cap, C, B2 * n_c - 1)
    dst_rows = jnp.moveaxis(dst_t[:, 0:2, :], 1, 0).reshape(2, T)
    return (x1.reshape(T, D_MODEL), dst_rows, rw.reshape(T, LANES),
            npool.reshape(B, POOL_HIST, POOL_IN), nret.reshape(B, RET_HEADS, RET_DK, RET_DV), counts,
            h2.reshape((T,) + ROW_TILE))


def _prep_weights(g_mix, w_pool, pool_scale, ret_gn, w_out, g_ffn, w_grp, w_exp):
    w_rt = jnp.concatenate([w_grp, w_exp.reshape(D_MODEL, N_EXPERTS)], axis=1)
    w_rt = jnp.pad(w_rt, ((0, 0), (0, ROUTER_W - w_rt.shape[1])))
    row = lambda v: v.reshape(1, D_MODEL)
    return dict(g_mix=row(g_mix), w_pool=w_pool.astype(_BF), pool_scale=row(pool_scale),
                ret_gn=row(ret_gn), w_out=w_out.astype(_BF), g_ffn=row(g_ffn), w_rt=w_rt.astype(_BF))


PROJ_COLS = 512


def _sample_proj_kernel(x_ref, g_mix_ref, w_in_ref, cos_ref, sin_ref, zeta_ref,
                        w_bf_ref, u_ref, q_ref, k_ref, kq_ref, v_ref, ga_ref, gb_ref, h_scr, z_scr):
    c = pl.program_id(0)

    @pl.when(c == 0)
    def _():
        h_scr[...] = _rms(x_ref[...], g_mix_ref[...]).astype(_BF)

    w = w_in_ref[...].astype(_BF)
    w_bf_ref[...] = w
    z_scr[c] = _dot(h_scr[...], w)

    @pl.when(c == pl.num_programs(0) - 1)
    def _():
        def cols(off, width):
            parts = [z_scr[i] for i in range(off // PROJ_COLS, (off + width) // PROJ_COLS)]
            return parts[0] if len(parts) == 1 else jnp.concatenate(parts, axis=1)

        u_ref[...] = cols(OFF_U, POOL_IN)
        q = cols(OFF_Q, QK_W)
        k = cols(OFF_K, QK_W)
        for j in range(RET_HEADS):
            qs = slice(j * RET_DK, (j + 1) * RET_DK)
            qf = _rotary(q[:, qs], cos_ref[...], sin_ref[...])
            kf = _rotary(k[:, qs], cos_ref[...], sin_ref[...]) * (RET_DK ** -0.5)
            q_ref[:, qs] = qf
            k_ref[:, qs] = kf
            kq_ref[0, :, qs] = kf * zeta_ref[j]
            kq_ref[1, :, qs] = _round_bf16(qf)
        v_ref[...] = cols(OFF_V, V_W)
        ga_ref[...] = cols(OFF_GA, D_MODEL)
        gb_ref[...] = cols(OFF_GB, D_MODEL)


def _sample_proj(x, g_mix, w_in, pos0):
    Bs = x.shape[0]
    offsets = (OFF_U, OFF_Q, OFF_K, OFF_V, OFF_GA, OFF_GB, IN_WIDTH)
    assert all(o % PROJ_COLS == 0 for o in offsets)
    n_c = IN_WIDTH // PROJ_COLS
    _, _, zeta, _ = _decay_tables(1)
    cos2, sin2 = _rope_tables((pos0 + jnp.arange(1)).astype(_F32))
    shapes = ((Bs, POOL_IN), (Bs, QK_W), (Bs, QK_W), (2, Bs, QK_W), (Bs, V_W), (Bs, D_MODEL), (Bs, D_MODEL))
    return pl.pallas_call(
        _sample_proj_kernel,
        grid=(n_c,),
        in_specs=[_full((Bs, D_MODEL)), _full((1, D_MODEL)), pl.BlockSpec((D_MODEL, PROJ_COLS), lambda c: (0, c)),
                  _full((1, RET_DK)), _full((1, RET_DK)), _full((RET_HEADS, 1, 1))],
        out_specs=(pl.BlockSpec((D_MODEL, PROJ_COLS), lambda c: (0, c)),) + tuple(_full(sh) for sh in shapes),
        out_shape=(jax.ShapeDtypeStruct((D_MODEL, IN_WIDTH), _BF),)
        + tuple(jax.ShapeDtypeStruct(sh, _F32) for sh in shapes),
        scratch_shapes=[pltpu.VMEM((Bs, D_MODEL), _BF), pltpu.VMEM((n_c, Bs, PROJ_COLS), _F32)],
        compiler_params=pltpu.CompilerParams(dimension_semantics=("arbitrary",), vmem_limit_bytes=VMEM_LIMIT),
        name="sample_proj",
    )(x, g_mix, w_in, cos2, sin2, _const(zeta[:, :, None], (RET_HEADS, 1, 1)))


def _mixer_sample_kernel(x_ref, u_ref, q_ref, k_ref, v_ref, ga_ref, gb_ref, spool_ref, qs_ref,
                         w_pool_ref, pscale_ref, gn_ref, w_out_ref, g_ffn_ref, w_rt_ref, dm_ref, xi_ref,
                         carry_in_ref,
                         x1_ref, dst_ref, rw_ref, npool_ref, cnt_ref, h2_ref,
                         mixed_scr, carry_scr, *, cap):
    u = u_ref[...]
    npool_ref[0:POOL_HIST - 1] = spool_ref[1:POOL_HIST]
    npool_ref[POOL_HIST - 1] = u
    for j in range(RET_HEADS):
        cs = slice(j * POOL_GC, (j + 1) * POOL_GC)
        qs = slice(j * RET_DK, (j + 1) * RET_DK)
        osl = slice(j * POOL_OUT_GC, (j + 1) * POOL_OUT_GC)
        win = POOL_WINDOWS[j]
        s = u[:, cs]
        for r in range(POOL_HIST - (win - 1), POOL_HIST):
            s = s + spool_ref[r, :, cs]
        pooled = s / jnp.float32(win) - u[:, cs]
        pool_out = _dot(pooled.astype(_BF), w_pool_ref[j]) * pscale_ref[:, osl]
        score = jnp.sum(q_ref[:, qs] * k_ref[:, qs], axis=-1, keepdims=True) * dm_ref[j]
        q_state = jnp.concatenate([qs_ref[hf, pl.ds(j, u.shape[0], stride=RET_HEADS), :]
                                   for hf in range(qs_ref.shape[0])], axis=1)
        o = score * v_ref[:, osl] + q_state * xi_ref[j]
        ret_out = _group_norm(o, gn_ref[:, osl])
        mixed_scr[:, osl] = (jax.nn.sigmoid(ga_ref[:, osl]) * pool_out
                             + jax.nn.sigmoid(gb_ref[:, osl]) * ret_out).astype(_BF)
    h2, e0, e1 = _post_mix(x_ref[...], mixed_scr, w_out_ref, g_ffn_ref, w_rt_ref, x1_ref, rw_ref)
    h2_ref[...] = _pack_rows(h2).reshape(h2_ref.shape)
    carry_scr[...] = carry_in_ref[...]
    dst_ref[...] = _rank_rows(e0, e1, cap, carry_scr).astype(jnp.int32)
    cnt_ref[...] = carry_scr[...]


def _mixer_sample(x, proj, state_pool, q_state, wts, carry_in, cap):
    Bs = x.shape[0]
    assert Bs == LANES and POOL_GROUPS == RET_HEADS
    u, q, k, v, ga, gb = proj
    dmask, xi, _, _ = _decay_tables(1)
    dm_b = _const(dmask, (RET_HEADS, 1, 1))
    xi_b = _const(xi[:, :, None], (RET_HEADS, 1, RET_DV))

    in_specs = [
        _full((Bs, D_MODEL)),
        _full((Bs, POOL_IN)), _full((Bs, QK_W)), _full((Bs, QK_W)), _full((Bs, V_W)),
        _full((Bs, D_MODEL)), _full((Bs, D_MODEL)),
        _full((POOL_HIST, Bs, POOL_IN)), _full(q_state.shape),
        _full((POOL_GROUPS, POOL_GC, POOL_OUT_GC)),
        _full((1, D_MODEL)), _full((1, D_MODEL)), _full((D_MODEL, D_MODEL)), _full((1, D_MODEL)),
        _full((D_MODEL, ROUTER_W)),
        _full((RET_HEADS, 1, 1)), _full((RET_HEADS, 1, RET_DV)),
        _full((1, LANES)),
    ]
    out_shape = (
        jax.ShapeDtypeStruct((Bs, D_MODEL), _F32),
        jax.ShapeDtypeStruct((Bs, LANES), jnp.int32),
        jax.ShapeDtypeStruct((Bs, LANES), _F32),
        jax.ShapeDtypeStruct((POOL_HIST, Bs, POOL_IN), _F32),
        jax.ShapeDtypeStruct((1, LANES), _F32),
        jax.ShapeDtypeStruct((Bs,) + ROW_TILE, ROW_DTYPE),
    )
    out_specs = (
        _full((Bs, D_MODEL)), _full((Bs, LANES)), _full((Bs, LANES)), _full((POOL_HIST, Bs, POOL_IN)),
        _full((1, LANES)), _full((Bs,) + ROW_TILE),
    )
    return pl.pallas_call(
        functools.partial(_mixer_sample_kernel, cap=cap),
        grid=(1,),
        in_specs=in_specs,
        out_specs=out_specs,
        out_shape=out_shape,
        scratch_shapes=[pltpu.VMEM((Bs, D_MODEL), _BF), pltpu.VMEM((1, LANES), _F32)],
        compiler_params=pltpu.CompilerParams(dimension_semantics=("arbitrary",), vmem_limit_bytes=VMEM_LIMIT),
        name="mixer_sample",
    )(x, u, q, k, v, ga, gb, state_pool, q_state, wts["w_pool"], wts["pool_scale"], wts["ret_gn"],
      wts["w_out"], wts["g_ffn"], wts["w_rt"], dm_b, xi_b, carry_in)


SC_CORES = 2
SC_SUBCORES = 16
SC_LANES = 16
STATE_ROWS = 32
STATE_UNROLL = 8
STATE_COLS = 8


def _sc_round_bf16(x):
    b = lax.bitcast_convert_type(x, jnp.uint32)
    r = b + jnp.uint32(0x7FFF) + ((b >> 16) & jnp.uint32(1))
    return lax.bitcast_convert_type(r & jnp.uint32(0xFFFF0000), _F32)


def _sc_state_update(s0, kq, v):
    P = s0.shape[0]
    n_workers = SC_CORES * SC_SUBCORES
    R = STATE_ROWS
    n_parts = RET_DK // R
    n_vc = RET_DV // SC_LANES
    assert P % (n_workers * RET_HEADS) == 0 and RET_DK % R == 0 and n_parts % 2 == 0 and n_parts >= 2
    assert R % STATE_UNROLL == 0 and n_vc % STATE_COLS == 0
    per_w = P // n_workers
    seqs_w = per_w // RET_HEADS
    _, _, _, g_chunk = _decay_tables(1)
    decay = _const(g_chunk[:, None], (RET_HEADS, SC_LANES))
    mesh = plsc.VectorSubcoreMesh(core_axis_name="c", subcore_axis_name="s")

    def body(s0_hbm, kq_hbm, v_hbm, g_hbm, out_hbm, qs_hbm, in_v, out_v, k_v, q_v, v_v, g_v, o_v, sem_in, sem_out):
        wid = lax.axis_index("s") * SC_CORES + lax.axis_index("c")
        base = wid * per_w
        seqs = pl.ds(wid * seqs_w, seqs_w)
        pltpu.sync_copy(kq_hbm.at[0, seqs], k_v)
        pltpu.sync_copy(kq_hbm.at[1, seqs], q_v)
        pltpu.sync_copy(v_hbm.at[seqs], v_v)
        pltpu.sync_copy(g_hbm, g_v)

        def load(p, part, slot):
            return pltpu.make_async_copy(s0_hbm.at[p, pl.ds(part * R, R)], in_v.at[slot], sem_in.at[slot])

        def store(p, part, slot):
            return pltpu.make_async_copy(out_v.at[slot], out_hbm.at[p, pl.ds(part * R, R)], sem_out.at[slot])

        def o_at(pp, c):
            return (c * SC_LANES // LANES, pp, pl.ds(c * SC_LANES % LANES, SC_LANES))

        load(base, 0, 0).start()

        def pair(pp, carry):
            p = base + pp
            seq, head = pp // RET_HEADS, pp % RET_HEADS
            g = g_v[head, :]
            sidx = jnp.full((SC_LANES,), seq, jnp.int32)
            for c in range(n_vc):
                o_v[o_at(pp, c)] = jnp.zeros((SC_LANES,), _F32)
            for part in range(n_parts):
                slot = part % 2
                load(p, part, slot).wait()
                if part + 1 < n_parts:
                    load(p, part + 1, 1 - slot).start()
                else:
                    @pl.when(pp + 1 < per_w)
                    def _():
                        load(p + 1, 0, 1 - slot).start()

                if part < 2:
                    @pl.when(pp > 0)
                    def _():
                        store(p, part, slot).wait()
                else:
                    store(p, part, slot).wait()

                def rows(i8, c2):
                    for c0 in range(0, n_vc, STATE_COLS):
                        cols = [pl.ds((c0 + c) * SC_LANES, SC_LANES) for c in range(STATE_COLS)]
                        vs = [v_v[seq, pl.ds(head * RET_DV + (c0 + c) * SC_LANES, SC_LANES)]
                              for c in range(STATE_COLS)]
                        acc = [None] * STATE_COLS
                        for j in range(STATE_UNROLL):
                            i = i8 * STATE_UNROLL + j
                            ridx = jnp.full((SC_LANES,), head * RET_DK + part * R + i, jnp.int32)
                            ki = plsc.load_gather(k_v, [sidx, ridx])
                            qi = plsc.load_gather(q_v, [sidx, ridx])
                            ss = [in_v[slot, i, cs] for cs in cols]
                            for c, cs in enumerate(cols):
                                out_v[slot, i, cs] = g * ss[c] + ki * vs[c]
                            for c in range(STATE_COLS):
                                t = qi * _sc_round_bf16(ss[c])
                                acc[c] = t if acc[c] is None else acc[c] + t
                        for c in range(STATE_COLS):
                            o_v[o_at(pp, c0 + c)] = o_v[o_at(pp, c0 + c)] + acc[c]
                    return c2

                lax.fori_loop(0, R // STATE_UNROLL, rows, 0)
                store(p, part, slot).start()
            return carry

        lax.fori_loop(0, per_w, pair, 0)
        for slot in range(2):
            store(base, slot, slot).wait()
        for hf in range(RET_DV // LANES):
            pltpu.sync_copy(o_v.at[hf], qs_hbm.at[hf, pl.ds(base, per_w)])

    keys = pltpu.VMEM((seqs_w, RET_HEADS * RET_DK), _F32)
    return pl.kernel(
        body, mesh=mesh,
        out_type=(jax.ShapeDtypeStruct(s0.shape, _F32), jax.ShapeDtypeStruct((RET_DV // LANES, P, LANES), _F32)),
        scratch_types=[pltpu.VMEM((2, R, RET_DV), _F32), pltpu.VMEM((2, R, RET_DV), _F32), keys, keys,
                       pltpu.VMEM((seqs_w, RET_HEADS * RET_DV), _F32), pltpu.VMEM((RET_HEADS, SC_LANES), _F32),
                       pltpu.VMEM((RET_DV // LANES, per_w, LANES), _F32),
                       pltpu.SemaphoreType.DMA((2,)), pltpu.SemaphoreType.DMA((2,))],
        compiler_params=dataclasses.replace(pltpu.CompilerParams(), needs_layout_passes=False),
        name="sample_state_update",
    )(s0, kq, v, decay)


IDX_CHUNK = 1024
ISSUE_UNROLL = 8
SC_WINDOW = 64


def _sc_scatter(h2, dst_rows, n_rows):
    T = h2.shape[0]
    n_workers = SC_CORES * SC_SUBCORES
    W = SC_WINDOW
    assert T % (n_workers * W) == 0
    per_w = T // n_workers
    n_win = per_w // W
    d0 = dst_rows[0].reshape(T // W, W)
    d1 = dst_rows[1].reshape(T // W, W)
    mesh = plsc.VectorSubcoreMesh(core_axis_name="c", subcore_axis_name="s")

    def body(h2_hbm, d0_hbm, d1_hbm, xs_hbm, i0_v, i1_v, rows_v, sem_load, sem_store):
        wid = lax.axis_index("s") * SC_CORES + lax.axis_index("c")
        base = wid * per_w
        pltpu.sync_copy(d0_hbm.at[pl.ds(wid * n_win, n_win)], i0_v)
        pltpu.sync_copy(d1_hbm.at[pl.ds(wid * n_win, n_win)], i1_v)
        pltpu.async_copy(h2_hbm.at[pl.ds(base, W)], rows_v.at[0], sem_load)
        for i in range(n_win):
            b = i % 2
            pltpu.make_async_copy(h2_hbm.at[pl.ds(base, W)], rows_v.at[b], sem_load).wait()
            if i + 1 < n_win:
                pltpu.async_copy(h2_hbm.at[pl.ds(base + (i + 1) * W, W)], rows_v.at[1 - b], sem_load)
            c0 = pltpu.async_copy(rows_v.at[b], xs_hbm.at[i0_v.at[i]], sem_store)
            c1 = pltpu.async_copy(rows_v.at[b], xs_hbm.at[i1_v.at[i]], sem_store)
            c0.wait()
            c1.wait()

    return pl.kernel(
        body, mesh=mesh,
        out_type=jax.ShapeDtypeStruct((n_rows,) + ROW_TILE, ROW_DTYPE),
        scratch_types=[pltpu.VMEM((n_win, W), jnp.int32), pltpu.VMEM((n_win, W), jnp.int32),
                       pltpu.VMEM((2, W) + ROW_TILE, ROW_DTYPE),
                       pltpu.SemaphoreType.DMA, pltpu.SemaphoreType.DMA],
        name="moe_sc_scatter",
    )(h2, d0, d1)


ZERO_CHUNK = 64


def _finish_dispatch_kernel(zstart_ref, zchunks_ref, slots_hbm, h2_ref, xs_in_hbm, xs_hbm, idx_smem, zbuf, sem_idx,
                            sem_rows, sem_zero):
    del xs_in_hbm
    TS = h2_ref.shape[0]
    idx_cp = pltpu.make_async_copy(slots_hbm.at[pl.ds(0, IDX_CHUNK)], idx_smem, sem_idx)
    idx_cp.start()
    zbuf[...] = jnp.zeros(zbuf.shape, zbuf.dtype)

    def zero_copy(e, j):
        return pltpu.make_async_copy(zbuf, xs_hbm.at[pl.ds(zstart_ref[e] + j * ZERO_CHUNK, ZERO_CHUNK)], sem_zero)

    for e in range(N_EXPERTS):
        lax.fori_loop(0, zchunks_ref[e], lambda j, c, e=e: (zero_copy(e, j).start(), c)[1], 0)
    idx_cp.wait()

    def issue(r, carry):
        for kk in range(2):
            pltpu.make_async_copy(h2_ref.at[r], xs_hbm.at[idx_smem[2 * r + kk]], sem_rows).start(priority=kk)
        return carry

    lax.fori_loop(0, TS, issue, 0, unroll=ISSUE_UNROLL)
    for e in range(N_EXPERTS):
        lax.fori_loop(0, zchunks_ref[e], lambda j, c, e=e: (zero_copy(e, j).wait(), c)[1], 0)
    for kk in range(2):
        pltpu.make_async_copy(h2_ref, xs_hbm.at[pl.ds(0, TS)], sem_rows).wait()


def _finish_dispatch(xs, h2s, slots, zstart, zchunks):
    Bs = h2s.shape[0]
    assert 2 * Bs <= IDX_CHUNK and slots.shape[0] == IDX_CHUNK and MOE_BLOCK % ZERO_CHUNK == 0
    any_spec = pl.BlockSpec(memory_space=pl.ANY)
    grid_spec = pltpu.PrefetchScalarGridSpec(
        num_scalar_prefetch=2,
        grid=(1,),
        in_specs=[any_spec, pl.BlockSpec((Bs,) + ROW_TILE, lambda t, z, n: (0, 0, 0)), any_spec],
        out_specs=any_spec,
        scratch_shapes=[pltpu.SMEM((IDX_CHUNK,), jnp.int32), pltpu.VMEM((ZERO_CHUNK,) + ROW_TILE, ROW_DTYPE),
                        pltpu.SemaphoreType.DMA, pltpu.SemaphoreType.DMA, pltpu.SemaphoreType.DMA],
    )
    return pl.pallas_call(
        _finish_dispatch_kernel,
        grid_spec=grid_spec,
        out_shape=jax.ShapeDtypeStruct(xs.shape, ROW_DTYPE),
        input_output_aliases={4: 0},
        compiler_params=pltpu.CompilerParams(dimension_semantics=("arbitrary",), has_side_effects=True),
        name="moe_finish_dispatch",
    )(zstart, zchunks, slots, h2s, xs)


FFN_IN_BUFFERS = 4
FFN_OUT_BUFFERS = 3


def _ffn_kernel(cnt_ref, xs_hbm, w1_hbm, w3_hbm, w2_hbm, ys_hbm,
                xs_buf, ys_buf, xb_scr, st1, st3, st2, w13_scr, w2_scr, t_row, t_exp, t_len, sem_in, sem_out, sem_w,
                *, cap_blocks):
    B = MOE_BLOCK

    def fill_expert(e, g):
        nb = (cnt_ref[e] + (B - 1)) // B

        def fill_block(j, carry):
            t_row[g + j] = e * cap_blocks + j
            t_exp[g + j] = e
            t_len[g + j] = nb
            return carry

        lax.fori_loop(0, nb, fill_block, 0)
        return g + nb

    n_used = lax.fori_loop(0, N_EXPERTS, fill_expert, 0)

    def in_copy(g):
        s = g % FFN_IN_BUFFERS
        return pltpu.make_async_copy(xs_hbm.at[pl.ds(t_row[g] * B, B)], xs_buf.at[s], sem_in.at[s])

    def out_copy(g):
        s = g % FFN_OUT_BUFFERS
        return pltpu.make_async_copy(ys_buf.at[s], ys_hbm.at[pl.ds(t_row[g] * B, B)], sem_out.at[s])

    def weight_copies(e, s):
        return (pltpu.make_async_copy(w1_hbm.at[e], st1.at[s], sem_w.at[s]),
                pltpu.make_async_copy(w3_hbm.at[e], st3.at[s], sem_w.at[s]),
                pltpu.make_async_copy(w2_hbm.at[e], st2.at[s], sem_w.at[s]))

    @pl.when(n_used > 0)
    def _():
        for cp in weight_copies(t_exp[0], 0):
            cp.start()

    for g0 in range(FFN_IN_BUFFERS):
        @pl.when(g0 < n_used)
        def _(g0=g0):
            in_copy(g0).start()

    half = B // 2

    def unpack_half(g, h):
        rows = pl.ds(h * half, half)
        xb_scr[g % 2, rows] = _unpack_rows(xs_buf[g % FFN_IN_BUFFERS, rows]).astype(_BF)

    @pl.when(n_used > 0)
    def _():
        in_copy(0).wait()
        unpack_half(0, 0)
        unpack_half(0, 1)

    def block(g, wslot):
        first = (g == 0) | (t_exp[g] != t_exp[jnp.maximum(g - 1, 0)])
        wslot = jnp.where(first & (g > 0), 1 - wslot, wslot)

        @pl.when(first)
        def _():
            for cp in weight_copies(t_exp[g], wslot):
                cp.wait()
            nxt = g + t_len[g]

            @pl.when(nxt < n_used)
            def _():
                for cp in weight_copies(t_exp[jnp.minimum(nxt, n_used - 1)], 1 - wslot):
                    cp.start()

            w13_scr[:, 0:D_EXPERT] = st1[wslot].astype(_BF)
            w13_scr[:, D_EXPERT:2 * D_EXPERT] = st3[wslot].astype(_BF)
            w2_scr[...] = st2[wslot].astype(_BF)

        @pl.when(g + 1 < n_used)
        def _():
            in_copy(g + 1).wait()

        @pl.when((g > 0) & (g + FFN_IN_BUFFERS - 1 < n_used))
        def _():
            in_copy(g + FFN_IN_BUFFERS - 1).start()

        @pl.when(g >= FFN_OUT_BUFFERS)
        def _():
            out_copy(g - FFN_OUT_BUFFERS).wait()

        nxt_blk = jnp.minimum(g + 1, n_used - 1)
        ab = _dot(xb_scr[g % 2], w13_scr[...])
        unpack_half(nxt_blk, 0)
        hid = jax.nn.silu(ab[:, 0:D_EXPERT]) * ab[:, D_EXPERT:2 * D_EXPERT]
        y = _dot(hid.astype(_BF), w2_scr[...])
        unpack_half(nxt_blk, 1)
        ys_buf[g % FFN_OUT_BUFFERS] = _pack_rows(y)
        out_copy(g).start()
        return wslot

    lax.fori_loop(0, n_used, block, 0)

    for back in range(FFN_OUT_BUFFERS, 0, -1):
        @pl.when(n_used >= back)
        def _(back=back):
            out_copy(n_used - back).wait()


def _ffn(xs, counts, w1, w3, w2, cap_blocks, n_blocks):
    any_spec = pl.BlockSpec(memory_space=pl.ANY)
    row_buf = lambda n: pltpu.VMEM((n, MOE_BLOCK) + ROW_TILE, ROW_DTYPE)
    grid_spec = pltpu.PrefetchScalarGridSpec(
        num_scalar_prefetch=1,
        grid=(1,),
        in_specs=[any_spec, any_spec, any_spec, any_spec],
        out_specs=any_spec,
        scratch_shapes=[row_buf(FFN_IN_BUFFERS), row_buf(FFN_OUT_BUFFERS),
                        pltpu.VMEM((2, MOE_BLOCK, D_MODEL), _BF),
                        pltpu.VMEM((2, D_MODEL, D_EXPERT), _F32), pltpu.VMEM((2, D_MODEL, D_EXPERT), _F32),
                        pltpu.VMEM((2, D_EXPERT, D_MODEL), _F32),
                        pltpu.VMEM((D_MODEL, 2 * D_EXPERT), _BF), pltpu.VMEM((D_EXPERT, D_MODEL), _BF),
                        pltpu.SMEM((n_blocks,), jnp.int32), pltpu.SMEM((n_blocks,), jnp.int32),
                        pltpu.SMEM((n_blocks,), jnp.int32),
                        pltpu.SemaphoreType.DMA((FFN_IN_BUFFERS,)), pltpu.SemaphoreType.DMA((FFN_OUT_BUFFERS,)),
                        pltpu.SemaphoreType.DMA((2,))],
    )
    return pl.pallas_call(
        functools.partial(_ffn_kernel, cap_blocks=cap_blocks),
        grid_spec=grid_spec,
        out_shape=jax.ShapeDtypeStruct(xs.shape, ROW_DTYPE),
        compiler_params=pltpu.CompilerParams(dimension_semantics=("arbitrary",), vmem_limit_bytes=VMEM_LIMIT,
                                             has_side_effects=True),
        name="moe_ffn",
    )(counts, xs, w1, w3, w2)


def _combine_kernel(slots_hbm, x1_ref, rw_ref, g_ref, ys_hbm, y_ref, idx_smem, buf, sem_idx, sem_rows):
    t = pl.program_id(0)
    TC = x1_ref.shape[0]
    idx_cp = pltpu.make_async_copy(slots_hbm.at[pl.ds(t * IDX_CHUNK, IDX_CHUNK)], idx_smem, sem_idx)
    idx_cp.start()
    idx_cp.wait()

    def issue(r, carry):
        for kk in range(2):
            pltpu.make_async_copy(ys_hbm.at[idx_smem[2 * r + kk]], buf.at[kk, r], sem_rows).start(priority=kk)
        return carry

    lax.fori_loop(0, TC, issue, 0, unroll=ISSUE_UNROLL)
    for kk in range(2):
        pltpu.make_async_copy(ys_hbm.at[pl.ds(0, TC)], buf.at[kk], sem_rows).wait()

    w = rw_ref[...]
    y0 = _unpack_rows(buf[0])
    y1 = _unpack_rows(buf[1])
    x2 = x1_ref[...] + (w[:, 0:1] * y0 + w[:, 1:2] * y1)
    y_ref[...] = _rms(x2, g_ref[...])


SC_GATHER_CHUNKS = 4
DENSE_TILE = 1024
SC_GATHER_BUFFERS = 3
SC_GATHER_MAX_WINDOW = 80


def _sc_gather(ys, rows):
    Tg = rows.shape[1]
    n_workers = SC_CORES * SC_SUBCORES
    per_w = Tg // n_workers
    assert per_w * n_workers == Tg and per_w % SUBLANES == 0
    W = max(w for w in range(SUBLANES, SC_GATHER_MAX_WINDOW + 1, SUBLANES) if per_w % w == 0)
    n_win = per_w // W
    NB = SC_GATHER_BUFFERS
    idx = rows.reshape(2, n_workers, n_win, W)
    jobs = [(kk, i) for i in range(n_win) for kk in range(2)]
    mesh = plsc.VectorSubcoreMesh(core_axis_name="c", subcore_axis_name="s")

    def body(ys_hbm, idx_hbm, out_hbm, i_v, rows_v, sem_g, sem_s):
        wid = lax.axis_index("s") * SC_CORES + lax.axis_index("c")
        base = wid * per_w
        for kk in range(2):
            pltpu.sync_copy(idx_hbm.at[kk, wid], i_v.at[kk])

        def gather(j):
            kk, i = jobs[j]
            return pltpu.make_async_copy(ys_hbm.at[i_v.at[kk, i]], rows_v.at[j % NB], sem_g.at[j % NB])

        def store(j):
            kk, i = jobs[j]
            return pltpu.make_async_copy(rows_v.at[j % NB], out_hbm.at[kk, pl.ds(base + i * W, W)], sem_s.at[j % NB])

        gather(0).start()
        for j in range(len(jobs)):
            gather(j).wait()
            store(j).start()
            if j + 1 < len(jobs):
                if j + 1 >= NB:
                    store(j + 1 - NB).wait()
                gather(j + 1).start()
        for j in range(max(0, len(jobs) - NB), len(jobs)):
            store(j).wait()

    return pl.kernel(
        body, mesh=mesh,
        out_type=jax.ShapeDtypeStruct((2, Tg) + ROW_TILE, ROW_DTYPE),
        scratch_types=[pltpu.VMEM((2, n_win, W), jnp.int32), pltpu.VMEM((NB, W) + ROW_TILE, ROW_DTYPE),
                       pltpu.SemaphoreType.DMA((NB,)), pltpu.SemaphoreType.DMA((NB,))],
        name="moe_sc_gather",
    )(ys, idx)


def _combine_dense_kernel(x1_ref, rw_ref, g_ref, rows_ref, *rest):
    y_ref = rest[-1]
    w = rw_ref[...]
    y0 = _unpack_rows(rows_ref[0])
    y1 = _unpack_rows(rows_ref[1])
    x2 = x1_ref[...] + (w[:, 0:1] * y0 + w[:, 1:2] * y1)
    y_ref[...] = _rms(x2, g_ref[...])


def _combine_dense(x1, rw, rows, y, g_final, tile, first_tile):
    Tg = rows.shape[1]
    assert Tg % tile == 0
    tok = lambda t: (first_tile + t, 0)
    in_specs = [pl.BlockSpec((tile, D_MODEL), tok), pl.BlockSpec((tile, LANES), tok), _full((1, D_MODEL)),
                pl.BlockSpec((2, tile) + ROW_TILE, lambda t: (0, t, 0, 0))]
    args = [x1, rw, g_final.reshape(1, D_MODEL), rows]
    aliases = {}
    if y is not None:
        in_specs.append(pl.BlockSpec(memory_space=pl.ANY))
        args.append(y)
        aliases = {4: 0}
    return pl.pallas_call(
        _combine_dense_kernel,
        grid=(Tg // tile,),
        in_specs=in_specs,
        out_specs=pl.BlockSpec((tile, D_MODEL), tok),
        out_shape=jax.ShapeDtypeStruct(x1.shape, _F32),
        input_output_aliases=aliases,
        compiler_params=pltpu.CompilerParams(dimension_semantics=("arbitrary",), vmem_limit_bytes=VMEM_LIMIT),
        name="moe_combine_dense",
    )(*args)


def _combine(x1, rw, slots, ys, g_final, tile, n_tiles):
    T = x1.shape[0]
    assert T % tile == 0 and 2 * tile <= IDX_CHUNK and slots.shape[0] == n_tiles * IDX_CHUNK
    any_spec = pl.BlockSpec(memory_space=pl.ANY)
    return pl.pallas_call(
        _combine_kernel,
        grid=(n_tiles,),
        in_specs=[any_spec, pl.BlockSpec((tile, D_MODEL), lambda t: (t, 0)),
                  pl.BlockSpec((tile, LANES), lambda t: (t, 0)), _full((1, D_MODEL)), any_spec],
        out_specs=pl.BlockSpec((tile, D_MODEL), lambda t: (t, 0)),
        out_shape=jax.ShapeDtypeStruct((T, D_MODEL), _F32),
        scratch_shapes=[pltpu.SMEM((IDX_CHUNK,), jnp.int32), pltpu.VMEM((2, tile) + ROW_TILE, ROW_DTYPE),
                        pltpu.SemaphoreType.DMA, pltpu.SemaphoreType.DMA],
        compiler_params=pltpu.CompilerParams(dimension_semantics=("arbitrary",), vmem_limit_bytes=VMEM_LIMIT),
        name="moe_combine",
    )(slots, x1, rw, g_final.reshape(1, D_MODEL), ys)


def _tile_for(n_tokens):
    tile = IDX_CHUNK // 2
    return tile if n_tokens % tile == 0 else n_tokens


def _chunked_slots(slot, tile):
    n_tiles = slot.shape[0] // tile
    s = slot.reshape(n_tiles, 2 * tile)
    return jnp.pad(s, ((0, 0), (0, IDX_CHUNK - 2 * tile))).reshape(-1)


def kernel(x_prompt, x_sample, state_pool, state_ret, g_mix, w_in, w_pool, pool_scale, ret_gn, w_out, g_ffn, w_grp, w_exp, w1, w3, w2, g_final):
    Bp, Lp, _ = x_prompt.shape
    Bs = x_sample.shape[0]
    Tp = Bp * Lp
    wts = _prep_weights(g_mix[0], w_pool[0], pool_scale[0], ret_gn[0], w_out[0], g_ffn[0], w_grp[0], w_exp[0])

    T_all = Tp + Bs
    cap = (-(-T_all // MOE_BLOCK) + 1) * MOE_BLOCK
    cap_blocks = cap // MOE_BLOCK

    xs_tok = x_sample.reshape(Bs, D_MODEL)
    wts["w_in"], u_s, q_s, k_s, kq_s, v_s, ga_s, gb_s = _sample_proj(xs_tok, wts["g_mix"], w_in[0], PAST_LEN)
    nret_s, q_state = _sc_state_update(state_ret[0].reshape(Bs * RET_HEADS, RET_DK, RET_DV), kq_s, v_s)
    x1p, dst_rows_p, rwp, npool_p, nret_p, counts_p, h2p = _mixer_prompt(x_prompt, wts, cap)
    xs = _sc_scatter(h2p, dst_rows_p, N_EXPERTS * cap)
    x1s, dst_s, rws, npool_s, counts, h2s = _mixer_sample(
        xs_tok, (u_s, q_s, k_s, v_s, ga_s, gb_s), jnp.swapaxes(state_pool[0], 0, 1), q_state, wts, counts_p, cap)

    counts = counts[0, :N_EXPERTS].astype(jnp.int32)
    tile_p, tile_s = _tile_for(Tp), _tile_for(Bs)
    slots_s = _chunked_slots(dst_s[:, :2], tile_s)
    zstart = jnp.arange(N_EXPERTS, dtype=jnp.int32) * cap + counts
    zchunks = ((-counts) % MOE_BLOCK + ZERO_CHUNK - 1) // ZERO_CHUNK
    xs = _finish_dispatch(xs, h2s, slots_s, zstart, zchunks.astype(jnp.int32))
    n_blocks = -(-2 * T_all // MOE_BLOCK) + N_EXPERTS
    ys = _ffn(xs, counts, w1[0], w3[0], w2[0], cap_blocks, n_blocks)
    tile_d = DENSE_TILE if Tp % DENSE_TILE == 0 else tile_p
    n_tiles_d = Tp // tile_d
    chunk_tiles = [n_tiles_d // SC_GATHER_CHUNKS + (k < n_tiles_d % SC_GATHER_CHUNKS)
                   for k in range(SC_GATHER_CHUNKS)]
    gathered, first = [], 0
    for nt in chunk_tiles:
        if nt:
            gathered.append((first, _sc_gather(ys, dst_rows_p[:, first * tile_d:(first + nt) * tile_d])))
            first += nt
    y_s = _combine(x1s, rws, slots_s, ys, g_final, tile_s, 1)
    y_p = None
    for first, rows in gathered:
        y_p = _combine_dense(x1p, rwp, rows, y_p, g_final, tile_d, first)

    return (y_p.reshape(Bp, Lp, D_MODEL), y_s.reshape(Bs, 1, D_MODEL),
            npool_p[None], nret_p[None], jnp.swapaxes(npool_s, 0, 1)[None], nret_s.reshape(state_ret.shape))
```

```python
import dataclasses
import functools

import jax
import jax.numpy as jnp
import numpy as np
from jax import lax
from jax.experimental import pallas as pl
from jax.experimental.pallas import tpu as pltpu
from jax.experimental.pallas import tpu_sc as plsc

D_MODEL = 1024
EPS = 1e-6
POOL_GROUPS = 4
POOL_IN = D_MODEL // 2
POOL_GC = POOL_IN // POOL_GROUPS
POOL_OUT_GC = D_MODEL // POOL_GROUPS
POOL_WINDOWS = (2, 4, 8, 16)
POOL_HIST = max(POOL_WINDOWS) - 1
RET_HEADS = 4
RET_DK = D_MODEL // 8
RET_DV = D_MODEL // RET_HEADS
ROPE_BASE = 10000.0
PAST_LEN = 16384
N_GROUPS = 4
EXPERTS_PER_GROUP = 8
N_EXPERTS = N_GROUPS * EXPERTS_PER_GROUP
D_EXPERT = D_MODEL // 4
QK_W = RET_HEADS * RET_DK
V_W = RET_HEADS * RET_DV
OFF_U = 0
OFF_Q = POOL_IN
OFF_K = OFF_Q + QK_W
OFF_V = OFF_K + QK_W
OFF_GA = OFF_V + V_W
OFF_GB = OFF_GA + D_MODEL
IN_WIDTH = OFF_GB + D_MODEL

LANES = 128
SUBLANES = 8
ROW_WORDS = D_MODEL // 2
ROW_TILE = (ROW_WORDS // LANES, LANES)
ROW_DTYPE = jnp.uint32
HALO = 16
RET_CHUNK = 256
MIXER_SEQS = 2
MOE_BLOCK = 512
ROUTER_W = LANES
RANK_GROUP = 128
VMEM_LIMIT = 56 * 1024 * 1024

_BF = jnp.bfloat16
_F32 = jnp.float32


def _rms(x, g):
    inv = lax.rsqrt(jnp.mean(x * x, axis=-1, keepdims=True) + EPS)
    return x * inv * g


def _dot(a, b):
    return jnp.dot(a, b, preferred_element_type=_F32)


def _round_bf16(x):
    return x.astype(_BF).astype(_F32)


def _pack_rows(x):
    lo = lax.bitcast_convert_type(x[:, :ROW_WORDS].astype(_BF).astype(_F32), jnp.uint32)
    hi = lax.bitcast_convert_type(x[:, ROW_WORDS:].astype(_BF).astype(_F32), jnp.uint32)
    return ((lo >> 16) | hi).reshape((x.shape[0],) + ROW_TILE)


def _unpack_rows(w):
    w = w.reshape(w.shape[0], ROW_WORDS)
    lo = lax.bitcast_convert_type(w << 16, _F32)
    hi = lax.bitcast_convert_type(w & jnp.uint32(0xFFFF0000), _F32)
    return jnp.concatenate([lo, hi], axis=1)


def _rotary(x, cos2, sin2):
    return x * cos2 + pltpu.roll(x, RET_DK // 2, 1) * sin2


def _route(logits):
    lane = lax.broadcasted_iota(jnp.int32, logits.shape, 1).astype(_F32)
    neg = jnp.float32(-jnp.inf)
    big = jnp.float32(1 << 20)
    lg = jnp.where(lane < N_GROUPS, logits, neg)
    mg = jnp.max(lg, axis=-1, keepdims=True)
    g_idx = jnp.min(jnp.where(lg == mg, lane, big), axis=-1, keepdims=True)
    p_g = 1.0 / jnp.sum(jnp.exp(lg - mg), axis=-1, keepdims=True)
    lo = N_GROUPS + g_idx * EXPERTS_PER_GROUP
    in_grp = (lane >= lo) & (lane < lo + EXPERTS_PER_GROUP)
    le = jnp.where(in_grp, logits, neg)
    m1 = jnp.max(le, axis=-1, keepdims=True)
    i1 = jnp.min(jnp.where(le == m1, lane, big), axis=-1, keepdims=True)
    le2 = jnp.where(lane == i1, neg, le)
    m2 = jnp.max(le2, axis=-1, keepdims=True)
    i2 = jnp.min(jnp.where(le2 == m2, lane, big), axis=-1, keepdims=True)
    t = jnp.exp(m2 - m1)
    den = 1.0 + t
    e0 = (i1 - N_GROUPS).astype(jnp.int32)
    e1 = (i2 - N_GROUPS).astype(jnp.int32)
    return e0, e1, p_g * (1.0 / den), p_g * (t / den)


def _post_mix(x, mixed_ref, w_out_ref, g_ffn_ref, w_rt_ref, x1_ref, rw_ref):
    x1 = x + _dot(mixed_ref[...], w_out_ref[...])
    x1_ref[...] = x1.reshape(x1_ref.shape)
    h2 = _rms(x1, g_ffn_ref[...])
    e0, e1, w0, w1 = _route(_dot(h2.astype(_BF), w_rt_ref[...]))
    lane = lax.broadcasted_iota(jnp.int32, (x.shape[0], LANES), 1)
    rw_ref[...] = jnp.where(lane == 0, w0, jnp.where(lane == 1, w1, 0.0)).reshape(rw_ref.shape)
    return h2, e0, e1


def _route_block(x1, g_ffn_ref, w_rt_ref, carry_scr, cap, valid, h2_ref, rw_ref, dst_t_ref):
    for _ in _route_stages(x1, g_ffn_ref, w_rt_ref, carry_scr, cap, valid, h2_ref, rw_ref, dst_t_ref):
        pass


def _route_stages(x1, g_ffn_ref, w_rt_ref, carry_scr, cap, valid, h2_ref, rw_ref, dst_t_ref):
    h2 = _rms(x1, g_ffn_ref[...])
    logits = _dot(h2.astype(_BF), w_rt_ref[...])
    h2_ref[...] = _pack_rows(h2).reshape(h2_ref.shape)
    yield
    e0, e1, w0, w1 = _route(logits)
    lane = lax.broadcasted_iota(jnp.int32, (x1.shape[0], LANES), 1)
    rw_ref[...] = jnp.where(lane == 0, w0, jnp.where(lane == 1, w1, 0.0)).reshape(rw_ref.shape)
    yield
    dst = _rank_rows(e0, e1, cap, carry_scr, valid)
    dst_t = dst.T[0:SUBLANES, :].astype(jnp.int32)
    n_parts, _, width = dst_t_ref.shape
    for part in range(n_parts):
        dst_t_ref[part] = dst_t[:, part * width:(part + 1) * width]
    yield


def _rank_rows(e0, e1, cap, carry_scr, valid=None):
    R = e0.shape[0]
    lane = lax.broadcasted_iota(jnp.int32, (R, LANES), 1)
    m0 = lane == e0
    m1 = lane == e1
    onehot = jnp.where(m0 | m1, 1.0, 0.0)
    G = min(RANK_GROUP, R)
    r_i = lax.broadcasted_iota(jnp.int32, (G, G), 0)
    c_i = lax.broadcasted_iota(jnp.int32, (G, G), 1)
    tri = jnp.where(c_i < r_i, 1.0, 0.0).astype(_BF)
    running = carry_scr[...]
    parts = []
    for g in range(R // G):
        grp = onehot[g * G:(g + 1) * G]
        parts.append(_dot(tri, grp.astype(_BF)) + running)
        running = running + jnp.sum(grp, axis=0, keepdims=True)
    before = jnp.concatenate(parts, axis=0)
    d0 = e0.astype(_F32) * cap + jnp.sum(jnp.where(m0, before, 0.0), axis=-1, keepdims=True)
    d1 = e1.astype(_F32) * cap + jnp.sum(jnp.where(m1, before, 0.0), axis=-1, keepdims=True)
    added = running - carry_scr[...]
    carry_scr[...] += added if valid is None else added * valid
    return jnp.where(lane == 0, d0, jnp.where(lane == 1, d1, 0.0))


def _group_norm(o, gain):
    mu = jnp.mean(o, axis=-1, keepdims=True)
    d = o - mu
    var = jnp.mean(d * d, axis=-1, keepdims=True)
    return d * lax.rsqrt(var + EPS) * gain


def _mixer_prompt_kernel(x_ref, g_mix_ref, w_in_ref, w_pool_ref, pscale_ref, gn_ref, w_out_ref, g_ffn_ref,
                         w_rt_ref, cos_ref, sin_ref, dmask_ref, xi_ref, zeta_ref, gc_ref,
                         x1_ref, dst_t_ref, rw_ref, npool_ref, nret_ref, cnt_ref, h2_ref,
                         u_scr, s_scr, mixed_scr, carry_scr, z_scr, x1_prev_scr, *, cap):
    c = pl.program_id(1)
    n_c = pl.num_programs(1)
    step = pl.program_id(0) * n_c + c
    NSEQ, C, _ = x_ref.shape
    R = NSEQ * C

    @pl.when(step == 0)
    def _():
        carry_scr[...] = jnp.zeros(carry_scr.shape, _F32)
        x1_prev_scr[...] = jnp.zeros(x1_prev_scr.shape, _F32)

    @pl.when(c == 0)
    def _():
        u_scr[:, 0:HALO, :] = jnp.zeros((NSEQ, HALO, POOL_IN), _F32)
        s_scr[...] = jnp.zeros(s_scr.shape, _F32)

    x = x_ref[...].reshape(R, D_MODEL)
    h = _rms(x, g_mix_ref[...]).astype(_BF)

    stages = _route_stages(x1_prev_scr[...], g_ffn_ref, w_rt_ref, carry_scr, cap, (step > 0).astype(_F32),
                           h2_ref, rw_ref, dst_t_ref)
    for lo, hi in ((0, OFF_V), (OFF_V, OFF_GA), (OFF_GA, OFF_GB), (OFF_GB, IN_WIDTH)):
        z_scr[:, lo:hi] = _dot(h, w_in_ref[:, lo:hi])
        next(stages, None)
    cnt_ref[...] = carry_scr[...]

    u_scr[:, HALO:HALO + C, :] = z_scr[:, OFF_U:OFF_U + POOL_IN].reshape(NSEQ, C, POOL_IN)
    q = z_scr[:, OFF_Q:OFF_Q + QK_W]
    k = z_scr[:, OFF_K:OFF_K + QK_W]
    cos2 = jnp.concatenate([cos_ref[...]] * NSEQ, axis=0)
    sin2 = jnp.concatenate([sin_ref[...]] * NSEQ, axis=0)
    pos1 = (c * C + 1 + lax.broadcasted_iota(jnp.int32, (C, POOL_GC), 0)).astype(_F32)

    for j in range(RET_HEADS):
        win = POOL_WINDOWS[j]
        cs = slice(j * POOL_GC, (j + 1) * POOL_GC)
        n_rows = jnp.minimum(pos1, jnp.float32(win))
        pooled = []
        for sq in range(NSEQ):
            u_j = u_scr[sq, HALO:HALO + C, cs]
            s = u_j
            for d in range(1, win):
                s = s + u_scr[sq, HALO - d:HALO - d + C, cs]
            pooled.append(s / n_rows - u_j)
        osl = slice(j * POOL_OUT_GC, (j + 1) * POOL_OUT_GC)
        pool_out = _dot(jnp.concatenate(pooled, axis=0).astype(_BF), w_pool_ref[j]) * pscale_ref[:, osl]

        qs = slice(j * RET_DK, (j + 1) * RET_DK)
        qb = _rotary(q[:, qs], cos2, sin2).astype(_BF)
        kf = _rotary(k[:, qs], cos2, sin2) * (RET_DK ** -0.5)
        kb = kf.astype(_BF)
        vb = z_scr[:, OFF_V + j * RET_DV:OFF_V + (j + 1) * RET_DV].astype(_BF)
        ret = []
        for sq in range(NSEQ):
            rows = slice(sq * C, (sq + 1) * C)
            scores = lax.dot_general(qb[rows], kb[rows], (((1,), (1,)), ((), ())),
                                     preferred_element_type=_F32) * dmask_ref[j]
            s_old = s_scr[sq, j]
            o = _dot(scores.astype(_BF), vb[rows]) + _dot(qb[rows], s_old.astype(_BF)) * xi_ref[j]
            kz = (kf[rows] * zeta_ref[j]).astype(_BF)
            s_scr[sq, j] = gc_ref[j] * s_old + lax.dot_general(kz, vb[rows], (((0,), (0,)), ((), ())),
                                                               preferred_element_type=_F32)
            ret.append(_group_norm(o, gn_ref[:, osl]))
        ret_out = jnp.concatenate(ret, axis=0)

        ga = z_scr[:, OFF_GA + j * RET_DV:OFF_GA + (j + 1) * RET_DV]
        gb = z_scr[:, OFF_GB + j * RET_DV:OFF_GB + (j + 1) * RET_DV]
        mixed_scr[:, osl] = (jax.nn.sigmoid(ga) * pool_out + jax.nn.sigmoid(gb) * ret_out).astype(_BF)

    x1 = x + _dot(mixed_scr[...], w_out_ref[...])
    x1_ref[...] = x1.reshape(x1_ref.shape)
    x1_prev_scr[...] = x1

    u_scr[:, 0:HALO, :] = u_scr[:, C:C + HALO, :]

    @pl.when(c == n_c - 1)
    def _():
        npool_ref[...] = u_scr[:, HALO + C - POOL_HIST:HALO + C, :]
        nret_ref[...] = s_scr[...]


def _decay_tables(C):
    f32 = np.float32
    log_g = np.log(f32(1.0) - np.exp2(f32(-5.0) - np.arange(RET_HEADS, dtype=f32)))
    i = np.arange(C, dtype=f32)
    diff = i[:, None] - i[None, :]
    dmask = np.where(diff >= 0, np.exp(np.maximum(diff, f32(0.0))[None] * log_g[:, None, None]), f32(0.0))
    xi = np.exp((i[None, :] + f32(1.0)) * log_g[:, None])
    zeta = np.exp((f32(C) - f32(1.0) - i)[None, :] * log_g[:, None])
    g_chunk = np.exp(f32(C) * log_g)
    return dmask.astype(f32), xi.astype(f32), zeta.astype(f32), g_chunk.astype(f32)


def _rope_tables(pos):
    half = RET_DK // 2
    freqs = ROPE_BASE ** (-jnp.arange(half, dtype=_F32) / half)
    ang = pos[:, None] * freqs[None, :]
    cos, sin = jnp.cos(ang), jnp.sin(ang)
    return jnp.concatenate([cos, cos], axis=-1), jnp.concatenate([-sin, sin], axis=-1)


def _const(a, shape=None):
    if shape is not None:
        a = np.ascontiguousarray(np.broadcast_to(a, shape))
    return jnp.asarray(a)


def _full(shape):
    n = len(shape)
    return pl.BlockSpec(shape, lambda *_: (0,) * n)


def _route_tail_kernel(x1_ref, g_ffn_ref, w_rt_ref, cnt_in_ref, dst_t_in, rw_in, h2_in,
                       dst_t_ref, rw_ref, cnt_ref, h2_ref, carry_scr, *, cap):
    del dst_t_in, rw_in, h2_in
    carry_scr[...] = cnt_in_ref[...]
    x1 = x1_ref[...]
    _route_block(x1.reshape(x1.shape[0] * x1.shape[1], D_MODEL), g_ffn_ref, w_rt_ref, carry_scr, cap, None,
                 h2_ref, rw_ref, dst_t_ref)
    cnt_ref[...] = carry_scr[...]


def _route_tail(x1, dst_t, rw, counts, h2, wts, cap, C, last):
    NSEQ = x1.shape[0]
    any_spec = pl.BlockSpec(memory_space=pl.ANY)
    return pl.pallas_call(
        functools.partial(_route_tail_kernel, cap=cap),
        grid=(1,),
        in_specs=[pl.BlockSpec((NSEQ, C, D_MODEL), lambda i: (0, last, 0)), _full((1, D_MODEL)),
                  _full((D_MODEL, ROUTER_W)), _full((1, LANES)), any_spec, any_spec, any_spec],
        out_specs=(pl.BlockSpec((NSEQ, SUBLANES, C), lambda i: (0, 0, last)),
                   pl.BlockSpec((NSEQ, C, LANES), lambda i: (0, last, 0)),
                   _full((1, LANES)),
                   pl.BlockSpec((NSEQ, C) + ROW_TILE, lambda i: (0, last, 0, 0))),
        out_shape=(jax.ShapeDtypeStruct(dst_t.shape, dst_t.dtype), jax.ShapeDtypeStruct(rw.shape, rw.dtype),
                   jax.ShapeDtypeStruct((1, LANES), _F32), jax.ShapeDtypeStruct(h2.shape, h2.dtype)),
        input_output_aliases={4: 0, 5: 1, 6: 3},
        scratch_shapes=[pltpu.VMEM((1, LANES), _F32)],
        compiler_params=pltpu.CompilerParams(dimension_semantics=("arbitrary",), vmem_limit_bytes=VMEM_LIMIT),
        name="mixer_route_tail",
    )(x1, wts["g_ffn"], wts["w_rt"], counts, dst_t, rw, h2)


def _mixer_prompt(x, wts, cap):
    B, L, _ = x.shape
    C = RET_CHUNK if L % RET_CHUNK == 0 else L
    n_c = L // C
    T = B * L
    dmask, xi, zeta, g_chunk = _decay_tables(C)
    dmask = _const(dmask)
    xi_b = _const(xi[:, :, None], (RET_HEADS, C, RET_DV))
    zeta_b = _const(zeta[:, :, None], (RET_HEADS, C, RET_DK))
    gc_b = _const(g_chunk[:, None, None], (RET_HEADS, 1, RET_DV))
    cos2, sin2 = _rope_tables(jnp.arange(L).astype(_F32))

    NSEQ = MIXER_SEQS if B % MIXER_SEQS == 0 else 1
    B2 = B // NSEQ
    T2 = T // NSEQ
    x = x.reshape(NSEQ, B2, L, D_MODEL)
    tok = lambda b, c: (0, b * n_c + c, 0)
    in_specs = [
        pl.BlockSpec((NSEQ, None, C, D_MODEL), lambda b, c: (0, b, c, 0)),
        _full((1, D_MODEL)), _full((D_MODEL, IN_WIDTH)), _full((POOL_GROUPS, POOL_GC, POOL_OUT_GC)),
        _full((1, D_MODEL)), _full((1, D_MODEL)), _full((D_MODEL, D_MODEL)), _full((1, D_MODEL)),
        _full((D_MODEL, ROUTER_W)),
        pl.BlockSpec((C, RET_DK), lambda b, c: (c, 0)), pl.BlockSpec((C, RET_DK), lambda b, c: (c, 0)),
        _full((RET_HEADS, C, C)), _full((RET_HEADS, C, RET_DV)), _full((RET_HEADS, C, RET_DK)),
        _full((RET_HEADS, 1, RET_DV)),
    ]
    R = NSEQ * C
    out_shape = (
        jax.ShapeDtypeStruct((NSEQ, T2, D_MODEL), _F32),
        jax.ShapeDtypeStruct((NSEQ, SUBLANES, T2), jnp.int32),
        jax.ShapeDtypeStruct((NSEQ, T2, LANES), _F32),
        jax.ShapeDtypeStruct((NSEQ, B2, POOL_HIST, POOL_IN), _F32),
        jax.ShapeDtypeStruct((NSEQ, B2, RET_HEADS, RET_DK, RET_DV), _F32),
        jax.ShapeDtypeStruct((1, LANES), _F32),
        jax.ShapeDtypeStruct((NSEQ, T2) + ROW_TILE, ROW_DTYPE),
    )
    prev = lambda b, c: jnp.maximum(b * n_c + c - 1, 0)
    out_specs = (
        pl.BlockSpec((NSEQ, C, D_MODEL), tok),
        pl.BlockSpec((NSEQ, SUBLANES, C), lambda b, c: (0, 0, prev(b, c))),
        pl.BlockSpec((NSEQ, C, LANES), lambda b, c: (0, prev(b, c), 0)),
        pl.BlockSpec((NSEQ, None, POOL_HIST, POOL_IN), lambda b, c: (0, b, 0, 0)),
        pl.BlockSpec((NSEQ, None, RET_HEADS, RET_DK, RET_DV), lambda b, c: (0, b, 0, 0, 0)),
        _full((1, LANES)),
        pl.BlockSpec((NSEQ, C) + ROW_TILE, lambda b, c: (0, prev(b, c), 0, 0)),
    )
    x1, dst_t, rw, npool, nret, counts, h2 = pl.pallas_call(
        functools.partial(_mixer_prompt_kernel, cap=cap),
        grid=(B2, n_c),
        in_specs=in_specs,
        out_specs=out_specs,
        out_shape=out_shape,
        scratch_shapes=[pltpu.VMEM((NSEQ, HALO + C, POOL_IN), _F32),
                        pltpu.VMEM((NSEQ, RET_HEADS, RET_DK, RET_DV), _F32),
                        pltpu.VMEM((R, D_MODEL), _BF),
                        pltpu.VMEM((1, LANES), _F32),
                        pltpu.VMEM((R, IN_WIDTH), _F32),
                        pltpu.VMEM((R, D_MODEL), _F32)],
        compiler_params=pltpu.CompilerParams(dimension_semantics=("arbitrary", "arbitrary"),
                                             vmem_limit_bytes=VMEM_LIMIT),
        name="mixer_prompt",
    )(x, wts["g_mix"], wts["w_in"], wts["w_pool"], wts["pool_scale"], wts["ret_gn"], wts["w_out"],
      wts["g_ffn"], wts["w_rt"], cos2, sin2, dmask, xi_b, zeta_b, gc_b)
    dst_t, rw, counts, h2 = _route_tail(x1, dst_t, rw, counts, h2, wts, cap, C, B2 * n_c - 1)
    dst_rows = jnp.moveaxis(dst_t[:, 0:2, :], 1, 0).reshape(2, T)
    return (x1.reshape(T, D_MODEL), dst_rows, rw.reshape(T, LANES),
            npool.reshape(B, POOL_HIST, POOL_IN), nret.reshape(B, RET_HEADS, RET_DK, RET_DV), counts,
            h2.reshape((T,) + ROW_TILE))


def _prep_weights(g_mix, w_pool, pool_scale, ret_gn, w_out, g_ffn, w_grp, w_exp):
    w_rt = jnp.concatenate([w_grp, w_exp.reshape(D_MODEL, N_EXPERTS)], axis=1)
    w_rt = jnp.pad(w_rt, ((0, 0), (0, ROUTER_W - w_rt.shape[1])))
    row = lambda v: v.reshape(1, D_MODEL)
    return dict(g_mix=row(g_mix), w_pool=w_pool.astype(_BF), pool_scale=row(pool_scale),
                ret_gn=row(ret_gn), w_out=w_out.astype(_BF), g_ffn=row(g_ffn), w_rt=w_rt.astype(_BF))


PROJ_COLS = 512


def _sample_proj_kernel(x_ref, g_mix_ref, w_in_ref, cos_ref, sin_ref, zeta_ref,
                        w_bf_ref, u_ref, q_ref, k_ref, kq_ref, v_ref, ga_ref, gb_ref, h_scr, z_scr):
    c = pl.program_id(0)

    @pl.when(c == 0)
    def _():
        h_scr[...] = _rms(x_ref[...], g_mix_ref[...]).astype(_BF)

    w = w_in_ref[...].astype(_BF)
    w_bf_ref[...] = w
    z_scr[c] = _dot(h_scr[...], w)

    @pl.when(c == pl.num_programs(0) - 1)
    def _():
        def cols(off, width):
            parts = [z_scr[i] for i in range(off // PROJ_COLS, (off + width) // PROJ_COLS)]
            return parts[0] if len(parts) == 1 else jnp.concatenate(parts, axis=1)

        u_ref[...] = cols(OFF_U, POOL_IN)
        q = cols(OFF_Q, QK_W)
        k = cols(OFF_K, QK_W)
        for j in range(RET_HEADS):
            qs = slice(j * RET_DK, (j + 1) * RET_DK)
            qf = _rotary(q[:, qs], cos_ref[...], sin_ref[...])
            kf = _rotary(k[:, qs], cos_ref[...], sin_ref[...]) * (RET_DK ** -0.5)
            q_ref[:, qs] = qf
            k_ref[:, qs] = kf
            kq_ref[0, :, qs] = kf * zeta_ref[j]
            kq_ref[1, :, qs] = _round_bf16(qf)
        v_ref[...] = cols(OFF_V, V_W)
        ga_ref[...] = cols(OFF_GA, D_MODEL)
        gb_ref[...] = cols(OFF_GB, D_MODEL)


def _sample_proj(x, g_mix, w_in, pos0):
    Bs = x.shape[0]
    offsets = (OFF_U, OFF_Q, OFF_K, OFF_V, OFF_GA, OFF_GB, IN_WIDTH)
    assert all(o % PROJ_COLS == 0 for o in offsets)
    n_c = IN_WIDTH // PROJ_COLS
    _, _, zeta, _ = _decay_tables(1)
    cos2, sin2 = _rope_tables((pos0 + jnp.arange(1)).astype(_F32))
    shapes = ((Bs, POOL_IN), (Bs, QK_W), (Bs, QK_W), (2, Bs, QK_W), (Bs, V_W), (Bs, D_MODEL), (Bs, D_MODEL))
    return pl.pallas_call(
        _sample_proj_kernel,
        grid=(n_c,),
        in_specs=[_full((Bs, D_MODEL)), _full((1, D_MODEL)), pl.BlockSpec((D_MODEL, PROJ_COLS), lambda c: (0, c)),
                  _full((1, RET_DK)), _full((1, RET_DK)), _full((RET_HEADS, 1, 1))],
        out_specs=(pl.BlockSpec((D_MODEL, PROJ_COLS), lambda c: (0, c)),) + tuple(_full(sh) for sh in shapes),
        out_shape=(jax.ShapeDtypeStruct((D_MODEL, IN_WIDTH), _BF),)
        + tuple(jax.ShapeDtypeStruct(sh, _F32) for sh in shapes),
        scratch_shapes=[pltpu.VMEM((Bs, D_MODEL), _BF), pltpu.VMEM((n_c, Bs, PROJ_COLS), _F32)],
        compiler_params=pltpu.CompilerParams(dimension_semantics=("arbitrary",), vmem_limit_bytes=VMEM_LIMIT),
        name="sample_proj",
    )(x, g_mix, w_in, cos2, sin2, _const(zeta[:, :, None], (RET_HEADS, 1, 1)))


def _mixer_sample_kernel(x_ref, u_ref, q_ref, k_ref, v_ref, ga_ref, gb_ref, spool_ref, qs_ref,
                         w_pool_ref, pscale_ref, gn_ref, w_out_ref, g_ffn_ref, w_rt_ref, dm_ref, xi_ref,
                         carry_in_ref,
                         x1_ref, dst_ref, rw_ref, npool_ref, cnt_ref, h2_ref,
                         mixed_scr, carry_scr, *, cap):
    u = u_ref[...]
    npool_ref[0:POOL_HIST - 1] = spool_ref[1:POOL_HIST]
    npool_ref[POOL_HIST - 1] = u
    for j in range(RET_HEADS):
        cs = slice(j * POOL_GC, (j + 1) * POOL_GC)
        qs = slice(j * RET_DK, (j + 1) * RET_DK)
        osl = slice(j * POOL_OUT_GC, (j + 1) * POOL_OUT_GC)
        win = POOL_WINDOWS[j]
        s = u[:, cs]
        for r in range(POOL_HIST - (win - 1), POOL_HIST):
            s = s + spool_ref[r, :, cs]
        pooled = s / jnp.float32(win) - u[:, cs]
        pool_out = _dot(pooled.astype(_BF), w_pool_ref[j]) * pscale_ref[:, osl]
        score = jnp.sum(q_ref[:, qs] * k_ref[:, qs], axis=-1, keepdims=True) * dm_ref[j]
        q_state = jnp.concatenate([qs_ref[hf, pl.ds(j, u.shape[0], stride=RET_HEADS), :]
                                   for hf in range(qs_ref.shape[0])], axis=1)
        o = score * v_ref[:, osl] + q_state * xi_ref[j]
        ret_out = _group_norm(o, gn_ref[:, osl])
        mixed_scr[:, osl] = (jax.nn.sigmoid(ga_ref[:, osl]) * pool_out
                             + jax.nn.sigmoid(gb_ref[:, osl]) * ret_out).astype(_BF)
    h2, e0, e1 = _post_mix(x_ref[...], mixed_scr, w_out_ref, g_ffn_ref, w_rt_ref, x1_ref, rw_ref)
    h2_ref[...] = _pack_rows(h2).reshape(h2_ref.shape)
    carry_scr[...] = carry_in_ref[...]
    dst_ref[...] = _rank_rows(e0, e1, cap, carry_scr).astype(jnp.int32)
    cnt_ref[...] = carry_scr[...]


def _mixer_sample(x, proj, state_pool, q_state, wts, carry_in, cap):
    Bs = x.shape[0]
    assert Bs == LANES and POOL_GROUPS == RET_HEADS
    u, q, k, v, ga, gb = proj
    dmask, xi, _, _ = _decay_tables(1)
    dm_b = _const(dmask, (RET_HEADS, 1, 1))
    xi_b = _const(xi[:, :, None], (RET_HEADS, 1, RET_DV))

    in_specs = [
        _full((Bs, D_MODEL)),
        _full((Bs, POOL_IN)), _full((Bs, QK_W)), _full((Bs, QK_W)), _full((Bs, V_W)),
        _full((Bs, D_MODEL)), _full((Bs, D_MODEL)),
        _full((POOL_HIST, Bs, POOL_IN)), _full(q_state.shape),
        _full((POOL_GROUPS, POOL_GC, POOL_OUT_GC)),
        _full((1, D_MODEL)), _full((1, D_MODEL)), _full((D_MODEL, D_MODEL)), _full((1, D_MODEL)),
        _full((D_MODEL, ROUTER_W)),
        _full((RET_HEADS, 1, 1)), _full((RET_HEADS, 1, RET_DV)),
        _full((1, LANES)),
    ]
    out_shape = (
        jax.ShapeDtypeStruct((Bs, D_MODEL), _F32),
        jax.ShapeDtypeStruct((Bs, LANES), jnp.int32),
        jax.ShapeDtypeStruct((Bs, LANES), _F32),
        jax.ShapeDtypeStruct((POOL_HIST, Bs, POOL_IN), _F32),
        jax.ShapeDtypeStruct((1, LANES), _F32),
        jax.ShapeDtypeStruct((Bs,) + ROW_TILE, ROW_DTYPE),
    )
    out_specs = (
        _full((Bs, D_MODEL)), _full((Bs, LANES)), _full((Bs, LANES)), _full((POOL_HIST, Bs, POOL_IN)),
        _full((1, LANES)), _full((Bs,) + ROW_TILE),
    )
    return pl.pallas_call(
        functools.partial(_mixer_sample_kernel, cap=cap),
        grid=(1,),
        in_specs=in_specs,
        out_specs=out_specs,
        out_shape=out_shape,
        scratch_shapes=[pltpu.VMEM((Bs, D_MODEL), _BF), pltpu.VMEM((1, LANES), _F32)],
        compiler_params=pltpu.CompilerParams(dimension_semantics=("arbitrary",), vmem_limit_bytes=VMEM_LIMIT),
        name="mixer_sample",
    )(x, u, q, k, v, ga, gb, state_pool, q_state, wts["w_pool"], wts["pool_scale"], wts["ret_gn"],
      wts["w_out"], wts["g_ffn"], wts["w_rt"], dm_b, xi_b, carry_in)


SC_CORES = 2
SC_SUBCORES = 16
SC_LANES = 16
STATE_ROWS = 32
STATE_UNROLL = 8
STATE_COLS = 8


def _sc_round_bf16(x):
    b = lax.bitcast_convert_type(x, jnp.uint32)
    r = b + jnp.uint32(0x7FFF) + ((b >> 16) & jnp.uint32(1))
    return lax.bitcast_convert_type(r & jnp.uint32(0xFFFF0000), _F32)


def _sc_state_update(s0, kq, v):
    P = s0.shape[0]
    n_workers = SC_CORES * SC_SUBCORES
    R = STATE_ROWS
    n_parts = RET_DK // R
    n_vc = RET_DV // SC_LANES
    assert P % (n_workers * RET_HEADS) == 0 and RET_DK % R == 0 and n_parts % 2 == 0 and n_parts >= 2
    assert R % STATE_UNROLL == 0 and n_vc % STATE_COLS == 0
    per_w = P // n_workers
    seqs_w = per_w // RET_HEADS
    _, _, _, g_chunk = _decay_tables(1)
    decay = _const(g_chunk[:, None], (RET_HEADS, SC_LANES))
    mesh = plsc.VectorSubcoreMesh(core_axis_name="c", subcore_axis_name="s")

    def body(s0_hbm, kq_hbm, v_hbm, g_hbm, out_hbm, qs_hbm, in_v, out_v, k_v, q_v, v_v, g_v, o_v, sem_in, sem_out):
        wid = lax.axis_index("s") * SC_CORES + lax.axis_index("c")
        base = wid * per_w
        seqs = pl.ds(wid * seqs_w, seqs_w)
        pltpu.sync_copy(kq_hbm.at[0, seqs], k_v)
        pltpu.sync_copy(kq_hbm.at[1, seqs], q_v)
        pltpu.sync_copy(v_hbm.at[seqs], v_v)
        pltpu.sync_copy(g_hbm, g_v)

        def load(p, part, slot):
            return pltpu.make_async_copy(s0_hbm.at[p, pl.ds(part * R, R)], in_v.at[slot], sem_in.at[slot])

        def store(p, part, slot):
            return pltpu.make_async_copy(out_v.at[slot], out_hbm.at[p, pl.ds(part * R, R)], sem_out.at[slot])

        def o_at(pp, c):
            return (c * SC_LANES // LANES, pp, pl.ds(c * SC_LANES % LANES, SC_LANES))

        load(base, 0, 0).start()

        def pair(pp, carry):
            p = base + pp
            seq, head = pp // RET_HEADS, pp % RET_HEADS
            g = g_v[head, :]
            sidx = jnp.full((SC_LANES,), seq, jnp.int32)
            for c in range(n_vc):
                o_v[o_at(pp, c)] = jnp.zeros((SC_LANES,), _F32)
            for part in range(n_parts):
                slot = part % 2
                load(p, part, slot).wait()
                if part + 1 < n_parts:
                    load(p, part + 1, 1 - slot).start()
                else:
                    @pl.when(pp + 1 < per_w)
                    def _():
                        load(p + 1, 0, 1 - slot).start()

                if part < 2:
                    @pl.when(pp > 0)
                    def _():
                        store(p, part, slot).wait()
                else:
                    store(p, part, slot).wait()

                def rows(i8, c2):
                    for c0 in range(0, n_vc, STATE_COLS):
                        cols = [pl.ds((c0 + c) * SC_LANES, SC_LANES) for c in range(STATE_COLS)]
                        vs = [v_v[seq, pl.ds(head * RET_DV + (c0 + c) * SC_LANES, SC_LANES)]
                              for c in range(STATE_COLS)]
                        acc = [None] * STATE_COLS
                        for j in range(STATE_UNROLL):
                            i = i8 * STATE_UNROLL + j
                            ridx = jnp.full((SC_LANES,), head * RET_DK + part * R + i, jnp.int32)
                            ki = plsc.load_gather(k_v, [sidx, ridx])
                            qi = plsc.load_gather(q_v, [sidx, ridx])
                            ss = [in_v[slot, i, cs] for cs in cols]
                            for c, cs in enumerate(cols):
                                out_v[slot, i, cs] = g * ss[c] + ki * vs[c]
                            for c in range(STATE_COLS):
                                t = qi * _sc_round_bf16(ss[c])
                                acc[c] = t if acc[c] is None else acc[c] + t
                        for c in range(STATE_COLS):
                            o_v[o_at(pp, c0 + c)] = o_v[o_at(pp, c0 + c)] + acc[c]
                    return c2

                lax.fori_loop(0, R // STATE_UNROLL, rows, 0)
                store(p, part, slot).start()
            return carry

        lax.fori_loop(0, per_w, pair, 0)
        for slot in range(2):
            store(base, slot, slot).wait()
        for hf in range(RET_DV // LANES):
            pltpu.sync_copy(o_v.at[hf], qs_hbm.at[hf, pl.ds(base, per_w)])

    keys = pltpu.VMEM((seqs_w, RET_HEADS * RET_DK), _F32)
    return pl.kernel(
        body, mesh=mesh,
        out_type=(jax.ShapeDtypeStruct(s0.shape, _F32), jax.ShapeDtypeStruct((RET_DV // LANES, P, LANES), _F32)),
        scratch_types=[pltpu.VMEM((2, R, RET_DV), _F32), pltpu.VMEM((2, R, RET_DV), _F32), keys, keys,
                       pltpu.VMEM((seqs_w, RET_HEADS * RET_DV), _F32), pltpu.VMEM((RET_HEADS, SC_LANES), _F32),
                       pltpu.VMEM((RET_DV // LANES, per_w, LANES), _F32),
                       pltpu.SemaphoreType.DMA((2,)), pltpu.SemaphoreType.DMA((2,))],
        compiler_params=dataclasses.replace(pltpu.CompilerParams(), needs_layout_passes=False),
        name="sample_state_update",
    )(s0, kq, v, decay)


IDX_CHUNK = 1024
ISSUE_UNROLL = 8
SC_WINDOW = 64


def _sc_scatter(h2, dst_rows, n_rows):
    T = h2.shape[0]
    n_workers = SC_CORES * SC_SUBCORES
    W = SC_WINDOW
    assert T % (n_workers * W) == 0
    per_w = T // n_workers
    n_win = per_w // W
    d0 = dst_rows[0].reshape(T // W, W)
    d1 = dst_rows[1].reshape(T // W, W)
    mesh = plsc.VectorSubcoreMesh(core_axis_name="c", subcore_axis_name="s")

    def body(h2_hbm, d0_hbm, d1_hbm, xs_hbm, i0_v, i1_v, rows_v, sem_load, sem_store):
        wid = lax.axis_index("s") * SC_CORES + lax.axis_index("c")
        base = wid * per_w
        pltpu.sync_copy(d0_hbm.at[pl.ds(wid * n_win, n_win)], i0_v)
        pltpu.sync_copy(d1_hbm.at[pl.ds(wid * n_win, n_win)], i1_v)
        pltpu.async_copy(h2_hbm.at[pl.ds(base, W)], rows_v.at[0], sem_load)
        for i in range(n_win):
            b = i % 2
            pltpu.make_async_copy(h2_hbm.at[pl.ds(base, W)], rows_v.at[b], sem_load).wait()
            if i + 1 < n_win:
                pltpu.async_copy(h2_hbm.at[pl.ds(base + (i + 1) * W, W)], rows_v.at[1 - b], sem_load)
            c0 = pltpu.async_copy(rows_v.at[b], xs_hbm.at[i0_v.at[i]], sem_store)
            c1 = pltpu.async_copy(rows_v.at[b], xs_hbm.at[i1_v.at[i]], sem_store)
            c0.wait()
            c1.wait()

    return pl.kernel(
        body, mesh=mesh,
        out_type=jax.ShapeDtypeStruct((n_rows,) + ROW_TILE, ROW_DTYPE),
        scratch_types=[pltpu.VMEM((n_win, W), jnp.int32), pltpu.VMEM((n_win, W), jnp.int32),
                       pltpu.VMEM((2, W) + ROW_TILE, ROW_DTYPE),
                       pltpu.SemaphoreType.DMA, pltpu.SemaphoreType.DMA],
        name="moe_sc_scatter",
    )(h2, d0, d1)


ZERO_CHUNK = 64


def _finish_dispatch_kernel(zstart_ref, zchunks_ref, slots_hbm, h2_ref, xs_in_hbm, xs_hbm, idx_smem, zbuf, sem_idx,
                            sem_rows, sem_zero):
    del xs_in_hbm
    TS = h2_ref.shape[0]
    idx_cp = pltpu.make_async_copy(slots_hbm.at[pl.ds(0, IDX_CHUNK)], idx_smem, sem_idx)
    idx_cp.start()
    zbuf[...] = jnp.zeros(zbuf.shape, zbuf.dtype)

    def zero_copy(e, j):
        return pltpu.make_async_copy(zbuf, xs_hbm.at[pl.ds(zstart_ref[e] + j * ZERO_CHUNK, ZERO_CHUNK)], sem_zero)

    for e in range(N_EXPERTS):
        lax.fori_loop(0, zchunks_ref[e], lambda j, c, e=e: (zero_copy(e, j).start(), c)[1], 0)
    idx_cp.wait()

    def issue(r, carry):
        for kk in range(2):
            pltpu.make_async_copy(h2_ref.at[r], xs_hbm.at[idx_smem[2 * r + kk]], sem_rows).start(priority=kk)
        return carry

    lax.fori_loop(0, TS, issue, 0, unroll=ISSUE_UNROLL)
    for e in range(N_EXPERTS):
        lax.fori_loop(0, zchunks_ref[e], lambda j, c, e=e: (zero_copy(e, j).wait(), c)[1], 0)
    for kk in range(2):
        pltpu.make_async_copy(h2_ref, xs_hbm.at[pl.ds(0, TS)], sem_rows).wait()


def _finish_dispatch(xs, h2s, slots, zstart, zchunks):
    Bs = h2s.shape[0]
    assert 2 * Bs <= IDX_CHUNK and slots.shape[0] == IDX_CHUNK and MOE_BLOCK % ZERO_CHUNK == 0
    any_spec = pl.BlockSpec(memory_space=pl.ANY)
    grid_spec = pltpu.PrefetchScalarGridSpec(
        num_scalar_prefetch=2,
        grid=(1,),
        in_specs=[any_spec, pl.BlockSpec((Bs,) + ROW_TILE, lambda t, z, n: (0, 0, 0)), any_spec],
        out_specs=any_spec,
        scratch_shapes=[pltpu.SMEM((IDX_CHUNK,), jnp.int32), pltpu.VMEM((ZERO_CHUNK,) + ROW_TILE, ROW_DTYPE),
                        pltpu.SemaphoreType.DMA, pltpu.SemaphoreType.DMA, pltpu.SemaphoreType.DMA],
    )
    return pl.pallas_call(
        _finish_dispatch_kernel,
        grid_spec=grid_spec,
        out_shape=jax.ShapeDtypeStruct(xs.shape, ROW_DTYPE),
        input_output_aliases={4: 0},
        compiler_params=pltpu.CompilerParams(dimension_semantics=("arbitrary",), has_side_effects=True),
        name="moe_finish_dispatch",
    )(zstart, zchunks, slots, h2s, xs)


FFN_IN_BUFFERS = 4
FFN_OUT_BUFFERS = 3


def _ffn_kernel(cnt_ref, zstart_ref, zchunks_ref, slots_hbm, h2_ref, xs_in_hbm, w1_hbm, w3_hbm, w2_hbm, xs_hbm, ys_hbm,
                xs_buf, ys_buf, st1, st3, st2, w13_scr, w2_scr, t_row, t_exp, t_len, idx_smem, zbuf,
                sem_in, sem_out, sem_w, sem_idx, sem_rows, sem_zero, *, cap_blocks):
    B = MOE_BLOCK
    _finish_dispatch_kernel(zstart_ref, zchunks_ref, slots_hbm, h2_ref, xs_in_hbm, xs_hbm, idx_smem, zbuf,
                            sem_idx, sem_rows, sem_zero)

    def fill_expert(e, g):
        nb = (cnt_ref[e] + (B - 1)) // B

        def fill_block(j, carry):
            t_row[g + j] = e * cap_blocks + j
            t_exp[g + j] = e
            t_len[g + j] = nb
            return carry

        lax.fori_loop(0, nb, fill_block, 0)
        return g + nb

    n_used = lax.fori_loop(0, N_EXPERTS, fill_expert, 0)

    def in_copy(g):
        s = g % FFN_IN_BUFFERS
        return pltpu.make_async_copy(xs_hbm.at[pl.ds(t_row[g] * B, B)], xs_buf.at[s], sem_in.at[s])

    def out_copy(g):
        s = g % FFN_OUT_BUFFERS
        return pltpu.make_async_copy(ys_buf.at[s], ys_hbm.at[pl.ds(t_row[g] * B, B)], sem_out.at[s])

    def weight_copies(e, s):
        return (pltpu.make_async_copy(w1_hbm.at[e], st1.at[s], sem_w.at[s]),
                pltpu.make_async_copy(w3_hbm.at[e], st3.at[s], sem_w.at[s]),
                pltpu.make_async_copy(w2_hbm.at[e], st2.at[s], sem_w.at[s]))

    @pl.when(n_used > 0)
    def _():
        for cp in weight_copies(t_exp[0], 0):
            cp.start()

    for g0 in range(FFN_IN_BUFFERS - 1):
        @pl.when(g0 < n_used)
        def _(g0=g0):
            in_copy(g0).start()

    def block(g, wslot):
        first = (g == 0) | (t_exp[g] != t_exp[jnp.maximum(g - 1, 0)])
        wslot = jnp.where(first & (g > 0), 1 - wslot, wslot)

        @pl.when(first)
        def _():
            for cp in weight_copies(t_exp[g], wslot):
                cp.wait()
            nxt = g + t_len[g]

            @pl.when(nxt < n_used)
            def _():
                for cp in weight_copies(t_exp[jnp.minimum(nxt, n_used - 1)], 1 - wslot):
                    cp.start()

            w13_scr[:, 0:D_EXPERT] = st1[wslot].astype(_BF)
            w13_scr[:, D_EXPERT:2 * D_EXPERT] = st3[wslot].astype(_BF)
            w2_scr[...] = st2[wslot].astype(_BF)

        in_copy(g).wait()

        @pl.when(g + FFN_IN_BUFFERS - 1 < n_used)
        def _():
            in_copy(g + FFN_IN_BUFFERS - 1).start()

        @pl.when(g >= FFN_OUT_BUFFERS)
        def _():
            out_copy(g - FFN_OUT_BUFFERS).wait()

        xb = _unpack_rows(xs_buf[g % FFN_IN_BUFFERS]).astype(_BF)
        ab = _dot(xb, w13_scr[...])
        hid = jax.nn.silu(ab[:, 0:D_EXPERT]) * ab[:, D_EXPERT:2 * D_EXPERT]
        ys_buf[g % FFN_OUT_BUFFERS] = _pack_rows(_dot(hid.astype(_BF), w2_scr[...]))
        out_copy(g).start()
        return wslot

    lax.fori_loop(0, n_used, block, 0)

    for back in range(FFN_OUT_BUFFERS, 0, -1):
        @pl.when(n_used >= back)
        def _(back=back):
            out_copy(n_used - back).wait()


def _ffn(xs, h2s, slots, zstart, zchunks, counts, w1, w3, w2, cap_blocks, n_blocks):
    Bs = h2s.shape[0]
    assert 2 * Bs <= IDX_CHUNK and slots.shape[0] == IDX_CHUNK and MOE_BLOCK % ZERO_CHUNK == 0
    any_spec = pl.BlockSpec(memory_space=pl.ANY)
    row_buf = lambda n: pltpu.VMEM((n, MOE_BLOCK) + ROW_TILE, ROW_DTYPE)
    grid_spec = pltpu.PrefetchScalarGridSpec(
        num_scalar_prefetch=3,
        grid=(1,),
        in_specs=[any_spec, pl.BlockSpec((Bs,) + ROW_TILE, lambda t, c, z, n: (0, 0, 0)), any_spec,
                  any_spec, any_spec, any_spec],
        out_specs=(any_spec, any_spec),
        scratch_shapes=[row_buf(FFN_IN_BUFFERS), row_buf(FFN_OUT_BUFFERS),
                        pltpu.VMEM((2, D_MODEL, D_EXPERT), _F32), pltpu.VMEM((2, D_MODEL, D_EXPERT), _F32),
                        pltpu.VMEM((2, D_EXPERT, D_MODEL), _F32),
                        pltpu.VMEM((D_MODEL, 2 * D_EXPERT), _BF), pltpu.VMEM((D_EXPERT, D_MODEL), _BF),
                        pltpu.SMEM((n_blocks,), jnp.int32), pltpu.SMEM((n_blocks,), jnp.int32),
                        pltpu.SMEM((n_blocks,), jnp.int32),
                        pltpu.SMEM((IDX_CHUNK,), jnp.int32), pltpu.VMEM((ZERO_CHUNK,) + ROW_TILE, ROW_DTYPE),
                        pltpu.SemaphoreType.DMA((FFN_IN_BUFFERS,)), pltpu.SemaphoreType.DMA((FFN_OUT_BUFFERS,)),
                        pltpu.SemaphoreType.DMA((2,)),
                        pltpu.SemaphoreType.DMA, pltpu.SemaphoreType.DMA, pltpu.SemaphoreType.DMA],
    )
    return pl.pallas_call(
        functools.partial(_ffn_kernel, cap_blocks=cap_blocks),
        grid_spec=grid_spec,
        out_shape=(jax.ShapeDtypeStruct(xs.shape, ROW_DTYPE), jax.ShapeDtypeStruct(xs.shape, ROW_DTYPE)),
        input_output_aliases={5: 0},
        compiler_params=pltpu.CompilerParams(dimension_semantics=("arbitrary",), vmem_limit_bytes=VMEM_LIMIT,
                                             has_side_effects=True),
        name="moe_ffn",
    )(counts, zstart, zchunks, slots, h2s, xs, w1, w3, w2)[1]


def _combine_kernel(slots_hbm, x1_ref, rw_ref, g_ref, ys_hbm, y_ref, idx_smem, buf, sem_idx, sem_rows):
    t = pl.program_id(0)
    TC = x1_ref.shape[0]
    idx_cp = pltpu.make_async_copy(slots_hbm.at[pl.ds(t * IDX_CHUNK, IDX_CHUNK)], idx_smem, sem_idx)
    idx_cp.start()
    idx_cp.wait()

    def issue(r, carry):
        for kk in range(2):
            pltpu.make_async_copy(ys_hbm.at[idx_smem[2 * r + kk]], buf.at[kk, r], sem_rows).start(priority=kk)
        return carry

    lax.fori_loop(0, TC, issue, 0, unroll=ISSUE_UNROLL)
    for kk in range(2):
        pltpu.make_async_copy(ys_hbm.at[pl.ds(0, TC)], buf.at[kk], sem_rows).wait()

    w = rw_ref[...]
    y0 = _unpack_rows(buf[0])
    y1 = _unpack_rows(buf[1])
    x2 = x1_ref[...] + (w[:, 0:1] * y0 + w[:, 1:2] * y1)
    y_ref[...] = _rms(x2, g_ref[...])


SC_GATHER_CHUNKS = 4
DENSE_TILE = 1024
SC_GATHER_BUFFERS = 3
SC_GATHER_MAX_WINDOW = 80


def _sc_gather(ys, rows):
    Tg = rows.shape[1]
    n_workers = SC_CORES * SC_SUBCORES
    per_w = Tg // n_workers
    assert per_w * n_workers == Tg and per_w % SUBLANES == 0
    W = max(w for w in range(SUBLANES, SC_GATHER_MAX_WINDOW + 1, SUBLANES) if per_w % w == 0)
    n_win = per_w // W
    NB = SC_GATHER_BUFFERS
    idx = rows.reshape(2, n_workers, n_win, W)
    jobs = [(kk, i) for i in range(n_win) for kk in range(2)]
    mesh = plsc.VectorSubcoreMesh(core_axis_name="c", subcore_axis_name="s")

    def body(ys_hbm, idx_hbm, out_hbm, i_v, rows_v, sem_g, sem_s):
        wid = lax.axis_index("s") * SC_CORES + lax.axis_index("c")
        base = wid * per_w
        for kk in range(2):
            pltpu.sync_copy(idx_hbm.at[kk, wid], i_v.at[kk])

        def gather(j):
            kk, i = jobs[j]
            return pltpu.make_async_copy(ys_hbm.at[i_v.at[kk, i]], rows_v.at[j % NB], sem_g.at[j % NB])

        def store(j):
            kk, i = jobs[j]
            return pltpu.make_async_copy(rows_v.at[j % NB], out_hbm.at[kk, pl.ds(base + i * W, W)], sem_s.at[j % NB])

        gather(0).start()
        for j in range(len(jobs)):
            gather(j).wait()
            store(j).start()
            if j + 1 < len(jobs):
                if j + 1 >= NB:
                    store(j + 1 - NB).wait()
                gather(j + 1).start()
        for j in range(max(0, len(jobs) - NB), len(jobs)):
            store(j).wait()

    return pl.kernel(
        body, mesh=mesh,
        out_type=jax.ShapeDtypeStruct((2, Tg) + ROW_TILE, ROW_DTYPE),
        scratch_types=[pltpu.VMEM((2, n_win, W), jnp.int32), pltpu.VMEM((NB, W) + ROW_TILE, ROW_DTYPE),
                       pltpu.SemaphoreType.DMA((NB,)), pltpu.SemaphoreType.DMA((NB,))],
        name="moe_sc_gather",
    )(ys, idx)


def _combine_dense_kernel(x1_ref, rw_ref, g_ref, rows_ref, *rest):
    y_ref = rest[-1]
    w = rw_ref[...]
    y0 = _unpack_rows(rows_ref[0])
    y1 = _unpack_rows(rows_ref[1])
    x2 = x1_ref[...] + (w[:, 0:1] * y0 + w[:, 1:2] * y1)
    y_ref[...] = _rms(x2, g_ref[...])


def _combine_dense(x1, rw, rows, y, g_final, tile, first_tile):
    Tg = rows.shape[1]
    assert Tg % tile == 0
    tok = lambda t: (first_tile + t, 0)
    in_specs = [pl.BlockSpec((tile, D_MODEL), tok), pl.BlockSpec((tile, LANES), tok), _full((1, D_MODEL)),
                pl.BlockSpec((2, tile) + ROW_TILE, lambda t: (0, t, 0, 0))]
    args = [x1, rw, g_final.reshape(1, D_MODEL), rows]
    aliases = {}
    if y is not None:
        in_specs.append(pl.BlockSpec(memory_space=pl.ANY))
        args.append(y)
        aliases = {4: 0}
    return pl.pallas_call(
        _combine_dense_kernel,
        grid=(Tg // tile,),
        in_specs=in_specs,
        out_specs=pl.BlockSpec((tile, D_MODEL), tok),
        out_shape=jax.ShapeDtypeStruct(x1.shape, _F32),
        input_output_aliases=aliases,
        compiler_params=pltpu.CompilerParams(dimension_semantics=("arbitrary",), vmem_limit_bytes=VMEM_LIMIT),
        name="moe_combine_dense",
    )(*args)


def _combine(x1, rw, slots, ys, g_final, tile, n_tiles):
    T = x1.shape[0]
    assert T % tile == 0 and 2 * tile <= IDX_CHUNK and slots.shape[0] == n_tiles * IDX_CHUNK
    any_spec = pl.BlockSpec(memory_space=pl.ANY)
    return pl.pallas_call(
        _combine_kernel,
        grid=(n_tiles,),
        in_specs=[any_spec, pl.BlockSpec((tile, D_MODEL), lambda t: (t, 0)),
                  pl.BlockSpec((tile, LANES), lambda t: (t, 0)), _full((1, D_MODEL)), any_spec],
        out_specs=pl.BlockSpec((tile, D_MODEL), lambda t: (t, 0)),
        out_shape=jax.ShapeDtypeStruct((T, D_MODEL), _F32),
        scratch_shapes=[pltpu.SMEM((IDX_CHUNK,), jnp.int32), pltpu.VMEM((2, tile) + ROW_TILE, ROW_DTYPE),
                        pltpu.SemaphoreType.DMA, pltpu.SemaphoreType.DMA],
        compiler_params=pltpu.CompilerParams(dimension_semantics=("arbitrary",), vmem_limit_bytes=VMEM_LIMIT),
        name="moe_combine",
    )(slots, x1, rw, g_final.reshape(1, D_MODEL), ys)


def _tile_for(n_tokens):
    tile = IDX_CHUNK // 2
    return tile if n_tokens % tile == 0 else n_tokens


def _chunked_slots(slot, tile):
    n_tiles = slot.shape[0] // tile
    s = slot.reshape(n_tiles, 2 * tile)
    return jnp.pad(s, ((0, 0), (0, IDX_CHUNK - 2 * tile))).reshape(-1)


def kernel(x_prompt, x_sample, state_pool, state_ret, g_mix, w_in, w_pool, pool_scale, ret_gn, w_out, g_ffn, w_grp, w_exp, w1, w3, w2, g_final):
    Bp, Lp, _ = x_prompt.shape
    Bs = x_sample.shape[0]
    Tp = Bp * Lp
    wts = _prep_weights(g_mix[0], w_pool[0], pool_scale[0], ret_gn[0], w_out[0], g_ffn[0], w_grp[0], w_exp[0])

    T_all = Tp + Bs
    cap = (-(-T_all // MOE_BLOCK) + 1) * MOE_BLOCK
    cap_blocks = cap // MOE_BLOCK

    xs_tok = x_sample.reshape(Bs, D_MODEL)
    wts["w_in"], u_s, q_s, k_s, kq_s, v_s, ga_s, gb_s = _sample_proj(xs_tok, wts["g_mix"], w_in[0], PAST_LEN)
    nret_s, q_state = _sc_state_update(state_ret[0].reshape(Bs * RET_HEADS, RET_DK, RET_DV), kq_s, v_s)
    x1p, dst_rows_p, rwp, npool_p, nret_p, counts_p, h2p = _mixer_prompt(x_prompt, wts, cap)
    xs = _sc_scatter(h2p, dst_rows_p, N_EXPERTS * cap)
    x1s, dst_s, rws, npool_s, counts, h2s = _mixer_sample(
        xs_tok, (u_s, q_s, k_s, v_s, ga_s, gb_s), jnp.swapaxes(state_pool[0], 0, 1), q_state, wts, counts_p, cap)

    counts = counts[0, :N_EXPERTS].astype(jnp.int32)
    tile_p, tile_s = _tile_for(Tp), _tile_for(Bs)
    slots_s = _chunked_slots(dst_s[:, :2], tile_s)
    zstart = jnp.arange(N_EXPERTS, dtype=jnp.int32) * cap + counts
    zchunks = ((-counts) % MOE_BLOCK + ZERO_CHUNK - 1) // ZERO_CHUNK
    n_blocks = -(-2 * T_all // MOE_BLOCK) + N_EXPERTS
    ys = _ffn(xs, h2s, slots_s, zstart, zchunks.astype(jnp.int32), counts, w1[0], w3[0], w2[0], cap_blocks, n_blocks)
    tile_d = DENSE_TILE if Tp % DENSE_TILE == 0 else tile_p
    n_tiles_d = Tp // tile_d
    chunk_tiles = [n_tiles_d // SC_GATHER_CHUNKS + (k < n_tiles_d % SC_GATHER_CHUNKS)
                   for k in range(SC_GATHER_CHUNKS)]
    gathered, first = [], 0
    for nt in chunk_tiles:
        if nt:
            gathered.append((first, _sc_gather(ys, dst_rows_p[:, first * tile_d:(first + nt) * tile_d])))
            first += nt
    y_s = _combine(x1s, rws, slots_s, ys, g_final, tile_s, 1)
    y_p = None
    for first, rows in gathered:
        y_p = _combine_dense(x1p, rwp, rows, y_p, g_final, tile_d, first)

    return (y_p.reshape(Bp, Lp, D_MODEL), y_s.reshape(Bs, 1, D_MODEL),
            npool_p[None], nret_p[None], jnp.swapaxes(npool_s, 0, 1)[None], nret_s.reshape(state_ret.shape))
```
